```python
import jax, jax.numpy as jnp
from jax import lax
import numpy as np

D_MODEL = 1024
BATCH = 16
SEQ = 2048
DEPTH = 1

CHUNK = 64
LEFT_CHUNKS = 8
BAND = (LEFT_CHUNKS + 1) * CHUNK
REL_CLIP = 128
REL_TABLE = CHUNK + REL_CLIP

RWKV_HEAD_DIM = 64
RWKV_WIDTH = D_MODEL
RWKV_HEADS = RWKV_WIDTH // RWKV_HEAD_DIM
DECAY_LORA = 64
ICLR_LORA = 64
GATE_LORA = 160
RWKV_IN = 3 * RWKV_WIDTH + DECAY_LORA + ICLR_LORA + GATE_LORA

ATT_HEAD_DIM = 64
ATT_WIDTH = D_MODEL
ATT_HEADS = ATT_WIDTH // ATT_HEAD_DIM

IN_WIDTH = RWKV_IN + 3 * ATT_WIDTH + 2 * D_MODEL

MEM_TOKENS = 256
MEM_HEADS = 4
MEM_WIDTH = D_MODEL // 2
MEM_HEAD_DIM = MEM_WIDTH // MEM_HEADS

FFN_HIDDEN = ((8 * D_MODEL) // 3 + 255) // 256 * 256

NORM_EPS = 1e-6
GROUP_NORM_EPS = 64e-5
MASK_VALUE = -1e30

kernel_name = "hybrid_rwkv7_chunkattn_gated_block"


def rms_norm(x, gain):
    xf = x.astype(jnp.float32)
    y = xf * lax.rsqrt(jnp.mean(xf * xf, axis=-1, keepdims=True) + NORM_EPS)
    return (y * gain.astype(jnp.float32)).astype(x.dtype)


def token_shift(p):
    return jnp.pad(p, ((0, 0), (1, 0), (0, 0)))[:, :-1]


def wkv7_scan(r, decay, k, v, kk, a):
    B, S, H, N = r.shape

    def step(state, inp):
        r_t, w_t, k_t, v_t, kk_t, a_t = inp
        sa = jnp.einsum('bhvk,bhk->bhv', state, kk_t)
        state = (state * w_t[:, :, None, :]
                 - sa[..., None] * (kk_t * a_t)[:, :, None, :]
                 + v_t[..., None] * k_t[:, :, None, :])
        return state, jnp.einsum('bhvk,bhk->bhv', state, r_t)

    xs = tuple(jnp.swapaxes(t, 0, 1) for t in (r, decay, k, v, kk, a))
    s0 = jnp.zeros((B, H, N, N), jnp.float32)
    _, ys = lax.scan(step, s0, xs)
    return jnp.swapaxes(ys, 0, 1)


def rwkv7_time_mix(p, shift_mix, decay_base, decay_up, iclr_base, iclr_up, gate_up,
                   key_norm_scale, key_iclr_scale, bonus_scale, lnx_w, lnx_b):
    B, S, _ = p.shape
    f32 = jnp.float32
    z = p + (token_shift(p) - p) * shift_mix
    W = RWKV_WIDTH
    r, k, v, zw, za, zg = jnp.split(
        z, [W, 2 * W, 3 * W, 3 * W + DECAY_LORA, 3 * W + DECAY_LORA + ICLR_LORA], axis=-1)
    w_log = -jax.nn.softplus(-(decay_base + jnp.tanh(zw) @ decay_up)) - 0.5
    decay = jnp.exp(-jnp.exp(w_log.astype(f32)))
    a = jax.nn.sigmoid(iclr_base + za @ iclr_up)
    g = jax.nn.sigmoid(zg) @ gate_up

    def heads(t):
        return t.astype(f32).reshape(B, S, RWKV_HEADS, RWKV_HEAD_DIM)

    kk = heads(k * key_norm_scale)
    kk = kk * lax.rsqrt(jnp.maximum(jnp.sum(kk * kk, axis=-1, keepdims=True), 1e-24))
    k = k * (1.0 + (a - 1.0) * key_iclr_scale)
    rh, kh, vh, ah = heads(r), heads(k), heads(v), heads(a)
    y = wkv7_scan(rh, heads(decay), kh, vh, kk, ah)
    mu = jnp.mean(y, axis=-1, keepdims=True)
    var = jnp.mean(jnp.square(y - mu), axis=-1, keepdims=True)
    y = ((y - mu) * lax.rsqrt(var + GROUP_NORM_EPS)).reshape(B, S, W)
    y = y * lnx_w.astype(f32) + lnx_b.astype(f32)
    bonus = jnp.sum(rh * kh * bonus_scale.astype(f32), axis=-1, keepdims=True) * vh
    out = (y + bonus.reshape(B, S, W)) * g.astype(f32)
    return out.astype(p.dtype)


def chunk_attention(q, k, v, rel_bias):
    B, S, _ = q.shape
    n_chunks = S // CHUNK
    pad = LEFT_CHUNKS * CHUNK
    q = q.reshape(B, S, ATT_HEADS, ATT_HEAD_DIM)
    k_pad = jnp.pad(k.reshape(B, S, ATT_HEADS, ATT_HEAD_DIM), ((0, 0), (pad, 0), (0, 0), (0, 0)))
    v_pad = jnp.pad(v.reshape(B, S, ATT_HEADS, ATT_HEAD_DIM), ((0, 0), (pad, 0), (0, 0), (0, 0)))
    dist = jnp.arange(CHUNK)[:, None] - jnp.arange(BAND)[None, :] + pad
    idx = jnp.minimum(dist, REL_CLIP) + (CHUNK - 1)
    bias = rel_bias.astype(jnp.float32)[:, idx]
    scale = ATT_HEAD_DIM ** -0.5

    def one_chunk(c):
        start = c * CHUNK
        qc = lax.dynamic_slice_in_dim(q, start, CHUNK, axis=1)
        kc = lax.dynamic_slice_in_dim(k_pad, start, BAND, axis=1)
        vc = lax.dynamic_slice_in_dim(v_pad, start, BAND, axis=1)
        s = jnp.einsum('bqhd,bkhd->bhqk', qc, kc).astype(jnp.float32) * scale + bias
        valid = (start - pad + jnp.arange(BAND)) >= 0
        s = jnp.where(valid[None, None, None, :], s, MASK_VALUE)
        pr = jax.nn.softmax(s, axis=-1).astype(vc.dtype)
        return jnp.einsum('bhqk,bkhd->bqhd', pr, vc)

    out = lax.map(one_chunk, jnp.arange(n_chunks))
    return jnp.transpose(out, (1, 0, 2, 3, 4)).reshape(B, S, ATT_WIDTH)


def memory_cross_attention(h, mem_n, w_q, w_kv, w_o):
    B, S, _ = h.shape
    M = mem_n.shape[1]
    q = (h @ w_q).reshape(B, S, MEM_HEADS, MEM_HEAD_DIM)
    k, v = jnp.split(mem_n @ w_kv, 2, axis=-1)
    k = k.reshape(B, M, MEM_HEADS, MEM_HEAD_DIM)
    v = v.reshape(B, M, MEM_HEADS, MEM_HEAD_DIM)
    s = jnp.einsum('bshd,bmhd->bhsm', q, k).astype(jnp.float32) * (MEM_HEAD_DIM ** -0.5)
    pr = jax.nn.softmax(s, axis=-1).astype(v.dtype)
    o = jnp.einsum('bhsm,bmhd->bshd', pr, v).reshape(B, S, MEM_WIDTH)
    return o @ w_o


def swiglu_ffn(h, w_in, w_out):
    gate, up = jnp.split(h @ w_in, 2, axis=-1)
    return (jax.nn.silu(gate) * up) @ w_out


def _fwd_setup_inputs(seed: int = 0) -> dict:
    key = jax.random.key(seed)
    ks = iter(jax.random.split(key, 48))

    def nrm(shape, scale):
        return jax.random.normal(next(ks), shape, jnp.float32) * scale

    def gain(width=D_MODEL):
        return 1.0 + nrm((DEPTH, width), 0.02)

    L = DEPTH
    return {
        "x": nrm((BATCH, SEQ, D_MODEL), 1.0),
        "mem": nrm((BATCH, MEM_TOKENS, D_MODEL), 1.0),
        "g_pre_mix": gain(),
        "g_post_mix": gain(),
        "w_in": nrm((L, D_MODEL, IN_WIDTH), D_MODEL ** -0.5),
        "shift_mix": jax.random.uniform(next(ks), (L, RWKV_IN), jnp.float32),
        "decay_base": jax.random.uniform(next(ks), (L, RWKV_WIDTH), jnp.float32, -6.0, -1.0),
        "decay_up": nrm((L, DECAY_LORA, RWKV_WIDTH), DECAY_LORA ** -0.5),
        "iclr_base": nrm((L, RWKV_WIDTH), 0.5),
        "iclr_up": nrm((L, ICLR_LORA, RWKV_WIDTH), ICLR_LORA ** -0.5),
        "gate_up": nrm((L, GATE_LORA, RWKV_WIDTH), GATE_LORA ** -0.5),
        "key_norm_scale": 0.85 + nrm((L, RWKV_WIDTH), 0.02),
        "key_iclr_scale": 1.0 + nrm((L, RWKV_WIDTH), 0.02),
        "bonus_scale": nrm((L, RWKV_HEADS, RWKV_HEAD_DIM), 0.1),
        "lnx_w": gain(RWKV_WIDTH),
        "lnx_b": nrm((L, RWKV_WIDTH), 0.02),
        "rel_bias": nrm((L, ATT_HEADS, REL_TABLE), 0.1),
        "w_branch_a": nrm((L, RWKV_WIDTH, D_MODEL), RWKV_WIDTH ** -0.5),
        "w_branch_b": nrm((L, ATT_WIDTH, D_MODEL), ATT_WIDTH ** -0.5),
        "w_out": nrm((L, D_MODEL, D_MODEL), D_MODEL ** -0.5),
        "g_pre_cross": gain(),
        "g_post_cross": gain(),
        "g_mem": gain(),
        "w_q_mem": nrm((L, D_MODEL, MEM_WIDTH), D_MODEL ** -0.5),
        "w_kv_mem": nrm((L, D_MODEL, 2 * MEM_WIDTH), D_MODEL ** -0.5),
        "w_o_mem": nrm((L, MEM_WIDTH, D_MODEL), MEM_WIDTH ** -0.5),
        "g_pre_ffn": gain(),
        "g_post_ffn": gain(),
        "w_ffn_in": nrm((L, D_MODEL, 2 * FFN_HIDDEN), D_MODEL ** -0.5),
        "w_ffn_out": nrm((L, FFN_HIDDEN, D_MODEL), FFN_HIDDEN ** -0.5),
    }


def _fwd_reference(x, mem, g_pre_mix, g_post_mix, w_in, shift_mix, decay_base, decay_up,
              iclr_base, iclr_up, gate_up, key_norm_scale, key_iclr_scale, bonus_scale,
              lnx_w, lnx_b, rel_bias, w_branch_a, w_branch_b, w_out,
              g_pre_cross, g_post_cross, g_mem, w_q_mem, w_kv_mem, w_o_mem,
              g_pre_ffn, g_post_ffn, w_ffn_in, w_ffn_out):
    split_at = [RWKV_IN,
                RWKV_IN + ATT_WIDTH,
                RWKV_IN + 2 * ATT_WIDTH,
                RWKV_IN + 3 * ATT_WIDTH,
                RWKV_IN + 3 * ATT_WIDTH + D_MODEL]
    for l in range(DEPTH):
        h = rms_norm(x, g_pre_mix[l])
        proj = h @ w_in[l]
        p_rwkv, q, k, v, z_ga, z_gb = jnp.split(proj, split_at, axis=-1)
        y_a = rwkv7_time_mix(p_rwkv, shift_mix[l], decay_base[l], decay_up[l],
                             iclr_base[l], iclr_up[l], gate_up[l], key_norm_scale[l],
                             key_iclr_scale[l], bonus_scale[l], lnx_w[l], lnx_b[l])
        y_b = chunk_attention(q, k, v, rel_bias[l])
        mixed = (jax.nn.sigmoid(z_ga) * (y_a @ w_branch_a[l])
                 + jax.nn.sigmoid(z_gb) * (y_b @ w_branch_b[l]))
        x = x + rms_norm(mixed @ w_out[l], g_post_mix[l])
        h = rms_norm(x, g_pre_cross[l])
        m = rms_norm(mem, g_mem[l])
        x = x + rms_norm(memory_cross_attention(h, m, w_q_mem[l], w_kv_mem[l], w_o_mem[l]),
                         g_post_cross[l])
        h = rms_norm(x, g_pre_ffn[l])
        x = x + rms_norm(swiglu_ffn(h, w_ffn_in[l], w_ffn_out[l]), g_post_ffn[l])
    return x


import jax as _jax
import jax.numpy as _jnp

TWIN_FORMAT = 'train_step'
FWD_PARAMS = ['x', 'mem', 'g_pre_mix', 'g_post_mix', 'w_in', 'shift_mix', 'decay_base', 'decay_up', 'iclr_base', 'iclr_up', 'gate_up', 'key_norm_scale', 'key_iclr_scale', 'bonus_scale', 'lnx_w', 'lnx_b', 'rel_bias', 'w_branch_a', 'w_branch_b', 'w_out', 'g_pre_cross', 'g_post_cross', 'g_mem', 'w_q_mem', 'w_kv_mem', 'w_o_mem', 'g_pre_ffn', 'g_post_ffn', 'w_ffn_in', 'w_ffn_out']
TWIN_WEIGHTS = ['g_pre_mix', 'g_post_mix', 'w_in', 'shift_mix', 'decay_base', 'decay_up', 'iclr_base', 'iclr_up', 'gate_up', 'key_norm_scale', 'key_iclr_scale', 'bonus_scale', 'lnx_w', 'lnx_b', 'rel_bias', 'w_branch_a', 'w_branch_b', 'w_out', 'g_pre_cross', 'g_post_cross', 'g_mem', 'w_q_mem', 'w_kv_mem', 'w_o_mem', 'g_pre_ffn', 'g_post_ffn', 'w_ffn_in', 'w_ffn_out']
TWIN_DIFF_INPUT = 'x'
TWIN_INPUTS = ['x', 'mem', 'g_pre_mix', 'g_post_mix', 'w_in', 'shift_mix', 'decay_base', 'decay_up', 'iclr_base', 'iclr_up', 'gate_up', 'key_norm_scale', 'key_iclr_scale', 'bonus_scale', 'lnx_w', 'lnx_b', 'rel_bias', 'w_branch_a', 'w_branch_b', 'w_out', 'g_pre_cross', 'g_post_cross', 'g_mem', 'w_q_mem', 'w_kv_mem', 'w_o_mem', 'g_pre_ffn', 'g_post_ffn', 'w_ffn_in', 'w_ffn_out', 'loss_target', 'm_g_pre_mix', 'm_g_post_mix', 'm_w_in', 'm_shift_mix', 'm_decay_base', 'm_decay_up', 'm_iclr_base', 'm_iclr_up', 'm_gate_up', 'm_key_norm_scale', 'm_key_iclr_scale', 'm_bonus_scale', 'm_lnx_w', 'm_lnx_b', 'm_rel_bias', 'm_w_branch_a', 'm_w_branch_b', 'm_w_out', 'm_g_pre_cross', 'm_g_post_cross', 'm_g_mem', 'm_w_q_mem', 'm_w_kv_mem', 'm_w_o_mem', 'm_g_pre_ffn', 'm_g_post_ffn', 'm_w_ffn_in', 'm_w_ffn_out', 'v_g_pre_mix', 'v_g_post_mix', 'v_w_in', 'v_shift_mix', 'v_decay_base', 'v_decay_up', 'v_iclr_base', 'v_iclr_up', 'v_gate_up', 'v_key_norm_scale', 'v_key_iclr_scale', 'v_bonus_scale', 'v_lnx_w', 'v_lnx_b', 'v_rel_bias', 'v_w_branch_a', 'v_w_branch_b', 'v_w_out', 'v_g_pre_cross', 'v_g_post_cross', 'v_g_mem', 'v_w_q_mem', 'v_w_kv_mem', 'v_w_o_mem', 'v_g_pre_ffn', 'v_g_post_ffn', 'v_w_ffn_in', 'v_w_ffn_out']
TWIN_OUTPUTS = ['loss', 'grad_x', 'grad_g_pre_mix', 'grad_g_post_mix', 'grad_w_in', 'grad_shift_mix', 'grad_decay_base', 'grad_decay_up', 'grad_iclr_base', 'grad_iclr_up', 'grad_gate_up', 'grad_key_norm_scale', 'grad_key_iclr_scale', 'grad_bonus_scale', 'grad_lnx_w', 'grad_lnx_b', 'grad_rel_bias', 'grad_w_branch_a', 'grad_w_branch_b', 'grad_w_out', 'grad_g_pre_cross', 'grad_g_post_cross', 'grad_g_mem', 'grad_w_q_mem', 'grad_w_kv_mem', 'grad_w_o_mem', 'grad_g_pre_ffn', 'grad_g_post_ffn', 'grad_w_ffn_in', 'grad_w_ffn_out', 'delta_g_pre_mix', 'delta_g_post_mix', 'delta_w_in', 'delta_shift_mix', 'delta_decay_base', 'delta_decay_up', 'delta_iclr_base', 'delta_iclr_up', 'delta_gate_up', 'delta_key_norm_scale', 'delta_key_iclr_scale', 'delta_bonus_scale', 'delta_lnx_w', 'delta_lnx_b', 'delta_rel_bias', 'delta_w_branch_a', 'delta_w_branch_b', 'delta_w_out', 'delta_g_pre_cross', 'delta_g_post_cross', 'delta_g_mem', 'delta_w_q_mem', 'delta_w_kv_mem', 'delta_w_o_mem', 'delta_g_pre_ffn', 'delta_g_post_ffn', 'delta_w_ffn_in', 'delta_w_ffn_out', 'new_m_g_pre_mix', 'new_m_g_post_mix', 'new_m_w_in', 'new_m_shift_mix', 'new_m_decay_base', 'new_m_decay_up', 'new_m_iclr_base', 'new_m_iclr_up', 'new_m_gate_up', 'new_m_key_norm_scale', 'new_m_key_iclr_scale', 'new_m_bonus_scale', 'new_m_lnx_w', 'new_m_lnx_b', 'new_m_rel_bias', 'new_m_w_branch_a', 'new_m_w_branch_b', 'new_m_w_out', 'new_m_g_pre_cross', 'new_m_g_post_cross', 'new_m_g_mem', 'new_m_w_q_mem', 'new_m_w_kv_mem', 'new_m_w_o_mem', 'new_m_g_pre_ffn', 'new_m_g_post_ffn', 'new_m_w_ffn_in', 'new_m_w_ffn_out', 'new_v_g_pre_mix', 'new_v_g_post_mix', 'new_v_w_in', 'new_v_shift_mix', 'new_v_decay_base', 'new_v_decay_up', 'new_v_iclr_base', 'new_v_iclr_up', 'new_v_gate_up', 'new_v_key_norm_scale', 'new_v_key_iclr_scale', 'new_v_bonus_scale', 'new_v_lnx_w', 'new_v_lnx_b', 'new_v_rel_bias', 'new_v_w_branch_a', 'new_v_w_branch_b', 'new_v_w_out', 'new_v_g_pre_cross', 'new_v_g_post_cross', 'new_v_g_mem', 'new_v_w_q_mem', 'new_v_w_kv_mem', 'new_v_w_o_mem', 'new_v_g_pre_ffn', 'new_v_g_post_ffn', 'new_v_w_ffn_in', 'new_v_w_ffn_out']
TWIN_LEAF_KINDS = {'loss': 'loss', 'grad_x': 'grad_x', 'grad_g_pre_mix': 'grad_w', 'grad_g_post_mix': 'grad_w', 'grad_w_in': 'grad_w', 'grad_shift_mix': 'grad_w', 'grad_decay_base': 'grad_w', 'grad_decay_up': 'grad_w', 'grad_iclr_base': 'grad_w', 'grad_iclr_up': 'grad_w', 'grad_gate_up': 'grad_w', 'grad_key_norm_scale': 'grad_w', 'grad_key_iclr_scale': 'grad_w', 'grad_bonus_scale': 'grad_w', 'grad_lnx_w': 'grad_w', 'grad_lnx_b': 'grad_w', 'grad_rel_bias': 'grad_w', 'grad_w_branch_a': 'grad_w', 'grad_w_branch_b': 'grad_w', 'grad_w_out': 'grad_w', 'grad_g_pre_cross': 'grad_w', 'grad_g_post_cross': 'grad_w', 'grad_g_mem': 'grad_w', 'grad_w_q_mem': 'grad_w', 'grad_w_kv_mem': 'grad_w', 'grad_w_o_mem': 'grad_w', 'grad_g_pre_ffn': 'grad_w', 'grad_g_post_ffn': 'grad_w', 'grad_w_ffn_in': 'grad_w', 'grad_w_ffn_out': 'grad_w', 'delta_g_pre_mix': 'delta_w', 'delta_g_post_mix': 'delta_w', 'delta_w_in': 'delta_w', 'delta_shift_mix': 'delta_w', 'delta_decay_base': 'delta_w', 'delta_decay_up': 'delta_w', 'delta_iclr_base': 'delta_w', 'delta_iclr_up': 'delta_w', 'delta_gate_up': 'delta_w', 'delta_key_norm_scale': 'delta_w', 'delta_key_iclr_scale': 'delta_w', 'delta_bonus_scale': 'delta_w', 'delta_lnx_w': 'delta_w', 'delta_lnx_b': 'delta_w', 'delta_rel_bias': 'delta_w', 'delta_w_branch_a': 'delta_w', 'delta_w_branch_b': 'delta_w', 'delta_w_out': 'delta_w', 'delta_g_pre_cross': 'delta_w', 'delta_g_post_cross': 'delta_w', 'delta_g_mem': 'delta_w', 'delta_w_q_mem': 'delta_w', 'delta_w_kv_mem': 'delta_w', 'delta_w_o_mem': 'delta_w', 'delta_g_pre_ffn': 'delta_w', 'delta_g_post_ffn': 'delta_w', 'delta_w_ffn_in': 'delta_w', 'delta_w_ffn_out': 'delta_w', 'new_m_g_pre_mix': 'new_m', 'new_m_g_post_mix': 'new_m', 'new_m_w_in': 'new_m', 'new_m_shift_mix': 'new_m', 'new_m_decay_base': 'new_m', 'new_m_decay_up': 'new_m', 'new_m_iclr_base': 'new_m', 'new_m_iclr_up': 'new_m', 'new_m_gate_up': 'new_m', 'new_m_key_norm_scale': 'new_m', 'new_m_key_iclr_scale': 'new_m', 'new_m_bonus_scale': 'new_m', 'new_m_lnx_w': 'new_m', 'new_m_lnx_b': 'new_m', 'new_m_rel_bias': 'new_m', 'new_m_w_branch_a': 'new_m', 'new_m_w_branch_b': 'new_m', 'new_m_w_out': 'new_m', 'new_m_g_pre_cross': 'new_m', 'new_m_g_post_cross': 'new_m', 'new_m_g_mem': 'new_m', 'new_m_w_q_mem': 'new_m', 'new_m_w_kv_mem': 'new_m', 'new_m_w_o_mem': 'new_m', 'new_m_g_pre_ffn': 'new_m', 'new_m_g_post_ffn': 'new_m', 'new_m_w_ffn_in': 'new_m', 'new_m_w_ffn_out': 'new_m', 'new_v_g_pre_mix': 'new_v', 'new_v_g_post_mix': 'new_v', 'new_v_w_in': 'new_v', 'new_v_shift_mix': 'new_v', 'new_v_decay_base': 'new_v', 'new_v_decay_up': 'new_v', 'new_v_iclr_base': 'new_v', 'new_v_iclr_up': 'new_v', 'new_v_gate_up': 'new_v', 'new_v_key_norm_scale': 'new_v', 'new_v_key_iclr_scale': 'new_v', 'new_v_bonus_scale': 'new_v', 'new_v_lnx_w': 'new_v', 'new_v_lnx_b': 'new_v', 'new_v_rel_bias': 'new_v', 'new_v_w_branch_a': 'new_v', 'new_v_w_branch_b': 'new_v', 'new_v_w_out': 'new_v', 'new_v_g_pre_cross': 'new_v', 'new_v_g_post_cross': 'new_v', 'new_v_g_mem': 'new_v', 'new_v_w_q_mem': 'new_v', 'new_v_w_kv_mem': 'new_v', 'new_v_w_o_mem': 'new_v', 'new_v_g_pre_ffn': 'new_v', 'new_v_g_post_ffn': 'new_v', 'new_v_w_ffn_in': 'new_v', 'new_v_w_ffn_out': 'new_v'}


def _forward(args):
    return _fwd_reference(*[args[k] for k in FWD_PARAMS])


def _output_shape():
    out = _jax.eval_shape(lambda: _forward(_fwd_setup_inputs(0)))
    return out.shape, out.dtype

N_MICROBATCH = 1
ADAM_LR = 0.001
ADAM_B1 = 0.9
ADAM_B2 = 0.999
ADAM_EPS = 1e-08
ADAM_WD = 0.01
ADAM_STEP = 10
PER_EXAMPLE_BATCH_AXIS = {'x': 0, 'mem': 0, 'loss_target': 0}
SHARED_INPUTS = []
_WEIGHT_DTYPES = {'g_pre_mix': _jnp.float32, 'g_post_mix': _jnp.float32, 'w_in': _jnp.float32, 'shift_mix': _jnp.float32, 'decay_base': _jnp.float32, 'decay_up': _jnp.float32, 'iclr_base': _jnp.float32, 'iclr_up': _jnp.float32, 'gate_up': _jnp.float32, 'key_norm_scale': _jnp.float32, 'key_iclr_scale': _jnp.float32, 'bonus_scale': _jnp.float32, 'lnx_w': _jnp.float32, 'lnx_b': _jnp.float32, 'rel_bias': _jnp.float32, 'w_branch_a': _jnp.float32, 'w_branch_b': _jnp.float32, 'w_out': _jnp.float32, 'g_pre_cross': _jnp.float32, 'g_post_cross': _jnp.float32, 'g_mem': _jnp.float32, 'w_q_mem': _jnp.float32, 'w_kv_mem': _jnp.float32, 'w_o_mem': _jnp.float32, 'g_pre_ffn': _jnp.float32, 'g_post_ffn': _jnp.float32, 'w_ffn_in': _jnp.float32, 'w_ffn_out': _jnp.float32}
MOMENT_SCALE = {'g_pre_mix': 1.229192e+00, 'g_post_mix': 3.184033e+01, 'w_in': 3.999676e-01, 'shift_mix': 9.559677e-01, 'decay_base': 2.637411e-01, 'decay_up': 3.011407e-02, 'iclr_base': 2.164379e-01, 'iclr_up': 1.980085e-01, 'gate_up': 6.570559e-01, 'key_norm_scale': 8.162913e-01, 'key_iclr_scale': 6.729345e-01, 'bonus_scale': 1.281859e+00, 'lnx_w': 8.237700e-01, 'lnx_b': 8.470877e+00, 'rel_bias': 7.727919e-02, 'w_branch_a': 7.422250e-01, 'w_branch_b': 2.748605e-01, 'w_out': 8.281301e-01, 'g_pre_cross': 7.174454e-01, 'g_post_cross': 3.297164e+01, 'g_mem': 2.039888e+00, 'w_q_mem': 9.460748e-01, 'w_kv_mem': 2.011412e+00, 'w_o_mem': 1.971168e+00, 'g_pre_ffn': 1.548124e+00, 'g_post_ffn': 3.184347e+01, 'w_ffn_in': 6.584903e-01, 'w_ffn_out': 1.339730e+00}


def _to_microbatches(a, axis):
    t = _jnp.moveaxis(a, axis, 0)
    t = t.reshape((N_MICROBATCH, t.shape[0] // N_MICROBATCH) + t.shape[1:])
    return _jnp.moveaxis(t, 1, axis + 1)


def setup_inputs(seed: int = 0) -> dict:
    inp = _fwd_setup_inputs(seed)
    key = _jax.random.fold_in(_jax.random.key(seed), 7919)
    shape, _ = _output_shape()
    out = dict(inp)
    out["loss_target"] = _jax.random.normal(_jax.random.fold_in(key, 0), shape, _jnp.float32)
    for i, name in enumerate(TWIN_WEIGHTS):
        w = inp[name].astype(_jnp.float32)
        if MOMENT_SCALE is None:
            s = _jnp.sqrt(_jnp.mean(_jnp.square(w)) + 1e-30)
        else:
            s = MOMENT_SCALE[name]
        km, kv = _jax.random.split(_jax.random.fold_in(key, i + 1))
        out[name] = w
        out["m_" + name] = s * _jax.random.normal(km, w.shape, _jnp.float32)
        out["v_" + name] = (s * s) * _jax.random.uniform(kv, w.shape, _jnp.float32, 0.5, 1.5)
    if N_MICROBATCH > 1:
        for name, axis in PER_EXAMPLE_BATCH_AXIS.items():
            out[name] = _to_microbatches(out[name], axis)
    return {'x': out['x'], 'mem': out['mem'], 'g_pre_mix': out['g_pre_mix'], 'g_post_mix': out['g_post_mix'], 'w_in': out['w_in'], 'shift_mix': out['shift_mix'], 'decay_base': out['decay_base'], 'decay_up': out['decay_up'], 'iclr_base': out['iclr_base'], 'iclr_up': out['iclr_up'], 'gate_up': out['gate_up'], 'key_norm_scale': out['key_norm_scale'], 'key_iclr_scale': out['key_iclr_scale'], 'bonus_scale': out['bonus_scale'], 'lnx_w': out['lnx_w'], 'lnx_b': out['lnx_b'], 'rel_bias': out['rel_bias'], 'w_branch_a': out['w_branch_a'], 'w_branch_b': out['w_branch_b'], 'w_out': out['w_out'], 'g_pre_cross': out['g_pre_cross'], 'g_post_cross': out['g_post_cross'], 'g_mem': out['g_mem'], 'w_q_mem': out['w_q_mem'], 'w_kv_mem': out['w_kv_mem'], 'w_o_mem': out['w_o_mem'], 'g_pre_ffn': out['g_pre_ffn'], 'g_post_ffn': out['g_post_ffn'], 'w_ffn_in': out['w_ffn_in'], 'w_ffn_out': out['w_ffn_out'], 'loss_target': out['loss_target'], 'm_g_pre_mix': out['m_g_pre_mix'], 'm_g_post_mix': out['m_g_post_mix'], 'm_w_in': out['m_w_in'], 'm_shift_mix': out['m_shift_mix'], 'm_decay_base': out['m_decay_base'], 'm_decay_up': out['m_decay_up'], 'm_iclr_base': out['m_iclr_base'], 'm_iclr_up': out['m_iclr_up'], 'm_gate_up': out['m_gate_up'], 'm_key_norm_scale': out['m_key_norm_scale'], 'm_key_iclr_scale': out['m_key_iclr_scale'], 'm_bonus_scale': out['m_bonus_scale'], 'm_lnx_w': out['m_lnx_w'], 'm_lnx_b': out['m_lnx_b'], 'm_rel_bias': out['m_rel_bias'], 'm_w_branch_a': out['m_w_branch_a'], 'm_w_branch_b': out['m_w_branch_b'], 'm_w_out': out['m_w_out'], 'm_g_pre_cross': out['m_g_pre_cross'], 'm_g_post_cross': out['m_g_post_cross'], 'm_g_mem': out['m_g_mem'], 'm_w_q_mem': out['m_w_q_mem'], 'm_w_kv_mem': out['m_w_kv_mem'], 'm_w_o_mem': out['m_w_o_mem'], 'm_g_pre_ffn': out['m_g_pre_ffn'], 'm_g_post_ffn': out['m_g_post_ffn'], 'm_w_ffn_in': out['m_w_ffn_in'], 'm_w_ffn_out': out['m_w_ffn_out'], 'v_g_pre_mix': out['v_g_pre_mix'], 'v_g_post_mix': out['v_g_post_mix'], 'v_w_in': out['v_w_in'], 'v_shift_mix': out['v_shift_mix'], 'v_decay_base': out['v_decay_base'], 'v_decay_up': out['v_decay_up'], 'v_iclr_base': out['v_iclr_base'], 'v_iclr_up': out['v_iclr_up'], 'v_gate_up': out['v_gate_up'], 'v_key_norm_scale': out['v_key_norm_scale'], 'v_key_iclr_scale': out['v_key_iclr_scale'], 'v_bonus_scale': out['v_bonus_scale'], 'v_lnx_w': out['v_lnx_w'], 'v_lnx_b': out['v_lnx_b'], 'v_rel_bias': out['v_rel_bias'], 'v_w_branch_a': out['v_w_branch_a'], 'v_w_branch_b': out['v_w_branch_b'], 'v_w_out': out['v_w_out'], 'v_g_pre_cross': out['v_g_pre_cross'], 'v_g_post_cross': out['v_g_post_cross'], 'v_g_mem': out['v_g_mem'], 'v_w_q_mem': out['v_w_q_mem'], 'v_w_kv_mem': out['v_w_kv_mem'], 'v_w_o_mem': out['v_w_o_mem'], 'v_g_pre_ffn': out['v_g_pre_ffn'], 'v_g_post_ffn': out['v_g_post_ffn'], 'v_w_ffn_in': out['v_w_ffn_in'], 'v_w_ffn_out': out['v_w_ffn_out']}


def _loss(weights, diff, rest, loss_target):
    with _jax.named_scope("forward"):
        args = {**rest, TWIN_DIFF_INPUT: diff, **{k: w.astype(_WEIGHT_DTYPES[k]) for k, w in weights.items()}}
        y = _forward(args)
    with _jax.named_scope("loss_head"):
        err = _jnp.square(y.astype(_jnp.float32) - loss_target)
        return 0.5 * _jnp.sum(_jnp.mean(err, axis=-1)) if err.ndim else 0.5 * err


def _adamw(w, g, m, v):
    m = ADAM_B1 * m + (1.0 - ADAM_B1) * g
    v = ADAM_B2 * v + (1.0 - ADAM_B2) * _jnp.square(g)
    m_hat = m / (1.0 - ADAM_B1 ** ADAM_STEP)
    v_hat = v / (1.0 - ADAM_B2 ** ADAM_STEP)
    delta = -ADAM_LR * (m_hat / (_jnp.sqrt(v_hat) + ADAM_EPS) + ADAM_WD * w)
    return delta, m, v


def reference(x, mem, g_pre_mix, g_post_mix, w_in, shift_mix, decay_base, decay_up, iclr_base, iclr_up, gate_up, key_norm_scale, key_iclr_scale, bonus_scale, lnx_w, lnx_b, rel_bias, w_branch_a, w_branch_b, w_out, g_pre_cross, g_post_cross, g_mem, w_q_mem, w_kv_mem, w_o_mem, g_pre_ffn, g_post_ffn, w_ffn_in, w_ffn_out, loss_target, m_g_pre_mix, m_g_post_mix, m_w_in, m_shift_mix, m_decay_base, m_decay_up, m_iclr_base, m_iclr_up, m_gate_up, m_key_norm_scale, m_key_iclr_scale, m_bonus_scale, m_lnx_w, m_lnx_b, m_rel_bias, m_w_branch_a, m_w_branch_b, m_w_out, m_g_pre_cross, m_g_post_cross, m_g_mem, m_w_q_mem, m_w_kv_mem, m_w_o_mem, m_g_pre_ffn, m_g_post_ffn, m_w_ffn_in, m_w_ffn_out, v_g_pre_mix, v_g_post_mix, v_w_in, v_shift_mix, v_decay_base, v_decay_up, v_iclr_base, v_iclr_up, v_gate_up, v_key_norm_scale, v_key_iclr_scale, v_bonus_scale, v_lnx_w, v_lnx_b, v_rel_bias, v_w_branch_a, v_w_branch_b, v_w_out, v_g_pre_cross, v_g_post_cross, v_g_mem, v_w_q_mem, v_w_kv_mem, v_w_o_mem, v_g_pre_ffn, v_g_post_ffn, v_w_ffn_in, v_w_ffn_out):
    given = dict(x=x, mem=mem, g_pre_mix=g_pre_mix, g_post_mix=g_post_mix, w_in=w_in, shift_mix=shift_mix, decay_base=decay_base, decay_up=decay_up, iclr_base=iclr_base, iclr_up=iclr_up, gate_up=gate_up, key_norm_scale=key_norm_scale, key_iclr_scale=key_iclr_scale, bonus_scale=bonus_scale, lnx_w=lnx_w, lnx_b=lnx_b, rel_bias=rel_bias, w_branch_a=w_branch_a, w_branch_b=w_branch_b, w_out=w_out, g_pre_cross=g_pre_cross, g_post_cross=g_post_cross, g_mem=g_mem, w_q_mem=w_q_mem, w_kv_mem=w_kv_mem, w_o_mem=w_o_mem, g_pre_ffn=g_pre_ffn, g_post_ffn=g_post_ffn, w_ffn_in=w_ffn_in, w_ffn_out=w_ffn_out, loss_target=loss_target, m_g_pre_mix=m_g_pre_mix, m_g_post_mix=m_g_post_mix, m_w_in=m_w_in, m_shift_mix=m_shift_mix, m_decay_base=m_decay_base, m_decay_up=m_decay_up, m_iclr_base=m_iclr_base, m_iclr_up=m_iclr_up, m_gate_up=m_gate_up, m_key_norm_scale=m_key_norm_scale, m_key_iclr_scale=m_key_iclr_scale, m_bonus_scale=m_bonus_scale, m_lnx_w=m_lnx_w, m_lnx_b=m_lnx_b, m_rel_bias=m_rel_bias, m_w_branch_a=m_w_branch_a, m_w_branch_b=m_w_branch_b, m_w_out=m_w_out, m_g_pre_cross=m_g_pre_cross, m_g_post_cross=m_g_post_cross, m_g_mem=m_g_mem, m_w_q_mem=m_w_q_mem, m_w_kv_mem=m_w_kv_mem, m_w_o_mem=m_w_o_mem, m_g_pre_ffn=m_g_pre_ffn, m_g_post_ffn=m_g_post_ffn, m_w_ffn_in=m_w_ffn_in, m_w_ffn_out=m_w_ffn_out, v_g_pre_mix=v_g_pre_mix, v_g_post_mix=v_g_post_mix, v_w_in=v_w_in, v_shift_mix=v_shift_mix, v_decay_base=v_decay_base, v_decay_up=v_decay_up, v_iclr_base=v_iclr_base, v_iclr_up=v_iclr_up, v_gate_up=v_gate_up, v_key_norm_scale=v_key_norm_scale, v_key_iclr_scale=v_key_iclr_scale, v_bonus_scale=v_bonus_scale, v_lnx_w=v_lnx_w, v_lnx_b=v_lnx_b, v_rel_bias=v_rel_bias, v_w_branch_a=v_w_branch_a, v_w_branch_b=v_w_branch_b, v_w_out=v_w_out, v_g_pre_cross=v_g_pre_cross, v_g_post_cross=v_g_post_cross, v_g_mem=v_g_mem, v_w_q_mem=v_w_q_mem, v_w_kv_mem=v_w_kv_mem, v_w_o_mem=v_w_o_mem, v_g_pre_ffn=v_g_pre_ffn, v_g_post_ffn=v_g_post_ffn, v_w_ffn_in=v_w_ffn_in, v_w_ffn_out=v_w_ffn_out)
    weights = {n: given[n] for n in TWIN_WEIGHTS}
    shared = {n: given[n] for n in SHARED_INPUTS}
    per_example = {n: given[n] for n in ['x', 'mem']}
    grad_fn = _jax.value_and_grad(_loss, argnums=(0, 1))

    def one_microbatch(ex, loss_target):
        ex = dict(ex)
        diff = ex.pop(TWIN_DIFF_INPUT)
        return grad_fn(weights, diff, {**shared, **ex}, loss_target)

    if N_MICROBATCH == 1:
        loss, (grad_w, grad_x) = one_microbatch(per_example, given["loss_target"])
    else:
        def body(carry, xs):
            loss_sum, grad_sum = carry
            l_k, (gw_k, gx_k) = one_microbatch(xs[0], xs[1])
            with _jax.named_scope("update"):
                return (loss_sum + l_k, _jax.tree.map(_jnp.add, grad_sum, gw_k)), gx_k

        init = (_jnp.zeros((), _jnp.float32), _jax.tree.map(_jnp.zeros_like, weights))
        (loss, grad_w), grad_x = _jax.lax.scan(body, init, (per_example, given["loss_target"]))
    with _jax.named_scope("update"):
        delta_w, new_m, new_v = {}, {}, {}
        for n in TWIN_WEIGHTS:
            delta_w[n], new_m[n], new_v[n] = _adamw(weights[n], grad_w[n], given["m_" + n], given["v_" + n])
    return (loss, grad_x, *[grad_w[n] for n in TWIN_WEIGHTS], *[delta_w[n] for n in TWIN_WEIGHTS],
            *[new_m[n] for n in TWIN_WEIGHTS], *[new_v[n] for n in TWIN_WEIGHTS])
```

```python
import functools
import math

import jax
import jax.numpy as jnp
from jax import lax
from jax.experimental import pallas as pl
from jax.experimental.pallas import tpu as pltpu

F32 = jnp.float32
BF16 = jnp.bfloat16

N_DEV = 8
D = 1024
HEAD = 64
N_HEADS = D // HEAD
LANE = 128
N_PAIRS = D // LANE
CHUNK = 64
LEFT = 8 * CHUNK
BAND = LEFT + CHUNK
REL_CLIP = 128
REL_TABLE = CHUNK + REL_CLIP
MEM_WIDTH = D // 2
MEM_HEADS = 4
FFN = 2816
LORA_W, LORA_A, LORA_G = 64, 64, 160
P_WIDTH = 3 * D + 3 * D + 2 * D + 128 + 128 + 256
C_Q, C_GA, C_LORA = 3 * D, 6 * D, 8 * D
NORM_EPS = 1e-6
GROUP_NORM_EPS = 64e-5
MASK_VALUE = -1e30
ADAM_LR, ADAM_B1, ADAM_B2, ADAM_EPS, ADAM_WD, ADAM_STEP = 0.001, 0.9, 0.999, 1e-08, 0.01, 10
VMEM_LIMIT = 56 * 1024 * 1024


def _cp(*sem):
    return pltpu.CompilerParams(dimension_semantics=sem, vmem_limit_bytes=VMEM_LIMIT)


_NN, _NT, _TN = ((1,), (0,)), ((1,), (1,)), ((0,), (0,))


def _dot_raw(a, b, dims):
    return lax.dot_general(a.astype(BF16), b.astype(BF16), (dims, ((), ())), preferred_element_type=F32)


@functools.partial(jax.custom_vjp, nondiff_argnums=(2,))
def _dot_dims(a, b, dims):
    return _dot_raw(a, b, dims)


def _dot_dims_fwd(a, b, dims):
    return _dot_raw(a, b, dims), (a, b)


def _dot_dims_bwd(dims, res, g):
    a, b = res
    if dims == _NN:
        da, db = _dot_raw(g, b, _NT), _dot_raw(a, g, _TN)
    elif dims == _NT:
        da, db = _dot_raw(g, b, _NN), _dot_raw(g, a, _TN)
    else:
        da, db = _dot_raw(b, g, _NT), _dot_raw(a, g, _NN)
    return da.astype(a.dtype), db.astype(b.dtype)


_dot_dims.defvjp(_dot_dims_fwd, _dot_dims_bwd)


def _dot(a, b, dims=_NN):
    return _dot_dims(a, b, dims)


def _dot_nt(a, b):
    return _dot_dims(a, b, _NT)


def _dot_tn(a, b):
    return _dot_dims(a, b, _TN)


def _split(x, terms):
    parts, rest = [], x.astype(F32)
    for _ in range(terms):
        p = rest.astype(BF16)
        parts.append(p)
        rest = rest - p.astype(F32)
    return parts


def _dot_split_a(a, b, terms=2):
    out = None
    for p in _split(a, terms):
        t = _dot(p, b)
        out = t if out is None else out + t
    return out


def _dot_split_b(a, b, terms=3):
    out = None
    for p in _split(b, terms):
        t = _dot(a, p)
        out = t if out is None else out + t
    return out


def _dot_hi(a, b, dims=_NN):
    ah, al = _split(a, 2)
    bh, bl = _split(b, 2)
    return _dot(ah, bh, dims) + (_dot(ah, bl, dims) + _dot(al, bh, dims))


def _mm(a, b, *, name, ta=False, tb=False, out_dtype=F32, tm=512, tn=512, tk=512, split_a=1):
    m, k = (a.shape[1], a.shape[0]) if ta else a.shape
    n, kb = (b.shape[0], b.shape[1]) if tb else (b.shape[1], b.shape[0])
    assert k == kb, (a.shape, b.shape, ta, tb)
    tm, tn, tk = min(tm, m), min(tn, n), min(tk, k)
    assert m % tm == 0 and n % tn == 0 and k % tk == 0, (m, n, k, tm, tn, tk)
    nk = k // tk
    dims = ((0 if ta else 1,), (1 if tb else 0,))

    def body(a_ref, b_ref, o_ref, acc_ref):
        kk = pl.program_id(2)

        @pl.when(kk == 0)
        def _():
            acc_ref[...] = jnp.zeros_like(acc_ref)

        if split_a == 1:
            acc_ref[...] += _dot_raw(a_ref[...], b_ref[...], dims)
        else:
            for p in _split(a_ref[...], split_a):
                acc_ref[...] += _dot_raw(p, b_ref[...], dims)

        @pl.when(kk == nk - 1)
        def _():
            o_ref[...] = acc_ref[...].astype(o_ref.dtype)

    a_spec = pl.BlockSpec((tk, tm), lambda i, j, q: (q, i)) if ta else pl.BlockSpec((tm, tk), lambda i, j, q: (i, q))
    b_spec = pl.BlockSpec((tn, tk), lambda i, j, q: (j, q)) if tb else pl.BlockSpec((tk, tn), lambda i, j, q: (q, j))
    return pl.pallas_call(
        body, name=name, grid=(m // tm, n // tn, nk),
        in_specs=[a_spec, b_spec], out_specs=pl.BlockSpec((tm, tn), lambda i, j, q: (i, j)),
        out_shape=jax.ShapeDtypeStruct((m, n), out_dtype),
        scratch_shapes=[pltpu.VMEM((tm, tn), F32)],
        compiler_params=_cp("parallel", "parallel", "arbitrary"),
    )(a, b)


def _win(arr, start=0, width=None):
    width = arr.shape[1] if width is None else width
    assert start % width == 0
    return (arr, start // width, width)


def _row_specs(rows, tm):
    return [pl.BlockSpec((tm, w), functools.partial(lambda i, cb: (i, cb), cb=cb)) for (_, cb, w) in rows]


def _full_spec(p):
    nd = p.ndim
    return pl.BlockSpec(p.shape, lambda i, nd=nd: (0,) * nd)


def _rowwise(fn, rows, params, outs, *, name, tm):
    t = rows[0][0].shape[0]
    tm = min(tm, t)
    assert t % tm == 0
    nr, npar = len(rows), len(params)

    def body(*refs):
        vals = [r[...] for r in refs[:nr + npar]]
        res = fn(*vals)
        for o_ref, r in zip(refs[nr + npar:], res):
            o_ref[...] = r.astype(o_ref.dtype)

    return pl.pallas_call(
        body, name=name, grid=(t // tm,),
        in_specs=_row_specs(rows, tm) + [_full_spec(p) for p in params],
        out_specs=[pl.BlockSpec((tm, w), lambda i: (i, 0)) for (w, _) in outs],
        out_shape=[jax.ShapeDtypeStruct((t, w), dt) for (w, dt) in outs],
        compiler_params=_cp("parallel"),
    )(*[r[0] for r in rows], *params)


def _rowwise_bwd(fn, rows, params, n_const, cots, *, name, tm, row_grad, add_to=None):
    t = rows[0][0].shape[0]
    tm = min(tm, t)
    assert t % tm == 0
    nr, npar = len(rows), len(params)
    ndp = npar - n_const
    add_to = add_to or {}
    add_idx = sorted(add_to)
    flat_cots = [c for group in cots for c in group]
    kept = [i for i in range(nr) if row_grad[i] is not None]

    def body(*refs):
        pos = 0
        row_v = [r[...] for r in refs[pos:pos + nr]]; pos += nr
        par_v = [r[...] for r in refs[pos:pos + npar]]; pos += npar
        cot_v = [r[...] for r in refs[pos:pos + len(flat_cots)]]; pos += len(flat_cots)
        add_v = [r[...] for r in refs[pos:pos + len(add_idx)]]; pos += len(add_idx)
        rg_refs = refs[pos:pos + len(kept)]; pos += len(kept)
        pg_refs = refs[pos:pos + ndp]

        consts = par_v[ndp:]
        res, vjp = jax.vjp(lambda *args: tuple(fn(*args, *consts)), *row_v, *par_v[:ndp])
        cot_in, q = [], 0
        for j, group in enumerate(cots):
            c = None
            for _ in group:
                cv = cot_v[q].astype(F32); q += 1
                c = cv if c is None else c + cv
            c = jnp.zeros(res[j].shape, F32) if c is None else c
            cot_in.append(c.astype(res[j].dtype))
        grads = vjp(tuple(cot_in))
        for ref, i in zip(rg_refs, kept):
            g = grads[i].astype(F32)
            if i in add_to:
                g = g + add_v[add_idx.index(i)].astype(F32)
            ref[...] = g.astype(ref.dtype)

        @pl.when(pl.program_id(0) == 0)
        def _():
            for ref in pg_refs:
                ref[...] = jnp.zeros_like(ref)

        for ref, g in zip(pg_refs, grads[nr:]):
            ref[...] += g.astype(F32)

    cot_specs = [pl.BlockSpec((tm, c.shape[1]), lambda i: (i, 0)) for c in flat_cots]
    add_specs = [pl.BlockSpec((tm, add_to[i].shape[1]), lambda i_: (i_, 0)) for i in add_idx]
    out_specs = [pl.BlockSpec((tm, rows[i][2]), lambda i_: (i_, 0)) for i in kept] + [_full_spec(p) for p in params[:ndp]]
    out_shape = [jax.ShapeDtypeStruct((t, rows[i][2]), row_grad[i]) for i in kept] + [
        jax.ShapeDtypeStruct(p.shape, F32) for p in params[:ndp]]
    res = pl.pallas_call(
        body, name=name, grid=(t // tm,),
        in_specs=_row_specs(rows, tm) + [_full_spec(p) for p in params] + cot_specs + add_specs,
        out_specs=out_specs, out_shape=out_shape,
        compiler_params=_cp("arbitrary"),
    )(*[r[0] for r in rows], *params, *flat_cots, *[add_to[i] for i in add_idx])
    return list(res[:len(kept)]), list(res[len(kept):])


def _rms(x, g):
    xf = x.astype(F32)
    return xf * lax.rsqrt(jnp.mean(xf * xf, axis=-1, keepdims=True) + NORM_EPS) * g


def _softplus(x):
    return jnp.maximum(x, 0.0) + jnp.log(1.0 + jnp.exp(-jnp.abs(x)))


def _fn_pre(x, g):
    return (_rms(x, g).astype(BF16),)


def _fn_res(x, u, g_post):
    return (x + _rms(u, g_post),)


def _fn_res_pre(x, u, g_post, g_pre):
    xn = x + _rms(u, g_post)
    return xn, _rms(xn, g_pre).astype(BF16)


def _fn_mix(zga, zgb, ya, yb):
    return ((jax.nn.sigmoid(zga) * ya + jax.nn.sigmoid(zgb) * yb).astype(BF16),)


def _fn_swiglu(gate, up):
    return ((gate * jax.nn.sigmoid(gate) * up).astype(BF16),)


def _fn_prep(zk, zw, za, zg, decay_base, d_up, iclr_base, i_up, g_up, kns, kis, e_hd, e_dh):
    w_log = -_softplus(-(decay_base + _dot(jnp.tanh(zw), d_up))) - 0.5
    lw = -jnp.exp(w_log)
    a = jax.nn.sigmoid(iclr_base + _dot(za, i_up))
    g = _dot(jax.nn.sigmoid(zg), g_up)
    kn = zk * kns
    ss = _dot_split_a(kn * kn, e_dh)
    inv = lax.rsqrt(jnp.maximum(ss, 1e-24))
    kk = kn * _dot_split_a(inv, e_hd)
    k2 = zk * (1.0 + (a - 1.0) * kis)
    return lw, k2, kk, a, g


def _fn_post(y, r, k2, v, g, lnx_w, lnx_b, bonus, e_hd, e_dh):
    mu = _dot_split_a(_dot_split_a(y, e_dh) * (1.0 / HEAD), e_hd)
    yc = y - mu
    var = _dot_split_a(yc * yc, e_dh) * (1.0 / HEAD)
    yn = yc * _dot_split_a(lax.rsqrt(var + GROUP_NORM_EPS), e_hd)
    bs = _dot_split_a(_dot_split_a(r * k2 * bonus, e_dh), e_hd)
    return (((yn * lnx_w + lnx_b + bs * v) * g).astype(BF16),)


def _shift_fwd(p, col0, ncols, mix, seq, *, name, cw=256):
    t = p.shape[0]
    assert col0 % cw == 0 and ncols % cw == 0 and t % seq == 0
    cb0 = col0 // cw

    def body(p_ref, m_ref, z_ref):
        pv = p_ref[...]
        row = lax.broadcasted_iota(jnp.int32, pv.shape, 0)
        prev = jnp.where(row == 0, 0.0, pltpu.roll(pv, 1, axis=0))
        z_ref[...] = pv + (prev - pv) * m_ref[...]

    return pl.pallas_call(
        body, name=name, grid=(t // seq, ncols // cw),
        in_specs=[pl.BlockSpec((seq, cw), lambda b, c: (b, c + cb0)), pl.BlockSpec((1, cw), lambda b, c: (0, c))],
        out_specs=pl.BlockSpec((seq, cw), lambda b, c: (b, c)),
        out_shape=jax.ShapeDtypeStruct((t, ncols), F32),
        compiler_params=_cp("parallel", "parallel"),
    )(p, mix)


def _shift_bwd(p, col0, ncols, mix, dz_parts, seq, *, name, cw=256):
    t = p.shape[0]
    cb0 = col0 // cw
    n = len(dz_parts)

    def body(*refs):
        p_ref, m_ref = refs[:2]
        dp_ref, dm_ref = refs[2 + n:]
        dz = refs[2][...].astype(F32)
        for r in refs[3:2 + n]:
            dz = dz + r[...].astype(F32)
        pv = p_ref[...]
        mixv = m_ref[...]
        row = lax.broadcasted_iota(jnp.int32, pv.shape, 0)
        prev = jnp.where(row == 0, 0.0, pltpu.roll(pv, 1, axis=0))
        u = dz * mixv
        nxt = jnp.where(row == seq - 1, 0.0, pltpu.roll(u, seq - 1, axis=0))
        dp_ref[...] = (dz - u + nxt).astype(dp_ref.dtype)

        @pl.when(pl.program_id(1) == 0)
        def _():
            dm_ref[...] = jnp.zeros_like(dm_ref)

        dm_ref[...] += jnp.sum(dz * (prev - pv), axis=0, keepdims=True)

    return pl.pallas_call(
        body, name=name, grid=(ncols // cw, t // seq),
        in_specs=[pl.BlockSpec((seq, cw), lambda c, b: (b, c + cb0)), pl.BlockSpec((1, cw), lambda c, b: (0, c))]
        + [pl.BlockSpec((seq, cw), lambda c, b: (b, c))] * n,
        out_specs=[pl.BlockSpec((seq, cw), lambda c, b: (b, c)), pl.BlockSpec((1, cw), lambda c, b: (0, c))],
        out_shape=[jax.ShapeDtypeStruct((t, ncols), BF16), jax.ShapeDtypeStruct((1, ncols), F32)],
        compiler_params=_cp("parallel", "arbitrary"),
    )(p, mix, *dz_parts)


def _tri_inv(low):
    c = low.shape[0]
    eye = (lax.broadcasted_iota(jnp.int32, (c, c), 0) == lax.broadcasted_iota(jnp.int32, (c, c), 1)).astype(F32)
    acc, pw = eye - low, low
    for _ in range(int(math.log2(c)) - 1):
        pw = _dot_hi(pw, pw)
        acc = _dot_hi(acc, eye + pw)
    return acc


def _wkv_chunk(s0, r, lw, k, v, kk, a):
    c = r.shape[0]
    ti = lax.broadcasted_iota(jnp.int32, (c, c), 0)
    si = lax.broadcasted_iota(jnp.int32, (c, c), 1)
    incl, strict = ti >= si, ti > si
    cum = _dot_split_b(incl.astype(F32), lw, 3)
    eg, egp, ei = jnp.exp(cum), jnp.exp(cum - lw), jnp.exp(-cum)
    rh, kkh, kt, bt = r * eg, kk * egp, k * ei, (a * kk) * ei
    lb = jnp.where(strict, _dot_nt(kkh, bt), 0.0)
    lk = jnp.where(strict, _dot_nt(kkh, kt), 0.0)
    mb = jnp.where(incl, _dot_nt(rh, bt), 0.0)
    mk = jnp.where(incl, _dot_nt(rh, kt), 0.0)
    rhs = _dot_nt(kkh, s0) + _dot(lk, v)
    u = -_dot_hi(_tri_inv(lb), rhs)
    y = _dot_nt(rh, s0) + _dot(mb, u) + _dot(mk, v)
    g_end = jnp.exp(jnp.sum(lw, axis=0, keepdims=True))
    s1 = (s0 + _dot_tn(u, bt) + _dot_tn(v, kt)) * g_end
    return y, s1


def _wkv_specs(seq, rev):
    nc = seq // CHUNK

    def rows(col_blk0):
        if rev:
            return pl.BlockSpec((CHUNK, LANE), lambda b, h, c: (b * nc + nc - 1 - c, col_blk0 + h))
        return pl.BlockSpec((CHUNK, LANE), lambda b, h, c: (b * nc + c, col_blk0 + h))

    if rev:
        st = pl.BlockSpec((1, 1, 2, HEAD, HEAD), lambda b, h, c: (b * N_PAIRS + h, nc - 1 - c, 0, 0, 0))
    else:
        st = pl.BlockSpec((1, 1, 2, HEAD, HEAD), lambda b, h, c: (b * N_PAIRS + h, c, 0, 0, 0))
    return rows, st


def _wkv_fwd(z_rkv, lw, k2, kk, a, seq):
    t = z_rkv.shape[0]
    nb, nc = t // seq, seq // CHUNK
    rows, st = _wkv_specs(seq, False)

    def body(r_ref, v_ref, lw_ref, k_ref, kk_ref, a_ref, y_ref, st_ref, s_scr):
        @pl.when(pl.program_id(2) == 0)
        def _():
            s_scr[...] = jnp.zeros_like(s_scr)

        for h in range(2):
            sl = slice(h * HEAD, (h + 1) * HEAD)
            s0 = s_scr[h]
            st_ref[0, 0, h] = s0
            y, s1 = _wkv_chunk(s0, r_ref[:, sl], lw_ref[:, sl], k_ref[:, sl], v_ref[:, sl], kk_ref[:, sl], a_ref[:, sl])
            y_ref[:, sl] = y
            s_scr[h] = s1

    return pl.pallas_call(
        body, name="wkv_fwd", grid=(nb, N_PAIRS, nc),
        in_specs=[rows(0), rows(2 * N_PAIRS), rows(0), rows(0), rows(0), rows(0)],
        out_specs=[rows(0), st],
        out_shape=[jax.ShapeDtypeStruct((t, D), F32), jax.ShapeDtypeStruct((nb * N_PAIRS, nc, 2, HEAD, HEAD), F32)],
        scratch_shapes=[pltpu.VMEM((2, HEAD, HEAD), F32)],
        compiler_params=_cp("parallel", "parallel", "arbitrary"),
    )(z_rkv, z_rkv, lw, k2, kk, a)


def _wkv_bwd(z_rkv, lw, k2, kk, a, states, dy, seq):
    t = z_rkv.shape[0]
    nb, nc = t // seq, seq // CHUNK
    rows, st = _wkv_specs(seq, True)

    def body(r_ref, v_ref, lw_ref, k_ref, kk_ref, a_ref, st_ref, dy_ref,
             dr_ref, dlw_ref, dk_ref, dv_ref, dkk_ref, da_ref, ds_scr):
        @pl.when(pl.program_id(2) == 0)
        def _():
            ds_scr[...] = jnp.zeros_like(ds_scr)

        for h in range(2):
            sl = slice(h * HEAD, (h + 1) * HEAD)
            args = (st_ref[0, 0, h], r_ref[:, sl], lw_ref[:, sl], k_ref[:, sl], v_ref[:, sl], kk_ref[:, sl], a_ref[:, sl])
            _, vjp = jax.vjp(_wkv_chunk, *args)
            ds0, dr, dlw, dk, dv, dkk, da = vjp((dy_ref[:, sl].astype(F32), ds_scr[h]))
            ds_scr[h] = ds0
            dr_ref[:, sl] = dr
            dlw_ref[:, sl] = dlw
            dk_ref[:, sl] = dk
            dv_ref[:, sl] = dv
            dkk_ref[:, sl] = dkk
            da_ref[:, sl] = da

    return pl.pallas_call(
        body, name="wkv_bwd", grid=(nb, N_PAIRS, nc),
        in_specs=[rows(0), rows(2 * N_PAIRS), rows(0), rows(0), rows(0), rows(0), st, rows(0)],
        out_specs=[rows(0)] * 6,
        out_shape=[jax.ShapeDtypeStruct((t, D), F32)] * 6,
        scratch_shapes=[pltpu.VMEM((2, HEAD, HEAD), F32)],
        compiler_params=_cp("parallel", "parallel", "arbitrary"),
    )(z_rkv, z_rkv, lw, k2, kk, a, states, dy)


def _softmax(s):
    e = jnp.exp(s - jnp.max(s, axis=-1, keepdims=True))
    return e / jnp.sum(e, axis=-1, keepdims=True)


def _attn_chunk(q, kb, vb, bias, valid):
    s = _dot_nt(q, kb) * (HEAD ** -0.5) + bias
    s = jnp.where(valid, s, MASK_VALUE)
    return _dot(_softmax(s), vb)


def _band_valid(c):
    return (c * CHUNK - LEFT + lax.broadcasted_iota(jnp.int32, (1, BAND), 1)) >= 0


def _attn_fwd(proj, bias, seq):
    t = proj.shape[0]
    nb, nc = t // seq, seq // CHUNK
    cq = C_Q // LANE

    def body(q_ref, k_ref, v_ref, b_ref, o_ref, kpad, vpad):
        c = pl.program_id(2)

        @pl.when(c == 0)
        def _():
            kpad[0:LEFT, :] = jnp.zeros((LEFT, LANE), F32)
            vpad[0:LEFT, :] = jnp.zeros((LEFT, LANE), F32)
            kpad[LEFT:, :] = k_ref[...]
            vpad[LEFT:, :] = v_ref[...]

        start = pl.multiple_of(c * CHUNK, CHUNK)
        valid = _band_valid(c)
        for h in range(2):
            sl = slice(h * HEAD, (h + 1) * HEAD)
            o_ref[:, sl] = _attn_chunk(q_ref[:, sl], kpad[pl.ds(start, BAND), sl], vpad[pl.ds(start, BAND), sl],
                                       b_ref[h], valid).astype(o_ref.dtype)

    return pl.pallas_call(
        body, name="attn_fwd", grid=(N_PAIRS, nb, nc),
        in_specs=[pl.BlockSpec((CHUNK, LANE), lambda h, b, c: (b * nc + c, cq + h)),
                  pl.BlockSpec((seq, LANE), lambda h, b, c: (b, cq + N_PAIRS + h)),
                  pl.BlockSpec((seq, LANE), lambda h, b, c: (b, cq + 2 * N_PAIRS + h)),
                  pl.BlockSpec((2, CHUNK, BAND), lambda h, b, c: (h, 0, 0))],
        out_specs=pl.BlockSpec((CHUNK, LANE), lambda h, b, c: (b * nc + c, h)),
        out_shape=jax.ShapeDtypeStruct((t, D), BF16),
        scratch_shapes=[pltpu.VMEM((seq + LEFT, LANE), F32), pltpu.VMEM((seq + LEFT, LANE), F32)],
        compiler_params=_cp("parallel", "arbitrary", "arbitrary"),
    )(proj, proj, proj, bias)


def _attn_bwd(proj, bias, do, seq):
    t = proj.shape[0]
    nb, nc = t // seq, seq // CHUNK
    cq = C_Q // LANE

    def body(q_ref, k_ref, v_ref, b_ref, do_ref, dq_ref, dk_ref, dv_ref, db_ref, kpad, vpad, dkpad, dvpad):
        b, c = pl.program_id(1), pl.program_id(2)

        @pl.when(c == 0)
        def _():
            kpad[0:LEFT, :] = jnp.zeros((LEFT, LANE), F32)
            vpad[0:LEFT, :] = jnp.zeros((LEFT, LANE), F32)
            kpad[LEFT:, :] = k_ref[...]
            vpad[LEFT:, :] = v_ref[...]
            dkpad[...] = jnp.zeros_like(dkpad)
            dvpad[...] = jnp.zeros_like(dvpad)

        @pl.when(jnp.logical_and(b == 0, c == 0))
        def _():
            db_ref[...] = jnp.zeros_like(db_ref)

        start = pl.multiple_of(c * CHUNK, CHUNK)
        valid = _band_valid(c)
        for h in range(2):
            sl = slice(h * HEAD, (h + 1) * HEAD)
            _, vjp = jax.vjp(functools.partial(_attn_chunk, valid=valid),
                             q_ref[:, sl], kpad[pl.ds(start, BAND), sl], vpad[pl.ds(start, BAND), sl], b_ref[h])
            dq, dkb, dvb, dbias = vjp(do_ref[:, sl].astype(F32))
            dq_ref[:, sl] = dq.astype(dq_ref.dtype)
            dkpad[pl.ds(start, BAND), sl] += dkb
            dvpad[pl.ds(start, BAND), sl] += dvb
            db_ref[h] += dbias

        @pl.when(c == nc - 1)
        def _():
            dk_ref[...] = dkpad[LEFT:, :].astype(dk_ref.dtype)
            dv_ref[...] = dvpad[LEFT:, :].astype(dv_ref.dtype)

    kv_out = pl.BlockSpec((seq, LANE), lambda h, b, c: (b, h))
    return pl.pallas_call(
        body, name="attn_bwd", grid=(N_PAIRS, nb, nc),
        in_specs=[pl.BlockSpec((CHUNK, LANE), lambda h, b, c: (b * nc + c, cq + h)),
                  pl.BlockSpec((seq, LANE), lambda h, b, c: (b, cq + N_PAIRS + h)),
                  pl.BlockSpec((seq, LANE), lambda h, b, c: (b, cq + 2 * N_PAIRS + h)),
                  pl.BlockSpec((2, CHUNK, BAND), lambda h, b, c: (h, 0, 0)),
                  pl.BlockSpec((CHUNK, LANE), lambda h, b, c: (b * nc + c, h))],
        out_specs=[pl.BlockSpec((CHUNK, LANE), lambda h, b, c: (b * nc + c, h)), kv_out, kv_out,
                   pl.BlockSpec((2, CHUNK, BAND), lambda h, b, c: (h, 0, 0))],
        out_shape=[jax.ShapeDtypeStruct((t, D), BF16)] * 3 + [jax.ShapeDtypeStruct((N_HEADS, CHUNK, BAND), F32)],
        scratch_shapes=[pltpu.VMEM((seq + LEFT, LANE), F32)] * 4,
        compiler_params=_cp("parallel", "arbitrary", "arbitrary"),
    )(proj, proj, proj, bias, do)


def _xattn_tile(q, k, v):
    s = _dot_nt(q, k) * ((MEM_WIDTH // MEM_HEADS) ** -0.5)
    return _dot(_softmax(s), v)


def _xattn_fwd(qm, kvm, seq, n_mem, tq=512):
    t = qm.shape[0]
    tq = min(tq, seq)
    nb, nq = t // seq, seq // tq

    def body(q_ref, k_ref, v_ref, o_ref):
        o_ref[...] = _xattn_tile(q_ref[...], k_ref[...], v_ref[...]).astype(o_ref.dtype)

    return pl.pallas_call(
        body, name="xattn_fwd", grid=(nb, MEM_HEADS, nq),
        in_specs=[pl.BlockSpec((tq, LANE), lambda b, h, i: (b * nq + i, h)),
                  pl.BlockSpec((n_mem, LANE), lambda b, h, i: (b, h)),
                  pl.BlockSpec((n_mem, LANE), lambda b, h, i: (b, MEM_HEADS + h))],
        out_specs=pl.BlockSpec((tq, LANE), lambda b, h, i: (b * nq + i, h)),
        out_shape=jax.ShapeDtypeStruct((t, MEM_WIDTH), BF16),
        compiler_params=_cp("parallel", "parallel", "parallel"),
    )(qm, kvm, kvm)


def _xattn_bwd(qm, kvm, do, seq, n_mem, tq=512):
    t = qm.shape[0]
    tq = min(tq, seq)
    nb, nq = t // seq, seq // tq

    def body(q_ref, k_ref, v_ref, do_ref, dq_ref, dkv_ref, dk_acc, dv_acc):
        i = pl.program_id(2)

        @pl.when(i == 0)
        def _():
            dk_acc[...] = jnp.zeros_like(dk_acc)
            dv_acc[...] = jnp.zeros_like(dv_acc)

        _, vjp = jax.vjp(_xattn_tile, q_ref[...], k_ref[...], v_ref[...])
        dq, dk, dv = vjp(do_ref[...].astype(F32))
        dq_ref[...] = dq.astype(dq_ref.dtype)
        dk_acc[...] += dk
        dv_acc[...] += dv

        @pl.when(i == nq - 1)
        def _():
            dkv_ref[0] = dk_acc[...].astype(dkv_ref.dtype)
            dkv_ref[1] = dv_acc[...].astype(dkv_ref.dtype)

    dq, dkv = pl.pallas_call(
        body, name="xattn_bwd", grid=(nb, MEM_HEADS, nq),
        in_specs=[pl.BlockSpec((tq, LANE), lambda b, h, i: (b * nq + i, h)),
                  pl.BlockSpec((n_mem, LANE), lambda b, h, i: (b, h)),
                  pl.BlockSpec((n_mem, LANE), lambda b, h, i: (b, MEM_HEADS + h)),
                  pl.BlockSpec((tq, LANE), lambda b, h, i: (b * nq + i, h))],
        out_specs=[pl.BlockSpec((tq, LANE), lambda b, h, i: (b * nq + i, h)),
                   pl.BlockSpec((2, n_mem, LANE), lambda b, h, i: (0, b, h))],
        out_shape=[jax.ShapeDtypeStruct((t, MEM_WIDTH), BF16), jax.ShapeDtypeStruct((2, nb * n_mem, MEM_WIDTH), BF16)],
        scratch_shapes=[pltpu.VMEM((n_mem, LANE), F32)] * 2,
        compiler_params=_cp("parallel", "parallel", "arbitrary"),
    )(qm, kvm, kvm, do)
    return dq, jnp.concatenate([dkv[0], dkv[1]], axis=1)


def _loss_head(y, target, tm=512):
    t, d = y.shape
    tm = min(tm, t)

    def body(y_ref, t_ref, dy_ref, l_ref):
        @pl.when(pl.program_id(0) == 0)
        def _():
            l_ref[...] = jnp.zeros_like(l_ref)

        diff = y_ref[...] - t_ref[...]
        dy_ref[...] = diff * (1.0 / d)
        l_ref[...] += 0.5 * jnp.sum(jnp.mean(diff * diff, axis=-1, keepdims=True), axis=0, keepdims=True)

    dy, loss = pl.pallas_call(
        body, name="loss_head", grid=(t // tm,),
        in_specs=[pl.BlockSpec((tm, d), lambda i: (i, 0))] * 2,
        out_specs=[pl.BlockSpec((tm, d), lambda i: (i, 0)), pl.BlockSpec((8, LANE), lambda i: (0, 0))],
        out_shape=[jax.ShapeDtypeStruct((t, d), F32), jax.ShapeDtypeStruct((8, LANE), F32)],
        compiler_params=_cp("arbitrary"),
    )(y, target)
    return dy, loss


def _mesh_pos():
    return lax.axis_index("x"), lax.axis_index("y"), lax.axis_index("c")


def _peer(pos, d):
    x, y, c = pos
    return ((1 - x) if d & 4 else x, (1 - y) if d & 2 else y, (1 - c) if d & 1 else c)


def _flat(pos):
    return 4 * pos[0] + 2 * pos[1] + pos[2]


def _exchange(arrays, scatter, *, name):
    n = len(arrays)
    shapes = [a.shape[1:] if scatter else a.shape for a in arrays]

    def body(*refs):
        ins, outs = refs[:n], refs[n:2 * n]
        send, recv, loc = refs[2 * n:]
        pos = _mesh_pos()
        me = _flat(pos)
        pending = []
        for i in range(n):
            own = pltpu.make_async_copy(ins[i].at[me] if scatter else ins[i], outs[i].at[me], loc.at[i])
            own.start()
            pending.append(own)
            for d in range(1, N_DEV):
                peer = _peer(pos, d)
                src = ins[i].at[_flat(peer)] if scatter else ins[i]
                out_cp = pltpu.make_async_remote_copy(
                    src_ref=src, dst_ref=outs[i].at[me], send_sem=send.at[i, d - 1], recv_sem=recv.at[i, d - 1],
                    device_id=peer, device_id_type=pl.DeviceIdType.MESH)
                out_cp.start()
                pending.append(out_cp)
        for i in range(n):
            own = pending[i * N_DEV]
            for d in range(1, N_DEV):
                peer = _peer(pos, d)
                src = ins[i].at[_flat(peer)] if scatter else ins[i]
                pending[i * N_DEV + d].wait_send()
                pltpu.make_async_remote_copy(
                    src_ref=src, dst_ref=outs[i].at[_flat(peer)], send_sem=send.at[i, d - 1], recv_sem=recv.at[i, d - 1],
                    device_id=peer, device_id_type=pl.DeviceIdType.MESH).wait_recv()
            own.wait()

    hbm = pl.BlockSpec(memory_space=pltpu.HBM)
    return pl.pallas_call(
        body, name=name,
        in_specs=[hbm] * n, out_specs=[hbm] * n,
        out_shape=[jax.ShapeDtypeStruct((N_DEV,) + tuple(s), a.dtype) for s, a in zip(shapes, arrays)],
        scratch_shapes=[pltpu.SemaphoreType.DMA((n, N_DEV - 1)), pltpu.SemaphoreType.DMA((n, N_DEV - 1)),
                        pltpu.SemaphoreType.DMA((n,))],
    )(*arrays)


def _adamw(parts, w, m, v, *, name, tr=128):
    r, c = w.shape
    tr = max(d for d in range(8, min(tr, r) + 1, 8) if r % d == 0)

    def body(p_ref, w_ref, m_ref, v_ref, g_ref, d_ref, nm_ref, nv_ref):
        g = p_ref[0]
        for j in range(1, N_DEV):
            g = g + p_ref[j]
        m2 = ADAM_B1 * m_ref[...] + (1.0 - ADAM_B1) * g
        v2 = ADAM_B2 * v_ref[...] + (1.0 - ADAM_B2) * (g * g)
        m_hat = m2 / (1.0 - ADAM_B1 ** ADAM_STEP)
        v_hat = v2 / (1.0 - ADAM_B2 ** ADAM_STEP)
        g_ref[...] = g
        d_ref[...] = -ADAM_LR * (m_hat / (jnp.sqrt(v_hat) + ADAM_EPS) + ADAM_WD * w_ref[...])
        nm_ref[...] = m2
        nv_ref[...] = v2

    spec = pl.BlockSpec((tr, c), lambda i: (i, 0))
    return pl.pallas_call(
        body, name=name, grid=(r // tr,),
        in_specs=[pl.BlockSpec((N_DEV, tr, c), lambda i: (0, i, 0)), spec, spec, spec],
        out_specs=[spec] * 4, out_shape=[jax.ShapeDtypeStruct((r, c), F32)] * 4,
        compiler_params=_cp("parallel"),
    )(parts, w, m, v)


def _cols_to_full(g):
    return jnp.transpose(g, (1, 0, 2)).reshape(g.shape[1], N_DEV * g.shape[2])


def _full_to_cols(w):
    r, c = w.shape
    return jnp.transpose(w.reshape(r, N_DEV, c // N_DEV), (1, 0, 2))


def _pad_cols(a, width):
    return jnp.pad(a, ((0, 0), (0, width - a.shape[1])))


def _pad_lora(w):
    return jnp.concatenate([
        _pad_cols(w[:, :LORA_W], 128), _pad_cols(w[:, LORA_W:LORA_W + LORA_A], 128),
        _pad_cols(w[:, LORA_W + LORA_A:], 256)], axis=1)


def _unpad_lora(wp):
    return jnp.concatenate([wp[:, :LORA_W], wp[:, 128:128 + LORA_A], wp[:, 256:256 + LORA_G]], axis=1)


def _permute_in(w):
    rk = 3 * D
    lo = rk + LORA_W + LORA_A + LORA_G
    return jnp.concatenate([w[:, :rk], w[:, lo:], _pad_lora(w[:, rk:lo])], axis=1)


def _unpermute_in(wp):
    return jnp.concatenate([wp[:, :3 * D], _unpad_lora(wp[:, C_LORA:]), wp[:, 3 * D:C_LORA]], axis=1)


def _rel_index():
    dist = jnp.arange(CHUNK)[:, None] - jnp.arange(BAND)[None, :] + LEFT
    return (jnp.minimum(dist, REL_CLIP) + (CHUNK - 1)).reshape(-1)


def _local_step(x, mem, target, wt, seq, n_mem):
    t = x.shape[0]
    row = lambda a: a.reshape(1, -1).astype(F32)
    g_pre_mix, g_post_mix = row(wt["g_pre_mix"]), row(wt["g_post_mix"])
    g_pre_cross, g_post_cross, g_mem = row(wt["g_pre_cross"]), row(wt["g_post_cross"]), row(wt["g_mem"])
    g_pre_ffn, g_post_ffn = row(wt["g_pre_ffn"]), row(wt["g_post_ffn"])
    w_in = wt["w_in_p"]
    mix = row(wt["shift_mix"])
    mix_rkv, mix_lora = mix[:, :3 * D], _pad_lora(mix[:, 3 * D:])
    d_up = jnp.pad(wt["decay_up"].astype(F32), ((0, 128 - LORA_W), (0, 0)))
    i_up = jnp.pad(wt["iclr_up"].astype(F32), ((0, 128 - LORA_A), (0, 0)))
    g_up = jnp.pad(wt["gate_up"].astype(F32), ((0, 256 - LORA_G), (0, 0)))
    decay_base, iclr_base = row(wt["decay_base"]), row(wt["iclr_base"])
    kns, kis = row(wt["key_norm_scale"]), row(wt["key_iclr_scale"])
    lnx_w, lnx_b, bonus = row(wt["lnx_w"]), row(wt["lnx_b"]), row(wt["bonus_scale"])
    e_dh = (jnp.arange(D)[:, None] // HEAD == jnp.arange(N_HEADS)[None, :]).astype(F32)
    e_hd = e_dh.T
    onehot = (jnp.arange(REL_TABLE)[:, None] == _rel_index()[None, :]).astype(BF16)

    (h1,) = _rowwise(_fn_pre, [_win(x)], [g_pre_mix], [(D, BF16)], name="pre_mix", tm=512)
    proj = _mm(h1, w_in, name="mm_in")
    z_rkv = _shift_fwd(proj, 0, 3 * D, mix_rkv, seq, name="shift_rkv")
    z_lora = _shift_fwd(proj, C_LORA, 512, mix_lora, seq, name="shift_lora")
    prep_rows = [_win(z_rkv, D, D), _win(z_lora, 0, 128), _win(z_lora, 128, 128), _win(z_lora, 256, 256)]
    prep_params = [decay_base, d_up, iclr_base, i_up, g_up, kns, kis, e_hd, e_dh]
    lw, k2, kk, a, g = _rowwise(_fn_prep, prep_rows, prep_params, [(D, F32)] * 5, name="rwkv_prep", tm=256)
    y, states = _wkv_fwd(z_rkv, lw, k2, kk, a, seq)
    post_rows = [_win(y), _win(z_rkv, 0, D), _win(k2), _win(z_rkv, 2 * D, D), _win(g)]
    post_params = [lnx_w, lnx_b, bonus, e_hd, e_dh]
    (y_a,) = _rowwise(_fn_post, post_rows, post_params, [(D, BF16)], name="rwkv_post", tm=256)
    bias = _mm(wt["rel_bias"].astype(F32), onehot, name="mm_bias", split_a=3, tm=N_HEADS, tk=REL_TABLE).reshape(N_HEADS, CHUNK, BAND)
    y_b = _attn_fwd(proj, bias, seq)
    ya_p = _mm(y_a, wt["w_branch_a"], name="mm_a")
    yb_p = _mm(y_b, wt["w_branch_b"], name="mm_b")
    mix_rows = [_win(proj, C_GA, D), _win(proj, C_GA + D, D), _win(ya_p), _win(yb_p)]
    (mixed,) = _rowwise(_fn_mix, mix_rows, [], [(D, BF16)], name="gate_mix", tm=512)
    mo = _mm(mixed, wt["w_out"], name="mm_out")
    x1, h2 = _rowwise(_fn_res_pre, [_win(x), _win(mo)], [g_post_mix, g_pre_cross], [(D, F32), (D, BF16)],
                      name="res_mix", tm=512)
    qm = _mm(h2, wt["w_q_mem"], name="mm_q")
    (mn,) = _rowwise(_fn_pre, [_win(mem)], [g_mem], [(D, BF16)], name="pre_mem", tm=512)
    kvm = _mm(mn, wt["w_kv_mem"], name="mm_kv")
    om = _xattn_fwd(qm, kvm, seq, n_mem)
    co = _mm(om, wt["w_o_mem"], name="mm_o")
    x2, h3 = _rowwise(_fn_res_pre, [_win(x1), _win(co)], [g_post_cross, g_pre_ffn], [(D, F32), (D, BF16)],
                      name="res_cross", tm=512)
    gu = _mm(h3, wt["w_ffn_in"], name="mm_ffn_in")
    (act,) = _rowwise(_fn_swiglu, [_win(gu, 0, FFN), _win(gu, FFN, FFN)], [], [(FFN, BF16)], name="swiglu", tm=256)
    ff = _mm(act, wt["w_ffn_out"], name="mm_ffn_out", tk=256)
    (x3,) = _rowwise(_fn_res, [_win(x2), _win(ff)], [g_post_ffn], [(D, F32)], name="res_ffn", tm=512)
    dx3, loss = _loss_head(x3, target)

    gw = {}
    (dx2, dff), (gw["g_post_ffn"],) = _rowwise_bwd(
        _fn_res, [_win(x2), _win(ff)], [g_post_ffn], 0, [[dx3]], name="res_ffn_bwd", tm=256, row_grad=[F32, BF16])
    dact = _mm(dff, wt["w_ffn_out"], tb=True, name="mm_ffn_out_dx", out_dtype=BF16, tn=256)
    gw["w_ffn_out"] = _mm(act, dff, ta=True, name="mm_ffn_out_dw", tm=256)
    (dgate, dup), _ = _rowwise_bwd(_fn_swiglu, [_win(gu, 0, FFN), _win(gu, FFN, FFN)], [], 0, [[dact]],
                                   name="swiglu_bwd", tm=256, row_grad=[BF16, BF16])
    dgu = jnp.concatenate([dgate, dup], axis=1)
    dh3 = _mm(dgu, wt["w_ffn_in"], tb=True, name="mm_ffn_in_dx", out_dtype=BF16)
    gw["w_ffn_in"] = _mm(h3, dgu, ta=True, name="mm_ffn_in_dw")
    (dx1, dco), (gw["g_post_cross"], gw["g_pre_ffn"]) = _rowwise_bwd(
        _fn_res_pre, [_win(x1), _win(co)], [g_post_cross, g_pre_ffn], 0, [[dx2], [dh3]],
        name="res_cross_bwd", tm=256, row_grad=[F32, BF16])
    dom = _mm(dco, wt["w_o_mem"], tb=True, name="mm_o_dx", out_dtype=BF16)
    gw["w_o_mem"] = _mm(om, dco, ta=True, name="mm_o_dw")
    dqm, dkvm = _xattn_bwd(qm, kvm, dom, seq, n_mem)
    dh2 = _mm(dqm, wt["w_q_mem"], tb=True, name="mm_q_dx", out_dtype=BF16)
    gw["w_q_mem"] = _mm(h2, dqm, ta=True, name="mm_q_dw")
    dmn = _mm(dkvm, wt["w_kv_mem"], tb=True, name="mm_kv_dx", out_dtype=BF16)
    gw["w_kv_mem"] = _mm(mn, dkvm, ta=True, name="mm_kv_dw")
    _, (gw["g_mem"],) = _rowwise_bwd(_fn_pre, [_win(mem)], [g_mem], 0, [[dmn]], name="pre_mem_bwd", tm=256,
                                     row_grad=[None])
    (dx0, dmo), (gw["g_post_mix"], gw["g_pre_cross"]) = _rowwise_bwd(
        _fn_res_pre, [_win(x), _win(mo)], [g_post_mix, g_pre_cross], 0, [[dx1], [dh2]],
        name="res_mix_bwd", tm=256, row_grad=[F32, BF16])
    dmixed = _mm(dmo, wt["w_out"], tb=True, name="mm_out_dx", out_dtype=BF16)
    gw["w_out"] = _mm(mixed, dmo, ta=True, name="mm_out_dw")
    (dzga, dzgb, dya_p, dyb_p), _ = _rowwise_bwd(_fn_mix, mix_rows, [], 0, [[dmixed]], name="gate_mix_bwd", tm=256,
                                                 row_grad=[BF16] * 4)
    dy_a = _mm(dya_p, wt["w_branch_a"], tb=True, name="mm_a_dx", out_dtype=BF16)
    gw["w_branch_a"] = _mm(y_a, dya_p, ta=True, name="mm_a_dw")
    dy_b = _mm(dyb_p, wt["w_branch_b"], tb=True, name="mm_b_dx", out_dtype=BF16)
    gw["w_branch_b"] = _mm(y_b, dyb_p, ta=True, name="mm_b_dw")
    dq, dk, dv, dbias = _attn_bwd(proj, bias, dy_b, seq)
    gw["rel_bias"] = _mm(dbias.reshape(N_HEADS, CHUNK * BAND), onehot, tb=True, name="mm_bias_dw", split_a=2,
                         tm=N_HEADS, tn=REL_TABLE, tk=512)
    (dy, dr_p, dk2_p, dv_p, dg), (gw["lnx_w"], gw["lnx_b"], gw["bonus_scale"]) = _rowwise_bwd(
        _fn_post, post_rows, post_params, 2, [[dy_a]], name="rwkv_post_bwd", tm=128, row_grad=[F32] * 5)
    dr_s, dlw, dk2_s, dv_s, dkk, da = _wkv_bwd(z_rkv, lw, k2, kk, a, states, dy, seq)
    (dzk, dzw, dza, dzg), pg = _rowwise_bwd(
        _fn_prep, prep_rows, prep_params, 2, [[dlw], [dk2_p, dk2_s], [dkk], [da], [dg]],
        name="rwkv_prep_bwd", tm=128, row_grad=[F32] * 4)
    gw["decay_base"], gd_up, gw["iclr_base"], gi_up, gg_up, gw["key_norm_scale"], gw["key_iclr_scale"] = pg
    gw["decay_up"], gw["iclr_up"], gw["gate_up"] = gd_up[:LORA_W], gi_up[:LORA_A], gg_up[:LORA_G]
    dp_r, gmix_r = _shift_bwd(proj, 0, D, mix_rkv[:, :D], [dr_p, dr_s], seq, name="shift_r_bwd")
    dp_k, gmix_k = _shift_bwd(proj, D, D, mix_rkv[:, D:2 * D], [dzk], seq, name="shift_k_bwd")
    dp_v, gmix_v = _shift_bwd(proj, 2 * D, D, mix_rkv[:, 2 * D:], [dv_p, dv_s], seq, name="shift_v_bwd")
    dp_lora, gmix_lora = _shift_bwd(proj, C_LORA, 512, mix_lora, [jnp.concatenate([dzw, dza, dzg], axis=1)], seq,
                                    name="shift_lora_bwd")
    gw["shift_mix"] = jnp.concatenate([gmix_r, gmix_k, gmix_v, _unpad_lora(gmix_lora)], axis=1)
    dproj = jnp.concatenate([dp_r, dp_k, dp_v, dq, dk, dv, dzga, dzgb, dp_lora], axis=1)
    dh1 = _mm(dproj, w_in, tb=True, name="mm_in_dx", out_dtype=BF16)
    gw["w_in_p"] = _mm(h1, dproj, ta=True, name="mm_in_dw")
    (grad_x,), (gw["g_pre_mix"],) = _rowwise_bwd(_fn_pre, [_win(x)], [g_pre_mix], 0, [[dh1]], name="pre_mix_bwd",
                                                 tm=256, row_grad=[F32], add_to={0: dx0})
    return loss, grad_x, gw


_COL_SHARDED = ("w_in", "decay_up", "iclr_up", "gate_up", "w_o_mem", "w_ffn_in")
_ROW_SHARDED = ("w_branch_a", "w_branch_b", "w_out", "w_q_mem", "w_kv_mem", "w_ffn_out")
_REPLICATED = ("g_pre_mix", "g_post_mix", "shift_mix", "decay_base", "iclr_base", "key_norm_scale", "key_iclr_scale",
               "bonus_scale", "lnx_w", "lnx_b", "rel_bias", "g_pre_cross", "g_post_cross", "g_mem", "g_pre_ffn",
               "g_post_ffn")
_WEIGHTS = ("g_pre_mix", "g_post_mix", "w_in", "shift_mix", "decay_base", "decay_up", "iclr_base", "iclr_up", "gate_up",
            "key_norm_scale", "key_iclr_scale", "bonus_scale", "lnx_w", "lnx_b", "rel_bias", "w_branch_a", "w_branch_b",
            "w_out", "g_pre_cross", "g_post_cross", "g_mem", "w_q_mem", "w_kv_mem", "w_o_mem", "g_pre_ffn", "g_post_ffn",
            "w_ffn_in", "w_ffn_out")
_PACK_ROWS = 8 * ((sum({"shift_mix": 3360, "bonus_scale": 1024, "rel_bias": 3072}.get(n, D) for n in _REPLICATED)
                   + 1 + 8 * LANE - 1) // (8 * LANE))


def _pack(vals):
    flat = jnp.concatenate([v.reshape(-1).astype(F32) for v in vals])
    return jnp.pad(flat, (0, _PACK_ROWS * LANE - flat.shape[0])).reshape(_PACK_ROWS, LANE)


def _unpack(packed, shapes):
    flat, out, pos = packed.reshape(-1), [], 0
    for s in shapes:
        n = math.prod(s)
        out.append(flat[pos:pos + n].reshape(s))
        pos += n
    return out


def _step(args, seq, n_mem):
    names = ("x", "mem") + _WEIGHTS + ("loss_target",) + tuple("m_" + n for n in _WEIGHTS) + tuple("v_" + n for n in _WEIGHTS)
    given = dict(zip(names, args))
    nb = given["x"].shape[0]
    x = given["x"].reshape(nb * seq, D)
    mem = given["mem"].reshape(nb * n_mem, D)
    target = given["loss_target"].reshape(nb * seq, D)
    shard = {n: given[n][0] for n in _COL_SHARDED + _ROW_SHARDED}

    order = _COL_SHARDED + _ROW_SHARDED
    gathered = dict(zip(order, _exchange([shard[n].astype(BF16) for n in order], False, name="gather_weights")))
    wt = {n: given[n][0] for n in _REPLICATED}
    for n in _COL_SHARDED:
        wt[n] = _cols_to_full(gathered[n])
    for n in _ROW_SHARDED:
        wt[n] = gathered[n].reshape(-1, gathered[n].shape[-1])
    wt["w_in_p"] = _permute_in(wt.pop("w_in"))

    loss_tile, grad_x, gw = _local_step(x, mem, target, wt, seq, n_mem)
    gw["w_in"] = _unpermute_in(gw.pop("w_in_p"))

    blocks = [(_full_to_cols(gw[n]) if n in _COL_SHARDED else gw[n].reshape((N_DEV,) + shard[n].shape)) for n in order]
    landed = dict(zip(order, _exchange(blocks, True, name="scatter_grads")))
    rep_shapes = [given[n].shape for n in _REPLICATED]
    small = _exchange([_pack([gw[n] for n in _REPLICATED] + [loss_tile[0, 0]])], False, name="gather_small")[0]

    out = {}
    for n in order:
        res = _adamw(landed[n], shard[n], given["m_" + n][0], given["v_" + n][0], name="adamw_" + n)
        for kind, r in zip(("grad_", "delta_", "new_m_", "new_v_"), res):
            out[kind + n] = r[None]
    zero = jnp.zeros((), F32)
    res = _adamw(small, *[_pack([given[p + n] for n in _REPLICATED] + [zero]) for p in ("", "m_", "v_")],
                 name="adamw_small", tr=_PACK_ROWS)
    for kind, r in zip(("grad_", "delta_", "new_m_", "new_v_"), res):
        for n, val in zip(_REPLICATED, _unpack(r, rep_shapes)):
            out[kind + n] = val
    loss = res[0].reshape(-1)[sum(math.prod(s) for s in rep_shapes)]
    grad_x = grad_x.reshape(nb, seq, D)
    return (loss, grad_x, *[out[k + n] for k in ("grad_", "delta_", "new_m_", "new_v_") for n in _WEIGHTS])


def kernel(x, mem, g_pre_mix, g_post_mix, w_in, shift_mix, decay_base, decay_up, iclr_base, iclr_up, gate_up, key_norm_scale, key_iclr_scale, bonus_scale, lnx_w, lnx_b, rel_bias, w_branch_a, w_branch_b, w_out, g_pre_cross, g_post_cross, g_mem, w_q_mem, w_kv_mem, w_o_mem, g_pre_ffn, g_post_ffn, w_ffn_in, w_ffn_out, loss_target, m_g_pre_mix, m_g_post_mix, m_w_in, m_shift_mix, m_decay_base, m_decay_up, m_iclr_base, m_iclr_up, m_gate_up, m_key_norm_scale, m_key_iclr_scale, m_bonus_scale, m_lnx_w, m_lnx_b, m_rel_bias, m_w_branch_a, m_w_branch_b, m_w_out, m_g_pre_cross, m_g_post_cross, m_g_mem, m_w_q_mem, m_w_kv_mem, m_w_o_mem, m_g_pre_ffn, m_g_post_ffn, m_w_ffn_in, m_w_ffn_out, v_g_pre_mix, v_g_post_mix, v_w_in, v_shift_mix, v_decay_base, v_decay_up, v_iclr_base, v_iclr_up, v_gate_up, v_key_norm_scale, v_key_iclr_scale, v_bonus_scale, v_lnx_w, v_lnx_b, v_rel_bias, v_w_branch_a, v_w_branch_b, v_w_out, v_g_pre_cross, v_g_post_cross, v_g_mem, v_w_q_mem, v_w_kv_mem, v_w_o_mem, v_g_pre_ffn, v_g_post_ffn, v_w_ffn_in, v_w_ffn_out):
    args = (x, mem, g_pre_mix, g_post_mix, w_in, shift_mix, decay_base, decay_up, iclr_base, iclr_up, gate_up, key_norm_scale, key_iclr_scale, bonus_scale, lnx_w, lnx_b, rel_bias, w_branch_a, w_branch_b, w_out, g_pre_cross, g_post_cross, g_mem, w_q_mem, w_kv_mem, w_o_mem, g_pre_ffn, g_post_ffn, w_ffn_in, w_ffn_out, loss_target, m_g_pre_mix, m_g_post_mix, m_w_in, m_shift_mix, m_decay_base, m_decay_up, m_iclr_base, m_iclr_up, m_gate_up, m_key_norm_scale, m_key_iclr_scale, m_bonus_scale, m_lnx_w, m_lnx_b, m_rel_bias, m_w_branch_a, m_w_branch_b, m_w_out, m_g_pre_cross, m_g_post_cross, m_g_mem, m_w_q_mem, m_w_kv_mem, m_w_o_mem, m_g_pre_ffn, m_g_post_ffn, m_w_ffn_in, m_w_ffn_out, v_g_pre_mix, v_g_post_mix, v_w_in, v_shift_mix, v_decay_base, v_decay_up, v_iclr_base, v_iclr_up, v_gate_up, v_key_norm_scale, v_key_iclr_scale, v_bonus_scale, v_lnx_w, v_lnx_b, v_rel_bias, v_w_branch_a, v_w_branch_b, v_w_out, v_g_pre_cross, v_g_post_cross, v_g_mem, v_w_q_mem, v_w_kv_mem, v_w_o_mem, v_g_pre_ffn, v_g_post_ffn, v_w_ffn_in, v_w_ffn_out)
    return _step(args, x.shape[1], mem.shape[1])
```

```python
import functools
import math

import jax
import jax.numpy as jnp
from jax import lax
from jax.experimental import pallas as pl
from jax.experimental.pallas import tpu as pltpu

F32 = jnp.float32
BF16 = jnp.bfloat16

N_DEV = 8
D = 1024
HEAD = 64
N_HEADS = D // HEAD
LANE = 128
N_PAIRS = D // LANE
CHUNK = 64
LEFT = 8 * CHUNK
BAND = LEFT + CHUNK
REL_CLIP = 128
REL_TABLE = CHUNK + REL_CLIP
MEM_WIDTH = D // 2
MEM_HEADS = 4
FFN = 2816
LORA_W, LORA_A, LORA_G = 64, 64, 160
P_WIDTH = 3 * D + 3 * D + 2 * D + 128 + 128 + 256
C_Q, C_GA, C_LORA = 3 * D, 6 * D, 8 * D
NORM_EPS = 1e-6
GROUP_NORM_EPS = 64e-5
MASK_VALUE = -1e30
ADAM_LR, ADAM_B1, ADAM_B2, ADAM_EPS, ADAM_WD, ADAM_STEP = 0.001, 0.9, 0.999, 1e-08, 0.01, 10
VMEM_LIMIT = 56 * 1024 * 1024


def _cp(*sem):
    return pltpu.CompilerParams(dimension_semantics=sem, vmem_limit_bytes=VMEM_LIMIT)


_NN, _NT, _TN = ((1,), (0,)), ((1,), (1,)), ((0,), (0,))


def _dot_raw(a, b, dims):
    return lax.dot_general(a.astype(BF16), b.astype(BF16), (dims, ((), ())), preferred_element_type=F32)


@functools.partial(jax.custom_vjp, nondiff_argnums=(2,))
def _dot_dims(a, b, dims):
    return _dot_raw(a, b, dims)


def _dot_dims_fwd(a, b, dims):
    return _dot_raw(a, b, dims), (a, b)


def _dot_dims_bwd(dims, res, g):
    a, b = res
    if dims == _NN:
        da, db = _dot_raw(g, b, _NT), _dot_raw(a, g, _TN)
    elif dims == _NT:
        da, db = _dot_raw(g, b, _NN), _dot_raw(g, a, _TN)
    else:
        da, db = _dot_raw(b, g, _NT), _dot_raw(a, g, _NN)
    return da.astype(a.dtype), db.astype(b.dtype)


_dot_dims.defvjp(_dot_dims_fwd, _dot_dims_bwd)


def _dot(a, b, dims=_NN):
    return _dot_dims(a, b, dims)


def _dot_nt(a, b):
    return _dot_dims(a, b, _NT)


def _dot_tn(a, b):
    return _dot_dims(a, b, _TN)


def _split(x, terms):
    parts, rest = [], x.astype(F32)
    for _ in range(terms):
        p = rest.astype(BF16)
        parts.append(p)
        rest = rest - p.astype(F32)
    return parts


def _dot_split_a(a, b, terms=2):
    out = None
    for p in _split(a, terms):
        t = _dot(p, b)
        out = t if out is None else out + t
    return out


def _dot_split_b(a, b, terms=3):
    out = None
    for p in _split(b, terms):
        t = _dot(a, p)
        out = t if out is None else out + t
    return out


def _dot_hi(a, b, dims=_NN):
    ah, al = _split(a, 2)
    bh, bl = _split(b, 2)
    return _dot(ah, bh, dims) + (_dot(ah, bl, dims) + _dot(al, bh, dims))


MM_VMEM_BUDGET = 30 * 1024 * 1024
MM_HBM_BPS = 3.2e12
MM_MXU_FPS = 8.5e14
MM_STEP_S = 0.35e-6


def _divisors(n, align, cap):
    out = [d for d in range(align, min(n, cap) + 1, align) if n % d == 0]
    return out or [n]


def _mm_tiles(m, n, k, ea, eb, eo, ta):
    best = None
    for tm in _divisors(m, LANE if ta else 8, 2048):
        for tn in _divisors(n, LANE, 2048):
            for tk in _divisors(k, LANE, 2048):
                nk = k // tk
                vmem = 2 * (tm * tk * ea + tk * tn * eb + tm * tn * eo) + (tm * tn * 4 if nk > 1 else 0)
                if vmem > MM_VMEM_BUDGET:
                    continue
                dma = (tm * tk * ea if (nk > 1 or n // tn == 1) else tm * tk * ea * tn / n) + tk * tn * eb + tm * tn * eo / nk
                step = max(2.0 * tm * tn * tk / MM_MXU_FPS, dma / MM_HBM_BPS) + MM_STEP_S
                cost = (m // tm) * (n // tn) * nk * step
                if best is None or cost < best[0]:
                    best = (cost, tm, tn, tk)
    return best[1:]


def _mm(a, b, *, name, ta=False, tb=False, out_dtype=F32, tm=None, tn=None, tk=None, split_a=1):
    m, k = (a.shape[1], a.shape[0]) if ta else a.shape
    n, kb = (b.shape[0], b.shape[1]) if tb else (b.shape[1], b.shape[0])
    assert k == kb, (a.shape, b.shape, ta, tb)
    if tm is None:
        tm, tn, tk = _mm_tiles(m, n, k, a.dtype.itemsize, b.dtype.itemsize, jnp.dtype(out_dtype).itemsize, ta)
    assert m % tm == 0 and n % tn == 0 and k % tk == 0, (m, n, k, tm, tn, tk)
    nk = k // tk
    dims = ((0 if ta else 1,), (1 if tb else 0,))

    def body(a_ref, b_ref, o_ref, *scratch):
        prod = None
        for p in _split(a_ref[...], split_a) if split_a > 1 else [a_ref[...]]:
            t = _dot_raw(p, b_ref[...], dims)
            prod = t if prod is None else prod + t
        if nk == 1:
            o_ref[...] = prod.astype(o_ref.dtype)
            return
        acc_ref, kk = scratch[0], pl.program_id(2)

        @pl.when(kk == 0)
        def _():
            acc_ref[...] = prod

        @pl.when(kk > 0)
        def _():
            acc_ref[...] += prod

        @pl.when(kk == nk - 1)
        def _():
            o_ref[...] = acc_ref[...].astype(o_ref.dtype)

    a_spec = pl.BlockSpec((tk, tm), lambda i, j, q: (q, i)) if ta else pl.BlockSpec((tm, tk), lambda i, j, q: (i, q))
    b_spec = pl.BlockSpec((tn, tk), lambda i, j, q: (j, q)) if tb else pl.BlockSpec((tk, tn), lambda i, j, q: (q, j))
    return pl.pallas_call(
        body, name=name, grid=(m // tm, n // tn, nk),
        in_specs=[a_spec, b_spec], out_specs=pl.BlockSpec((tm, tn), lambda i, j, q: (i, j)),
        out_shape=jax.ShapeDtypeStruct((m, n), out_dtype),
        scratch_shapes=[pltpu.VMEM((tm, tn), F32)] if nk > 1 else [],
        compiler_params=_cp("parallel", "parallel", "arbitrary"),
    )(a, b)


def _win(arr, start=0, width=None):
    width = arr.shape[1] if width is None else width
    assert start % width == 0
    return (arr, start // width, width)


def _row_specs(rows, tm):
    return [pl.BlockSpec((tm, w), functools.partial(lambda i, cb: (i, cb), cb=cb)) for (_, cb, w) in rows]


def _full_spec(p):
    nd = p.ndim
    return pl.BlockSpec(p.shape, lambda i, nd=nd: (0,) * nd)


def _rowwise(fn, rows, params, outs, *, name, tm):
    t = rows[0][0].shape[0]
    tm = min(tm, t)
    assert t % tm == 0
    nr, npar = len(rows), len(params)

    def body(*refs):
        vals = [r[...] for r in refs[:nr + npar]]
        res = fn(*vals)
        for o_ref, r in zip(refs[nr + npar:], res):
            o_ref[...] = r.astype(o_ref.dtype)

    return pl.pallas_call(
        body, name=name, grid=(t // tm,),
        in_specs=_row_specs(rows, tm) + [_full_spec(p) for p in params],
        out_specs=[pl.BlockSpec((tm, w), lambda i: (i, 0)) for (w, _) in outs],
        out_shape=[jax.ShapeDtypeStruct((t, w), dt) for (w, dt) in outs],
        compiler_params=_cp("parallel"),
    )(*[r[0] for r in rows], *params)


def _rowwise_bwd(fn, rows, params, n_const, cots, *, name, tm, row_grad, add_to=None):
    t = rows[0][0].shape[0]
    tm = min(tm, t)
    assert t % tm == 0
    nr, npar = len(rows), len(params)
    ndp = npar - n_const
    add_to = add_to or {}
    add_idx = sorted(add_to)
    flat_cots = [c for group in cots for c in group]
    kept = [i for i in range(nr) if row_grad[i] is not None]

    def body(*refs):
        pos = 0
        row_v = [r[...] for r in refs[pos:pos + nr]]; pos += nr
        par_v = [r[...] for r in refs[pos:pos + npar]]; pos += npar
        cot_v = [r[...] for r in refs[pos:pos + len(flat_cots)]]; pos += len(flat_cots)
        add_v = [r[...] for r in refs[pos:pos + len(add_idx)]]; pos += len(add_idx)
        rg_refs = refs[pos:pos + len(kept)]; pos += len(kept)
        pg_refs = refs[pos:pos + ndp]

        consts = par_v[ndp:]
        res, vjp = jax.vjp(lambda *args: tuple(fn(*args, *consts)), *row_v, *par_v[:ndp])
        cot_in, q = [], 0
        for j, group in enumerate(cots):
            c = None
            for _ in group:
                cv = cot_v[q].astype(F32); q += 1
                c = cv if c is None else c + cv
            c = jnp.zeros(res[j].shape, F32) if c is None else c
            cot_in.append(c.astype(res[j].dtype))
        grads = vjp(tuple(cot_in))
        for ref, i in zip(rg_refs, kept):
            g = grads[i].astype(F32)
            if i in add_to:
                g = g + add_v[add_idx.index(i)].astype(F32)
            ref[...] = g.astype(ref.dtype)

        @pl.when(pl.program_id(0) == 0)
        def _():
            for ref in pg_refs:
                ref[...] = jnp.zeros_like(ref)

        for ref, g in zip(pg_refs, grads[nr:]):
            ref[...] += g.astype(F32)

    cot_specs = [pl.BlockSpec((tm, c.shape[1]), lambda i: (i, 0)) for c in flat_cots]
    add_specs = [pl.BlockSpec((tm, add_to[i].shape[1]), lambda i_: (i_, 0)) for i in add_idx]
    out_specs = [pl.BlockSpec((tm, rows[i][2]), lambda i_: (i_, 0)) for i in kept] + [_full_spec(p) for p in params[:ndp]]
    out_shape = [jax.ShapeDtypeStruct((t, rows[i][2]), row_grad[i]) for i in kept] + [
        jax.ShapeDtypeStruct(p.shape, F32) for p in params[:ndp]]
    res = pl.pallas_call(
        body, name=name, grid=(t // tm,),
        in_specs=_row_specs(rows, tm) + [_full_spec(p) for p in params] + cot_specs + add_specs,
        out_specs=out_specs, out_shape=out_shape,
        compiler_params=_cp("arbitrary"),
    )(*[r[0] for r in rows], *params, *flat_cots, *[add_to[i] for i in add_idx])
    return list(res[:len(kept)]), list(res[len(kept):])


def _rms(x, g):
    xf = x.astype(F32)
    return xf * lax.rsqrt(jnp.mean(xf * xf, axis=-1, keepdims=True) + NORM_EPS) * g


def _softplus(x):
    return jnp.maximum(x, 0.0) + jnp.log(1.0 + jnp.exp(-jnp.abs(x)))


def _fn_pre(x, g):
    return (_rms(x, g).astype(BF16),)


def _fn_res(x, u, g_post):
    return (x + _rms(u, g_post),)


def _fn_res_pre(x, u, g_post, g_pre):
    xn = x + _rms(u, g_post)
    return xn, _rms(xn, g_pre).astype(BF16)


def _fn_mix(zga, zgb, ya, yb):
    return ((jax.nn.sigmoid(zga) * ya + jax.nn.sigmoid(zgb) * yb).astype(BF16),)


def _fn_swiglu(gate, up):
    return ((gate * jax.nn.sigmoid(gate) * up).astype(BF16),)


def _fn_prep(zk, zw, za, zg, decay_base, d_up, iclr_base, i_up, g_up, kns, kis, e_hd, e_dh):
    w_log = -_softplus(-(decay_base + _dot(jnp.tanh(zw), d_up))) - 0.5
    lw = -jnp.exp(w_log)
    a = jax.nn.sigmoid(iclr_base + _dot(za, i_up))
    g = _dot(jax.nn.sigmoid(zg), g_up)
    kn = zk * kns
    ss = _dot_split_a(kn * kn, e_dh)
    inv = lax.rsqrt(jnp.maximum(ss, 1e-24))
    kk = kn * _dot_split_a(inv, e_hd)
    k2 = zk * (1.0 + (a - 1.0) * kis)
    return lw, k2, kk, a, g


def _fn_post(y, r, k2, v, g, lnx_w, lnx_b, bonus, e_hd, e_dh):
    mu = _dot_split_a(_dot_split_a(y, e_dh) * (1.0 / HEAD), e_hd)
    yc = y - mu
    var = _dot_split_a(yc * yc, e_dh) * (1.0 / HEAD)
    yn = yc * _dot_split_a(lax.rsqrt(var + GROUP_NORM_EPS), e_hd)
    bs = _dot_split_a(_dot_split_a(r * k2 * bonus, e_dh), e_hd)
    return (((yn * lnx_w + lnx_b + bs * v) * g).astype(BF16),)


def _shift_fwd(p, col0, ncols, mix, seq, *, name, cw=256):
    t = p.shape[0]
    assert col0 % cw == 0 and ncols % cw == 0 and t % seq == 0
    cb0 = col0 // cw

    def body(p_ref, m_ref, z_ref):
        pv = p_ref[...]
        row = lax.broadcasted_iota(jnp.int32, pv.shape, 0)
        prev = jnp.where(row == 0, 0.0, pltpu.roll(pv, 1, axis=0))
        z_ref[...] = pv + (prev - pv) * m_ref[...]

    return pl.pallas_call(
        body, name=name, grid=(t // seq, ncols // cw),
        in_specs=[pl.BlockSpec((seq, cw), lambda b, c: (b, c + cb0)), pl.BlockSpec((1, cw), lambda b, c: (0, c))],
        out_specs=pl.BlockSpec((seq, cw), lambda b, c: (b, c)),
        out_shape=jax.ShapeDtypeStruct((t, ncols), F32),
        compiler_params=_cp("parallel", "parallel"),
    )(p, mix)


def _shift_bwd(p, col0, ncols, mix, dz_parts, seq, *, name, cw=256):
    t = p.shape[0]
    cb0 = col0 // cw
    n = len(dz_parts)

    def body(*refs):
        p_ref, m_ref = refs[:2]
        dp_ref, dm_ref = refs[2 + n:]
        dz = refs[2][...].astype(F32)
        for r in refs[3:2 + n]:
            dz = dz + r[...].astype(F32)
        pv = p_ref[...]
        mixv = m_ref[...]
        row = lax.broadcasted_iota(jnp.int32, pv.shape, 0)
        prev = jnp.where(row == 0, 0.0, pltpu.roll(pv, 1, axis=0))
        u = dz * mixv
        nxt = jnp.where(row == seq - 1, 0.0, pltpu.roll(u, seq - 1, axis=0))
        dp_ref[...] = (dz - u + nxt).astype(dp_ref.dtype)

        @pl.when(pl.program_id(1) == 0)
        def _():
            dm_ref[...] = jnp.zeros_like(dm_ref)

        dm_ref[...] += jnp.sum(dz * (prev - pv), axis=0, keepdims=True)

    return pl.pallas_call(
        body, name=name, grid=(ncols // cw, t // seq),
        in_specs=[pl.BlockSpec((seq, cw), lambda c, b: (b, c + cb0)), pl.BlockSpec((1, cw), lambda c, b: (0, c))]
        + [pl.BlockSpec((seq, cw), lambda c, b: (b, c))] * n,
        out_specs=[pl.BlockSpec((seq, cw), lambda c, b: (b, c)), pl.BlockSpec((1, cw), lambda c, b: (0, c))],
        out_shape=[jax.ShapeDtypeStruct((t, ncols), BF16), jax.ShapeDtypeStruct((1, ncols), F32)],
        compiler_params=_cp("parallel", "arbitrary"),
    )(p, mix, *dz_parts)


def _each(f, *lists):
    return [f(*xs) for xs in zip(*lists)]


def _tri_inv(low):
    c = low[0].shape[0]
    eye = (lax.broadcasted_iota(jnp.int32, (c, c), 0) == lax.broadcasted_iota(jnp.int32, (c, c), 1)).astype(F32)
    acc, pw = [eye - m for m in low], low
    for _ in range(int(math.log2(c)) - 1):
        pw = _each(_dot_hi, pw, pw)
        acc = _each(lambda p, q: _dot_hi(p, eye + q), acc, pw)
    return acc


def _wkv_chunk(s0, r, lw, k, v, kk, a):
    c = r[0].shape[0]
    ti = lax.broadcasted_iota(jnp.int32, (c, c), 0)
    si = lax.broadcasted_iota(jnp.int32, (c, c), 1)
    incl, strict = ti >= si, ti > si
    tri = incl.astype(F32)
    cum = _each(lambda x: _dot_split_b(tri, x, 3), lw)
    eg = _each(jnp.exp, cum)
    egp = _each(lambda cs, x: jnp.exp(cs - x), cum, lw)
    ei = _each(lambda cs: jnp.exp(-cs), cum)
    rh, kkh, kt = _each(jnp.multiply, r, eg), _each(jnp.multiply, kk, egp), _each(jnp.multiply, k, ei)
    bt = _each(lambda p, q, e: (p * q) * e, a, kk, ei)
    lb = _each(lambda p, q: jnp.where(strict, _dot_nt(p, q), 0.0), kkh, bt)
    lk = _each(lambda p, q: jnp.where(strict, _dot_nt(p, q), 0.0), kkh, kt)
    mb = _each(lambda p, q: jnp.where(incl, _dot_nt(p, q), 0.0), rh, bt)
    mk = _each(lambda p, q: jnp.where(incl, _dot_nt(p, q), 0.0), rh, kt)
    rhs = _each(lambda p, s, m, x: _dot_nt(p, s) + _dot(m, x), kkh, s0, lk, v)
    u = _each(lambda t, x: -_dot_hi(t, x), _tri_inv(lb), rhs)
    y = _each(lambda p, s, m1, uu, m2, x: _dot_nt(p, s) + _dot(m1, uu) + _dot(m2, x), rh, s0, mb, u, mk, v)
    s1 = _each(lambda s, uu, b, x, kq, w: (s + _dot_tn(uu, b) + _dot_tn(x, kq)) * jnp.exp(jnp.sum(w, axis=0, keepdims=True)),
               s0, u, bt, v, kt, lw)
    return y, s1


WKV_HEADS = 8
WKV_COLS = WKV_HEADS * HEAD
WKV_GROUPS = N_HEADS // WKV_HEADS


def _head_cols(ref):
    return [ref[:, h * HEAD:(h + 1) * HEAD] for h in range(ref.shape[1] // HEAD)]


def _wkv_specs(seq, rev):
    nc = seq // CHUNK

    def rows(col0):
        cb0 = col0 // WKV_COLS
        if rev:
            return pl.BlockSpec((CHUNK, WKV_COLS), lambda b, h, c: (b * nc + nc - 1 - c, cb0 + h))
        return pl.BlockSpec((CHUNK, WKV_COLS), lambda b, h, c: (b * nc + c, cb0 + h))

    if rev:
        st = pl.BlockSpec((1, 1, WKV_HEADS, HEAD, HEAD), lambda b, h, c: (b * WKV_GROUPS + h, nc - 1 - c, 0, 0, 0))
    else:
        st = pl.BlockSpec((1, 1, WKV_HEADS, HEAD, HEAD), lambda b, h, c: (b * WKV_GROUPS + h, c, 0, 0, 0))
    return rows, st


def _wkv_fwd(z_rkv, lw, k2, kk, a, seq):
    t = z_rkv.shape[0]
    nb, nc = t // seq, seq // CHUNK
    rows, st = _wkv_specs(seq, False)

    def body(r_ref, v_ref, lw_ref, k_ref, kk_ref, a_ref, y_ref, st_ref, s_scr):
        @pl.when(pl.program_id(2) == 0)
        def _():
            s_scr[...] = jnp.zeros_like(s_scr)

        s0 = [s_scr[h] for h in range(WKV_HEADS)]
        y, s1 = _wkv_chunk(s0, *[_head_cols(ref) for ref in (r_ref, lw_ref, k_ref, v_ref, kk_ref, a_ref)])
        for h in range(WKV_HEADS):
            st_ref[0, 0, h] = s0[h]
            y_ref[:, h * HEAD:(h + 1) * HEAD] = y[h]
            s_scr[h] = s1[h]

    return pl.pallas_call(
        body, name="wkv_fwd", grid=(nb, WKV_GROUPS, nc),
        in_specs=[rows(0), rows(2 * D), rows(0), rows(0), rows(0), rows(0)],
        out_specs=[rows(0), st],
        out_shape=[jax.ShapeDtypeStruct((t, D), F32),
                   jax.ShapeDtypeStruct((nb * WKV_GROUPS, nc, WKV_HEADS, HEAD, HEAD), F32)],
        scratch_shapes=[pltpu.VMEM((WKV_HEADS, HEAD, HEAD), F32)],
        compiler_params=_cp("parallel", "parallel", "arbitrary"),
    )(z_rkv, z_rkv, lw, k2, kk, a)


def _wkv_bwd(z_rkv, lw, k2, kk, a, states, dy, seq):
    t = z_rkv.shape[0]
    nb, nc = t // seq, seq // CHUNK
    rows, st = _wkv_specs(seq, True)

    def body(r_ref, v_ref, lw_ref, k_ref, kk_ref, a_ref, st_ref, dy_ref,
             dr_ref, dlw_ref, dk_ref, dv_ref, dkk_ref, da_ref, ds_scr):
        @pl.when(pl.program_id(2) == 0)
        def _():
            ds_scr[...] = jnp.zeros_like(ds_scr)

        s0 = [st_ref[0, 0, h] for h in range(WKV_HEADS)]
        _, vjp = jax.vjp(_wkv_chunk, s0, *[_head_cols(ref) for ref in (r_ref, lw_ref, k_ref, v_ref, kk_ref, a_ref)])
        grads = vjp(([x.astype(F32) for x in _head_cols(dy_ref)], [ds_scr[h] for h in range(WKV_HEADS)]))
        for h in range(WKV_HEADS):
            ds_scr[h] = grads[0][h]
            for ref, g in zip((dr_ref, dlw_ref, dk_ref, dv_ref, dkk_ref, da_ref), grads[1:]):
                ref[:, h * HEAD:(h + 1) * HEAD] = g[h]

    return pl.pallas_call(
        body, name="wkv_bwd", grid=(nb, WKV_GROUPS, nc),
        in_specs=[rows(0), rows(2 * D), rows(0), rows(0), rows(0), rows(0), st, rows(0)],
        out_specs=[rows(0)] * 6,
        out_shape=[jax.ShapeDtypeStruct((t, D), F32)] * 6,
        scratch_shapes=[pltpu.VMEM((WKV_HEADS, HEAD, HEAD), F32)],
        compiler_params=_cp("parallel", "parallel", "arbitrary"),
    )(z_rkv, z_rkv, lw, k2, kk, a, states, dy)


def _softmax(s):
    e = jnp.exp(s - jnp.max(s, axis=-1, keepdims=True))
    return e / jnp.sum(e, axis=-1, keepdims=True)


ATT_HEADS = 8
ATT_COLS = ATT_HEADS * HEAD
ATT_GROUPS = N_HEADS // ATT_HEADS


def _attn_chunk(q, kb, vb, bias, valid):
    s = _each(lambda x, y, z: jnp.where(valid, _dot_nt(x, y) * (HEAD ** -0.5) + z, MASK_VALUE), q, kb, bias)
    return _each(_dot, _each(_softmax, s), vb)


def _pad_fill(pad_ref, src_ref):
    pad_ref[0:LEFT, :] = jnp.zeros((LEFT, pad_ref.shape[1]), pad_ref.dtype)
    pad_ref[LEFT:, :] = src_ref[...].astype(pad_ref.dtype)


def _band_heads(pad_ref, start):
    return [pad_ref[pl.ds(start, BAND), h * HEAD:(h + 1) * HEAD].astype(F32) for h in range(ATT_HEADS)]


def _band_valid(c):
    return (c * CHUNK - LEFT + lax.broadcasted_iota(jnp.int32, (1, BAND), 1)) >= 0


def _attn_fwd(proj, bias, seq):
    t = proj.shape[0]
    nb, nc = t // seq, seq // CHUNK
    cq = C_Q // ATT_COLS

    def body(q_ref, k_ref, v_ref, b_ref, o_ref, kpad, vpad):
        c = pl.program_id(2)

        @pl.when(c == 0)
        def _():
            _pad_fill(kpad, k_ref)
            _pad_fill(vpad, v_ref)

        start = pl.multiple_of(c * CHUNK, CHUNK)
        o = _attn_chunk(_head_cols(q_ref), _band_heads(kpad, start), _band_heads(vpad, start),
                        [b_ref[h] for h in range(ATT_HEADS)], _band_valid(c))
        for h in range(ATT_HEADS):
            o_ref[:, h * HEAD:(h + 1) * HEAD] = o[h].astype(o_ref.dtype)

    return pl.pallas_call(
        body, name="attn_fwd", grid=(ATT_GROUPS, nb, nc),
        in_specs=[pl.BlockSpec((CHUNK, ATT_COLS), lambda h, b, c: (b * nc + c, cq + h)),
                  pl.BlockSpec((seq, ATT_COLS), lambda h, b, c: (b, cq + ATT_GROUPS + h)),
                  pl.BlockSpec((seq, ATT_COLS), lambda h, b, c: (b, cq + 2 * ATT_GROUPS + h)),
                  pl.BlockSpec((ATT_HEADS, CHUNK, BAND), lambda h, b, c: (h, 0, 0))],
        out_specs=pl.BlockSpec((CHUNK, ATT_COLS), lambda h, b, c: (b * nc + c, h)),
        out_shape=jax.ShapeDtypeStruct((t, D), BF16),
        scratch_shapes=[pltpu.VMEM((seq + LEFT, ATT_COLS), BF16)] * 2,
        compiler_params=_cp("parallel", "arbitrary", "arbitrary"),
    )(proj, proj, proj, bias)


def _attn_bwd(proj, bias, do, seq):
    t = proj.shape[0]
    nb, nc = t // seq, seq // CHUNK
    cq = C_Q // ATT_COLS

    def body(q_ref, k_ref, v_ref, b_ref, do_ref, dq_ref, dk_ref, dv_ref, db_ref, kpad, vpad, dkpad, dvpad):
        b, c = pl.program_id(1), pl.program_id(2)

        @pl.when(c == 0)
        def _():
            _pad_fill(kpad, k_ref)
            _pad_fill(vpad, v_ref)
            dkpad[...] = jnp.zeros_like(dkpad)
            dvpad[...] = jnp.zeros_like(dvpad)

        @pl.when(jnp.logical_and(b == 0, c == 0))
        def _():
            db_ref[...] = jnp.zeros_like(db_ref)

        start = pl.multiple_of(c * CHUNK, CHUNK)
        _, vjp = jax.vjp(functools.partial(_attn_chunk, valid=_band_valid(c)),
                         _head_cols(q_ref), _band_heads(kpad, start), _band_heads(vpad, start),
                         [b_ref[h] for h in range(ATT_HEADS)])
        dq, dkb, dvb, dbias = vjp([x.astype(F32) for x in _head_cols(do_ref)])
        for h in range(ATT_HEADS):
            sl = slice(h * HEAD, (h + 1) * HEAD)
            dq_ref[:, sl] = dq[h].astype(dq_ref.dtype)
            dkpad[pl.ds(start, BAND), sl] += dkb[h].astype(F32)
            dvpad[pl.ds(start, BAND), sl] += dvb[h].astype(F32)
            db_ref[h] += dbias[h]

        @pl.when(c == nc - 1)
        def _():
            dk_ref[...] = dkpad[LEFT:, :].astype(dk_ref.dtype)
            dv_ref[...] = dvpad[LEFT:, :].astype(dv_ref.dtype)

    kv_out = pl.BlockSpec((seq, ATT_COLS), lambda h, b, c: (b, h))
    return pl.pallas_call(
        body, name="attn_bwd", grid=(ATT_GROUPS, nb, nc),
        in_specs=[pl.BlockSpec((CHUNK, ATT_COLS), lambda h, b, c: (b * nc + c, cq + h)),
                  pl.BlockSpec((seq, ATT_COLS), lambda h, b, c: (b, cq + ATT_GROUPS + h)),
                  pl.BlockSpec((seq, ATT_COLS), lambda h, b, c: (b, cq + 2 * ATT_GROUPS + h)),
                  pl.BlockSpec((ATT_HEADS, CHUNK, BAND), lambda h, b, c: (h, 0, 0)),
                  pl.BlockSpec((CHUNK, ATT_COLS), lambda h, b, c: (b * nc + c, h))],
        out_specs=[pl.BlockSpec((CHUNK, ATT_COLS), lambda h, b, c: (b * nc + c, h)), kv_out, kv_out,
                   pl.BlockSpec((ATT_HEADS, CHUNK, BAND), lambda h, b, c: (h, 0, 0))],
        out_shape=[jax.ShapeDtypeStruct((t, D), BF16)] * 3 + [jax.ShapeDtypeStruct((N_HEADS, CHUNK, BAND), F32)],
        scratch_shapes=[pltpu.VMEM((seq + LEFT, ATT_COLS), BF16)] * 2 + [pltpu.VMEM((seq + LEFT, ATT_COLS), F32)] * 2,
        compiler_params=_cp("parallel", "arbitrary", "arbitrary"),
    )(proj, proj, proj, bias, do)


def _xattn_tile(q, k, v):
    s = _dot_nt(q, k) * ((MEM_WIDTH // MEM_HEADS) ** -0.5)
    return _dot(_softmax(s), v)


def _xattn_fwd(qm, kvm, seq, n_mem, tq=512):
    t = qm.shape[0]
    tq = min(tq, seq)
    nb, nq = t // seq, seq // tq

    def body(q_ref, k_ref, v_ref, o_ref):
        o_ref[...] = _xattn_tile(q_ref[...], k_ref[...], v_ref[...]).astype(o_ref.dtype)

    return pl.pallas_call(
        body, name="xattn_fwd", grid=(nb, MEM_HEADS, nq),
        in_specs=[pl.BlockSpec((tq, LANE), lambda b, h, i: (b * nq + i, h)),
                  pl.BlockSpec((n_mem, LANE), lambda b, h, i: (b, h)),
                  pl.BlockSpec((n_mem, LANE), lambda b, h, i: (b, MEM_HEADS + h))],
        out_specs=pl.BlockSpec((tq, LANE), lambda b, h, i: (b * nq + i, h)),
        out_shape=jax.ShapeDtypeStruct((t, MEM_WIDTH), BF16),
        compiler_params=_cp("parallel", "parallel", "parallel"),
    )(qm, kvm, kvm)


def _xattn_bwd(qm, kvm, do, seq, n_mem, tq=512):
    t = qm.shape[0]
    tq = min(tq, seq)
    nb, nq = t // seq, seq // tq

    def body(q_ref, k_ref, v_ref, do_ref, dq_ref, dkv_ref, dk_acc, dv_acc):
        i = pl.program_id(2)

        @pl.when(i == 0)
        def _():
            dk_acc[...] = jnp.zeros_like(dk_acc)
            dv_acc[...] = jnp.zeros_like(dv_acc)

        _, vjp = jax.vjp(_xattn_tile, q_ref[...], k_ref[...], v_ref[...])
        dq, dk, dv = vjp(do_ref[...].astype(F32))
        dq_ref[...] = dq.astype(dq_ref.dtype)
        dk_acc[...] += dk
        dv_acc[...] += dv

        @pl.when(i == nq - 1)
        def _():
            dkv_ref[0] = dk_acc[...].astype(dkv_ref.dtype)
            dkv_ref[1] = dv_acc[...].astype(dkv_ref.dtype)

    dq, dkv = pl.pallas_call(
        body, name="xattn_bwd", grid=(nb, MEM_HEADS, nq),
        in_specs=[pl.BlockSpec((tq, LANE), lambda b, h, i: (b * nq + i, h)),
                  pl.BlockSpec((n_mem, LANE), lambda b, h, i: (b, h)),
                  pl.BlockSpec((n_mem, LANE), lambda b, h, i: (b, MEM_HEADS + h)),
                  pl.BlockSpec((tq, LANE), lambda b, h, i: (b * nq + i, h))],
        out_specs=[pl.BlockSpec((tq, LANE), lambda b, h, i: (b * nq + i, h)),
                   pl.BlockSpec((2, n_mem, LANE), lambda b, h, i: (0, b, h))],
        out_shape=[jax.ShapeDtypeStruct((t, MEM_WIDTH), BF16), jax.ShapeDtypeStruct((2, nb * n_mem, MEM_WIDTH), BF16)],
        scratch_shapes=[pltpu.VMEM((n_mem, LANE), F32)] * 2,
        compiler_params=_cp("parallel", "parallel", "arbitrary"),
    )(qm, kvm, kvm, do)
    return dq, jnp.concatenate([dkv[0], dkv[1]], axis=1)


def _loss_head(y, target, tm=512):
    t, d = y.shape
    tm = min(tm, t)

    def body(y_ref, t_ref, dy_ref, l_ref):
        @pl.when(pl.program_id(0) == 0)
        def _():
            l_ref[...] = jnp.zeros_like(l_ref)

        diff = y_ref[...] - t_ref[...]
        dy_ref[...] = diff * (1.0 / d)
        l_ref[...] += 0.5 * jnp.sum(jnp.mean(diff * diff, axis=-1, keepdims=True), axis=0, keepdims=True)

    dy, loss = pl.pallas_call(
        body, name="loss_head", grid=(t // tm,),
        in_specs=[pl.BlockSpec((tm, d), lambda i: (i, 0))] * 2,
        out_specs=[pl.BlockSpec((tm, d), lambda i: (i, 0)), pl.BlockSpec((8, LANE), lambda i: (0, 0))],
        out_shape=[jax.ShapeDtypeStruct((t, d), F32), jax.ShapeDtypeStruct((8, LANE), F32)],
        compiler_params=_cp("arbitrary"),
    )(y, target)
    return dy, loss


def _mesh_pos():
    return lax.axis_index("x"), lax.axis_index("y"), lax.axis_index("c")


def _peer(pos, d):
    x, y, c = pos
    return ((1 - x) if d & 4 else x, (1 - y) if d & 2 else y, (1 - c) if d & 1 else c)


def _flat(pos):
    return 4 * pos[0] + 2 * pos[1] + pos[2]


def _exchange(arrays, scatter, *, name):
    n = len(arrays)
    shapes = [a.shape[1:] if scatter else a.shape for a in arrays]

    def body(*refs):
        ins, outs = refs[:n], refs[n:2 * n]
        send, recv, loc = refs[2 * n:]
        pos = _mesh_pos()
        me = _flat(pos)
        pending = []
        for i in range(n):
            own = pltpu.make_async_copy(ins[i].at[me] if scatter else ins[i], outs[i].at[me], loc.at[i])
            own.start()
            pending.append(own)
            for d in range(1, N_DEV):
                peer = _peer(pos, d)
                src = ins[i].at[_flat(peer)] if scatter else ins[i]
                out_cp = pltpu.make_async_remote_copy(
                    src_ref=src, dst_ref=outs[i].at[me], send_sem=send.at[i, d - 1], recv_sem=recv.at[i, d - 1],
                    device_id=peer, device_id_type=pl.DeviceIdType.MESH)
                out_cp.start()
                pending.append(out_cp)
        for i in range(n):
            own = pending[i * N_DEV]
            for d in range(1, N_DEV):
                peer = _peer(pos, d)
                src = ins[i].at[_flat(peer)] if scatter else ins[i]
                pending[i * N_DEV + d].wait_send()
                pltpu.make_async_remote_copy(
                    src_ref=src, dst_ref=outs[i].at[_flat(peer)], send_sem=send.at[i, d - 1], recv_sem=recv.at[i, d - 1],
                    device_id=peer, device_id_type=pl.DeviceIdType.MESH).wait_recv()
            own.wait()

    hbm = pl.BlockSpec(memory_space=pltpu.HBM)
    return pl.pallas_call(
        body, name=name,
        in_specs=[hbm] * n, out_specs=[hbm] * n,
        out_shape=[jax.ShapeDtypeStruct((N_DEV,) + tuple(s), a.dtype) for s, a in zip(shapes, arrays)],
        scratch_shapes=[pltpu.SemaphoreType.DMA((n, N_DEV - 1)), pltpu.SemaphoreType.DMA((n, N_DEV - 1)),
                        pltpu.SemaphoreType.DMA((n,))],
    )(*arrays)


def _adamw(parts, w, m, v, *, name, tr=128):
    r, c = w.shape
    align = 8 * 4 // parts.dtype.itemsize
    tr = max(d for d in range(align, min(tr, r) + 1, align) if r % d == 0)

    def body(p_ref, w_ref, m_ref, v_ref, g_ref, d_ref, nm_ref, nv_ref):
        g = p_ref[0].astype(F32)
        for j in range(1, N_DEV):
            g = g + p_ref[j].astype(F32)
        m2 = ADAM_B1 * m_ref[...] + (1.0 - ADAM_B1) * g
        v2 = ADAM_B2 * v_ref[...] + (1.0 - ADAM_B2) * (g * g)
        m_hat = m2 / (1.0 - ADAM_B1 ** ADAM_STEP)
        v_hat = v2 / (1.0 - ADAM_B2 ** ADAM_STEP)
        g_ref[...] = g
        d_ref[...] = -ADAM_LR * (m_hat / (jnp.sqrt(v_hat) + ADAM_EPS) + ADAM_WD * w_ref[...])
        nm_ref[...] = m2
        nv_ref[...] = v2

    spec = pl.BlockSpec((tr, c), lambda i: (i, 0))
    return pl.pallas_call(
        body, name=name, grid=(r // tr,),
        in_specs=[pl.BlockSpec((N_DEV, tr, c), lambda i: (0, i, 0)), spec, spec, spec],
        out_specs=[spec] * 4, out_shape=[jax.ShapeDtypeStruct((r, c), F32)] * 4,
        compiler_params=_cp("parallel"),
    )(parts, w, m, v)


def _cols_to_full(g):
    return jnp.transpose(g, (1, 0, 2)).reshape(g.shape[1], N_DEV * g.shape[2])


def _full_to_cols(w):
    r, c = w.shape
    return jnp.transpose(w.reshape(r, N_DEV, c // N_DEV), (1, 0, 2))


def _pad_cols(a, width):
    return jnp.pad(a, ((0, 0), (0, width - a.shape[1])))


def _pad_lora(w):
    return jnp.concatenate([
        _pad_cols(w[:, :LORA_W], 128), _pad_cols(w[:, LORA_W:LORA_W + LORA_A], 128),
        _pad_cols(w[:, LORA_W + LORA_A:], 256)], axis=1)


def _unpad_lora(wp):
    return jnp.concatenate([wp[:, :LORA_W], wp[:, 128:128 + LORA_A], wp[:, 256:256 + LORA_G]], axis=1)


def _permute_in(w):
    rk = 3 * D
    lo = rk + LORA_W + LORA_A + LORA_G
    return jnp.concatenate([w[:, :rk], w[:, lo:], _pad_lora(w[:, rk:lo])], axis=1)


def _unpermute_in(wp):
    return jnp.concatenate([wp[:, :3 * D], _unpad_lora(wp[:, C_LORA:]), wp[:, 3 * D:C_LORA]], axis=1)


def _rel_index():
    dist = jnp.arange(CHUNK)[:, None] - jnp.arange(BAND)[None, :] + LEFT
    return (jnp.minimum(dist, REL_CLIP) + (CHUNK - 1)).reshape(-1)


def _local_step(x, mem, target, wt, seq, n_mem):
    t = x.shape[0]
    row = lambda a: a.reshape(1, -1).astype(F32)
    g_pre_mix, g_post_mix = row(wt["g_pre_mix"]), row(wt["g_post_mix"])
    g_pre_cross, g_post_cross, g_mem = row(wt["g_pre_cross"]), row(wt["g_post_cross"]), row(wt["g_mem"])
    g_pre_ffn, g_post_ffn = row(wt["g_pre_ffn"]), row(wt["g_post_ffn"])
    w_in = wt["w_in_p"]
    mix = row(wt["shift_mix"])
    mix_rkv, mix_lora = mix[:, :3 * D], _pad_lora(mix[:, 3 * D:])
    d_up = jnp.pad(wt["decay_up"].astype(F32), ((0, 128 - LORA_W), (0, 0)))
    i_up = jnp.pad(wt["iclr_up"].astype(F32), ((0, 128 - LORA_A), (0, 0)))
    g_up = jnp.pad(wt["gate_up"].astype(F32), ((0, 256 - LORA_G), (0, 0)))
    decay_base, iclr_base = row(wt["decay_base"]), row(wt["iclr_base"])
    kns, kis = row(wt["key_norm_scale"]), row(wt["key_iclr_scale"])
    lnx_w, lnx_b, bonus = row(wt["lnx_w"]), row(wt["lnx_b"]), row(wt["bonus_scale"])
    e_dh = (jnp.arange(D)[:, None] // HEAD == jnp.arange(N_HEADS)[None, :]).astype(F32)
    e_hd = e_dh.T
    onehot = (jnp.arange(REL_TABLE)[:, None] == _rel_index()[None, :]).astype(BF16)

    (h1,) = _rowwise(_fn_pre, [_win(x)], [g_pre_mix], [(D, BF16)], name="pre_mix", tm=512)
    proj = _mm(h1, w_in, name="mm_in")
    z_rkv = _shift_fwd(proj, 0, 3 * D, mix_rkv, seq, name="shift_rkv")
    z_lora = _shift_fwd(proj, C_LORA, 512, mix_lora, seq, name="shift_lora")
    prep_rows = [_win(z_rkv, D, D), _win(z_lora, 0, 128), _win(z_lora, 128, 128), _win(z_lora, 256, 256)]
    prep_params = [decay_base, d_up, iclr_base, i_up, g_up, kns, kis, e_hd, e_dh]
    lw, k2, kk, a, g = _rowwise(_fn_prep, prep_rows, prep_params, [(D, F32)] * 5, name="rwkv_prep", tm=256)
    y, states = _wkv_fwd(z_rkv, lw, k2, kk, a, seq)
    post_rows = [_win(y), _win(z_rkv, 0, D), _win(k2), _win(z_rkv, 2 * D, D), _win(g)]
    post_params = [lnx_w, lnx_b, bonus, e_hd, e_dh]
    (y_a,) = _rowwise(_fn_post, post_rows, post_params, [(D, BF16)], name="rwkv_post", tm=256)
    bias = _mm(wt["rel_bias"].astype(F32), onehot, name="mm_bias", split_a=3).reshape(N_HEADS, CHUNK, BAND)
    y_b = _attn_fwd(proj, bias, seq)
    ya_p = _mm(y_a, wt["w_branch_a"], name="mm_a")
    yb_p = _mm(y_b, wt["w_branch_b"], name="mm_b")
    mix_rows = [_win(proj, C_GA, D), _win(proj, C_GA + D, D), _win(ya_p), _win(yb_p)]
    (mixed,) = _rowwise(_fn_mix, mix_rows, [], [(D, BF16)], name="gate_mix", tm=512)
    mo = _mm(mixed, wt["w_out"], name="mm_out")
    x1, h2 = _rowwise(_fn_res_pre, [_win(x), _win(mo)], [g_post_mix, g_pre_cross], [(D, F32), (D, BF16)],
                      name="res_mix", tm=512)
    qm = _mm(h2, wt["w_q_mem"], name="mm_q")
    (mn,) = _rowwise(_fn_pre, [_win(mem)], [g_mem], [(D, BF16)], name="pre_mem", tm=512)
    kvm = _mm(mn, wt["w_kv_mem"], name="mm_kv")
    om = _xattn_fwd(qm, kvm, seq, n_mem)
    co = _mm(om, wt["w_o_mem"], name="mm_o")
    x2, h3 = _rowwise(_fn_res_pre, [_win(x1), _win(co)], [g_post_cross, g_pre_ffn], [(D, F32), (D, BF16)],
                      name="res_cross", tm=512)
    gu = _mm(h3, wt["w_ffn_in"], name="mm_ffn_in")
    (act,) = _rowwise(_fn_swiglu, [_win(gu, 0, FFN), _win(gu, FFN, FFN)], [], [(FFN, BF16)], name="swiglu", tm=256)
    ff = _mm(act, wt["w_ffn_out"], name="mm_ffn_out")
    (x3,) = _rowwise(_fn_res, [_win(x2), _win(ff)], [g_post_ffn], [(D, F32)], name="res_ffn", tm=512)
    dx3, loss = _loss_head(x3, target)

    gw = {}
    (dx2, dff), (gw["g_post_ffn"],) = _rowwise_bwd(
        _fn_res, [_win(x2), _win(ff)], [g_post_ffn], 0, [[dx3]], name="res_ffn_bwd", tm=256, row_grad=[F32, BF16])
    dact = _mm(dff, wt["w_ffn_out"], tb=True, name="mm_ffn_out_dx", out_dtype=BF16)
    gw["w_ffn_out"] = _mm(act, dff, ta=True, name="mm_ffn_out_dw")
    (dgate, dup), _ = _rowwise_bwd(_fn_swiglu, [_win(gu, 0, FFN), _win(gu, FFN, FFN)], [], 0, [[dact]],
                                   name="swiglu_bwd", tm=256, row_grad=[BF16, BF16])
    dgu = jnp.concatenate([dgate, dup], axis=1)
    dh3 = _mm(dgu, wt["w_ffn_in"], tb=True, name="mm_ffn_in_dx", out_dtype=BF16)
    gw["w_ffn_in"] = _mm(h3, dgu, ta=True, name="mm_ffn_in_dw")
    (dx1, dco), (gw["g_post_cross"], gw["g_pre_ffn"]) = _rowwise_bwd(
        _fn_res_pre, [_win(x1), _win(co)], [g_post_cross, g_pre_ffn], 0, [[dx2], [dh3]],
        name="res_cross_bwd", tm=256, row_grad=[F32, BF16])
    dom = _mm(dco, wt["w_o_mem"], tb=True, name="mm_o_dx", out_dtype=BF16)
    gw["w_o_mem"] = _mm(om, dco, ta=True, name="mm_o_dw")
    dqm, dkvm = _xattn_bwd(qm, kvm, dom, seq, n_mem)
    dh2 = _mm(dqm, wt["w_q_mem"], tb=True, name="mm_q_dx", out_dtype=BF16)
    gw["w_q_mem"] = _mm(h2, dqm, ta=True, name="mm_q_dw")
    dmn = _mm(dkvm, wt["w_kv_mem"], tb=True, name="mm_kv_dx", out_dtype=BF16)
    gw["w_kv_mem"] = _mm(mn, dkvm, ta=True, name="mm_kv_dw")
    _, (gw["g_mem"],) = _rowwise_bwd(_fn_pre, [_win(mem)], [g_mem], 0, [[dmn]], name="pre_mem_bwd", tm=256,
                                     row_grad=[None])
    (dx0, dmo), (gw["g_post_mix"], gw["g_pre_cross"]) = _rowwise_bwd(
        _fn_res_pre, [_win(x), _win(mo)], [g_post_mix, g_pre_cross], 0, [[dx1], [dh2]],
        name="res_mix_bwd", tm=256, row_grad=[F32, BF16])
    dmixed = _mm(dmo, wt["w_out"], tb=True, name="mm_out_dx", out_dtype=BF16)
    gw["w_out"] = _mm(mixed, dmo, ta=True, name="mm_out_dw")
    (dzga, dzgb, dya_p, dyb_p), _ = _rowwise_bwd(_fn_mix, mix_rows, [], 0, [[dmixed]], name="gate_mix_bwd", tm=256,
                                                 row_grad=[BF16] * 4)
    dy_a = _mm(dya_p, wt["w_branch_a"], tb=True, name="mm_a_dx", out_dtype=BF16)
    gw["w_branch_a"] = _mm(y_a, dya_p, ta=True, name="mm_a_dw")
    dy_b = _mm(dyb_p, wt["w_branch_b"], tb=True, name="mm_b_dx", out_dtype=BF16)
    gw["w_branch_b"] = _mm(y_b, dyb_p, ta=True, name="mm_b_dw")
    dq, dk, dv, dbias = _attn_bwd(proj, bias, dy_b, seq)
    gw["rel_bias"] = _mm(dbias.reshape(N_HEADS, CHUNK * BAND), onehot, tb=True, name="mm_bias_dw", split_a=2)
    (dy, dr_p, dk2_p, dv_p, dg), (gw["lnx_w"], gw["lnx_b"], gw["bonus_scale"]) = _rowwise_bwd(
        _fn_post, post_rows, post_params, 2, [[dy_a]], name="rwkv_post_bwd", tm=128, row_grad=[F32] * 5)
    dr_s, dlw, dk2_s, dv_s, dkk, da = _wkv_bwd(z_rkv, lw, k2, kk, a, states, dy, seq)
    (dzk, dzw, dza, dzg), pg = _rowwise_bwd(
        _fn_prep, prep_rows, prep_params, 2, [[dlw], [dk2_p, dk2_s], [dkk], [da], [dg]],
        name="rwkv_prep_bwd", tm=128, row_grad=[F32] * 4)
    gw["decay_base"], gd_up, gw["iclr_base"], gi_up, gg_up, gw["key_norm_scale"], gw["key_iclr_scale"] = pg
    gw["decay_up"], gw["iclr_up"], gw["gate_up"] = gd_up[:LORA_W], gi_up[:LORA_A], gg_up[:LORA_G]
    dp_r, gmix_r = _shift_bwd(proj, 0, D, mix_rkv[:, :D], [dr_p, dr_s], seq, name="shift_r_bwd")
    dp_k, gmix_k = _shift_bwd(proj, D, D, mix_rkv[:, D:2 * D], [dzk], seq, name="shift_k_bwd")
    dp_v, gmix_v = _shift_bwd(proj, 2 * D, D, mix_rkv[:, 2 * D:], [dv_p, dv_s], seq, name="shift_v_bwd")
    dp_lora, gmix_lora = _shift_bwd(proj, C_LORA, 512, mix_lora, [jnp.concatenate([dzw, dza, dzg], axis=1)], seq,
                                    name="shift_lora_bwd")
    gw["shift_mix"] = jnp.concatenate([gmix_r, gmix_k, gmix_v, _unpad_lora(gmix_lora)], axis=1)
    dproj = jnp.concatenate([dp_r, dp_k, dp_v, dq, dk, dv, dzga, dzgb, dp_lora], axis=1)
    dh1 = _mm(dproj, w_in, tb=True, name="mm_in_dx", out_dtype=BF16)
    gw["w_in_p"] = _mm(h1, dproj, ta=True, name="mm_in_dw")
    (grad_x,), (gw["g_pre_mix"],) = _rowwise_bwd(_fn_pre, [_win(x)], [g_pre_mix], 0, [[dh1]], name="pre_mix_bwd",
                                                 tm=256, row_grad=[F32], add_to={0: dx0})
    return loss, grad_x, gw


_COL_SHARDED = ("w_in", "decay_up", "iclr_up", "gate_up", "w_o_mem", "w_ffn_in")
_ROW_SHARDED = ("w_branch_a", "w_branch_b", "w_out", "w_q_mem", "w_kv_mem", "w_ffn_out")
_REPLICATED = ("g_pre_mix", "g_post_mix", "shift_mix", "decay_base", "iclr_base", "key_norm_scale", "key_iclr_scale",
               "bonus_scale", "lnx_w", "lnx_b", "rel_bias", "g_pre_cross", "g_post_cross", "g_mem", "g_pre_ffn",
               "g_post_ffn")
_WEIGHTS = ("g_pre_mix", "g_post_mix", "w_in", "shift_mix", "decay_base", "decay_up", "iclr_base", "iclr_up", "gate_up",
            "key_norm_scale", "key_iclr_scale", "bonus_scale", "lnx_w", "lnx_b", "rel_bias", "w_branch_a", "w_branch_b",
            "w_out", "g_pre_cross", "g_post_cross", "g_mem", "w_q_mem", "w_kv_mem", "w_o_mem", "g_pre_ffn", "g_post_ffn",
            "w_ffn_in", "w_ffn_out")
_PACK_ROWS = 8 * ((sum({"shift_mix": 3360, "bonus_scale": 1024, "rel_bias": 3072}.get(n, D) for n in _REPLICATED)
                   + 1 + 8 * LANE - 1) // (8 * LANE))


def _pack(vals):
    flat = jnp.concatenate([v.reshape(-1).astype(F32) for v in vals])
    return jnp.pad(flat, (0, _PACK_ROWS * LANE - flat.shape[0])).reshape(_PACK_ROWS, LANE)


def _unpack(packed, shapes):
    flat, out, pos = packed.reshape(-1), [], 0
    for s in shapes:
        n = math.prod(s)
        out.append(flat[pos:pos + n].reshape(s))
        pos += n
    return out


def _step(args, seq, n_mem):
    names = ("x", "mem") + _WEIGHTS + ("loss_target",) + tuple("m_" + n for n in _WEIGHTS) + tuple("v_" + n for n in _WEIGHTS)
    given = dict(zip(names, args))
    nb = given["x"].shape[0]
    x = given["x"].reshape(nb * seq, D)
    mem = given["mem"].reshape(nb * n_mem, D)
    target = given["loss_target"].reshape(nb * seq, D)
    shard = {n: given[n][0] for n in _COL_SHARDED + _ROW_SHARDED}

    order = _COL_SHARDED + _ROW_SHARDED
    gathered = dict(zip(order, _exchange([shard[n].astype(BF16) for n in order], False, name="gather_weights")))
    wt = {n: given[n][0] for n in _REPLICATED}
    for n in _COL_SHARDED:
        wt[n] = _cols_to_full(gathered[n])
    for n in _ROW_SHARDED:
        wt[n] = gathered[n].reshape(-1, gathered[n].shape[-1])
    wt["w_in_p"] = _permute_in(wt.pop("w_in"))

    loss_tile, grad_x, gw = _local_step(x, mem, target, wt, seq, n_mem)
    gw["w_in"] = _unpermute_in(gw.pop("w_in_p"))

    blocks = [(_full_to_cols(gw[n]) if n in _COL_SHARDED else gw[n].reshape((N_DEV,) + shard[n].shape)).astype(BF16)
              for n in order]
    landed = dict(zip(order, _exchange(blocks, True, name="scatter_grads")))
    rep_shapes = [given[n].shape for n in _REPLICATED]
    small = _exchange([_pack([gw[n] for n in _REPLICATED] + [loss_tile[0, 0]])], False, name="gather_small")[0]

    out = {}
    for n in order:
        res = _adamw(landed[n], shard[n], given["m_" + n][0], given["v_" + n][0], name="adamw_" + n)
        for kind, r in zip(("grad_", "delta_", "new_m_", "new_v_"), res):
            out[kind + n] = r[None]
    zero = jnp.zeros((), F32)
    res = _adamw(small, *[_pack([given[p + n] for n in _REPLICATED] + [zero]) for p in ("", "m_", "v_")],
                 name="adamw_small", tr=_PACK_ROWS)
    for kind, r in zip(("grad_", "delta_", "new_m_", "new_v_"), res):
        for n, val in zip(_REPLICATED, _unpack(r, rep_shapes)):
            out[kind + n] = val
    loss = res[0].reshape(-1)[sum(math.prod(s) for s in rep_shapes)]
    grad_x = grad_x.reshape(nb, seq, D)
    return (loss, grad_x, *[out[k + n] for k in ("grad_", "delta_", "new_m_", "new_v_") for n in _WEIGHTS])


def kernel(x, mem, g_pre_mix, g_post_mix, w_in, shift_mix, decay_base, decay_up, iclr_base, iclr_up, gate_up, key_norm_scale, key_iclr_scale, bonus_scale, lnx_w, lnx_b, rel_bias, w_branch_a, w_branch_b, w_out, g_pre_cross, g_post_cross, g_mem, w_q_mem, w_kv_mem, w_o_mem, g_pre_ffn, g_post_ffn, w_ffn_in, w_ffn_out, loss_target, m_g_pre_mix, m_g_post_mix, m_w_in, m_shift_mix, m_decay_base, m_decay_up, m_iclr_base, m_iclr_up, m_gate_up, m_key_norm_scale, m_key_iclr_scale, m_bonus_scale, m_lnx_w, m_lnx_b, m_rel_bias, m_w_branch_a, m_w_branch_b, m_w_out, m_g_pre_cross, m_g_post_cross, m_g_mem, m_w_q_mem, m_w_kv_mem, m_w_o_mem, m_g_pre_ffn, m_g_post_ffn, m_w_ffn_in, m_w_ffn_out, v_g_pre_mix, v_g_post_mix, v_w_in, v_shift_mix, v_decay_base, v_decay_up, v_iclr_base, v_iclr_up, v_gate_up, v_key_norm_scale, v_key_iclr_scale, v_bonus_scale, v_lnx_w, v_lnx_b, v_rel_bias, v_w_branch_a, v_w_branch_b, v_w_out, v_g_pre_cross, v_g_post_cross, v_g_mem, v_w_q_mem, v_w_kv_mem, v_w_o_mem, v_g_pre_ffn, v_g_post_ffn, v_w_ffn_in, v_w_ffn_out):
    args = (x, mem, g_pre_mix, g_post_mix, w_in, shift_mix, decay_base, decay_up, iclr_base, iclr_up, gate_up, key_norm_scale, key_iclr_scale, bonus_scale, lnx_w, lnx_b, rel_bias, w_branch_a, w_branch_b, w_out, g_pre_cross, g_post_cross, g_mem, w_q_mem, w_kv_mem, w_o_mem, g_pre_ffn, g_post_ffn, w_ffn_in, w_ffn_out, loss_target, m_g_pre_mix, m_g_post_mix, m_w_in, m_shift_mix, m_decay_base, m_decay_up, m_iclr_base, m_iclr_up, m_gate_up, m_key_norm_scale, m_key_iclr_scale, m_bonus_scale, m_lnx_w, m_lnx_b, m_rel_bias, m_w_branch_a, m_w_branch_b, m_w_out, m_g_pre_cross, m_g_post_cross, m_g_mem, m_w_q_mem, m_w_kv_mem, m_w_o_mem, m_g_pre_ffn, m_g_post_ffn, m_w_ffn_in, m_w_ffn_out, v_g_pre_mix, v_g_post_mix, v_w_in, v_shift_mix, v_decay_base, v_decay_up, v_iclr_base, v_iclr_up, v_gate_up, v_key_norm_scale, v_key_iclr_scale, v_bonus_scale, v_lnx_w, v_lnx_b, v_rel_bias, v_w_branch_a, v_w_branch_b, v_w_out, v_g_pre_cross, v_g_post_cross, v_g_mem, v_w_q_mem, v_w_kv_mem, v_w_o_mem, v_g_pre_ffn, v_g_post_ffn, v_w_ffn_in, v_w_ffn_out)
    return _step(args, x.shape[1], mem.shape[1])
```

```python
import functools
import math

import jax
import jax.numpy as jnp
from jax import lax
from jax.experimental import pallas as pl
from jax.experimental.pallas import tpu as pltpu

F32 = jnp.float32
BF16 = jnp.bfloat16

N_DEV = 8
D = 1024
HEAD = 64
N_HEADS = D // HEAD
LANE = 128
N_PAIRS = D // LANE
CHUNK = 64
LEFT = 8 * CHUNK
BAND = LEFT + CHUNK
REL_CLIP = 128
REL_TABLE = CHUNK + REL_CLIP
MEM_WIDTH = D // 2
MEM_HEADS = 4
FFN = 2816
LORA_W, LORA_A, LORA_G = 64, 64, 160
P_WIDTH = 3 * D + 3 * D + 2 * D + 128 + 128 + 256
C_Q, C_GA, C_LORA = 3 * D, 6 * D, 8 * D
NORM_EPS = 1e-6
GROUP_NORM_EPS = 64e-5
MASK_VALUE = -1e30
ADAM_LR, ADAM_B1, ADAM_B2, ADAM_EPS, ADAM_WD, ADAM_STEP = 0.001, 0.9, 0.999, 1e-08, 0.01, 10
VMEM_LIMIT = 56 * 1024 * 1024


def _cp(*sem):
    return pltpu.CompilerParams(dimension_semantics=sem, vmem_limit_bytes=VMEM_LIMIT)


_NN, _NT, _TN = ((1,), (0,)), ((1,), (1,)), ((0,), (0,))


def _dot_raw(a, b, dims):
    return lax.dot_general(a.astype(BF16), b.astype(BF16), (dims, ((), ())), preferred_element_type=F32)


@functools.partial(jax.custom_vjp, nondiff_argnums=(2,))
def _dot_dims(a, b, dims):
    return _dot_raw(a, b, dims)


def _dot_dims_fwd(a, b, dims):
    return _dot_raw(a, b, dims), (a, b)


def _dot_dims_bwd(dims, res, g):
    a, b = res
    if dims == _NN:
        da, db = _dot_raw(g, b, _NT), _dot_raw(a, g, _TN)
    elif dims == _NT:
        da, db = _dot_raw(g, b, _NN), _dot_raw(g, a, _TN)
    else:
        da, db = _dot_raw(b, g, _NT), _dot_raw(a, g, _NN)
    return da.astype(a.dtype), db.astype(b.dtype)


_dot_dims.defvjp(_dot_dims_fwd, _dot_dims_bwd)


def _dot(a, b, dims=_NN):
    return _dot_dims(a, b, dims)


def _dot_nt(a, b):
    return _dot_dims(a, b, _NT)


def _dot_tn(a, b):
    return _dot_dims(a, b, _TN)


def _split(x, terms):
    parts, rest = [], x.astype(F32)
    for _ in range(terms):
        p = rest.astype(BF16)
        parts.append(p)
        rest = rest - p.astype(F32)
    return parts


def _dot_split_a(a, b, terms=2):
    out = None
    for p in _split(a, terms):
        t = _dot(p, b)
        out = t if out is None else out + t
    return out


def _dot_split_b(a, b, terms=3):
    out = None
    for p in _split(b, terms):
        t = _dot(a, p)
        out = t if out is None else out + t
    return out


def _dot_hi(a, b, dims=_NN):
    ah, al = _split(a, 2)
    bh, bl = _split(b, 2)
    return _dot(ah, bh, dims) + (_dot(ah, bl, dims) + _dot(al, bh, dims))


MM_VMEM_BUDGET = 30 * 1024 * 1024
MM_HBM_BPS = 3.2e12
MM_MXU_FPS = 8.5e14
MM_STEP_S = 0.35e-6


def _divisors(n, align, cap):
    out = [d for d in range(align, min(n, cap) + 1, align) if n % d == 0]
    return out or [n]


def _mm_tiles(m, n, k, ea, eb, eo, ta):
    best = None
    for tm in _divisors(m, LANE if ta else 8, 2048):
        for tn in _divisors(n, LANE, 2048):
            for tk in _divisors(k, LANE, 2048):
                nk = k // tk
                vmem = 2 * (tm * tk * ea + tk * tn * eb + tm * tn * eo) + (tm * tn * 4 if nk > 1 else 0)
                if vmem > MM_VMEM_BUDGET:
                    continue
                dma = (tm * tk * ea if (nk > 1 or n // tn == 1) else tm * tk * ea * tn / n) + tk * tn * eb + tm * tn * eo / nk
                step = max(2.0 * tm * tn * tk / MM_MXU_FPS, dma / MM_HBM_BPS) + MM_STEP_S
                cost = (m // tm) * (n // tn) * nk * step
                if best is None or cost < best[0]:
                    best = (cost, tm, tn, tk)
    return best[1:]


def _mm(a, b, *, name, ta=False, tb=False, out_dtype=F32, tm=None, tn=None, tk=None, split_a=1, after=None):
    m, k = (a.shape[1], a.shape[0]) if ta else a.shape
    n, kb = (b.shape[0], b.shape[1]) if tb else (b.shape[1], b.shape[0])
    assert k == kb, (a.shape, b.shape, ta, tb)
    if tm is None:
        tm, tn, tk = _mm_tiles(m, n, k, a.dtype.itemsize, b.dtype.itemsize, jnp.dtype(out_dtype).itemsize, ta)
    assert m % tm == 0 and n % tn == 0 and k % tk == 0, (m, n, k, tm, tn, tk)
    nk = k // tk
    dims = ((0 if ta else 1,), (1 if tb else 0,))

    n_after = 0 if after is None else 1

    def body(a_ref, b_ref, *rest):
        o_ref, scratch = rest[n_after], rest[n_after + 1:]
        prod = None
        for p in _split(a_ref[...], split_a) if split_a > 1 else [a_ref[...]]:
            t = _dot_raw(p, b_ref[...], dims)
            prod = t if prod is None else prod + t
        if nk == 1:
            o_ref[...] = prod.astype(o_ref.dtype)
            return
        acc_ref, kk = scratch[0], pl.program_id(2)

        @pl.when(kk == 0)
        def _():
            acc_ref[...] = prod

        @pl.when(kk > 0)
        def _():
            acc_ref[...] += prod

        @pl.when(kk == nk - 1)
        def _():
            o_ref[...] = acc_ref[...].astype(o_ref.dtype)

    a_spec = pl.BlockSpec((tk, tm), lambda i, j, q: (q, i)) if ta else pl.BlockSpec((tm, tk), lambda i, j, q: (i, q))
    b_spec = pl.BlockSpec((tn, tk), lambda i, j, q: (j, q)) if tb else pl.BlockSpec((tk, tn), lambda i, j, q: (q, j))
    return pl.pallas_call(
        body, name=name, grid=(m // tm, n // tn, nk),
        in_specs=[a_spec, b_spec] + [pl.BlockSpec(memory_space=pl.ANY)] * n_after,
        out_specs=pl.BlockSpec((tm, tn), lambda i, j, q: (i, j)),
        out_shape=jax.ShapeDtypeStruct((m, n), out_dtype),
        scratch_shapes=[pltpu.VMEM((tm, tn), F32)] if nk > 1 else [],
        compiler_params=_cp("parallel", "parallel", "arbitrary"),
    )(a, b, *([] if after is None else [after]))


def _win(arr, start=0, width=None):
    width = arr.shape[1] if width is None else width
    assert start % width == 0
    return (arr, start // width, width)


def _row_specs(rows, tm):
    return [pl.BlockSpec((tm, w), functools.partial(lambda i, cb: (i, cb), cb=cb)) for (_, cb, w) in rows]


def _full_spec(p):
    nd = p.ndim
    return pl.BlockSpec(p.shape, lambda i, nd=nd: (0,) * nd)


def _rowwise(fn, rows, params, outs, *, name, tm):
    t = rows[0][0].shape[0]
    tm = min(tm, t)
    assert t % tm == 0
    nr, npar = len(rows), len(params)

    def body(*refs):
        vals = [r[...] for r in refs[:nr + npar]]
        res = fn(*vals)
        for o_ref, r in zip(refs[nr + npar:], res):
            o_ref[...] = r.astype(o_ref.dtype)

    return pl.pallas_call(
        body, name=name, grid=(t // tm,),
        in_specs=_row_specs(rows, tm) + [_full_spec(p) for p in params],
        out_specs=[pl.BlockSpec((tm, w), lambda i: (i, 0)) for (w, _) in outs],
        out_shape=[jax.ShapeDtypeStruct((t, w), dt) for (w, dt) in outs],
        compiler_params=_cp("parallel"),
    )(*[r[0] for r in rows], *params)


def _rowwise_bwd(fn, rows, params, n_const, cots, *, name, tm, row_grad, add_to=None):
    t = rows[0][0].shape[0]
    tm = min(tm, t)
    assert t % tm == 0
    nr, npar = len(rows), len(params)
    ndp = npar - n_const
    add_to = add_to or {}
    add_idx = sorted(add_to)
    flat_cots = [c for group in cots for c in group]
    kept = [i for i in range(nr) if row_grad[i] is not None]

    def body(*refs):
        pos = 0
        row_v = [r[...] for r in refs[pos:pos + nr]]; pos += nr
        par_v = [r[...] for r in refs[pos:pos + npar]]; pos += npar
        cot_v = [r[...] for r in refs[pos:pos + len(flat_cots)]]; pos += len(flat_cots)
        add_v = [r[...] for r in refs[pos:pos + len(add_idx)]]; pos += len(add_idx)
        rg_refs = refs[pos:pos + len(kept)]; pos += len(kept)
        pg_refs = refs[pos:pos + ndp]

        consts = par_v[ndp:]
        res, vjp = jax.vjp(lambda *args: tuple(fn(*args, *consts)), *row_v, *par_v[:ndp])
        cot_in, q = [], 0
        for j, group in enumerate(cots):
            c = None
            for _ in group:
                cv = cot_v[q].astype(F32); q += 1
                c = cv if c is None else c + cv
            c = jnp.zeros(res[j].shape, F32) if c is None else c
            cot_in.append(c.astype(res[j].dtype))
        grads = vjp(tuple(cot_in))
        for ref, i in zip(rg_refs, kept):
            g = grads[i].astype(F32)
            if i in add_to:
                g = g + add_v[add_idx.index(i)].astype(F32)
            ref[...] = g.astype(ref.dtype)

        @pl.when(pl.program_id(0) == 0)
        def _():
            for ref in pg_refs:
                ref[...] = jnp.zeros_like(ref)

        for ref, g in zip(pg_refs, grads[nr:]):
            ref[...] += g.astype(F32)

    cot_specs = [pl.BlockSpec((tm, c.shape[1]), lambda i: (i, 0)) for c in flat_cots]
    add_specs = [pl.BlockSpec((tm, add_to[i].shape[1]), lambda i_: (i_, 0)) for i in add_idx]
    out_specs = [pl.BlockSpec((tm, rows[i][2]), lambda i_: (i_, 0)) for i in kept] + [_full_spec(p) for p in params[:ndp]]
    out_shape = [jax.ShapeDtypeStruct((t, rows[i][2]), row_grad[i]) for i in kept] + [
        jax.ShapeDtypeStruct(p.shape, F32) for p in params[:ndp]]
    res = pl.pallas_call(
        body, name=name, grid=(t // tm,),
        in_specs=_row_specs(rows, tm) + [_full_spec(p) for p in params] + cot_specs + add_specs,
        out_specs=out_specs, out_shape=out_shape,
        compiler_params=_cp("arbitrary"),
    )(*[r[0] for r in rows], *params, *flat_cots, *[add_to[i] for i in add_idx])
    return list(res[:len(kept)]), list(res[len(kept):])


def _rms(x, g):
    xf = x.astype(F32)
    return xf * lax.rsqrt(jnp.mean(xf * xf, axis=-1, keepdims=True) + NORM_EPS) * g


def _softplus(x):
    return jnp.maximum(x, 0.0) + jnp.log(1.0 + jnp.exp(-jnp.abs(x)))


def _fn_pre(x, g):
    return (_rms(x, g).astype(BF16),)


def _fn_res(x, u, g_post):
    return (x + _rms(u, g_post),)


def _fn_res_pre(x, u, g_post, g_pre):
    xn = x + _rms(u, g_post)
    return xn, _rms(xn, g_pre).astype(BF16)


def _fn_mix(zga, zgb, ya, yb):
    return ((jax.nn.sigmoid(zga) * ya + jax.nn.sigmoid(zgb) * yb).astype(BF16),)


def _fn_swiglu(gate, up):
    return ((gate * jax.nn.sigmoid(gate) * up).astype(BF16),)


def _fn_prep(zk, zw, za, zg, decay_base, d_up, iclr_base, i_up, g_up, kns, kis, e_hd, e_dh):
    w_log = -_softplus(-(decay_base + _dot(jnp.tanh(zw), d_up))) - 0.5
    lw = -jnp.exp(w_log)
    a = jax.nn.sigmoid(iclr_base + _dot(za, i_up))
    g = _dot(jax.nn.sigmoid(zg), g_up)
    kn = zk * kns
    ss = _dot_split_a(kn * kn, e_dh)
    inv = lax.rsqrt(jnp.maximum(ss, 1e-24))
    kk = kn * _dot_split_a(inv, e_hd)
    k2 = zk * (1.0 + (a - 1.0) * kis)
    return lw, k2, kk, a, g


def _fn_post(y, r, k2, v, g, lnx_w, lnx_b, bonus, e_hd, e_dh):
    mu = _dot_split_a(_dot_split_a(y, e_dh) * (1.0 / HEAD), e_hd)
    yc = y - mu
    var = _dot_split_a(yc * yc, e_dh) * (1.0 / HEAD)
    yn = yc * _dot_split_a(lax.rsqrt(var + GROUP_NORM_EPS), e_hd)
    bs = _dot_split_a(_dot_split_a(r * k2 * bonus, e_dh), e_hd)
    return (((yn * lnx_w + lnx_b + bs * v) * g).astype(BF16),)


def _shift_fwd(p, col0, ncols, mix, seq, *, name, cw=256):
    t = p.shape[0]
    assert col0 % cw == 0 and ncols % cw == 0 and t % seq == 0
    cb0 = col0 // cw

    def body(p_ref, m_ref, z_ref):
        pv = p_ref[...]
        row = lax.broadcasted_iota(jnp.int32, pv.shape, 0)
        prev = jnp.where(row == 0, 0.0, pltpu.roll(pv, 1, axis=0))
        z_ref[...] = pv + (prev - pv) * m_ref[...]

    return pl.pallas_call(
        body, name=name, grid=(t // seq, ncols // cw),
        in_specs=[pl.BlockSpec((seq, cw), lambda b, c: (b, c + cb0)), pl.BlockSpec((1, cw), lambda b, c: (0, c))],
        out_specs=pl.BlockSpec((seq, cw), lambda b, c: (b, c)),
        out_shape=jax.ShapeDtypeStruct((t, ncols), F32),
        compiler_params=_cp("parallel", "parallel"),
    )(p, mix)


def _shift_bwd(p, col0, ncols, mix, dz_parts, seq, *, name, cw=256):
    t = p.shape[0]
    cb0 = col0 // cw
    n = len(dz_parts)

    def body(*refs):
        p_ref, m_ref = refs[:2]
        dp_ref, dm_ref = refs[2 + n:]
        dz = refs[2][...].astype(F32)
        for r in refs[3:2 + n]:
            dz = dz + r[...].astype(F32)
        pv = p_ref[...]
        mixv = m_ref[...]
        row = lax.broadcasted_iota(jnp.int32, pv.shape, 0)
        prev = jnp.where(row == 0, 0.0, pltpu.roll(pv, 1, axis=0))
        u = dz * mixv
        nxt = jnp.where(row == seq - 1, 0.0, pltpu.roll(u, seq - 1, axis=0))
        dp_ref[...] = (dz - u + nxt).astype(dp_ref.dtype)

        @pl.when(pl.program_id(1) == 0)
        def _():
            dm_ref[...] = jnp.zeros_like(dm_ref)

        dm_ref[...] += jnp.sum(dz * (prev - pv), axis=0, keepdims=True)

    return pl.pallas_call(
        body, name=name, grid=(ncols // cw, t // seq),
        in_specs=[pl.BlockSpec((seq, cw), lambda c, b: (b, c + cb0)), pl.BlockSpec((1, cw), lambda c, b: (0, c))]
        + [pl.BlockSpec((seq, cw), lambda c, b: (b, c))] * n,
        out_specs=[pl.BlockSpec((seq, cw), lambda c, b: (b, c)), pl.BlockSpec((1, cw), lambda c, b: (0, c))],
        out_shape=[jax.ShapeDtypeStruct((t, ncols), BF16), jax.ShapeDtypeStruct((1, ncols), F32)],
        compiler_params=_cp("parallel", "arbitrary"),
    )(p, mix, *dz_parts)


def _each(f, *lists):
    return [f(*xs) for xs in zip(*lists)]


def _tri_inv(low):
    c = low[0].shape[0]
    eye = (lax.broadcasted_iota(jnp.int32, (c, c), 0) == lax.broadcasted_iota(jnp.int32, (c, c), 1)).astype(F32)
    acc, pw = [eye - m for m in low], low
    for _ in range(int(math.log2(c)) - 1):
        pw = _each(_dot_hi, pw, pw)
        acc = _each(lambda p, q: _dot_hi(p, eye + q), acc, pw)
    return acc


def _wkv_chunk(s0, r, lw, k, v, kk, a):
    c = r[0].shape[0]
    ti = lax.broadcasted_iota(jnp.int32, (c, c), 0)
    si = lax.broadcasted_iota(jnp.int32, (c, c), 1)
    incl, strict = ti >= si, ti > si
    tri = incl.astype(F32)
    cum = _each(lambda x: _dot_split_b(tri, x, 3), lw)
    eg = _each(jnp.exp, cum)
    egp = _each(lambda cs, x: jnp.exp(cs - x), cum, lw)
    ei = _each(lambda cs: jnp.exp(-cs), cum)
    rh, kkh, kt = _each(jnp.multiply, r, eg), _each(jnp.multiply, kk, egp), _each(jnp.multiply, k, ei)
    bt = _each(lambda p, q, e: (p * q) * e, a, kk, ei)
    lb = _each(lambda p, q: jnp.where(strict, _dot_nt(p, q), 0.0), kkh, bt)
    lk = _each(lambda p, q: jnp.where(strict, _dot_nt(p, q), 0.0), kkh, kt)
    mb = _each(lambda p, q: jnp.where(incl, _dot_nt(p, q), 0.0), rh, bt)
    mk = _each(lambda p, q: jnp.where(incl, _dot_nt(p, q), 0.0), rh, kt)
    rhs = _each(lambda p, s, m, x: _dot_nt(p, s) + _dot(m, x), kkh, s0, lk, v)
    u = _each(lambda t, x: -_dot_hi(t, x), _tri_inv(lb), rhs)
    y = _each(lambda p, s, m1, uu, m2, x: _dot_nt(p, s) + _dot(m1, uu) + _dot(m2, x), rh, s0, mb, u, mk, v)
    s1 = _each(lambda s, uu, b, x, kq, w: (s + _dot_tn(uu, b) + _dot_tn(x, kq)) * jnp.exp(jnp.sum(w, axis=0, keepdims=True)),
               s0, u, bt, v, kt, lw)
    return y, s1


WKV_HEADS = 8
WKV_COLS = WKV_HEADS * HEAD
WKV_GROUPS = N_HEADS // WKV_HEADS


def _head_cols(ref):
    return [ref[:, h * HEAD:(h + 1) * HEAD] for h in range(ref.shape[1] // HEAD)]


def _wkv_specs(seq, rev):
    nc = seq // CHUNK

    def rows(col0):
        cb0 = col0 // WKV_COLS
        if rev:
            return pl.BlockSpec((CHUNK, WKV_COLS), lambda b, h, c: (b * nc + nc - 1 - c, cb0 + h))
        return pl.BlockSpec((CHUNK, WKV_COLS), lambda b, h, c: (b * nc + c, cb0 + h))

    if rev:
        st = pl.BlockSpec((1, 1, WKV_HEADS, HEAD, HEAD), lambda b, h, c: (b * WKV_GROUPS + h, nc - 1 - c, 0, 0, 0))
    else:
        st = pl.BlockSpec((1, 1, WKV_HEADS, HEAD, HEAD), lambda b, h, c: (b * WKV_GROUPS + h, c, 0, 0, 0))
    return rows, st


def _wkv_fwd(z_rkv, lw, k2, kk, a, seq):
    t = z_rkv.shape[0]
    nb, nc = t // seq, seq // CHUNK
    rows, st = _wkv_specs(seq, False)

    def body(r_ref, v_ref, lw_ref, k_ref, kk_ref, a_ref, y_ref, st_ref, s_scr):
        @pl.when(pl.program_id(2) == 0)
        def _():
            s_scr[...] = jnp.zeros_like(s_scr)

        s0 = [s_scr[h] for h in range(WKV_HEADS)]
        y, s1 = _wkv_chunk(s0, *[_head_cols(ref) for ref in (r_ref, lw_ref, k_ref, v_ref, kk_ref, a_ref)])
        for h in range(WKV_HEADS):
            st_ref[0, 0, h] = s0[h]
            y_ref[:, h * HEAD:(h + 1) * HEAD] = y[h]
            s_scr[h] = s1[h]

    return pl.pallas_call(
        body, name="wkv_fwd", grid=(nb, WKV_GROUPS, nc),
        in_specs=[rows(0), rows(2 * D), rows(0), rows(0), rows(0), rows(0)],
        out_specs=[rows(0), st],
        out_shape=[jax.ShapeDtypeStruct((t, D), F32),
                   jax.ShapeDtypeStruct((nb * WKV_GROUPS, nc, WKV_HEADS, HEAD, HEAD), F32)],
        scratch_shapes=[pltpu.VMEM((WKV_HEADS, HEAD, HEAD), F32)],
        compiler_params=_cp("parallel", "parallel", "arbitrary"),
    )(z_rkv, z_rkv, lw, k2, kk, a)


def _wkv_bwd(z_rkv, lw, k2, kk, a, states, dy, seq):
    t = z_rkv.shape[0]
    nb, nc = t // seq, seq // CHUNK
    rows, st = _wkv_specs(seq, True)

    def body(r_ref, v_ref, lw_ref, k_ref, kk_ref, a_ref, st_ref, dy_ref,
             dr_ref, dlw_ref, dk_ref, dv_ref, dkk_ref, da_ref, ds_scr):
        @pl.when(pl.program_id(2) == 0)
        def _():
            ds_scr[...] = jnp.zeros_like(ds_scr)

        s0 = [st_ref[0, 0, h] for h in range(WKV_HEADS)]
        _, vjp = jax.vjp(_wkv_chunk, s0, *[_head_cols(ref) for ref in (r_ref, lw_ref, k_ref, v_ref, kk_ref, a_ref)])
        grads = vjp(([x.astype(F32) for x in _head_cols(dy_ref)], [ds_scr[h] for h in range(WKV_HEADS)]))
        for h in range(WKV_HEADS):
            ds_scr[h] = grads[0][h]
            for ref, g in zip((dr_ref, dlw_ref, dk_ref, dv_ref, dkk_ref, da_ref), grads[1:]):
                ref[:, h * HEAD:(h + 1) * HEAD] = g[h]

    return pl.pallas_call(
        body, name="wkv_bwd", grid=(nb, WKV_GROUPS, nc),
        in_specs=[rows(0), rows(2 * D), rows(0), rows(0), rows(0), rows(0), st, rows(0)],
        out_specs=[rows(0)] * 6,
        out_shape=[jax.ShapeDtypeStruct((t, D), F32)] * 6,
        scratch_shapes=[pltpu.VMEM((WKV_HEADS, HEAD, HEAD), F32)],
        compiler_params=_cp("parallel", "parallel", "arbitrary"),
    )(z_rkv, z_rkv, lw, k2, kk, a, states, dy)


def _softmax(s):
    e = jnp.exp(s - jnp.max(s, axis=-1, keepdims=True))
    return e / jnp.sum(e, axis=-1, keepdims=True)


ATT_HEADS = 8
ATT_COLS = ATT_HEADS * HEAD
ATT_GROUPS = N_HEADS // ATT_HEADS


def _attn_chunk(q, kb, vb, bias, valid):
    s = _each(lambda x, y, z: jnp.where(valid, _dot_nt(x, y) * (HEAD ** -0.5) + z, MASK_VALUE), q, kb, bias)
    return _each(_dot, _each(_softmax, s), vb)


def _pad_fill(pad_ref, src_ref):
    pad_ref[0:LEFT, :] = jnp.zeros((LEFT, pad_ref.shape[1]), pad_ref.dtype)
    pad_ref[LEFT:, :] = src_ref[...].astype(pad_ref.dtype)


def _band_heads(pad_ref, start):
    return [pad_ref[pl.ds(start, BAND), h * HEAD:(h + 1) * HEAD].astype(F32) for h in range(ATT_HEADS)]


def _band_valid(c):
    return (c * CHUNK - LEFT + lax.broadcasted_iota(jnp.int32, (1, BAND), 1)) >= 0


def _attn_fwd(proj, bias, seq):
    t = proj.shape[0]
    nb, nc = t // seq, seq // CHUNK
    cq = C_Q // ATT_COLS

    def body(q_ref, k_ref, v_ref, b_ref, o_ref, kpad, vpad):
        c = pl.program_id(2)

        @pl.when(c == 0)
        def _():
            _pad_fill(kpad, k_ref)
            _pad_fill(vpad, v_ref)

        start = pl.multiple_of(c * CHUNK, CHUNK)
        o = _attn_chunk(_head_cols(q_ref), _band_heads(kpad, start), _band_heads(vpad, start),
                        [b_ref[h] for h in range(ATT_HEADS)], _band_valid(c))
        for h in range(ATT_HEADS):
            o_ref[:, h * HEAD:(h + 1) * HEAD] = o[h].astype(o_ref.dtype)

    return pl.pallas_call(
        body, name="attn_fwd", grid=(ATT_GROUPS, nb, nc),
        in_specs=[pl.BlockSpec((CHUNK, ATT_COLS), lambda h, b, c: (b * nc + c, cq + h)),
                  pl.BlockSpec((seq, ATT_COLS), lambda h, b, c: (b, cq + ATT_GROUPS + h)),
                  pl.BlockSpec((seq, ATT_COLS), lambda h, b, c: (b, cq + 2 * ATT_GROUPS + h)),
                  pl.BlockSpec((ATT_HEADS, CHUNK, BAND), lambda h, b, c: (h, 0, 0))],
        out_specs=pl.BlockSpec((CHUNK, ATT_COLS), lambda h, b, c: (b * nc + c, h)),
        out_shape=jax.ShapeDtypeStruct((t, D), BF16),
        scratch_shapes=[pltpu.VMEM((seq + LEFT, ATT_COLS), BF16)] * 2,
        compiler_params=_cp("parallel", "arbitrary", "arbitrary"),
    )(proj, proj, proj, bias)


def _attn_bwd(proj, bias, do, seq):
    t = proj.shape[0]
    nb, nc = t // seq, seq // CHUNK
    cq = C_Q // ATT_COLS

    def body(q_ref, k_ref, v_ref, b_ref, do_ref, dq_ref, dk_ref, dv_ref, db_ref, kpad, vpad, dkpad, dvpad):
        b, c = pl.program_id(1), pl.program_id(2)

        @pl.when(c == 0)
        def _():
            _pad_fill(kpad, k_ref)
            _pad_fill(vpad, v_ref)
            dkpad[...] = jnp.zeros_like(dkpad)
            dvpad[...] = jnp.zeros_like(dvpad)

        @pl.when(jnp.logical_and(b == 0, c == 0))
        def _():
            db_ref[...] = jnp.zeros_like(db_ref)

        start = pl.multiple_of(c * CHUNK, CHUNK)
        _, vjp = jax.vjp(functools.partial(_attn_chunk, valid=_band_valid(c)),
                         _head_cols(q_ref), _band_heads(kpad, start), _band_heads(vpad, start),
                         [b_ref[h] for h in range(ATT_HEADS)])
        dq, dkb, dvb, dbias = vjp([x.astype(F32) for x in _head_cols(do_ref)])
        for h in range(ATT_HEADS):
            sl = slice(h * HEAD, (h + 1) * HEAD)
            dq_ref[:, sl] = dq[h].astype(dq_ref.dtype)
            dkpad[pl.ds(start, BAND), sl] += dkb[h].astype(F32)
            dvpad[pl.ds(start, BAND), sl] += dvb[h].astype(F32)
            db_ref[h] += dbias[h]

        @pl.when(c == nc - 1)
        def _():
            dk_ref[...] = dkpad[LEFT:, :].astype(dk_ref.dtype)
            dv_ref[...] = dvpad[LEFT:, :].astype(dv_ref.dtype)

    kv_out = pl.BlockSpec((seq, ATT_COLS), lambda h, b, c: (b, h))
    return pl.pallas_call(
        body, name="attn_bwd", grid=(ATT_GROUPS, nb, nc),
        in_specs=[pl.BlockSpec((CHUNK, ATT_COLS), lambda h, b, c: (b * nc + c, cq + h)),
                  pl.BlockSpec((seq, ATT_COLS), lambda h, b, c: (b, cq + ATT_GROUPS + h)),
                  pl.BlockSpec((seq, ATT_COLS), lambda h, b, c: (b, cq + 2 * ATT_GROUPS + h)),
                  pl.BlockSpec((ATT_HEADS, CHUNK, BAND), lambda h, b, c: (h, 0, 0)),
                  pl.BlockSpec((CHUNK, ATT_COLS), lambda h, b, c: (b * nc + c, h))],
        out_specs=[pl.BlockSpec((CHUNK, ATT_COLS), lambda h, b, c: (b * nc + c, h)), kv_out, kv_out,
                   pl.BlockSpec((ATT_HEADS, CHUNK, BAND), lambda h, b, c: (h, 0, 0))],
        out_shape=[jax.ShapeDtypeStruct((t, D), BF16)] * 3 + [jax.ShapeDtypeStruct((N_HEADS, CHUNK, BAND), F32)],
        scratch_shapes=[pltpu.VMEM((seq + LEFT, ATT_COLS), BF16)] * 2 + [pltpu.VMEM((seq + LEFT, ATT_COLS), F32)] * 2,
        compiler_params=_cp("parallel", "arbitrary", "arbitrary"),
    )(proj, proj, proj, bias, do)


def _xattn_tile(q, k, v):
    s = _dot_nt(q, k) * ((MEM_WIDTH // MEM_HEADS) ** -0.5)
    return _dot(_softmax(s), v)


def _xattn_fwd(qm, kvm, seq, n_mem, tq=512):
    t = qm.shape[0]
    tq = min(tq, seq)
    nb, nq = t // seq, seq // tq

    def body(q_ref, k_ref, v_ref, o_ref):
        o_ref[...] = _xattn_tile(q_ref[...], k_ref[...], v_ref[...]).astype(o_ref.dtype)

    return pl.pallas_call(
        body, name="xattn_fwd", grid=(nb, MEM_HEADS, nq),
        in_specs=[pl.BlockSpec((tq, LANE), lambda b, h, i: (b * nq + i, h)),
                  pl.BlockSpec((n_mem, LANE), lambda b, h, i: (b, h)),
                  pl.BlockSpec((n_mem, LANE), lambda b, h, i: (b, MEM_HEADS + h))],
        out_specs=pl.BlockSpec((tq, LANE), lambda b, h, i: (b * nq + i, h)),
        out_shape=jax.ShapeDtypeStruct((t, MEM_WIDTH), BF16),
        compiler_params=_cp("parallel", "parallel", "parallel"),
    )(qm, kvm, kvm)


def _xattn_bwd(qm, kvm, do, seq, n_mem, tq=512):
    t = qm.shape[0]
    tq = min(tq, seq)
    nb, nq = t // seq, seq // tq

    def body(q_ref, k_ref, v_ref, do_ref, dq_ref, dkv_ref, dk_acc, dv_acc):
        i = pl.program_id(2)

        @pl.when(i == 0)
        def _():
            dk_acc[...] = jnp.zeros_like(dk_acc)
            dv_acc[...] = jnp.zeros_like(dv_acc)

        _, vjp = jax.vjp(_xattn_tile, q_ref[...], k_ref[...], v_ref[...])
        dq, dk, dv = vjp(do_ref[...].astype(F32))
        dq_ref[...] = dq.astype(dq_ref.dtype)
        dk_acc[...] += dk
        dv_acc[...] += dv

        @pl.when(i == nq - 1)
        def _():
            dkv_ref[0] = dk_acc[...].astype(dkv_ref.dtype)
            dkv_ref[1] = dv_acc[...].astype(dkv_ref.dtype)

    dq, dkv = pl.pallas_call(
        body, name="xattn_bwd", grid=(nb, MEM_HEADS, nq),
        in_specs=[pl.BlockSpec((tq, LANE), lambda b, h, i: (b * nq + i, h)),
                  pl.BlockSpec((n_mem, LANE), lambda b, h, i: (b, h)),
                  pl.BlockSpec((n_mem, LANE), lambda b, h, i: (b, MEM_HEADS + h)),
                  pl.BlockSpec((tq, LANE), lambda b, h, i: (b * nq + i, h))],
        out_specs=[pl.BlockSpec((tq, LANE), lambda b, h, i: (b * nq + i, h)),
                   pl.BlockSpec((2, n_mem, LANE), lambda b, h, i: (0, b, h))],
        out_shape=[jax.ShapeDtypeStruct((t, MEM_WIDTH), BF16), jax.ShapeDtypeStruct((2, nb * n_mem, MEM_WIDTH), BF16)],
        scratch_shapes=[pltpu.VMEM((n_mem, LANE), F32)] * 2,
        compiler_params=_cp("parallel", "parallel", "arbitrary"),
    )(qm, kvm, kvm, do)
    return dq, jnp.concatenate([dkv[0], dkv[1]], axis=1)


def _loss_head(y, target, tm=512):
    t, d = y.shape
    tm = min(tm, t)

    def body(y_ref, t_ref, dy_ref, l_ref):
        @pl.when(pl.program_id(0) == 0)
        def _():
            l_ref[...] = jnp.zeros_like(l_ref)

        diff = y_ref[...] - t_ref[...]
        dy_ref[...] = diff * (1.0 / d)
        l_ref[...] += 0.5 * jnp.sum(jnp.mean(diff * diff, axis=-1, keepdims=True), axis=0, keepdims=True)

    dy, loss = pl.pallas_call(
        body, name="loss_head", grid=(t // tm,),
        in_specs=[pl.BlockSpec((tm, d), lambda i: (i, 0))] * 2,
        out_specs=[pl.BlockSpec((tm, d), lambda i: (i, 0)), pl.BlockSpec((8, LANE), lambda i: (0, 0))],
        out_shape=[jax.ShapeDtypeStruct((t, d), F32), jax.ShapeDtypeStruct((8, LANE), F32)],
        compiler_params=_cp("arbitrary"),
    )(y, target)
    return dy, loss


def _mesh_pos():
    return lax.axis_index("x"), lax.axis_index("y"), lax.axis_index("c")


def _peer(pos, d):
    x, y, c = pos
    return ((1 - x) if d & 4 else x, (1 - y) if d & 2 else y, (1 - c) if d & 1 else c)


def _flat(pos):
    return 4 * pos[0] + 2 * pos[1] + pos[2]


def _exchange(arrays, scatter, *, name):
    n = len(arrays)
    shapes = [a.shape[1:] if scatter else a.shape for a in arrays]

    def body(*refs):
        ins, outs = refs[:n], refs[n:2 * n]
        send, recv, loc = refs[2 * n:]
        pos = _mesh_pos()
        me = _flat(pos)
        pending = []
        for i in range(n):
            own = pltpu.make_async_copy(ins[i].at[me] if scatter else ins[i], outs[i].at[me], loc.at[i])
            own.start()
            pending.append(own)
            for d in range(1, N_DEV):
                peer = _peer(pos, d)
                src = ins[i].at[_flat(peer)] if scatter else ins[i]
                out_cp = pltpu.make_async_remote_copy(
                    src_ref=src, dst_ref=outs[i].at[me], send_sem=send.at[i, d - 1], recv_sem=recv.at[i, d - 1],
                    device_id=peer, device_id_type=pl.DeviceIdType.MESH)
                out_cp.start()
                pending.append(out_cp)
        for i in range(n):
            own = pending[i * N_DEV]
            for d in range(1, N_DEV):
                peer = _peer(pos, d)
                src = ins[i].at[_flat(peer)] if scatter else ins[i]
                pending[i * N_DEV + d].wait_send()
                pltpu.make_async_remote_copy(
                    src_ref=src, dst_ref=outs[i].at[_flat(peer)], send_sem=send.at[i, d - 1], recv_sem=recv.at[i, d - 1],
                    device_id=peer, device_id_type=pl.DeviceIdType.MESH).wait_recv()
            own.wait()

    hbm = pl.BlockSpec(memory_space=pltpu.HBM)
    return pl.pallas_call(
        body, name=name,
        in_specs=[hbm] * n, out_specs=[hbm] * n,
        out_shape=[jax.ShapeDtypeStruct((N_DEV,) + tuple(s), a.dtype) for s, a in zip(shapes, arrays)],
        scratch_shapes=[pltpu.SemaphoreType.DMA((n, N_DEV - 1)), pltpu.SemaphoreType.DMA((n, N_DEV - 1)),
                        pltpu.SemaphoreType.DMA((n,))],
    )(*arrays)


_HBM = pl.BlockSpec(memory_space=pltpu.HBM)
_SEM = pl.BlockSpec(memory_space=pltpu.SEMAPHORE)
_DATAFLOW = pltpu.SideEffectType.DATAFLOW_SIDE_EFFECTING


def _remote_copies(ins, lands, send, recv, scatter):
    pos = _mesh_pos()
    me = _flat(pos)
    out = []
    for i in range(len(ins)):
        for d in range(1, N_DEV):
            peer = _peer(pos, d)
            src = ins[i].at[_flat(peer)] if scatter else ins[i]
            pair = i * (N_DEV - 1) + d - 1
            sems = dict(send_sem=send.at[pair], recv_sem=recv.at[pair], device_id=peer,
                        device_id_type=pl.DeviceIdType.MESH)
            out.append((pltpu.make_async_remote_copy(src_ref=src, dst_ref=lands[i].at[me], **sems),
                        pltpu.make_async_remote_copy(src_ref=src, dst_ref=lands[i].at[_flat(peer)], **sems)))
    return out


def _exchange_start(arrays, scatter, after, *, name):
    n = len(arrays)
    shapes = [a.shape[1:] if scatter else a.shape for a in arrays]
    lands = [pltpu.with_memory_space_constraint(lax.empty((N_DEV,) + tuple(s), a.dtype), pltpu.HBM)
             for s, a in zip(shapes, arrays)]
    srcs = [pltpu.with_memory_space_constraint(a, pltpu.HBM) for a in arrays]

    def body(*refs):
        ins, land_refs = refs[:n], refs[n:2 * n]
        send, recv, token = refs[2 * n + 1], refs[2 * n + 2], refs[-1]
        for going, _ in _remote_copies(ins, land_refs, send, recv, scatter):
            going.start()
        token[...] = jnp.zeros_like(token)

    sems = pltpu.SemaphoreType.DMA((n * (N_DEV - 1),))
    res = pl.pallas_call(
        body, name=name,
        out_shape=(sems, sems, *[pltpu.HBM(a.shape, a.dtype) for a in srcs + lands], jax.ShapeDtypeStruct((8, LANE), F32)),
        in_specs=[_HBM] * (2 * n) + [pl.BlockSpec(memory_space=pl.ANY)],
        out_specs=(_SEM, _SEM, *[_HBM] * (2 * n), pl.BlockSpec(memory_space=pltpu.VMEM)),
        input_output_aliases={i: 2 + i for i in range(2 * n)},
        compiler_params=pltpu.CompilerParams(has_side_effects=_DATAFLOW),
    )(*srcs, *lands, after)
    return (n, scatter, res[0], res[1], list(res[2:2 + 2 * n])), res[-1]


def _exchange_wait(handle, after, own, *, name):
    n, scatter, send, recv, thru = handle

    def body(*refs):
        ins, land_refs = refs[:n], refs[n:2 * n]
        for going, coming in _remote_copies(ins, land_refs, refs[2 * n], refs[2 * n + 1], scatter):
            going.wait_send()
            coming.wait_recv()

    res = pl.pallas_call(
        body, name=name,
        out_shape=tuple(pltpu.HBM(a.shape, a.dtype) for a in thru),
        in_specs=[_HBM] * (2 * n) + [_SEM, _SEM] + [pl.BlockSpec(memory_space=pl.ANY)] * len(after),
        out_specs=tuple([_HBM] * (2 * n)),
        input_output_aliases={i: i for i in range(2 * n)},
        compiler_params=pltpu.CompilerParams(has_side_effects=_DATAFLOW),
    )(*thru, send, recv, *after)
    me = _flat(_mesh_pos())
    return [lax.dynamic_update_slice_in_dim(land, o[None].astype(land.dtype), me, 0) for land, o in zip(res[n:], own)]


def _adamw(parts, w, m, v, *, name, tr=128):
    r, c = w.shape
    align = 8 * 4 // parts.dtype.itemsize
    tr = max(d for d in range(align, min(tr, r) + 1, align) if r % d == 0)

    def body(p_ref, w_ref, m_ref, v_ref, g_ref, d_ref, nm_ref, nv_ref):
        g = p_ref[0].astype(F32)
        for j in range(1, N_DEV):
            g = g + p_ref[j].astype(F32)
        m2 = ADAM_B1 * m_ref[...] + (1.0 - ADAM_B1) * g
        v2 = ADAM_B2 * v_ref[...] + (1.0 - ADAM_B2) * (g * g)
        m_hat = m2 / (1.0 - ADAM_B1 ** ADAM_STEP)
        v_hat = v2 / (1.0 - ADAM_B2 ** ADAM_STEP)
        g_ref[...] = g
        d_ref[...] = -ADAM_LR * (m_hat / (jnp.sqrt(v_hat) + ADAM_EPS) + ADAM_WD * w_ref[...])
        nm_ref[...] = m2
        nv_ref[...] = v2

    spec = pl.BlockSpec((tr, c), lambda i: (i, 0))
    return pl.pallas_call(
        body, name=name, grid=(r // tr,),
        in_specs=[pl.BlockSpec((N_DEV, tr, c), lambda i: (0, i, 0)), spec, spec, spec],
        out_specs=[spec] * 4, out_shape=[jax.ShapeDtypeStruct((r, c), F32)] * 4,
        compiler_params=_cp("parallel"),
    )(parts, w, m, v)


def _cols_to_full(g):
    return jnp.transpose(g, (1, 0, 2)).reshape(g.shape[1], N_DEV * g.shape[2])


def _full_to_cols(w):
    r, c = w.shape
    return jnp.transpose(w.reshape(r, N_DEV, c // N_DEV), (1, 0, 2))


def _pad_cols(a, width):
    return jnp.pad(a, ((0, 0), (0, width - a.shape[1])))


def _pad_lora(w):
    return jnp.concatenate([
        _pad_cols(w[:, :LORA_W], 128), _pad_cols(w[:, LORA_W:LORA_W + LORA_A], 128),
        _pad_cols(w[:, LORA_W + LORA_A:], 256)], axis=1)


def _unpad_lora(wp):
    return jnp.concatenate([wp[:, :LORA_W], wp[:, 128:128 + LORA_A], wp[:, 256:256 + LORA_G]], axis=1)


def _permute_in(w):
    rk = 3 * D
    lo = rk + LORA_W + LORA_A + LORA_G
    return jnp.concatenate([w[:, :rk], w[:, lo:], _pad_lora(w[:, rk:lo])], axis=1)


def _unpermute_in(wp):
    return jnp.concatenate([wp[:, :3 * D], _unpad_lora(wp[:, C_LORA:]), wp[:, 3 * D:C_LORA]], axis=1)


def _rel_index():
    dist = jnp.arange(CHUNK)[:, None] - jnp.arange(BAND)[None, :] + LEFT
    return (jnp.minimum(dist, REL_CLIP) + (CHUNK - 1)).reshape(-1)


def _local_step(x, mem, target, wt, seq, n_mem, comm):
    t = x.shape[0]
    row = lambda a: a.reshape(1, -1).astype(F32)
    g_pre_mix, g_post_mix = row(wt["g_pre_mix"]), row(wt["g_post_mix"])
    g_pre_cross, g_post_cross, g_mem = row(wt["g_pre_cross"]), row(wt["g_post_cross"]), row(wt["g_mem"])
    g_pre_ffn, g_post_ffn = row(wt["g_pre_ffn"]), row(wt["g_post_ffn"])
    w_in = wt["w_in_p"]
    mix = row(wt["shift_mix"])
    mix_rkv, mix_lora = mix[:, :3 * D], _pad_lora(mix[:, 3 * D:])
    d_up = jnp.pad(wt["decay_up"].astype(F32), ((0, 128 - LORA_W), (0, 0)))
    i_up = jnp.pad(wt["iclr_up"].astype(F32), ((0, 128 - LORA_A), (0, 0)))
    g_up = jnp.pad(wt["gate_up"].astype(F32), ((0, 256 - LORA_G), (0, 0)))
    decay_base, iclr_base = row(wt["decay_base"]), row(wt["iclr_base"])
    kns, kis = row(wt["key_norm_scale"]), row(wt["key_iclr_scale"])
    lnx_w, lnx_b, bonus = row(wt["lnx_w"]), row(wt["lnx_b"]), row(wt["bonus_scale"])
    e_dh = (jnp.arange(D)[:, None] // HEAD == jnp.arange(N_HEADS)[None, :]).astype(F32)
    e_hd = e_dh.T
    onehot = (jnp.arange(REL_TABLE)[:, None] == _rel_index()[None, :]).astype(BF16)

    (h1,) = _rowwise(_fn_pre, [_win(x)], [g_pre_mix], [(D, BF16)], name="pre_mix", tm=512)
    proj = _mm(h1, w_in, name="mm_in", after=comm.first_token)
    z_rkv = _shift_fwd(proj, 0, 3 * D, mix_rkv, seq, name="shift_rkv")
    z_lora = _shift_fwd(proj, C_LORA, 512, mix_lora, seq, name="shift_lora")
    prep_rows = [_win(z_rkv, D, D), _win(z_lora, 0, 128), _win(z_lora, 128, 128), _win(z_lora, 256, 256)]
    prep_params = [decay_base, d_up, iclr_base, i_up, g_up, kns, kis, e_hd, e_dh]
    lw, k2, kk, a, g = _rowwise(_fn_prep, prep_rows, prep_params, [(D, F32)] * 5, name="rwkv_prep", tm=256)
    y, states = _wkv_fwd(z_rkv, lw, k2, kk, a, seq)
    post_rows = [_win(y), _win(z_rkv, 0, D), _win(k2), _win(z_rkv, 2 * D, D), _win(g)]
    post_params = [lnx_w, lnx_b, bonus, e_hd, e_dh]
    (y_a,) = _rowwise(_fn_post, post_rows, post_params, [(D, BF16)], name="rwkv_post", tm=256)
    bias = _mm(wt["rel_bias"].astype(F32), onehot, name="mm_bias", split_a=3).reshape(N_HEADS, CHUNK, BAND)
    y_b = _attn_fwd(proj, bias, seq)
    wt = {**wt, **comm.late_weights(y_b)}
    ya_p = _mm(y_a, wt["w_branch_a"], name="mm_a")
    yb_p = _mm(y_b, wt["w_branch_b"], name="mm_b")
    mix_rows = [_win(proj, C_GA, D), _win(proj, C_GA + D, D), _win(ya_p), _win(yb_p)]
    (mixed,) = _rowwise(_fn_mix, mix_rows, [], [(D, BF16)], name="gate_mix", tm=512)
    mo = _mm(mixed, wt["w_out"], name="mm_out")
    x1, h2 = _rowwise(_fn_res_pre, [_win(x), _win(mo)], [g_post_mix, g_pre_cross], [(D, F32), (D, BF16)],
                      name="res_mix", tm=512)
    qm = _mm(h2, wt["w_q_mem"], name="mm_q")
    (mn,) = _rowwise(_fn_pre, [_win(mem)], [g_mem], [(D, BF16)], name="pre_mem", tm=512)
    kvm = _mm(mn, wt["w_kv_mem"], name="mm_kv")
    om = _xattn_fwd(qm, kvm, seq, n_mem)
    co = _mm(om, wt["w_o_mem"], name="mm_o")
    x2, h3 = _rowwise(_fn_res_pre, [_win(x1), _win(co)], [g_post_cross, g_pre_ffn], [(D, F32), (D, BF16)],
                      name="res_cross", tm=512)
    gu = _mm(h3, wt["w_ffn_in"], name="mm_ffn_in")
    (act,) = _rowwise(_fn_swiglu, [_win(gu, 0, FFN), _win(gu, FFN, FFN)], [], [(FFN, BF16)], name="swiglu", tm=256)
    ff = _mm(act, wt["w_ffn_out"], name="mm_ffn_out")
    (x3,) = _rowwise(_fn_res, [_win(x2), _win(ff)], [g_post_ffn], [(D, F32)], name="res_ffn", tm=512)
    dx3, loss = _loss_head(x3, target)

    gw = {}
    (dx2, dff), (gw["g_post_ffn"],) = _rowwise_bwd(
        _fn_res, [_win(x2), _win(ff)], [g_post_ffn], 0, [[dx3]], name="res_ffn_bwd", tm=256, row_grad=[F32, BF16])
    dact = _mm(dff, wt["w_ffn_out"], tb=True, name="mm_ffn_out_dx", out_dtype=BF16)
    gw["w_ffn_out"] = _mm(act, dff, ta=True, name="mm_ffn_out_dw")
    (dgate, dup), _ = _rowwise_bwd(_fn_swiglu, [_win(gu, 0, FFN), _win(gu, FFN, FFN)], [], 0, [[dact]],
                                   name="swiglu_bwd", tm=256, row_grad=[BF16, BF16])
    dgu = jnp.concatenate([dgate, dup], axis=1)
    dh3 = _mm(dgu, wt["w_ffn_in"], tb=True, name="mm_ffn_in_dx", out_dtype=BF16)
    gw["w_ffn_in"] = _mm(h3, dgu, ta=True, name="mm_ffn_in_dw")
    (dx1, dco), (gw["g_post_cross"], gw["g_pre_ffn"]) = _rowwise_bwd(
        _fn_res_pre, [_win(x1), _win(co)], [g_post_cross, g_pre_ffn], 0, [[dx2], [dh3]],
        name="res_cross_bwd", tm=256, row_grad=[F32, BF16])
    dom = _mm(dco, wt["w_o_mem"], tb=True, name="mm_o_dx", out_dtype=BF16)
    gw["w_o_mem"] = _mm(om, dco, ta=True, name="mm_o_dw")
    dqm, dkvm = _xattn_bwd(qm, kvm, dom, seq, n_mem)
    dh2 = _mm(dqm, wt["w_q_mem"], tb=True, name="mm_q_dx", out_dtype=BF16)
    gw["w_q_mem"] = _mm(h2, dqm, ta=True, name="mm_q_dw")
    dmn = _mm(dkvm, wt["w_kv_mem"], tb=True, name="mm_kv_dx", out_dtype=BF16)
    gw["w_kv_mem"] = _mm(mn, dkvm, ta=True, name="mm_kv_dw")
    _, (gw["g_mem"],) = _rowwise_bwd(_fn_pre, [_win(mem)], [g_mem], 0, [[dmn]], name="pre_mem_bwd", tm=256,
                                     row_grad=[None])
    (dx0, dmo), (gw["g_post_mix"], gw["g_pre_cross"]) = _rowwise_bwd(
        _fn_res_pre, [_win(x), _win(mo)], [g_post_mix, g_pre_cross], 0, [[dx1], [dh2]],
        name="res_mix_bwd", tm=256, row_grad=[F32, BF16])
    dmixed = _mm(dmo, wt["w_out"], tb=True, name="mm_out_dx", out_dtype=BF16)
    gw["w_out"] = _mm(mixed, dmo, ta=True, name="mm_out_dw")
    (dzga, dzgb, dya_p, dyb_p), _ = _rowwise_bwd(_fn_mix, mix_rows, [], 0, [[dmixed]], name="gate_mix_bwd", tm=256,
                                                 row_grad=[BF16] * 4)
    gw["w_branch_a"] = _mm(y_a, dya_p, ta=True, name="mm_a_dw")
    gw["w_branch_b"] = _mm(y_b, dyb_p, ta=True, name="mm_b_dw")
    token = comm.send_early(gw)
    dy_a = _mm(dya_p, wt["w_branch_a"], tb=True, name="mm_a_dx", out_dtype=BF16, after=token)
    dy_b = _mm(dyb_p, wt["w_branch_b"], tb=True, name="mm_b_dx", out_dtype=BF16, after=token)
    dq, dk, dv, dbias = _attn_bwd(proj, bias, dy_b, seq)
    gw["rel_bias"] = _mm(dbias.reshape(N_HEADS, CHUNK * BAND), onehot, tb=True, name="mm_bias_dw", split_a=2)
    (dy, dr_p, dk2_p, dv_p, dg), (gw["lnx_w"], gw["lnx_b"], gw["bonus_scale"]) = _rowwise_bwd(
        _fn_post, post_rows, post_params, 2, [[dy_a]], name="rwkv_post_bwd", tm=128, row_grad=[F32] * 5)
    dr_s, dlw, dk2_s, dv_s, dkk, da = _wkv_bwd(z_rkv, lw, k2, kk, a, states, dy, seq)
    (dzk, dzw, dza, dzg), pg = _rowwise_bwd(
        _fn_prep, prep_rows, prep_params, 2, [[dlw], [dk2_p, dk2_s], [dkk], [da], [dg]],
        name="rwkv_prep_bwd", tm=128, row_grad=[F32] * 4)
    gw["decay_base"], gd_up, gw["iclr_base"], gi_up, gg_up, gw["key_norm_scale"], gw["key_iclr_scale"] = pg
    gw["decay_up"], gw["iclr_up"], gw["gate_up"] = gd_up[:LORA_W], gi_up[:LORA_A], gg_up[:LORA_G]
    dp_r, gmix_r = _shift_bwd(proj, 0, D, mix_rkv[:, :D], [dr_p, dr_s], seq, name="shift_r_bwd")
    dp_k, gmix_k = _shift_bwd(proj, D, D, mix_rkv[:, D:2 * D], [dzk], seq, name="shift_k_bwd")
    dp_v, gmix_v = _shift_bwd(proj, 2 * D, D, mix_rkv[:, 2 * D:], [dv_p, dv_s], seq, name="shift_v_bwd")
    dp_lora, gmix_lora = _shift_bwd(proj, C_LORA, 512, mix_lora, [jnp.concatenate([dzw, dza, dzg], axis=1)], seq,
                                    name="shift_lora_bwd")
    gw["shift_mix"] = jnp.concatenate([gmix_r, gmix_k, gmix_v, _unpad_lora(gmix_lora)], axis=1)
    dproj = jnp.concatenate([dp_r, dp_k, dp_v, dq, dk, dv, dzga, dzgb, dp_lora], axis=1)
    gw["w_in_p"] = _mm(h1, dproj, ta=True, name="mm_in_dw")
    token = comm.send_late(gw)
    dh1 = _mm(dproj, w_in, tb=True, name="mm_in_dx", out_dtype=BF16, after=token)
    (grad_x,), (gw["g_pre_mix"],) = _rowwise_bwd(_fn_pre, [_win(x)], [g_pre_mix], 0, [[dh1]], name="pre_mix_bwd",
                                                 tm=256, row_grad=[F32], add_to={0: dx0})
    return loss, grad_x, gw


_COL_SHARDED = ("w_in", "decay_up", "iclr_up", "gate_up", "w_o_mem", "w_ffn_in")
_ROW_SHARDED = ("w_branch_a", "w_branch_b", "w_out", "w_q_mem", "w_kv_mem", "w_ffn_out")
_FIRST = ("w_in", "decay_up", "iclr_up", "gate_up")
_REST = ("w_o_mem", "w_ffn_in", "w_branch_a", "w_branch_b", "w_out", "w_q_mem", "w_kv_mem", "w_ffn_out")
_REPLICATED = ("g_pre_mix", "g_post_mix", "shift_mix", "decay_base", "iclr_base", "key_norm_scale", "key_iclr_scale",
               "bonus_scale", "lnx_w", "lnx_b", "rel_bias", "g_pre_cross", "g_post_cross", "g_mem", "g_pre_ffn",
               "g_post_ffn")
_WEIGHTS = ("g_pre_mix", "g_post_mix", "w_in", "shift_mix", "decay_base", "decay_up", "iclr_base", "iclr_up", "gate_up",
            "key_norm_scale", "key_iclr_scale", "bonus_scale", "lnx_w", "lnx_b", "rel_bias", "w_branch_a", "w_branch_b",
            "w_out", "g_pre_cross", "g_post_cross", "g_mem", "w_q_mem", "w_kv_mem", "w_o_mem", "g_pre_ffn", "g_post_ffn",
            "w_ffn_in", "w_ffn_out")
_PACK_ROWS = 8 * ((sum({"shift_mix": 3360, "bonus_scale": 1024, "rel_bias": 3072}.get(n, D) for n in _REPLICATED)
                   + 1 + 8 * LANE - 1) // (8 * LANE))


def _pack(vals):
    flat = jnp.concatenate([v.reshape(-1).astype(F32) for v in vals])
    return jnp.pad(flat, (0, _PACK_ROWS * LANE - flat.shape[0])).reshape(_PACK_ROWS, LANE)


def _unpack(packed, shapes):
    flat, out, pos = packed.reshape(-1), [], 0
    for s in shapes:
        n = math.prod(s)
        out.append(flat[pos:pos + n].reshape(s))
        pos += n
    return out


def _step(args, seq, n_mem):
    names = ("x", "mem") + _WEIGHTS + ("loss_target",) + tuple("m_" + n for n in _WEIGHTS) + tuple("v_" + n for n in _WEIGHTS)
    given = dict(zip(names, args))
    nb = given["x"].shape[0]
    x = given["x"].reshape(nb * seq, D)
    mem = given["mem"].reshape(nb * n_mem, D)
    target = given["loss_target"].reshape(nb * seq, D)
    shard = {n: given[n][0] for n in _COL_SHARDED + _ROW_SHARDED}
    out = {}

    def full(name, g):
        return _cols_to_full(g) if name in _COL_SHARDED else g.reshape(-1, g.shape[-1])

    def blocks_of(name, g):
        return (_full_to_cols(g) if name in _COL_SHARDED else g.reshape((N_DEV,) + shard[name].shape)).astype(BF16)

    def update(names, landed):
        res = None
        for n, parts in zip(names, landed):
            res = _adamw(parts, shard[n], given["m_" + n][0], given["v_" + n][0], name="adamw_" + n)
            for kind, r in zip(("grad_", "delta_", "new_m_", "new_v_"), res):
                out[kind + n] = r[None]
        return res[0]

    class Exchanges:
        def __init__(self):
            first = _exchange([shard[n].astype(BF16) for n in _FIRST], False, name="gather_first")
            self.first = {n: full(n, g) for n, g in zip(_FIRST, first)}
            self.rest, self.first_token = _exchange_start(
                [shard[n].astype(BF16) for n in _REST], False, first[0], name="gather_rest_start")

        def late_weights(self, after):
            got = _exchange_wait(self.rest, [after], [shard[n] for n in _REST], name="gather_rest_wait")
            return {n: full(n, g) for n, g in zip(_REST, got)}

        def send_early(self, gw):
            self.early_blocks = [blocks_of(n, gw[n]) for n in _REST]
            self.early, token = _exchange_start(self.early_blocks, True, self.early_blocks[-1], name="scatter_rest_start")
            return token

        def send_late(self, gw):
            me = _flat(_mesh_pos())
            own = [lax.dynamic_index_in_dim(b, me, 0, keepdims=False) for b in self.early_blocks]
            landed = _exchange_wait(self.early, [gw["w_in_p"]], own, name="scatter_rest_wait")
            self.updated = update(_REST, landed)
            grads = {**gw, "w_in": _unpermute_in(gw["w_in_p"])}
            self.late_blocks = [blocks_of(n, grads[n]) for n in _FIRST]
            self.late, token = _exchange_start(self.late_blocks, True, landed[0], name="scatter_first_start")
            return token

        def finish(self, after):
            me = _flat(_mesh_pos())
            own = [lax.dynamic_index_in_dim(b, me, 0, keepdims=False) for b in self.late_blocks]
            update(_FIRST, _exchange_wait(self.late, [after, self.updated], own, name="scatter_first_wait"))

    comm = Exchanges()
    wt = {n: given[n][0] for n in _REPLICATED}
    wt.update({n: comm.first[n] for n in _FIRST[1:]})
    wt["w_in_p"] = _permute_in(comm.first["w_in"])
    loss_tile, grad_x, gw = _local_step(x, mem, target, wt, seq, n_mem, comm)
    comm.finish(grad_x)
    rep_shapes = [given[n].shape for n in _REPLICATED]
    small = _exchange([_pack([gw[n] for n in _REPLICATED] + [loss_tile[0, 0]])], False, name="gather_small")[0]
    zero = jnp.zeros((), F32)
    res = _adamw(small, *[_pack([given[p + n] for n in _REPLICATED] + [zero]) for p in ("", "m_", "v_")],
                 name="adamw_small", tr=_PACK_ROWS)
    for kind, r in zip(("grad_", "delta_", "new_m_", "new_v_"), res):
        for n, val in zip(_REPLICATED, _unpack(r, rep_shapes)):
            out[kind + n] = val
    loss = res[0].reshape(-1)[sum(math.prod(s) for s in rep_shapes)]
    grad_x = grad_x.reshape(nb, seq, D)
    return (loss, grad_x, *[out[k + n] for k in ("grad_", "delta_", "new_m_", "new_v_") for n in _WEIGHTS])


def kernel(x, mem, g_pre_mix, g_post_mix, w_in, shift_mix, decay_base, decay_up, iclr_base, iclr_up, gate_up, key_norm_scale, key_iclr_scale, bonus_scale, lnx_w, lnx_b, rel_bias, w_branch_a, w_branch_b, w_out, g_pre_cross, g_post_cross, g_mem, w_q_mem, w_kv_mem, w_o_mem, g_pre_ffn, g_post_ffn, w_ffn_in, w_ffn_out, loss_target, m_g_pre_mix, m_g_post_mix, m_w_in, m_shift_mix, m_decay_base, m_decay_up, m_iclr_base, m_iclr_up, m_gate_up, m_key_norm_scale, m_key_iclr_scale, m_bonus_scale, m_lnx_w, m_lnx_b, m_rel_bias, m_w_branch_a, m_w_branch_b, m_w_out, m_g_pre_cross, m_g_post_cross, m_g_mem, m_w_q_mem, m_w_kv_mem, m_w_o_mem, m_g_pre_ffn, m_g_post_ffn, m_w_ffn_in, m_w_ffn_out, v_g_pre_mix, v_g_post_mix, v_w_in, v_shift_mix, v_decay_base, v_decay_up, v_iclr_base, v_iclr_up, v_gate_up, v_key_norm_scale, v_key_iclr_scale, v_bonus_scale, v_lnx_w, v_lnx_b, v_rel_bias, v_w_branch_a, v_w_branch_b, v_w_out, v_g_pre_cross, v_g_post_cross, v_g_mem, v_w_q_mem, v_w_kv_mem, v_w_o_mem, v_g_pre_ffn, v_g_post_ffn, v_w_ffn_in, v_w_ffn_out):
    args = (x, mem, g_pre_mix, g_post_mix, w_in, shift_mix, decay_base, decay_up, iclr_base, iclr_up, gate_up, key_norm_scale, key_iclr_scale, bonus_scale, lnx_w, lnx_b, rel_bias, w_branch_a, w_branch_b, w_out, g_pre_cross, g_post_cross, g_mem, w_q_mem, w_kv_mem, w_o_mem, g_pre_ffn, g_post_ffn, w_ffn_in, w_ffn_out, loss_target, m_g_pre_mix, m_g_post_mix, m_w_in, m_shift_mix, m_decay_base, m_decay_up, m_iclr_base, m_iclr_up, m_gate_up, m_key_norm_scale, m_key_iclr_scale, m_bonus_scale, m_lnx_w, m_lnx_b, m_rel_bias, m_w_branch_a, m_w_branch_b, m_w_out, m_g_pre_cross, m_g_post_cross, m_g_mem, m_w_q_mem, m_w_kv_mem, m_w_o_mem, m_g_pre_ffn, m_g_post_ffn, m_w_ffn_in, m_w_ffn_out, v_g_pre_mix, v_g_post_mix, v_w_in, v_shift_mix, v_decay_base, v_decay_up, v_iclr_base, v_iclr_up, v_gate_up, v_key_norm_scale, v_key_iclr_scale, v_bonus_scale, v_lnx_w, v_lnx_b, v_rel_bias, v_w_branch_a, v_w_branch_b, v_w_out, v_g_pre_cross, v_g_post_cross, v_g_mem, v_w_q_mem, v_w_kv_mem, v_w_o_mem, v_g_pre_ffn, v_g_post_ffn, v_w_ffn_in, v_w_ffn_out)
    return _step(args, x.shape[1], mem.shape[1])
```

```python
import functools
import math

import jax
import jax.numpy as jnp
from jax import lax
from jax.experimental import pallas as pl
from jax.experimental.pallas import tpu as pltpu

F32 = jnp.float32
BF16 = jnp.bfloat16

N_DEV = 8
D = 1024
HEAD = 64
N_HEADS = D // HEAD
LANE = 128
N_PAIRS = D // LANE
CHUNK = 64
LEFT = 8 * CHUNK
BAND = LEFT + CHUNK
REL_CLIP = 128
REL_TABLE = CHUNK + REL_CLIP
MEM_WIDTH = D // 2
MEM_HEADS = 4
FFN = 2816
LORA_W, LORA_A, LORA_G = 64, 64, 160
P_WIDTH = 3 * D + 3 * D + 2 * D + 128 + 128 + 256
C_Q, C_GA, C_LORA = 3 * D, 6 * D, 8 * D
NORM_EPS = 1e-6
GROUP_NORM_EPS = 64e-5
MASK_VALUE = -1e30
ADAM_LR, ADAM_B1, ADAM_B2, ADAM_EPS, ADAM_WD, ADAM_STEP = 0.001, 0.9, 0.999, 1e-08, 0.01, 10
VMEM_LIMIT = 56 * 1024 * 1024


def _cp(*sem):
    return pltpu.CompilerParams(dimension_semantics=sem, vmem_limit_bytes=VMEM_LIMIT)


_NN, _NT, _TN = ((1,), (0,)), ((1,), (1,)), ((0,), (0,))


def _dot_raw(a, b, dims):
    return lax.dot_general(a.astype(BF16), b.astype(BF16), (dims, ((), ())), preferred_element_type=F32)


@functools.partial(jax.custom_vjp, nondiff_argnums=(2,))
def _dot_dims(a, b, dims):
    return _dot_raw(a, b, dims)


def _dot_dims_fwd(a, b, dims):
    return _dot_raw(a, b, dims), (a, b)


def _dot_dims_bwd(dims, res, g):
    a, b = res
    if dims == _NN:
        da, db = _dot_raw(g, b, _NT), _dot_raw(a, g, _TN)
    elif dims == _NT:
        da, db = _dot_raw(g, b, _NN), _dot_raw(g, a, _TN)
    else:
        da, db = _dot_raw(b, g, _NT), _dot_raw(a, g, _NN)
    return da.astype(a.dtype), db.astype(b.dtype)


_dot_dims.defvjp(_dot_dims_fwd, _dot_dims_bwd)


def _dot(a, b, dims=_NN):
    return _dot_dims(a, b, dims)


def _dot_nt(a, b):
    return _dot_dims(a, b, _NT)


def _dot_tn(a, b):
    return _dot_dims(a, b, _TN)


def _split(x, terms):
    parts, rest = [], x.astype(F32)
    for _ in range(terms):
        p = rest.astype(BF16)
        parts.append(p)
        rest = rest - p.astype(F32)
    return parts


def _dot_split_a(a, b, terms=2):
    out = None
    for p in _split(a, terms):
        t = _dot(p, b)
        out = t if out is None else out + t
    return out


def _dot_split_b(a, b, terms=3):
    out = None
    for p in _split(b, terms):
        t = _dot(a, p)
        out = t if out is None else out + t
    return out


def _dot_hi(a, b, dims=_NN):
    ah, al = _split(a, 2)
    bh, bl = _split(b, 2)
    return _dot(ah, bh, dims) + (_dot(ah, bl, dims) + _dot(al, bh, dims))


MM_VMEM_BUDGET = 30 * 1024 * 1024
MM_HBM_BPS = 3.2e12
MM_MXU_FPS = 8.5e14
MM_STEP_S = 0.35e-6


def _divisors(n, align, cap):
    out = [d for d in range(align, min(n, cap) + 1, align) if n % d == 0]
    return out or [n]


def _mm_tiles(m, n, k, ea, eb, eo, ta):
    best = None
    for tm in _divisors(m, LANE if ta else 8, 2048):
        for tn in _divisors(n, LANE, 2048):
            for tk in _divisors(k, LANE, 2048):
                nk = k // tk
                vmem = 2 * (tm * tk * ea + tk * tn * eb + tm * tn * eo) + (tm * tn * 4 if nk > 1 else 0)
                if vmem > MM_VMEM_BUDGET:
                    continue
                dma = (tm * tk * ea if (nk > 1 or n // tn == 1) else tm * tk * ea * tn / n) + tk * tn * eb + tm * tn * eo / nk
                step = max(2.0 * tm * tn * tk / MM_MXU_FPS, dma / MM_HBM_BPS) + MM_STEP_S
                cost = (m // tm) * (n // tn) * nk * step
                if best is None or cost < best[0]:
                    best = (cost, tm, tn, tk)
    return best[1:]


def _mm(a, b, *, name, ta=False, tb=False, out_dtype=F32, tm=None, tn=None, tk=None, split_a=1, after=None):
    m, k = (a.shape[1], a.shape[0]) if ta else a.shape
    n, kb = (b.shape[0], b.shape[1]) if tb else (b.shape[1], b.shape[0])
    assert k == kb, (a.shape, b.shape, ta, tb)
    if tm is None:
        tm, tn, tk = _mm_tiles(m, n, k, a.dtype.itemsize, b.dtype.itemsize, jnp.dtype(out_dtype).itemsize, ta)
    assert m % tm == 0 and n % tn == 0 and k % tk == 0, (m, n, k, tm, tn, tk)
    nk = k // tk
    dims = ((0 if ta else 1,), (1 if tb else 0,))

    n_after = 0 if after is None else 1

    def body(a_ref, b_ref, *rest):
        o_ref, scratch = rest[n_after], rest[n_after + 1:]
        prod = None
        for p in _split(a_ref[...], split_a) if split_a > 1 else [a_ref[...]]:
            t = _dot_raw(p, b_ref[...], dims)
            prod = t if prod is None else prod + t
        if nk == 1:
            o_ref[...] = prod.astype(o_ref.dtype)
            return
        acc_ref, kk = scratch[0], pl.program_id(2)

        @pl.when(kk == 0)
        def _():
            acc_ref[...] = prod

        @pl.when(kk > 0)
        def _():
            acc_ref[...] += prod

        @pl.when(kk == nk - 1)
        def _():
            o_ref[...] = acc_ref[...].astype(o_ref.dtype)

    a_spec = pl.BlockSpec((tk, tm), lambda i, j, q: (q, i)) if ta else pl.BlockSpec((tm, tk), lambda i, j, q: (i, q))
    b_spec = pl.BlockSpec((tn, tk), lambda i, j, q: (j, q)) if tb else pl.BlockSpec((tk, tn), lambda i, j, q: (q, j))
    return pl.pallas_call(
        body, name=name, grid=(m // tm, n // tn, nk),
        in_specs=[a_spec, b_spec] + [pl.BlockSpec(memory_space=pl.ANY)] * n_after,
        out_specs=pl.BlockSpec((tm, tn), lambda i, j, q: (i, j)),
        out_shape=jax.ShapeDtypeStruct((m, n), out_dtype),
        scratch_shapes=[pltpu.VMEM((tm, tn), F32)] if nk > 1 else [],
        compiler_params=_cp("parallel", "parallel", "arbitrary"),
    )(a, b, *([] if after is None else [after]))


def _win(arr, start=0, width=None):
    width = arr.shape[1] if width is None else width
    assert start % width == 0
    return (arr, start // width, width)


def _row_specs(rows, tm):
    return [pl.BlockSpec((tm, w), functools.partial(lambda i, cb: (i, cb), cb=cb)) for (_, cb, w) in rows]


def _full_spec(p):
    nd = p.ndim
    return pl.BlockSpec(p.shape, lambda i, nd=nd: (0,) * nd)


def _rowwise(fn, rows, params, outs, *, name, tm):
    t = rows[0][0].shape[0]
    tm = min(tm, t)
    assert t % tm == 0
    nr, npar = len(rows), len(params)

    def body(*refs):
        vals = [r[...] for r in refs[:nr + npar]]
        res = fn(*vals)
        for o_ref, r in zip(refs[nr + npar:], res):
            o_ref[...] = r.astype(o_ref.dtype)

    return pl.pallas_call(
        body, name=name, grid=(t // tm,),
        in_specs=_row_specs(rows, tm) + [_full_spec(p) for p in params],
        out_specs=[pl.BlockSpec((tm, w), lambda i: (i, 0)) for (w, _) in outs],
        out_shape=[jax.ShapeDtypeStruct((t, w), dt) for (w, dt) in outs],
        compiler_params=_cp("parallel"),
    )(*[r[0] for r in rows], *params)


def _rowwise_bwd(fn, rows, params, n_const, cots, *, name, tm, row_grad, add_to=None):
    t = rows[0][0].shape[0]
    tm = min(tm, t)
    assert t % tm == 0
    nr, npar = len(rows), len(params)
    ndp = npar - n_const
    add_to = add_to or {}
    add_idx = sorted(add_to)
    flat_cots = [c for group in cots for c in group]
    kept = [i for i in range(nr) if row_grad[i] is not None]

    def body(*refs):
        pos = 0
        row_v = [r[...] for r in refs[pos:pos + nr]]; pos += nr
        par_v = [r[...] for r in refs[pos:pos + npar]]; pos += npar
        cot_v = [r[...] for r in refs[pos:pos + len(flat_cots)]]; pos += len(flat_cots)
        add_v = [r[...] for r in refs[pos:pos + len(add_idx)]]; pos += len(add_idx)
        rg_refs = refs[pos:pos + len(kept)]; pos += len(kept)
        pg_refs = refs[pos:pos + ndp]

        consts = par_v[ndp:]
        res, vjp = jax.vjp(lambda *args: tuple(fn(*args, *consts)), *row_v, *par_v[:ndp])
        cot_in, q = [], 0
        for j, group in enumerate(cots):
            c = None
            for _ in group:
                cv = cot_v[q].astype(F32); q += 1
                c = cv if c is None else c + cv
            c = jnp.zeros(res[j].shape, F32) if c is None else c
            cot_in.append(c.astype(res[j].dtype))
        grads = vjp(tuple(cot_in))
        for ref, i in zip(rg_refs, kept):
            g = grads[i].astype(F32)
            if i in add_to:
                g = g + add_v[add_idx.index(i)].astype(F32)
            ref[...] = g.astype(ref.dtype)

        @pl.when(pl.program_id(0) == 0)
        def _():
            for ref in pg_refs:
                ref[...] = jnp.zeros_like(ref)

        for ref, g in zip(pg_refs, grads[nr:]):
            ref[...] += g.astype(F32)

    cot_specs = [pl.BlockSpec((tm, c.shape[1]), lambda i: (i, 0)) for c in flat_cots]
    add_specs = [pl.BlockSpec((tm, add_to[i].shape[1]), lambda i_: (i_, 0)) for i in add_idx]
    out_specs = [pl.BlockSpec((tm, rows[i][2]), lambda i_: (i_, 0)) for i in kept] + [_full_spec(p) for p in params[:ndp]]
    out_shape = [jax.ShapeDtypeStruct((t, rows[i][2]), row_grad[i]) for i in kept] + [
        jax.ShapeDtypeStruct(p.shape, F32) for p in params[:ndp]]
    res = pl.pallas_call(
        body, name=name, grid=(t // tm,),
        in_specs=_row_specs(rows, tm) + [_full_spec(p) for p in params] + cot_specs + add_specs,
        out_specs=out_specs, out_shape=out_shape,
        compiler_params=_cp("arbitrary"),
    )(*[r[0] for r in rows], *params, *flat_cots, *[add_to[i] for i in add_idx])
    return list(res[:len(kept)]), list(res[len(kept):])


def _rms(x, g):
    xf = x.astype(F32)
    return xf * lax.rsqrt(jnp.mean(xf * xf, axis=-1, keepdims=True) + NORM_EPS) * g


def _softplus(x):
    return jnp.maximum(x, 0.0) + jnp.log(1.0 + jnp.exp(-jnp.abs(x)))


def _fn_pre(x, g):
    return (_rms(x, g).astype(BF16),)


def _fn_res(x, u, g_post):
    return (x + _rms(u, g_post),)


def _fn_res_pre(x, u, g_post, g_pre):
    xn = x + _rms(u, g_post)
    return xn, _rms(xn, g_pre).astype(BF16)


def _fn_mix(zga, zgb, ya, yb):
    return ((jax.nn.sigmoid(zga) * ya + jax.nn.sigmoid(zgb) * yb).astype(BF16),)


def _fn_swiglu(gate, up):
    return ((gate * jax.nn.sigmoid(gate) * up).astype(BF16),)


def _fn_prep(zk, zw, za, zg, decay_base, d_up, iclr_base, i_up, g_up, kns, kis, e_hd, e_dh):
    w_log = -_softplus(-(decay_base + _dot(jnp.tanh(zw), d_up))) - 0.5
    lw = -jnp.exp(w_log)
    a = jax.nn.sigmoid(iclr_base + _dot(za, i_up))
    g = _dot(jax.nn.sigmoid(zg), g_up)
    kn = zk * kns
    ss = _dot_split_a(kn * kn, e_dh)
    inv = lax.rsqrt(jnp.maximum(ss, 1e-24))
    kk = kn * _dot_split_a(inv, e_hd)
    k2 = zk * (1.0 + (a - 1.0) * kis)
    return lw, k2, kk, a, g


def _fn_post(y, r, k2, v, g, lnx_w, lnx_b, bonus, e_hd, e_dh):
    mu = _dot_split_a(_dot_split_a(y, e_dh) * (1.0 / HEAD), e_hd)
    yc = y - mu
    var = _dot_split_a(yc * yc, e_dh) * (1.0 / HEAD)
    yn = yc * _dot_split_a(lax.rsqrt(var + GROUP_NORM_EPS), e_hd)
    bs = _dot_split_a(_dot_split_a(r * k2 * bonus, e_dh), e_hd)
    return (((yn * lnx_w + lnx_b + bs * v) * g).astype(BF16),)


def _shift_fwd(p, col0, ncols, mix, seq, *, name, cw=256):
    t = p.shape[0]
    assert col0 % cw == 0 and ncols % cw == 0 and t % seq == 0
    cb0 = col0 // cw

    def body(p_ref, m_ref, z_ref):
        pv = p_ref[...]
        row = lax.broadcasted_iota(jnp.int32, pv.shape, 0)
        prev = jnp.where(row == 0, 0.0, pltpu.roll(pv, 1, axis=0))
        z_ref[...] = pv + (prev - pv) * m_ref[...]

    return pl.pallas_call(
        body, name=name, grid=(t // seq, ncols // cw),
        in_specs=[pl.BlockSpec((seq, cw), lambda b, c: (b, c + cb0)), pl.BlockSpec((1, cw), lambda b, c: (0, c))],
        out_specs=pl.BlockSpec((seq, cw), lambda b, c: (b, c)),
        out_shape=jax.ShapeDtypeStruct((t, ncols), F32),
        compiler_params=_cp("parallel", "parallel"),
    )(p, mix)


def _shift_bwd(p, col0, ncols, mix, dz_parts, seq, *, name, cw=256):
    t = p.shape[0]
    cb0 = col0 // cw
    n = len(dz_parts)

    def body(*refs):
        p_ref, m_ref = refs[:2]
        dp_ref, dm_ref = refs[2 + n:]
        dz = refs[2][...].astype(F32)
        for r in refs[3:2 + n]:
            dz = dz + r[...].astype(F32)
        pv = p_ref[...]
        mixv = m_ref[...]
        row = lax.broadcasted_iota(jnp.int32, pv.shape, 0)
        prev = jnp.where(row == 0, 0.0, pltpu.roll(pv, 1, axis=0))
        u = dz * mixv
        nxt = jnp.where(row == seq - 1, 0.0, pltpu.roll(u, seq - 1, axis=0))
        dp_ref[...] = (dz - u + nxt).astype(dp_ref.dtype)

        @pl.when(pl.program_id(1) == 0)
        def _():
            dm_ref[...] = jnp.zeros_like(dm_ref)

        dm_ref[...] += jnp.sum(dz * (prev - pv), axis=0, keepdims=True)

    return pl.pallas_call(
        body, name=name, grid=(ncols // cw, t // seq),
        in_specs=[pl.BlockSpec((seq, cw), lambda c, b: (b, c + cb0)), pl.BlockSpec((1, cw), lambda c, b: (0, c))]
        + [pl.BlockSpec((seq, cw), lambda c, b: (b, c))] * n,
        out_specs=[pl.BlockSpec((seq, cw), lambda c, b: (b, c)), pl.BlockSpec((1, cw), lambda c, b: (0, c))],
        out_shape=[jax.ShapeDtypeStruct((t, ncols), BF16), jax.ShapeDtypeStruct((1, ncols), F32)],
        compiler_params=_cp("parallel", "arbitrary"),
    )(p, mix, *dz_parts)


def _each(f, *lists):
    return [f(*xs) for xs in zip(*lists)]


def _tri_inv(low):
    c = low[0].shape[0]
    ti = lax.broadcasted_iota(jnp.int32, (c, c), 0)
    si = lax.broadcasted_iota(jnp.int32, (c, c), 1)
    eye = (ti == si).astype(F32)
    inside = (ti // 4) == (si // 4)
    base = [jnp.where(inside, m, 0.0) for m in low]
    acc = _each(lambda m: _dot(eye - m, eye + _dot(m, m)), base)
    size = 8
    while size <= c:
        wider = (ti // size) == (si // size)
        keep = jnp.logical_and(wider, jnp.logical_not(inside))
        acc = _each(lambda p, m: p - _dot(_dot(p, jnp.where(keep, m, 0.0)), p), acc, low)
        inside, size = wider, size * 2
    return acc


@jax.custom_vjp
def _tri_inv_known(low, inv):
    return inv


def _tri_inv_known_fwd(low, inv):
    return inv, inv


def _tri_inv_known_bwd(inv, g):
    dlow = _each(lambda t, gg: -_dot(_dot(t, gg, _TN), t, _NT), inv, g)
    return dlow, _each(jnp.zeros_like, inv)


_tri_inv_known.defvjp(_tri_inv_known_fwd, _tri_inv_known_bwd)


def _wkv_chunk(s0, r, lw, k, v, kk, a, inv=None):
    c = r[0].shape[0]
    ti = lax.broadcasted_iota(jnp.int32, (c, c), 0)
    si = lax.broadcasted_iota(jnp.int32, (c, c), 1)
    incl, strict = ti >= si, ti > si
    tri = incl.astype(F32)
    cum = _each(lambda x: _dot_split_b(tri, x, 3), lw)
    eg = _each(jnp.exp, cum)
    egp = _each(lambda cs, x: jnp.exp(cs - x), cum, lw)
    ei = _each(lambda cs: jnp.exp(-cs), cum)
    rh, kkh, kt = _each(jnp.multiply, r, eg), _each(jnp.multiply, kk, egp), _each(jnp.multiply, k, ei)
    bt = _each(lambda p, q, e: (p * q) * e, a, kk, ei)
    lb = _each(lambda p, q: jnp.where(strict, _dot_nt(p, q), 0.0), kkh, bt)
    lk = _each(lambda p, q: jnp.where(strict, _dot_nt(p, q), 0.0), kkh, kt)
    mb = _each(lambda p, q: jnp.where(incl, _dot_nt(p, q), 0.0), rh, bt)
    mk = _each(lambda p, q: jnp.where(incl, _dot_nt(p, q), 0.0), rh, kt)
    rhs = _each(lambda p, s, m, x: _dot_nt(p, s) + _dot(m, x), kkh, s0, lk, v)
    inv = _tri_inv(lb) if inv is None else _tri_inv_known(lb, inv)
    u = _each(lambda t, x: -_dot(t, x), inv, rhs)
    y = _each(lambda p, s, m1, uu, m2, x: _dot_nt(p, s) + _dot(m1, uu) + _dot(m2, x), rh, s0, mb, u, mk, v)
    s1 = _each(lambda s, uu, b, x, kq, w: (s + _dot_tn(uu, b) + _dot_tn(x, kq)) * jnp.exp(jnp.sum(w, axis=0, keepdims=True)),
               s0, u, bt, v, kt, lw)
    return y, s1, inv


WKV_HEADS = 16
WKV_COLS = WKV_HEADS * HEAD
WKV_GROUPS = N_HEADS // WKV_HEADS


def _head_cols(ref):
    return [ref[:, h * HEAD:(h + 1) * HEAD] for h in range(ref.shape[1] // HEAD)]


def _wkv_specs(seq, rev):
    nc = seq // CHUNK

    def rows(col0):
        cb0 = col0 // WKV_COLS
        if rev:
            return pl.BlockSpec((CHUNK, WKV_COLS), lambda b, h, c: (b * nc + nc - 1 - c, cb0 + h))
        return pl.BlockSpec((CHUNK, WKV_COLS), lambda b, h, c: (b * nc + c, cb0 + h))

    if rev:
        st = pl.BlockSpec((1, 1, WKV_HEADS, HEAD, HEAD), lambda b, h, c: (b * WKV_GROUPS + h, nc - 1 - c, 0, 0, 0))
    else:
        st = pl.BlockSpec((1, 1, WKV_HEADS, HEAD, HEAD), lambda b, h, c: (b * WKV_GROUPS + h, c, 0, 0, 0))
    return rows, st


def _wkv_fwd(z_rkv, lw, k2, kk, a, seq):
    t = z_rkv.shape[0]
    nb, nc = t // seq, seq // CHUNK
    rows, st = _wkv_specs(seq, False)

    def body(r_ref, v_ref, lw_ref, k_ref, kk_ref, a_ref, y_ref, st_ref, inv_ref, s_scr):
        @pl.when(pl.program_id(2) == 0)
        def _():
            s_scr[...] = jnp.zeros_like(s_scr)

        s0 = [s_scr[h] for h in range(WKV_HEADS)]
        y, s1, inv = _wkv_chunk(s0, *[_head_cols(ref) for ref in (r_ref, lw_ref, k_ref, v_ref, kk_ref, a_ref)])
        for h in range(WKV_HEADS):
            st_ref[0, 0, h] = s0[h]
            inv_ref[0, 0, h] = inv[h]
            y_ref[:, h * HEAD:(h + 1) * HEAD] = y[h]
            s_scr[h] = s1[h]

    per_chunk = jax.ShapeDtypeStruct((nb * WKV_GROUPS, nc, WKV_HEADS, HEAD, HEAD), F32)
    return pl.pallas_call(
        body, name="wkv_fwd", grid=(nb, WKV_GROUPS, nc),
        in_specs=[rows(0), rows(2 * D), rows(0), rows(0), rows(0), rows(0)],
        out_specs=[rows(0), st, st],
        out_shape=[jax.ShapeDtypeStruct((t, D), F32), per_chunk, per_chunk],
        scratch_shapes=[pltpu.VMEM((WKV_HEADS, HEAD, HEAD), F32)],
        compiler_params=_cp("parallel", "parallel", "arbitrary"),
    )(z_rkv, z_rkv, lw, k2, kk, a)


def _wkv_bwd(z_rkv, lw, k2, kk, a, states, invs, dy, seq):
    t = z_rkv.shape[0]
    nb, nc = t // seq, seq // CHUNK
    rows, st = _wkv_specs(seq, True)

    def body(r_ref, v_ref, lw_ref, k_ref, kk_ref, a_ref, st_ref, inv_ref, dy_ref,
             dr_ref, dlw_ref, dk_ref, dv_ref, dkk_ref, da_ref, ds_scr):
        @pl.when(pl.program_id(2) == 0)
        def _():
            ds_scr[...] = jnp.zeros_like(ds_scr)

        s0 = [st_ref[0, 0, h] for h in range(WKV_HEADS)]
        inv = [inv_ref[0, 0, h] for h in range(WKV_HEADS)]
        _, vjp = jax.vjp(lambda *args: _wkv_chunk(*args, inv=inv)[:2],
                         s0, *[_head_cols(ref) for ref in (r_ref, lw_ref, k_ref, v_ref, kk_ref, a_ref)])
        grads = vjp(([x.astype(F32) for x in _head_cols(dy_ref)], [ds_scr[h] for h in range(WKV_HEADS)]))
        for h in range(WKV_HEADS):
            ds_scr[h] = grads[0][h]
            for ref, g in zip((dr_ref, dlw_ref, dk_ref, dv_ref, dkk_ref, da_ref), grads[1:]):
                ref[:, h * HEAD:(h + 1) * HEAD] = g[h]

    return pl.pallas_call(
        body, name="wkv_bwd", grid=(nb, WKV_GROUPS, nc),
        in_specs=[rows(0), rows(2 * D), rows(0), rows(0), rows(0), rows(0), st, st, rows(0)],
        out_specs=[rows(0)] * 6,
        out_shape=[jax.ShapeDtypeStruct((t, D), F32)] * 6,
        scratch_shapes=[pltpu.VMEM((WKV_HEADS, HEAD, HEAD), F32)],
        compiler_params=_cp("parallel", "parallel", "arbitrary"),
    )(z_rkv, z_rkv, lw, k2, kk, a, states, invs, dy)


def _softmax(s):
    e = jnp.exp(s - jnp.max(s, axis=-1, keepdims=True))
    return e / jnp.sum(e, axis=-1, keepdims=True)


ATT_HEADS = 8
ATT_COLS = ATT_HEADS * HEAD
ATT_GROUPS = N_HEADS // ATT_HEADS


def _attn_chunk(q, kb, vb, bias, valid):
    s = _each(lambda x, y, z: jnp.where(valid, _dot_nt(x, y) * (HEAD ** -0.5) + z, MASK_VALUE), q, kb, bias)
    return _each(_dot, _each(_softmax, s), vb)


def _pad_fill(pad_ref, src_ref):
    pad_ref[0:LEFT, :] = jnp.zeros((LEFT, pad_ref.shape[1]), pad_ref.dtype)
    pad_ref[LEFT:, :] = src_ref[...].astype(pad_ref.dtype)


def _band_heads(pad_ref, start):
    return [pad_ref[pl.ds(start, BAND), h * HEAD:(h + 1) * HEAD].astype(F32) for h in range(ATT_HEADS)]


def _band_valid(c):
    return (c * CHUNK - LEFT + lax.broadcasted_iota(jnp.int32, (1, BAND), 1)) >= 0


def _attn_fwd(proj, bias, seq):
    t = proj.shape[0]
    nb, nc = t // seq, seq // CHUNK
    cq = C_Q // ATT_COLS

    def body(q_ref, k_ref, v_ref, b_ref, o_ref, kpad, vpad):
        c = pl.program_id(2)

        @pl.when(c == 0)
        def _():
            _pad_fill(kpad, k_ref)
            _pad_fill(vpad, v_ref)

        start = pl.multiple_of(c * CHUNK, CHUNK)
        o = _attn_chunk(_head_cols(q_ref), _band_heads(kpad, start), _band_heads(vpad, start),
                        [b_ref[h] for h in range(ATT_HEADS)], _band_valid(c))
        for h in range(ATT_HEADS):
            o_ref[:, h * HEAD:(h + 1) * HEAD] = o[h].astype(o_ref.dtype)

    return pl.pallas_call(
        body, name="attn_fwd", grid=(ATT_GROUPS, nb, nc),
        in_specs=[pl.BlockSpec((CHUNK, ATT_COLS), lambda h, b, c: (b * nc + c, cq + h)),
                  pl.BlockSpec((seq, ATT_COLS), lambda h, b, c: (b, cq + ATT_GROUPS + h)),
                  pl.BlockSpec((seq, ATT_COLS), lambda h, b, c: (b, cq + 2 * ATT_GROUPS + h)),
                  pl.BlockSpec((ATT_HEADS, CHUNK, BAND), lambda h, b, c: (h, 0, 0))],
        out_specs=pl.BlockSpec((CHUNK, ATT_COLS), lambda h, b, c: (b * nc + c, h)),
        out_shape=jax.ShapeDtypeStruct((t, D), BF16),
        scratch_shapes=[pltpu.VMEM((seq + LEFT, ATT_COLS), BF16)] * 2,
        compiler_params=_cp("parallel", "arbitrary", "arbitrary"),
    )(proj, proj, proj, bias)


def _attn_bwd(proj, bias, do, seq):
    t = proj.shape[0]
    nb, nc = t // seq, seq // CHUNK
    cq = C_Q // ATT_COLS

    def body(q_ref, k_ref, v_ref, b_ref, do_ref, dq_ref, dk_ref, dv_ref, db_ref, kpad, vpad, dkpad, dvpad):
        b, c = pl.program_id(1), pl.program_id(2)

        @pl.when(c == 0)
        def _():
            _pad_fill(kpad, k_ref)
            _pad_fill(vpad, v_ref)
            dkpad[...] = jnp.zeros_like(dkpad)
            dvpad[...] = jnp.zeros_like(dvpad)

        @pl.when(jnp.logical_and(b == 0, c == 0))
        def _():
            db_ref[...] = jnp.zeros_like(db_ref)

        start = pl.multiple_of(c * CHUNK, CHUNK)
        _, vjp = jax.vjp(functools.partial(_attn_chunk, valid=_band_valid(c)),
                         _head_cols(q_ref), _band_heads(kpad, start), _band_heads(vpad, start),
                         [b_ref[h] for h in range(ATT_HEADS)])
        dq, dkb, dvb, dbias = vjp([x.astype(F32) for x in _head_cols(do_ref)])
        for h in range(ATT_HEADS):
            sl = slice(h * HEAD, (h + 1) * HEAD)
            dq_ref[:, sl] = dq[h].astype(dq_ref.dtype)
            dkpad[pl.ds(start, BAND), sl] += dkb[h].astype(F32)
            dvpad[pl.ds(start, BAND), sl] += dvb[h].astype(F32)
            db_ref[h] += dbias[h]

        @pl.when(c == nc - 1)
        def _():
            dk_ref[...] = dkpad[LEFT:, :].astype(dk_ref.dtype)
            dv_ref[...] = dvpad[LEFT:, :].astype(dv_ref.dtype)

    kv_out = pl.BlockSpec((seq, ATT_COLS), lambda h, b, c: (b, h))
    return pl.pallas_call(
        body, name="attn_bwd", grid=(ATT_GROUPS, nb, nc),
        in_specs=[pl.BlockSpec((CHUNK, ATT_COLS), lambda h, b, c: (b * nc + c, cq + h)),
                  pl.BlockSpec((seq, ATT_COLS), lambda h, b, c: (b, cq + ATT_GROUPS + h)),
                  pl.BlockSpec((seq, ATT_COLS), lambda h, b, c: (b, cq + 2 * ATT_GROUPS + h)),
                  pl.BlockSpec((ATT_HEADS, CHUNK, BAND), lambda h, b, c: (h, 0, 0)),
                  pl.BlockSpec((CHUNK, ATT_COLS), lambda h, b, c: (b * nc + c, h))],
        out_specs=[pl.BlockSpec((CHUNK, ATT_COLS), lambda h, b, c: (b * nc + c, h)), kv_out, kv_out,
                   pl.BlockSpec((ATT_HEADS, CHUNK, BAND), lambda h, b, c: (h, 0, 0))],
        out_shape=[jax.ShapeDtypeStruct((t, D), BF16)] * 3 + [jax.ShapeDtypeStruct((N_HEADS, CHUNK, BAND), F32)],
        scratch_shapes=[pltpu.VMEM((seq + LEFT, ATT_COLS), BF16)] * 2 + [pltpu.VMEM((seq + LEFT, ATT_COLS), F32)] * 2,
        compiler_params=_cp("parallel", "arbitrary", "arbitrary"),
    )(proj, proj, proj, bias, do)


def _xattn_tile(q, k, v):
    s = _dot_nt(q, k) * ((MEM_WIDTH // MEM_HEADS) ** -0.5)
    return _dot(_softmax(s), v)


def _xattn_fwd(qm, kvm, seq, n_mem, tq=512):
    t = qm.shape[0]
    tq = min(tq, seq)
    nb, nq = t // seq, seq // tq

    def body(q_ref, k_ref, v_ref, o_ref):
        o_ref[...] = _xattn_tile(q_ref[...], k_ref[...], v_ref[...]).astype(o_ref.dtype)

    return pl.pallas_call(
        body, name="xattn_fwd", grid=(nb, MEM_HEADS, nq),
        in_specs=[pl.BlockSpec((tq, LANE), lambda b, h, i: (b * nq + i, h)),
                  pl.BlockSpec((n_mem, LANE), lambda b, h, i: (b, h)),
                  pl.BlockSpec((n_mem, LANE), lambda b, h, i: (b, MEM_HEADS + h))],
        out_specs=pl.BlockSpec((tq, LANE), lambda b, h, i: (b * nq + i, h)),
        out_shape=jax.ShapeDtypeStruct((t, MEM_WIDTH), BF16),
        compiler_params=_cp("parallel", "parallel", "parallel"),
    )(qm, kvm, kvm)


def _xattn_bwd(qm, kvm, do, seq, n_mem, tq=512):
    t = qm.shape[0]
    tq = min(tq, seq)
    nb, nq = t // seq, seq // tq

    def body(q_ref, k_ref, v_ref, do_ref, dq_ref, dkv_ref, dk_acc, dv_acc):
        i = pl.program_id(2)

        @pl.when(i == 0)
        def _():
            dk_acc[...] = jnp.zeros_like(dk_acc)
            dv_acc[...] = jnp.zeros_like(dv_acc)

        _, vjp = jax.vjp(_xattn_tile, q_ref[...], k_ref[...], v_ref[...])
        dq, dk, dv = vjp(do_ref[...].astype(F32))
        dq_ref[...] = dq.astype(dq_ref.dtype)
        dk_acc[...] += dk
        dv_acc[...] += dv

        @pl.when(i == nq - 1)
        def _():
            dkv_ref[0] = dk_acc[...].astype(dkv_ref.dtype)
            dkv_ref[1] = dv_acc[...].astype(dkv_ref.dtype)

    dq, dkv = pl.pallas_call(
        body, name="xattn_bwd", grid=(nb, MEM_HEADS, nq),
        in_specs=[pl.BlockSpec((tq, LANE), lambda b, h, i: (b * nq + i, h)),
                  pl.BlockSpec((n_mem, LANE), lambda b, h, i: (b, h)),
                  pl.BlockSpec((n_mem, LANE), lambda b, h, i: (b, MEM_HEADS + h)),
                  pl.BlockSpec((tq, LANE), lambda b, h, i: (b * nq + i, h))],
        out_specs=[pl.BlockSpec((tq, LANE), lambda b, h, i: (b * nq + i, h)),
                   pl.BlockSpec((2, n_mem, LANE), lambda b, h, i: (0, b, h))],
        out_shape=[jax.ShapeDtypeStruct((t, MEM_WIDTH), BF16), jax.ShapeDtypeStruct((2, nb * n_mem, MEM_WIDTH), BF16)],
        scratch_shapes=[pltpu.VMEM((n_mem, LANE), F32)] * 2,
        compiler_params=_cp("parallel", "parallel", "arbitrary"),
    )(qm, kvm, kvm, do)
    return dq, jnp.concatenate([dkv[0], dkv[1]], axis=1)


def _loss_head(y, target, tm=512):
    t, d = y.shape
    tm = min(tm, t)

    def body(y_ref, t_ref, dy_ref, l_ref):
        @pl.when(pl.program_id(0) == 0)
        def _():
            l_ref[...] = jnp.zeros_like(l_ref)

        diff = y_ref[...] - t_ref[...]
        dy_ref[...] = diff * (1.0 / d)
        l_ref[...] += 0.5 * jnp.sum(jnp.mean(diff * diff, axis=-1, keepdims=True), axis=0, keepdims=True)

    dy, loss = pl.pallas_call(
        body, name="loss_head", grid=(t // tm,),
        in_specs=[pl.BlockSpec((tm, d), lambda i: (i, 0))] * 2,
        out_specs=[pl.BlockSpec((tm, d), lambda i: (i, 0)), pl.BlockSpec((8, LANE), lambda i: (0, 0))],
        out_shape=[jax.ShapeDtypeStruct((t, d), F32), jax.ShapeDtypeStruct((8, LANE), F32)],
        compiler_params=_cp("arbitrary"),
    )(y, target)
    return dy, loss


def _mesh_pos():
    return lax.axis_index("x"), lax.axis_index("y"), lax.axis_index("c")


def _peer(pos, d):
    x, y, c = pos
    return ((1 - x) if d & 4 else x, (1 - y) if d & 2 else y, (1 - c) if d & 1 else c)


def _flat(pos):
    return 4 * pos[0] + 2 * pos[1] + pos[2]


def _exchange(arrays, scatter, *, name):
    n = len(arrays)
    shapes = [a.shape[1:] if scatter else a.shape for a in arrays]

    def body(*refs):
        ins, outs = refs[:n], refs[n:2 * n]
        send, recv, loc = refs[2 * n:]
        pos = _mesh_pos()
        me = _flat(pos)
        pending = []
        for i in range(n):
            own = pltpu.make_async_copy(ins[i].at[me] if scatter else ins[i], outs[i].at[me], loc.at[i])
            own.start()
            pending.append(own)
            for d in range(1, N_DEV):
                peer = _peer(pos, d)
                src = ins[i].at[_flat(peer)] if scatter else ins[i]
                out_cp = pltpu.make_async_remote_copy(
                    src_ref=src, dst_ref=outs[i].at[me], send_sem=send.at[i, d - 1], recv_sem=recv.at[i, d - 1],
                    device_id=peer, device_id_type=pl.DeviceIdType.MESH)
                out_cp.start()
                pending.append(out_cp)
        for i in range(n):
            own = pending[i * N_DEV]
            for d in range(1, N_DEV):
                peer = _peer(pos, d)
                src = ins[i].at[_flat(peer)] if scatter else ins[i]
                pending[i * N_DEV + d].wait_send()
                pltpu.make_async_remote_copy(
                    src_ref=src, dst_ref=outs[i].at[_flat(peer)], send_sem=send.at[i, d - 1], recv_sem=recv.at[i, d - 1],
                    device_id=peer, device_id_type=pl.DeviceIdType.MESH).wait_recv()
            own.wait()

    hbm = pl.BlockSpec(memory_space=pltpu.HBM)
    return pl.pallas_call(
        body, name=name,
        in_specs=[hbm] * n, out_specs=[hbm] * n,
        out_shape=[jax.ShapeDtypeStruct((N_DEV,) + tuple(s), a.dtype) for s, a in zip(shapes, arrays)],
        scratch_shapes=[pltpu.SemaphoreType.DMA((n, N_DEV - 1)), pltpu.SemaphoreType.DMA((n, N_DEV - 1)),
                        pltpu.SemaphoreType.DMA((n,))],
    )(*arrays)


_HBM = pl.BlockSpec(memory_space=pltpu.HBM)
_SEM = pl.BlockSpec(memory_space=pltpu.SEMAPHORE)
_DATAFLOW = pltpu.SideEffectType.DATAFLOW_SIDE_EFFECTING


def _remote_copies(ins, lands, send, recv, scatter):
    pos = _mesh_pos()
    me = _flat(pos)
    out = []
    for i in range(len(ins)):
        for d in range(1, N_DEV):
            peer = _peer(pos, d)
            src = ins[i].at[_flat(peer)] if scatter else ins[i]
            pair = i * (N_DEV - 1) + d - 1
            sems = dict(send_sem=send.at[pair], recv_sem=recv.at[pair], device_id=peer,
                        device_id_type=pl.DeviceIdType.MESH)
            out.append((pltpu.make_async_remote_copy(src_ref=src, dst_ref=lands[i].at[me], **sems),
                        pltpu.make_async_remote_copy(src_ref=src, dst_ref=lands[i].at[_flat(peer)], **sems)))
    return out


def _exchange_start(arrays, scatter, after, *, name):
    n = len(arrays)
    shapes = [a.shape[1:] if scatter else a.shape for a in arrays]
    lands = [pltpu.with_memory_space_constraint(lax.empty((N_DEV,) + tuple(s), a.dtype), pltpu.HBM)
             for s, a in zip(shapes, arrays)]
    srcs = [pltpu.with_memory_space_constraint(a, pltpu.HBM) for a in arrays]

    def body(*refs):
        ins, land_refs = refs[:n], refs[n:2 * n]
        send, recv, token = refs[2 * n + 1], refs[2 * n + 2], refs[-1]
        for going, _ in _remote_copies(ins, land_refs, send, recv, scatter):
            going.start()
        token[...] = jnp.zeros_like(token)

    sems = pltpu.SemaphoreType.DMA((n * (N_DEV - 1),))
    res = pl.pallas_call(
        body, name=name,
        out_shape=(sems, sems, *[pltpu.HBM(a.shape, a.dtype) for a in srcs + lands], jax.ShapeDtypeStruct((8, LANE), F32)),
        in_specs=[_HBM] * (2 * n) + [pl.BlockSpec(memory_space=pl.ANY)],
        out_specs=(_SEM, _SEM, *[_HBM] * (2 * n), pl.BlockSpec(memory_space=pltpu.VMEM)),
        input_output_aliases={i: 2 + i for i in range(2 * n)},
        compiler_params=pltpu.CompilerParams(has_side_effects=_DATAFLOW),
    )(*srcs, *lands, after)
    return (n, scatter, res[0], res[1], list(res[2:2 + 2 * n])), res[-1]


def _exchange_wait(handle, after, own, *, name):
    n, scatter, send, recv, thru = handle

    def body(*refs):
        ins, land_refs = refs[:n], refs[n:2 * n]
        for going, coming in _remote_copies(ins, land_refs, refs[2 * n], refs[2 * n + 1], scatter):
            going.wait_send()
            coming.wait_recv()

    res = pl.pallas_call(
        body, name=name,
        out_shape=tuple(pltpu.HBM(a.shape, a.dtype) for a in thru),
        in_specs=[_HBM] * (2 * n) + [_SEM, _SEM] + [pl.BlockSpec(memory_space=pl.ANY)] * len(after),
        out_specs=tuple([_HBM] * (2 * n)),
        input_output_aliases={i: i for i in range(2 * n)},
        compiler_params=pltpu.CompilerParams(has_side_effects=_DATAFLOW),
    )(*thru, send, recv, *after)
    me = _flat(_mesh_pos())
    return [lax.dynamic_update_slice_in_dim(land, o[None].astype(land.dtype), me, 0) for land, o in zip(res[n:], own)]


def _adamw(parts, w, m, v, *, name, tr=128):
    r, c = w.shape
    align = 8 * 4 // parts.dtype.itemsize
    tr = max(d for d in range(align, min(tr, r) + 1, align) if r % d == 0)

    def body(p_ref, w_ref, m_ref, v_ref, g_ref, d_ref, nm_ref, nv_ref):
        g = p_ref[0].astype(F32)
        for j in range(1, N_DEV):
            g = g + p_ref[j].astype(F32)
        m2 = ADAM_B1 * m_ref[...] + (1.0 - ADAM_B1) * g
        v2 = ADAM_B2 * v_ref[...] + (1.0 - ADAM_B2) * (g * g)
        m_hat = m2 / (1.0 - ADAM_B1 ** ADAM_STEP)
        v_hat = v2 / (1.0 - ADAM_B2 ** ADAM_STEP)
        g_ref[...] = g
        d_ref[...] = -ADAM_LR * (m_hat / (jnp.sqrt(v_hat) + ADAM_EPS) + ADAM_WD * w_ref[...])
        nm_ref[...] = m2
        nv_ref[...] = v2

    spec = pl.BlockSpec((tr, c), lambda i: (i, 0))
    return pl.pallas_call(
        body, name=name, grid=(r // tr,),
        in_specs=[pl.BlockSpec((N_DEV, tr, c), lambda i: (0, i, 0)), spec, spec, spec],
        out_specs=[spec] * 4, out_shape=[jax.ShapeDtypeStruct((r, c), F32)] * 4,
        compiler_params=_cp("parallel"),
    )(parts, w, m, v)


def _cols_to_full(g):
    return jnp.transpose(g, (1, 0, 2)).reshape(g.shape[1], N_DEV * g.shape[2])


def _full_to_cols(w):
    r, c = w.shape
    return jnp.transpose(w.reshape(r, N_DEV, c // N_DEV), (1, 0, 2))


def _pad_cols(a, width):
    return jnp.pad(a, ((0, 0), (0, width - a.shape[1])))


def _pad_lora(w):
    return jnp.concatenate([
        _pad_cols(w[:, :LORA_W], 128), _pad_cols(w[:, LORA_W:LORA_W + LORA_A], 128),
        _pad_cols(w[:, LORA_W + LORA_A:], 256)], axis=1)


def _unpad_lora(wp):
    return jnp.concatenate([wp[:, :LORA_W], wp[:, 128:128 + LORA_A], wp[:, 256:256 + LORA_G]], axis=1)


def _permute_in(w):
    rk = 3 * D
    lo = rk + LORA_W + LORA_A + LORA_G
    return jnp.concatenate([w[:, :rk], w[:, lo:], _pad_lora(w[:, rk:lo])], axis=1)


def _unpermute_in(wp):
    return jnp.concatenate([wp[:, :3 * D], _unpad_lora(wp[:, C_LORA:]), wp[:, 3 * D:C_LORA]], axis=1)


def _rel_index():
    dist = jnp.arange(CHUNK)[:, None] - jnp.arange(BAND)[None, :] + LEFT
    return (jnp.minimum(dist, REL_CLIP) + (CHUNK - 1)).reshape(-1)


def _local_step(x, mem, target, wt, seq, n_mem, comm):
    t = x.shape[0]
    row = lambda a: a.reshape(1, -1).astype(F32)
    g_pre_mix, g_post_mix = row(wt["g_pre_mix"]), row(wt["g_post_mix"])
    g_pre_cross, g_post_cross, g_mem = row(wt["g_pre_cross"]), row(wt["g_post_cross"]), row(wt["g_mem"])
    g_pre_ffn, g_post_ffn = row(wt["g_pre_ffn"]), row(wt["g_post_ffn"])
    w_in = wt["w_in_p"]
    mix = row(wt["shift_mix"])
    mix_rkv, mix_lora = mix[:, :3 * D], _pad_lora(mix[:, 3 * D:])
    d_up = jnp.pad(wt["decay_up"].astype(F32), ((0, 128 - LORA_W), (0, 0)))
    i_up = jnp.pad(wt["iclr_up"].astype(F32), ((0, 128 - LORA_A), (0, 0)))
    g_up = jnp.pad(wt["gate_up"].astype(F32), ((0, 256 - LORA_G), (0, 0)))
    decay_base, iclr_base = row(wt["decay_base"]), row(wt["iclr_base"])
    kns, kis = row(wt["key_norm_scale"]), row(wt["key_iclr_scale"])
    lnx_w, lnx_b, bonus = row(wt["lnx_w"]), row(wt["lnx_b"]), row(wt["bonus_scale"])
    e_dh = (jnp.arange(D)[:, None] // HEAD == jnp.arange(N_HEADS)[None, :]).astype(F32)
    e_hd = e_dh.T
    onehot = (jnp.arange(REL_TABLE)[:, None] == _rel_index()[None, :]).astype(BF16)

    (h1,) = _rowwise(_fn_pre, [_win(x)], [g_pre_mix], [(D, BF16)], name="pre_mix", tm=512)
    proj = _mm(h1, w_in, name="mm_in", after=comm.first_token)
    z_rkv = _shift_fwd(proj, 0, 3 * D, mix_rkv, seq, name="shift_rkv")
    z_lora = _shift_fwd(proj, C_LORA, 512, mix_lora, seq, name="shift_lora")
    prep_rows = [_win(z_rkv, D, D), _win(z_lora, 0, 128), _win(z_lora, 128, 128), _win(z_lora, 256, 256)]
    prep_params = [decay_base, d_up, iclr_base, i_up, g_up, kns, kis, e_hd, e_dh]
    lw, k2, kk, a, g = _rowwise(_fn_prep, prep_rows, prep_params, [(D, F32)] * 5, name="rwkv_prep", tm=256)
    y, states, invs = _wkv_fwd(z_rkv, lw, k2, kk, a, seq)
    post_rows = [_win(y), _win(z_rkv, 0, D), _win(k2), _win(z_rkv, 2 * D, D), _win(g)]
    post_params = [lnx_w, lnx_b, bonus, e_hd, e_dh]
    (y_a,) = _rowwise(_fn_post, post_rows, post_params, [(D, BF16)], name="rwkv_post", tm=256)
    bias = _mm(wt["rel_bias"].astype(F32), onehot, name="mm_bias", split_a=3).reshape(N_HEADS, CHUNK, BAND)
    y_b = _attn_fwd(proj, bias, seq)
    wt = {**wt, **comm.late_weights(y_b)}
    ya_p = _mm(y_a, wt["w_branch_a"], name="mm_a")
    yb_p = _mm(y_b, wt["w_branch_b"], name="mm_b")
    mix_rows = [_win(proj, C_GA, D), _win(proj, C_GA + D, D), _win(ya_p), _win(yb_p)]
    (mixed,) = _rowwise(_fn_mix, mix_rows, [], [(D, BF16)], name="gate_mix", tm=512)
    mo = _mm(mixed, wt["w_out"], name="mm_out")
    x1, h2 = _rowwise(_fn_res_pre, [_win(x), _win(mo)], [g_post_mix, g_pre_cross], [(D, F32), (D, BF16)],
                      name="res_mix", tm=512)
    qm = _mm(h2, wt["w_q_mem"], name="mm_q")
    (mn,) = _rowwise(_fn_pre, [_win(mem)], [g_mem], [(D, BF16)], name="pre_mem", tm=512)
    kvm = _mm(mn, wt["w_kv_mem"], name="mm_kv")
    om = _xattn_fwd(qm, kvm, seq, n_mem)
    co = _mm(om, wt["w_o_mem"], name="mm_o")
    x2, h3 = _rowwise(_fn_res_pre, [_win(x1), _win(co)], [g_post_cross, g_pre_ffn], [(D, F32), (D, BF16)],
                      name="res_cross", tm=512)
    gu = _mm(h3, wt["w_ffn_in"], name="mm_ffn_in")
    (act,) = _rowwise(_fn_swiglu, [_win(gu, 0, FFN), _win(gu, FFN, FFN)], [], [(FFN, BF16)], name="swiglu", tm=256)
    ff = _mm(act, wt["w_ffn_out"], name="mm_ffn_out")
    (x3,) = _rowwise(_fn_res, [_win(x2), _win(ff)], [g_post_ffn], [(D, F32)], name="res_ffn", tm=512)
    dx3, loss = _loss_head(x3, target)

    gw = {}
    (dx2, dff), (gw["g_post_ffn"],) = _rowwise_bwd(
        _fn_res, [_win(x2), _win(ff)], [g_post_ffn], 0, [[dx3]], name="res_ffn_bwd", tm=256, row_grad=[F32, BF16])
    dact = _mm(dff, wt["w_ffn_out"], tb=True, name="mm_ffn_out_dx", out_dtype=BF16)
    gw["w_ffn_out"] = _mm(act, dff, ta=True, name="mm_ffn_out_dw")
    (dgate, dup), _ = _rowwise_bwd(_fn_swiglu, [_win(gu, 0, FFN), _win(gu, FFN, FFN)], [], 0, [[dact]],
                                   name="swiglu_bwd", tm=256, row_grad=[BF16, BF16])
    dgu = jnp.concatenate([dgate, dup], axis=1)
    dh3 = _mm(dgu, wt["w_ffn_in"], tb=True, name="mm_ffn_in_dx", out_dtype=BF16)
    gw["w_ffn_in"] = _mm(h3, dgu, ta=True, name="mm_ffn_in_dw")
    (dx1, dco), (gw["g_post_cross"], gw["g_pre_ffn"]) = _rowwise_bwd(
        _fn_res_pre, [_win(x1), _win(co)], [g_post_cross, g_pre_ffn], 0, [[dx2], [dh3]],
        name="res_cross_bwd", tm=256, row_grad=[F32, BF16])
    dom = _mm(dco, wt["w_o_mem"], tb=True, name="mm_o_dx", out_dtype=BF16)
    gw["w_o_mem"] = _mm(om, dco, ta=True, name="mm_o_dw")
    dqm, dkvm = _xattn_bwd(qm, kvm, dom, seq, n_mem)
    dh2 = _mm(dqm, wt["w_q_mem"], tb=True, name="mm_q_dx", out_dtype=BF16)
    gw["w_q_mem"] = _mm(h2, dqm, ta=True, name="mm_q_dw")
    dmn = _mm(dkvm, wt["w_kv_mem"], tb=True, name="mm_kv_dx", out_dtype=BF16)
    gw["w_kv_mem"] = _mm(mn, dkvm, ta=True, name="mm_kv_dw")
    _, (gw["g_mem"],) = _rowwise_bwd(_fn_pre, [_win(mem)], [g_mem], 0, [[dmn]], name="pre_mem_bwd", tm=256,
                                     row_grad=[None])
    (dx0, dmo), (gw["g_post_mix"], gw["g_pre_cross"]) = _rowwise_bwd(
        _fn_res_pre, [_win(x), _win(mo)], [g_post_mix, g_pre_cross], 0, [[dx1], [dh2]],
        name="res_mix_bwd", tm=256, row_grad=[F32, BF16])
    dmixed = _mm(dmo, wt["w_out"], tb=True, name="mm_out_dx", out_dtype=BF16)
    gw["w_out"] = _mm(mixed, dmo, ta=True, name="mm_out_dw")
    (dzga, dzgb, dya_p, dyb_p), _ = _rowwise_bwd(_fn_mix, mix_rows, [], 0, [[dmixed]], name="gate_mix_bwd", tm=256,
                                                 row_grad=[BF16] * 4)
    gw["w_branch_a"] = _mm(y_a, dya_p, ta=True, name="mm_a_dw")
    gw["w_branch_b"] = _mm(y_b, dyb_p, ta=True, name="mm_b_dw")
    token = comm.send_early(gw)
    dy_a = _mm(dya_p, wt["w_branch_a"], tb=True, name="mm_a_dx", out_dtype=BF16, after=token)
    dy_b = _mm(dyb_p, wt["w_branch_b"], tb=True, name="mm_b_dx", out_dtype=BF16, after=token)
    dq, dk, dv, dbias = _attn_bwd(proj, bias, dy_b, seq)
    gw["rel_bias"] = _mm(dbias.reshape(N_HEADS, CHUNK * BAND), onehot, tb=True, name="mm_bias_dw", split_a=2)
    (dy, dr_p, dk2_p, dv_p, dg), (gw["lnx_w"], gw["lnx_b"], gw["bonus_scale"]) = _rowwise_bwd(
        _fn_post, post_rows, post_params, 2, [[dy_a]], name="rwkv_post_bwd", tm=128, row_grad=[F32] * 5)
    dr_s, dlw, dk2_s, dv_s, dkk, da = _wkv_bwd(z_rkv, lw, k2, kk, a, states, invs, dy, seq)
    (dzk, dzw, dza, dzg), pg = _rowwise_bwd(
        _fn_prep, prep_rows, prep_params, 2, [[dlw], [dk2_p, dk2_s], [dkk], [da], [dg]],
        name="rwkv_prep_bwd", tm=128, row_grad=[F32] * 4)
    gw["decay_base"], gd_up, gw["iclr_base"], gi_up, gg_up, gw["key_norm_scale"], gw["key_iclr_scale"] = pg
    gw["decay_up"], gw["iclr_up"], gw["gate_up"] = gd_up[:LORA_W], gi_up[:LORA_A], gg_up[:LORA_G]
    dp_r, gmix_r = _shift_bwd(proj, 0, D, mix_rkv[:, :D], [dr_p, dr_s], seq, name="shift_r_bwd")
    dp_k, gmix_k = _shift_bwd(proj, D, D, mix_rkv[:, D:2 * D], [dzk], seq, name="shift_k_bwd")
    dp_v, gmix_v = _shift_bwd(proj, 2 * D, D, mix_rkv[:, 2 * D:], [dv_p, dv_s], seq, name="shift_v_bwd")
    dp_lora, gmix_lora = _shift_bwd(proj, C_LORA, 512, mix_lora, [jnp.concatenate([dzw, dza, dzg], axis=1)], seq,
                                    name="shift_lora_bwd")
    gw["shift_mix"] = jnp.concatenate([gmix_r, gmix_k, gmix_v, _unpad_lora(gmix_lora)], axis=1)
    dproj = jnp.concatenate([dp_r, dp_k, dp_v, dq, dk, dv, dzga, dzgb, dp_lora], axis=1)
    gw["w_in_p"] = _mm(h1, dproj, ta=True, name="mm_in_dw")
    token = comm.send_late(gw)
    dh1 = _mm(dproj, w_in, tb=True, name="mm_in_dx", out_dtype=BF16, after=token)
    (grad_x,), (gw["g_pre_mix"],) = _rowwise_bwd(_fn_pre, [_win(x)], [g_pre_mix], 0, [[dh1]], name="pre_mix_bwd",
                                                 tm=256, row_grad=[F32], add_to={0: dx0})
    return loss, grad_x, gw


_COL_SHARDED = ("w_in", "decay_up", "iclr_up", "gate_up", "w_o_mem", "w_ffn_in")
_ROW_SHARDED = ("w_branch_a", "w_branch_b", "w_out", "w_q_mem", "w_kv_mem", "w_ffn_out")
_FIRST = ("w_in", "decay_up", "iclr_up", "gate_up")
_REST = ("w_o_mem", "w_ffn_in", "w_branch_a", "w_branch_b", "w_out", "w_q_mem", "w_kv_mem", "w_ffn_out")
_REPLICATED = ("g_pre_mix", "g_post_mix", "shift_mix", "decay_base", "iclr_base", "key_norm_scale", "key_iclr_scale",
               "bonus_scale", "lnx_w", "lnx_b", "rel_bias", "g_pre_cross", "g_post_cross", "g_mem", "g_pre_ffn",
               "g_post_ffn")
_WEIGHTS = ("g_pre_mix", "g_post_mix", "w_in", "shift_mix", "decay_base", "decay_up", "iclr_base", "iclr_up", "gate_up",
            "key_norm_scale", "key_iclr_scale", "bonus_scale", "lnx_w", "lnx_b", "rel_bias", "w_branch_a", "w_branch_b",
            "w_out", "g_pre_cross", "g_post_cross", "g_mem", "w_q_mem", "w_kv_mem", "w_o_mem", "g_pre_ffn", "g_post_ffn",
            "w_ffn_in", "w_ffn_out")
_PACK_ROWS = 8 * ((sum({"shift_mix": 3360, "bonus_scale": 1024, "rel_bias": 3072}.get(n, D) for n in _REPLICATED)
                   + 1 + 8 * LANE - 1) // (8 * LANE))


def _pack(vals):
    flat = jnp.concatenate([v.reshape(-1).astype(F32) for v in vals])
    return jnp.pad(flat, (0, _PACK_ROWS * LANE - flat.shape[0])).reshape(_PACK_ROWS, LANE)


def _unpack(packed, shapes):
    flat, out, pos = packed.reshape(-1), [], 0
    for s in shapes:
        n = math.prod(s)
        out.append(flat[pos:pos + n].reshape(s))
        pos += n
    return out


def _step(args, seq, n_mem):
    names = ("x", "mem") + _WEIGHTS + ("loss_target",) + tuple("m_" + n for n in _WEIGHTS) + tuple("v_" + n for n in _WEIGHTS)
    given = dict(zip(names, args))
    nb = given["x"].shape[0]
    x = given["x"].reshape(nb * seq, D)
    mem = given["mem"].reshape(nb * n_mem, D)
    target = given["loss_target"].reshape(nb * seq, D)
    shard = {n: given[n][0] for n in _COL_SHARDED + _ROW_SHARDED}
    out = {}

    def full(name, g):
        return _cols_to_full(g) if name in _COL_SHARDED else g.reshape(-1, g.shape[-1])

    def blocks_of(name, g):
        return (_full_to_cols(g) if name in _COL_SHARDED else g.reshape((N_DEV,) + shard[name].shape)).astype(BF16)

    def update(names, landed):
        res = None
        for n, parts in zip(names, landed):
            res = _adamw(parts, shard[n], given["m_" + n][0], given["v_" + n][0], name="adamw_" + n)
            for kind, r in zip(("grad_", "delta_", "new_m_", "new_v_"), res):
                out[kind + n] = r[None]
        return res[0]

    class Exchanges:
        def __init__(self):
            first = _exchange([shard[n].astype(BF16) for n in _FIRST], False, name="gather_first")
            self.first = {n: full(n, g) for n, g in zip(_FIRST, first)}
            self.rest, self.first_token = _exchange_start(
                [shard[n].astype(BF16) for n in _REST], False, first[0], name="gather_rest_start")

        def late_weights(self, after):
            got = _exchange_wait(self.rest, [after], [shard[n] for n in _REST], name="gather_rest_wait")
            return {n: full(n, g) for n, g in zip(_REST, got)}

        def send_early(self, gw):
            self.early_blocks = [blocks_of(n, gw[n]) for n in _REST]
            self.early, token = _exchange_start(self.early_blocks, True, self.early_blocks[-1], name="scatter_rest_start")
            return token

        def send_late(self, gw):
            me = _flat(_mesh_pos())
            own = [lax.dynamic_index_in_dim(b, me, 0, keepdims=False) for b in self.early_blocks]
            landed = _exchange_wait(self.early, [gw["w_in_p"]], own, name="scatter_rest_wait")
            self.updated = update(_REST, landed)
            grads = {**gw, "w_in": _unpermute_in(gw["w_in_p"])}
            self.late_blocks = [blocks_of(n, grads[n]) for n in _FIRST]
            self.late, token = _exchange_start(self.late_blocks, True, landed[0], name="scatter_first_start")
            return token

        def finish(self, after):
            me = _flat(_mesh_pos())
            own = [lax.dynamic_index_in_dim(b, me, 0, keepdims=False) for b in self.late_blocks]
            update(_FIRST, _exchange_wait(self.late, [after, self.updated], own, name="scatter_first_wait"))

    comm = Exchanges()
    wt = {n: given[n][0] for n in _REPLICATED}
    wt.update({n: comm.first[n] for n in _FIRST[1:]})
    wt["w_in_p"] = _permute_in(comm.first["w_in"])
    loss_tile, grad_x, gw = _local_step(x, mem, target, wt, seq, n_mem, comm)
    comm.finish(grad_x)
    rep_shapes = [given[n].shape for n in _REPLICATED]
    small = _exchange([_pack([gw[n] for n in _REPLICATED] + [loss_tile[0, 0]])], False, name="gather_small")[0]
    zero = jnp.zeros((), F32)
    res = _adamw(small, *[_pack([given[p + n] for n in _REPLICATED] + [zero]) for p in ("", "m_", "v_")],
                 name="adamw_small", tr=_PACK_ROWS)
    for kind, r in zip(("grad_", "delta_", "new_m_", "new_v_"), res):
        for n, val in zip(_REPLICATED, _unpack(r, rep_shapes)):
            out[kind + n] = val
    loss = res[0].reshape(-1)[sum(math.prod(s) for s in rep_shapes)]
    grad_x = grad_x.reshape(nb, seq, D)
    return (loss, grad_x, *[out[k + n] for k in ("grad_", "delta_", "new_m_", "new_v_") for n in _WEIGHTS])


def kernel(x, mem, g_pre_mix, g_post_mix, w_in, shift_mix, decay_base, decay_up, iclr_base, iclr_up, gate_up, key_norm_scale, key_iclr_scale, bonus_scale, lnx_w, lnx_b, rel_bias, w_branch_a, w_branch_b, w_out, g_pre_cross, g_post_cross, g_mem, w_q_mem, w_kv_mem, w_o_mem, g_pre_ffn, g_post_ffn, w_ffn_in, w_ffn_out, loss_target, m_g_pre_mix, m_g_post_mix, m_w_in, m_shift_mix, m_decay_base, m_decay_up, m_iclr_base, m_iclr_up, m_gate_up, m_key_norm_scale, m_key_iclr_scale, m_bonus_scale, m_lnx_w, m_lnx_b, m_rel_bias, m_w_branch_a, m_w_branch_b, m_w_out, m_g_pre_cross, m_g_post_cross, m_g_mem, m_w_q_mem, m_w_kv_mem, m_w_o_mem, m_g_pre_ffn, m_g_post_ffn, m_w_ffn_in, m_w_ffn_out, v_g_pre_mix, v_g_post_mix, v_w_in, v_shift_mix, v_decay_base, v_decay_up, v_iclr_base, v_iclr_up, v_gate_up, v_key_norm_scale, v_key_iclr_scale, v_bonus_scale, v_lnx_w, v_lnx_b, v_rel_bias, v_w_branch_a, v_w_branch_b, v_w_out, v_g_pre_cross, v_g_post_cross, v_g_mem, v_w_q_mem, v_w_kv_mem, v_w_o_mem, v_g_pre_ffn, v_g_post_ffn, v_w_ffn_in, v_w_ffn_out):
    args = (x, mem, g_pre_mix, g_post_mix, w_in, shift_mix, decay_base, decay_up, iclr_base, iclr_up, gate_up, key_norm_scale, key_iclr_scale, bonus_scale, lnx_w, lnx_b, rel_bias, w_branch_a, w_branch_b, w_out, g_pre_cross, g_post_cross, g_mem, w_q_mem, w_kv_mem, w_o_mem, g_pre_ffn, g_post_ffn, w_ffn_in, w_ffn_out, loss_target, m_g_pre_mix, m_g_post_mix, m_w_in, m_shift_mix, m_decay_base, m_decay_up, m_iclr_base, m_iclr_up, m_gate_up, m_key_norm_scale, m_key_iclr_scale, m_bonus_scale, m_lnx_w, m_lnx_b, m_rel_bias, m_w_branch_a, m_w_branch_b, m_w_out, m_g_pre_cross, m_g_post_cross, m_g_mem, m_w_q_mem, m_w_kv_mem, m_w_o_mem, m_g_pre_ffn, m_g_post_ffn, m_w_ffn_in, m_w_ffn_out, v_g_pre_mix, v_g_post_mix, v_w_in, v_shift_mix, v_decay_base, v_decay_up, v_iclr_base, v_iclr_up, v_gate_up, v_key_norm_scale, v_key_iclr_scale, v_bonus_scale, v_lnx_w, v_lnx_b, v_rel_bias, v_w_branch_a, v_w_branch_b, v_w_out, v_g_pre_cross, v_g_post_cross, v_g_mem, v_w_q_mem, v_w_kv_mem, v_w_o_mem, v_g_pre_ffn, v_g_post_ffn, v_w_ffn_in, v_w_ffn_out)
    return _step(args, x.shape[1], mem.shape[1])
```

```python
import functools
import math

import jax
import jax.numpy as jnp
from jax import lax
from jax.experimental import pallas as pl
from jax.experimental.pallas import tpu as pltpu

F32 = jnp.float32
BF16 = jnp.bfloat16

N_DEV = 8
D = 1024
HEAD = 64
N_HEADS = D // HEAD
LANE = 128
N_PAIRS = D // LANE
CHUNK = 64
LEFT = 8 * CHUNK
BAND = LEFT + CHUNK
REL_CLIP = 128
REL_TABLE = CHUNK + REL_CLIP
MEM_WIDTH = D // 2
MEM_HEADS = 4
FFN = 2816
LORA_W, LORA_A, LORA_G = 64, 64, 160
P_WIDTH = 3 * D + 3 * D + 2 * D + 128 + 128 + 256
C_Q, C_GA, C_LORA = 3 * D, 6 * D, 8 * D
NORM_EPS = 1e-6
GROUP_NORM_EPS = 64e-5
MASK_VALUE = -1e30
ADAM_LR, ADAM_B1, ADAM_B2, ADAM_EPS, ADAM_WD, ADAM_STEP = 0.001, 0.9, 0.999, 1e-08, 0.01, 10
VMEM_LIMIT = 56 * 1024 * 1024


def _cp(*sem):
    return pltpu.CompilerParams(dimension_semantics=sem, vmem_limit_bytes=VMEM_LIMIT)


_NN, _NT, _TN = ((1,), (0,)), ((1,), (1,)), ((0,), (0,))


def _dot_raw(a, b, dims):
    return lax.dot_general(a.astype(BF16), b.astype(BF16), (dims, ((), ())), preferred_element_type=F32)


@functools.partial(jax.custom_vjp, nondiff_argnums=(2,))
def _dot_dims(a, b, dims):
    return _dot_raw(a, b, dims)


def _dot_dims_fwd(a, b, dims):
    return _dot_raw(a, b, dims), (a, b)


def _dot_dims_bwd(dims, res, g):
    a, b = res
    if dims == _NN:
        da, db = _dot_raw(g, b, _NT), _dot_raw(a, g, _TN)
    elif dims == _NT:
        da, db = _dot_raw(g, b, _NN), _dot_raw(g, a, _TN)
    else:
        da, db = _dot_raw(b, g, _NT), _dot_raw(a, g, _NN)
    return da.astype(a.dtype), db.astype(b.dtype)


_dot_dims.defvjp(_dot_dims_fwd, _dot_dims_bwd)


def _dot(a, b, dims=_NN):
    return _dot_dims(a, b, dims)


def _dot_nt(a, b):
    return _dot_dims(a, b, _NT)


def _dot_tn(a, b):
    return _dot_dims(a, b, _TN)


def _split(x, terms):
    parts, rest = [], x.astype(F32)
    for _ in range(terms):
        p = rest.astype(BF16)
        parts.append(p)
        rest = rest - p.astype(F32)
    return parts


def _dot_split_a(a, b, terms=2):
    out = None
    for p in _split(a, terms):
        t = _dot(p, b)
        out = t if out is None else out + t
    return out


def _dot_split_b(a, b, terms=3):
    out = None
    for p in _split(b, terms):
        t = _dot(a, p)
        out = t if out is None else out + t
    return out


def _dot_hi(a, b, dims=_NN):
    ah, al = _split(a, 2)
    bh, bl = _split(b, 2)
    return _dot(ah, bh, dims) + (_dot(ah, bl, dims) + _dot(al, bh, dims))


MM_VMEM_BUDGET = 30 * 1024 * 1024
MM_HBM_BPS = 3.2e12
MM_MXU_FPS = 8.5e14
MM_STEP_S = 0.35e-6


def _divisors(n, align, cap):
    out = [d for d in range(align, min(n, cap) + 1, align) if n % d == 0]
    return out or [n]


def _mm_tiles(m, n, k, ea, eb, eo, ta):
    best = None
    for tm in _divisors(m, LANE if ta else 8, 2048):
        for tn in _divisors(n, LANE, 2048):
            for tk in _divisors(k, LANE, 2048):
                nk = k // tk
                vmem = 2 * (tm * tk * ea + tk * tn * eb + tm * tn * eo) + (tm * tn * 4 if nk > 1 else 0)
                if vmem > MM_VMEM_BUDGET:
                    continue
                dma = (tm * tk * ea if (nk > 1 or n // tn == 1) else tm * tk * ea * tn / n) + tk * tn * eb + tm * tn * eo / nk
                step = max(2.0 * tm * tn * tk / MM_MXU_FPS, dma / MM_HBM_BPS) + MM_STEP_S
                cost = (m // tm) * (n // tn) * nk * step
                if best is None or cost < best[0]:
                    best = (cost, tm, tn, tk)
    return best[1:]


def _mm(a, b, *, name, ta=False, tb=False, out_dtype=F32, tm=None, tn=None, tk=None, split_a=1, after=None):
    m, k = (a.shape[1], a.shape[0]) if ta else a.shape
    n, kb = (b.shape[0], b.shape[1]) if tb else (b.shape[1], b.shape[0])
    assert k == kb, (a.shape, b.shape, ta, tb)
    if tm is None:
        tm, tn, tk = _mm_tiles(m, n, k, a.dtype.itemsize, b.dtype.itemsize, jnp.dtype(out_dtype).itemsize, ta)
    assert m % tm == 0 and n % tn == 0 and k % tk == 0, (m, n, k, tm, tn, tk)
    nk = k // tk
    dims = ((0 if ta else 1,), (1 if tb else 0,))

    n_after = 0 if after is None else 1

    def body(a_ref, b_ref, *rest):
        o_ref, scratch = rest[n_after], rest[n_after + 1:]
        prod = None
        for p in _split(a_ref[...], split_a) if split_a > 1 else [a_ref[...]]:
            t = _dot_raw(p, b_ref[...], dims)
            prod = t if prod is None else prod + t
        if nk == 1:
            o_ref[...] = prod.astype(o_ref.dtype)
            return
        acc_ref, kk = scratch[0], pl.program_id(2)

        @pl.when(kk == 0)
        def _():
            acc_ref[...] = prod

        @pl.when(kk > 0)
        def _():
            acc_ref[...] += prod

        @pl.when(kk == nk - 1)
        def _():
            o_ref[...] = acc_ref[...].astype(o_ref.dtype)

    a_spec = pl.BlockSpec((tk, tm), lambda i, j, q: (q, i)) if ta else pl.BlockSpec((tm, tk), lambda i, j, q: (i, q))
    b_spec = pl.BlockSpec((tn, tk), lambda i, j, q: (j, q)) if tb else pl.BlockSpec((tk, tn), lambda i, j, q: (q, j))
    return pl.pallas_call(
        body, name=name, grid=(m // tm, n // tn, nk),
        in_specs=[a_spec, b_spec] + [pl.BlockSpec(memory_space=pl.ANY)] * n_after,
        out_specs=pl.BlockSpec((tm, tn), lambda i, j, q: (i, j)),
        out_shape=jax.ShapeDtypeStruct((m, n), out_dtype),
        scratch_shapes=[pltpu.VMEM((tm, tn), F32)] if nk > 1 else [],
        compiler_params=_cp("parallel", "parallel", "arbitrary"),
    )(a, b, *([] if after is None else [after]))


def _win(arr, start=0, width=None):
    width = arr.shape[1] if width is None else width
    assert start % width == 0
    return (arr, start // width, width)


def _row_specs(rows, tm):
    return [pl.BlockSpec((tm, w), functools.partial(lambda i, cb: (i, cb), cb=cb)) for (_, cb, w) in rows]


def _full_spec(p):
    nd = p.ndim
    return pl.BlockSpec(p.shape, lambda i, nd=nd: (0,) * nd)


def _rowwise(fn, rows, params, outs, *, name, tm):
    t = rows[0][0].shape[0]
    tm = min(tm, t)
    assert t % tm == 0
    nr, npar = len(rows), len(params)

    def body(*refs):
        vals = [r[...] for r in refs[:nr + npar]]
        res = fn(*vals)
        for o_ref, r in zip(refs[nr + npar:], res):
            o_ref[...] = r.astype(o_ref.dtype)

    return pl.pallas_call(
        body, name=name, grid=(t // tm,),
        in_specs=_row_specs(rows, tm) + [_full_spec(p) for p in params],
        out_specs=[pl.BlockSpec((tm, w), lambda i: (i, 0)) for (w, _) in outs],
        out_shape=[jax.ShapeDtypeStruct((t, w), dt) for (w, dt) in outs],
        compiler_params=_cp("parallel"),
    )(*[r[0] for r in rows], *params)


def _rowwise_bwd(fn, rows, params, n_const, cots, *, name, tm, row_grad, add_to=None):
    t = rows[0][0].shape[0]
    tm = min(tm, t)
    assert t % tm == 0
    nr, npar = len(rows), len(params)
    ndp = npar - n_const
    add_to = add_to or {}
    add_idx = sorted(add_to)
    flat_cots = [c for group in cots for c in group]
    kept = [i for i in range(nr) if row_grad[i] is not None]

    def body(*refs):
        pos = 0
        row_v = [r[...] for r in refs[pos:pos + nr]]; pos += nr
        par_v = [r[...] for r in refs[pos:pos + npar]]; pos += npar
        cot_v = [r[...] for r in refs[pos:pos + len(flat_cots)]]; pos += len(flat_cots)
        add_v = [r[...] for r in refs[pos:pos + len(add_idx)]]; pos += len(add_idx)
        rg_refs = refs[pos:pos + len(kept)]; pos += len(kept)
        pg_refs = refs[pos:pos + ndp]

        consts = par_v[ndp:]
        res, vjp = jax.vjp(lambda *args: tuple(fn(*args, *consts)), *row_v, *par_v[:ndp])
        cot_in, q = [], 0
        for j, group in enumerate(cots):
            c = None
            for _ in group:
                cv = cot_v[q].astype(F32); q += 1
                c = cv if c is None else c + cv
            c = jnp.zeros(res[j].shape, F32) if c is None else c
            cot_in.append(c.astype(res[j].dtype))
        grads = vjp(tuple(cot_in))
        for ref, i in zip(rg_refs, kept):
            g = grads[i].astype(F32)
            if i in add_to:
                g = g + add_v[add_idx.index(i)].astype(F32)
            ref[...] = g.astype(ref.dtype)

        @pl.when(pl.program_id(0) == 0)
        def _():
            for ref in pg_refs:
                ref[...] = jnp.zeros_like(ref)

        for ref, g in zip(pg_refs, grads[nr:]):
            ref[...] += g.astype(F32)

    cot_specs = [pl.BlockSpec((tm, c.shape[1]), lambda i: (i, 0)) for c in flat_cots]
    add_specs = [pl.BlockSpec((tm, add_to[i].shape[1]), lambda i_: (i_, 0)) for i in add_idx]
    out_specs = [pl.BlockSpec((tm, rows[i][2]), lambda i_: (i_, 0)) for i in kept] + [_full_spec(p) for p in params[:ndp]]
    out_shape = [jax.ShapeDtypeStruct((t, rows[i][2]), row_grad[i]) for i in kept] + [
        jax.ShapeDtypeStruct(p.shape, F32) for p in params[:ndp]]
    res = pl.pallas_call(
        body, name=name, grid=(t // tm,),
        in_specs=_row_specs(rows, tm) + [_full_spec(p) for p in params] + cot_specs + add_specs,
        out_specs=out_specs, out_shape=out_shape,
        compiler_params=_cp("arbitrary"),
    )(*[r[0] for r in rows], *params, *flat_cots, *[add_to[i] for i in add_idx])
    return list(res[:len(kept)]), list(res[len(kept):])


def _rms(x, g):
    xf = x.astype(F32)
    return xf * lax.rsqrt(jnp.mean(xf * xf, axis=-1, keepdims=True) + NORM_EPS) * g


def _softplus(x):
    return jnp.maximum(x, 0.0) + jnp.log(1.0 + jnp.exp(-jnp.abs(x)))


def _fn_pre(x, g):
    return (_rms(x, g).astype(BF16),)


def _fn_res(x, u, g_post):
    return (x + _rms(u, g_post),)


def _fn_res_pre(x, u, g_post, g_pre):
    xn = x + _rms(u, g_post)
    return xn, _rms(xn, g_pre).astype(BF16)


def _fn_mix(zga, zgb, ya, yb):
    return ((jax.nn.sigmoid(zga) * ya + jax.nn.sigmoid(zgb) * yb).astype(BF16),)


def _fn_swiglu(gate, up):
    return ((gate * jax.nn.sigmoid(gate) * up).astype(BF16),)


def _fn_prep(zk, zw, za, zg, decay_base, d_up, iclr_base, i_up, g_up, kns, kis, e_hd, e_dh):
    w_log = -_softplus(-(decay_base + _dot(jnp.tanh(zw), d_up))) - 0.5
    lw = -jnp.exp(w_log)
    a = jax.nn.sigmoid(iclr_base + _dot(za, i_up))
    g = _dot(jax.nn.sigmoid(zg), g_up)
    kn = zk * kns
    ss = _dot_split_a(kn * kn, e_dh)
    inv = lax.rsqrt(jnp.maximum(ss, 1e-24))
    kk = kn * _dot_split_a(inv, e_hd)
    k2 = zk * (1.0 + (a - 1.0) * kis)
    return lw, k2, kk, a, g


def _fn_post(y, r, k2, v, g, lnx_w, lnx_b, bonus, e_hd, e_dh):
    mu = _dot_split_a(_dot_split_a(y, e_dh) * (1.0 / HEAD), e_hd)
    yc = y - mu
    var = _dot_split_a(yc * yc, e_dh) * (1.0 / HEAD)
    yn = yc * _dot_split_a(lax.rsqrt(var + GROUP_NORM_EPS), e_hd)
    bs = _dot_split_a(_dot_split_a(r * k2 * bonus, e_dh), e_hd)
    return (((yn * lnx_w + lnx_b + bs * v) * g).astype(BF16),)


def _shift_fwd(p, col0, ncols, mix, seq, *, name, cw=256):
    t = p.shape[0]
    assert col0 % cw == 0 and ncols % cw == 0 and t % seq == 0
    cb0 = col0 // cw

    def body(p_ref, m_ref, z_ref):
        pv = p_ref[...]
        row = lax.broadcasted_iota(jnp.int32, pv.shape, 0)
        prev = jnp.where(row == 0, 0.0, pltpu.roll(pv, 1, axis=0))
        z_ref[...] = pv + (prev - pv) * m_ref[...]

    return pl.pallas_call(
        body, name=name, grid=(t // seq, ncols // cw),
        in_specs=[pl.BlockSpec((seq, cw), lambda b, c: (b, c + cb0)), pl.BlockSpec((1, cw), lambda b, c: (0, c))],
        out_specs=pl.BlockSpec((seq, cw), lambda b, c: (b, c)),
        out_shape=jax.ShapeDtypeStruct((t, ncols), F32),
        compiler_params=_cp("parallel", "parallel"),
    )(p, mix)


def _shift_bwd(p, col0, ncols, mix, dz_parts, seq, *, name, cw=256):
    t = p.shape[0]
    cb0 = col0 // cw
    n = len(dz_parts)

    def body(*refs):
        p_ref, m_ref = refs[:2]
        dp_ref, dm_ref = refs[2 + n:]
        dz = refs[2][...].astype(F32)
        for r in refs[3:2 + n]:
            dz = dz + r[...].astype(F32)
        pv = p_ref[...]
        mixv = m_ref[...]
        row = lax.broadcasted_iota(jnp.int32, pv.shape, 0)
        prev = jnp.where(row == 0, 0.0, pltpu.roll(pv, 1, axis=0))
        u = dz * mixv
        nxt = jnp.where(row == seq - 1, 0.0, pltpu.roll(u, seq - 1, axis=0))
        dp_ref[...] = (dz - u + nxt).astype(dp_ref.dtype)

        @pl.when(pl.program_id(1) == 0)
        def _():
            dm_ref[...] = jnp.zeros_like(dm_ref)

        dm_ref[...] += jnp.sum(dz * (prev - pv), axis=0, keepdims=True)

    return pl.pallas_call(
        body, name=name, grid=(ncols // cw, t // seq),
        in_specs=[pl.BlockSpec((seq, cw), lambda c, b: (b, c + cb0)), pl.BlockSpec((1, cw), lambda c, b: (0, c))]
        + [pl.BlockSpec((seq, cw), lambda c, b: (b, c))] * n,
        out_specs=[pl.BlockSpec((seq, cw), lambda c, b: (b, c)), pl.BlockSpec((1, cw), lambda c, b: (0, c))],
        out_shape=[jax.ShapeDtypeStruct((t, ncols), BF16), jax.ShapeDtypeStruct((1, ncols), F32)],
        compiler_params=_cp("parallel", "arbitrary"),
    )(p, mix, *dz_parts)


def _each(f, *lists):
    return [f(*xs) for xs in zip(*lists)]


def _tri_inv(low):
    c = low[0].shape[0]
    ti = lax.broadcasted_iota(jnp.int32, (c, c), 0)
    si = lax.broadcasted_iota(jnp.int32, (c, c), 1)
    eye = (ti == si).astype(F32)
    inside = (ti // 4) == (si // 4)
    base = [jnp.where(inside, m, 0.0) for m in low]
    acc = _each(lambda m: _dot(eye - m, eye + _dot(m, m)), base)
    size = 8
    while size <= c:
        wider = (ti // size) == (si // size)
        keep = jnp.logical_and(wider, jnp.logical_not(inside))
        acc = _each(lambda p, m: p - _dot(_dot(p, jnp.where(keep, m, 0.0)), p), acc, low)
        inside, size = wider, size * 2
    return acc


@jax.custom_vjp
def _tri_inv_known(low, inv):
    return inv


def _tri_inv_known_fwd(low, inv):
    return inv, inv


def _tri_inv_known_bwd(inv, g):
    dlow = _each(lambda t, gg: -_dot(_dot(t, gg, _TN), t, _NT), inv, g)
    return dlow, _each(jnp.zeros_like, inv)


_tri_inv_known.defvjp(_tri_inv_known_fwd, _tri_inv_known_bwd)


def _wkv_chunk(s0, r, lw, k, v, kk, a, inv=None):
    c = r[0].shape[0]
    ti = lax.broadcasted_iota(jnp.int32, (c, c), 0)
    si = lax.broadcasted_iota(jnp.int32, (c, c), 1)
    incl, strict = ti >= si, ti > si
    tri = incl.astype(F32)
    cum = _each(lambda x: _dot_split_b(tri, x, 3), lw)
    eg = _each(jnp.exp, cum)
    egp = _each(lambda cs, x: jnp.exp(cs - x), cum, lw)
    ei = _each(lambda cs: jnp.exp(-cs), cum)
    rh, kkh, kt = _each(jnp.multiply, r, eg), _each(jnp.multiply, kk, egp), _each(jnp.multiply, k, ei)
    bt = _each(lambda p, q, e: (p * q) * e, a, kk, ei)
    lb = _each(lambda p, q: jnp.where(strict, _dot_nt(p, q), 0.0), kkh, bt)
    lk = _each(lambda p, q: jnp.where(strict, _dot_nt(p, q), 0.0), kkh, kt)
    mb = _each(lambda p, q: jnp.where(incl, _dot_nt(p, q), 0.0), rh, bt)
    mk = _each(lambda p, q: jnp.where(incl, _dot_nt(p, q), 0.0), rh, kt)
    rhs = _each(lambda p, s, m, x: _dot_nt(p, s) + _dot(m, x), kkh, s0, lk, v)
    inv = _tri_inv(lb) if inv is None else _tri_inv_known(lb, inv)
    u = _each(lambda t, x: -_dot(t, x), inv, rhs)
    y = _each(lambda p, s, m1, uu, m2, x: _dot_nt(p, s) + _dot(m1, uu) + _dot(m2, x), rh, s0, mb, u, mk, v)
    s1 = _each(lambda s, uu, b, x, kq, w: (s + _dot_tn(uu, b) + _dot_tn(x, kq)) * jnp.exp(jnp.sum(w, axis=0, keepdims=True)),
               s0, u, bt, v, kt, lw)
    return y, s1, inv


WKV_HEADS = 16
WKV_COLS = WKV_HEADS * HEAD
WKV_GROUPS = N_HEADS // WKV_HEADS


def _head_cols(ref):
    return [ref[:, h * HEAD:(h + 1) * HEAD] for h in range(ref.shape[1] // HEAD)]


def _wkv_specs(seq, rev):
    nc = seq // CHUNK

    def rows(col0):
        cb0 = col0 // WKV_COLS
        if rev:
            return pl.BlockSpec((CHUNK, WKV_COLS), lambda b, h, c: (b * nc + nc - 1 - c, cb0 + h))
        return pl.BlockSpec((CHUNK, WKV_COLS), lambda b, h, c: (b * nc + c, cb0 + h))

    if rev:
        st = pl.BlockSpec((1, 1, WKV_HEADS, HEAD, HEAD), lambda b, h, c: (b * WKV_GROUPS + h, nc - 1 - c, 0, 0, 0))
    else:
        st = pl.BlockSpec((1, 1, WKV_HEADS, HEAD, HEAD), lambda b, h, c: (b * WKV_GROUPS + h, c, 0, 0, 0))
    return rows, st


def _wkv_fwd(z_rkv, lw, k2, kk, a, seq):
    t = z_rkv.shape[0]
    nb, nc = t // seq, seq // CHUNK
    rows, st = _wkv_specs(seq, False)

    def body(r_ref, v_ref, lw_ref, k_ref, kk_ref, a_ref, y_ref, st_ref, inv_ref, s_scr):
        @pl.when(pl.program_id(2) == 0)
        def _():
            s_scr[...] = jnp.zeros_like(s_scr)

        s0 = [s_scr[h] for h in range(WKV_HEADS)]
        y, s1, inv = _wkv_chunk(s0, *[_head_cols(ref) for ref in (r_ref, lw_ref, k_ref, v_ref, kk_ref, a_ref)])
        for h in range(WKV_HEADS):
            st_ref[0, 0, h] = s0[h]
            inv_ref[0, 0, h] = inv[h]
            y_ref[:, h * HEAD:(h + 1) * HEAD] = y[h]
            s_scr[h] = s1[h]

    per_chunk = jax.ShapeDtypeStruct((nb * WKV_GROUPS, nc, WKV_HEADS, HEAD, HEAD), F32)
    return pl.pallas_call(
        body, name="wkv_fwd", grid=(nb, WKV_GROUPS, nc),
        in_specs=[rows(0), rows(2 * D), rows(0), rows(0), rows(0), rows(0)],
        out_specs=[rows(0), st, st],
        out_shape=[jax.ShapeDtypeStruct((t, D), F32), per_chunk, per_chunk],
        scratch_shapes=[pltpu.VMEM((WKV_HEADS, HEAD, HEAD), F32)],
        compiler_params=_cp("parallel", "parallel", "arbitrary"),
    )(z_rkv, z_rkv, lw, k2, kk, a)


def _wkv_bwd(z_rkv, lw, k2, kk, a, states, invs, dy, seq):
    t = z_rkv.shape[0]
    nb, nc = t // seq, seq // CHUNK
    rows, st = _wkv_specs(seq, True)

    def body(r_ref, v_ref, lw_ref, k_ref, kk_ref, a_ref, st_ref, inv_ref, dy_ref,
             dr_ref, dlw_ref, dk_ref, dv_ref, dkk_ref, da_ref, ds_scr):
        @pl.when(pl.program_id(2) == 0)
        def _():
            ds_scr[...] = jnp.zeros_like(ds_scr)

        s0 = [st_ref[0, 0, h] for h in range(WKV_HEADS)]
        inv = [inv_ref[0, 0, h] for h in range(WKV_HEADS)]
        _, vjp = jax.vjp(lambda *args: _wkv_chunk(*args, inv=inv)[:2],
                         s0, *[_head_cols(ref) for ref in (r_ref, lw_ref, k_ref, v_ref, kk_ref, a_ref)])
        grads = vjp(([x.astype(F32) for x in _head_cols(dy_ref)], [ds_scr[h] for h in range(WKV_HEADS)]))
        for h in range(WKV_HEADS):
            ds_scr[h] = grads[0][h]
            for ref, g in zip((dr_ref, dlw_ref, dk_ref, dv_ref, dkk_ref, da_ref), grads[1:]):
                ref[:, h * HEAD:(h + 1) * HEAD] = g[h]

    return pl.pallas_call(
        body, name="wkv_bwd", grid=(nb, WKV_GROUPS, nc),
        in_specs=[rows(0), rows(2 * D), rows(0), rows(0), rows(0), rows(0), st, st, rows(0)],
        out_specs=[rows(0)] * 6,
        out_shape=[jax.ShapeDtypeStruct((t, D), F32)] * 6,
        scratch_shapes=[pltpu.VMEM((WKV_HEADS, HEAD, HEAD), F32)],
        compiler_params=_cp("parallel", "parallel", "arbitrary"),
    )(z_rkv, z_rkv, lw, k2, kk, a, states, invs, dy)


def _softmax(s):
    e = jnp.exp(s - jnp.max(s, axis=-1, keepdims=True))
    return e / jnp.sum(e, axis=-1, keepdims=True)


ATT_HEADS = 8
ATT_COLS = ATT_HEADS * HEAD
ATT_GROUPS = N_HEADS // ATT_HEADS


def _attn_chunk(q, kb, vb, bias, valid):
    s = _each(lambda x, y, z: jnp.where(valid, _dot_nt(x, y) * (HEAD ** -0.5) + z, MASK_VALUE), q, kb, bias)
    return _each(_dot, _each(_softmax, s), vb)


def _pad_fill(pad_ref, src_ref):
    pad_ref[0:LEFT, :] = jnp.zeros((LEFT, pad_ref.shape[1]), pad_ref.dtype)
    pad_ref[LEFT:, :] = src_ref[...].astype(pad_ref.dtype)


def _band_heads(pad_ref, start):
    return [pad_ref[pl.ds(start, BAND), h * HEAD:(h + 1) * HEAD].astype(F32) for h in range(ATT_HEADS)]


def _band_valid(c):
    return (c * CHUNK - LEFT + lax.broadcasted_iota(jnp.int32, (1, BAND), 1)) >= 0


def _attn_fwd(proj, bias, seq):
    t = proj.shape[0]
    nb, nc = t // seq, seq // CHUNK
    cq = C_Q // ATT_COLS

    def body(q_ref, k_ref, v_ref, b_ref, o_ref, kpad, vpad):
        c = pl.program_id(2)

        @pl.when(c == 0)
        def _():
            _pad_fill(kpad, k_ref)
            _pad_fill(vpad, v_ref)

        start = pl.multiple_of(c * CHUNK, CHUNK)
        o = _attn_chunk(_head_cols(q_ref), _band_heads(kpad, start), _band_heads(vpad, start),
                        [b_ref[h] for h in range(ATT_HEADS)], _band_valid(c))
        for h in range(ATT_HEADS):
            o_ref[:, h * HEAD:(h + 1) * HEAD] = o[h].astype(o_ref.dtype)

    return pl.pallas_call(
        body, name="attn_fwd", grid=(ATT_GROUPS, nb, nc),
        in_specs=[pl.BlockSpec((CHUNK, ATT_COLS), lambda h, b, c: (b * nc + c, cq + h)),
                  pl.BlockSpec((seq, ATT_COLS), lambda h, b, c: (b, cq + ATT_GROUPS + h)),
                  pl.BlockSpec((seq, ATT_COLS), lambda h, b, c: (b, cq + 2 * ATT_GROUPS + h)),
                  pl.BlockSpec((ATT_HEADS, CHUNK, BAND), lambda h, b, c: (h, 0, 0))],
        out_specs=pl.BlockSpec((CHUNK, ATT_COLS), lambda h, b, c: (b * nc + c, h)),
        out_shape=jax.ShapeDtypeStruct((t, D), BF16),
        scratch_shapes=[pltpu.VMEM((seq + LEFT, ATT_COLS), BF16)] * 2,
        compiler_params=_cp("parallel", "arbitrary", "arbitrary"),
    )(proj, proj, proj, bias)


def _attn_bwd(proj, bias, do, seq):
    t = proj.shape[0]
    nb, nc = t // seq, seq // CHUNK
    cq = C_Q // ATT_COLS

    def body(q_ref, k_ref, v_ref, b_ref, do_ref, dq_ref, dk_ref, dv_ref, db_ref, kpad, vpad, dkpad, dvpad):
        b, c = pl.program_id(1), pl.program_id(2)

        @pl.when(c == 0)
        def _():
            _pad_fill(kpad, k_ref)
            _pad_fill(vpad, v_ref)
            dkpad[...] = jnp.zeros_like(dkpad)
            dvpad[...] = jnp.zeros_like(dvpad)

        @pl.when(jnp.logical_and(b == 0, c == 0))
        def _():
            db_ref[...] = jnp.zeros_like(db_ref)

        start = pl.multiple_of(c * CHUNK, CHUNK)
        _, vjp = jax.vjp(functools.partial(_attn_chunk, valid=_band_valid(c)),
                         _head_cols(q_ref), _band_heads(kpad, start), _band_heads(vpad, start),
                         [b_ref[h] for h in range(ATT_HEADS)])
        dq, dkb, dvb, dbias = vjp([x.astype(F32) for x in _head_cols(do_ref)])
        for h in range(ATT_HEADS):
            sl = slice(h * HEAD, (h + 1) * HEAD)
            dq_ref[:, sl] = dq[h].astype(dq_ref.dtype)
            dkpad[pl.ds(start, BAND), sl] += dkb[h].astype(F32)
            dvpad[pl.ds(start, BAND), sl] += dvb[h].astype(F32)
            db_ref[h] += dbias[h]

        @pl.when(c == nc - 1)
        def _():
            dk_ref[...] = dkpad[LEFT:, :].astype(dk_ref.dtype)
            dv_ref[...] = dvpad[LEFT:, :].astype(dv_ref.dtype)

    kv_out = pl.BlockSpec((seq, ATT_COLS), lambda h, b, c: (b, h))
    return pl.pallas_call(
        body, name="attn_bwd", grid=(ATT_GROUPS, nb, nc),
        in_specs=[pl.BlockSpec((CHUNK, ATT_COLS), lambda h, b, c: (b * nc + c, cq + h)),
                  pl.BlockSpec((seq, ATT_COLS), lambda h, b, c: (b, cq + ATT_GROUPS + h)),
                  pl.BlockSpec((seq, ATT_COLS), lambda h, b, c: (b, cq + 2 * ATT_GROUPS + h)),
                  pl.BlockSpec((ATT_HEADS, CHUNK, BAND), lambda h, b, c: (h, 0, 0)),
                  pl.BlockSpec((CHUNK, ATT_COLS), lambda h, b, c: (b * nc + c, h))],
        out_specs=[pl.BlockSpec((CHUNK, ATT_COLS), lambda h, b, c: (b * nc + c, h)), kv_out, kv_out,
                   pl.BlockSpec((ATT_HEADS, CHUNK, BAND), lambda h, b, c: (h, 0, 0))],
        out_shape=[jax.ShapeDtypeStruct((t, D), BF16)] * 3 + [jax.ShapeDtypeStruct((N_HEADS, CHUNK, BAND), F32)],
        scratch_shapes=[pltpu.VMEM((seq + LEFT, ATT_COLS), BF16)] * 2 + [pltpu.VMEM((seq + LEFT, ATT_COLS), F32)] * 2,
        compiler_params=_cp("parallel", "arbitrary", "arbitrary"),
    )(proj, proj, proj, bias, do)


def _xattn_tile(q, k, v):
    s = _dot_nt(q, k) * ((MEM_WIDTH // MEM_HEADS) ** -0.5)
    return _dot(_softmax(s), v)


def _xattn_fwd(qm, kvm, seq, n_mem, tq=512):
    t = qm.shape[0]
    tq = min(tq, seq)
    nb, nq = t // seq, seq // tq

    def body(q_ref, k_ref, v_ref, o_ref):
        o_ref[...] = _xattn_tile(q_ref[...], k_ref[...], v_ref[...]).astype(o_ref.dtype)

    return pl.pallas_call(
        body, name="xattn_fwd", grid=(nb, MEM_HEADS, nq),
        in_specs=[pl.BlockSpec((tq, LANE), lambda b, h, i: (b * nq + i, h)),
                  pl.BlockSpec((n_mem, LANE), lambda b, h, i: (b, h)),
                  pl.BlockSpec((n_mem, LANE), lambda b, h, i: (b, MEM_HEADS + h))],
        out_specs=pl.BlockSpec((tq, LANE), lambda b, h, i: (b * nq + i, h)),
        out_shape=jax.ShapeDtypeStruct((t, MEM_WIDTH), BF16),
        compiler_params=_cp("parallel", "parallel", "parallel"),
    )(qm, kvm, kvm)


def _xattn_bwd(qm, kvm, do, seq, n_mem, tq=512):
    t = qm.shape[0]
    tq = min(tq, seq)
    nb, nq = t // seq, seq // tq

    def body(q_ref, k_ref, v_ref, do_ref, dq_ref, dkv_ref, dk_acc, dv_acc):
        i = pl.program_id(2)

        @pl.when(i == 0)
        def _():
            dk_acc[...] = jnp.zeros_like(dk_acc)
            dv_acc[...] = jnp.zeros_like(dv_acc)

        _, vjp = jax.vjp(_xattn_tile, q_ref[...], k_ref[...], v_ref[...])
        dq, dk, dv = vjp(do_ref[...].astype(F32))
        dq_ref[...] = dq.astype(dq_ref.dtype)
        dk_acc[...] += dk
        dv_acc[...] += dv

        @pl.when(i == nq - 1)
        def _():
            dkv_ref[0] = dk_acc[...].astype(dkv_ref.dtype)
            dkv_ref[1] = dv_acc[...].astype(dkv_ref.dtype)

    dq, dkv = pl.pallas_call(
        body, name="xattn_bwd", grid=(nb, MEM_HEADS, nq),
        in_specs=[pl.BlockSpec((tq, LANE), lambda b, h, i: (b * nq + i, h)),
                  pl.BlockSpec((n_mem, LANE), lambda b, h, i: (b, h)),
                  pl.BlockSpec((n_mem, LANE), lambda b, h, i: (b, MEM_HEADS + h)),
                  pl.BlockSpec((tq, LANE), lambda b, h, i: (b * nq + i, h))],
        out_specs=[pl.BlockSpec((tq, LANE), lambda b, h, i: (b * nq + i, h)),
                   pl.BlockSpec((2, n_mem, LANE), lambda b, h, i: (0, b, h))],
        out_shape=[jax.ShapeDtypeStruct((t, MEM_WIDTH), BF16), jax.ShapeDtypeStruct((2, nb * n_mem, MEM_WIDTH), BF16)],
        scratch_shapes=[pltpu.VMEM((n_mem, LANE), F32)] * 2,
        compiler_params=_cp("parallel", "parallel", "arbitrary"),
    )(qm, kvm, kvm, do)
    return dq, jnp.concatenate([dkv[0], dkv[1]], axis=1)


def _loss_head(y, target, tm=512):
    t, d = y.shape
    tm = min(tm, t)

    def body(y_ref, t_ref, dy_ref, l_ref):
        @pl.when(pl.program_id(0) == 0)
        def _():
            l_ref[...] = jnp.zeros_like(l_ref)

        diff = y_ref[...] - t_ref[...]
        dy_ref[...] = diff * (1.0 / d)
        l_ref[...] += 0.5 * jnp.sum(jnp.mean(diff * diff, axis=-1, keepdims=True), axis=0, keepdims=True)

    dy, loss = pl.pallas_call(
        body, name="loss_head", grid=(t // tm,),
        in_specs=[pl.BlockSpec((tm, d), lambda i: (i, 0))] * 2,
        out_specs=[pl.BlockSpec((tm, d), lambda i: (i, 0)), pl.BlockSpec((8, LANE), lambda i: (0, 0))],
        out_shape=[jax.ShapeDtypeStruct((t, d), F32), jax.ShapeDtypeStruct((8, LANE), F32)],
        compiler_params=_cp("arbitrary"),
    )(y, target)
    return dy, loss


def _mesh_pos():
    return lax.axis_index("x"), lax.axis_index("y"), lax.axis_index("c")


def _peer(pos, d):
    x, y, c = pos
    return ((1 - x) if d & 4 else x, (1 - y) if d & 2 else y, (1 - c) if d & 1 else c)


def _flat(pos):
    return 4 * pos[0] + 2 * pos[1] + pos[2]


def _exchange(arrays, scatter, *, name):
    n = len(arrays)
    shapes = [a.shape[1:] if scatter else a.shape for a in arrays]

    def body(*refs):
        ins, outs = refs[:n], refs[n:2 * n]
        send, recv, loc = refs[2 * n:]
        pos = _mesh_pos()
        me = _flat(pos)
        pending = []
        for i in range(n):
            own = pltpu.make_async_copy(ins[i].at[me] if scatter else ins[i], outs[i].at[me], loc.at[i])
            own.start()
            pending.append(own)
            for d in range(1, N_DEV):
                peer = _peer(pos, d)
                src = ins[i].at[_flat(peer)] if scatter else ins[i]
                out_cp = pltpu.make_async_remote_copy(
                    src_ref=src, dst_ref=outs[i].at[me], send_sem=send.at[i, d - 1], recv_sem=recv.at[i, d - 1],
                    device_id=peer, device_id_type=pl.DeviceIdType.MESH)
                out_cp.start()
                pending.append(out_cp)
        for i in range(n):
            own = pending[i * N_DEV]
            for d in range(1, N_DEV):
                peer = _peer(pos, d)
                src = ins[i].at[_flat(peer)] if scatter else ins[i]
                pending[i * N_DEV + d].wait_send()
                pltpu.make_async_remote_copy(
                    src_ref=src, dst_ref=outs[i].at[_flat(peer)], send_sem=send.at[i, d - 1], recv_sem=recv.at[i, d - 1],
                    device_id=peer, device_id_type=pl.DeviceIdType.MESH).wait_recv()
            own.wait()

    hbm = pl.BlockSpec(memory_space=pltpu.HBM)
    return pl.pallas_call(
        body, name=name,
        in_specs=[hbm] * n, out_specs=[hbm] * n,
        out_shape=[jax.ShapeDtypeStruct((N_DEV,) + tuple(s), a.dtype) for s, a in zip(shapes, arrays)],
        scratch_shapes=[pltpu.SemaphoreType.DMA((n, N_DEV - 1)), pltpu.SemaphoreType.DMA((n, N_DEV - 1)),
                        pltpu.SemaphoreType.DMA((n,))],
    )(*arrays)


_HBM = pl.BlockSpec(memory_space=pltpu.HBM)
_SEM = pl.BlockSpec(memory_space=pltpu.SEMAPHORE)
_DATAFLOW = pltpu.SideEffectType.DATAFLOW_SIDE_EFFECTING


_ALL_PEERS = tuple(range(1, N_DEV))
_SIBLING_AND_SAME_CORE = (1, 2, 4, 6)


def _remote_copies(ins, lands, send, recv, scatter, dists):
    pos = _mesh_pos()
    me = _flat(pos)
    out = []
    for i in range(len(ins)):
        for j, d in enumerate(dists):
            peer = _peer(pos, d)
            src = ins[i].at[_flat(peer)] if scatter else ins[i]
            pair = i * len(dists) + j
            sems = dict(send_sem=send.at[pair], recv_sem=recv.at[pair], device_id=peer,
                        device_id_type=pl.DeviceIdType.MESH)
            out.append((pltpu.make_async_remote_copy(src_ref=src, dst_ref=lands[i].at[me], **sems),
                        pltpu.make_async_remote_copy(src_ref=src, dst_ref=lands[i].at[_flat(peer)], **sems)))
    return out


def _exchange_start(arrays, scatter, after, *, name, dists=_ALL_PEERS):
    n = len(arrays)
    shapes = [a.shape[1:] if scatter else a.shape for a in arrays]
    lands = [pltpu.with_memory_space_constraint(lax.empty((N_DEV,) + tuple(s), a.dtype), pltpu.HBM)
             for s, a in zip(shapes, arrays)]
    srcs = [pltpu.with_memory_space_constraint(a, pltpu.HBM) for a in arrays]

    def body(*refs):
        ins, land_refs = refs[:n], refs[n:2 * n]
        send, recv, token = refs[2 * n + 1], refs[2 * n + 2], refs[-1]
        for going, _ in _remote_copies(ins, land_refs, send, recv, scatter, dists):
            going.start()
        token[...] = jnp.zeros_like(token)

    sems = pltpu.SemaphoreType.DMA((n * len(dists),))
    res = pl.pallas_call(
        body, name=name,
        out_shape=(sems, sems, *[pltpu.HBM(a.shape, a.dtype) for a in srcs + lands], jax.ShapeDtypeStruct((8, LANE), F32)),
        in_specs=[_HBM] * (2 * n) + [pl.BlockSpec(memory_space=pl.ANY)],
        out_specs=(_SEM, _SEM, *[_HBM] * (2 * n), pl.BlockSpec(memory_space=pltpu.VMEM)),
        input_output_aliases={i: 2 + i for i in range(2 * n)},
        compiler_params=pltpu.CompilerParams(has_side_effects=_DATAFLOW),
    )(*srcs, *lands, after)
    return (n, scatter, dists, res[0], res[1], list(res[2:2 + 2 * n])), res[-1]


def _exchange_wait(handle, after, own, *, name):
    n, scatter, dists, send, recv, thru = handle

    def body(*refs):
        ins, land_refs = refs[:n], refs[n:2 * n]
        for going, coming in _remote_copies(ins, land_refs, refs[2 * n], refs[2 * n + 1], scatter, dists):
            going.wait_send()
            coming.wait_recv()

    res = pl.pallas_call(
        body, name=name,
        out_shape=tuple(pltpu.HBM(a.shape, a.dtype) for a in thru),
        in_specs=[_HBM] * (2 * n) + [_SEM, _SEM] + [pl.BlockSpec(memory_space=pl.ANY)] * len(after),
        out_specs=tuple([_HBM] * (2 * n)),
        input_output_aliases={i: i for i in range(2 * n)},
        compiler_params=pltpu.CompilerParams(has_side_effects=_DATAFLOW),
    )(*thru, send, recv, *after)
    me = _flat(_mesh_pos())
    return [lax.dynamic_update_slice_in_dim(land, o[None].astype(land.dtype), me, 0) for land, o in zip(res[n:], own)]


_OTHER_CHIPS = (2, 4, 6)


def _relay_to_sibling(gathered, *, name):
    n, k = len(gathered), len(_OTHER_CHIPS)

    def body(*refs):
        ins, outs = refs[:n], refs[n:2 * n]
        send, recv = refs[2 * n:]
        pos = _mesh_pos()
        copies = []
        for i in range(n):
            for j, d in enumerate(_OTHER_CHIPS):
                cp = pltpu.make_async_remote_copy(
                    src_ref=ins[i].at[_flat(_peer(pos, d))], dst_ref=outs[i].at[j],
                    send_sem=send.at[i * k + j], recv_sem=recv.at[i * k + j],
                    device_id=_peer(pos, 1), device_id_type=pl.DeviceIdType.MESH)
                cp.start()
                copies.append(cp)
        for cp in copies:
            cp.wait()

    return pl.pallas_call(
        body, name=name, in_specs=[_HBM] * n, out_specs=[_HBM] * n,
        out_shape=[jax.ShapeDtypeStruct((k,) + g.shape[1:], g.dtype) for g in gathered],
        scratch_shapes=[pltpu.SemaphoreType.DMA((n * k,)), pltpu.SemaphoreType.DMA((n * k,))],
    )(*gathered)


def _adamw(parts, w, m, v, *, name, tr=128):
    r, c = w.shape
    align = 8 * 4 // parts.dtype.itemsize
    tr = max(d for d in range(align, min(tr, r) + 1, align) if r % d == 0)

    def body(p_ref, w_ref, m_ref, v_ref, g_ref, d_ref, nm_ref, nv_ref):
        g = p_ref[0].astype(F32)
        for j in range(1, N_DEV):
            g = g + p_ref[j].astype(F32)
        m2 = ADAM_B1 * m_ref[...] + (1.0 - ADAM_B1) * g
        v2 = ADAM_B2 * v_ref[...] + (1.0 - ADAM_B2) * (g * g)
        m_hat = m2 / (1.0 - ADAM_B1 ** ADAM_STEP)
        v_hat = v2 / (1.0 - ADAM_B2 ** ADAM_STEP)
        g_ref[...] = g
        d_ref[...] = -ADAM_LR * (m_hat / (jnp.sqrt(v_hat) + ADAM_EPS) + ADAM_WD * w_ref[...])
        nm_ref[...] = m2
        nv_ref[...] = v2

    spec = pl.BlockSpec((tr, c), lambda i: (i, 0))
    return pl.pallas_call(
        body, name=name, grid=(r // tr,),
        in_specs=[pl.BlockSpec((N_DEV, tr, c), lambda i: (0, i, 0)), spec, spec, spec],
        out_specs=[spec] * 4, out_shape=[jax.ShapeDtypeStruct((r, c), F32)] * 4,
        compiler_params=_cp("parallel"),
    )(parts, w, m, v)


def _cols_to_full(g):
    return jnp.transpose(g, (1, 0, 2)).reshape(g.shape[1], N_DEV * g.shape[2])


def _full_to_cols(w):
    r, c = w.shape
    return jnp.transpose(w.reshape(r, N_DEV, c // N_DEV), (1, 0, 2))


def _pad_cols(a, width):
    return jnp.pad(a, ((0, 0), (0, width - a.shape[1])))


def _pad_lora(w):
    return jnp.concatenate([
        _pad_cols(w[:, :LORA_W], 128), _pad_cols(w[:, LORA_W:LORA_W + LORA_A], 128),
        _pad_cols(w[:, LORA_W + LORA_A:], 256)], axis=1)


def _unpad_lora(wp):
    return jnp.concatenate([wp[:, :LORA_W], wp[:, 128:128 + LORA_A], wp[:, 256:256 + LORA_G]], axis=1)


def _permute_in(w):
    rk = 3 * D
    lo = rk + LORA_W + LORA_A + LORA_G
    return jnp.concatenate([w[:, :rk], w[:, lo:], _pad_lora(w[:, rk:lo])], axis=1)


def _unpermute_in(wp):
    return jnp.concatenate([wp[:, :3 * D], _unpad_lora(wp[:, C_LORA:]), wp[:, 3 * D:C_LORA]], axis=1)


def _rel_index():
    dist = jnp.arange(CHUNK)[:, None] - jnp.arange(BAND)[None, :] + LEFT
    return (jnp.minimum(dist, REL_CLIP) + (CHUNK - 1)).reshape(-1)


def _local_step(x, mem, target, wt, seq, n_mem, comm):
    t = x.shape[0]
    row = lambda a: a.reshape(1, -1).astype(F32)
    g_pre_mix, g_post_mix = row(wt["g_pre_mix"]), row(wt["g_post_mix"])
    g_pre_cross, g_post_cross, g_mem = row(wt["g_pre_cross"]), row(wt["g_post_cross"]), row(wt["g_mem"])
    g_pre_ffn, g_post_ffn = row(wt["g_pre_ffn"]), row(wt["g_post_ffn"])
    mix = row(wt["shift_mix"])
    mix_rkv, mix_lora = mix[:, :3 * D], _pad_lora(mix[:, 3 * D:])
    decay_base, iclr_base = row(wt["decay_base"]), row(wt["iclr_base"])
    kns, kis = row(wt["key_norm_scale"]), row(wt["key_iclr_scale"])
    lnx_w, lnx_b, bonus = row(wt["lnx_w"]), row(wt["lnx_b"]), row(wt["bonus_scale"])
    e_dh = (jnp.arange(D)[:, None] // HEAD == jnp.arange(N_HEADS)[None, :]).astype(F32)
    e_hd = e_dh.T
    onehot = (jnp.arange(REL_TABLE)[:, None] == _rel_index()[None, :]).astype(BF16)

    (h1,) = _rowwise(_fn_pre, [_win(x)], [g_pre_mix], [(D, BF16)], name="pre_mix", tm=512)
    (mn,) = _rowwise(_fn_pre, [_win(mem)], [g_mem], [(D, BF16)], name="pre_mem", tm=512)
    bias = _mm(wt["rel_bias"].astype(F32), onehot, name="mm_bias", split_a=3).reshape(N_HEADS, CHUNK, BAND)
    wt = {**wt, **comm.first_weights([h1, mn, bias])}
    w_in = wt["w_in_p"]
    d_up = jnp.pad(wt["decay_up"].astype(F32), ((0, 128 - LORA_W), (0, 0)))
    i_up = jnp.pad(wt["iclr_up"].astype(F32), ((0, 128 - LORA_A), (0, 0)))
    g_up = jnp.pad(wt["gate_up"].astype(F32), ((0, 256 - LORA_G), (0, 0)))
    proj = _mm(h1, w_in, name="mm_in", after=comm.first_token)
    z_rkv = _shift_fwd(proj, 0, 3 * D, mix_rkv, seq, name="shift_rkv")
    z_lora = _shift_fwd(proj, C_LORA, 512, mix_lora, seq, name="shift_lora")
    prep_rows = [_win(z_rkv, D, D), _win(z_lora, 0, 128), _win(z_lora, 128, 128), _win(z_lora, 256, 256)]
    prep_params = [decay_base, d_up, iclr_base, i_up, g_up, kns, kis, e_hd, e_dh]
    lw, k2, kk, a, g = _rowwise(_fn_prep, prep_rows, prep_params, [(D, F32)] * 5, name="rwkv_prep", tm=256)
    y, states, invs = _wkv_fwd(z_rkv, lw, k2, kk, a, seq)
    post_rows = [_win(y), _win(z_rkv, 0, D), _win(k2), _win(z_rkv, 2 * D, D), _win(g)]
    post_params = [lnx_w, lnx_b, bonus, e_hd, e_dh]
    (y_a,) = _rowwise(_fn_post, post_rows, post_params, [(D, BF16)], name="rwkv_post", tm=256)
    y_b = _attn_fwd(proj, bias, seq)
    wt = {**wt, **comm.late_weights(y_b)}
    ya_p = _mm(y_a, wt["w_branch_a"], name="mm_a")
    yb_p = _mm(y_b, wt["w_branch_b"], name="mm_b")
    mix_rows = [_win(proj, C_GA, D), _win(proj, C_GA + D, D), _win(ya_p), _win(yb_p)]
    (mixed,) = _rowwise(_fn_mix, mix_rows, [], [(D, BF16)], name="gate_mix", tm=512)
    mo = _mm(mixed, wt["w_out"], name="mm_out")
    x1, h2 = _rowwise(_fn_res_pre, [_win(x), _win(mo)], [g_post_mix, g_pre_cross], [(D, F32), (D, BF16)],
                      name="res_mix", tm=512)
    qm = _mm(h2, wt["w_q_mem"], name="mm_q")
    kvm = _mm(mn, wt["w_kv_mem"], name="mm_kv")
    om = _xattn_fwd(qm, kvm, seq, n_mem)
    co = _mm(om, wt["w_o_mem"], name="mm_o")
    x2, h3 = _rowwise(_fn_res_pre, [_win(x1), _win(co)], [g_post_cross, g_pre_ffn], [(D, F32), (D, BF16)],
                      name="res_cross", tm=512)
    gu = _mm(h3, wt["w_ffn_in"], name="mm_ffn_in")
    (act,) = _rowwise(_fn_swiglu, [_win(gu, 0, FFN), _win(gu, FFN, FFN)], [], [(FFN, BF16)], name="swiglu", tm=256)
    ff = _mm(act, wt["w_ffn_out"], name="mm_ffn_out")
    (x3,) = _rowwise(_fn_res, [_win(x2), _win(ff)], [g_post_ffn], [(D, F32)], name="res_ffn", tm=512)
    dx3, loss = _loss_head(x3, target)

    gw = {}
    (dx2, dff), (gw["g_post_ffn"],) = _rowwise_bwd(
        _fn_res, [_win(x2), _win(ff)], [g_post_ffn], 0, [[dx3]], name="res_ffn_bwd", tm=256, row_grad=[F32, BF16])
    dact = _mm(dff, wt["w_ffn_out"], tb=True, name="mm_ffn_out_dx", out_dtype=BF16)
    gw["w_ffn_out"] = _mm(act, dff, ta=True, name="mm_ffn_out_dw")
    (dgate, dup), _ = _rowwise_bwd(_fn_swiglu, [_win(gu, 0, FFN), _win(gu, FFN, FFN)], [], 0, [[dact]],
                                   name="swiglu_bwd", tm=256, row_grad=[BF16, BF16])
    dgu = jnp.concatenate([dgate, dup], axis=1)
    dh3 = _mm(dgu, wt["w_ffn_in"], tb=True, name="mm_ffn_in_dx", out_dtype=BF16)
    gw["w_ffn_in"] = _mm(h3, dgu, ta=True, name="mm_ffn_in_dw")
    (dx1, dco), (gw["g_post_cross"], gw["g_pre_ffn"]) = _rowwise_bwd(
        _fn_res_pre, [_win(x1), _win(co)], [g_post_cross, g_pre_ffn], 0, [[dx2], [dh3]],
        name="res_cross_bwd", tm=256, row_grad=[F32, BF16])
    dom = _mm(dco, wt["w_o_mem"], tb=True, name="mm_o_dx", out_dtype=BF16)
    gw["w_o_mem"] = _mm(om, dco, ta=True, name="mm_o_dw")
    dqm, dkvm = _xattn_bwd(qm, kvm, dom, seq, n_mem)
    dh2 = _mm(dqm, wt["w_q_mem"], tb=True, name="mm_q_dx", out_dtype=BF16)
    gw["w_q_mem"] = _mm(h2, dqm, ta=True, name="mm_q_dw")
    dmn = _mm(dkvm, wt["w_kv_mem"], tb=True, name="mm_kv_dx", out_dtype=BF16)
    gw["w_kv_mem"] = _mm(mn, dkvm, ta=True, name="mm_kv_dw")
    _, (gw["g_mem"],) = _rowwise_bwd(_fn_pre, [_win(mem)], [g_mem], 0, [[dmn]], name="pre_mem_bwd", tm=256,
                                     row_grad=[None])
    (dx0, dmo), (gw["g_post_mix"], gw["g_pre_cross"]) = _rowwise_bwd(
        _fn_res_pre, [_win(x), _win(mo)], [g_post_mix, g_pre_cross], 0, [[dx1], [dh2]],
        name="res_mix_bwd", tm=256, row_grad=[F32, BF16])
    dmixed = _mm(dmo, wt["w_out"], tb=True, name="mm_out_dx", out_dtype=BF16)
    gw["w_out"] = _mm(mixed, dmo, ta=True, name="mm_out_dw")
    (dzga, dzgb, dya_p, dyb_p), _ = _rowwise_bwd(_fn_mix, mix_rows, [], 0, [[dmixed]], name="gate_mix_bwd", tm=256,
                                                 row_grad=[BF16] * 4)
    gw["w_branch_a"] = _mm(y_a, dya_p, ta=True, name="mm_a_dw")
    gw["w_branch_b"] = _mm(y_b, dyb_p, ta=True, name="mm_b_dw")
    token = comm.send_early(gw)
    dy_a = _mm(dya_p, wt["w_branch_a"], tb=True, name="mm_a_dx", out_dtype=BF16, after=token)
    dy_b = _mm(dyb_p, wt["w_branch_b"], tb=True, name="mm_b_dx", out_dtype=BF16, after=token)
    dq, dk, dv, dbias = _attn_bwd(proj, bias, dy_b, seq)
    gw["rel_bias"] = _mm(dbias.reshape(N_HEADS, CHUNK * BAND), onehot, tb=True, name="mm_bias_dw", split_a=2)
    (dy, dr_p, dk2_p, dv_p, dg), (gw["lnx_w"], gw["lnx_b"], gw["bonus_scale"]) = _rowwise_bwd(
        _fn_post, post_rows, post_params, 2, [[dy_a]], name="rwkv_post_bwd", tm=128, row_grad=[F32] * 5)
    dr_s, dlw, dk2_s, dv_s, dkk, da = _wkv_bwd(z_rkv, lw, k2, kk, a, states, invs, dy, seq)
    (dzk, dzw, dza, dzg), pg = _rowwise_bwd(
        _fn_prep, prep_rows, prep_params, 2, [[dlw], [dk2_p, dk2_s], [dkk], [da], [dg]],
        name="rwkv_prep_bwd", tm=128, row_grad=[F32] * 4)
    gw["decay_base"], gd_up, gw["iclr_base"], gi_up, gg_up, gw["key_norm_scale"], gw["key_iclr_scale"] = pg
    gw["decay_up"], gw["iclr_up"], gw["gate_up"] = gd_up[:LORA_W], gi_up[:LORA_A], gg_up[:LORA_G]
    dp_r, gmix_r = _shift_bwd(proj, 0, D, mix_rkv[:, :D], [dr_p, dr_s], seq, name="shift_r_bwd")
    dp_k, gmix_k = _shift_bwd(proj, D, D, mix_rkv[:, D:2 * D], [dzk], seq, name="shift_k_bwd")
    dp_v, gmix_v = _shift_bwd(proj, 2 * D, D, mix_rkv[:, 2 * D:], [dv_p, dv_s], seq, name="shift_v_bwd")
    dp_lora, gmix_lora = _shift_bwd(proj, C_LORA, 512, mix_lora, [jnp.concatenate([dzw, dza, dzg], axis=1)], seq,
                                    name="shift_lora_bwd")
    gw["shift_mix"] = jnp.concatenate([gmix_r, gmix_k, gmix_v, _unpad_lora(gmix_lora)], axis=1)
    dproj = jnp.concatenate([dp_r, dp_k, dp_v, dq, dk, dv, dzga, dzgb, dp_lora], axis=1)
    gw["w_in_p"] = _mm(h1, dproj, ta=True, name="mm_in_dw")
    token = comm.send_late(gw)
    dh1 = _mm(dproj, w_in, tb=True, name="mm_in_dx", out_dtype=BF16, after=token)
    (grad_x,), (gw["g_pre_mix"],) = _rowwise_bwd(_fn_pre, [_win(x)], [g_pre_mix], 0, [[dh1]], name="pre_mix_bwd",
                                                 tm=256, row_grad=[F32], add_to={0: dx0})
    return loss, grad_x, gw


_COL_SHARDED = ("w_in", "decay_up", "iclr_up", "gate_up", "w_o_mem", "w_ffn_in")
_ROW_SHARDED = ("w_branch_a", "w_branch_b", "w_out", "w_q_mem", "w_kv_mem", "w_ffn_out")
_FIRST = ("w_in", "decay_up", "iclr_up", "gate_up")
_REST = ("w_o_mem", "w_ffn_in", "w_branch_a", "w_branch_b", "w_out", "w_q_mem", "w_kv_mem", "w_ffn_out")
_REPLICATED = ("g_pre_mix", "g_post_mix", "shift_mix", "decay_base", "iclr_base", "key_norm_scale", "key_iclr_scale",
               "bonus_scale", "lnx_w", "lnx_b", "rel_bias", "g_pre_cross", "g_post_cross", "g_mem", "g_pre_ffn",
               "g_post_ffn")
_WEIGHTS = ("g_pre_mix", "g_post_mix", "w_in", "shift_mix", "decay_base", "decay_up", "iclr_base", "iclr_up", "gate_up",
            "key_norm_scale", "key_iclr_scale", "bonus_scale", "lnx_w", "lnx_b", "rel_bias", "w_branch_a", "w_branch_b",
            "w_out", "g_pre_cross", "g_post_cross", "g_mem", "w_q_mem", "w_kv_mem", "w_o_mem", "g_pre_ffn", "g_post_ffn",
            "w_ffn_in", "w_ffn_out")
_PACK_ROWS = 8 * ((sum({"shift_mix": 3360, "bonus_scale": 1024, "rel_bias": 3072}.get(n, D) for n in _REPLICATED)
                   + 1 + 8 * LANE - 1) // (8 * LANE))


def _pack(vals):
    flat = jnp.concatenate([v.reshape(-1).astype(F32) for v in vals])
    return jnp.pad(flat, (0, _PACK_ROWS * LANE - flat.shape[0])).reshape(_PACK_ROWS, LANE)


def _unpack(packed, shapes):
    flat, out, pos = packed.reshape(-1), [], 0
    for s in shapes:
        n = math.prod(s)
        out.append(flat[pos:pos + n].reshape(s))
        pos += n
    return out


def _step(args, seq, n_mem):
    names = ("x", "mem") + _WEIGHTS + ("loss_target",) + tuple("m_" + n for n in _WEIGHTS) + tuple("v_" + n for n in _WEIGHTS)
    given = dict(zip(names, args))
    nb = given["x"].shape[0]
    x = given["x"].reshape(nb * seq, D)
    mem = given["mem"].reshape(nb * n_mem, D)
    target = given["loss_target"].reshape(nb * seq, D)
    shard = {n: given[n][0] for n in _COL_SHARDED + _ROW_SHARDED}
    out = {}

    def full(name, g):
        return _cols_to_full(g) if name in _COL_SHARDED else g.reshape(-1, g.shape[-1])

    def blocks_of(name, g):
        return (_full_to_cols(g) if name in _COL_SHARDED else g.reshape((N_DEV,) + shard[name].shape)).astype(BF16)

    def update(names, landed):
        res = None
        for n, parts in zip(names, landed):
            res = _adamw(parts, shard[n], given["m_" + n][0], given["v_" + n][0], name="adamw_" + n)
            for kind, r in zip(("grad_", "delta_", "new_m_", "new_v_"), res):
                out[kind + n] = r[None]
        return res[0]

    class Exchanges:
        def __init__(self):
            srcs = [shard[n].astype(BF16) for n in _FIRST]
            self.first, _ = _exchange_start(srcs, False, srcs[0], name="gather_first_start",
                                            dists=_SIBLING_AND_SAME_CORE)

        def first_weights(self, after):
            got = _exchange_wait(self.first, after, [shard[n] for n in _FIRST], name="gather_first_wait")
            relayed = _relay_to_sibling(got, name="gather_first_relay")
            pos = _mesh_pos()
            for j, d in enumerate(_OTHER_CHIPS):
                slot = _flat(_peer(pos, d | 1))
                got = [lax.dynamic_update_slice_in_dim(g, r[j][None], slot, 0) for g, r in zip(got, relayed)]
            self.rest, self.first_token = _exchange_start(
                [shard[n].astype(BF16) for n in _REST], False, got[0], name="gather_rest_start")
            first = {n: full(n, g) for n, g in zip(_FIRST, got)}
            first["w_in_p"] = _permute_in(first.pop("w_in"))
            return first

        def late_weights(self, after):
            got = _exchange_wait(self.rest, [after], [shard[n] for n in _REST], name="gather_rest_wait")
            return {n: full(n, g) for n, g in zip(_REST, got)}

        def send_early(self, gw):
            self.early_blocks = [blocks_of(n, gw[n]) for n in _REST]
            self.early, token = _exchange_start(self.early_blocks, True, self.early_blocks[-1], name="scatter_rest_start")
            return token

        def send_late(self, gw):
            me = _flat(_mesh_pos())
            own = [lax.dynamic_index_in_dim(b, me, 0, keepdims=False) for b in self.early_blocks]
            landed = _exchange_wait(self.early, [gw["w_in_p"]], own, name="scatter_rest_wait")
            self.updated = update(_REST, landed)
            grads = {**gw, "w_in": _unpermute_in(gw["w_in_p"])}
            self.late_blocks = [blocks_of(n, grads[n]) for n in _FIRST]
            self.late, token = _exchange_start(self.late_blocks, True, landed[0], name="scatter_first_start")
            return token

        def finish(self, after):
            me = _flat(_mesh_pos())
            own = [lax.dynamic_index_in_dim(b, me, 0, keepdims=False) for b in self.late_blocks]
            update(_FIRST, _exchange_wait(self.late, [after, self.updated], own, name="scatter_first_wait"))

    comm = Exchanges()
    wt = {n: given[n][0] for n in _REPLICATED}
    loss_tile, grad_x, gw = _local_step(x, mem, target, wt, seq, n_mem, comm)
    comm.finish(grad_x)
    rep_shapes = [given[n].shape for n in _REPLICATED]
    small = _exchange([_pack([gw[n] for n in _REPLICATED] + [loss_tile[0, 0]])], False, name="gather_small")[0]
    zero = jnp.zeros((), F32)
    res = _adamw(small, *[_pack([given[p + n] for n in _REPLICATED] + [zero]) for p in ("", "m_", "v_")],
                 name="adamw_small", tr=_PACK_ROWS)
    for kind, r in zip(("grad_", "delta_", "new_m_", "new_v_"), res):
        for n, val in zip(_REPLICATED, _unpack(r, rep_shapes)):
            out[kind + n] = val
    loss = res[0].reshape(-1)[sum(math.prod(s) for s in rep_shapes)]
    grad_x = grad_x.reshape(nb, seq, D)
    return (loss, grad_x, *[out[k + n] for k in ("grad_", "delta_", "new_m_", "new_v_") for n in _WEIGHTS])


def kernel(x, mem, g_pre_mix, g_post_mix, w_in, shift_mix, decay_base, decay_up, iclr_base, iclr_up, gate_up, key_norm_scale, key_iclr_scale, bonus_scale, lnx_w, lnx_b, rel_bias, w_branch_a, w_branch_b, w_out, g_pre_cross, g_post_cross, g_mem, w_q_mem, w_kv_mem, w_o_mem, g_pre_ffn, g_post_ffn, w_ffn_in, w_ffn_out, loss_target, m_g_pre_mix, m_g_post_mix, m_w_in, m_shift_mix, m_decay_base, m_decay_up, m_iclr_base, m_iclr_up, m_gate_up, m_key_norm_scale, m_key_iclr_scale, m_bonus_scale, m_lnx_w, m_lnx_b, m_rel_bias, m_w_branch_a, m_w_branch_b, m_w_out, m_g_pre_cross, m_g_post_cross, m_g_mem, m_w_q_mem, m_w_kv_mem, m_w_o_mem, m_g_pre_ffn, m_g_post_ffn, m_w_ffn_in, m_w_ffn_out, v_g_pre_mix, v_g_post_mix, v_w_in, v_shift_mix, v_decay_base, v_decay_up, v_iclr_base, v_iclr_up, v_gate_up, v_key_norm_scale, v_key_iclr_scale, v_bonus_scale, v_lnx_w, v_lnx_b, v_rel_bias, v_w_branch_a, v_w_branch_b, v_w_out, v_g_pre_cross, v_g_post_cross, v_g_mem, v_w_q_mem, v_w_kv_mem, v_w_o_mem, v_g_pre_ffn, v_g_post_ffn, v_w_ffn_in, v_w_ffn_out):
    args = (x, mem, g_pre_mix, g_post_mix, w_in, shift_mix, decay_base, decay_up, iclr_base, iclr_up, gate_up, key_norm_scale, key_iclr_scale, bonus_scale, lnx_w, lnx_b, rel_bias, w_branch_a, w_branch_b, w_out, g_pre_cross, g_post_cross, g_mem, w_q_mem, w_kv_mem, w_o_mem, g_pre_ffn, g_post_ffn, w_ffn_in, w_ffn_out, loss_target, m_g_pre_mix, m_g_post_mix, m_w_in, m_shift_mix, m_decay_base, m_decay_up, m_iclr_base, m_iclr_up, m_gate_up, m_key_norm_scale, m_key_iclr_scale, m_bonus_scale, m_lnx_w, m_lnx_b, m_rel_bias, m_w_branch_a, m_w_branch_b, m_w_out, m_g_pre_cross, m_g_post_cross, m_g_mem, m_w_q_mem, m_w_kv_mem, m_w_o_mem, m_g_pre_ffn, m_g_post_ffn, m_w_ffn_in, m_w_ffn_out, v_g_pre_mix, v_g_post_mix, v_w_in, v_shift_mix, v_decay_base, v_decay_up, v_iclr_base, v_iclr_up, v_gate_up, v_key_norm_scale, v_key_iclr_scale, v_bonus_scale, v_lnx_w, v_lnx_b, v_rel_bias, v_w_branch_a, v_w_branch_b, v_w_out, v_g_pre_cross, v_g_post_cross, v_g_mem, v_w_q_mem, v_w_kv_mem, v_w_o_mem, v_g_pre_ffn, v_g_post_ffn, v_w_ffn_in, v_w_ffn_out)
    return _step(args, x.shape[1], mem.shape[1])
```

```python
import functools
import math

import jax
import jax.numpy as jnp
from jax import lax
from jax.experimental import pallas as pl
from jax.experimental.pallas import tpu as pltpu

F32 = jnp.float32
BF16 = jnp.bfloat16

N_DEV = 8
D = 1024
HEAD = 64
N_HEADS = D // HEAD
LANE = 128
N_PAIRS = D // LANE
CHUNK = 64
LEFT = 8 * CHUNK
BAND = LEFT + CHUNK
REL_CLIP = 128
REL_TABLE = CHUNK + REL_CLIP
MEM_WIDTH = D // 2
MEM_HEADS = 4
FFN = 2816
LORA_W, LORA_A, LORA_G = 64, 64, 160
P_WIDTH = 3 * D + 3 * D + 2 * D + 128 + 128 + 256
C_Q, C_GA, C_LORA = 3 * D, 6 * D, 8 * D
NORM_EPS = 1e-6
GROUP_NORM_EPS = 64e-5
MASK_VALUE = -1e30
ADAM_LR, ADAM_B1, ADAM_B2, ADAM_EPS, ADAM_WD, ADAM_STEP = 0.001, 0.9, 0.999, 1e-08, 0.01, 10
VMEM_LIMIT = 56 * 1024 * 1024


def _cp(*sem):
    return pltpu.CompilerParams(dimension_semantics=sem, vmem_limit_bytes=VMEM_LIMIT)


_NN, _NT, _TN = ((1,), (0,)), ((1,), (1,)), ((0,), (0,))


def _dot_raw(a, b, dims):
    return lax.dot_general(a.astype(BF16), b.astype(BF16), (dims, ((), ())), preferred_element_type=F32)


@functools.partial(jax.custom_vjp, nondiff_argnums=(2,))
def _dot_dims(a, b, dims):
    return _dot_raw(a, b, dims)


def _dot_dims_fwd(a, b, dims):
    return _dot_raw(a, b, dims), (a, b)


def _dot_dims_bwd(dims, res, g):
    a, b = res
    if dims == _NN:
        da, db = _dot_raw(g, b, _NT), _dot_raw(a, g, _TN)
    elif dims == _NT:
        da, db = _dot_raw(g, b, _NN), _dot_raw(g, a, _TN)
    else:
        da, db = _dot_raw(b, g, _NT), _dot_raw(a, g, _NN)
    return da.astype(a.dtype), db.astype(b.dtype)


_dot_dims.defvjp(_dot_dims_fwd, _dot_dims_bwd)


def _dot(a, b, dims=_NN):
    return _dot_dims(a, b, dims)


def _dot_nt(a, b):
    return _dot_dims(a, b, _NT)


def _dot_tn(a, b):
    return _dot_dims(a, b, _TN)


def _split(x, terms):
    parts, rest = [], x.astype(F32)
    for _ in range(terms):
        p = rest.astype(BF16)
        parts.append(p)
        rest = rest - p.astype(F32)
    return parts


def _dot_split_a(a, b, terms=2):
    out = None
    for p in _split(a, terms):
        t = _dot(p, b)
        out = t if out is None else out + t
    return out


def _dot_split_b(a, b, terms=3):
    out = None
    for p in _split(b, terms):
        t = _dot(a, p)
        out = t if out is None else out + t
    return out


def _dot_hi(a, b, dims=_NN):
    ah, al = _split(a, 2)
    bh, bl = _split(b, 2)
    return _dot(ah, bh, dims) + (_dot(ah, bl, dims) + _dot(al, bh, dims))


MM_VMEM_BUDGET = 30 * 1024 * 1024
MM_HBM_BPS = 3.2e12
MM_MXU_FPS = 8.5e14
MM_STEP_S = 0.35e-6


def _divisors(n, align, cap):
    out = [d for d in range(align, min(n, cap) + 1, align) if n % d == 0]
    return out or [n]


def _mm_tiles(m, n, k, ea, eb, eo, ta):
    best = None
    for tm in _divisors(m, LANE if ta else 8, 2048):
        for tn in _divisors(n, LANE, 2048):
            for tk in _divisors(k, LANE, 2048):
                nk = k // tk
                vmem = 2 * (tm * tk * ea + tk * tn * eb + tm * tn * eo) + (tm * tn * 4 if nk > 1 else 0)
                if vmem > MM_VMEM_BUDGET:
                    continue
                dma = (tm * tk * ea if (nk > 1 or n // tn == 1) else tm * tk * ea * tn / n) + tk * tn * eb + tm * tn * eo / nk
                step = max(2.0 * tm * tn * tk / MM_MXU_FPS, dma / MM_HBM_BPS) + MM_STEP_S
                cost = (m // tm) * (n // tn) * nk * step
                if best is None or cost < best[0]:
                    best = (cost, tm, tn, tk)
    return best[1:]


def _mm(a, b, *, name, ta=False, tb=False, out_dtype=F32, tm=None, tn=None, tk=None, split_a=1, after=None):
    m, k = (a.shape[1], a.shape[0]) if ta else a.shape
    n, kb = (b.shape[0], b.shape[1]) if tb else (b.shape[1], b.shape[0])
    assert k == kb, (a.shape, b.shape, ta, tb)
    if tm is None:
        tm, tn, tk = _mm_tiles(m, n, k, a.dtype.itemsize, b.dtype.itemsize, jnp.dtype(out_dtype).itemsize, ta)
    assert m % tm == 0 and n % tn == 0 and k % tk == 0, (m, n, k, tm, tn, tk)
    nk = k // tk
    dims = ((0 if ta else 1,), (1 if tb else 0,))

    n_after = 0 if after is None else 1

    def body(a_ref, b_ref, *rest):
        o_ref, scratch = rest[n_after], rest[n_after + 1:]
        prod = None
        for p in _split(a_ref[...], split_a) if split_a > 1 else [a_ref[...]]:
            t = _dot_raw(p, b_ref[...], dims)
            prod = t if prod is None else prod + t
        if nk == 1:
            o_ref[...] = prod.astype(o_ref.dtype)
            return
        acc_ref, kk = scratch[0], pl.program_id(2)

        @pl.when(kk == 0)
        def _():
            acc_ref[...] = prod

        @pl.when(kk > 0)
        def _():
            acc_ref[...] += prod

        @pl.when(kk == nk - 1)
        def _():
            o_ref[...] = acc_ref[...].astype(o_ref.dtype)

    a_spec = pl.BlockSpec((tk, tm), lambda i, j, q: (q, i)) if ta else pl.BlockSpec((tm, tk), lambda i, j, q: (i, q))
    b_spec = pl.BlockSpec((tn, tk), lambda i, j, q: (j, q)) if tb else pl.BlockSpec((tk, tn), lambda i, j, q: (q, j))
    return pl.pallas_call(
        body, name=name, grid=(m // tm, n // tn, nk),
        in_specs=[a_spec, b_spec] + [pl.BlockSpec(memory_space=pl.ANY)] * n_after,
        out_specs=pl.BlockSpec((tm, tn), lambda i, j, q: (i, j)),
        out_shape=jax.ShapeDtypeStruct((m, n), out_dtype),
        scratch_shapes=[pltpu.VMEM((tm, tn), F32)] if nk > 1 else [],
        compiler_params=_cp("parallel", "parallel", "arbitrary"),
    )(a, b, *([] if after is None else [after]))


def _win(arr, start=0, width=None):
    width = arr.shape[1] if width is None else width
    assert start % width == 0
    return (arr, start // width, width)


def _row_specs(rows, tm):
    return [pl.BlockSpec((tm, w), functools.partial(lambda i, cb: (i, cb), cb=cb)) for (_, cb, w) in rows]


def _full_spec(p):
    nd = p.ndim
    return pl.BlockSpec(p.shape, lambda i, nd=nd: (0,) * nd)


def _rowwise(fn, rows, params, outs, *, name, tm, after=None):
    t = rows[0][0].shape[0]
    tm = min(tm, t)
    assert t % tm == 0
    nr, npar = len(rows), len(params)
    n_after = 0 if after is None else 1

    def body(*refs):
        vals = [r[...] for r in refs[:nr + npar]]
        res = fn(*vals)
        for o_ref, r in zip(refs[nr + npar + n_after:], res):
            o_ref[...] = r.astype(o_ref.dtype)

    return pl.pallas_call(
        body, name=name, grid=(t // tm,),
        in_specs=_row_specs(rows, tm) + [_full_spec(p) for p in params] + [pl.BlockSpec(memory_space=pl.ANY)] * n_after,
        out_specs=[pl.BlockSpec((tm, w), lambda i: (i, 0)) for (w, _) in outs],
        out_shape=[jax.ShapeDtypeStruct((t, w), dt) for (w, dt) in outs],
        compiler_params=_cp("parallel"),
    )(*[r[0] for r in rows], *params, *([] if after is None else [after]))


def _rowwise_bwd(fn, rows, params, n_const, cots, *, name, tm, row_grad, add_to=None, packed=False):
    t = rows[0][0].shape[0]
    tm = min(tm, t)
    assert t % tm == 0
    nr, npar = len(rows), len(params)
    ndp = npar - n_const
    add_to = add_to or {}
    add_idx = sorted(add_to)
    flat_cots = [c for group in cots for c in group]
    kept = [i for i in range(nr) if row_grad[i] is not None]

    def body(*refs):
        pos = 0
        row_v = [r[...] for r in refs[pos:pos + nr]]; pos += nr
        par_v = [r[...] for r in refs[pos:pos + npar]]; pos += npar
        cot_v = [r[...] for r in refs[pos:pos + len(flat_cots)]]; pos += len(flat_cots)
        add_v = [r[...] for r in refs[pos:pos + len(add_idx)]]; pos += len(add_idx)
        if packed:
            offs = [sum(rows[i][2] for i in kept[:q]) for q in range(len(kept))]
            rg_refs = [refs[pos].at[:, o:o + rows[i][2]] for o, i in zip(offs, kept)]; pos += 1
        else:
            rg_refs = refs[pos:pos + len(kept)]; pos += len(kept)
        pg_refs = refs[pos:pos + ndp]

        consts = par_v[ndp:]
        res, vjp = jax.vjp(lambda *args: tuple(fn(*args, *consts)), *row_v, *par_v[:ndp])
        cot_in, q = [], 0
        for j, group in enumerate(cots):
            c = None
            for _ in group:
                cv = cot_v[q].astype(F32); q += 1
                c = cv if c is None else c + cv
            c = jnp.zeros(res[j].shape, F32) if c is None else c
            cot_in.append(c.astype(res[j].dtype))
        grads = vjp(tuple(cot_in))
        for ref, i in zip(rg_refs, kept):
            g = grads[i].astype(F32)
            if i in add_to:
                g = g + add_v[add_idx.index(i)].astype(F32)
            ref[...] = g.astype(ref.dtype)

        @pl.when(pl.program_id(0) == 0)
        def _():
            for ref in pg_refs:
                ref[...] = jnp.zeros_like(ref)

        for ref, g in zip(pg_refs, grads[nr:]):
            ref[...] += g.astype(F32)

    cot_specs = [pl.BlockSpec((tm, c.shape[1]), lambda i: (i, 0)) for c in flat_cots]
    add_specs = [pl.BlockSpec((tm, add_to[i].shape[1]), lambda i_: (i_, 0)) for i in add_idx]
    widths = [sum(rows[i][2] for i in kept)] if packed else [rows[i][2] for i in kept]
    n_rg = len(widths)
    out_specs = [pl.BlockSpec((tm, w), lambda i_: (i_, 0)) for w in widths] + [_full_spec(p) for p in params[:ndp]]
    out_shape = [jax.ShapeDtypeStruct((t, w), row_grad[kept[q]]) for q, w in enumerate(widths)] + [
        jax.ShapeDtypeStruct(p.shape, F32) for p in params[:ndp]]
    res = pl.pallas_call(
        body, name=name, grid=(t // tm,),
        in_specs=_row_specs(rows, tm) + [_full_spec(p) for p in params] + cot_specs + add_specs,
        out_specs=out_specs, out_shape=out_shape,
        compiler_params=_cp("arbitrary"),
    )(*[r[0] for r in rows], *params, *flat_cots, *[add_to[i] for i in add_idx])
    return list(res[:n_rg]), list(res[n_rg:])


def _rms(x, g):
    xf = x.astype(F32)
    return xf * lax.rsqrt(jnp.mean(xf * xf, axis=-1, keepdims=True) + NORM_EPS) * g


def _softplus(x):
    return jnp.maximum(x, 0.0) + jnp.log(1.0 + jnp.exp(-jnp.abs(x)))


def _fn_pre(x, g):
    return (_rms(x, g).astype(BF16),)


def _fn_res(x, u, g_post):
    return (x + _rms(u, g_post),)


def _fn_res_pre(x, u, g_post, g_pre):
    xn = x + _rms(u, g_post)
    return xn, _rms(xn, g_pre).astype(BF16)


def _fn_mix(zga, zgb, ya, yb):
    return ((jax.nn.sigmoid(zga) * ya + jax.nn.sigmoid(zgb) * yb).astype(BF16),)


def _fn_swiglu(gate, up):
    return ((gate * jax.nn.sigmoid(gate) * up).astype(BF16),)


def _fn_prep(zk, zw, za, zg, decay_base, d_up, iclr_base, i_up, g_up, kns, kis, e_hd, e_dh):
    w_log = -_softplus(-(decay_base + _dot(jnp.tanh(zw), d_up))) - 0.5
    lw = -jnp.exp(w_log)
    a = jax.nn.sigmoid(iclr_base + _dot(za, i_up))
    g = _dot(jax.nn.sigmoid(zg), g_up)
    kn = zk * kns
    ss = _dot_split_a(kn * kn, e_dh)
    inv = lax.rsqrt(jnp.maximum(ss, 1e-24))
    kk = kn * _dot_split_a(inv, e_hd)
    k2 = zk * (1.0 + (a - 1.0) * kis)
    return lw, k2, kk, a, g


def _fn_post(y, r, k2, v, g, lnx_w, lnx_b, bonus, e_hd, e_dh):
    mu = _dot_split_a(_dot_split_a(y, e_dh) * (1.0 / HEAD), e_hd)
    yc = y - mu
    var = _dot_split_a(yc * yc, e_dh) * (1.0 / HEAD)
    yn = yc * _dot_split_a(lax.rsqrt(var + GROUP_NORM_EPS), e_hd)
    bs = _dot_split_a(_dot_split_a(r * k2 * bonus, e_dh), e_hd)
    return (((yn * lnx_w + lnx_b + bs * v) * g).astype(BF16),)


def _shift_fwd(p, col0, ncols, mix, seq, *, name, cw=256):
    t = p.shape[0]
    assert col0 % cw == 0 and ncols % cw == 0 and t % seq == 0
    cb0 = col0 // cw

    def body(p_ref, m_ref, z_ref):
        pv = p_ref[...]
        row = lax.broadcasted_iota(jnp.int32, pv.shape, 0)
        prev = jnp.where(row == 0, 0.0, pltpu.roll(pv, 1, axis=0))
        z_ref[...] = pv + (prev - pv) * m_ref[...]

    return pl.pallas_call(
        body, name=name, grid=(t // seq, ncols // cw),
        in_specs=[pl.BlockSpec((seq, cw), lambda b, c: (b, c + cb0)), pl.BlockSpec((1, cw), lambda b, c: (0, c))],
        out_specs=pl.BlockSpec((seq, cw), lambda b, c: (b, c)),
        out_shape=jax.ShapeDtypeStruct((t, ncols), F32),
        compiler_params=_cp("parallel", "parallel"),
    )(p, mix)


def _shift_bwd(p, col0, ncols, mix, dz_parts, seq, *, name, cw=256):
    t = p.shape[0]
    cb0 = col0 // cw
    n = len(dz_parts)

    def body(*refs):
        p_ref, m_ref = refs[:2]
        dp_ref, dm_ref = refs[2 + n:]
        dz = refs[2][...].astype(F32)
        for r in refs[3:2 + n]:
            dz = dz + r[...].astype(F32)
        pv = p_ref[...]
        mixv = m_ref[...]
        row = lax.broadcasted_iota(jnp.int32, pv.shape, 0)
        prev = jnp.where(row == 0, 0.0, pltpu.roll(pv, 1, axis=0))
        u = dz * mixv
        nxt = jnp.where(row == seq - 1, 0.0, pltpu.roll(u, seq - 1, axis=0))
        dp_ref[...] = (dz - u + nxt).astype(dp_ref.dtype)

        @pl.when(pl.program_id(1) == 0)
        def _():
            dm_ref[...] = jnp.zeros_like(dm_ref)

        dm_ref[...] += jnp.sum(dz * (prev - pv), axis=0, keepdims=True)

    return pl.pallas_call(
        body, name=name, grid=(ncols // cw, t // seq),
        in_specs=[pl.BlockSpec((seq, cw), lambda c, b: (b, c + cb0)), pl.BlockSpec((1, cw), lambda c, b: (0, c))]
        + [pl.BlockSpec((seq, cw), lambda c, b: (b, c))] * n,
        out_specs=[pl.BlockSpec((seq, cw), lambda c, b: (b, c)), pl.BlockSpec((1, cw), lambda c, b: (0, c))],
        out_shape=[jax.ShapeDtypeStruct((t, ncols), BF16), jax.ShapeDtypeStruct((1, ncols), F32)],
        compiler_params=_cp("parallel", "arbitrary"),
    )(p, mix, *dz_parts)


def _each(f, *lists):
    return [f(*xs) for xs in zip(*lists)]


def _tri_inv(low):
    c = low[0].shape[0]
    ti = lax.broadcasted_iota(jnp.int32, (c, c), 0)
    si = lax.broadcasted_iota(jnp.int32, (c, c), 1)
    eye = (ti == si).astype(F32)
    inside = (ti // 4) == (si // 4)
    base = [jnp.where(inside, m, 0.0) for m in low]
    acc = _each(lambda m: _dot(eye - m, eye + _dot(m, m)), base)
    size = 8
    while size <= c:
        wider = (ti // size) == (si // size)
        keep = jnp.logical_and(wider, jnp.logical_not(inside))
        acc = _each(lambda p, m: p - _dot(_dot(p, jnp.where(keep, m, 0.0)), p), acc, low)
        inside, size = wider, size * 2
    return acc


@jax.custom_vjp
def _tri_inv_known(low, inv):
    return inv


def _tri_inv_known_fwd(low, inv):
    return inv, inv


def _tri_inv_known_bwd(inv, g):
    dlow = _each(lambda t, gg: -_dot(_dot(t, gg, _TN), t, _NT), inv, g)
    return dlow, _each(jnp.zeros_like, inv)


_tri_inv_known.defvjp(_tri_inv_known_fwd, _tri_inv_known_bwd)


def _wkv_chunk(s0, r, lw, k, v, kk, a, inv=None):
    c = r[0].shape[0]
    ti = lax.broadcasted_iota(jnp.int32, (c, c), 0)
    si = lax.broadcasted_iota(jnp.int32, (c, c), 1)
    incl, strict = ti >= si, ti > si
    tri = incl.astype(F32)
    cum = _each(lambda x: _dot_split_b(tri, x, 3), lw)
    eg = _each(jnp.exp, cum)
    egp = _each(lambda cs, x: jnp.exp(cs - x), cum, lw)
    ei = _each(lambda cs: jnp.exp(-cs), cum)
    rh, kkh, kt = _each(jnp.multiply, r, eg), _each(jnp.multiply, kk, egp), _each(jnp.multiply, k, ei)
    bt = _each(lambda p, q, e: (p * q) * e, a, kk, ei)
    lb = _each(lambda p, q: jnp.where(strict, _dot_nt(p, q), 0.0), kkh, bt)
    lk = _each(lambda p, q: jnp.where(strict, _dot_nt(p, q), 0.0), kkh, kt)
    mb = _each(lambda p, q: jnp.where(incl, _dot_nt(p, q), 0.0), rh, bt)
    mk = _each(lambda p, q: jnp.where(incl, _dot_nt(p, q), 0.0), rh, kt)
    rhs = _each(lambda p, s, m, x: _dot_nt(p, s) + _dot(m, x), kkh, s0, lk, v)
    inv = _tri_inv(lb) if inv is None else _tri_inv_known(lb, inv)
    u = _each(lambda t, x: -_dot(t, x), inv, rhs)
    y = _each(lambda p, s, m1, uu, m2, x: _dot_nt(p, s) + _dot(m1, uu) + _dot(m2, x), rh, s0, mb, u, mk, v)
    s1 = _each(lambda s, uu, b, x, kq, w: (s + _dot_tn(uu, b) + _dot_tn(x, kq)) * jnp.exp(jnp.sum(w, axis=0, keepdims=True)),
               s0, u, bt, v, kt, lw)
    return y, s1, inv


WKV_HEADS = 16
WKV_COLS = WKV_HEADS * HEAD
WKV_GROUPS = N_HEADS // WKV_HEADS


def _head_cols(ref):
    return [ref[:, h * HEAD:(h + 1) * HEAD] for h in range(ref.shape[1] // HEAD)]


def _wkv_specs(seq, rev):
    nc = seq // CHUNK

    def rows(col0):
        cb0 = col0 // WKV_COLS
        if rev:
            return pl.BlockSpec((CHUNK, WKV_COLS), lambda b, h, c: (b * nc + nc - 1 - c, cb0 + h))
        return pl.BlockSpec((CHUNK, WKV_COLS), lambda b, h, c: (b * nc + c, cb0 + h))

    if rev:
        st = pl.BlockSpec((1, 1, WKV_HEADS, HEAD, HEAD), lambda b, h, c: (b * WKV_GROUPS + h, nc - 1 - c, 0, 0, 0))
    else:
        st = pl.BlockSpec((1, 1, WKV_HEADS, HEAD, HEAD), lambda b, h, c: (b * WKV_GROUPS + h, c, 0, 0, 0))
    return rows, st


def _wkv_fwd(z_rkv, lw, k2, kk, a, seq):
    t = z_rkv.shape[0]
    nb, nc = t // seq, seq // CHUNK
    rows, st = _wkv_specs(seq, False)

    def body(r_ref, v_ref, lw_ref, k_ref, kk_ref, a_ref, y_ref, st_ref, inv_ref, s_scr):
        @pl.when(pl.program_id(2) == 0)
        def _():
            s_scr[...] = jnp.zeros_like(s_scr)

        s0 = [s_scr[h] for h in range(WKV_HEADS)]
        y, s1, inv = _wkv_chunk(s0, *[_head_cols(ref) for ref in (r_ref, lw_ref, k_ref, v_ref, kk_ref, a_ref)])
        for h in range(WKV_HEADS):
            st_ref[0, 0, h] = s0[h]
            inv_ref[0, 0, h] = inv[h]
            y_ref[:, h * HEAD:(h + 1) * HEAD] = y[h]
            s_scr[h] = s1[h]

    per_chunk = jax.ShapeDtypeStruct((nb * WKV_GROUPS, nc, WKV_HEADS, HEAD, HEAD), F32)
    return pl.pallas_call(
        body, name="wkv_fwd", grid=(nb, WKV_GROUPS, nc),
        in_specs=[rows(0), rows(2 * D), rows(0), rows(0), rows(0), rows(0)],
        out_specs=[rows(0), st, st],
        out_shape=[jax.ShapeDtypeStruct((t, D), F32), per_chunk, per_chunk],
        scratch_shapes=[pltpu.VMEM((WKV_HEADS, HEAD, HEAD), F32)],
        compiler_params=_cp("parallel", "parallel", "arbitrary"),
    )(z_rkv, z_rkv, lw, k2, kk, a)


def _wkv_bwd(z_rkv, lw, k2, kk, a, states, invs, dy, seq):
    t = z_rkv.shape[0]
    nb, nc = t // seq, seq // CHUNK
    rows, st = _wkv_specs(seq, True)

    def body(r_ref, v_ref, lw_ref, k_ref, kk_ref, a_ref, st_ref, inv_ref, dy_ref,
             dr_ref, dlw_ref, dk_ref, dv_ref, dkk_ref, da_ref, ds_scr):
        @pl.when(pl.program_id(2) == 0)
        def _():
            ds_scr[...] = jnp.zeros_like(ds_scr)

        s0 = [st_ref[0, 0, h] for h in range(WKV_HEADS)]
        inv = [inv_ref[0, 0, h] for h in range(WKV_HEADS)]
        _, vjp = jax.vjp(lambda *args: _wkv_chunk(*args, inv=inv)[:2],
                         s0, *[_head_cols(ref) for ref in (r_ref, lw_ref, k_ref, v_ref, kk_ref, a_ref)])
        grads = vjp(([x.astype(F32) for x in _head_cols(dy_ref)], [ds_scr[h] for h in range(WKV_HEADS)]))
        for h in range(WKV_HEADS):
            ds_scr[h] = grads[0][h]
            for ref, g in zip((dr_ref, dlw_ref, dk_ref, dv_ref, dkk_ref, da_ref), grads[1:]):
                ref[:, h * HEAD:(h + 1) * HEAD] = g[h]

    return pl.pallas_call(
        body, name="wkv_bwd", grid=(nb, WKV_GROUPS, nc),
        in_specs=[rows(0), rows(2 * D), rows(0), rows(0), rows(0), rows(0), st, st, rows(0)],
        out_specs=[rows(0)] * 6,
        out_shape=[jax.ShapeDtypeStruct((t, D), F32)] * 6,
        scratch_shapes=[pltpu.VMEM((WKV_HEADS, HEAD, HEAD), F32)],
        compiler_params=_cp("parallel", "parallel", "arbitrary"),
    )(z_rkv, z_rkv, lw, k2, kk, a, states, invs, dy)


def _softmax(s):
    e = jnp.exp(s - jnp.max(s, axis=-1, keepdims=True))
    return e / jnp.sum(e, axis=-1, keepdims=True)


ATT_HEADS = 8
ATT_COLS = ATT_HEADS * HEAD
ATT_GROUPS = N_HEADS // ATT_HEADS


def _attn_chunk(q, kb, vb, bias, valid):
    s = _each(lambda x, y, z: jnp.where(valid, _dot_nt(x, y) * (HEAD ** -0.5) + z, MASK_VALUE), q, kb, bias)
    return _each(_dot, _each(_softmax, s), vb)


def _pad_fill(pad_ref, src_ref):
    pad_ref[0:LEFT, :] = jnp.zeros((LEFT, pad_ref.shape[1]), pad_ref.dtype)
    pad_ref[LEFT:, :] = src_ref[...].astype(pad_ref.dtype)


def _band_heads(pad_ref, start):
    return [pad_ref[pl.ds(start, BAND), h * HEAD:(h + 1) * HEAD].astype(F32) for h in range(ATT_HEADS)]


def _band_valid(c):
    return (c * CHUNK - LEFT + lax.broadcasted_iota(jnp.int32, (1, BAND), 1)) >= 0


def _attn_fwd(proj, bias, seq):
    t = proj.shape[0]
    nb, nc = t // seq, seq // CHUNK
    cq = C_Q // ATT_COLS

    def body(q_ref, k_ref, v_ref, b_ref, o_ref, kpad, vpad):
        c = pl.program_id(2)

        @pl.when(c == 0)
        def _():
            _pad_fill(kpad, k_ref)
            _pad_fill(vpad, v_ref)

        start = pl.multiple_of(c * CHUNK, CHUNK)
        o = _attn_chunk(_head_cols(q_ref), _band_heads(kpad, start), _band_heads(vpad, start),
                        [b_ref[h] for h in range(ATT_HEADS)], _band_valid(c))
        for h in range(ATT_HEADS):
            o_ref[:, h * HEAD:(h + 1) * HEAD] = o[h].astype(o_ref.dtype)

    return pl.pallas_call(
        body, name="attn_fwd", grid=(ATT_GROUPS, nb, nc),
        in_specs=[pl.BlockSpec((CHUNK, ATT_COLS), lambda h, b, c: (b * nc + c, cq + h)),
                  pl.BlockSpec((seq, ATT_COLS), lambda h, b, c: (b, cq + ATT_GROUPS + h)),
                  pl.BlockSpec((seq, ATT_COLS), lambda h, b, c: (b, cq + 2 * ATT_GROUPS + h)),
                  pl.BlockSpec((ATT_HEADS, CHUNK, BAND), lambda h, b, c: (h, 0, 0))],
        out_specs=pl.BlockSpec((CHUNK, ATT_COLS), lambda h, b, c: (b * nc + c, h)),
        out_shape=jax.ShapeDtypeStruct((t, D), BF16),
        scratch_shapes=[pltpu.VMEM((seq + LEFT, ATT_COLS), BF16)] * 2,
        compiler_params=_cp("parallel", "arbitrary", "arbitrary"),
    )(proj, proj, proj, bias)


def _attn_bwd(proj, bias, do, seq):
    t = proj.shape[0]
    nb, nc = t // seq, seq // CHUNK
    cq = C_Q // ATT_COLS

    def body(q_ref, k_ref, v_ref, b_ref, do_ref, dq_ref, dk_ref, dv_ref, db_ref, kpad, vpad, dkpad, dvpad):
        b, c = pl.program_id(1), pl.program_id(2)

        @pl.when(c == 0)
        def _():
            _pad_fill(kpad, k_ref)
            _pad_fill(vpad, v_ref)
            dkpad[...] = jnp.zeros_like(dkpad)
            dvpad[...] = jnp.zeros_like(dvpad)

        @pl.when(jnp.logical_and(b == 0, c == 0))
        def _():
            db_ref[...] = jnp.zeros_like(db_ref)

        start = pl.multiple_of(c * CHUNK, CHUNK)
        _, vjp = jax.vjp(functools.partial(_attn_chunk, valid=_band_valid(c)),
                         _head_cols(q_ref), _band_heads(kpad, start), _band_heads(vpad, start),
                         [b_ref[h] for h in range(ATT_HEADS)])
        dq, dkb, dvb, dbias = vjp([x.astype(F32) for x in _head_cols(do_ref)])
        for h in range(ATT_HEADS):
            sl = slice(h * HEAD, (h + 1) * HEAD)
            dq_ref[:, sl] = dq[h].astype(dq_ref.dtype)
            dkpad[pl.ds(start, BAND), sl] += dkb[h].astype(F32)
            dvpad[pl.ds(start, BAND), sl] += dvb[h].astype(F32)
            db_ref[h] += dbias[h]

        @pl.when(c == nc - 1)
        def _():
            dk_ref[...] = dkpad[LEFT:, :].astype(dk_ref.dtype)
            dv_ref[...] = dvpad[LEFT:, :].astype(dv_ref.dtype)

    kv_out = pl.BlockSpec((seq, ATT_COLS), lambda h, b, c: (b, h))
    return pl.pallas_call(
        body, name="attn_bwd", grid=(ATT_GROUPS, nb, nc),
        in_specs=[pl.BlockSpec((CHUNK, ATT_COLS), lambda h, b, c: (b * nc + c, cq + h)),
                  pl.BlockSpec((seq, ATT_COLS), lambda h, b, c: (b, cq + ATT_GROUPS + h)),
                  pl.BlockSpec((seq, ATT_COLS), lambda h, b, c: (b, cq + 2 * ATT_GROUPS + h)),
                  pl.BlockSpec((ATT_HEADS, CHUNK, BAND), lambda h, b, c: (h, 0, 0)),
                  pl.BlockSpec((CHUNK, ATT_COLS), lambda h, b, c: (b * nc + c, h))],
        out_specs=[pl.BlockSpec((CHUNK, ATT_COLS), lambda h, b, c: (b * nc + c, h)), kv_out, kv_out,
                   pl.BlockSpec((ATT_HEADS, CHUNK, BAND), lambda h, b, c: (h, 0, 0))],
        out_shape=[jax.ShapeDtypeStruct((t, D), BF16)] * 3 + [jax.ShapeDtypeStruct((N_HEADS, CHUNK, BAND), F32)],
        scratch_shapes=[pltpu.VMEM((seq + LEFT, ATT_COLS), BF16)] * 2 + [pltpu.VMEM((seq + LEFT, ATT_COLS), F32)] * 2,
        compiler_params=_cp("parallel", "arbitrary", "arbitrary"),
    )(proj, proj, proj, bias, do)


def _xattn_tile(q, k, v):
    s = _dot_nt(q, k) * ((MEM_WIDTH // MEM_HEADS) ** -0.5)
    return _dot(_softmax(s), v)


def _xattn_fwd(qm, kvm, seq, n_mem, tq=512):
    t = qm.shape[0]
    tq = min(tq, seq)
    nb, nq = t // seq, seq // tq

    def body(q_ref, k_ref, v_ref, o_ref):
        o_ref[...] = _xattn_tile(q_ref[...], k_ref[...], v_ref[...]).astype(o_ref.dtype)

    return pl.pallas_call(
        body, name="xattn_fwd", grid=(nb, MEM_HEADS, nq),
        in_specs=[pl.BlockSpec((tq, LANE), lambda b, h, i: (b * nq + i, h)),
                  pl.BlockSpec((n_mem, LANE), lambda b, h, i: (b, h)),
                  pl.BlockSpec((n_mem, LANE), lambda b, h, i: (b, MEM_HEADS + h))],
        out_specs=pl.BlockSpec((tq, LANE), lambda b, h, i: (b * nq + i, h)),
        out_shape=jax.ShapeDtypeStruct((t, MEM_WIDTH), BF16),
        compiler_params=_cp("parallel", "parallel", "parallel"),
    )(qm, kvm, kvm)


def _xattn_bwd(qm, kvm, do, seq, n_mem, tq=512):
    t = qm.shape[0]
    tq = min(tq, seq)
    nb, nq = t // seq, seq // tq

    def body(q_ref, k_ref, v_ref, do_ref, dq_ref, dkv_ref, dk_acc, dv_acc):
        i = pl.program_id(2)

        @pl.when(i == 0)
        def _():
            dk_acc[...] = jnp.zeros_like(dk_acc)
            dv_acc[...] = jnp.zeros_like(dv_acc)

        _, vjp = jax.vjp(_xattn_tile, q_ref[...], k_ref[...], v_ref[...])
        dq, dk, dv = vjp(do_ref[...].astype(F32))
        dq_ref[...] = dq.astype(dq_ref.dtype)
        dk_acc[...] += dk
        dv_acc[...] += dv

        @pl.when(i == nq - 1)
        def _():
            dkv_ref[0] = dk_acc[...].astype(dkv_ref.dtype)
            dkv_ref[1] = dv_acc[...].astype(dkv_ref.dtype)

    dq, dkv = pl.pallas_call(
        body, name="xattn_bwd", grid=(nb, MEM_HEADS, nq),
        in_specs=[pl.BlockSpec((tq, LANE), lambda b, h, i: (b * nq + i, h)),
                  pl.BlockSpec((n_mem, LANE), lambda b, h, i: (b, h)),
                  pl.BlockSpec((n_mem, LANE), lambda b, h, i: (b, MEM_HEADS + h)),
                  pl.BlockSpec((tq, LANE), lambda b, h, i: (b * nq + i, h))],
        out_specs=[pl.BlockSpec((tq, LANE), lambda b, h, i: (b * nq + i, h)),
                   pl.BlockSpec((2, n_mem, LANE), lambda b, h, i: (0, b, h))],
        out_shape=[jax.ShapeDtypeStruct((t, MEM_WIDTH), BF16), jax.ShapeDtypeStruct((2, nb * n_mem, MEM_WIDTH), BF16)],
        scratch_shapes=[pltpu.VMEM((n_mem, LANE), F32)] * 2,
        compiler_params=_cp("parallel", "parallel", "arbitrary"),
    )(qm, kvm, kvm, do)
    return dq, jnp.concatenate([dkv[0], dkv[1]], axis=1)


def _loss_head(y, target, tm=512):
    t, d = y.shape
    tm = min(tm, t)

    def body(y_ref, t_ref, dy_ref, l_ref):
        @pl.when(pl.program_id(0) == 0)
        def _():
            l_ref[...] = jnp.zeros_like(l_ref)

        diff = y_ref[...] - t_ref[...]
        dy_ref[...] = diff * (1.0 / d)
        l_ref[...] += 0.5 * jnp.sum(jnp.mean(diff * diff, axis=-1, keepdims=True), axis=0, keepdims=True)

    dy, loss = pl.pallas_call(
        body, name="loss_head", grid=(t // tm,),
        in_specs=[pl.BlockSpec((tm, d), lambda i: (i, 0))] * 2,
        out_specs=[pl.BlockSpec((tm, d), lambda i: (i, 0)), pl.BlockSpec((8, LANE), lambda i: (0, 0))],
        out_shape=[jax.ShapeDtypeStruct((t, d), F32), jax.ShapeDtypeStruct((8, LANE), F32)],
        compiler_params=_cp("arbitrary"),
    )(y, target)
    return dy, loss


def _mesh_pos():
    return lax.axis_index("x"), lax.axis_index("y"), lax.axis_index("c")


def _peer(pos, d):
    x, y, c = pos
    return ((1 - x) if d & 4 else x, (1 - y) if d & 2 else y, (1 - c) if d & 1 else c)


def _flat(pos):
    return 4 * pos[0] + 2 * pos[1] + pos[2]


def _exchange(arrays, scatter, *, name):
    n = len(arrays)
    shapes = [a.shape[1:] if scatter else a.shape for a in arrays]

    def body(*refs):
        ins, outs = refs[:n], refs[n:2 * n]
        send, recv, loc = refs[2 * n:]
        pos = _mesh_pos()
        me = _flat(pos)
        pending = []
        for i in range(n):
            own = pltpu.make_async_copy(ins[i].at[me] if scatter else ins[i], outs[i].at[me], loc.at[i])
            own.start()
            pending.append(own)
            for d in range(1, N_DEV):
                peer = _peer(pos, d)
                src = ins[i].at[_flat(peer)] if scatter else ins[i]
                out_cp = pltpu.make_async_remote_copy(
                    src_ref=src, dst_ref=outs[i].at[me], send_sem=send.at[i, d - 1], recv_sem=recv.at[i, d - 1],
                    device_id=peer, device_id_type=pl.DeviceIdType.MESH)
                out_cp.start()
                pending.append(out_cp)
        for i in range(n):
            own = pending[i * N_DEV]
            for d in range(1, N_DEV):
                peer = _peer(pos, d)
                src = ins[i].at[_flat(peer)] if scatter else ins[i]
                pending[i * N_DEV + d].wait_send()
                pltpu.make_async_remote_copy(
                    src_ref=src, dst_ref=outs[i].at[_flat(peer)], send_sem=send.at[i, d - 1], recv_sem=recv.at[i, d - 1],
                    device_id=peer, device_id_type=pl.DeviceIdType.MESH).wait_recv()
            own.wait()

    hbm = pl.BlockSpec(memory_space=pltpu.HBM)
    return pl.pallas_call(
        body, name=name,
        in_specs=[hbm] * n, out_specs=[hbm] * n,
        out_shape=[jax.ShapeDtypeStruct((N_DEV,) + tuple(s), a.dtype) for s, a in zip(shapes, arrays)],
        scratch_shapes=[pltpu.SemaphoreType.DMA((n, N_DEV - 1)), pltpu.SemaphoreType.DMA((n, N_DEV - 1)),
                        pltpu.SemaphoreType.DMA((n,))],
    )(*arrays)


_HBM = pl.BlockSpec(memory_space=pltpu.HBM)
_SEM = pl.BlockSpec(memory_space=pltpu.SEMAPHORE)
_DATAFLOW = pltpu.SideEffectType.DATAFLOW_SIDE_EFFECTING


_ALL_PEERS = tuple(range(1, N_DEV))
_SIBLING_AND_SAME_CORE = (1, 2, 4, 6)


def _remote_copies(ins, lands, send, recv, scatter, dists):
    pos = _mesh_pos()
    me = _flat(pos)
    out = []
    for i in range(len(ins)):
        for j, d in enumerate(dists):
            peer = _peer(pos, d)
            src = ins[i].at[_flat(peer)] if scatter else ins[i]
            pair = i * len(dists) + j
            sems = dict(send_sem=send.at[pair], recv_sem=recv.at[pair], device_id=peer,
                        device_id_type=pl.DeviceIdType.MESH)
            out.append((pltpu.make_async_remote_copy(src_ref=src, dst_ref=lands[i].at[me], **sems),
                        pltpu.make_async_remote_copy(src_ref=src, dst_ref=lands[i].at[_flat(peer)], **sems)))
    return out


def _exchange_start(arrays, scatter, after, *, name, dists=_ALL_PEERS):
    n = len(arrays)
    shapes = [a.shape[1:] if scatter else a.shape for a in arrays]
    lands = [pltpu.with_memory_space_constraint(lax.empty((N_DEV,) + tuple(s), a.dtype), pltpu.HBM)
             for s, a in zip(shapes, arrays)]
    srcs = [pltpu.with_memory_space_constraint(a, pltpu.HBM) for a in arrays]

    def body(*refs):
        ins, land_refs = refs[:n], refs[n:2 * n]
        send, recv, token = refs[2 * n + 1], refs[2 * n + 2], refs[-1]
        for going, _ in _remote_copies(ins, land_refs, send, recv, scatter, dists):
            going.start()
        token[...] = jnp.zeros_like(token)

    sems = pltpu.SemaphoreType.DMA((n * len(dists),))
    res = pl.pallas_call(
        body, name=name,
        out_shape=(sems, sems, *[pltpu.HBM(a.shape, a.dtype) for a in srcs + lands], jax.ShapeDtypeStruct((8, LANE), F32)),
        in_specs=[_HBM] * (2 * n) + [pl.BlockSpec(memory_space=pl.ANY)],
        out_specs=(_SEM, _SEM, *[_HBM] * (2 * n), pl.BlockSpec(memory_space=pltpu.VMEM)),
        input_output_aliases={i: 2 + i for i in range(2 * n)},
        compiler_params=pltpu.CompilerParams(has_side_effects=_DATAFLOW),
    )(*srcs, *lands, after)
    return (n, scatter, dists, res[0], res[1], list(res[2:2 + 2 * n])), res[-1]


def _exchange_wait(handle, after, own, *, name):
    n, scatter, dists, send, recv, thru = handle

    def body(*refs):
        ins, land_refs = refs[:n], refs[n:2 * n]
        for going, coming in _remote_copies(ins, land_refs, refs[2 * n], refs[2 * n + 1], scatter, dists):
            going.wait_send()
            coming.wait_recv()

    res = pl.pallas_call(
        body, name=name,
        out_shape=tuple(pltpu.HBM(a.shape, a.dtype) for a in thru),
        in_specs=[_HBM] * (2 * n) + [_SEM, _SEM] + [pl.BlockSpec(memory_space=pl.ANY)] * len(after),
        out_specs=tuple([_HBM] * (2 * n)),
        input_output_aliases={i: i for i in range(2 * n)},
        compiler_params=pltpu.CompilerParams(has_side_effects=_DATAFLOW),
    )(*thru, send, recv, *after)
    me = _flat(_mesh_pos())
    return [lax.dynamic_update_slice_in_dim(land, o[None].astype(land.dtype), me, 0) for land, o in zip(res[n:], own)]


_OTHER_CHIPS = (2, 4, 6)


def _relay_to_sibling(gathered, *, name):
    n, k = len(gathered), len(_OTHER_CHIPS)

    def body(*refs):
        ins, outs = refs[:n], refs[n:2 * n]
        send, recv = refs[2 * n:]
        pos = _mesh_pos()
        copies = []
        for i in range(n):
            for j, d in enumerate(_OTHER_CHIPS):
                cp = pltpu.make_async_remote_copy(
                    src_ref=ins[i].at[_flat(_peer(pos, d))], dst_ref=outs[i].at[j],
                    send_sem=send.at[i * k + j], recv_sem=recv.at[i * k + j],
                    device_id=_peer(pos, 1), device_id_type=pl.DeviceIdType.MESH)
                cp.start()
                copies.append(cp)
        for cp in copies:
            cp.wait()

    return pl.pallas_call(
        body, name=name, in_specs=[_HBM] * n, out_specs=[_HBM] * n,
        out_shape=[jax.ShapeDtypeStruct((k,) + g.shape[1:], g.dtype) for g in gathered],
        scratch_shapes=[pltpu.SemaphoreType.DMA((n * k,)), pltpu.SemaphoreType.DMA((n * k,))],
    )(*gathered)


def _adamw(parts, w, m, v, *, name, tr=128, after=None):
    _, r, c = w.shape
    align = 8 * 4 // parts.dtype.itemsize
    tr = max(d for d in range(align, min(tr, r) + 1, align) if r % d == 0)
    n_after = 0 if after is None else 1

    def body(p_ref, w_ref, m_ref, v_ref, *rest):
        g_ref, d_ref, nm_ref, nv_ref = rest[n_after:]
        g = p_ref[0].astype(F32)
        for j in range(1, N_DEV):
            g = g + p_ref[j].astype(F32)
        m2 = ADAM_B1 * m_ref[0] + (1.0 - ADAM_B1) * g
        v2 = ADAM_B2 * v_ref[0] + (1.0 - ADAM_B2) * (g * g)
        m_hat = m2 / (1.0 - ADAM_B1 ** ADAM_STEP)
        v_hat = v2 / (1.0 - ADAM_B2 ** ADAM_STEP)
        g_ref[0] = g
        d_ref[0] = -ADAM_LR * (m_hat / (jnp.sqrt(v_hat) + ADAM_EPS) + ADAM_WD * w_ref[0])
        nm_ref[0] = m2
        nv_ref[0] = v2

    spec = pl.BlockSpec((1, tr, c), lambda i: (0, i, 0))
    return pl.pallas_call(
        body, name=name, grid=(r // tr,),
        in_specs=[pl.BlockSpec((N_DEV, tr, c), lambda i: (0, i, 0)), spec, spec, spec]
        + [pl.BlockSpec(memory_space=pl.ANY)] * n_after,
        out_specs=[spec] * 4, out_shape=[jax.ShapeDtypeStruct((1, r, c), F32)] * 4,
        compiler_params=_cp("parallel"),
    )(parts, w, m, v, *([] if after is None else [after]))


def _cols_to_full(g):
    return jnp.transpose(g, (1, 0, 2)).reshape(g.shape[1], N_DEV * g.shape[2])


def _full_to_cols(w):
    r, c = w.shape
    return jnp.transpose(w.reshape(r, N_DEV, c // N_DEV), (1, 0, 2))


def _pad_cols(a, width):
    return jnp.pad(a, ((0, 0), (0, width - a.shape[1])))


def _pad_lora(w):
    return jnp.concatenate([
        _pad_cols(w[:, :LORA_W], 128), _pad_cols(w[:, LORA_W:LORA_W + LORA_A], 128),
        _pad_cols(w[:, LORA_W + LORA_A:], 256)], axis=1)


def _unpad_lora(wp):
    return jnp.concatenate([wp[:, :LORA_W], wp[:, 128:128 + LORA_A], wp[:, 256:256 + LORA_G]], axis=1)


def _permute_in(w):
    rk = 3 * D
    lo = rk + LORA_W + LORA_A + LORA_G
    return jnp.concatenate([w[:, :rk], w[:, lo:], _pad_lora(w[:, rk:lo])], axis=1)


def _unpermute_in(wp):
    return jnp.concatenate([wp[:, :3 * D], _unpad_lora(wp[:, C_LORA:]), wp[:, 3 * D:C_LORA]], axis=1)


def _rel_index():
    dist = jnp.arange(CHUNK)[:, None] - jnp.arange(BAND)[None, :] + LEFT
    return (jnp.minimum(dist, REL_CLIP) + (CHUNK - 1)).reshape(-1)


def _local_step(x, mem, target, wt, seq, n_mem, comm):
    t = x.shape[0]
    row = lambda a: a.reshape(1, -1).astype(F32)
    g_pre_mix, g_post_mix = row(wt["g_pre_mix"]), row(wt["g_post_mix"])
    g_pre_cross, g_post_cross, g_mem = row(wt["g_pre_cross"]), row(wt["g_post_cross"]), row(wt["g_mem"])
    g_pre_ffn, g_post_ffn = row(wt["g_pre_ffn"]), row(wt["g_post_ffn"])
    mix = row(wt["shift_mix"])
    mix_rkv, mix_lora = mix[:, :3 * D], _pad_lora(mix[:, 3 * D:])
    decay_base, iclr_base = row(wt["decay_base"]), row(wt["iclr_base"])
    kns, kis = row(wt["key_norm_scale"]), row(wt["key_iclr_scale"])
    lnx_w, lnx_b, bonus = row(wt["lnx_w"]), row(wt["lnx_b"]), row(wt["bonus_scale"])
    e_dh = (jnp.arange(D)[:, None] // HEAD == jnp.arange(N_HEADS)[None, :]).astype(F32)
    e_hd = e_dh.T
    onehot = (jnp.arange(REL_TABLE)[:, None] == _rel_index()[None, :]).astype(BF16)

    begun = comm.begun
    (h1,) = _rowwise(_fn_pre, [_win(x)], [g_pre_mix], [(D, BF16)], name="pre_mix", tm=512, after=begun)
    (mn,) = _rowwise(_fn_pre, [_win(mem)], [g_mem], [(D, BF16)], name="pre_mem", tm=512, after=begun)
    bias = _mm(wt["rel_bias"].astype(F32), onehot, name="mm_bias", split_a=3, after=begun).reshape(N_HEADS, CHUNK, BAND)
    wt = {**wt, **comm.first_weights([h1, mn, bias])}
    w_in = wt["w_in_p"]
    d_up = jnp.pad(wt["decay_up"].astype(F32), ((0, 128 - LORA_W), (0, 0)))
    i_up = jnp.pad(wt["iclr_up"].astype(F32), ((0, 128 - LORA_A), (0, 0)))
    g_up = jnp.pad(wt["gate_up"].astype(F32), ((0, 256 - LORA_G), (0, 0)))
    proj = _mm(h1, w_in, name="mm_in", after=comm.first_token)
    z_rkv = _shift_fwd(proj, 0, 3 * D, mix_rkv, seq, name="shift_rkv")
    z_lora = _shift_fwd(proj, C_LORA, 512, mix_lora, seq, name="shift_lora")
    prep_rows = [_win(z_rkv, D, D), _win(z_lora, 0, 128), _win(z_lora, 128, 128), _win(z_lora, 256, 256)]
    prep_params = [decay_base, d_up, iclr_base, i_up, g_up, kns, kis, e_hd, e_dh]
    lw, k2, kk, a, g = _rowwise(_fn_prep, prep_rows, prep_params, [(D, F32)] * 5, name="rwkv_prep", tm=256)
    y, states, invs = _wkv_fwd(z_rkv, lw, k2, kk, a, seq)
    post_rows = [_win(y), _win(z_rkv, 0, D), _win(k2), _win(z_rkv, 2 * D, D), _win(g)]
    post_params = [lnx_w, lnx_b, bonus, e_hd, e_dh]
    (y_a,) = _rowwise(_fn_post, post_rows, post_params, [(D, BF16)], name="rwkv_post", tm=256)
    y_b = _attn_fwd(proj, bias, seq)
    wt = {**wt, **comm.late_weights(y_b)}
    ya_p = _mm(y_a, wt["w_branch_a"], name="mm_a")
    yb_p = _mm(y_b, wt["w_branch_b"], name="mm_b")
    mix_rows = [_win(proj, C_GA, D), _win(proj, C_GA + D, D), _win(ya_p), _win(yb_p)]
    (mixed,) = _rowwise(_fn_mix, mix_rows, [], [(D, BF16)], name="gate_mix", tm=512)
    mo = _mm(mixed, wt["w_out"], name="mm_out")
    x1, h2 = _rowwise(_fn_res_pre, [_win(x), _win(mo)], [g_post_mix, g_pre_cross], [(D, F32), (D, BF16)],
                      name="res_mix", tm=512)
    qm = _mm(h2, wt["w_q_mem"], name="mm_q")
    kvm = _mm(mn, wt["w_kv_mem"], name="mm_kv")
    om = _xattn_fwd(qm, kvm, seq, n_mem)
    co = _mm(om, wt["w_o_mem"], name="mm_o")
    x2, h3 = _rowwise(_fn_res_pre, [_win(x1), _win(co)], [g_post_cross, g_pre_ffn], [(D, F32), (D, BF16)],
                      name="res_cross", tm=512)
    gu = _mm(h3, wt["w_ffn_in"], name="mm_ffn_in")
    (act,) = _rowwise(_fn_swiglu, [_win(gu, 0, FFN), _win(gu, FFN, FFN)], [], [(FFN, BF16)], name="swiglu", tm=256)
    ff = _mm(act, wt["w_ffn_out"], name="mm_ffn_out")
    (x3,) = _rowwise(_fn_res, [_win(x2), _win(ff)], [g_post_ffn], [(D, F32)], name="res_ffn", tm=512)
    dx3, loss = _loss_head(x3, target)

    gw = {}
    (dx2, dff), (gw["g_post_ffn"],) = _rowwise_bwd(
        _fn_res, [_win(x2), _win(ff)], [g_post_ffn], 0, [[dx3]], name="res_ffn_bwd", tm=256, row_grad=[F32, BF16])
    dact = _mm(dff, wt["w_ffn_out"], tb=True, name="mm_ffn_out_dx", out_dtype=BF16)
    gw["w_ffn_out"] = _mm(act, dff, ta=True, name="mm_ffn_out_dw", out_dtype=BF16)
    (dgu,), _ = _rowwise_bwd(_fn_swiglu, [_win(gu, 0, FFN), _win(gu, FFN, FFN)], [], 0, [[dact]],
                             name="swiglu_bwd", tm=256, row_grad=[BF16, BF16], packed=True)
    dh3 = _mm(dgu, wt["w_ffn_in"], tb=True, name="mm_ffn_in_dx", out_dtype=BF16)
    gw["w_ffn_in"] = _mm(h3, dgu, ta=True, name="mm_ffn_in_dw", out_dtype=BF16)
    (dx1, dco), (gw["g_post_cross"], gw["g_pre_ffn"]) = _rowwise_bwd(
        _fn_res_pre, [_win(x1), _win(co)], [g_post_cross, g_pre_ffn], 0, [[dx2], [dh3]],
        name="res_cross_bwd", tm=256, row_grad=[F32, BF16])
    dom = _mm(dco, wt["w_o_mem"], tb=True, name="mm_o_dx", out_dtype=BF16)
    gw["w_o_mem"] = _mm(om, dco, ta=True, name="mm_o_dw", out_dtype=BF16)
    dqm, dkvm = _xattn_bwd(qm, kvm, dom, seq, n_mem)
    dh2 = _mm(dqm, wt["w_q_mem"], tb=True, name="mm_q_dx", out_dtype=BF16)
    gw["w_q_mem"] = _mm(h2, dqm, ta=True, name="mm_q_dw", out_dtype=BF16)
    dmn = _mm(dkvm, wt["w_kv_mem"], tb=True, name="mm_kv_dx", out_dtype=BF16)
    gw["w_kv_mem"] = _mm(mn, dkvm, ta=True, name="mm_kv_dw", out_dtype=BF16)
    _, (gw["g_mem"],) = _rowwise_bwd(_fn_pre, [_win(mem)], [g_mem], 0, [[dmn]], name="pre_mem_bwd", tm=256,
                                     row_grad=[None])
    (dx0, dmo), (gw["g_post_mix"], gw["g_pre_cross"]) = _rowwise_bwd(
        _fn_res_pre, [_win(x), _win(mo)], [g_post_mix, g_pre_cross], 0, [[dx1], [dh2]],
        name="res_mix_bwd", tm=256, row_grad=[F32, BF16])
    dmixed = _mm(dmo, wt["w_out"], tb=True, name="mm_out_dx", out_dtype=BF16)
    gw["w_out"] = _mm(mixed, dmo, ta=True, name="mm_out_dw", out_dtype=BF16)
    (dzga, dzgb, dya_p, dyb_p), _ = _rowwise_bwd(_fn_mix, mix_rows, [], 0, [[dmixed]], name="gate_mix_bwd", tm=256,
                                                 row_grad=[BF16] * 4)
    gw["w_branch_a"] = _mm(y_a, dya_p, ta=True, name="mm_a_dw", out_dtype=BF16)
    gw["w_branch_b"] = _mm(y_b, dyb_p, ta=True, name="mm_b_dw", out_dtype=BF16)
    token = comm.send_early(gw)
    dy_a = _mm(dya_p, wt["w_branch_a"], tb=True, name="mm_a_dx", out_dtype=BF16, after=token)
    dy_b = _mm(dyb_p, wt["w_branch_b"], tb=True, name="mm_b_dx", out_dtype=BF16, after=token)
    dq, dk, dv, dbias = _attn_bwd(proj, bias, dy_b, seq)
    gw["rel_bias"] = _mm(dbias.reshape(N_HEADS, CHUNK * BAND), onehot, tb=True, name="mm_bias_dw", split_a=2)
    (dy, dr_p, dk2_p, dv_p, dg), (gw["lnx_w"], gw["lnx_b"], gw["bonus_scale"]) = _rowwise_bwd(
        _fn_post, post_rows, post_params, 2, [[dy_a]], name="rwkv_post_bwd", tm=128, row_grad=[F32] * 5)
    dr_s, dlw, dk2_s, dv_s, dkk, da = _wkv_bwd(z_rkv, lw, k2, kk, a, states, invs, dy, seq)
    (dzk, dzw, dza, dzg), pg = _rowwise_bwd(
        _fn_prep, prep_rows, prep_params, 2, [[dlw], [dk2_p, dk2_s], [dkk], [da], [dg]],
        name="rwkv_prep_bwd", tm=128, row_grad=[F32] * 4)
    gw["decay_base"], gd_up, gw["iclr_base"], gi_up, gg_up, gw["key_norm_scale"], gw["key_iclr_scale"] = pg
    gw["decay_up"], gw["iclr_up"], gw["gate_up"] = gd_up[:LORA_W], gi_up[:LORA_A], gg_up[:LORA_G]
    dp_r, gmix_r = _shift_bwd(proj, 0, D, mix_rkv[:, :D], [dr_p, dr_s], seq, name="shift_r_bwd")
    dp_k, gmix_k = _shift_bwd(proj, D, D, mix_rkv[:, D:2 * D], [dzk], seq, name="shift_k_bwd")
    dp_v, gmix_v = _shift_bwd(proj, 2 * D, D, mix_rkv[:, 2 * D:], [dv_p, dv_s], seq, name="shift_v_bwd")
    dp_lora, gmix_lora = _shift_bwd(proj, C_LORA, 512, mix_lora, [jnp.concatenate([dzw, dza, dzg], axis=1)], seq,
                                    name="shift_lora_bwd")
    gw["shift_mix"] = jnp.concatenate([gmix_r, gmix_k, gmix_v, _unpad_lora(gmix_lora)], axis=1)
    dproj = jnp.concatenate([dp_r, dp_k, dp_v, dq, dk, dv, dzga, dzgb, dp_lora], axis=1)
    gw["w_in_p"] = _mm(h1, dproj, ta=True, name="mm_in_dw", out_dtype=BF16, after=gw["rel_bias"])
    token = comm.send_late(gw)
    dh1 = _mm(dproj, w_in, tb=True, name="mm_in_dx", out_dtype=BF16, after=token)
    (grad_x,), (gw["g_pre_mix"],) = _rowwise_bwd(_fn_pre, [_win(x)], [g_pre_mix], 0, [[dh1]], name="pre_mix_bwd",
                                                 tm=256, row_grad=[F32], add_to={0: dx0})
    return loss, grad_x, gw


_COL_SHARDED = ("w_in", "decay_up", "iclr_up", "gate_up", "w_o_mem", "w_ffn_in")
_ROW_SHARDED = ("w_branch_a", "w_branch_b", "w_out", "w_q_mem", "w_kv_mem", "w_ffn_out")
_FIRST = ("w_in", "decay_up", "iclr_up", "gate_up")
_REST = ("w_o_mem", "w_ffn_in", "w_branch_a", "w_branch_b", "w_out", "w_q_mem", "w_kv_mem", "w_ffn_out")
_REPLICATED = ("g_pre_mix", "g_post_mix", "shift_mix", "decay_base", "iclr_base", "key_norm_scale", "key_iclr_scale",
               "bonus_scale", "lnx_w", "lnx_b", "rel_bias", "g_pre_cross", "g_post_cross", "g_mem", "g_pre_ffn",
               "g_post_ffn")
_WEIGHTS = ("g_pre_mix", "g_post_mix", "w_in", "shift_mix", "decay_base", "decay_up", "iclr_base", "iclr_up", "gate_up",
            "key_norm_scale", "key_iclr_scale", "bonus_scale", "lnx_w", "lnx_b", "rel_bias", "w_branch_a", "w_branch_b",
            "w_out", "g_pre_cross", "g_post_cross", "g_mem", "w_q_mem", "w_kv_mem", "w_o_mem", "g_pre_ffn", "g_post_ffn",
            "w_ffn_in", "w_ffn_out")
_PACK_ROWS = 8 * ((sum({"shift_mix": 3360, "bonus_scale": 1024, "rel_bias": 3072}.get(n, D) for n in _REPLICATED)
                   + 1 + 8 * LANE - 1) // (8 * LANE))


def _pack(vals):
    flat = jnp.concatenate([v.reshape(-1).astype(F32) for v in vals])
    return jnp.pad(flat, (0, _PACK_ROWS * LANE - flat.shape[0])).reshape(_PACK_ROWS, LANE)


def _unpack(packed, shapes):
    flat, out, pos = packed.reshape(-1), [], 0
    for s in shapes:
        n = math.prod(s)
        out.append(flat[pos:pos + n].reshape(s))
        pos += n
    return out


def _step(args, seq, n_mem):
    names = ("x", "mem") + _WEIGHTS + ("loss_target",) + tuple("m_" + n for n in _WEIGHTS) + tuple("v_" + n for n in _WEIGHTS)
    given = dict(zip(names, args))
    nb = given["x"].shape[0]
    x = given["x"].reshape(nb * seq, D)
    mem = given["mem"].reshape(nb * n_mem, D)
    target = given["loss_target"].reshape(nb * seq, D)
    shard = {n: given[n][0] for n in _COL_SHARDED + _ROW_SHARDED}
    out = {}

    def full(name, g):
        return _cols_to_full(g) if name in _COL_SHARDED else g.reshape(-1, g.shape[-1])

    def blocks_of(name, g):
        return (_full_to_cols(g) if name in _COL_SHARDED else g.reshape((N_DEV,) + shard[name].shape)).astype(BF16)

    def update(names, landed, after=None):
        res = None
        for n, parts in zip(names, landed):
            res = _adamw(parts, given[n], given["m_" + n], given["v_" + n], name="adamw_" + n, after=after)
            for kind, r in zip(("grad_", "delta_", "new_m_", "new_v_"), res):
                out[kind + n] = r
        return res[0]

    class Exchanges:
        def __init__(self):
            srcs = [shard[n].astype(BF16) for n in _FIRST]
            self.first, self.begun = _exchange_start(srcs, False, srcs[0], name="gather_first_start",
                                                     dists=_SIBLING_AND_SAME_CORE)

        def first_weights(self, after):
            got = _exchange_wait(self.first, after, [shard[n] for n in _FIRST], name="gather_first_wait")
            relayed = _relay_to_sibling(got, name="gather_first_relay")
            pos = _mesh_pos()
            for j, d in enumerate(_OTHER_CHIPS):
                slot = _flat(_peer(pos, d | 1))
                got = [lax.dynamic_update_slice_in_dim(g, r[j][None], slot, 0) for g, r in zip(got, relayed)]
            self.rest, self.first_token = _exchange_start(
                [shard[n].astype(BF16) for n in _REST], False, got[0], name="gather_rest_start")
            first = {n: full(n, g) for n, g in zip(_FIRST, got)}
            first["w_in_p"] = _permute_in(first.pop("w_in"))
            return first

        def late_weights(self, after):
            got = _exchange_wait(self.rest, [after], [shard[n] for n in _REST], name="gather_rest_wait")
            return {n: full(n, g) for n, g in zip(_REST, got)}

        def send_early(self, gw):
            self.early_blocks = [blocks_of(n, gw[n]) for n in _REST]
            self.early, token = _exchange_start(self.early_blocks, True, self.early_blocks[-1], name="scatter_rest_start")
            return token

        def send_late(self, gw):
            me = _flat(_mesh_pos())
            own = [lax.dynamic_index_in_dim(b, me, 0, keepdims=False) for b in self.early_blocks]
            landed = _exchange_wait(self.early, [gw["w_in_p"]], own, name="scatter_rest_wait")
            grads = {**gw, "w_in": _unpermute_in(gw["w_in_p"])}
            self.late_blocks = [blocks_of(n, grads[n]) for n in _FIRST]
            self.late, token = _exchange_start(self.late_blocks, True, landed[0], name="scatter_first_start")
            self.updated = update(_REST, landed, after=token)
            return token

        def finish(self, after):
            me = _flat(_mesh_pos())
            own = [lax.dynamic_index_in_dim(b, me, 0, keepdims=False) for b in self.late_blocks]
            update(_FIRST, _exchange_wait(self.late, [after, self.updated], own, name="scatter_first_wait"))

    comm = Exchanges()
    wt = {n: given[n][0] for n in _REPLICATED}
    loss_tile, grad_x, gw = _local_step(x, mem, target, wt, seq, n_mem, comm)
    comm.finish(grad_x)
    rep_shapes = [given[n].shape for n in _REPLICATED]
    small = _exchange([_pack([gw[n] for n in _REPLICATED] + [loss_tile[0, 0]])], False, name="gather_small")[0]
    zero = jnp.zeros((), F32)
    res = _adamw(small, *[_pack([given[p + n] for n in _REPLICATED] + [zero])[None] for p in ("", "m_", "v_")],
                 name="adamw_small", tr=_PACK_ROWS)
    for kind, r in zip(("grad_", "delta_", "new_m_", "new_v_"), res):
        for n, val in zip(_REPLICATED, _unpack(r[0], rep_shapes)):
            out[kind + n] = val
    loss = res[0].reshape(-1)[sum(math.prod(s) for s in rep_shapes)]
    grad_x = grad_x.reshape(nb, seq, D)
    return (loss, grad_x, *[out[k + n] for k in ("grad_", "delta_", "new_m_", "new_v_") for n in _WEIGHTS])


def kernel(x, mem, g_pre_mix, g_post_mix, w_in, shift_mix, decay_base, decay_up, iclr_base, iclr_up, gate_up, key_norm_scale, key_iclr_scale, bonus_scale, lnx_w, lnx_b, rel_bias, w_branch_a, w_branch_b, w_out, g_pre_cross, g_post_cross, g_mem, w_q_mem, w_kv_mem, w_o_mem, g_pre_ffn, g_post_ffn, w_ffn_in, w_ffn_out, loss_target, m_g_pre_mix, m_g_post_mix, m_w_in, m_shift_mix, m_decay_base, m_decay_up, m_iclr_base, m_iclr_up, m_gate_up, m_key_norm_scale, m_key_iclr_scale, m_bonus_scale, m_lnx_w, m_lnx_b, m_rel_bias, m_w_branch_a, m_w_branch_b, m_w_out, m_g_pre_cross, m_g_post_cross, m_g_mem, m_w_q_mem, m_w_kv_mem, m_w_o_mem, m_g_pre_ffn, m_g_post_ffn, m_w_ffn_in, m_w_ffn_out, v_g_pre_mix, v_g_post_mix, v_w_in, v_shift_mix, v_decay_base, v_decay_up, v_iclr_base, v_iclr_up, v_gate_up, v_key_norm_scale, v_key_iclr_scale, v_bonus_scale, v_lnx_w, v_lnx_b, v_rel_bias, v_w_branch_a, v_w_branch_b, v_w_out, v_g_pre_cross, v_g_post_cross, v_g_mem, v_w_q_mem, v_w_kv_mem, v_w_o_mem, v_g_pre_ffn, v_g_post_ffn, v_w_ffn_in, v_w_ffn_out):
    args = (x, mem, g_pre_mix, g_post_mix, w_in, shift_mix, decay_base, decay_up, iclr_base, iclr_up, gate_up, key_norm_scale, key_iclr_scale, bonus_scale, lnx_w, lnx_b, rel_bias, w_branch_a, w_branch_b, w_out, g_pre_cross, g_post_cross, g_mem, w_q_mem, w_kv_mem, w_o_mem, g_pre_ffn, g_post_ffn, w_ffn_in, w_ffn_out, loss_target, m_g_pre_mix, m_g_post_mix, m_w_in, m_shift_mix, m_decay_base, m_decay_up, m_iclr_base, m_iclr_up, m_gate_up, m_key_norm_scale, m_key_iclr_scale, m_bonus_scale, m_lnx_w, m_lnx_b, m_rel_bias, m_w_branch_a, m_w_branch_b, m_w_out, m_g_pre_cross, m_g_post_cross, m_g_mem, m_w_q_mem, m_w_kv_mem, m_w_o_mem, m_g_pre_ffn, m_g_post_ffn, m_w_ffn_in, m_w_ffn_out, v_g_pre_mix, v_g_post_mix, v_w_in, v_shift_mix, v_decay_base, v_decay_up, v_iclr_base, v_iclr_up, v_gate_up, v_key_norm_scale, v_key_iclr_scale, v_bonus_scale, v_lnx_w, v_lnx_b, v_rel_bias, v_w_branch_a, v_w_branch_b, v_w_out, v_g_pre_cross, v_g_post_cross, v_g_mem, v_w_q_mem, v_w_kv_mem, v_w_o_mem, v_g_pre_ffn, v_g_post_ffn, v_w_ffn_in, v_w_ffn_out)
    return _step(args, x.shape[1], mem.shape[1])
```

```python
import functools
import math

import jax
import jax.numpy as jnp
from jax import lax
from jax.experimental import pallas as pl
from jax.experimental.pallas import tpu as pltpu

F32 = jnp.float32
BF16 = jnp.bfloat16

N_DEV = 8
D = 1024
HEAD = 64
N_HEADS = D // HEAD
LANE = 128
N_PAIRS = D // LANE
CHUNK = 64
LEFT = 8 * CHUNK
BAND = LEFT + CHUNK
REL_CLIP = 128
REL_TABLE = CHUNK + REL_CLIP
MEM_WIDTH = D // 2
MEM_HEADS = 4
FFN = 2816
LORA_W, LORA_A, LORA_G = 64, 64, 160
P_WIDTH = 3 * D + 3 * D + 2 * D + 128 + 128 + 256
C_Q, C_GA, C_LORA = 3 * D, 6 * D, 8 * D
NORM_EPS = 1e-6
GROUP_NORM_EPS = 64e-5
MASK_VALUE = -1e30
ADAM_LR, ADAM_B1, ADAM_B2, ADAM_EPS, ADAM_WD, ADAM_STEP = 0.001, 0.9, 0.999, 1e-08, 0.01, 10
VMEM_LIMIT = 56 * 1024 * 1024


def _cp(*sem):
    return pltpu.CompilerParams(dimension_semantics=sem, vmem_limit_bytes=VMEM_LIMIT)


_NN, _NT, _TN = ((1,), (0,)), ((1,), (1,)), ((0,), (0,))


def _dot_raw(a, b, dims):
    return lax.dot_general(a.astype(BF16), b.astype(BF16), (dims, ((), ())), preferred_element_type=F32)


@functools.partial(jax.custom_vjp, nondiff_argnums=(2,))
def _dot_dims(a, b, dims):
    return _dot_raw(a, b, dims)


def _dot_dims_fwd(a, b, dims):
    return _dot_raw(a, b, dims), (a, b)


def _dot_dims_bwd(dims, res, g):
    a, b = res
    if dims == _NN:
        da, db = _dot_raw(g, b, _NT), _dot_raw(a, g, _TN)
    elif dims == _NT:
        da, db = _dot_raw(g, b, _NN), _dot_raw(g, a, _TN)
    else:
        da, db = _dot_raw(b, g, _NT), _dot_raw(a, g, _NN)
    return da.astype(a.dtype), db.astype(b.dtype)


_dot_dims.defvjp(_dot_dims_fwd, _dot_dims_bwd)


def _dot(a, b, dims=_NN):
    return _dot_dims(a, b, dims)


def _dot_nt(a, b):
    return _dot_dims(a, b, _NT)


def _dot_tn(a, b):
    return _dot_dims(a, b, _TN)


def _split(x, terms):
    parts, rest = [], x.astype(F32)
    for _ in range(terms):
        p = rest.astype(BF16)
        parts.append(p)
        rest = rest - p.astype(F32)
    return parts


def _dot_split_a(a, b, terms=2):
    out = None
    for p in _split(a, terms):
        t = _dot(p, b)
        out = t if out is None else out + t
    return out


def _dot_split_b(a, b, terms=3):
    out = None
    for p in _split(b, terms):
        t = _dot(a, p)
        out = t if out is None else out + t
    return out


def _dot_hi(a, b, dims=_NN):
    ah, al = _split(a, 2)
    bh, bl = _split(b, 2)
    return _dot(ah, bh, dims) + (_dot(ah, bl, dims) + _dot(al, bh, dims))


MM_VMEM_BUDGET = 30 * 1024 * 1024
MM_HBM_BPS = 3.2e12
MM_MXU_FPS = 8.5e14
MM_STEP_S = 0.35e-6


def _divisors(n, align, cap):
    out = [d for d in range(align, min(n, cap) + 1, align) if n % d == 0]
    return out or [n]


def _mm_tiles(m, n, k, ea, eb, eo, ta):
    best = None
    for tm in _divisors(m, LANE if ta else 8, 2048):
        for tn in _divisors(n, LANE, 2048):
            for tk in _divisors(k, LANE, 2048):
                nk = k // tk
                vmem = 2 * (tm * tk * ea + tk * tn * eb + tm * tn * eo) + (tm * tn * 4 if nk > 1 else 0)
                if vmem > MM_VMEM_BUDGET:
                    continue
                dma = (tm * tk * ea if (nk > 1 or n // tn == 1) else tm * tk * ea * tn / n) + tk * tn * eb + tm * tn * eo / nk
                step = max(2.0 * tm * tn * tk / MM_MXU_FPS, dma / MM_HBM_BPS) + MM_STEP_S
                cost = (m // tm) * (n // tn) * nk * step
                if best is None or cost < best[0]:
                    best = (cost, tm, tn, tk)
    return best[1:]


def _mm(a, b, *, name, ta=False, tb=False, out_dtype=F32, tm=None, tn=None, tk=None, split_a=1, after=None):
    m, k = (a.shape[1], a.shape[0]) if ta else a.shape
    n, kb = (b.shape[0], b.shape[1]) if tb else (b.shape[1], b.shape[0])
    assert k == kb, (a.shape, b.shape, ta, tb)
    if tm is None:
        tm, tn, tk = _mm_tiles(m, n, k, a.dtype.itemsize, b.dtype.itemsize, jnp.dtype(out_dtype).itemsize, ta)
    assert m % tm == 0 and n % tn == 0 and k % tk == 0, (m, n, k, tm, tn, tk)
    nk = k // tk
    dims = ((0 if ta else 1,), (1 if tb else 0,))

    n_after = 0 if after is None else 1

    def body(a_ref, b_ref, *rest):
        o_ref, scratch = rest[n_after], rest[n_after + 1:]
        prod = None
        for p in _split(a_ref[...], split_a) if split_a > 1 else [a_ref[...]]:
            t = _dot_raw(p, b_ref[...], dims)
            prod = t if prod is None else prod + t
        if nk == 1:
            o_ref[...] = prod.astype(o_ref.dtype)
            return
        acc_ref, kk = scratch[0], pl.program_id(2)

        @pl.when(kk == 0)
        def _():
            acc_ref[...] = prod

        @pl.when(kk > 0)
        def _():
            acc_ref[...] += prod

        @pl.when(kk == nk - 1)
        def _():
            o_ref[...] = acc_ref[...].astype(o_ref.dtype)

    a_spec = pl.BlockSpec((tk, tm), lambda i, j, q: (q, i)) if ta else pl.BlockSpec((tm, tk), lambda i, j, q: (i, q))
    b_spec = pl.BlockSpec((tn, tk), lambda i, j, q: (j, q)) if tb else pl.BlockSpec((tk, tn), lambda i, j, q: (q, j))
    return pl.pallas_call(
        body, name=name, grid=(m // tm, n // tn, nk),
        in_specs=[a_spec, b_spec] + [pl.BlockSpec(memory_space=pl.ANY)] * n_after,
        out_specs=pl.BlockSpec((tm, tn), lambda i, j, q: (i, j)),
        out_shape=jax.ShapeDtypeStruct((m, n), out_dtype),
        scratch_shapes=[pltpu.VMEM((tm, tn), F32)] if nk > 1 else [],
        compiler_params=_cp("parallel", "parallel", "arbitrary"),
    )(a, b, *([] if after is None else [after]))


def _piece_steps(pieces, tile):
    counts = [p.shape[1] // tile for p in pieces]
    assert all(p.shape[1] % tile == 0 for p in pieces)
    return [(sum(counts[:i]), c) for i, c in enumerate(counts)], sum(counts)


def _mm_cat_nt(pieces, w, *, name, after=None, tm=1024, tk=512):
    t, n = pieces[0].shape[0], w.shape[0]
    tm = min(tm, t)
    spans, nk = _piece_steps(pieces, tk)
    npc = len(pieces)
    n_after = 0 if after is None else 1

    def body(*refs):
        w_ref, o_ref, acc_ref = refs[npc], refs[npc + 1 + n_after], refs[npc + 2 + n_after]
        q = pl.program_id(1)

        @pl.when(q == 0)
        def _():
            acc_ref[...] = jnp.zeros_like(acc_ref)

        for p_ref, (first, count) in zip(refs[:npc], spans):
            @pl.when(jnp.logical_and(q >= first, q < first + count))
            def _(p_ref=p_ref):
                acc_ref[...] += _dot_raw(p_ref[...], w_ref[...], _NT)

        @pl.when(q == nk - 1)
        def _():
            o_ref[...] = acc_ref[...].astype(o_ref.dtype)

    def piece_spec(first, count):
        return pl.BlockSpec((tm, tk), lambda i, q: (i, jnp.clip(q - first, 0, count - 1)))

    return pl.pallas_call(
        body, name=name, grid=(t // tm, nk),
        in_specs=[piece_spec(*s) for s in spans] + [pl.BlockSpec((n, tk), lambda i, q: (0, q))]
        + [pl.BlockSpec(memory_space=pl.ANY)] * n_after,
        out_specs=pl.BlockSpec((tm, n), lambda i, q: (i, 0)),
        out_shape=jax.ShapeDtypeStruct((t, n), BF16),
        scratch_shapes=[pltpu.VMEM((tm, n), F32)],
        compiler_params=_cp("parallel", "arbitrary"),
    )(*pieces, w, *([] if after is None else [after]))


def _mm_cat_tn(a, pieces, *, name, after=None, tk=1024, tn=512):
    t, m = a.shape
    tk = min(tk, t)
    spans, nj = _piece_steps(pieces, tn)
    npc, nk = len(pieces), t // tk
    n_after = 0 if after is None else 1

    def body(a_ref, *refs):
        o_ref, acc_ref = refs[npc + n_after], refs[npc + 1 + n_after]
        j, q = pl.program_id(0), pl.program_id(1)

        @pl.when(q == 0)
        def _():
            acc_ref[...] = jnp.zeros_like(acc_ref)

        for p_ref, (first, count) in zip(refs[:npc], spans):
            @pl.when(jnp.logical_and(j >= first, j < first + count))
            def _(p_ref=p_ref):
                acc_ref[...] += _dot_raw(a_ref[...], p_ref[...], _TN)

        @pl.when(q == nk - 1)
        def _():
            o_ref[...] = acc_ref[...].astype(o_ref.dtype)

    def piece_spec(first, count):
        def index(j, q):
            mine = jnp.logical_and(j >= first, j < first + count)
            return jnp.where(mine, q, 0), jnp.clip(j - first, 0, count - 1)
        return pl.BlockSpec((tk, tn), index)

    return pl.pallas_call(
        body, name=name, grid=(nj, nk),
        in_specs=[pl.BlockSpec((tk, m), lambda j, q: (q, 0))] + [piece_spec(*s) for s in spans]
        + [pl.BlockSpec(memory_space=pl.ANY)] * n_after,
        out_specs=pl.BlockSpec((m, tn), lambda j, q: (0, j)),
        out_shape=jax.ShapeDtypeStruct((m, nj * tn), BF16),
        scratch_shapes=[pltpu.VMEM((m, tn), F32)],
        compiler_params=_cp("parallel", "arbitrary"),
    )(a, *pieces, *([] if after is None else [after]))


def _win(arr, start=0, width=None):
    width = arr.shape[1] if width is None else width
    assert start % width == 0
    return (arr, start // width, width)


def _row_specs(rows, tm):
    return [pl.BlockSpec((tm, w), functools.partial(lambda i, cb: (i, cb), cb=cb)) for (_, cb, w) in rows]


def _full_spec(p):
    nd = p.ndim
    return pl.BlockSpec(p.shape, lambda i, nd=nd: (0,) * nd)


def _rowwise(fn, rows, params, outs, *, name, tm, after=None):
    t = rows[0][0].shape[0]
    tm = min(tm, t)
    assert t % tm == 0
    nr, npar = len(rows), len(params)
    n_after = 0 if after is None else 1

    def body(*refs):
        vals = [r[...] for r in refs[:nr + npar]]
        res = fn(*vals)
        for o_ref, r in zip(refs[nr + npar + n_after:], res):
            o_ref[...] = r.astype(o_ref.dtype)

    return pl.pallas_call(
        body, name=name, grid=(t // tm,),
        in_specs=_row_specs(rows, tm) + [_full_spec(p) for p in params] + [pl.BlockSpec(memory_space=pl.ANY)] * n_after,
        out_specs=[pl.BlockSpec((tm, w), lambda i: (i, 0)) for (w, _) in outs],
        out_shape=[jax.ShapeDtypeStruct((t, w), dt) for (w, dt) in outs],
        compiler_params=_cp("parallel"),
    )(*[r[0] for r in rows], *params, *([] if after is None else [after]))


def _rowwise_bwd(fn, rows, params, n_const, cots, *, name, tm, row_grad, add_to=None, packed=False):
    t = rows[0][0].shape[0]
    tm = min(tm, t)
    assert t % tm == 0
    nr, npar = len(rows), len(params)
    ndp = npar - n_const
    add_to = add_to or {}
    add_idx = sorted(add_to)
    flat_cots = [c for group in cots for c in group]
    kept = [i for i in range(nr) if row_grad[i] is not None]

    def body(*refs):
        pos = 0
        row_v = [r[...] for r in refs[pos:pos + nr]]; pos += nr
        par_v = [r[...] for r in refs[pos:pos + npar]]; pos += npar
        cot_v = [r[...] for r in refs[pos:pos + len(flat_cots)]]; pos += len(flat_cots)
        add_v = [r[...] for r in refs[pos:pos + len(add_idx)]]; pos += len(add_idx)
        if packed:
            offs = [sum(rows[i][2] for i in kept[:q]) for q in range(len(kept))]
            rg_refs = [refs[pos].at[:, o:o + rows[i][2]] for o, i in zip(offs, kept)]; pos += 1
        else:
            rg_refs = refs[pos:pos + len(kept)]; pos += len(kept)
        pg_refs = refs[pos:pos + ndp]

        consts = par_v[ndp:]
        res, vjp = jax.vjp(lambda *args: tuple(fn(*args, *consts)), *row_v, *par_v[:ndp])
        cot_in, q = [], 0
        for j, group in enumerate(cots):
            c = None
            for _ in group:
                cv = cot_v[q].astype(F32); q += 1
                c = cv if c is None else c + cv
            c = jnp.zeros(res[j].shape, F32) if c is None else c
            cot_in.append(c.astype(res[j].dtype))
        grads = vjp(tuple(cot_in))
        for ref, i in zip(rg_refs, kept):
            g = grads[i].astype(F32)
            if i in add_to:
                g = g + add_v[add_idx.index(i)].astype(F32)
            ref[...] = g.astype(ref.dtype)

        @pl.when(pl.program_id(0) == 0)
        def _():
            for ref in pg_refs:
                ref[...] = jnp.zeros_like(ref)

        for ref, g in zip(pg_refs, grads[nr:]):
            ref[...] += g.astype(F32)

    cot_specs = [pl.BlockSpec((tm, c.shape[1]), lambda i: (i, 0)) for c in flat_cots]
    add_specs = [pl.BlockSpec((tm, add_to[i].shape[1]), lambda i_: (i_, 0)) for i in add_idx]
    widths = [sum(rows[i][2] for i in kept)] if packed else [rows[i][2] for i in kept]
    n_rg = len(widths)
    out_specs = [pl.BlockSpec((tm, w), lambda i_: (i_, 0)) for w in widths] + [_full_spec(p) for p in params[:ndp]]
    out_shape = [jax.ShapeDtypeStruct((t, w), row_grad[kept[q]]) for q, w in enumerate(widths)] + [
        jax.ShapeDtypeStruct(p.shape, F32) for p in params[:ndp]]
    res = pl.pallas_call(
        body, name=name, grid=(t // tm,),
        in_specs=_row_specs(rows, tm) + [_full_spec(p) for p in params] + cot_specs + add_specs,
        out_specs=out_specs, out_shape=out_shape,
        compiler_params=_cp("arbitrary"),
    )(*[r[0] for r in rows], *params, *flat_cots, *[add_to[i] for i in add_idx])
    return list(res[:n_rg]), list(res[n_rg:])


def _rms(x, g):
    xf = x.astype(F32)
    return xf * lax.rsqrt(jnp.mean(xf * xf, axis=-1, keepdims=True) + NORM_EPS) * g


def _softplus(x):
    return jnp.maximum(x, 0.0) + jnp.log(1.0 + jnp.exp(-jnp.abs(x)))


def _fn_pre(x, g):
    return (_rms(x, g).astype(BF16),)


def _fn_res(x, u, g_post):
    return (x + _rms(u, g_post),)


def _fn_res_pre(x, u, g_post, g_pre):
    xn = x + _rms(u, g_post)
    return xn, _rms(xn, g_pre).astype(BF16)


def _fn_mix(zga, zgb, ya, yb):
    return ((jax.nn.sigmoid(zga) * ya + jax.nn.sigmoid(zgb) * yb).astype(BF16),)


def _fn_swiglu(gate, up):
    return ((gate * jax.nn.sigmoid(gate) * up).astype(BF16),)


def _fn_prep(zk, zw, za, zg, decay_base, d_up, iclr_base, i_up, g_up, kns, kis, e_hd, e_dh):
    w_log = -_softplus(-(decay_base + _dot(jnp.tanh(zw), d_up))) - 0.5
    lw = -jnp.exp(w_log)
    a = jax.nn.sigmoid(iclr_base + _dot(za, i_up))
    g = _dot(jax.nn.sigmoid(zg), g_up)
    kn = zk * kns
    ss = _dot_split_a(kn * kn, e_dh)
    inv = lax.rsqrt(jnp.maximum(ss, 1e-24))
    kk = kn * _dot_split_a(inv, e_hd)
    k2 = zk * (1.0 + (a - 1.0) * kis)
    return lw, k2, kk, a, g


def _fn_post(y, r, k2, v, g, lnx_w, lnx_b, bonus, e_hd, e_dh):
    mu = _dot_split_a(_dot_split_a(y, e_dh) * (1.0 / HEAD), e_hd)
    yc = y - mu
    var = _dot_split_a(yc * yc, e_dh) * (1.0 / HEAD)
    yn = yc * _dot_split_a(lax.rsqrt(var + GROUP_NORM_EPS), e_hd)
    bs = _dot_split_a(_dot_split_a(r * k2 * bonus, e_dh), e_hd)
    return (((yn * lnx_w + lnx_b + bs * v) * g).astype(BF16),)


def _shift_fwd(p, col0, ncols, mix, seq, *, name, cw=256):
    t = p.shape[0]
    assert col0 % cw == 0 and ncols % cw == 0 and t % seq == 0
    cb0 = col0 // cw

    def body(p_ref, m_ref, z_ref):
        pv = p_ref[...]
        row = lax.broadcasted_iota(jnp.int32, pv.shape, 0)
        prev = jnp.where(row == 0, 0.0, pltpu.roll(pv, 1, axis=0))
        z_ref[...] = pv + (prev - pv) * m_ref[...]

    return pl.pallas_call(
        body, name=name, grid=(t // seq, ncols // cw),
        in_specs=[pl.BlockSpec((seq, cw), lambda b, c: (b, c + cb0)), pl.BlockSpec((1, cw), lambda b, c: (0, c))],
        out_specs=pl.BlockSpec((seq, cw), lambda b, c: (b, c)),
        out_shape=jax.ShapeDtypeStruct((t, ncols), F32),
        compiler_params=_cp("parallel", "parallel"),
    )(p, mix)


def _shift_bwd(p, col0, ncols, mix, dz_parts, seq, *, name, cw=256):
    t = p.shape[0]
    cb0 = col0 // cw
    n = len(dz_parts)

    def body(*refs):
        p_ref, m_ref = refs[:2]
        dp_ref, dm_ref = refs[2 + n:]
        dz = refs[2][...].astype(F32)
        for r in refs[3:2 + n]:
            dz = dz + r[...].astype(F32)
        pv = p_ref[...]
        mixv = m_ref[...]
        row = lax.broadcasted_iota(jnp.int32, pv.shape, 0)
        prev = jnp.where(row == 0, 0.0, pltpu.roll(pv, 1, axis=0))
        u = dz * mixv
        nxt = jnp.where(row == seq - 1, 0.0, pltpu.roll(u, seq - 1, axis=0))
        dp_ref[...] = (dz - u + nxt).astype(dp_ref.dtype)

        @pl.when(pl.program_id(1) == 0)
        def _():
            dm_ref[...] = jnp.zeros_like(dm_ref)

        dm_ref[...] += jnp.sum(dz * (prev - pv), axis=0, keepdims=True)

    return pl.pallas_call(
        body, name=name, grid=(ncols // cw, t // seq),
        in_specs=[pl.BlockSpec((seq, cw), lambda c, b: (b, c + cb0)), pl.BlockSpec((1, cw), lambda c, b: (0, c))]
        + [pl.BlockSpec((seq, cw), lambda c, b: (b, c))] * n,
        out_specs=[pl.BlockSpec((seq, cw), lambda c, b: (b, c)), pl.BlockSpec((1, cw), lambda c, b: (0, c))],
        out_shape=[jax.ShapeDtypeStruct((t, ncols), BF16), jax.ShapeDtypeStruct((1, ncols), F32)],
        compiler_params=_cp("parallel", "arbitrary"),
    )(p, mix, *dz_parts)


def _each(f, *lists):
    return [f(*xs) for xs in zip(*lists)]


def _tri_inv(low):
    c = low[0].shape[0]
    ti = lax.broadcasted_iota(jnp.int32, (c, c), 0)
    si = lax.broadcasted_iota(jnp.int32, (c, c), 1)
    eye = (ti == si).astype(F32)
    inside = (ti // 4) == (si // 4)
    base = [jnp.where(inside, m, 0.0) for m in low]
    acc = _each(lambda m: _dot(eye - m, eye + _dot(m, m)), base)
    size = 8
    while size <= c:
        wider = (ti // size) == (si // size)
        keep = jnp.logical_and(wider, jnp.logical_not(inside))
        acc = _each(lambda p, m: p - _dot(_dot(p, jnp.where(keep, m, 0.0)), p), acc, low)
        inside, size = wider, size * 2
    return acc


@jax.custom_vjp
def _tri_inv_known(low, inv):
    return inv


def _tri_inv_known_fwd(low, inv):
    return inv, inv


def _tri_inv_known_bwd(inv, g):
    dlow = _each(lambda t, gg: -_dot(_dot(t, gg, _TN), t, _NT), inv, g)
    return dlow, _each(jnp.zeros_like, inv)


_tri_inv_known.defvjp(_tri_inv_known_fwd, _tri_inv_known_bwd)


def _wkv_chunk(s0, r, lw, k, v, kk, a, inv=None):
    c = r[0].shape[0]
    ti = lax.broadcasted_iota(jnp.int32, (c, c), 0)
    si = lax.broadcasted_iota(jnp.int32, (c, c), 1)
    incl, strict = ti >= si, ti > si
    tri = incl.astype(F32)
    cum = _each(lambda x: _dot_split_b(tri, x, 3), lw)
    eg = _each(jnp.exp, cum)
    egp = _each(lambda cs, x: jnp.exp(cs - x), cum, lw)
    ei = _each(lambda cs: jnp.exp(-cs), cum)
    rh, kkh, kt = _each(jnp.multiply, r, eg), _each(jnp.multiply, kk, egp), _each(jnp.multiply, k, ei)
    bt = _each(lambda p, q, e: (p * q) * e, a, kk, ei)
    lb = _each(lambda p, q: jnp.where(strict, _dot_nt(p, q), 0.0), kkh, bt)
    lk = _each(lambda p, q: jnp.where(strict, _dot_nt(p, q), 0.0), kkh, kt)
    mb = _each(lambda p, q: jnp.where(incl, _dot_nt(p, q), 0.0), rh, bt)
    mk = _each(lambda p, q: jnp.where(incl, _dot_nt(p, q), 0.0), rh, kt)
    rhs = _each(lambda p, s, m, x: _dot_nt(p, s) + _dot(m, x), kkh, s0, lk, v)
    inv = _tri_inv(lb) if inv is None else _tri_inv_known(lb, inv)
    u = _each(lambda t, x: -_dot(t, x), inv, rhs)
    y = _each(lambda p, s, m1, uu, m2, x: _dot_nt(p, s) + _dot(m1, uu) + _dot(m2, x), rh, s0, mb, u, mk, v)
    s1 = _each(lambda s, uu, b, x, kq, w: (s + _dot_tn(uu, b) + _dot_tn(x, kq)) * jnp.exp(jnp.sum(w, axis=0, keepdims=True)),
               s0, u, bt, v, kt, lw)
    return y, s1, inv


WKV_HEADS = 16
WKV_COLS = WKV_HEADS * HEAD
WKV_GROUPS = N_HEADS // WKV_HEADS


def _head_cols(ref):
    return [ref[:, h * HEAD:(h + 1) * HEAD] for h in range(ref.shape[1] // HEAD)]


def _wkv_specs(seq, rev):
    nc = seq // CHUNK

    def rows(col0):
        cb0 = col0 // WKV_COLS
        if rev:
            return pl.BlockSpec((CHUNK, WKV_COLS), lambda b, h, c: (b * nc + nc - 1 - c, cb0 + h))
        return pl.BlockSpec((CHUNK, WKV_COLS), lambda b, h, c: (b * nc + c, cb0 + h))

    if rev:
        st = pl.BlockSpec((1, 1, WKV_HEADS, HEAD, HEAD), lambda b, h, c: (b * WKV_GROUPS + h, nc - 1 - c, 0, 0, 0))
    else:
        st = pl.BlockSpec((1, 1, WKV_HEADS, HEAD, HEAD), lambda b, h, c: (b * WKV_GROUPS + h, c, 0, 0, 0))
    return rows, st


def _wkv_fwd(z_rkv, lw, k2, kk, a, seq):
    t = z_rkv.shape[0]
    nb, nc = t // seq, seq // CHUNK
    rows, st = _wkv_specs(seq, False)

    def body(r_ref, v_ref, lw_ref, k_ref, kk_ref, a_ref, y_ref, st_ref, inv_ref, s_scr):
        @pl.when(pl.program_id(2) == 0)
        def _():
            s_scr[...] = jnp.zeros_like(s_scr)

        s0 = [s_scr[h] for h in range(WKV_HEADS)]
        y, s1, inv = _wkv_chunk(s0, *[_head_cols(ref) for ref in (r_ref, lw_ref, k_ref, v_ref, kk_ref, a_ref)])
        for h in range(WKV_HEADS):
            st_ref[0, 0, h] = s0[h]
            inv_ref[0, 0, h] = inv[h]
            y_ref[:, h * HEAD:(h + 1) * HEAD] = y[h]
            s_scr[h] = s1[h]

    per_chunk = jax.ShapeDtypeStruct((nb * WKV_GROUPS, nc, WKV_HEADS, HEAD, HEAD), F32)
    return pl.pallas_call(
        body, name="wkv_fwd", grid=(nb, WKV_GROUPS, nc),
        in_specs=[rows(0), rows(2 * D), rows(0), rows(0), rows(0), rows(0)],
        out_specs=[rows(0), st, st],
        out_shape=[jax.ShapeDtypeStruct((t, D), F32), per_chunk, per_chunk],
        scratch_shapes=[pltpu.VMEM((WKV_HEADS, HEAD, HEAD), F32)],
        compiler_params=_cp("parallel", "parallel", "arbitrary"),
    )(z_rkv, z_rkv, lw, k2, kk, a)


def _wkv_bwd(z_rkv, lw, k2, kk, a, states, invs, dy, seq):
    t = z_rkv.shape[0]
    nb, nc = t // seq, seq // CHUNK
    rows, st = _wkv_specs(seq, True)

    def body(r_ref, v_ref, lw_ref, k_ref, kk_ref, a_ref, st_ref, inv_ref, dy_ref,
             dr_ref, dlw_ref, dk_ref, dv_ref, dkk_ref, da_ref, ds_scr):
        @pl.when(pl.program_id(2) == 0)
        def _():
            ds_scr[...] = jnp.zeros_like(ds_scr)

        s0 = [st_ref[0, 0, h] for h in range(WKV_HEADS)]
        inv = [inv_ref[0, 0, h] for h in range(WKV_HEADS)]
        _, vjp = jax.vjp(lambda *args: _wkv_chunk(*args, inv=inv)[:2],
                         s0, *[_head_cols(ref) for ref in (r_ref, lw_ref, k_ref, v_ref, kk_ref, a_ref)])
        grads = vjp(([x.astype(F32) for x in _head_cols(dy_ref)], [ds_scr[h] for h in range(WKV_HEADS)]))
        for h in range(WKV_HEADS):
            ds_scr[h] = grads[0][h]
            for ref, g in zip((dr_ref, dlw_ref, dk_ref, dv_ref, dkk_ref, da_ref), grads[1:]):
                ref[:, h * HEAD:(h + 1) * HEAD] = g[h]

    return pl.pallas_call(
        body, name="wkv_bwd", grid=(nb, WKV_GROUPS, nc),
        in_specs=[rows(0), rows(2 * D), rows(0), rows(0), rows(0), rows(0), st, st, rows(0)],
        out_specs=[rows(0)] * 6,
        out_shape=[jax.ShapeDtypeStruct((t, D), F32)] * 6,
        scratch_shapes=[pltpu.VMEM((WKV_HEADS, HEAD, HEAD), F32)],
        compiler_params=_cp("parallel", "parallel", "arbitrary"),
    )(z_rkv, z_rkv, lw, k2, kk, a, states, invs, dy)


def _softmax(s):
    e = jnp.exp(s - jnp.max(s, axis=-1, keepdims=True))
    return e / jnp.sum(e, axis=-1, keepdims=True)


ATT_HEADS = 8
ATT_COLS = ATT_HEADS * HEAD
ATT_GROUPS = N_HEADS // ATT_HEADS


def _attn_chunk(q, kb, vb, bias, valid):
    s = _each(lambda x, y, z: jnp.where(valid, _dot_nt(x, y) * (HEAD ** -0.5) + z, MASK_VALUE), q, kb, bias)
    return _each(_dot, _each(_softmax, s), vb)


def _pad_fill(pad_ref, src_ref):
    pad_ref[0:LEFT, :] = jnp.zeros((LEFT, pad_ref.shape[1]), pad_ref.dtype)
    pad_ref[LEFT:, :] = src_ref[...].astype(pad_ref.dtype)


def _band_heads(pad_ref, start):
    return [pad_ref[pl.ds(start, BAND), h * HEAD:(h + 1) * HEAD].astype(F32) for h in range(ATT_HEADS)]


def _band_valid(c):
    return (c * CHUNK - LEFT + lax.broadcasted_iota(jnp.int32, (1, BAND), 1)) >= 0


def _attn_fwd(proj, bias, seq):
    t = proj.shape[0]
    nb, nc = t // seq, seq // CHUNK
    cq = C_Q // ATT_COLS

    def body(q_ref, k_ref, v_ref, b_ref, o_ref, kpad, vpad):
        c = pl.program_id(2)

        @pl.when(c == 0)
        def _():
            _pad_fill(kpad, k_ref)
            _pad_fill(vpad, v_ref)

        start = pl.multiple_of(c * CHUNK, CHUNK)
        o = _attn_chunk(_head_cols(q_ref), _band_heads(kpad, start), _band_heads(vpad, start),
                        [b_ref[h] for h in range(ATT_HEADS)], _band_valid(c))
        for h in range(ATT_HEADS):
            o_ref[:, h * HEAD:(h + 1) * HEAD] = o[h].astype(o_ref.dtype)

    return pl.pallas_call(
        body, name="attn_fwd", grid=(ATT_GROUPS, nb, nc),
        in_specs=[pl.BlockSpec((CHUNK, ATT_COLS), lambda h, b, c: (b * nc + c, cq + h)),
                  pl.BlockSpec((seq, ATT_COLS), lambda h, b, c: (b, cq + ATT_GROUPS + h)),
                  pl.BlockSpec((seq, ATT_COLS), lambda h, b, c: (b, cq + 2 * ATT_GROUPS + h)),
                  pl.BlockSpec((ATT_HEADS, CHUNK, BAND), lambda h, b, c: (h, 0, 0))],
        out_specs=pl.BlockSpec((CHUNK, ATT_COLS), lambda h, b, c: (b * nc + c, h)),
        out_shape=jax.ShapeDtypeStruct((t, D), BF16),
        scratch_shapes=[pltpu.VMEM((seq + LEFT, ATT_COLS), BF16)] * 2,
        compiler_params=_cp("parallel", "arbitrary", "arbitrary"),
    )(proj, proj, proj, bias)


def _attn_bwd(proj, bias, do, seq):
    t = proj.shape[0]
    nb, nc = t // seq, seq // CHUNK
    cq = C_Q // ATT_COLS

    def body(q_ref, k_ref, v_ref, b_ref, do_ref, dq_ref, dk_ref, dv_ref, db_ref, kpad, vpad, dkpad, dvpad):
        b, c = pl.program_id(1), pl.program_id(2)

        @pl.when(c == 0)
        def _():
            _pad_fill(kpad, k_ref)
            _pad_fill(vpad, v_ref)
            dkpad[...] = jnp.zeros_like(dkpad)
            dvpad[...] = jnp.zeros_like(dvpad)

        @pl.when(jnp.logical_and(b == 0, c == 0))
        def _():
            db_ref[...] = jnp.zeros_like(db_ref)

        start = pl.multiple_of(c * CHUNK, CHUNK)
        _, vjp = jax.vjp(functools.partial(_attn_chunk, valid=_band_valid(c)),
                         _head_cols(q_ref), _band_heads(kpad, start), _band_heads(vpad, start),
                         [b_ref[h] for h in range(ATT_HEADS)])
        dq, dkb, dvb, dbias = vjp([x.astype(F32) for x in _head_cols(do_ref)])
        for h in range(ATT_HEADS):
            sl = slice(h * HEAD, (h + 1) * HEAD)
            dq_ref[:, sl] = dq[h].astype(dq_ref.dtype)
            dkpad[pl.ds(start, BAND), sl] += dkb[h].astype(F32)
            dvpad[pl.ds(start, BAND), sl] += dvb[h].astype(F32)
            db_ref[h] += dbias[h]

        @pl.when(c == nc - 1)
        def _():
            dk_ref[...] = dkpad[LEFT:, :].astype(dk_ref.dtype)
            dv_ref[...] = dvpad[LEFT:, :].astype(dv_ref.dtype)

    kv_out = pl.BlockSpec((seq, ATT_COLS), lambda h, b, c: (b, h))
    return pl.pallas_call(
        body, name="attn_bwd", grid=(ATT_GROUPS, nb, nc),
        in_specs=[pl.BlockSpec((CHUNK, ATT_COLS), lambda h, b, c: (b * nc + c, cq + h)),
                  pl.BlockSpec((seq, ATT_COLS), lambda h, b, c: (b, cq + ATT_GROUPS + h)),
                  pl.BlockSpec((seq, ATT_COLS), lambda h, b, c: (b, cq + 2 * ATT_GROUPS + h)),
                  pl.BlockSpec((ATT_HEADS, CHUNK, BAND), lambda h, b, c: (h, 0, 0)),
                  pl.BlockSpec((CHUNK, ATT_COLS), lambda h, b, c: (b * nc + c, h))],
        out_specs=[pl.BlockSpec((CHUNK, ATT_COLS), lambda h, b, c: (b * nc + c, h)), kv_out, kv_out,
                   pl.BlockSpec((ATT_HEADS, CHUNK, BAND), lambda h, b, c: (h, 0, 0))],
        out_shape=[jax.ShapeDtypeStruct((t, D), BF16)] * 3 + [jax.ShapeDtypeStruct((N_HEADS, CHUNK, BAND), F32)],
        scratch_shapes=[pltpu.VMEM((seq + LEFT, ATT_COLS), BF16)] * 2 + [pltpu.VMEM((seq + LEFT, ATT_COLS), F32)] * 2,
        compiler_params=_cp("parallel", "arbitrary", "arbitrary"),
    )(proj, proj, proj, bias, do)


def _xattn_tile(q, k, v):
    s = _dot_nt(q, k) * ((MEM_WIDTH // MEM_HEADS) ** -0.5)
    return _dot(_softmax(s), v)


def _xattn_fwd(qm, kvm, seq, n_mem, tq=512):
    t = qm.shape[0]
    tq = min(tq, seq)
    nb, nq = t // seq, seq // tq

    def body(q_ref, k_ref, v_ref, o_ref):
        o_ref[...] = _xattn_tile(q_ref[...], k_ref[...], v_ref[...]).astype(o_ref.dtype)

    return pl.pallas_call(
        body, name="xattn_fwd", grid=(nb, MEM_HEADS, nq),
        in_specs=[pl.BlockSpec((tq, LANE), lambda b, h, i: (b * nq + i, h)),
                  pl.BlockSpec((n_mem, LANE), lambda b, h, i: (b, h)),
                  pl.BlockSpec((n_mem, LANE), lambda b, h, i: (b, MEM_HEADS + h))],
        out_specs=pl.BlockSpec((tq, LANE), lambda b, h, i: (b * nq + i, h)),
        out_shape=jax.ShapeDtypeStruct((t, MEM_WIDTH), BF16),
        compiler_params=_cp("parallel", "parallel", "parallel"),
    )(qm, kvm, kvm)


def _xattn_bwd(qm, kvm, do, seq, n_mem, tq=512):
    t = qm.shape[0]
    tq = min(tq, seq)
    nb, nq = t // seq, seq // tq

    def body(q_ref, k_ref, v_ref, do_ref, dq_ref, dkv_ref, dk_acc, dv_acc):
        i = pl.program_id(2)

        @pl.when(i == 0)
        def _():
            dk_acc[...] = jnp.zeros_like(dk_acc)
            dv_acc[...] = jnp.zeros_like(dv_acc)

        _, vjp = jax.vjp(_xattn_tile, q_ref[...], k_ref[...], v_ref[...])
        dq, dk, dv = vjp(do_ref[...].astype(F32))
        dq_ref[...] = dq.astype(dq_ref.dtype)
        dk_acc[...] += dk
        dv_acc[...] += dv

        @pl.when(i == nq - 1)
        def _():
            dkv_ref[0] = dk_acc[...].astype(dkv_ref.dtype)
            dkv_ref[1] = dv_acc[...].astype(dkv_ref.dtype)

    dq, dkv = pl.pallas_call(
        body, name="xattn_bwd", grid=(nb, MEM_HEADS, nq),
        in_specs=[pl.BlockSpec((tq, LANE), lambda b, h, i: (b * nq + i, h)),
                  pl.BlockSpec((n_mem, LANE), lambda b, h, i: (b, h)),
                  pl.BlockSpec((n_mem, LANE), lambda b, h, i: (b, MEM_HEADS + h)),
                  pl.BlockSpec((tq, LANE), lambda b, h, i: (b * nq + i, h))],
        out_specs=[pl.BlockSpec((tq, LANE), lambda b, h, i: (b * nq + i, h)),
                   pl.BlockSpec((2, n_mem, LANE), lambda b, h, i: (0, b, h))],
        out_shape=[jax.ShapeDtypeStruct((t, MEM_WIDTH), BF16), jax.ShapeDtypeStruct((2, nb * n_mem, MEM_WIDTH), BF16)],
        scratch_shapes=[pltpu.VMEM((n_mem, LANE), F32)] * 2,
        compiler_params=_cp("parallel", "parallel", "arbitrary"),
    )(qm, kvm, kvm, do)
    return dq, jnp.concatenate([dkv[0], dkv[1]], axis=1)


def _loss_head(x, u, g_post, target, tm=256):
    t, d = x.shape
    tm = min(tm, t)

    def tile_loss(xv, uv, gv, tv):
        diff = _fn_res(xv, uv, gv)[0] - tv
        return 0.5 * jnp.sum(jnp.mean(diff * diff, axis=-1, keepdims=True), axis=0, keepdims=True)

    def body(x_ref, u_ref, g_ref, t_ref, l_ref, dx_ref, du_ref, dg_ref):
        @pl.when(pl.program_id(0) == 0)
        def _():
            l_ref[...] = jnp.zeros_like(l_ref)
            dg_ref[...] = jnp.zeros_like(dg_ref)

        tv = t_ref[...]
        part, vjp = jax.vjp(lambda xv, uv, gv: tile_loss(xv, uv, gv, tv), x_ref[...], u_ref[...], g_ref[...])
        dx, du, dg = vjp(jnp.ones((1, 1), F32))
        l_ref[...] += part
        dx_ref[...] = dx
        du_ref[...] = du.astype(du_ref.dtype)
        dg_ref[...] += dg

    rows = pl.BlockSpec((tm, d), lambda i: (i, 0))
    vec = pl.BlockSpec((1, d), lambda i: (0, 0))
    return pl.pallas_call(
        body, name="loss_head", grid=(t // tm,),
        in_specs=[rows, rows, vec, rows],
        out_specs=[pl.BlockSpec((8, LANE), lambda i: (0, 0)), rows, rows, vec],
        out_shape=[jax.ShapeDtypeStruct((8, LANE), F32), jax.ShapeDtypeStruct((t, d), F32),
                   jax.ShapeDtypeStruct((t, d), BF16), jax.ShapeDtypeStruct((1, d), F32)],
        compiler_params=_cp("arbitrary"),
    )(x, u, g_post, target)


def _mesh_pos():
    return lax.axis_index("x"), lax.axis_index("y"), lax.axis_index("c")


def _peer(pos, d):
    x, y, c = pos
    return ((1 - x) if d & 4 else x, (1 - y) if d & 2 else y, (1 - c) if d & 1 else c)


def _flat(pos):
    return 4 * pos[0] + 2 * pos[1] + pos[2]


def _exchange(arrays, scatter, *, name):
    n = len(arrays)
    shapes = [a.shape[1:] if scatter else a.shape for a in arrays]

    def body(*refs):
        ins, outs = refs[:n], refs[n:2 * n]
        send, recv, loc = refs[2 * n:]
        pos = _mesh_pos()
        me = _flat(pos)
        pending = []
        for i in range(n):
            own = pltpu.make_async_copy(ins[i].at[me] if scatter else ins[i], outs[i].at[me], loc.at[i])
            own.start()
            pending.append(own)
            for d in range(1, N_DEV):
                peer = _peer(pos, d)
                src = ins[i].at[_flat(peer)] if scatter else ins[i]
                out_cp = pltpu.make_async_remote_copy(
                    src_ref=src, dst_ref=outs[i].at[me], send_sem=send.at[i, d - 1], recv_sem=recv.at[i, d - 1],
                    device_id=peer, device_id_type=pl.DeviceIdType.MESH)
                out_cp.start()
                pending.append(out_cp)
        for i in range(n):
            own = pending[i * N_DEV]
            for d in range(1, N_DEV):
                peer = _peer(pos, d)
                src = ins[i].at[_flat(peer)] if scatter else ins[i]
                pending[i * N_DEV + d].wait_send()
                pltpu.make_async_remote_copy(
                    src_ref=src, dst_ref=outs[i].at[_flat(peer)], send_sem=send.at[i, d - 1], recv_sem=recv.at[i, d - 1],
                    device_id=peer, device_id_type=pl.DeviceIdType.MESH).wait_recv()
            own.wait()

    hbm = pl.BlockSpec(memory_space=pltpu.HBM)
    return pl.pallas_call(
        body, name=name,
        in_specs=[hbm] * n, out_specs=[hbm] * n,
        out_shape=[jax.ShapeDtypeStruct((N_DEV,) + tuple(s), a.dtype) for s, a in zip(shapes, arrays)],
        scratch_shapes=[pltpu.SemaphoreType.DMA((n, N_DEV - 1)), pltpu.SemaphoreType.DMA((n, N_DEV - 1)),
                        pltpu.SemaphoreType.DMA((n,))],
    )(*arrays)


_HBM = pl.BlockSpec(memory_space=pltpu.HBM)
_SEM = pl.BlockSpec(memory_space=pltpu.SEMAPHORE)
_DATAFLOW = pltpu.SideEffectType.DATAFLOW_SIDE_EFFECTING


_ALL_PEERS = tuple(range(1, N_DEV))
_SIBLING_AND_SAME_CORE = (1, 2, 4, 6)


def _remote_copies(ins, lands, send, recv, scatter, dists):
    pos = _mesh_pos()
    me = _flat(pos)
    out = []
    for i in range(len(ins)):
        for j, d in enumerate(dists):
            peer = _peer(pos, d)
            src = ins[i].at[_flat(peer)] if scatter else ins[i]
            pair = i * len(dists) + j
            sems = dict(send_sem=send.at[pair], recv_sem=recv.at[pair], device_id=peer,
                        device_id_type=pl.DeviceIdType.MESH)
            out.append((pltpu.make_async_remote_copy(src_ref=src, dst_ref=lands[i].at[me], **sems),
                        pltpu.make_async_remote_copy(src_ref=src, dst_ref=lands[i].at[_flat(peer)], **sems)))
    return out


def _exchange_start(arrays, scatter, after, *, name, dists=_ALL_PEERS):
    n = len(arrays)
    shapes = [a.shape[1:] if scatter else a.shape for a in arrays]
    lands = [pltpu.with_memory_space_constraint(lax.empty((N_DEV,) + tuple(s), a.dtype), pltpu.HBM)
             for s, a in zip(shapes, arrays)]
    srcs = [pltpu.with_memory_space_constraint(a, pltpu.HBM) for a in arrays]

    def body(*refs):
        ins, land_refs = refs[:n], refs[n:2 * n]
        send, recv, token = refs[2 * n + 1], refs[2 * n + 2], refs[-1]
        for going, _ in _remote_copies(ins, land_refs, send, recv, scatter, dists):
            going.start()
        token[...] = jnp.zeros_like(token)

    sems = pltpu.SemaphoreType.DMA((n * len(dists),))
    res = pl.pallas_call(
        body, name=name,
        out_shape=(sems, sems, *[pltpu.HBM(a.shape, a.dtype) for a in srcs + lands], jax.ShapeDtypeStruct((8, LANE), F32)),
        in_specs=[_HBM] * (2 * n) + [pl.BlockSpec(memory_space=pl.ANY)],
        out_specs=(_SEM, _SEM, *[_HBM] * (2 * n), pl.BlockSpec(memory_space=pltpu.VMEM)),
        input_output_aliases={i: 2 + i for i in range(2 * n)},
        compiler_params=pltpu.CompilerParams(has_side_effects=_DATAFLOW),
    )(*srcs, *lands, after)
    return (n, scatter, dists, res[0], res[1], list(res[2:2 + 2 * n])), res[-1]


def _exchange_wait(handle, after, own, *, name):
    n, scatter, dists, send, recv, thru = handle

    def body(*refs):
        ins, land_refs = refs[:n], refs[n:2 * n]
        for going, coming in _remote_copies(ins, land_refs, refs[2 * n], refs[2 * n + 1], scatter, dists):
            going.wait_send()
            coming.wait_recv()

    res = pl.pallas_call(
        body, name=name,
        out_shape=tuple(pltpu.HBM(a.shape, a.dtype) for a in thru),
        in_specs=[_HBM] * (2 * n) + [_SEM, _SEM] + [pl.BlockSpec(memory_space=pl.ANY)] * len(after),
        out_specs=tuple([_HBM] * (2 * n)),
        input_output_aliases={i: i for i in range(2 * n)},
        compiler_params=pltpu.CompilerParams(has_side_effects=_DATAFLOW),
    )(*thru, send, recv, *after)
    me = _flat(_mesh_pos())
    return [lax.dynamic_update_slice_in_dim(land, o[None].astype(land.dtype), me, 0) for land, o in zip(res[n:], own)]


_OTHER_CHIPS = (2, 4, 6)


def _relay_to_sibling(gathered, *, name):
    n, k = len(gathered), len(_OTHER_CHIPS)

    def body(*refs):
        ins, outs = refs[:n], refs[n:2 * n]
        send, recv = refs[2 * n:]
        pos = _mesh_pos()
        copies = []
        for i in range(n):
            for j, d in enumerate(_OTHER_CHIPS):
                cp = pltpu.make_async_remote_copy(
                    src_ref=ins[i].at[_flat(_peer(pos, d))], dst_ref=outs[i].at[j],
                    send_sem=send.at[i * k + j], recv_sem=recv.at[i * k + j],
                    device_id=_peer(pos, 1), device_id_type=pl.DeviceIdType.MESH)
                cp.start()
                copies.append(cp)
        for cp in copies:
            cp.wait()

    return pl.pallas_call(
        body, name=name, in_specs=[_HBM] * n, out_specs=[_HBM] * n,
        out_shape=[jax.ShapeDtypeStruct((k,) + g.shape[1:], g.dtype) for g in gathered],
        scratch_shapes=[pltpu.SemaphoreType.DMA((n * k,)), pltpu.SemaphoreType.DMA((n * k,))],
    )(*gathered)


def _adamw(parts, w, m, v, *, name, tr=128, after=None):
    r, c = w.shape
    align = 8 * 4 // parts.dtype.itemsize
    tr = max(d for d in range(align, min(tr, r) + 1, align) if r % d == 0)
    n_after = 0 if after is None else 1

    def body(p_ref, w_ref, m_ref, v_ref, *rest):
        g_ref, d_ref, nm_ref, nv_ref = rest[n_after:]
        g = p_ref[0].astype(F32)
        for j in range(1, N_DEV):
            g = g + p_ref[j].astype(F32)
        m2 = ADAM_B1 * m_ref[...] + (1.0 - ADAM_B1) * g
        v2 = ADAM_B2 * v_ref[...] + (1.0 - ADAM_B2) * (g * g)
        m_hat = m2 / (1.0 - ADAM_B1 ** ADAM_STEP)
        v_hat = v2 / (1.0 - ADAM_B2 ** ADAM_STEP)
        g_ref[...] = g
        d_ref[...] = -ADAM_LR * (m_hat / (jnp.sqrt(v_hat) + ADAM_EPS) + ADAM_WD * w_ref[...])
        nm_ref[...] = m2
        nv_ref[...] = v2

    spec = pl.BlockSpec((tr, c), lambda i: (i, 0))
    return pl.pallas_call(
        body, name=name, grid=(r // tr,),
        in_specs=[pl.BlockSpec((N_DEV, tr, c), lambda i: (0, i, 0)), spec, spec, spec]
        + [pl.BlockSpec(memory_space=pl.ANY)] * n_after,
        out_specs=[spec] * 4, out_shape=[jax.ShapeDtypeStruct((r, c), F32)] * 4,
        compiler_params=_cp("parallel"),
    )(parts, w, m, v, *([] if after is None else [after]))


def _cols_to_full(g):
    return jnp.transpose(g, (1, 0, 2)).reshape(g.shape[1], N_DEV * g.shape[2])


def _full_to_cols(w):
    r, c = w.shape
    return jnp.transpose(w.reshape(r, N_DEV, c // N_DEV), (1, 0, 2))


def _pad_cols(a, width):
    return jnp.pad(a, ((0, 0), (0, width - a.shape[1])))


def _pad_lora(w):
    return jnp.concatenate([
        _pad_cols(w[:, :LORA_W], 128), _pad_cols(w[:, LORA_W:LORA_W + LORA_A], 128),
        _pad_cols(w[:, LORA_W + LORA_A:], 256)], axis=1)


def _unpad_lora(wp):
    return jnp.concatenate([wp[:, :LORA_W], wp[:, 128:128 + LORA_A], wp[:, 256:256 + LORA_G]], axis=1)


def _permute_in(w):
    rk = 3 * D
    lo = rk + LORA_W + LORA_A + LORA_G
    return jnp.concatenate([w[:, :rk], w[:, lo:], _pad_lora(w[:, rk:lo])], axis=1)


def _unpermute_in(wp):
    return jnp.concatenate([wp[:, :3 * D], _unpad_lora(wp[:, C_LORA:]), wp[:, 3 * D:C_LORA]], axis=1)


def _rel_index():
    dist = jnp.arange(CHUNK)[:, None] - jnp.arange(BAND)[None, :] + LEFT
    return (jnp.minimum(dist, REL_CLIP) + (CHUNK - 1)).reshape(-1)


def _local_step(x, mem, target, wt, seq, n_mem, comm):
    t = x.shape[0]
    row = lambda a: a.reshape(1, -1).astype(F32)
    g_pre_mix, g_post_mix = row(wt["g_pre_mix"]), row(wt["g_post_mix"])
    g_pre_cross, g_post_cross, g_mem = row(wt["g_pre_cross"]), row(wt["g_post_cross"]), row(wt["g_mem"])
    g_pre_ffn, g_post_ffn = row(wt["g_pre_ffn"]), row(wt["g_post_ffn"])
    mix = row(wt["shift_mix"])
    mix_rkv, mix_lora = mix[:, :3 * D], _pad_lora(mix[:, 3 * D:])
    decay_base, iclr_base = row(wt["decay_base"]), row(wt["iclr_base"])
    kns, kis = row(wt["key_norm_scale"]), row(wt["key_iclr_scale"])
    lnx_w, lnx_b, bonus = row(wt["lnx_w"]), row(wt["lnx_b"]), row(wt["bonus_scale"])
    e_dh = (jnp.arange(D)[:, None] // HEAD == jnp.arange(N_HEADS)[None, :]).astype(F32)
    e_hd = e_dh.T
    onehot = (jnp.arange(REL_TABLE)[:, None] == _rel_index()[None, :]).astype(BF16)

    begun = comm.begun
    (h1,) = _rowwise(_fn_pre, [_win(x)], [g_pre_mix], [(D, BF16)], name="pre_mix", tm=512, after=begun)
    (mn,) = _rowwise(_fn_pre, [_win(mem)], [g_mem], [(D, BF16)], name="pre_mem", tm=512, after=begun)
    bias = _mm(wt["rel_bias"].astype(F32), onehot, name="mm_bias", split_a=3, after=begun).reshape(N_HEADS, CHUNK, BAND)
    wt = {**wt, **comm.first_weights([h1, mn, bias])}
    w_in = wt["w_in_p"]
    d_up = jnp.pad(wt["decay_up"].astype(F32), ((0, 128 - LORA_W), (0, 0)))
    i_up = jnp.pad(wt["iclr_up"].astype(F32), ((0, 128 - LORA_A), (0, 0)))
    g_up = jnp.pad(wt["gate_up"].astype(F32), ((0, 256 - LORA_G), (0, 0)))
    proj = _mm(h1, w_in, name="mm_in", after=comm.first_token)
    z_rkv = _shift_fwd(proj, 0, 3 * D, mix_rkv, seq, name="shift_rkv")
    z_lora = _shift_fwd(proj, C_LORA, 512, mix_lora, seq, name="shift_lora")
    prep_rows = [_win(z_rkv, D, D), _win(z_lora, 0, 128), _win(z_lora, 128, 128), _win(z_lora, 256, 256)]
    prep_params = [decay_base, d_up, iclr_base, i_up, g_up, kns, kis, e_hd, e_dh]
    lw, k2, kk, a, g = _rowwise(_fn_prep, prep_rows, prep_params, [(D, F32)] * 5, name="rwkv_prep", tm=256)
    y, states, invs = _wkv_fwd(z_rkv, lw, k2, kk, a, seq)
    post_rows = [_win(y), _win(z_rkv, 0, D), _win(k2), _win(z_rkv, 2 * D, D), _win(g)]
    post_params = [lnx_w, lnx_b, bonus, e_hd, e_dh]
    (y_a,) = _rowwise(_fn_post, post_rows, post_params, [(D, BF16)], name="rwkv_post", tm=256)
    y_b = _attn_fwd(proj, bias, seq)
    wt = {**wt, **comm.late_weights(y_b)}
    ya_p = _mm(y_a, wt["w_branch_a"], name="mm_a")
    yb_p = _mm(y_b, wt["w_branch_b"], name="mm_b")
    mix_rows = [_win(proj, C_GA, D), _win(proj, C_GA + D, D), _win(ya_p), _win(yb_p)]
    (mixed,) = _rowwise(_fn_mix, mix_rows, [], [(D, BF16)], name="gate_mix", tm=512)
    mo = _mm(mixed, wt["w_out"], name="mm_out")
    x1, h2 = _rowwise(_fn_res_pre, [_win(x), _win(mo)], [g_post_mix, g_pre_cross], [(D, F32), (D, BF16)],
                      name="res_mix", tm=512)
    qm = _mm(h2, wt["w_q_mem"], name="mm_q")
    kvm = _mm(mn, wt["w_kv_mem"], name="mm_kv")
    om = _xattn_fwd(qm, kvm, seq, n_mem)
    co = _mm(om, wt["w_o_mem"], name="mm_o")
    x2, h3 = _rowwise(_fn_res_pre, [_win(x1), _win(co)], [g_post_cross, g_pre_ffn], [(D, F32), (D, BF16)],
                      name="res_cross", tm=512)
    gu = _mm(h3, wt["w_ffn_in"], name="mm_ffn_in")
    (act,) = _rowwise(_fn_swiglu, [_win(gu, 0, FFN), _win(gu, FFN, FFN)], [], [(FFN, BF16)], name="swiglu", tm=256)
    ff = _mm(act, wt["w_ffn_out"], name="mm_ffn_out")

    gw = {}
    loss, dx2, dff, gw["g_post_ffn"] = _loss_head(x2, ff, g_post_ffn, target)
    dact = _mm(dff, wt["w_ffn_out"], tb=True, name="mm_ffn_out_dx", out_dtype=BF16)
    gw["w_ffn_out"] = _mm(act, dff, ta=True, name="mm_ffn_out_dw", out_dtype=BF16)
    (dgu,), _ = _rowwise_bwd(_fn_swiglu, [_win(gu, 0, FFN), _win(gu, FFN, FFN)], [], 0, [[dact]],
                             name="swiglu_bwd", tm=256, row_grad=[BF16, BF16], packed=True)
    dh3 = _mm(dgu, wt["w_ffn_in"], tb=True, name="mm_ffn_in_dx", out_dtype=BF16)
    gw["w_ffn_in"] = _mm(h3, dgu, ta=True, name="mm_ffn_in_dw", out_dtype=BF16)
    (dx1, dco), (gw["g_post_cross"], gw["g_pre_ffn"]) = _rowwise_bwd(
        _fn_res_pre, [_win(x1), _win(co)], [g_post_cross, g_pre_ffn], 0, [[dx2], [dh3]],
        name="res_cross_bwd", tm=256, row_grad=[F32, BF16])
    dom = _mm(dco, wt["w_o_mem"], tb=True, name="mm_o_dx", out_dtype=BF16)
    gw["w_o_mem"] = _mm(om, dco, ta=True, name="mm_o_dw", out_dtype=BF16)
    dqm, dkvm = _xattn_bwd(qm, kvm, dom, seq, n_mem)
    dh2 = _mm(dqm, wt["w_q_mem"], tb=True, name="mm_q_dx", out_dtype=BF16)
    gw["w_q_mem"] = _mm(h2, dqm, ta=True, name="mm_q_dw", out_dtype=BF16)
    dmn = _mm(dkvm, wt["w_kv_mem"], tb=True, name="mm_kv_dx", out_dtype=BF16)
    gw["w_kv_mem"] = _mm(mn, dkvm, ta=True, name="mm_kv_dw", out_dtype=BF16)
    _, (gw["g_mem"],) = _rowwise_bwd(_fn_pre, [_win(mem)], [g_mem], 0, [[dmn]], name="pre_mem_bwd", tm=256,
                                     row_grad=[None])
    (dx0, dmo), (gw["g_post_mix"], gw["g_pre_cross"]) = _rowwise_bwd(
        _fn_res_pre, [_win(x), _win(mo)], [g_post_mix, g_pre_cross], 0, [[dx1], [dh2]],
        name="res_mix_bwd", tm=256, row_grad=[F32, BF16])
    dmixed = _mm(dmo, wt["w_out"], tb=True, name="mm_out_dx", out_dtype=BF16)
    gw["w_out"] = _mm(mixed, dmo, ta=True, name="mm_out_dw", out_dtype=BF16)
    (dzga, dzgb, dya_p, dyb_p), _ = _rowwise_bwd(_fn_mix, mix_rows, [], 0, [[dmixed]], name="gate_mix_bwd", tm=256,
                                                 row_grad=[BF16] * 4)
    gw["w_branch_a"] = _mm(y_a, dya_p, ta=True, name="mm_a_dw", out_dtype=BF16)
    gw["w_branch_b"] = _mm(y_b, dyb_p, ta=True, name="mm_b_dw", out_dtype=BF16)
    token = comm.send_early(gw)
    dy_a = _mm(dya_p, wt["w_branch_a"], tb=True, name="mm_a_dx", out_dtype=BF16, after=token)
    dy_b = _mm(dyb_p, wt["w_branch_b"], tb=True, name="mm_b_dx", out_dtype=BF16, after=token)
    dq, dk, dv, dbias = _attn_bwd(proj, bias, dy_b, seq)
    gw["rel_bias"] = _mm(dbias.reshape(N_HEADS, CHUNK * BAND), onehot, tb=True, name="mm_bias_dw", split_a=2)
    (dy, dr_p, dk2_p, dv_p, dg), (gw["lnx_w"], gw["lnx_b"], gw["bonus_scale"]) = _rowwise_bwd(
        _fn_post, post_rows, post_params, 2, [[dy_a]], name="rwkv_post_bwd", tm=128, row_grad=[F32] * 5)
    dr_s, dlw, dk2_s, dv_s, dkk, da = _wkv_bwd(z_rkv, lw, k2, kk, a, states, invs, dy, seq)
    (dzk, dzw, dza, dzg), pg = _rowwise_bwd(
        _fn_prep, prep_rows, prep_params, 2, [[dlw], [dk2_p, dk2_s], [dkk], [da], [dg]],
        name="rwkv_prep_bwd", tm=128, row_grad=[F32] * 4)
    gw["decay_base"], gd_up, gw["iclr_base"], gi_up, gg_up, gw["key_norm_scale"], gw["key_iclr_scale"] = pg
    gw["decay_up"], gw["iclr_up"], gw["gate_up"] = gd_up[:LORA_W], gi_up[:LORA_A], gg_up[:LORA_G]
    dp_r, gmix_r = _shift_bwd(proj, 0, D, mix_rkv[:, :D], [dr_p, dr_s], seq, name="shift_r_bwd")
    dp_k, gmix_k = _shift_bwd(proj, D, D, mix_rkv[:, D:2 * D], [dzk], seq, name="shift_k_bwd")
    dp_v, gmix_v = _shift_bwd(proj, 2 * D, D, mix_rkv[:, 2 * D:], [dv_p, dv_s], seq, name="shift_v_bwd")
    dp_lora, gmix_lora = _shift_bwd(proj, C_LORA, 512, mix_lora, [jnp.concatenate([dzw, dza, dzg], axis=1)], seq,
                                    name="shift_lora_bwd")
    gw["shift_mix"] = jnp.concatenate([gmix_r, gmix_k, gmix_v, _unpad_lora(gmix_lora)], axis=1)
    dproj = [dp_r, dp_k, dp_v, dq, dk, dv, dzga, dzgb, dp_lora]
    gw["w_in_p"] = _mm_cat_tn(h1, dproj, name="mm_in_dw", after=gw["rel_bias"])
    token = comm.send_late(gw)
    dh1 = _mm_cat_nt(dproj, w_in, name="mm_in_dx", after=token)
    (grad_x,), (gw["g_pre_mix"],) = _rowwise_bwd(_fn_pre, [_win(x)], [g_pre_mix], 0, [[dh1]], name="pre_mix_bwd",
                                                 tm=256, row_grad=[F32], add_to={0: dx0})
    return loss, grad_x, gw


_COL_SHARDED = ("w_in", "decay_up", "iclr_up", "gate_up", "w_o_mem", "w_ffn_in")
_ROW_SHARDED = ("w_branch_a", "w_branch_b", "w_out", "w_q_mem", "w_kv_mem", "w_ffn_out")
_FIRST = ("w_in", "decay_up", "iclr_up", "gate_up")
_REST = ("w_o_mem", "w_ffn_in", "w_branch_a", "w_branch_b", "w_out", "w_q_mem", "w_kv_mem", "w_ffn_out")
_REPLICATED = ("g_pre_mix", "g_post_mix", "shift_mix", "decay_base", "iclr_base", "key_norm_scale", "key_iclr_scale",
               "bonus_scale", "lnx_w", "lnx_b", "rel_bias", "g_pre_cross", "g_post_cross", "g_mem", "g_pre_ffn",
               "g_post_ffn")
_WEIGHTS = ("g_pre_mix", "g_post_mix", "w_in", "shift_mix", "decay_base", "decay_up", "iclr_base", "iclr_up", "gate_up",
            "key_norm_scale", "key_iclr_scale", "bonus_scale", "lnx_w", "lnx_b", "rel_bias", "w_branch_a", "w_branch_b",
            "w_out", "g_pre_cross", "g_post_cross", "g_mem", "w_q_mem", "w_kv_mem", "w_o_mem", "g_pre_ffn", "g_post_ffn",
            "w_ffn_in", "w_ffn_out")
_PACK_ROWS = 8 * ((sum({"shift_mix": 3360, "bonus_scale": 1024, "rel_bias": 3072}.get(n, D) for n in _REPLICATED)
                   + 1 + 8 * LANE - 1) // (8 * LANE))


def _pack(vals):
    flat = jnp.concatenate([v.reshape(-1).astype(F32) for v in vals])
    return jnp.pad(flat, (0, _PACK_ROWS * LANE - flat.shape[0])).reshape(_PACK_ROWS, LANE)


def _unpack(packed, shapes):
    flat, out, pos = packed.reshape(-1), [], 0
    for s in shapes:
        n = math.prod(s)
        out.append(flat[pos:pos + n].reshape(s))
        pos += n
    return out


def _step(args, seq, n_mem):
    names = ("x", "mem") + _WEIGHTS + ("loss_target",) + tuple("m_" + n for n in _WEIGHTS) + tuple("v_" + n for n in _WEIGHTS)
    given = dict(zip(names, args))
    nb = given["x"].shape[0]
    x = given["x"].reshape(nb * seq, D)
    mem = given["mem"].reshape(nb * n_mem, D)
    target = given["loss_target"].reshape(nb * seq, D)
    shard = {n: given[n][0] for n in _COL_SHARDED + _ROW_SHARDED}
    out = {}

    def full(name, g):
        return _cols_to_full(g) if name in _COL_SHARDED else g.reshape(-1, g.shape[-1])

    def blocks_of(name, g):
        return (_full_to_cols(g) if name in _COL_SHARDED else g.reshape((N_DEV,) + shard[name].shape)).astype(BF16)

    def update(names, landed, after=None):
        done = []
        for n, parts in zip(names, landed):
            res = _adamw(parts, shard[n], given["m_" + n][0], given["v_" + n][0], name="adamw_" + n, after=after)
            for kind, r in zip(("grad_", "delta_", "new_m_", "new_v_"), res):
                out[kind + n] = r[None]
            done.append(res[0])
        return done

    class Exchanges:
        def __init__(self):
            srcs = [shard[n].astype(BF16) for n in _FIRST]
            self.first, self.begun = _exchange_start(srcs, False, srcs[0], name="gather_first_start",
                                                     dists=_SIBLING_AND_SAME_CORE)

        def first_weights(self, after):
            got = _exchange_wait(self.first, after, [shard[n] for n in _FIRST], name="gather_first_wait")
            relayed = _relay_to_sibling(got, name="gather_first_relay")
            pos = _mesh_pos()
            for j, d in enumerate(_OTHER_CHIPS):
                slot = _flat(_peer(pos, d | 1))
                got = [lax.dynamic_update_slice_in_dim(g, r[j][None], slot, 0) for g, r in zip(got, relayed)]
            self.rest, self.first_token = _exchange_start(
                [shard[n].astype(BF16) for n in _REST], False, got[0], name="gather_rest_start")
            first = {n: full(n, g) for n, g in zip(_FIRST, got)}
            first["w_in_p"] = _permute_in(first.pop("w_in"))
            return first

        def late_weights(self, after):
            got = _exchange_wait(self.rest, [after], [shard[n] for n in _REST], name="gather_rest_wait")
            return {n: full(n, g) for n, g in zip(_REST, got)}

        def send_early(self, gw):
            self.early_blocks = [blocks_of(n, gw[n]) for n in _REST]
            self.early, token = _exchange_start(self.early_blocks, True, self.early_blocks[-1], name="scatter_rest_start")
            return token

        def send_late(self, gw):
            me = _flat(_mesh_pos())
            own = [lax.dynamic_index_in_dim(b, me, 0, keepdims=False) for b in self.early_blocks]
            landed = _exchange_wait(self.early, [gw["w_in_p"]], own, name="scatter_rest_wait")
            grads = {**gw, "w_in": _unpermute_in(gw["w_in_p"])}
            self.late_blocks = [blocks_of(n, grads[n]) for n in _FIRST]
            self.late, token = _exchange_start(self.late_blocks, True, landed[0], name="scatter_first_start")
            self.updated = update(_REST, landed, after=token)
            return token

        def finish(self, after):
            me = _flat(_mesh_pos())
            own = [lax.dynamic_index_in_dim(b, me, 0, keepdims=False) for b in self.late_blocks]
            update(_FIRST, _exchange_wait(self.late, [*after, *self.updated], own, name="scatter_first_wait"))

    comm = Exchanges()
    wt = {n: given[n][0] for n in _REPLICATED}
    loss_tile, grad_x, gw = _local_step(x, mem, target, wt, seq, n_mem, comm)
    rep_shapes = [given[n].shape for n in _REPLICATED]
    packed, _ = lax.optimization_barrier((_pack([gw[n] for n in _REPLICATED] + [loss_tile[0, 0]]), tuple(comm.updated)))
    small = _exchange([packed], False, name="gather_small")[0]
    zero = jnp.zeros((), F32)
    res = _adamw(small, *[_pack([given[p + n] for n in _REPLICATED] + [zero]) for p in ("", "m_", "v_")],
                 name="adamw_small", tr=_PACK_ROWS)
    for kind, r in zip(("grad_", "delta_", "new_m_", "new_v_"), res):
        for n, val in zip(_REPLICATED, _unpack(r, rep_shapes)):
            out[kind + n] = val
    loss = res[0].reshape(-1)[sum(math.prod(s) for s in rep_shapes)]
    comm.finish([grad_x, res[0]])
    grad_x = grad_x.reshape(nb, seq, D)
    return (loss, grad_x, *[out[k + n] for k in ("grad_", "delta_", "new_m_", "new_v_") for n in _WEIGHTS])


def kernel(x, mem, g_pre_mix, g_post_mix, w_in, shift_mix, decay_base, decay_up, iclr_base, iclr_up, gate_up, key_norm_scale, key_iclr_scale, bonus_scale, lnx_w, lnx_b, rel_bias, w_branch_a, w_branch_b, w_out, g_pre_cross, g_post_cross, g_mem, w_q_mem, w_kv_mem, w_o_mem, g_pre_ffn, g_post_ffn, w_ffn_in, w_ffn_out, loss_target, m_g_pre_mix, m_g_post_mix, m_w_in, m_shift_mix, m_decay_base, m_decay_up, m_iclr_base, m_iclr_up, m_gate_up, m_key_norm_scale, m_key_iclr_scale, m_bonus_scale, m_lnx_w, m_lnx_b, m_rel_bias, m_w_branch_a, m_w_branch_b, m_w_out, m_g_pre_cross, m_g_post_cross, m_g_mem, m_w_q_mem, m_w_kv_mem, m_w_o_mem, m_g_pre_ffn, m_g_post_ffn, m_w_ffn_in, m_w_ffn_out, v_g_pre_mix, v_g_post_mix, v_w_in, v_shift_mix, v_decay_base, v_decay_up, v_iclr_base, v_iclr_up, v_gate_up, v_key_norm_scale, v_key_iclr_scale, v_bonus_scale, v_lnx_w, v_lnx_b, v_rel_bias, v_w_branch_a, v_w_branch_b, v_w_out, v_g_pre_cross, v_g_post_cross, v_g_mem, v_w_q_mem, v_w_kv_mem, v_w_o_mem, v_g_pre_ffn, v_g_post_ffn, v_w_ffn_in, v_w_ffn_out):
    args = (x, mem, g_pre_mix, g_post_mix, w_in, shift_mix, decay_base, decay_up, iclr_base, iclr_up, gate_up, key_norm_scale, key_iclr_scale, bonus_scale, lnx_w, lnx_b, rel_bias, w_branch_a, w_branch_b, w_out, g_pre_cross, g_post_cross, g_mem, w_q_mem, w_kv_mem, w_o_mem, g_pre_ffn, g_post_ffn, w_ffn_in, w_ffn_out, loss_target, m_g_pre_mix, m_g_post_mix, m_w_in, m_shift_mix, m_decay_base, m_decay_up, m_iclr_base, m_iclr_up, m_gate_up, m_key_norm_scale, m_key_iclr_scale, m_bonus_scale, m_lnx_w, m_lnx_b, m_rel_bias, m_w_branch_a, m_w_branch_b, m_w_out, m_g_pre_cross, m_g_post_cross, m_g_mem, m_w_q_mem, m_w_kv_mem, m_w_o_mem, m_g_pre_ffn, m_g_post_ffn, m_w_ffn_in, m_w_ffn_out, v_g_pre_mix, v_g_post_mix, v_w_in, v_shift_mix, v_decay_base, v_decay_up, v_iclr_base, v_iclr_up, v_gate_up, v_key_norm_scale, v_key_iclr_scale, v_bonus_scale, v_lnx_w, v_lnx_b, v_rel_bias, v_w_branch_a, v_w_branch_b, v_w_out, v_g_pre_cross, v_g_post_cross, v_g_mem, v_w_q_mem, v_w_kv_mem, v_w_o_mem, v_g_pre_ffn, v_g_post_ffn, v_w_ffn_in, v_w_ffn_out)
    return _step(args, x.shape[1], mem.shape[1])
```

```python
import functools
import math

import jax
import jax.numpy as jnp
from jax import lax
from jax.experimental import pallas as pl
from jax.experimental.pallas import tpu as pltpu

F32 = jnp.float32
BF16 = jnp.bfloat16

N_DEV = 8
D = 1024
HEAD = 64
N_HEADS = D // HEAD
LANE = 128
N_PAIRS = D // LANE
CHUNK = 64
LEFT = 8 * CHUNK
BAND = LEFT + CHUNK
REL_CLIP = 128
REL_TABLE = CHUNK + REL_CLIP
MEM_WIDTH = D // 2
MEM_HEADS = 4
FFN = 2816
LORA_W, LORA_A, LORA_G = 64, 64, 160
P_WIDTH = 3 * D + 3 * D + 2 * D + 128 + 128 + 256
C_Q, C_GA, C_LORA = 3 * D, 6 * D, 8 * D
NORM_EPS = 1e-6
GROUP_NORM_EPS = 64e-5
MASK_VALUE = -1e30
ADAM_LR, ADAM_B1, ADAM_B2, ADAM_EPS, ADAM_WD, ADAM_STEP = 0.001, 0.9, 0.999, 1e-08, 0.01, 10
VMEM_LIMIT = 56 * 1024 * 1024


def _cp(*sem):
    return pltpu.CompilerParams(dimension_semantics=sem, vmem_limit_bytes=VMEM_LIMIT)


_NN, _NT, _TN = ((1,), (0,)), ((1,), (1,)), ((0,), (0,))


def _dot_raw(a, b, dims):
    return lax.dot_general(a.astype(BF16), b.astype(BF16), (dims, ((), ())), preferred_element_type=F32)


@functools.partial(jax.custom_vjp, nondiff_argnums=(2,))
def _dot_dims(a, b, dims):
    return _dot_raw(a, b, dims)


def _dot_dims_fwd(a, b, dims):
    return _dot_raw(a, b, dims), (a, b)


def _dot_dims_bwd(dims, res, g):
    a, b = res
    if dims == _NN:
        da, db = _dot_raw(g, b, _NT), _dot_raw(a, g, _TN)
    elif dims == _NT:
        da, db = _dot_raw(g, b, _NN), _dot_raw(g, a, _TN)
    else:
        da, db = _dot_raw(b, g, _NT), _dot_raw(a, g, _NN)
    return da.astype(a.dtype), db.astype(b.dtype)


_dot_dims.defvjp(_dot_dims_fwd, _dot_dims_bwd)


def _dot(a, b, dims=_NN):
    return _dot_dims(a, b, dims)


def _dot_nt(a, b):
    return _dot_dims(a, b, _NT)


def _dot_tn(a, b):
    return _dot_dims(a, b, _TN)


def _split(x, terms):
    parts, rest = [], x.astype(F32)
    for _ in range(terms):
        p = rest.astype(BF16)
        parts.append(p)
        rest = rest - p.astype(F32)
    return parts


def _dot_split_a(a, b, terms=2):
    out = None
    for p in _split(a, terms):
        t = _dot(p, b)
        out = t if out is None else out + t
    return out


def _dot_split_b(a, b, terms=3):
    out = None
    for p in _split(b, terms):
        t = _dot(a, p)
        out = t if out is None else out + t
    return out


def _dot_hi(a, b, dims=_NN):
    ah, al = _split(a, 2)
    bh, bl = _split(b, 2)
    return _dot(ah, bh, dims) + (_dot(ah, bl, dims) + _dot(al, bh, dims))


MM_VMEM_BUDGET = 30 * 1024 * 1024
MM_HBM_BPS = 3.2e12
MM_MXU_FPS = 8.5e14
MM_STEP_S = 0.35e-6


def _divisors(n, align, cap):
    out = [d for d in range(align, min(n, cap) + 1, align) if n % d == 0]
    return out or [n]


def _mm_tiles(m, n, k, ea, eb, eo, ta):
    best = None
    for tm in _divisors(m, LANE if ta else 8, 2048):
        for tn in _divisors(n, LANE, 2048):
            for tk in _divisors(k, LANE, 2048):
                nk = k // tk
                vmem = 2 * (tm * tk * ea + tk * tn * eb + tm * tn * eo) + (tm * tn * 4 if nk > 1 else 0)
                if vmem > MM_VMEM_BUDGET:
                    continue
                dma = (tm * tk * ea if (nk > 1 or n // tn == 1) else tm * tk * ea * tn / n) + tk * tn * eb + tm * tn * eo / nk
                step = max(2.0 * tm * tn * tk / MM_MXU_FPS, dma / MM_HBM_BPS) + MM_STEP_S
                cost = (m // tm) * (n // tn) * nk * step
                if best is None or cost < best[0]:
                    best = (cost, tm, tn, tk)
    return best[1:]


def _mm(a, b, *, name, ta=False, tb=False, out_dtype=F32, tm=None, tn=None, tk=None, split_a=1, after=None):
    m, k = (a.shape[1], a.shape[0]) if ta else a.shape
    n, kb = (b.shape[0], b.shape[1]) if tb else (b.shape[1], b.shape[0])
    assert k == kb, (a.shape, b.shape, ta, tb)
    if tm is None:
        tm, tn, tk = _mm_tiles(m, n, k, a.dtype.itemsize, b.dtype.itemsize, jnp.dtype(out_dtype).itemsize, ta)
    assert m % tm == 0 and n % tn == 0 and k % tk == 0, (m, n, k, tm, tn, tk)
    nk = k // tk
    dims = ((0 if ta else 1,), (1 if tb else 0,))

    n_after = 0 if after is None else 1

    def body(a_ref, b_ref, *rest):
        o_ref, scratch = rest[n_after], rest[n_after + 1:]
        prod = None
        for p in _split(a_ref[...], split_a) if split_a > 1 else [a_ref[...]]:
            t = _dot_raw(p, b_ref[...], dims)
            prod = t if prod is None else prod + t
        if nk == 1:
            o_ref[...] = prod.astype(o_ref.dtype)
            return
        acc_ref, kk = scratch[0], pl.program_id(2)

        @pl.when(kk == 0)
        def _():
            acc_ref[...] = prod

        @pl.when(kk > 0)
        def _():
            acc_ref[...] += prod

        @pl.when(kk == nk - 1)
        def _():
            o_ref[...] = acc_ref[...].astype(o_ref.dtype)

    a_spec = pl.BlockSpec((tk, tm), lambda i, j, q: (q, i)) if ta else pl.BlockSpec((tm, tk), lambda i, j, q: (i, q))
    b_spec = pl.BlockSpec((tn, tk), lambda i, j, q: (j, q)) if tb else pl.BlockSpec((tk, tn), lambda i, j, q: (q, j))
    return pl.pallas_call(
        body, name=name, grid=(m // tm, n // tn, nk),
        in_specs=[a_spec, b_spec] + [pl.BlockSpec(memory_space=pl.ANY)] * n_after,
        out_specs=pl.BlockSpec((tm, tn), lambda i, j, q: (i, j)),
        out_shape=jax.ShapeDtypeStruct((m, n), out_dtype),
        scratch_shapes=[pltpu.VMEM((tm, tn), F32)] if nk > 1 else [],
        compiler_params=_cp("parallel", "parallel", "arbitrary"),
    )(a, b, *([] if after is None else [after]))


def _piece_steps(pieces, tile):
    counts = [p.shape[1] // tile for p in pieces]
    assert all(p.shape[1] % tile == 0 for p in pieces)
    return [(sum(counts[:i]), c) for i, c in enumerate(counts)], sum(counts)


def _mm_cat_nn(pieces, w, *, name, after=None, tm=1024, tk=512):
    t, n = pieces[0].shape[0], w.shape[1]
    tm = min(tm, t)
    spans, nk = _piece_steps(pieces, tk)
    npc = len(pieces)
    n_after = 0 if after is None else 1

    def body(*refs):
        w_ref, o_ref, acc_ref = refs[npc], refs[npc + 1 + n_after], refs[npc + 2 + n_after]
        q = pl.program_id(1)

        @pl.when(q == 0)
        def _():
            acc_ref[...] = jnp.zeros_like(acc_ref)

        for p_ref, (first, count) in zip(refs[:npc], spans):
            @pl.when(jnp.logical_and(q >= first, q < first + count))
            def _(p_ref=p_ref):
                acc_ref[...] += _dot_raw(p_ref[...], w_ref[...], _NN)

        @pl.when(q == nk - 1)
        def _():
            o_ref[...] = acc_ref[...].astype(o_ref.dtype)

    def piece_spec(first, count):
        return pl.BlockSpec((tm, tk), lambda i, q: (i, jnp.clip(q - first, 0, count - 1)))

    return pl.pallas_call(
        body, name=name, grid=(t // tm, nk),
        in_specs=[piece_spec(*s) for s in spans] + [pl.BlockSpec((tk, n), lambda i, q: (q, 0))]
        + [pl.BlockSpec(memory_space=pl.ANY)] * n_after,
        out_specs=pl.BlockSpec((tm, n), lambda i, q: (i, 0)),
        out_shape=jax.ShapeDtypeStruct((t, n), BF16),
        scratch_shapes=[pltpu.VMEM((tm, n), F32)],
        compiler_params=_cp("parallel", "arbitrary"),
    )(*pieces, w, *([] if after is None else [after]))


def _mm_cat_tn(pieces, a, *, name, after=None, tk=1024, tn=512):
    t, m = a.shape
    tk = min(tk, t)
    spans, nj = _piece_steps(pieces, tn)
    npc, nk = len(pieces), t // tk
    n_after = 0 if after is None else 1

    def body(a_ref, *refs):
        o_ref, acc_ref = refs[npc + n_after], refs[npc + 1 + n_after]
        j, q = pl.program_id(0), pl.program_id(1)

        @pl.when(q == 0)
        def _():
            acc_ref[...] = jnp.zeros_like(acc_ref)

        for p_ref, (first, count) in zip(refs[:npc], spans):
            @pl.when(jnp.logical_and(j >= first, j < first + count))
            def _(p_ref=p_ref):
                acc_ref[...] += _dot_raw(p_ref[...], a_ref[...], _TN)

        @pl.when(q == nk - 1)
        def _():
            o_ref[...] = acc_ref[...].astype(o_ref.dtype)

    def piece_spec(first, count):
        def index(j, q):
            mine = jnp.logical_and(j >= first, j < first + count)
            return jnp.where(mine, q, 0), jnp.clip(j - first, 0, count - 1)
        return pl.BlockSpec((tk, tn), index)

    return pl.pallas_call(
        body, name=name, grid=(nj, nk),
        in_specs=[pl.BlockSpec((tk, m), lambda j, q: (q, 0))] + [piece_spec(*s) for s in spans]
        + [pl.BlockSpec(memory_space=pl.ANY)] * n_after,
        out_specs=pl.BlockSpec((tn, m), lambda j, q: (j, 0)),
        out_shape=jax.ShapeDtypeStruct((nj * tn, m), BF16),
        scratch_shapes=[pltpu.VMEM((tn, m), F32)],
        compiler_params=_cp("parallel", "arbitrary"),
    )(a, *pieces, *([] if after is None else [after]))


def _win(arr, start=0, width=None):
    width = arr.shape[1] if width is None else width
    assert start % width == 0
    return (arr, start // width, width)


def _row_specs(rows, tm):
    return [pl.BlockSpec((tm, w), functools.partial(lambda i, cb: (i, cb), cb=cb)) for (_, cb, w) in rows]


def _full_spec(p):
    nd = p.ndim
    return pl.BlockSpec(p.shape, lambda i, nd=nd: (0,) * nd)


def _rowwise(fn, rows, params, outs, *, name, tm, after=None):
    t = rows[0][0].shape[0]
    tm = min(tm, t)
    assert t % tm == 0
    nr, npar = len(rows), len(params)
    n_after = 0 if after is None else 1

    def body(*refs):
        vals = [r[...] for r in refs[:nr + npar]]
        res = fn(*vals)
        for o_ref, r in zip(refs[nr + npar + n_after:], res):
            o_ref[...] = r.astype(o_ref.dtype)

    return pl.pallas_call(
        body, name=name, grid=(t // tm,),
        in_specs=_row_specs(rows, tm) + [_full_spec(p) for p in params] + [pl.BlockSpec(memory_space=pl.ANY)] * n_after,
        out_specs=[pl.BlockSpec((tm, w), lambda i: (i, 0)) for (w, _) in outs],
        out_shape=[jax.ShapeDtypeStruct((t, w), dt) for (w, dt) in outs],
        compiler_params=_cp("parallel"),
    )(*[r[0] for r in rows], *params, *([] if after is None else [after]))


def _rowwise_bwd(fn, rows, params, n_const, cots, *, name, tm, row_grad, add_to=None, packed=False):
    t = rows[0][0].shape[0]
    tm = min(tm, t)
    assert t % tm == 0
    nr, npar = len(rows), len(params)
    ndp = npar - n_const
    add_to = add_to or {}
    add_idx = sorted(add_to)
    flat_cots = [c for group in cots for c in group]
    kept = [i for i in range(nr) if row_grad[i] is not None]

    def body(*refs):
        pos = 0
        row_v = [r[...] for r in refs[pos:pos + nr]]; pos += nr
        par_v = [r[...] for r in refs[pos:pos + npar]]; pos += npar
        cot_v = [r[...] for r in refs[pos:pos + len(flat_cots)]]; pos += len(flat_cots)
        add_v = [r[...] for r in refs[pos:pos + len(add_idx)]]; pos += len(add_idx)
        if packed:
            offs = [sum(rows[i][2] for i in kept[:q]) for q in range(len(kept))]
            rg_refs = [refs[pos].at[:, o:o + rows[i][2]] for o, i in zip(offs, kept)]; pos += 1
        else:
            rg_refs = refs[pos:pos + len(kept)]; pos += len(kept)
        pg_refs = refs[pos:pos + ndp]

        consts = par_v[ndp:]
        res, vjp = jax.vjp(lambda *args: tuple(fn(*args, *consts)), *row_v, *par_v[:ndp])
        cot_in, q = [], 0
        for j, group in enumerate(cots):
            c = None
            for _ in group:
                cv = cot_v[q].astype(F32); q += 1
                c = cv if c is None else c + cv
            c = jnp.zeros(res[j].shape, F32) if c is None else c
            cot_in.append(c.astype(res[j].dtype))
        grads = vjp(tuple(cot_in))
        for ref, i in zip(rg_refs, kept):
            g = grads[i].astype(F32)
            if i in add_to:
                g = g + add_v[add_idx.index(i)].astype(F32)
            ref[...] = g.astype(ref.dtype)

        @pl.when(pl.program_id(0) == 0)
        def _():
            for ref in pg_refs:
                ref[...] = jnp.zeros_like(ref)

        for ref, g in zip(pg_refs, grads[nr:]):
            ref[...] += g.astype(F32)

    cot_specs = [pl.BlockSpec((tm, c.shape[1]), lambda i: (i, 0)) for c in flat_cots]
    add_specs = [pl.BlockSpec((tm, add_to[i].shape[1]), lambda i_: (i_, 0)) for i in add_idx]
    widths = [sum(rows[i][2] for i in kept)] if packed else [rows[i][2] for i in kept]
    n_rg = len(widths)
    out_specs = [pl.BlockSpec((tm, w), lambda i_: (i_, 0)) for w in widths] + [_full_spec(p) for p in params[:ndp]]
    out_shape = [jax.ShapeDtypeStruct((t, w), row_grad[kept[q]]) for q, w in enumerate(widths)] + [
        jax.ShapeDtypeStruct(p.shape, F32) for p in params[:ndp]]
    res = pl.pallas_call(
        body, name=name, grid=(t // tm,),
        in_specs=_row_specs(rows, tm) + [_full_spec(p) for p in params] + cot_specs + add_specs,
        out_specs=out_specs, out_shape=out_shape,
        compiler_params=_cp("arbitrary"),
    )(*[r[0] for r in rows], *params, *flat_cots, *[add_to[i] for i in add_idx])
    return list(res[:n_rg]), list(res[n_rg:])


def _rms(x, g):
    xf = x.astype(F32)
    return xf * lax.rsqrt(jnp.mean(xf * xf, axis=-1, keepdims=True) + NORM_EPS) * g


def _softplus(x):
    return jnp.maximum(x, 0.0) + jnp.log(1.0 + jnp.exp(-jnp.abs(x)))


def _fn_pre(x, g):
    return (_rms(x, g).astype(BF16),)


def _fn_res(x, u, g_post):
    return (x + _rms(u, g_post),)


def _fn_res_pre(x, u, g_post, g_pre):
    xn = x + _rms(u, g_post)
    return xn, _rms(xn, g_pre).astype(BF16)


def _fn_mix(zga, zgb, ya, yb):
    return ((jax.nn.sigmoid(zga) * ya + jax.nn.sigmoid(zgb) * yb).astype(BF16),)


def _fn_swiglu(gate, up):
    gate, up = gate.astype(F32), up.astype(F32)
    return ((gate * jax.nn.sigmoid(gate) * up).astype(BF16),)


def _fn_prep(zk, zw, za, zg, decay_base, d_up, iclr_base, i_up, g_up, kns, kis, e_hd, e_dh):
    w_log = -_softplus(-(decay_base + _dot(jnp.tanh(zw), d_up))) - 0.5
    lw = -jnp.exp(w_log)
    a = jax.nn.sigmoid(iclr_base + _dot(za, i_up))
    g = _dot(jax.nn.sigmoid(zg), g_up)
    kn = zk * kns
    ss = _dot_split_a(kn * kn, e_dh)
    inv = lax.rsqrt(jnp.maximum(ss, 1e-24))
    kk = kn * _dot_split_a(inv, e_hd)
    k2 = zk * (1.0 + (a - 1.0) * kis)
    return lw, k2, kk, a, g


def _fn_post(y, r, k2, v, g, lnx_w, lnx_b, bonus, e_hd, e_dh):
    mu = _dot_split_a(_dot_split_a(y, e_dh) * (1.0 / HEAD), e_hd)
    yc = y - mu
    var = _dot_split_a(yc * yc, e_dh) * (1.0 / HEAD)
    yn = yc * _dot_split_a(lax.rsqrt(var + GROUP_NORM_EPS), e_hd)
    bs = _dot_split_a(_dot_split_a(r * k2 * bonus, e_dh), e_hd)
    return (((yn * lnx_w + lnx_b + bs * v) * g).astype(BF16),)


def _shift_fwd(p, col0, ncols, mix, seq, *, name, cw=256):
    t = p.shape[0]
    assert col0 % cw == 0 and ncols % cw == 0 and t % seq == 0
    cb0 = col0 // cw

    def body(p_ref, m_ref, z_ref):
        pv = p_ref[...]
        row = lax.broadcasted_iota(jnp.int32, pv.shape, 0)
        prev = jnp.where(row == 0, 0.0, pltpu.roll(pv, 1, axis=0))
        z_ref[...] = pv + (prev - pv) * m_ref[...]

    return pl.pallas_call(
        body, name=name, grid=(t // seq, ncols // cw),
        in_specs=[pl.BlockSpec((seq, cw), lambda b, c: (b, c + cb0)), pl.BlockSpec((1, cw), lambda b, c: (0, c))],
        out_specs=pl.BlockSpec((seq, cw), lambda b, c: (b, c)),
        out_shape=jax.ShapeDtypeStruct((t, ncols), F32),
        compiler_params=_cp("parallel", "parallel"),
    )(p, mix)


def _shift_bwd(p, col0, ncols, mix, dz_parts, seq, *, name, cw=256):
    t = p.shape[0]
    cb0 = col0 // cw
    n = len(dz_parts)

    def body(*refs):
        p_ref, m_ref = refs[:2]
        dp_ref, dm_ref = refs[2 + n:]
        dz = refs[2][...].astype(F32)
        for r in refs[3:2 + n]:
            dz = dz + r[...].astype(F32)
        pv = p_ref[...]
        mixv = m_ref[...]
        row = lax.broadcasted_iota(jnp.int32, pv.shape, 0)
        prev = jnp.where(row == 0, 0.0, pltpu.roll(pv, 1, axis=0))
        u = dz * mixv
        nxt = jnp.where(row == seq - 1, 0.0, pltpu.roll(u, seq - 1, axis=0))
        dp_ref[...] = (dz - u + nxt).astype(dp_ref.dtype)

        @pl.when(pl.program_id(1) == 0)
        def _():
            dm_ref[...] = jnp.zeros_like(dm_ref)

        dm_ref[...] += jnp.sum(dz * (prev - pv), axis=0, keepdims=True)

    return pl.pallas_call(
        body, name=name, grid=(ncols // cw, t // seq),
        in_specs=[pl.BlockSpec((seq, cw), lambda c, b: (b, c + cb0)), pl.BlockSpec((1, cw), lambda c, b: (0, c))]
        + [pl.BlockSpec((seq, cw), lambda c, b: (b, c))] * n,
        out_specs=[pl.BlockSpec((seq, cw), lambda c, b: (b, c)), pl.BlockSpec((1, cw), lambda c, b: (0, c))],
        out_shape=[jax.ShapeDtypeStruct((t, ncols), BF16), jax.ShapeDtypeStruct((1, ncols), F32)],
        compiler_params=_cp("parallel", "arbitrary"),
    )(p, mix, *dz_parts)


def _each(f, *lists):
    return [f(*xs) for xs in zip(*lists)]


def _tri_inv(low):
    c = low[0].shape[0]
    ti = lax.broadcasted_iota(jnp.int32, (c, c), 0)
    si = lax.broadcasted_iota(jnp.int32, (c, c), 1)
    eye = (ti == si).astype(F32)
    inside = (ti // 4) == (si // 4)
    base = [jnp.where(inside, m, 0.0) for m in low]
    acc = _each(lambda m: _dot(eye - m, eye + _dot(m, m)), base)
    size = 8
    while size <= c:
        wider = (ti // size) == (si // size)
        keep = jnp.logical_and(wider, jnp.logical_not(inside))
        acc = _each(lambda p, m: p - _dot(_dot(p, jnp.where(keep, m, 0.0)), p), acc, low)
        inside, size = wider, size * 2
    return acc


@jax.custom_vjp
def _tri_inv_known(low, inv):
    return inv


def _tri_inv_known_fwd(low, inv):
    return inv, inv


def _tri_inv_known_bwd(inv, g):
    dlow = _each(lambda t, gg: -_dot(_dot(t, gg, _TN), t, _NT), inv, g)
    return dlow, _each(jnp.zeros_like, inv)


_tri_inv_known.defvjp(_tri_inv_known_fwd, _tri_inv_known_bwd)


def _wkv_chunk(s0, r, lw, k, v, kk, a, inv=None):
    c = r[0].shape[0]
    ti = lax.broadcasted_iota(jnp.int32, (c, c), 0)
    si = lax.broadcasted_iota(jnp.int32, (c, c), 1)
    incl, strict = ti >= si, ti > si
    tri = incl.astype(F32)
    cum = _each(lambda x: _dot_split_b(tri, x, 3), lw)
    eg = _each(jnp.exp, cum)
    egp = _each(lambda cs, x: jnp.exp(cs - x), cum, lw)
    ei = _each(lambda cs: jnp.exp(-cs), cum)
    rh, kkh, kt = _each(jnp.multiply, r, eg), _each(jnp.multiply, kk, egp), _each(jnp.multiply, k, ei)
    bt = _each(lambda p, q, e: (p * q) * e, a, kk, ei)
    lb = _each(lambda p, q: jnp.where(strict, _dot_nt(p, q), 0.0), kkh, bt)
    lk = _each(lambda p, q: jnp.where(strict, _dot_nt(p, q), 0.0), kkh, kt)
    mb = _each(lambda p, q: jnp.where(incl, _dot_nt(p, q), 0.0), rh, bt)
    mk = _each(lambda p, q: jnp.where(incl, _dot_nt(p, q), 0.0), rh, kt)
    rhs = _each(lambda p, s, m, x: _dot_nt(p, s) + _dot(m, x), kkh, s0, lk, v)
    inv = _tri_inv(lb) if inv is None else _tri_inv_known(lb, inv)
    u = _each(lambda t, x: -_dot(t, x), inv, rhs)
    y = _each(lambda p, s, m1, uu, m2, x: _dot_nt(p, s) + _dot(m1, uu) + _dot(m2, x), rh, s0, mb, u, mk, v)
    s1 = _each(lambda s, uu, b, x, kq, w: (s + _dot_tn(uu, b) + _dot_tn(x, kq)) * jnp.exp(jnp.sum(w, axis=0, keepdims=True)),
               s0, u, bt, v, kt, lw)
    return y, s1, inv


WKV_HEADS = 16
WKV_COLS = WKV_HEADS * HEAD
WKV_GROUPS = N_HEADS // WKV_HEADS


def _head_cols(ref):
    return [ref[:, h * HEAD:(h + 1) * HEAD] for h in range(ref.shape[1] // HEAD)]


def _wkv_specs(seq, rev):
    nc = seq // CHUNK

    def rows(col0):
        cb0 = col0 // WKV_COLS
        if rev:
            return pl.BlockSpec((CHUNK, WKV_COLS), lambda b, h, c: (b * nc + nc - 1 - c, cb0 + h))
        return pl.BlockSpec((CHUNK, WKV_COLS), lambda b, h, c: (b * nc + c, cb0 + h))

    if rev:
        st = pl.BlockSpec((1, 1, WKV_HEADS, HEAD, HEAD), lambda b, h, c: (b * WKV_GROUPS + h, nc - 1 - c, 0, 0, 0))
    else:
        st = pl.BlockSpec((1, 1, WKV_HEADS, HEAD, HEAD), lambda b, h, c: (b * WKV_GROUPS + h, c, 0, 0, 0))
    return rows, st


def _wkv_fwd(z_rkv, lw, k2, kk, a, seq):
    t = z_rkv.shape[0]
    nb, nc = t // seq, seq // CHUNK
    rows, st = _wkv_specs(seq, False)

    def body(r_ref, v_ref, lw_ref, k_ref, kk_ref, a_ref, y_ref, st_ref, inv_ref, s_scr):
        @pl.when(pl.program_id(2) == 0)
        def _():
            s_scr[...] = jnp.zeros_like(s_scr)

        s0 = [s_scr[h] for h in range(WKV_HEADS)]
        y, s1, inv = _wkv_chunk(s0, *[_head_cols(ref) for ref in (r_ref, lw_ref, k_ref, v_ref, kk_ref, a_ref)])
        for h in range(WKV_HEADS):
            st_ref[0, 0, h] = s0[h]
            inv_ref[0, 0, h] = inv[h]
            y_ref[:, h * HEAD:(h + 1) * HEAD] = y[h]
            s_scr[h] = s1[h]

    per_chunk = jax.ShapeDtypeStruct((nb * WKV_GROUPS, nc, WKV_HEADS, HEAD, HEAD), F32)
    return pl.pallas_call(
        body, name="wkv_fwd", grid=(nb, WKV_GROUPS, nc),
        in_specs=[rows(0), rows(2 * D), rows(0), rows(0), rows(0), rows(0)],
        out_specs=[rows(0), st, st],
        out_shape=[jax.ShapeDtypeStruct((t, D), F32), per_chunk, per_chunk],
        scratch_shapes=[pltpu.VMEM((WKV_HEADS, HEAD, HEAD), F32)],
        compiler_params=_cp("parallel", "parallel", "arbitrary"),
    )(z_rkv, z_rkv, lw, k2, kk, a)


def _wkv_bwd(z_rkv, lw, k2, kk, a, states, invs, dy, seq):
    t = z_rkv.shape[0]
    nb, nc = t // seq, seq // CHUNK
    rows, st = _wkv_specs(seq, True)

    def body(r_ref, v_ref, lw_ref, k_ref, kk_ref, a_ref, st_ref, inv_ref, dy_ref,
             dr_ref, dlw_ref, dk_ref, dv_ref, dkk_ref, da_ref, ds_scr):
        @pl.when(pl.program_id(2) == 0)
        def _():
            ds_scr[...] = jnp.zeros_like(ds_scr)

        s0 = [st_ref[0, 0, h] for h in range(WKV_HEADS)]
        inv = [inv_ref[0, 0, h] for h in range(WKV_HEADS)]
        _, vjp = jax.vjp(lambda *args: _wkv_chunk(*args, inv=inv)[:2],
                         s0, *[_head_cols(ref) for ref in (r_ref, lw_ref, k_ref, v_ref, kk_ref, a_ref)])
        grads = vjp(([x.astype(F32) for x in _head_cols(dy_ref)], [ds_scr[h] for h in range(WKV_HEADS)]))
        for h in range(WKV_HEADS):
            ds_scr[h] = grads[0][h]
            for ref, g in zip((dr_ref, dlw_ref, dk_ref, dv_ref, dkk_ref, da_ref), grads[1:]):
                ref[:, h * HEAD:(h + 1) * HEAD] = g[h]

    return pl.pallas_call(
        body, name="wkv_bwd", grid=(nb, WKV_GROUPS, nc),
        in_specs=[rows(0), rows(2 * D), rows(0), rows(0), rows(0), rows(0), st, st, rows(0)],
        out_specs=[rows(0)] * 6,
        out_shape=[jax.ShapeDtypeStruct((t, D), F32)] * 6,
        scratch_shapes=[pltpu.VMEM((WKV_HEADS, HEAD, HEAD), F32)],
        compiler_params=_cp("parallel", "parallel", "arbitrary"),
    )(z_rkv, z_rkv, lw, k2, kk, a, states, invs, dy)


def _softmax(s):
    e = jnp.exp(s - jnp.max(s, axis=-1, keepdims=True))
    return e / jnp.sum(e, axis=-1, keepdims=True)


ATT_HEADS = 8
ATT_COLS = ATT_HEADS * HEAD
ATT_GROUPS = N_HEADS // ATT_HEADS


def _attn_chunk(q, kb, vb, bias, valid):
    s = _each(lambda x, y, z: jnp.where(valid, _dot_nt(x, y) * (HEAD ** -0.5) + z, MASK_VALUE), q, kb, bias)
    return _each(_dot, _each(_softmax, s), vb)


def _pad_fill(pad_ref, src_ref):
    pad_ref[0:LEFT, :] = jnp.zeros((LEFT, pad_ref.shape[1]), pad_ref.dtype)
    pad_ref[LEFT:, :] = src_ref[...].astype(pad_ref.dtype)


def _band_heads(pad_ref, start):
    return [pad_ref[pl.ds(start, BAND), h * HEAD:(h + 1) * HEAD].astype(F32) for h in range(ATT_HEADS)]


def _band_valid(c):
    return (c * CHUNK - LEFT + lax.broadcasted_iota(jnp.int32, (1, BAND), 1)) >= 0


def _attn_fwd(proj, bias, seq):
    t = proj.shape[0]
    nb, nc = t // seq, seq // CHUNK
    cq = C_Q // ATT_COLS

    def body(q_ref, k_ref, v_ref, b_ref, o_ref, kpad, vpad):
        c = pl.program_id(2)

        @pl.when(c == 0)
        def _():
            _pad_fill(kpad, k_ref)
            _pad_fill(vpad, v_ref)

        start = pl.multiple_of(c * CHUNK, CHUNK)
        o = _attn_chunk(_head_cols(q_ref), _band_heads(kpad, start), _band_heads(vpad, start),
                        [b_ref[h] for h in range(ATT_HEADS)], _band_valid(c))
        for h in range(ATT_HEADS):
            o_ref[:, h * HEAD:(h + 1) * HEAD] = o[h].astype(o_ref.dtype)

    return pl.pallas_call(
        body, name="attn_fwd", grid=(ATT_GROUPS, nb, nc),
        in_specs=[pl.BlockSpec((CHUNK, ATT_COLS), lambda h, b, c: (b * nc + c, cq + h)),
                  pl.BlockSpec((seq, ATT_COLS), lambda h, b, c: (b, cq + ATT_GROUPS + h)),
                  pl.BlockSpec((seq, ATT_COLS), lambda h, b, c: (b, cq + 2 * ATT_GROUPS + h)),
                  pl.BlockSpec((ATT_HEADS, CHUNK, BAND), lambda h, b, c: (h, 0, 0))],
        out_specs=pl.BlockSpec((CHUNK, ATT_COLS), lambda h, b, c: (b * nc + c, h)),
        out_shape=jax.ShapeDtypeStruct((t, D), BF16),
        scratch_shapes=[pltpu.VMEM((seq + LEFT, ATT_COLS), BF16)] * 2,
        compiler_params=_cp("parallel", "arbitrary", "arbitrary"),
    )(proj, proj, proj, bias)


def _attn_bwd(proj, bias, do, seq):
    t = proj.shape[0]
    nb, nc = t // seq, seq // CHUNK
    cq = C_Q // ATT_COLS

    def body(q_ref, k_ref, v_ref, b_ref, do_ref, dq_ref, dk_ref, dv_ref, db_ref, kpad, vpad, dkpad, dvpad):
        b, c = pl.program_id(1), pl.program_id(2)

        @pl.when(c == 0)
        def _():
            _pad_fill(kpad, k_ref)
            _pad_fill(vpad, v_ref)
            dkpad[...] = jnp.zeros_like(dkpad)
            dvpad[...] = jnp.zeros_like(dvpad)

        @pl.when(jnp.logical_and(b == 0, c == 0))
        def _():
            db_ref[...] = jnp.zeros_like(db_ref)

        start = pl.multiple_of(c * CHUNK, CHUNK)
        _, vjp = jax.vjp(functools.partial(_attn_chunk, valid=_band_valid(c)),
                         _head_cols(q_ref), _band_heads(kpad, start), _band_heads(vpad, start),
                         [b_ref[h] for h in range(ATT_HEADS)])
        dq, dkb, dvb, dbias = vjp([x.astype(F32) for x in _head_cols(do_ref)])
        for h in range(ATT_HEADS):
            sl = slice(h * HEAD, (h + 1) * HEAD)
            dq_ref[:, sl] = dq[h].astype(dq_ref.dtype)
            dkpad[pl.ds(start, BAND), sl] += dkb[h].astype(F32)
            dvpad[pl.ds(start, BAND), sl] += dvb[h].astype(F32)
            db_ref[h] += dbias[h]

        @pl.when(c == nc - 1)
        def _():
            dk_ref[...] = dkpad[LEFT:, :].astype(dk_ref.dtype)
            dv_ref[...] = dvpad[LEFT:, :].astype(dv_ref.dtype)

    kv_out = pl.BlockSpec((seq, ATT_COLS), lambda h, b, c: (b, h))
    return pl.pallas_call(
        body, name="attn_bwd", grid=(ATT_GROUPS, nb, nc),
        in_specs=[pl.BlockSpec((CHUNK, ATT_COLS), lambda h, b, c: (b * nc + c, cq + h)),
                  pl.BlockSpec((seq, ATT_COLS), lambda h, b, c: (b, cq + ATT_GROUPS + h)),
                  pl.BlockSpec((seq, ATT_COLS), lambda h, b, c: (b, cq + 2 * ATT_GROUPS + h)),
                  pl.BlockSpec((ATT_HEADS, CHUNK, BAND), lambda h, b, c: (h, 0, 0)),
                  pl.BlockSpec((CHUNK, ATT_COLS), lambda h, b, c: (b * nc + c, h))],
        out_specs=[pl.BlockSpec((CHUNK, ATT_COLS), lambda h, b, c: (b * nc + c, h)), kv_out, kv_out,
                   pl.BlockSpec((ATT_HEADS, CHUNK, BAND), lambda h, b, c: (h, 0, 0))],
        out_shape=[jax.ShapeDtypeStruct((t, D), BF16)] * 3 + [jax.ShapeDtypeStruct((N_HEADS, CHUNK, BAND), F32)],
        scratch_shapes=[pltpu.VMEM((seq + LEFT, ATT_COLS), BF16)] * 2 + [pltpu.VMEM((seq + LEFT, ATT_COLS), F32)] * 2,
        compiler_params=_cp("parallel", "arbitrary", "arbitrary"),
    )(proj, proj, proj, bias, do)


def _xattn_tile(q, k, v):
    s = _dot_nt(q, k) * ((MEM_WIDTH // MEM_HEADS) ** -0.5)
    return _dot(_softmax(s), v)


def _xattn_fwd(qm, kvm, seq, n_mem, tq=512):
    t = qm.shape[0]
    tq = min(tq, seq)
    nb, nq = t // seq, seq // tq

    def body(q_ref, k_ref, v_ref, o_ref):
        o_ref[...] = _xattn_tile(q_ref[...], k_ref[...], v_ref[...]).astype(o_ref.dtype)

    return pl.pallas_call(
        body, name="xattn_fwd", grid=(nb, MEM_HEADS, nq),
        in_specs=[pl.BlockSpec((tq, LANE), lambda b, h, i: (b * nq + i, h)),
                  pl.BlockSpec((n_mem, LANE), lambda b, h, i: (b, h)),
                  pl.BlockSpec((n_mem, LANE), lambda b, h, i: (b, MEM_HEADS + h))],
        out_specs=pl.BlockSpec((tq, LANE), lambda b, h, i: (b * nq + i, h)),
        out_shape=jax.ShapeDtypeStruct((t, MEM_WIDTH), BF16),
        compiler_params=_cp("parallel", "parallel", "parallel"),
    )(qm, kvm, kvm)


def _xattn_bwd(qm, kvm, do, seq, n_mem, tq=512):
    t = qm.shape[0]
    tq = min(tq, seq)
    nb, nq = t // seq, seq // tq

    def body(q_ref, k_ref, v_ref, do_ref, dq_ref, dkv_ref, dk_acc, dv_acc):
        i = pl.program_id(2)

        @pl.when(i == 0)
        def _():
            dk_acc[...] = jnp.zeros_like(dk_acc)
            dv_acc[...] = jnp.zeros_like(dv_acc)

        _, vjp = jax.vjp(_xattn_tile, q_ref[...], k_ref[...], v_ref[...])
        dq, dk, dv = vjp(do_ref[...].astype(F32))
        dq_ref[...] = dq.astype(dq_ref.dtype)
        dk_acc[...] += dk
        dv_acc[...] += dv

        @pl.when(i == nq - 1)
        def _():
            dkv_ref[0] = dk_acc[...].astype(dkv_ref.dtype)
            dkv_ref[1] = dv_acc[...].astype(dkv_ref.dtype)

    dq, dkv = pl.pallas_call(
        body, name="xattn_bwd", grid=(nb, MEM_HEADS, nq),
        in_specs=[pl.BlockSpec((tq, LANE), lambda b, h, i: (b * nq + i, h)),
                  pl.BlockSpec((n_mem, LANE), lambda b, h, i: (b, h)),
                  pl.BlockSpec((n_mem, LANE), lambda b, h, i: (b, MEM_HEADS + h)),
                  pl.BlockSpec((tq, LANE), lambda b, h, i: (b * nq + i, h))],
        out_specs=[pl.BlockSpec((tq, LANE), lambda b, h, i: (b * nq + i, h)),
                   pl.BlockSpec((2, n_mem, LANE), lambda b, h, i: (0, b, h))],
        out_shape=[jax.ShapeDtypeStruct((t, MEM_WIDTH), BF16), jax.ShapeDtypeStruct((2, nb * n_mem, MEM_WIDTH), BF16)],
        scratch_shapes=[pltpu.VMEM((n_mem, LANE), F32)] * 2,
        compiler_params=_cp("parallel", "parallel", "arbitrary"),
    )(qm, kvm, kvm, do)
    return dq, jnp.concatenate([dkv[0], dkv[1]], axis=1)


def _loss_head(x, u, g_post, target, tm=256):
    t, d = x.shape
    tm = min(tm, t)

    def tile_loss(xv, uv, gv, tv):
        diff = _fn_res(xv, uv, gv)[0] - tv
        return 0.5 * jnp.sum(jnp.mean(diff * diff, axis=-1, keepdims=True), axis=0, keepdims=True)

    def body(x_ref, u_ref, g_ref, t_ref, l_ref, dx_ref, du_ref, dg_ref):
        @pl.when(pl.program_id(0) == 0)
        def _():
            l_ref[...] = jnp.zeros_like(l_ref)
            dg_ref[...] = jnp.zeros_like(dg_ref)

        tv = t_ref[...]
        part, vjp = jax.vjp(lambda xv, uv, gv: tile_loss(xv, uv, gv, tv), x_ref[...], u_ref[...], g_ref[...])
        dx, du, dg = vjp(jnp.ones((1, 1), F32))
        l_ref[...] += part
        dx_ref[...] = dx
        du_ref[...] = du.astype(du_ref.dtype)
        dg_ref[...] += dg

    rows = pl.BlockSpec((tm, d), lambda i: (i, 0))
    vec = pl.BlockSpec((1, d), lambda i: (0, 0))
    return pl.pallas_call(
        body, name="loss_head", grid=(t // tm,),
        in_specs=[rows, rows, vec, rows],
        out_specs=[pl.BlockSpec((8, LANE), lambda i: (0, 0)), rows, rows, vec],
        out_shape=[jax.ShapeDtypeStruct((8, LANE), F32), jax.ShapeDtypeStruct((t, d), F32),
                   jax.ShapeDtypeStruct((t, d), BF16), jax.ShapeDtypeStruct((1, d), F32)],
        compiler_params=_cp("arbitrary"),
    )(x, u, g_post, target)


def _mesh_pos():
    return lax.axis_index("x"), lax.axis_index("y"), lax.axis_index("c")


def _peer(pos, d):
    x, y, c = pos
    return ((1 - x) if d & 4 else x, (1 - y) if d & 2 else y, (1 - c) if d & 1 else c)


def _flat(pos):
    return 4 * pos[0] + 2 * pos[1] + pos[2]


def _exchange(arrays, scatter, *, name):
    n = len(arrays)
    shapes = [a.shape[1:] if scatter else a.shape for a in arrays]

    def body(*refs):
        ins, outs = refs[:n], refs[n:2 * n]
        send, recv, loc = refs[2 * n:]
        pos = _mesh_pos()
        me = _flat(pos)
        pending = []
        for i in range(n):
            own = pltpu.make_async_copy(ins[i].at[me] if scatter else ins[i], outs[i].at[me], loc.at[i])
            own.start()
            pending.append(own)
            for d in range(1, N_DEV):
                peer = _peer(pos, d)
                src = ins[i].at[_flat(peer)] if scatter else ins[i]
                out_cp = pltpu.make_async_remote_copy(
                    src_ref=src, dst_ref=outs[i].at[me], send_sem=send.at[i, d - 1], recv_sem=recv.at[i, d - 1],
                    device_id=peer, device_id_type=pl.DeviceIdType.MESH)
                out_cp.start()
                pending.append(out_cp)
        for i in range(n):
            own = pending[i * N_DEV]
            for d in range(1, N_DEV):
                peer = _peer(pos, d)
                src = ins[i].at[_flat(peer)] if scatter else ins[i]
                pending[i * N_DEV + d].wait_send()
                pltpu.make_async_remote_copy(
                    src_ref=src, dst_ref=outs[i].at[_flat(peer)], send_sem=send.at[i, d - 1], recv_sem=recv.at[i, d - 1],
                    device_id=peer, device_id_type=pl.DeviceIdType.MESH).wait_recv()
            own.wait()

    hbm = pl.BlockSpec(memory_space=pltpu.HBM)
    return pl.pallas_call(
        body, name=name,
        in_specs=[hbm] * n, out_specs=[hbm] * n,
        out_shape=[jax.ShapeDtypeStruct((N_DEV,) + tuple(s), a.dtype) for s, a in zip(shapes, arrays)],
        scratch_shapes=[pltpu.SemaphoreType.DMA((n, N_DEV - 1)), pltpu.SemaphoreType.DMA((n, N_DEV - 1)),
                        pltpu.SemaphoreType.DMA((n,))],
    )(*arrays)


_HBM = pl.BlockSpec(memory_space=pltpu.HBM)
_SEM = pl.BlockSpec(memory_space=pltpu.SEMAPHORE)
_DATAFLOW = pltpu.SideEffectType.DATAFLOW_SIDE_EFFECTING


_ALL_PEERS = tuple(range(1, N_DEV))
_SIBLING_AND_SAME_CORE = (1, 2, 4, 6)


def _remote_copies(ins, lands, send, recv, scatter, dists):
    pos = _mesh_pos()
    me = _flat(pos)
    out = []
    for i in range(len(ins)):
        for j, d in enumerate(dists):
            peer = _peer(pos, d)
            src = ins[i].at[_flat(peer)] if scatter else ins[i]
            pair = i * len(dists) + j
            sems = dict(send_sem=send.at[pair], recv_sem=recv.at[pair], device_id=peer,
                        device_id_type=pl.DeviceIdType.MESH)
            out.append((pltpu.make_async_remote_copy(src_ref=src, dst_ref=lands[i].at[me], **sems),
                        pltpu.make_async_remote_copy(src_ref=src, dst_ref=lands[i].at[_flat(peer)], **sems)))
    return out


def _exchange_start(arrays, scatter, after, *, name, dists=_ALL_PEERS):
    n = len(arrays)
    shapes = [a.shape[1:] if scatter else a.shape for a in arrays]
    lands = [pltpu.with_memory_space_constraint(lax.empty((N_DEV,) + tuple(s), a.dtype), pltpu.HBM)
             for s, a in zip(shapes, arrays)]
    srcs = [pltpu.with_memory_space_constraint(a, pltpu.HBM) for a in arrays]

    def body(*refs):
        ins, land_refs = refs[:n], refs[n:2 * n]
        send, recv, token = refs[2 * n + 1], refs[2 * n + 2], refs[-1]
        for going, _ in _remote_copies(ins, land_refs, send, recv, scatter, dists):
            going.start()
        token[...] = jnp.zeros_like(token)

    sems = pltpu.SemaphoreType.DMA((n * len(dists),))
    res = pl.pallas_call(
        body, name=name,
        out_shape=(sems, sems, *[pltpu.HBM(a.shape, a.dtype) for a in srcs + lands], jax.ShapeDtypeStruct((8, LANE), F32)),
        in_specs=[_HBM] * (2 * n) + [pl.BlockSpec(memory_space=pl.ANY)],
        out_specs=(_SEM, _SEM, *[_HBM] * (2 * n), pl.BlockSpec(memory_space=pltpu.VMEM)),
        input_output_aliases={i: 2 + i for i in range(2 * n)},
        compiler_params=pltpu.CompilerParams(has_side_effects=_DATAFLOW),
    )(*srcs, *lands, after)
    return (n, scatter, dists, res[0], res[1], list(res[2:2 + 2 * n])), res[-1]


def _exchange_wait(handle, after, own, *, name):
    n, scatter, dists, send, recv, thru = handle

    def body(*refs):
        ins, land_refs = refs[:n], refs[n:2 * n]
        for going, coming in _remote_copies(ins, land_refs, refs[2 * n], refs[2 * n + 1], scatter, dists):
            going.wait_send()
            coming.wait_recv()

    res = pl.pallas_call(
        body, name=name,
        out_shape=tuple(pltpu.HBM(a.shape, a.dtype) for a in thru),
        in_specs=[_HBM] * (2 * n) + [_SEM, _SEM] + [pl.BlockSpec(memory_space=pl.ANY)] * len(after),
        out_specs=tuple([_HBM] * (2 * n)),
        input_output_aliases={i: i for i in range(2 * n)},
        compiler_params=pltpu.CompilerParams(has_side_effects=_DATAFLOW),
    )(*thru, send, recv, *after)
    me = _flat(_mesh_pos())
    return [lax.dynamic_update_slice_in_dim(land, o[None].astype(land.dtype), me, 0) for land, o in zip(res[n:], own)]


_OTHER_CHIPS = (2, 4, 6)


def _relay_to_sibling(gathered, *, name):
    n, k = len(gathered), len(_OTHER_CHIPS)

    def body(*refs):
        ins, outs = refs[:n], refs[n:2 * n]
        send, recv = refs[2 * n:]
        pos = _mesh_pos()
        copies = []
        for i in range(n):
            for j, d in enumerate(_OTHER_CHIPS):
                cp = pltpu.make_async_remote_copy(
                    src_ref=ins[i].at[_flat(_peer(pos, d))], dst_ref=outs[i].at[j],
                    send_sem=send.at[i * k + j], recv_sem=recv.at[i * k + j],
                    device_id=_peer(pos, 1), device_id_type=pl.DeviceIdType.MESH)
                cp.start()
                copies.append(cp)
        for cp in copies:
            cp.wait()

    return pl.pallas_call(
        body, name=name, in_specs=[_HBM] * n, out_specs=[_HBM] * n,
        out_shape=[jax.ShapeDtypeStruct((k,) + g.shape[1:], g.dtype) for g in gathered],
        scratch_shapes=[pltpu.SemaphoreType.DMA((n * k,)), pltpu.SemaphoreType.DMA((n * k,))],
    )(*gathered)


def _adamw(parts, w, m, v, *, name, tr=128, after=None):
    r, c = w.shape
    align = 8 * 4 // parts.dtype.itemsize
    row_tiles = [d for d in range(align, min(tr, r) + 1, align) if r % d == 0]
    tr, tc = (max(row_tiles), c) if row_tiles else (r, LANE)
    assert c % tc == 0
    n_after = 0 if after is None else 1

    def body(p_ref, w_ref, m_ref, v_ref, *rest):
        g_ref, d_ref, nm_ref, nv_ref = rest[n_after:]
        g = p_ref[0].astype(F32)
        for j in range(1, N_DEV):
            g = g + p_ref[j].astype(F32)
        m2 = ADAM_B1 * m_ref[...] + (1.0 - ADAM_B1) * g
        v2 = ADAM_B2 * v_ref[...] + (1.0 - ADAM_B2) * (g * g)
        m_hat = m2 / (1.0 - ADAM_B1 ** ADAM_STEP)
        v_hat = v2 / (1.0 - ADAM_B2 ** ADAM_STEP)
        g_ref[...] = g
        d_ref[...] = -ADAM_LR * (m_hat / (jnp.sqrt(v_hat) + ADAM_EPS) + ADAM_WD * w_ref[...])
        nm_ref[...] = m2
        nv_ref[...] = v2

    spec = pl.BlockSpec((tr, tc), lambda i, j: (i, j))
    return pl.pallas_call(
        body, name=name, grid=(r // tr, c // tc),
        in_specs=[pl.BlockSpec((N_DEV, tr, tc), lambda i, j: (0, i, j)), spec, spec, spec]
        + [pl.BlockSpec(memory_space=pl.ANY)] * n_after,
        out_specs=[spec] * 4, out_shape=[jax.ShapeDtypeStruct((r, c), F32)] * 4,
        compiler_params=_cp("parallel", "parallel"),
    )(parts, w, m, v, *([] if after is None else [after]))


def _cols_to_full(g):
    return jnp.transpose(g, (1, 0, 2)).reshape(g.shape[1], N_DEV * g.shape[2])


def _full_to_cols(w):
    r, c = w.shape
    return jnp.transpose(w.reshape(r, N_DEV, c // N_DEV), (1, 0, 2))


def _cut(a, lo, hi, axis):
    return lax.slice_in_dim(a, lo, hi, axis=axis)


def _pad_to(a, size, axis):
    pads = [(0, 0)] * a.ndim
    pads[axis] = (0, size - a.shape[axis])
    return jnp.pad(a, pads)


def _pad_lora(w, axis=1):
    return jnp.concatenate([
        _pad_to(_cut(w, 0, LORA_W, axis), 128, axis), _pad_to(_cut(w, LORA_W, LORA_W + LORA_A, axis), 128, axis),
        _pad_to(_cut(w, LORA_W + LORA_A, w.shape[axis], axis), 256, axis)], axis=axis)


def _unpad_lora(wp, axis=1):
    return jnp.concatenate([_cut(wp, 0, LORA_W, axis), _cut(wp, 128, 128 + LORA_A, axis),
                            _cut(wp, 256, 256 + LORA_G, axis)], axis=axis)


def _permute_in(w, axis):
    rk = 3 * D
    lo = rk + LORA_W + LORA_A + LORA_G
    return jnp.concatenate([_cut(w, 0, rk, axis), _cut(w, lo, w.shape[axis], axis), _pad_lora(_cut(w, rk, lo, axis), axis)],
                           axis=axis)


def _unpermute_in(wp, axis):
    return jnp.concatenate([_cut(wp, 0, 3 * D, axis), _unpad_lora(_cut(wp, C_LORA, P_WIDTH, axis), axis),
                            _cut(wp, 3 * D, C_LORA, axis)], axis=axis)


def _rel_index():
    dist = jnp.arange(CHUNK)[:, None] - jnp.arange(BAND)[None, :] + LEFT
    return (jnp.minimum(dist, REL_CLIP) + (CHUNK - 1)).reshape(-1)


def _local_step(x, mem, target, wt, seq, n_mem, comm):
    t = x.shape[0]
    row = lambda a: a.reshape(1, -1).astype(F32)
    g_pre_mix, g_post_mix = row(wt["g_pre_mix"]), row(wt["g_post_mix"])
    g_pre_cross, g_post_cross, g_mem = row(wt["g_pre_cross"]), row(wt["g_post_cross"]), row(wt["g_mem"])
    g_pre_ffn, g_post_ffn = row(wt["g_pre_ffn"]), row(wt["g_post_ffn"])
    mix = row(wt["shift_mix"])
    mix_rkv, mix_lora = mix[:, :3 * D], _pad_lora(mix[:, 3 * D:])
    decay_base, iclr_base = row(wt["decay_base"]), row(wt["iclr_base"])
    kns, kis = row(wt["key_norm_scale"]), row(wt["key_iclr_scale"])
    lnx_w, lnx_b, bonus = row(wt["lnx_w"]), row(wt["lnx_b"]), row(wt["bonus_scale"])
    e_dh = (jnp.arange(D)[:, None] // HEAD == jnp.arange(N_HEADS)[None, :]).astype(F32)
    e_hd = e_dh.T
    onehot = (jnp.arange(REL_TABLE)[:, None] == _rel_index()[None, :]).astype(BF16)

    begun = comm.begun
    (h1,) = _rowwise(_fn_pre, [_win(x)], [g_pre_mix], [(D, BF16)], name="pre_mix", tm=512, after=begun)
    (mn,) = _rowwise(_fn_pre, [_win(mem)], [g_mem], [(D, BF16)], name="pre_mem", tm=512, after=begun)
    bias = _mm(wt["rel_bias"].astype(F32), onehot, name="mm_bias", split_a=3, after=begun).reshape(N_HEADS, CHUNK, BAND)
    wt = {**wt, **comm.first_weights([h1, mn, bias])}
    w_in = wt["w_in_p"]
    d_up = jnp.pad(wt["decay_up"].astype(F32), ((0, 128 - LORA_W), (0, 0)))
    i_up = jnp.pad(wt["iclr_up"].astype(F32), ((0, 128 - LORA_A), (0, 0)))
    g_up = jnp.pad(wt["gate_up"].astype(F32), ((0, 256 - LORA_G), (0, 0)))
    proj = _mm(h1, w_in, tb=True, name="mm_in", after=comm.first_token)
    z_rkv = _shift_fwd(proj, 0, 3 * D, mix_rkv, seq, name="shift_rkv")
    z_lora = _shift_fwd(proj, C_LORA, 512, mix_lora, seq, name="shift_lora")
    prep_rows = [_win(z_rkv, D, D), _win(z_lora, 0, 128), _win(z_lora, 128, 128), _win(z_lora, 256, 256)]
    prep_params = [decay_base, d_up, iclr_base, i_up, g_up, kns, kis, e_hd, e_dh]
    lw, k2, kk, a, g = _rowwise(_fn_prep, prep_rows, prep_params, [(D, F32)] * 5, name="rwkv_prep", tm=256)
    y, states, invs = _wkv_fwd(z_rkv, lw, k2, kk, a, seq)
    post_rows = [_win(y), _win(z_rkv, 0, D), _win(k2), _win(z_rkv, 2 * D, D), _win(g)]
    post_params = [lnx_w, lnx_b, bonus, e_hd, e_dh]
    (y_a,) = _rowwise(_fn_post, post_rows, post_params, [(D, BF16)], name="rwkv_post", tm=256)
    y_b = _attn_fwd(proj, bias, seq)
    wt = {**wt, **comm.late_weights(y_b)}
    ya_p = _mm(y_a, wt["w_branch_a"], name="mm_a")
    yb_p = _mm(y_b, wt["w_branch_b"], name="mm_b")
    mix_rows = [_win(proj, C_GA, D), _win(proj, C_GA + D, D), _win(ya_p), _win(yb_p)]
    (mixed,) = _rowwise(_fn_mix, mix_rows, [], [(D, BF16)], name="gate_mix", tm=512)
    mo = _mm(mixed, wt["w_out"], name="mm_out")
    x1, h2 = _rowwise(_fn_res_pre, [_win(x), _win(mo)], [g_post_mix, g_pre_cross], [(D, F32), (D, BF16)],
                      name="res_mix", tm=512)
    qm = _mm(h2, wt["w_q_mem"], name="mm_q")
    kvm = _mm(mn, wt["w_kv_mem"], name="mm_kv")
    om = _xattn_fwd(qm, kvm, seq, n_mem)
    co = _mm(om, wt["w_o_mem"], name="mm_o")
    x2, h3 = _rowwise(_fn_res_pre, [_win(x1), _win(co)], [g_post_cross, g_pre_ffn], [(D, F32), (D, BF16)],
                      name="res_cross", tm=512)
    gu = _mm(h3, wt["w_ffn_in"], tb=True, name="mm_ffn_in", out_dtype=BF16)
    (act,) = _rowwise(_fn_swiglu, [_win(gu, 0, FFN), _win(gu, FFN, FFN)], [], [(FFN, BF16)], name="swiglu", tm=256)
    ff = _mm(act, wt["w_ffn_out"], name="mm_ffn_out")

    gw = {}
    loss, dx2, dff, gw["g_post_ffn"] = _loss_head(x2, ff, g_post_ffn, target)
    dact = _mm(dff, wt["w_ffn_out"], tb=True, name="mm_ffn_out_dx", out_dtype=BF16)
    gw["w_ffn_out"] = _mm(act, dff, ta=True, name="mm_ffn_out_dw", out_dtype=BF16)
    (dgu,), _ = _rowwise_bwd(_fn_swiglu, [_win(gu, 0, FFN), _win(gu, FFN, FFN)], [], 0, [[dact]],
                             name="swiglu_bwd", tm=256, row_grad=[BF16, BF16], packed=True)
    dh3 = _mm(dgu, wt["w_ffn_in"], name="mm_ffn_in_dx", out_dtype=BF16)
    gw["w_ffn_in"] = _mm(dgu, h3, ta=True, name="mm_ffn_in_dw", out_dtype=BF16)
    (dx1, dco), (gw["g_post_cross"], gw["g_pre_ffn"]) = _rowwise_bwd(
        _fn_res_pre, [_win(x1), _win(co)], [g_post_cross, g_pre_ffn], 0, [[dx2], [dh3]],
        name="res_cross_bwd", tm=256, row_grad=[F32, BF16])
    dom = _mm(dco, wt["w_o_mem"], tb=True, name="mm_o_dx", out_dtype=BF16)
    gw["w_o_mem"] = _mm(om, dco, ta=True, name="mm_o_dw", out_dtype=BF16)
    dqm, dkvm = _xattn_bwd(qm, kvm, dom, seq, n_mem)
    dh2 = _mm(dqm, wt["w_q_mem"], tb=True, name="mm_q_dx", out_dtype=BF16)
    gw["w_q_mem"] = _mm(h2, dqm, ta=True, name="mm_q_dw", out_dtype=BF16)
    dmn = _mm(dkvm, wt["w_kv_mem"], tb=True, name="mm_kv_dx", out_dtype=BF16)
    gw["w_kv_mem"] = _mm(mn, dkvm, ta=True, name="mm_kv_dw", out_dtype=BF16)
    _, (gw["g_mem"],) = _rowwise_bwd(_fn_pre, [_win(mem)], [g_mem], 0, [[dmn]], name="pre_mem_bwd", tm=256,
                                     row_grad=[None])
    (dx0, dmo), (gw["g_post_mix"], gw["g_pre_cross"]) = _rowwise_bwd(
        _fn_res_pre, [_win(x), _win(mo)], [g_post_mix, g_pre_cross], 0, [[dx1], [dh2]],
        name="res_mix_bwd", tm=256, row_grad=[F32, BF16])
    dmixed = _mm(dmo, wt["w_out"], tb=True, name="mm_out_dx", out_dtype=BF16)
    gw["w_out"] = _mm(mixed, dmo, ta=True, name="mm_out_dw", out_dtype=BF16)
    (dzga, dzgb, dya_p, dyb_p), _ = _rowwise_bwd(_fn_mix, mix_rows, [], 0, [[dmixed]], name="gate_mix_bwd", tm=256,
                                                 row_grad=[BF16] * 4)
    gw["w_branch_a"] = _mm(y_a, dya_p, ta=True, name="mm_a_dw", out_dtype=BF16)
    gw["w_branch_b"] = _mm(y_b, dyb_p, ta=True, name="mm_b_dw", out_dtype=BF16)
    token = comm.send_early(gw)
    dy_a = _mm(dya_p, wt["w_branch_a"], tb=True, name="mm_a_dx", out_dtype=BF16, after=token)
    dy_b = _mm(dyb_p, wt["w_branch_b"], tb=True, name="mm_b_dx", out_dtype=BF16, after=token)
    dq, dk, dv, dbias = _attn_bwd(proj, bias, dy_b, seq)
    gw["rel_bias"] = _mm(dbias.reshape(N_HEADS, CHUNK * BAND), onehot, tb=True, name="mm_bias_dw", split_a=2)
    (dy, dr_p, dk2_p, dv_p, dg), (gw["lnx_w"], gw["lnx_b"], gw["bonus_scale"]) = _rowwise_bwd(
        _fn_post, post_rows, post_params, 2, [[dy_a]], name="rwkv_post_bwd", tm=128, row_grad=[F32] * 5)
    dr_s, dlw, dk2_s, dv_s, dkk, da = _wkv_bwd(z_rkv, lw, k2, kk, a, states, invs, dy, seq)
    (dzk, dzw, dza, dzg), pg = _rowwise_bwd(
        _fn_prep, prep_rows, prep_params, 2, [[dlw], [dk2_p, dk2_s], [dkk], [da], [dg]],
        name="rwkv_prep_bwd", tm=128, row_grad=[F32] * 4)
    gw["decay_base"], gd_up, gw["iclr_base"], gi_up, gg_up, gw["key_norm_scale"], gw["key_iclr_scale"] = pg
    gw["decay_up"], gw["iclr_up"], gw["gate_up"] = gd_up[:LORA_W], gi_up[:LORA_A], gg_up[:LORA_G]
    dp_r, gmix_r = _shift_bwd(proj, 0, D, mix_rkv[:, :D], [dr_p, dr_s], seq, name="shift_r_bwd")
    dp_k, gmix_k = _shift_bwd(proj, D, D, mix_rkv[:, D:2 * D], [dzk], seq, name="shift_k_bwd")
    dp_v, gmix_v = _shift_bwd(proj, 2 * D, D, mix_rkv[:, 2 * D:], [dv_p, dv_s], seq, name="shift_v_bwd")
    dp_lora, gmix_lora = _shift_bwd(proj, C_LORA, 512, mix_lora, [jnp.concatenate([dzw, dza, dzg], axis=1)], seq,
                                    name="shift_lora_bwd")
    gw["shift_mix"] = jnp.concatenate([gmix_r, gmix_k, gmix_v, _unpad_lora(gmix_lora)], axis=1)
    dproj = [dp_r, dp_k, dp_v, dq, dk, dv, dzga, dzgb, dp_lora]
    gw["w_in_p"] = _mm_cat_tn(dproj, h1, name="mm_in_dw", after=gw["rel_bias"])
    token = comm.send_late(gw)
    dh1 = _mm_cat_nn(dproj, w_in, name="mm_in_dx", after=token)
    (grad_x,), (gw["g_pre_mix"],) = _rowwise_bwd(_fn_pre, [_win(x)], [g_pre_mix], 0, [[dh1]], name="pre_mix_bwd",
                                                 tm=256, row_grad=[F32], add_to={0: dx0})
    return loss, grad_x, gw


_COL_SHARDED = ("w_in", "decay_up", "iclr_up", "gate_up", "w_o_mem", "w_ffn_in")
_ROW_SHARDED = ("w_branch_a", "w_branch_b", "w_out", "w_q_mem", "w_kv_mem", "w_ffn_out")
_TRANSPOSED = ("w_in", "w_ffn_in")
_FIRST = ("w_in", "decay_up", "iclr_up", "gate_up")
_REST = ("w_o_mem", "w_ffn_in", "w_branch_a", "w_branch_b", "w_out", "w_q_mem", "w_kv_mem", "w_ffn_out")
_REPLICATED = ("g_pre_mix", "g_post_mix", "shift_mix", "decay_base", "iclr_base", "key_norm_scale", "key_iclr_scale",
               "bonus_scale", "lnx_w", "lnx_b", "rel_bias", "g_pre_cross", "g_post_cross", "g_mem", "g_pre_ffn",
               "g_post_ffn")
_WEIGHTS = ("g_pre_mix", "g_post_mix", "w_in", "shift_mix", "decay_base", "decay_up", "iclr_base", "iclr_up", "gate_up",
            "key_norm_scale", "key_iclr_scale", "bonus_scale", "lnx_w", "lnx_b", "rel_bias", "w_branch_a", "w_branch_b",
            "w_out", "g_pre_cross", "g_post_cross", "g_mem", "w_q_mem", "w_kv_mem", "w_o_mem", "g_pre_ffn", "g_post_ffn",
            "w_ffn_in", "w_ffn_out")
_PACK_ROWS = 8 * ((sum({"shift_mix": 3360, "bonus_scale": 1024, "rel_bias": 3072}.get(n, D) for n in _REPLICATED)
                   + 1 + 8 * LANE - 1) // (8 * LANE))


def _pack(vals):
    flat = jnp.concatenate([v.reshape(-1).astype(F32) for v in vals])
    return jnp.pad(flat, (0, _PACK_ROWS * LANE - flat.shape[0])).reshape(_PACK_ROWS, LANE)


def _unpack(packed, shapes):
    flat, out, pos = packed.reshape(-1), [], 0
    for s in shapes:
        n = math.prod(s)
        out.append(flat[pos:pos + n].reshape(s))
        pos += n
    return out


def _step(args, seq, n_mem):
    names = ("x", "mem") + _WEIGHTS + ("loss_target",) + tuple("m_" + n for n in _WEIGHTS) + tuple("v_" + n for n in _WEIGHTS)
    given = dict(zip(names, args))
    nb = given["x"].shape[0]
    x = given["x"].reshape(nb * seq, D)
    mem = given["mem"].reshape(nb * n_mem, D)
    target = given["loss_target"].reshape(nb * seq, D)
    def local(name, prefix=""):
        a = given[prefix + name][0]
        return a.T if name in _TRANSPOSED else a

    shard = {n: local(n) for n in _COL_SHARDED + _ROW_SHARDED}
    stacked = _ROW_SHARDED + _TRANSPOSED
    out = {}

    def full(name, g):
        return g.reshape(-1, g.shape[-1]) if name in stacked else _cols_to_full(g)

    def blocks_of(name, g):
        return (g.reshape((N_DEV,) + shard[name].shape) if name in stacked else _full_to_cols(g)).astype(BF16)

    def update(names, landed, after=None):
        done = []
        for n, parts in zip(names, landed):
            res = _adamw(parts, shard[n], local(n, "m_"), local(n, "v_"), name="adamw_" + n, after=after)
            for kind, r in zip(("grad_", "delta_", "new_m_", "new_v_"), res):
                out[kind + n] = (r.T if n in _TRANSPOSED else r)[None]
            done.append(res[0])
        return done

    class Exchanges:
        def __init__(self):
            srcs = [shard[n].astype(BF16) for n in _FIRST]
            self.first, self.begun = _exchange_start(srcs, False, srcs[0], name="gather_first_start",
                                                     dists=_SIBLING_AND_SAME_CORE)

        def first_weights(self, after):
            got = _exchange_wait(self.first, after, [shard[n] for n in _FIRST], name="gather_first_wait")
            relayed = _relay_to_sibling(got, name="gather_first_relay")
            pos = _mesh_pos()
            for j, d in enumerate(_OTHER_CHIPS):
                slot = _flat(_peer(pos, d | 1))
                got = [lax.dynamic_update_slice_in_dim(g, r[j][None], slot, 0) for g, r in zip(got, relayed)]
            self.rest, self.first_token = _exchange_start(
                [shard[n].astype(BF16) for n in _REST], False, got[0], name="gather_rest_start")
            first = {n: full(n, g) for n, g in zip(_FIRST, got)}
            first["w_in_p"] = _permute_in(first.pop("w_in"), 0)
            return first

        def late_weights(self, after):
            got = _exchange_wait(self.rest, [after], [shard[n] for n in _REST], name="gather_rest_wait")
            return {n: full(n, g) for n, g in zip(_REST, got)}

        def send_early(self, gw):
            self.early_blocks = [blocks_of(n, gw[n]) for n in _REST]
            self.early, token = _exchange_start(self.early_blocks, True, self.early_blocks[-1], name="scatter_rest_start")
            return token

        def send_late(self, gw):
            me = _flat(_mesh_pos())
            own = [lax.dynamic_index_in_dim(b, me, 0, keepdims=False) for b in self.early_blocks]
            landed = _exchange_wait(self.early, [gw["w_in_p"]], own, name="scatter_rest_wait")
            grads = {**gw, "w_in": _unpermute_in(gw["w_in_p"], 0)}
            self.late_blocks = [blocks_of(n, grads[n]) for n in _FIRST]
            self.late, token = _exchange_start(self.late_blocks, True, landed[0], name="scatter_first_start")
            self.updated = update(_REST, landed, after=token)
            return token

        def finish(self, after):
            me = _flat(_mesh_pos())
            own = [lax.dynamic_index_in_dim(b, me, 0, keepdims=False) for b in self.late_blocks]
            update(_FIRST, _exchange_wait(self.late, [*after, *self.updated], own, name="scatter_first_wait"))

    comm = Exchanges()
    wt = {n: given[n][0] for n in _REPLICATED}
    loss_tile, grad_x, gw = _local_step(x, mem, target, wt, seq, n_mem, comm)
    rep_shapes = [given[n].shape for n in _REPLICATED]
    packed, _ = lax.optimization_barrier((_pack([gw[n] for n in _REPLICATED] + [loss_tile[0, 0]]), tuple(comm.updated)))
    small = _exchange([packed], False, name="gather_small")[0]
    zero = jnp.zeros((), F32)
    res = _adamw(small, *[_pack([given[p + n] for n in _REPLICATED] + [zero]) for p in ("", "m_", "v_")],
                 name="adamw_small", tr=_PACK_ROWS)
    for kind, r in zip(("grad_", "delta_", "new_m_", "new_v_"), res):
        for n, val in zip(_REPLICATED, _unpack(r, rep_shapes)):
            out[kind + n] = val
    loss = res[0].reshape(-1)[sum(math.prod(s) for s in rep_shapes)]
    comm.finish([grad_x, res[0]])
    grad_x = grad_x.reshape(nb, seq, D)
    return (loss, grad_x, *[out[k + n] for k in ("grad_", "delta_", "new_m_", "new_v_") for n in _WEIGHTS])


def kernel(x, mem, g_pre_mix, g_post_mix, w_in, shift_mix, decay_base, decay_up, iclr_base, iclr_up, gate_up, key_norm_scale, key_iclr_scale, bonus_scale, lnx_w, lnx_b, rel_bias, w_branch_a, w_branch_b, w_out, g_pre_cross, g_post_cross, g_mem, w_q_mem, w_kv_mem, w_o_mem, g_pre_ffn, g_post_ffn, w_ffn_in, w_ffn_out, loss_target, m_g_pre_mix, m_g_post_mix, m_w_in, m_shift_mix, m_decay_base, m_decay_up, m_iclr_base, m_iclr_up, m_gate_up, m_key_norm_scale, m_key_iclr_scale, m_bonus_scale, m_lnx_w, m_lnx_b, m_rel_bias, m_w_branch_a, m_w_branch_b, m_w_out, m_g_pre_cross, m_g_post_cross, m_g_mem, m_w_q_mem, m_w_kv_mem, m_w_o_mem, m_g_pre_ffn, m_g_post_ffn, m_w_ffn_in, m_w_ffn_out, v_g_pre_mix, v_g_post_mix, v_w_in, v_shift_mix, v_decay_base, v_decay_up, v_iclr_base, v_iclr_up, v_gate_up, v_key_norm_scale, v_key_iclr_scale, v_bonus_scale, v_lnx_w, v_lnx_b, v_rel_bias, v_w_branch_a, v_w_branch_b, v_w_out, v_g_pre_cross, v_g_post_cross, v_g_mem, v_w_q_mem, v_w_kv_mem, v_w_o_mem, v_g_pre_ffn, v_g_post_ffn, v_w_ffn_in, v_w_ffn_out):
    args = (x, mem, g_pre_mix, g_post_mix, w_in, shift_mix, decay_base, decay_up, iclr_base, iclr_up, gate_up, key_norm_scale, key_iclr_scale, bonus_scale, lnx_w, lnx_b, rel_bias, w_branch_a, w_branch_b, w_out, g_pre_cross, g_post_cross, g_mem, w_q_mem, w_kv_mem, w_o_mem, g_pre_ffn, g_post_ffn, w_ffn_in, w_ffn_out, loss_target, m_g_pre_mix, m_g_post_mix, m_w_in, m_shift_mix, m_decay_base, m_decay_up, m_iclr_base, m_iclr_up, m_gate_up, m_key_norm_scale, m_key_iclr_scale, m_bonus_scale, m_lnx_w, m_lnx_b, m_rel_bias, m_w_branch_a, m_w_branch_b, m_w_out, m_g_pre_cross, m_g_post_cross, m_g_mem, m_w_q_mem, m_w_kv_mem, m_w_o_mem, m_g_pre_ffn, m_g_post_ffn, m_w_ffn_in, m_w_ffn_out, v_g_pre_mix, v_g_post_mix, v_w_in, v_shift_mix, v_decay_base, v_decay_up, v_iclr_base, v_iclr_up, v_gate_up, v_key_norm_scale, v_key_iclr_scale, v_bonus_scale, v_lnx_w, v_lnx_b, v_rel_bias, v_w_branch_a, v_w_branch_b, v_w_out, v_g_pre_cross, v_g_post_cross, v_g_mem, v_w_q_mem, v_w_kv_mem, v_w_o_mem, v_g_pre_ffn, v_g_post_ffn, v_w_ffn_in, v_w_ffn_out)
    return _step(args, x.shape[1], mem.shape[1])
```

```python
import functools
import math

import jax
import jax.numpy as jnp
from jax import lax
from jax.experimental import pallas as pl
from jax.experimental.pallas import tpu as pltpu

F32 = jnp.float32
BF16 = jnp.bfloat16

N_DEV = 8
D = 1024
HEAD = 64
N_HEADS = D // HEAD
LANE = 128
N_PAIRS = D // LANE
CHUNK = 64
LEFT = 8 * CHUNK
BAND = LEFT + CHUNK
REL_CLIP = 128
REL_TABLE = CHUNK + REL_CLIP
MEM_WIDTH = D // 2
MEM_HEADS = 4
FFN = 2816
LORA_W, LORA_A, LORA_G = 64, 64, 160
P_WIDTH = 3 * D + 3 * D + 2 * D + 128 + 128 + 256
C_Q, C_GA, C_LORA = 3 * D, 6 * D, 8 * D
NORM_EPS = 1e-6
GROUP_NORM_EPS = 64e-5
MASK_VALUE = -1e30
ADAM_LR, ADAM_B1, ADAM_B2, ADAM_EPS, ADAM_WD, ADAM_STEP = 0.001, 0.9, 0.999, 1e-08, 0.01, 10
VMEM_LIMIT = 56 * 1024 * 1024


def _cp(*sem):
    return pltpu.CompilerParams(dimension_semantics=sem, vmem_limit_bytes=VMEM_LIMIT)


_NN, _NT, _TN = ((1,), (0,)), ((1,), (1,)), ((0,), (0,))


def _dot_raw(a, b, dims):
    return lax.dot_general(a.astype(BF16), b.astype(BF16), (dims, ((), ())), preferred_element_type=F32)


@functools.partial(jax.custom_vjp, nondiff_argnums=(2,))
def _dot_dims(a, b, dims):
    return _dot_raw(a, b, dims)


def _dot_dims_fwd(a, b, dims):
    return _dot_raw(a, b, dims), (a, b)


def _dot_dims_bwd(dims, res, g):
    a, b = res
    if dims == _NN:
        da, db = _dot_raw(g, b, _NT), _dot_raw(a, g, _TN)
    elif dims == _NT:
        da, db = _dot_raw(g, b, _NN), _dot_raw(g, a, _TN)
    else:
        da, db = _dot_raw(b, g, _NT), _dot_raw(a, g, _NN)
    return da.astype(a.dtype), db.astype(b.dtype)


_dot_dims.defvjp(_dot_dims_fwd, _dot_dims_bwd)


def _dot(a, b, dims=_NN):
    return _dot_dims(a, b, dims)


def _dot_nt(a, b):
    return _dot_dims(a, b, _NT)


def _dot_tn(a, b):
    return _dot_dims(a, b, _TN)


def _split(x, terms):
    parts, rest = [], x.astype(F32)
    for _ in range(terms):
        p = rest.astype(BF16)
        parts.append(p)
        rest = rest - p.astype(F32)
    return parts


def _dot_split_a(a, b, terms=2):
    out = None
    for p in _split(a, terms):
        t = _dot(p, b)
        out = t if out is None else out + t
    return out


def _dot_split_b(a, b, terms=3):
    out = None
    for p in _split(b, terms):
        t = _dot(a, p)
        out = t if out is None else out + t
    return out


def _dot_hi(a, b, dims=_NN):
    ah, al = _split(a, 2)
    bh, bl = _split(b, 2)
    return _dot(ah, bh, dims) + (_dot(ah, bl, dims) + _dot(al, bh, dims))


MM_VMEM_BUDGET = 30 * 1024 * 1024
MM_HBM_BPS = 3.2e12
MM_MXU_FPS = 8.5e14
MM_STEP_S = 0.35e-6


def _divisors(n, align, cap):
    out = [d for d in range(align, min(n, cap) + 1, align) if n % d == 0]
    return out or [n]


def _mm_tiles(m, n, k, ea, eb, eo, ta):
    best = None
    for tm in _divisors(m, LANE if ta else 8, 2048):
        for tn in _divisors(n, LANE, 2048):
            for tk in _divisors(k, LANE, 2048):
                nk = k // tk
                vmem = 2 * (tm * tk * ea + tk * tn * eb + tm * tn * eo) + (tm * tn * 4 if nk > 1 else 0)
                if vmem > MM_VMEM_BUDGET:
                    continue
                dma = (tm * tk * ea if (nk > 1 or n // tn == 1) else tm * tk * ea * tn / n) + tk * tn * eb + tm * tn * eo / nk
                step = max(2.0 * tm * tn * tk / MM_MXU_FPS, dma / MM_HBM_BPS) + MM_STEP_S
                cost = (m // tm) * (n // tn) * nk * step
                if best is None or cost < best[0]:
                    best = (cost, tm, tn, tk)
    return best[1:]


def _mm(a, b, *, name, ta=False, tb=False, out_dtype=F32, tm=None, tn=None, tk=None, split_a=1, after=None):
    m, k = (a.shape[1], a.shape[0]) if ta else a.shape
    n, kb = (b.shape[0], b.shape[1]) if tb else (b.shape[1], b.shape[0])
    assert k == kb, (a.shape, b.shape, ta, tb)
    if tm is None:
        tm, tn, tk = _mm_tiles(m, n, k, a.dtype.itemsize, b.dtype.itemsize, jnp.dtype(out_dtype).itemsize, ta)
    assert m % tm == 0 and n % tn == 0 and k % tk == 0, (m, n, k, tm, tn, tk)
    nk = k // tk
    dims = ((0 if ta else 1,), (1 if tb else 0,))

    n_after = 0 if after is None else 1

    def body(a_ref, b_ref, *rest):
        o_ref, scratch = rest[n_after], rest[n_after + 1:]
        prod = None
        for p in _split(a_ref[...], split_a) if split_a > 1 else [a_ref[...]]:
            t = _dot_raw(p, b_ref[...], dims)
            prod = t if prod is None else prod + t
        if nk == 1:
            o_ref[...] = prod.astype(o_ref.dtype)
            return
        acc_ref, kk = scratch[0], pl.program_id(2)

        @pl.when(kk == 0)
        def _():
            acc_ref[...] = prod

        @pl.when(kk > 0)
        def _():
            acc_ref[...] += prod

        @pl.when(kk == nk - 1)
        def _():
            o_ref[...] = acc_ref[...].astype(o_ref.dtype)

    a_spec = pl.BlockSpec((tk, tm), lambda i, j, q: (q, i)) if ta else pl.BlockSpec((tm, tk), lambda i, j, q: (i, q))
    b_spec = pl.BlockSpec((tn, tk), lambda i, j, q: (j, q)) if tb else pl.BlockSpec((tk, tn), lambda i, j, q: (q, j))
    return pl.pallas_call(
        body, name=name, grid=(m // tm, n // tn, nk),
        in_specs=[a_spec, b_spec] + [pl.BlockSpec(memory_space=pl.ANY)] * n_after,
        out_specs=pl.BlockSpec((tm, tn), lambda i, j, q: (i, j)),
        out_shape=jax.ShapeDtypeStruct((m, n), out_dtype),
        scratch_shapes=[pltpu.VMEM((tm, tn), F32)] if nk > 1 else [],
        compiler_params=_cp("parallel", "parallel", "arbitrary"),
    )(a, b, *([] if after is None else [after]))


def _piece_steps(pieces, tile):
    counts = [p.shape[1] // tile for p in pieces]
    assert all(p.shape[1] % tile == 0 for p in pieces)
    return [(sum(counts[:i]), c) for i, c in enumerate(counts)], sum(counts)


def _mm_cat_nn(pieces, w, *, name, after=None, tm=1024, tk=512):
    t, n = pieces[0].shape[0], w.shape[1]
    tm = min(tm, t)
    spans, nk = _piece_steps(pieces, tk)
    npc = len(pieces)
    n_after = 0 if after is None else 1

    def body(*refs):
        w_ref, o_ref, acc_ref = refs[npc], refs[npc + 1 + n_after], refs[npc + 2 + n_after]
        q = pl.program_id(1)

        @pl.when(q == 0)
        def _():
            acc_ref[...] = jnp.zeros_like(acc_ref)

        for p_ref, (first, count) in zip(refs[:npc], spans):
            @pl.when(jnp.logical_and(q >= first, q < first + count))
            def _(p_ref=p_ref):
                acc_ref[...] += _dot_raw(p_ref[...], w_ref[...], _NN)

        @pl.when(q == nk - 1)
        def _():
            o_ref[...] = acc_ref[...].astype(o_ref.dtype)

    def piece_spec(first, count):
        return pl.BlockSpec((tm, tk), lambda i, q: (i, jnp.clip(q - first, 0, count - 1)))

    return pl.pallas_call(
        body, name=name, grid=(t // tm, nk),
        in_specs=[piece_spec(*s) for s in spans] + [pl.BlockSpec((tk, n), lambda i, q: (q, 0))]
        + [pl.BlockSpec(memory_space=pl.ANY)] * n_after,
        out_specs=pl.BlockSpec((tm, n), lambda i, q: (i, 0)),
        out_shape=jax.ShapeDtypeStruct((t, n), BF16),
        scratch_shapes=[pltpu.VMEM((tm, n), F32)],
        compiler_params=_cp("parallel", "arbitrary"),
    )(*pieces, w, *([] if after is None else [after]))


def _mm_cat_tn(pieces, a, *, name, after=None, tk=1024, tn=512):
    t, m = a.shape
    tk = min(tk, t)
    spans, nj = _piece_steps(pieces, tn)
    npc, nk = len(pieces), t // tk
    n_after = 0 if after is None else 1

    def body(a_ref, *refs):
        o_ref, acc_ref = refs[npc + n_after], refs[npc + 1 + n_after]
        j, q = pl.program_id(0), pl.program_id(1)

        @pl.when(q == 0)
        def _():
            acc_ref[...] = jnp.zeros_like(acc_ref)

        for p_ref, (first, count) in zip(refs[:npc], spans):
            @pl.when(jnp.logical_and(j >= first, j < first + count))
            def _(p_ref=p_ref):
                acc_ref[...] += _dot_raw(p_ref[...], a_ref[...], _TN)

        @pl.when(q == nk - 1)
        def _():
            o_ref[...] = acc_ref[...].astype(o_ref.dtype)

    def piece_spec(first, count):
        def index(j, q):
            mine = jnp.logical_and(j >= first, j < first + count)
            return jnp.where(mine, q, 0), jnp.clip(j - first, 0, count - 1)
        return pl.BlockSpec((tk, tn), index)

    return pl.pallas_call(
        body, name=name, grid=(nj, nk),
        in_specs=[pl.BlockSpec((tk, m), lambda j, q: (q, 0))] + [piece_spec(*s) for s in spans]
        + [pl.BlockSpec(memory_space=pl.ANY)] * n_after,
        out_specs=pl.BlockSpec((tn, m), lambda j, q: (j, 0)),
        out_shape=jax.ShapeDtypeStruct((nj * tn, m), BF16),
        scratch_shapes=[pltpu.VMEM((tn, m), F32)],
        compiler_params=_cp("parallel", "arbitrary"),
    )(a, *pieces, *([] if after is None else [after]))


def _win(arr, start=0, width=None):
    width = arr.shape[1] if width is None else width
    assert start % width == 0
    return (arr, start // width, width)


def _row_specs(rows, tm):
    return [pl.BlockSpec((tm, w), functools.partial(lambda i, cb: (i, cb), cb=cb)) for (_, cb, w) in rows]


def _full_spec(p):
    nd = p.ndim
    return pl.BlockSpec(p.shape, lambda i, nd=nd: (0,) * nd)


def _rowwise(fn, rows, params, outs, *, name, tm, after=None):
    t = rows[0][0].shape[0]
    tm = min(tm, t)
    assert t % tm == 0
    nr, npar = len(rows), len(params)
    n_after = 0 if after is None else 1

    def body(*refs):
        vals = [r[...] for r in refs[:nr + npar]]
        res = fn(*vals)
        for o_ref, r in zip(refs[nr + npar + n_after:], res):
            o_ref[...] = r.astype(o_ref.dtype)

    return pl.pallas_call(
        body, name=name, grid=(t // tm,),
        in_specs=_row_specs(rows, tm) + [_full_spec(p) for p in params] + [pl.BlockSpec(memory_space=pl.ANY)] * n_after,
        out_specs=[pl.BlockSpec((tm, w), lambda i: (i, 0)) for (w, _) in outs],
        out_shape=[jax.ShapeDtypeStruct((t, w), dt) for (w, dt) in outs],
        compiler_params=_cp("parallel"),
    )(*[r[0] for r in rows], *params, *([] if after is None else [after]))


def _rowwise_bwd(fn, rows, params, n_const, cots, *, name, tm, row_grad, add_to=None, packed=False):
    t = rows[0][0].shape[0]
    tm = min(tm, t)
    assert t % tm == 0
    nr, npar = len(rows), len(params)
    ndp = npar - n_const
    add_to = add_to or {}
    add_idx = sorted(add_to)
    flat_cots = [c for group in cots for c in group]
    kept = [i for i in range(nr) if row_grad[i] is not None]

    def body(*refs):
        pos = 0
        row_v = [r[...] for r in refs[pos:pos + nr]]; pos += nr
        par_v = [r[...] for r in refs[pos:pos + npar]]; pos += npar
        cot_v = [r[...] for r in refs[pos:pos + len(flat_cots)]]; pos += len(flat_cots)
        add_v = [r[...] for r in refs[pos:pos + len(add_idx)]]; pos += len(add_idx)
        if packed:
            offs = [sum(rows[i][2] for i in kept[:q]) for q in range(len(kept))]
            rg_refs = [refs[pos].at[:, o:o + rows[i][2]] for o, i in zip(offs, kept)]; pos += 1
        else:
            rg_refs = refs[pos:pos + len(kept)]; pos += len(kept)
        pg_refs = refs[pos:pos + ndp]

        consts = par_v[ndp:]
        res, vjp = jax.vjp(lambda *args: tuple(fn(*args, *consts)), *row_v, *par_v[:ndp])
        cot_in, q = [], 0
        for j, group in enumerate(cots):
            c = None
            for _ in group:
                cv = cot_v[q].astype(F32); q += 1
                c = cv if c is None else c + cv
            c = jnp.zeros(res[j].shape, F32) if c is None else c
            cot_in.append(c.astype(res[j].dtype))
        grads = vjp(tuple(cot_in))
        for ref, i in zip(rg_refs, kept):
            g = grads[i].astype(F32)
            if i in add_to:
                g = g + add_v[add_idx.index(i)].astype(F32)
            ref[...] = g.astype(ref.dtype)

        @pl.when(pl.program_id(0) == 0)
        def _():
            for ref in pg_refs:
                ref[...] = jnp.zeros_like(ref)

        for ref, g in zip(pg_refs, grads[nr:]):
            ref[...] += g.astype(F32)

    cot_specs = [pl.BlockSpec((tm, c.shape[1]), lambda i: (i, 0)) for c in flat_cots]
    add_specs = [pl.BlockSpec((tm, add_to[i].shape[1]), lambda i_: (i_, 0)) for i in add_idx]
    widths = [sum(rows[i][2] for i in kept)] if packed else [rows[i][2] for i in kept]
    n_rg = len(widths)
    out_specs = [pl.BlockSpec((tm, w), lambda i_: (i_, 0)) for w in widths] + [_full_spec(p) for p in params[:ndp]]
    out_shape = [jax.ShapeDtypeStruct((t, w), row_grad[kept[q]]) for q, w in enumerate(widths)] + [
        jax.ShapeDtypeStruct(p.shape, F32) for p in params[:ndp]]
    res = pl.pallas_call(
        body, name=name, grid=(t // tm,),
        in_specs=_row_specs(rows, tm) + [_full_spec(p) for p in params] + cot_specs + add_specs,
        out_specs=out_specs, out_shape=out_shape,
        compiler_params=_cp("arbitrary"),
    )(*[r[0] for r in rows], *params, *flat_cots, *[add_to[i] for i in add_idx])
    return list(res[:n_rg]), list(res[n_rg:])


def _rms(x, g):
    xf = x.astype(F32)
    return xf * lax.rsqrt(jnp.mean(xf * xf, axis=-1, keepdims=True) + NORM_EPS) * g


def _softplus(x):
    return jnp.maximum(x, 0.0) + jnp.log(1.0 + jnp.exp(-jnp.abs(x)))


def _fn_pre(x, g):
    return (_rms(x, g).astype(BF16),)


def _fn_res(x, u, g_post):
    return (x + _rms(u, g_post),)


def _fn_res_pre(x, u, g_post, g_pre):
    xn = x + _rms(u, g_post)
    return xn, _rms(xn, g_pre).astype(BF16)


def _fn_mix(zga, zgb, ya, yb):
    return ((jax.nn.sigmoid(zga) * ya + jax.nn.sigmoid(zgb) * yb).astype(BF16),)


def _fn_swiglu(gate, up):
    gate, up = gate.astype(F32), up.astype(F32)
    return ((gate * jax.nn.sigmoid(gate) * up).astype(BF16),)


def _fn_prep(zk, zw, za, zg, decay_base, d_up, iclr_base, i_up, g_up, kns, kis, e_hd, e_dh):
    w_log = -_softplus(-(decay_base + _dot(jnp.tanh(zw), d_up))) - 0.5
    lw = -jnp.exp(w_log)
    a = jax.nn.sigmoid(iclr_base + _dot(za, i_up))
    g = _dot(jax.nn.sigmoid(zg), g_up)
    kn = zk * kns
    ss = _dot_split_a(kn * kn, e_dh)
    inv = lax.rsqrt(jnp.maximum(ss, 1e-24))
    kk = kn * _dot_split_a(inv, e_hd)
    k2 = zk * (1.0 + (a - 1.0) * kis)
    return lw, k2, kk, a, g


def _fn_post(y, r, k2, v, g, lnx_w, lnx_b, bonus, e_hd, e_dh):
    mu = _dot_split_a(_dot_split_a(y, e_dh) * (1.0 / HEAD), e_hd)
    yc = y - mu
    var = _dot_split_a(yc * yc, e_dh) * (1.0 / HEAD)
    yn = yc * _dot_split_a(lax.rsqrt(var + GROUP_NORM_EPS), e_hd)
    bs = _dot_split_a(_dot_split_a(r * k2 * bonus, e_dh), e_hd)
    return (((yn * lnx_w + lnx_b + bs * v) * g).astype(BF16),)


def _shift_fwd(p, col0, ncols, mix, seq, *, name, cw=256):
    t = p.shape[0]
    assert col0 % cw == 0 and ncols % cw == 0 and t % seq == 0
    cb0 = col0 // cw

    def body(p_ref, m_ref, z_ref):
        pv = p_ref[...]
        row = lax.broadcasted_iota(jnp.int32, pv.shape, 0)
        prev = jnp.where(row == 0, 0.0, pltpu.roll(pv, 1, axis=0))
        z_ref[...] = pv + (prev - pv) * m_ref[...]

    return pl.pallas_call(
        body, name=name, grid=(t // seq, ncols // cw),
        in_specs=[pl.BlockSpec((seq, cw), lambda b, c: (b, c + cb0)), pl.BlockSpec((1, cw), lambda b, c: (0, c))],
        out_specs=pl.BlockSpec((seq, cw), lambda b, c: (b, c)),
        out_shape=jax.ShapeDtypeStruct((t, ncols), F32),
        compiler_params=_cp("parallel", "parallel"),
    )(p, mix)


def _shift_bwd(p, col0, ncols, mix, dz_parts, seq, *, name, cw=256):
    t = p.shape[0]
    cb0 = col0 // cw
    n = len(dz_parts)

    def body(*refs):
        p_ref, m_ref = refs[:2]
        dp_ref, dm_ref = refs[2 + n:]
        dz = refs[2][...].astype(F32)
        for r in refs[3:2 + n]:
            dz = dz + r[...].astype(F32)
        pv = p_ref[...]
        mixv = m_ref[...]
        row = lax.broadcasted_iota(jnp.int32, pv.shape, 0)
        prev = jnp.where(row == 0, 0.0, pltpu.roll(pv, 1, axis=0))
        u = dz * mixv
        nxt = jnp.where(row == seq - 1, 0.0, pltpu.roll(u, seq - 1, axis=0))
        dp_ref[...] = (dz - u + nxt).astype(dp_ref.dtype)

        @pl.when(pl.program_id(1) == 0)
        def _():
            dm_ref[...] = jnp.zeros_like(dm_ref)

        dm_ref[...] += jnp.sum(dz * (prev - pv), axis=0, keepdims=True)

    return pl.pallas_call(
        body, name=name, grid=(ncols // cw, t // seq),
        in_specs=[pl.BlockSpec((seq, cw), lambda c, b: (b, c + cb0)), pl.BlockSpec((1, cw), lambda c, b: (0, c))]
        + [pl.BlockSpec((seq, cw), lambda c, b: (b, c))] * n,
        out_specs=[pl.BlockSpec((seq, cw), lambda c, b: (b, c)), pl.BlockSpec((1, cw), lambda c, b: (0, c))],
        out_shape=[jax.ShapeDtypeStruct((t, ncols), BF16), jax.ShapeDtypeStruct((1, ncols), F32)],
        compiler_params=_cp("parallel", "arbitrary"),
    )(p, mix, *dz_parts)


def _each(f, *lists):
    return [f(*xs) for xs in zip(*lists)]


def _tri_inv(low):
    c = low[0].shape[0]
    ti = lax.broadcasted_iota(jnp.int32, (c, c), 0)
    si = lax.broadcasted_iota(jnp.int32, (c, c), 1)
    eye = (ti == si).astype(F32)
    inside = (ti // 4) == (si // 4)
    base = [jnp.where(inside, m, 0.0) for m in low]
    acc = _each(lambda m: _dot(eye - m, eye + _dot(m, m)), base)
    size = 8
    while size <= c:
        wider = (ti // size) == (si // size)
        keep = jnp.logical_and(wider, jnp.logical_not(inside))
        acc = _each(lambda p, m: p - _dot(_dot(p, jnp.where(keep, m, 0.0)), p), acc, low)
        inside, size = wider, size * 2
    return acc


def _stack_rows(a, b):
    return jnp.concatenate([a, b], axis=0)


@jax.custom_vjp
def _split_rows(x):
    h = x.shape[0] // 2
    return x[:h], x[h:]


def _split_rows_fwd(x):
    return _split_rows(x), None


def _split_rows_bwd(_, g):
    return (jnp.concatenate(g, axis=0),)


_split_rows.defvjp(_split_rows_fwd, _split_rows_bwd)


def _masked_halves(stacked, top_mask, bottom_mask):
    halves = _each(_split_rows, stacked)
    return ([jnp.where(top_mask, t, 0.0) for t, _ in halves], [jnp.where(bottom_mask, b, 0.0) for _, b in halves])


@jax.custom_vjp
def _tri_inv_known(low, inv):
    return inv


def _tri_inv_known_fwd(low, inv):
    return inv, inv


def _tri_inv_known_bwd(inv, g):
    dlow = _each(lambda t, gg: -_dot(_dot(t, gg, _TN), t, _NT), inv, g)
    return dlow, _each(jnp.zeros_like, inv)


_tri_inv_known.defvjp(_tri_inv_known_fwd, _tri_inv_known_bwd)


def _wkv_chunk(s0, r, lw, k, v, kk, a, inv=None):
    c = r[0].shape[0]
    ti = lax.broadcasted_iota(jnp.int32, (c, c), 0)
    si = lax.broadcasted_iota(jnp.int32, (c, c), 1)
    incl, strict = ti >= si, ti > si
    tri = incl.astype(F32)
    cum = _each(lambda x: _dot_split_b(tri, x, 3), lw)
    eg = _each(jnp.exp, cum)
    egp = _each(lambda cs, x: jnp.exp(cs - x), cum, lw)
    ei = _each(lambda cs: jnp.exp(-cs), cum)
    rh, kkh, kt = _each(jnp.multiply, r, eg), _each(jnp.multiply, kk, egp), _each(jnp.multiply, k, ei)
    bt = _each(lambda p, q, e: (p * q) * e, a, kk, ei)
    both = _each(_stack_rows, kkh, rh)
    on_b, on_k, on_s = _each(_dot_nt, both, bt), _each(_dot_nt, both, kt), _each(_dot_nt, both, s0)
    lb, mb = _masked_halves(on_b, strict, incl)
    lk, mk = _masked_halves(on_k, strict, incl)
    on_s = _each(_split_rows, on_s)
    on_v = _each(lambda p, q, x: _split_rows(_dot(_stack_rows(p, q), x)), lk, mk, v)
    rhs = _each(lambda p, q: p[0] + q[0], on_s, on_v)
    inv = _tri_inv(lb) if inv is None else _tri_inv_known(lb, inv)
    u = _each(lambda t, x: -_dot(t, x), inv, rhs)
    y = _each(lambda p, m1, uu, q: p[1] + _dot(m1, uu) + q[1], on_s, mb, u, on_v)
    s1 = _each(lambda s, uu, x, b, kq, w: (s + _dot_tn(_stack_rows(uu, x), _stack_rows(b, kq)))
               * jnp.exp(jnp.sum(w, axis=0, keepdims=True)), s0, u, v, bt, kt, lw)
    return y, s1, inv


WKV_HEADS = 16
WKV_COLS = WKV_HEADS * HEAD
WKV_GROUPS = N_HEADS // WKV_HEADS


def _head_cols(ref):
    return [ref[:, h * HEAD:(h + 1) * HEAD] for h in range(ref.shape[1] // HEAD)]


def _wkv_specs(seq, rev):
    nc = seq // CHUNK

    def rows(col0):
        cb0 = col0 // WKV_COLS
        if rev:
            return pl.BlockSpec((CHUNK, WKV_COLS), lambda b, h, c: (b * nc + nc - 1 - c, cb0 + h))
        return pl.BlockSpec((CHUNK, WKV_COLS), lambda b, h, c: (b * nc + c, cb0 + h))

    if rev:
        st = pl.BlockSpec((1, 1, WKV_HEADS, HEAD, HEAD), lambda b, h, c: (b * WKV_GROUPS + h, nc - 1 - c, 0, 0, 0))
    else:
        st = pl.BlockSpec((1, 1, WKV_HEADS, HEAD, HEAD), lambda b, h, c: (b * WKV_GROUPS + h, c, 0, 0, 0))
    return rows, st


def _wkv_fwd(z_rkv, lw, k2, kk, a, seq):
    t = z_rkv.shape[0]
    nb, nc = t // seq, seq // CHUNK
    rows, st = _wkv_specs(seq, False)

    def body(r_ref, v_ref, lw_ref, k_ref, kk_ref, a_ref, y_ref, st_ref, inv_ref, s_scr):
        @pl.when(pl.program_id(2) == 0)
        def _():
            s_scr[...] = jnp.zeros_like(s_scr)

        s0 = [s_scr[h] for h in range(WKV_HEADS)]
        y, s1, inv = _wkv_chunk(s0, *[_head_cols(ref) for ref in (r_ref, lw_ref, k_ref, v_ref, kk_ref, a_ref)])
        for h in range(WKV_HEADS):
            st_ref[0, 0, h] = s0[h]
            inv_ref[0, 0, h] = inv[h]
            y_ref[:, h * HEAD:(h + 1) * HEAD] = y[h]
            s_scr[h] = s1[h]

    per_chunk = jax.ShapeDtypeStruct((nb * WKV_GROUPS, nc, WKV_HEADS, HEAD, HEAD), F32)
    return pl.pallas_call(
        body, name="wkv_fwd", grid=(nb, WKV_GROUPS, nc),
        in_specs=[rows(0), rows(2 * D), rows(0), rows(0), rows(0), rows(0)],
        out_specs=[rows(0), st, st],
        out_shape=[jax.ShapeDtypeStruct((t, D), F32), per_chunk, per_chunk],
        scratch_shapes=[pltpu.VMEM((WKV_HEADS, HEAD, HEAD), F32)],
        compiler_params=_cp("parallel", "parallel", "arbitrary"),
    )(z_rkv, z_rkv, lw, k2, kk, a)


def _wkv_bwd(z_rkv, lw, k2, kk, a, states, invs, dy, seq):
    t = z_rkv.shape[0]
    nb, nc = t // seq, seq // CHUNK
    rows, st = _wkv_specs(seq, True)

    def body(r_ref, v_ref, lw_ref, k_ref, kk_ref, a_ref, st_ref, inv_ref, dy_ref,
             dr_ref, dlw_ref, dk_ref, dv_ref, dkk_ref, da_ref, ds_scr):
        @pl.when(pl.program_id(2) == 0)
        def _():
            ds_scr[...] = jnp.zeros_like(ds_scr)

        s0 = [st_ref[0, 0, h] for h in range(WKV_HEADS)]
        inv = [inv_ref[0, 0, h] for h in range(WKV_HEADS)]
        _, vjp = jax.vjp(lambda *args: _wkv_chunk(*args, inv=inv)[:2],
                         s0, *[_head_cols(ref) for ref in (r_ref, lw_ref, k_ref, v_ref, kk_ref, a_ref)])
        grads = vjp(([x.astype(F32) for x in _head_cols(dy_ref)], [ds_scr[h] for h in range(WKV_HEADS)]))
        for h in range(WKV_HEADS):
            ds_scr[h] = grads[0][h]
            for ref, g in zip((dr_ref, dlw_ref, dk_ref, dv_ref, dkk_ref, da_ref), grads[1:]):
                ref[:, h * HEAD:(h + 1) * HEAD] = g[h]

    return pl.pallas_call(
        body, name="wkv_bwd", grid=(nb, WKV_GROUPS, nc),
        in_specs=[rows(0), rows(2 * D), rows(0), rows(0), rows(0), rows(0), st, st, rows(0)],
        out_specs=[rows(0)] * 6,
        out_shape=[jax.ShapeDtypeStruct((t, D), F32)] * 6,
        scratch_shapes=[pltpu.VMEM((WKV_HEADS, HEAD, HEAD), F32)],
        compiler_params=_cp("parallel", "parallel", "arbitrary"),
    )(z_rkv, z_rkv, lw, k2, kk, a, states, invs, dy)


def _softmax(s):
    e = jnp.exp(s - jnp.max(s, axis=-1, keepdims=True))
    return e / jnp.sum(e, axis=-1, keepdims=True)


ATT_HEADS = 8
ATT_COLS = ATT_HEADS * HEAD
ATT_GROUPS = N_HEADS // ATT_HEADS


def _attn_chunk(q, kb, vb, bias, valid):
    s = _each(lambda x, y, z: jnp.where(valid, _dot_nt(x, y) * (HEAD ** -0.5) + z, MASK_VALUE), q, kb, bias)
    return _each(_dot, _each(_softmax, s), vb)


def _pad_fill(pad_ref, src_ref):
    pad_ref[0:LEFT, :] = jnp.zeros((LEFT, pad_ref.shape[1]), pad_ref.dtype)
    pad_ref[LEFT:, :] = src_ref[...].astype(pad_ref.dtype)


def _band_heads(pad_ref, start):
    return [pad_ref[pl.ds(start, BAND), h * HEAD:(h + 1) * HEAD].astype(F32) for h in range(ATT_HEADS)]


def _band_valid(c):
    return (c * CHUNK - LEFT + lax.broadcasted_iota(jnp.int32, (1, BAND), 1)) >= 0


def _attn_fwd(proj, bias, seq):
    t = proj.shape[0]
    nb, nc = t // seq, seq // CHUNK
    cq = C_Q // ATT_COLS

    def body(q_ref, k_ref, v_ref, b_ref, o_ref, kpad, vpad):
        c = pl.program_id(2)

        @pl.when(c == 0)
        def _():
            _pad_fill(kpad, k_ref)
            _pad_fill(vpad, v_ref)

        start = pl.multiple_of(c * CHUNK, CHUNK)
        o = _attn_chunk(_head_cols(q_ref), _band_heads(kpad, start), _band_heads(vpad, start),
                        [b_ref[h] for h in range(ATT_HEADS)], _band_valid(c))
        for h in range(ATT_HEADS):
            o_ref[:, h * HEAD:(h + 1) * HEAD] = o[h].astype(o_ref.dtype)

    return pl.pallas_call(
        body, name="attn_fwd", grid=(ATT_GROUPS, nb, nc),
        in_specs=[pl.BlockSpec((CHUNK, ATT_COLS), lambda h, b, c: (b * nc + c, cq + h)),
                  pl.BlockSpec((seq, ATT_COLS), lambda h, b, c: (b, cq + ATT_GROUPS + h)),
                  pl.BlockSpec((seq, ATT_COLS), lambda h, b, c: (b, cq + 2 * ATT_GROUPS + h)),
                  pl.BlockSpec((ATT_HEADS, CHUNK, BAND), lambda h, b, c: (h, 0, 0))],
        out_specs=pl.BlockSpec((CHUNK, ATT_COLS), lambda h, b, c: (b * nc + c, h)),
        out_shape=jax.ShapeDtypeStruct((t, D), BF16),
        scratch_shapes=[pltpu.VMEM((seq + LEFT, ATT_COLS), BF16)] * 2,
        compiler_params=_cp("parallel", "arbitrary", "arbitrary"),
    )(proj, proj, proj, bias)


def _attn_bwd(proj, bias, do, seq):
    t = proj.shape[0]
    nb, nc = t // seq, seq // CHUNK
    cq = C_Q // ATT_COLS

    def body(q_ref, k_ref, v_ref, b_ref, do_ref, dq_ref, dk_ref, dv_ref, db_ref, kpad, vpad, dkpad, dvpad):
        b, c = pl.program_id(1), pl.program_id(2)

        @pl.when(c == 0)
        def _():
            _pad_fill(kpad, k_ref)
            _pad_fill(vpad, v_ref)
            dkpad[...] = jnp.zeros_like(dkpad)
            dvpad[...] = jnp.zeros_like(dvpad)

        @pl.when(jnp.logical_and(b == 0, c == 0))
        def _():
            db_ref[...] = jnp.zeros_like(db_ref)

        start = pl.multiple_of(c * CHUNK, CHUNK)
        _, vjp = jax.vjp(functools.partial(_attn_chunk, valid=_band_valid(c)),
                         _head_cols(q_ref), _band_heads(kpad, start), _band_heads(vpad, start),
                         [b_ref[h] for h in range(ATT_HEADS)])
        dq, dkb, dvb, dbias = vjp([x.astype(F32) for x in _head_cols(do_ref)])
        for h in range(ATT_HEADS):
            sl = slice(h * HEAD, (h + 1) * HEAD)
            dq_ref[:, sl] = dq[h].astype(dq_ref.dtype)
            dkpad[pl.ds(start, BAND), sl] += dkb[h].astype(F32)
            dvpad[pl.ds(start, BAND), sl] += dvb[h].astype(F32)
            db_ref[h] += dbias[h]

        @pl.when(c == nc - 1)
        def _():
            dk_ref[...] = dkpad[LEFT:, :].astype(dk_ref.dtype)
            dv_ref[...] = dvpad[LEFT:, :].astype(dv_ref.dtype)

    kv_out = pl.BlockSpec((seq, ATT_COLS), lambda h, b, c: (b, h))
    return pl.pallas_call(
        body, name="attn_bwd", grid=(ATT_GROUPS, nb, nc),
        in_specs=[pl.BlockSpec((CHUNK, ATT_COLS), lambda h, b, c: (b * nc + c, cq + h)),
                  pl.BlockSpec((seq, ATT_COLS), lambda h, b, c: (b, cq + ATT_GROUPS + h)),
                  pl.BlockSpec((seq, ATT_COLS), lambda h, b, c: (b, cq + 2 * ATT_GROUPS + h)),
                  pl.BlockSpec((ATT_HEADS, CHUNK, BAND), lambda h, b, c: (h, 0, 0)),
                  pl.BlockSpec((CHUNK, ATT_COLS), lambda h, b, c: (b * nc + c, h))],
        out_specs=[pl.BlockSpec((CHUNK, ATT_COLS), lambda h, b, c: (b * nc + c, h)), kv_out, kv_out,
                   pl.BlockSpec((ATT_HEADS, CHUNK, BAND), lambda h, b, c: (h, 0, 0))],
        out_shape=[jax.ShapeDtypeStruct((t, D), BF16)] * 3 + [jax.ShapeDtypeStruct((N_HEADS, CHUNK, BAND), F32)],
        scratch_shapes=[pltpu.VMEM((seq + LEFT, ATT_COLS), BF16)] * 2 + [pltpu.VMEM((seq + LEFT, ATT_COLS), F32)] * 2,
        compiler_params=_cp("parallel", "arbitrary", "arbitrary"),
    )(proj, proj, proj, bias, do)


def _xattn_tile(q, k, v):
    s = _dot_nt(q, k) * ((MEM_WIDTH // MEM_HEADS) ** -0.5)
    return _dot(_softmax(s), v)


def _xattn_fwd(qm, kvm, seq, n_mem, tq=512):
    t = qm.shape[0]
    tq = min(tq, seq)
    nb, nq = t // seq, seq // tq

    def body(q_ref, k_ref, v_ref, o_ref):
        o_ref[...] = _xattn_tile(q_ref[...], k_ref[...], v_ref[...]).astype(o_ref.dtype)

    return pl.pallas_call(
        body, name="xattn_fwd", grid=(nb, MEM_HEADS, nq),
        in_specs=[pl.BlockSpec((tq, LANE), lambda b, h, i: (b * nq + i, h)),
                  pl.BlockSpec((n_mem, LANE), lambda b, h, i: (b, h)),
                  pl.BlockSpec((n_mem, LANE), lambda b, h, i: (b, MEM_HEADS + h))],
        out_specs=pl.BlockSpec((tq, LANE), lambda b, h, i: (b * nq + i, h)),
        out_shape=jax.ShapeDtypeStruct((t, MEM_WIDTH), BF16),
        compiler_params=_cp("parallel", "parallel", "parallel"),
    )(qm, kvm, kvm)


def _xattn_bwd(qm, kvm, do, seq, n_mem, tq=512):
    t = qm.shape[0]
    tq = min(tq, seq)
    nb, nq = t // seq, seq // tq

    def body(q_ref, k_ref, v_ref, do_ref, dq_ref, dkv_ref, dk_acc, dv_acc):
        i = pl.program_id(2)

        @pl.when(i == 0)
        def _():
            dk_acc[...] = jnp.zeros_like(dk_acc)
            dv_acc[...] = jnp.zeros_like(dv_acc)

        _, vjp = jax.vjp(_xattn_tile, q_ref[...], k_ref[...], v_ref[...])
        dq, dk, dv = vjp(do_ref[...].astype(F32))
        dq_ref[...] = dq.astype(dq_ref.dtype)
        dk_acc[...] += dk
        dv_acc[...] += dv

        @pl.when(i == nq - 1)
        def _():
            dkv_ref[0] = dk_acc[...].astype(dkv_ref.dtype)
            dkv_ref[1] = dv_acc[...].astype(dkv_ref.dtype)

    dq, dkv = pl.pallas_call(
        body, name="xattn_bwd", grid=(nb, MEM_HEADS, nq),
        in_specs=[pl.BlockSpec((tq, LANE), lambda b, h, i: (b * nq + i, h)),
                  pl.BlockSpec((n_mem, LANE), lambda b, h, i: (b, h)),
                  pl.BlockSpec((n_mem, LANE), lambda b, h, i: (b, MEM_HEADS + h)),
                  pl.BlockSpec((tq, LANE), lambda b, h, i: (b * nq + i, h))],
        out_specs=[pl.BlockSpec((tq, LANE), lambda b, h, i: (b * nq + i, h)),
                   pl.BlockSpec((2, n_mem, LANE), lambda b, h, i: (0, b, h))],
        out_shape=[jax.ShapeDtypeStruct((t, MEM_WIDTH), BF16), jax.ShapeDtypeStruct((2, nb * n_mem, MEM_WIDTH), BF16)],
        scratch_shapes=[pltpu.VMEM((n_mem, LANE), F32)] * 2,
        compiler_params=_cp("parallel", "parallel", "arbitrary"),
    )(qm, kvm, kvm, do)
    return dq, jnp.concatenate([dkv[0], dkv[1]], axis=1)


def _loss_head(x, u, g_post, target, tm=256):
    t, d = x.shape
    tm = min(tm, t)

    def tile_loss(xv, uv, gv, tv):
        diff = _fn_res(xv, uv, gv)[0] - tv
        return 0.5 * jnp.sum(jnp.mean(diff * diff, axis=-1, keepdims=True), axis=0, keepdims=True)

    def body(x_ref, u_ref, g_ref, t_ref, l_ref, dx_ref, du_ref, dg_ref):
        @pl.when(pl.program_id(0) == 0)
        def _():
            l_ref[...] = jnp.zeros_like(l_ref)
            dg_ref[...] = jnp.zeros_like(dg_ref)

        tv = t_ref[...]
        part, vjp = jax.vjp(lambda xv, uv, gv: tile_loss(xv, uv, gv, tv), x_ref[...], u_ref[...], g_ref[...])
        dx, du, dg = vjp(jnp.ones((1, 1), F32))
        l_ref[...] += part
        dx_ref[...] = dx
        du_ref[...] = du.astype(du_ref.dtype)
        dg_ref[...] += dg

    rows = pl.BlockSpec((tm, d), lambda i: (i, 0))
    vec = pl.BlockSpec((1, d), lambda i: (0, 0))
    return pl.pallas_call(
        body, name="loss_head", grid=(t // tm,),
        in_specs=[rows, rows, vec, rows],
        out_specs=[pl.BlockSpec((8, LANE), lambda i: (0, 0)), rows, rows, vec],
        out_shape=[jax.ShapeDtypeStruct((8, LANE), F32), jax.ShapeDtypeStruct((t, d), F32),
                   jax.ShapeDtypeStruct((t, d), BF16), jax.ShapeDtypeStruct((1, d), F32)],
        compiler_params=_cp("arbitrary"),
    )(x, u, g_post, target)


def _mesh_pos():
    return lax.axis_index("x"), lax.axis_index("y"), lax.axis_index("c")


def _peer(pos, d):
    x, y, c = pos
    return ((1 - x) if d & 4 else x, (1 - y) if d & 2 else y, (1 - c) if d & 1 else c)


def _flat(pos):
    return 4 * pos[0] + 2 * pos[1] + pos[2]


def _exchange(arrays, scatter, *, name):
    n = len(arrays)
    shapes = [a.shape[1:] if scatter else a.shape for a in arrays]

    def body(*refs):
        ins, outs = refs[:n], refs[n:2 * n]
        send, recv, loc = refs[2 * n:]
        pos = _mesh_pos()
        me = _flat(pos)
        pending = []
        for i in range(n):
            own = pltpu.make_async_copy(ins[i].at[me] if scatter else ins[i], outs[i].at[me], loc.at[i])
            own.start()
            pending.append(own)
            for d in range(1, N_DEV):
                peer = _peer(pos, d)
                src = ins[i].at[_flat(peer)] if scatter else ins[i]
                out_cp = pltpu.make_async_remote_copy(
                    src_ref=src, dst_ref=outs[i].at[me], send_sem=send.at[i, d - 1], recv_sem=recv.at[i, d - 1],
                    device_id=peer, device_id_type=pl.DeviceIdType.MESH)
                out_cp.start()
                pending.append(out_cp)
        for i in range(n):
            own = pending[i * N_DEV]
            for d in range(1, N_DEV):
                peer = _peer(pos, d)
                src = ins[i].at[_flat(peer)] if scatter else ins[i]
                pending[i * N_DEV + d].wait_send()
                pltpu.make_async_remote_copy(
                    src_ref=src, dst_ref=outs[i].at[_flat(peer)], send_sem=send.at[i, d - 1], recv_sem=recv.at[i, d - 1],
                    device_id=peer, device_id_type=pl.DeviceIdType.MESH).wait_recv()
            own.wait()

    hbm = pl.BlockSpec(memory_space=pltpu.HBM)
    return pl.pallas_call(
        body, name=name,
        in_specs=[hbm] * n, out_specs=[hbm] * n,
        out_shape=[jax.ShapeDtypeStruct((N_DEV,) + tuple(s), a.dtype) for s, a in zip(shapes, arrays)],
        scratch_shapes=[pltpu.SemaphoreType.DMA((n, N_DEV - 1)), pltpu.SemaphoreType.DMA((n, N_DEV - 1)),
                        pltpu.SemaphoreType.DMA((n,))],
    )(*arrays)


_HBM = pl.BlockSpec(memory_space=pltpu.HBM)
_SEM = pl.BlockSpec(memory_space=pltpu.SEMAPHORE)
_DATAFLOW = pltpu.SideEffectType.DATAFLOW_SIDE_EFFECTING


_ALL_PEERS = tuple(range(1, N_DEV))
_SIBLING_AND_SAME_CORE = (1, 2, 4, 6)


def _remote_copies(ins, lands, send, recv, scatter, dists):
    pos = _mesh_pos()
    me = _flat(pos)
    out = []
    for i in range(len(ins)):
        for j, d in enumerate(dists):
            peer = _peer(pos, d)
            src = ins[i].at[_flat(peer)] if scatter else ins[i]
            pair = i * len(dists) + j
            sems = dict(send_sem=send.at[pair], recv_sem=recv.at[pair], device_id=peer,
                        device_id_type=pl.DeviceIdType.MESH)
            out.append((pltpu.make_async_remote_copy(src_ref=src, dst_ref=lands[i].at[me], **sems),
                        pltpu.make_async_remote_copy(src_ref=src, dst_ref=lands[i].at[_flat(peer)], **sems)))
    return out


def _exchange_start(arrays, scatter, after, *, name, dists=_ALL_PEERS):
    n = len(arrays)
    shapes = [a.shape[1:] if scatter else a.shape for a in arrays]
    lands = [pltpu.with_memory_space_constraint(lax.empty((N_DEV,) + tuple(s), a.dtype), pltpu.HBM)
             for s, a in zip(shapes, arrays)]
    srcs = [pltpu.with_memory_space_constraint(a, pltpu.HBM) for a in arrays]

    def body(*refs):
        ins, land_refs = refs[:n], refs[n:2 * n]
        send, recv, token = refs[2 * n + 1], refs[2 * n + 2], refs[-1]
        for going, _ in _remote_copies(ins, land_refs, send, recv, scatter, dists):
            going.start()
        token[...] = jnp.zeros_like(token)

    sems = pltpu.SemaphoreType.DMA((n * len(dists),))
    res = pl.pallas_call(
        body, name=name,
        out_shape=(sems, sems, *[pltpu.HBM(a.shape, a.dtype) for a in srcs + lands], jax.ShapeDtypeStruct((8, LANE), F32)),
        in_specs=[_HBM] * (2 * n) + [pl.BlockSpec(memory_space=pl.ANY)],
        out_specs=(_SEM, _SEM, *[_HBM] * (2 * n), pl.BlockSpec(memory_space=pltpu.VMEM)),
        input_output_aliases={i: 2 + i for i in range(2 * n)},
        compiler_params=pltpu.CompilerParams(has_side_effects=_DATAFLOW),
    )(*srcs, *lands, after)
    return (n, scatter, dists, res[0], res[1], list(res[2:2 + 2 * n])), res[-1]


def _exchange_wait(handle, after, own, *, name):
    n, scatter, dists, send, recv, thru = handle

    def body(*refs):
        ins, land_refs = refs[:n], refs[n:2 * n]
        for going, coming in _remote_copies(ins, land_refs, refs[2 * n], refs[2 * n + 1], scatter, dists):
            going.wait_send()
            coming.wait_recv()

    res = pl.pallas_call(
        body, name=name,
        out_shape=tuple(pltpu.HBM(a.shape, a.dtype) for a in thru),
        in_specs=[_HBM] * (2 * n) + [_SEM, _SEM] + [pl.BlockSpec(memory_space=pl.ANY)] * len(after),
        out_specs=tuple([_HBM] * (2 * n)),
        input_output_aliases={i: i for i in range(2 * n)},
        compiler_params=pltpu.CompilerParams(has_side_effects=_DATAFLOW),
    )(*thru, send, recv, *after)
    me = _flat(_mesh_pos())
    return [lax.dynamic_update_slice_in_dim(land, o[None].astype(land.dtype), me, 0) for land, o in zip(res[n:], own)]


_OTHER_CHIPS = (2, 4, 6)


def _relay_to_sibling(gathered, *, name):
    n, k = len(gathered), len(_OTHER_CHIPS)

    def body(*refs):
        ins, outs = refs[:n], refs[n:2 * n]
        send, recv = refs[2 * n:]
        pos = _mesh_pos()
        copies = []
        for i in range(n):
            for j, d in enumerate(_OTHER_CHIPS):
                cp = pltpu.make_async_remote_copy(
                    src_ref=ins[i].at[_flat(_peer(pos, d))], dst_ref=outs[i].at[j],
                    send_sem=send.at[i * k + j], recv_sem=recv.at[i * k + j],
                    device_id=_peer(pos, 1), device_id_type=pl.DeviceIdType.MESH)
                cp.start()
                copies.append(cp)
        for cp in copies:
            cp.wait()

    return pl.pallas_call(
        body, name=name, in_specs=[_HBM] * n, out_specs=[_HBM] * n,
        out_shape=[jax.ShapeDtypeStruct((k,) + g.shape[1:], g.dtype) for g in gathered],
        scratch_shapes=[pltpu.SemaphoreType.DMA((n * k,)), pltpu.SemaphoreType.DMA((n * k,))],
    )(*gathered)


def _adamw(parts, w, m, v, *, name, tr=128, after=None):
    r, c = w.shape
    align = 8 * 4 // parts.dtype.itemsize
    row_tiles = [d for d in range(align, min(tr, r) + 1, align) if r % d == 0]
    tr, tc = (max(row_tiles), c) if row_tiles else (r, LANE)
    assert c % tc == 0
    n_after = 0 if after is None else 1

    def body(p_ref, w_ref, m_ref, v_ref, *rest):
        g_ref, d_ref, nm_ref, nv_ref = rest[n_after:]
        g = p_ref[0].astype(F32)
        for j in range(1, N_DEV):
            g = g + p_ref[j].astype(F32)
        m2 = ADAM_B1 * m_ref[...] + (1.0 - ADAM_B1) * g
        v2 = ADAM_B2 * v_ref[...] + (1.0 - ADAM_B2) * (g * g)
        m_hat = m2 / (1.0 - ADAM_B1 ** ADAM_STEP)
        v_hat = v2 / (1.0 - ADAM_B2 ** ADAM_STEP)
        g_ref[...] = g
        d_ref[...] = -ADAM_LR * (m_hat / (jnp.sqrt(v_hat) + ADAM_EPS) + ADAM_WD * w_ref[...])
        nm_ref[...] = m2
        nv_ref[...] = v2

    spec = pl.BlockSpec((tr, tc), lambda i, j: (i, j))
    return pl.pallas_call(
        body, name=name, grid=(r // tr, c // tc),
        in_specs=[pl.BlockSpec((N_DEV, tr, tc), lambda i, j: (0, i, j)), spec, spec, spec]
        + [pl.BlockSpec(memory_space=pl.ANY)] * n_after,
        out_specs=[spec] * 4, out_shape=[jax.ShapeDtypeStruct((r, c), F32)] * 4,
        compiler_params=_cp("parallel", "parallel"),
    )(parts, w, m, v, *([] if after is None else [after]))


def _cols_to_full(g):
    return jnp.transpose(g, (1, 0, 2)).reshape(g.shape[1], N_DEV * g.shape[2])


def _full_to_cols(w):
    r, c = w.shape
    return jnp.transpose(w.reshape(r, N_DEV, c // N_DEV), (1, 0, 2))


def _cut(a, lo, hi, axis):
    return lax.slice_in_dim(a, lo, hi, axis=axis)


def _pad_to(a, size, axis):
    pads = [(0, 0)] * a.ndim
    pads[axis] = (0, size - a.shape[axis])
    return jnp.pad(a, pads)


def _pad_lora(w, axis=1):
    return jnp.concatenate([
        _pad_to(_cut(w, 0, LORA_W, axis), 128, axis), _pad_to(_cut(w, LORA_W, LORA_W + LORA_A, axis), 128, axis),
        _pad_to(_cut(w, LORA_W + LORA_A, w.shape[axis], axis), 256, axis)], axis=axis)


def _unpad_lora(wp, axis=1):
    return jnp.concatenate([_cut(wp, 0, LORA_W, axis), _cut(wp, 128, 128 + LORA_A, axis),
                            _cut(wp, 256, 256 + LORA_G, axis)], axis=axis)


def _permute_in(w, axis):
    rk = 3 * D
    lo = rk + LORA_W + LORA_A + LORA_G
    return jnp.concatenate([_cut(w, 0, rk, axis), _cut(w, lo, w.shape[axis], axis), _pad_lora(_cut(w, rk, lo, axis), axis)],
                           axis=axis)


def _unpermute_in(wp, axis):
    return jnp.concatenate([_cut(wp, 0, 3 * D, axis), _unpad_lora(_cut(wp, C_LORA, P_WIDTH, axis), axis),
                            _cut(wp, 3 * D, C_LORA, axis)], axis=axis)


def _rel_index():
    dist = jnp.arange(CHUNK)[:, None] - jnp.arange(BAND)[None, :] + LEFT
    return (jnp.minimum(dist, REL_CLIP) + (CHUNK - 1)).reshape(-1)


def _local_step(x, mem, target, wt, seq, n_mem, comm):
    t = x.shape[0]
    row = lambda a: a.reshape(1, -1).astype(F32)
    g_pre_mix, g_post_mix = row(wt["g_pre_mix"]), row(wt["g_post_mix"])
    g_pre_cross, g_post_cross, g_mem = row(wt["g_pre_cross"]), row(wt["g_post_cross"]), row(wt["g_mem"])
    g_pre_ffn, g_post_ffn = row(wt["g_pre_ffn"]), row(wt["g_post_ffn"])
    mix = row(wt["shift_mix"])
    mix_rkv, mix_lora = mix[:, :3 * D], _pad_lora(mix[:, 3 * D:])
    decay_base, iclr_base = row(wt["decay_base"]), row(wt["iclr_base"])
    kns, kis = row(wt["key_norm_scale"]), row(wt["key_iclr_scale"])
    lnx_w, lnx_b, bonus = row(wt["lnx_w"]), row(wt["lnx_b"]), row(wt["bonus_scale"])
    e_dh = (jnp.arange(D)[:, None] // HEAD == jnp.arange(N_HEADS)[None, :]).astype(F32)
    e_hd = e_dh.T
    onehot = (jnp.arange(REL_TABLE)[:, None] == _rel_index()[None, :]).astype(BF16)

    begun = comm.begun
    (h1,) = _rowwise(_fn_pre, [_win(x)], [g_pre_mix], [(D, BF16)], name="pre_mix", tm=512, after=begun)
    (mn,) = _rowwise(_fn_pre, [_win(mem)], [g_mem], [(D, BF16)], name="pre_mem", tm=512, after=begun)
    bias = _mm(wt["rel_bias"].astype(F32), onehot, name="mm_bias", split_a=3, after=begun).reshape(N_HEADS, CHUNK, BAND)
    wt = {**wt, **comm.first_weights([h1, mn, bias])}
    w_in = wt["w_in_p"]
    d_up = jnp.pad(wt["decay_up"].astype(F32), ((0, 128 - LORA_W), (0, 0)))
    i_up = jnp.pad(wt["iclr_up"].astype(F32), ((0, 128 - LORA_A), (0, 0)))
    g_up = jnp.pad(wt["gate_up"].astype(F32), ((0, 256 - LORA_G), (0, 0)))
    proj = _mm(h1, w_in, tb=True, name="mm_in", after=comm.first_token)
    z_rkv = _shift_fwd(proj, 0, 3 * D, mix_rkv, seq, name="shift_rkv")
    z_lora = _shift_fwd(proj, C_LORA, 512, mix_lora, seq, name="shift_lora")
    prep_rows = [_win(z_rkv, D, D), _win(z_lora, 0, 128), _win(z_lora, 128, 128), _win(z_lora, 256, 256)]
    prep_params = [decay_base, d_up, iclr_base, i_up, g_up, kns, kis, e_hd, e_dh]
    lw, k2, kk, a, g = _rowwise(_fn_prep, prep_rows, prep_params, [(D, F32)] * 5, name="rwkv_prep", tm=256)
    y, states, invs = _wkv_fwd(z_rkv, lw, k2, kk, a, seq)
    post_rows = [_win(y), _win(z_rkv, 0, D), _win(k2), _win(z_rkv, 2 * D, D), _win(g)]
    post_params = [lnx_w, lnx_b, bonus, e_hd, e_dh]
    (y_a,) = _rowwise(_fn_post, post_rows, post_params, [(D, BF16)], name="rwkv_post", tm=256)
    y_b = _attn_fwd(proj, bias, seq)
    wt = {**wt, **comm.late_weights(y_b)}
    ya_p = _mm(y_a, wt["w_branch_a"], name="mm_a")
    yb_p = _mm(y_b, wt["w_branch_b"], name="mm_b")
    mix_rows = [_win(proj, C_GA, D), _win(proj, C_GA + D, D), _win(ya_p), _win(yb_p)]
    (mixed,) = _rowwise(_fn_mix, mix_rows, [], [(D, BF16)], name="gate_mix", tm=512)
    mo = _mm(mixed, wt["w_out"], name="mm_out")
    x1, h2 = _rowwise(_fn_res_pre, [_win(x), _win(mo)], [g_post_mix, g_pre_cross], [(D, F32), (D, BF16)],
                      name="res_mix", tm=512)
    qm = _mm(h2, wt["w_q_mem"], name="mm_q")
    kvm = _mm(mn, wt["w_kv_mem"], name="mm_kv")
    om = _xattn_fwd(qm, kvm, seq, n_mem)
    co = _mm(om, wt["w_o_mem"], name="mm_o")
    x2, h3 = _rowwise(_fn_res_pre, [_win(x1), _win(co)], [g_post_cross, g_pre_ffn], [(D, F32), (D, BF16)],
                      name="res_cross", tm=512)
    gu = _mm(h3, wt["w_ffn_in"], tb=True, name="mm_ffn_in", out_dtype=BF16)
    (act,) = _rowwise(_fn_swiglu, [_win(gu, 0, FFN), _win(gu, FFN, FFN)], [], [(FFN, BF16)], name="swiglu", tm=256)
    ff = _mm(act, wt["w_ffn_out"], name="mm_ffn_out")

    gw = {}
    loss, dx2, dff, gw["g_post_ffn"] = _loss_head(x2, ff, g_post_ffn, target)
    dact = _mm(dff, wt["w_ffn_out"], tb=True, name="mm_ffn_out_dx", out_dtype=BF16)
    gw["w_ffn_out"] = _mm(act, dff, ta=True, name="mm_ffn_out_dw", out_dtype=BF16)
    (dgu,), _ = _rowwise_bwd(_fn_swiglu, [_win(gu, 0, FFN), _win(gu, FFN, FFN)], [], 0, [[dact]],
                             name="swiglu_bwd", tm=256, row_grad=[BF16, BF16], packed=True)
    dh3 = _mm(dgu, wt["w_ffn_in"], name="mm_ffn_in_dx", out_dtype=BF16)
    gw["w_ffn_in"] = _mm(dgu, h3, ta=True, name="mm_ffn_in_dw", out_dtype=BF16)
    (dx1, dco), (gw["g_post_cross"], gw["g_pre_ffn"]) = _rowwise_bwd(
        _fn_res_pre, [_win(x1), _win(co)], [g_post_cross, g_pre_ffn], 0, [[dx2], [dh3]],
        name="res_cross_bwd", tm=256, row_grad=[F32, BF16])
    dom = _mm(dco, wt["w_o_mem"], tb=True, name="mm_o_dx", out_dtype=BF16)
    gw["w_o_mem"] = _mm(om, dco, ta=True, name="mm_o_dw", out_dtype=BF16)
    dqm, dkvm = _xattn_bwd(qm, kvm, dom, seq, n_mem)
    dh2 = _mm(dqm, wt["w_q_mem"], tb=True, name="mm_q_dx", out_dtype=BF16)
    gw["w_q_mem"] = _mm(h2, dqm, ta=True, name="mm_q_dw", out_dtype=BF16)
    dmn = _mm(dkvm, wt["w_kv_mem"], tb=True, name="mm_kv_dx", out_dtype=BF16)
    gw["w_kv_mem"] = _mm(mn, dkvm, ta=True, name="mm_kv_dw", out_dtype=BF16)
    _, (gw["g_mem"],) = _rowwise_bwd(_fn_pre, [_win(mem)], [g_mem], 0, [[dmn]], name="pre_mem_bwd", tm=256,
                                     row_grad=[None])
    (dx0, dmo), (gw["g_post_mix"], gw["g_pre_cross"]) = _rowwise_bwd(
        _fn_res_pre, [_win(x), _win(mo)], [g_post_mix, g_pre_cross], 0, [[dx1], [dh2]],
        name="res_mix_bwd", tm=256, row_grad=[F32, BF16])
    dmixed = _mm(dmo, wt["w_out"], tb=True, name="mm_out_dx", out_dtype=BF16)
    gw["w_out"] = _mm(mixed, dmo, ta=True, name="mm_out_dw", out_dtype=BF16)
    (dzga, dzgb, dya_p, dyb_p), _ = _rowwise_bwd(_fn_mix, mix_rows, [], 0, [[dmixed]], name="gate_mix_bwd", tm=256,
                                                 row_grad=[BF16] * 4)
    gw["w_branch_a"] = _mm(y_a, dya_p, ta=True, name="mm_a_dw", out_dtype=BF16)
    gw["w_branch_b"] = _mm(y_b, dyb_p, ta=True, name="mm_b_dw", out_dtype=BF16)
    token = comm.send_early(gw)
    dy_a = _mm(dya_p, wt["w_branch_a"], tb=True, name="mm_a_dx", out_dtype=BF16, after=token)
    dy_b = _mm(dyb_p, wt["w_branch_b"], tb=True, name="mm_b_dx", out_dtype=BF16, after=token)
    dq, dk, dv, dbias = _attn_bwd(proj, bias, dy_b, seq)
    gw["rel_bias"] = _mm(dbias.reshape(N_HEADS, CHUNK * BAND), onehot, tb=True, name="mm_bias_dw", split_a=2)
    (dy, dr_p, dk2_p, dv_p, dg), (gw["lnx_w"], gw["lnx_b"], gw["bonus_scale"]) = _rowwise_bwd(
        _fn_post, post_rows, post_params, 2, [[dy_a]], name="rwkv_post_bwd", tm=128, row_grad=[F32] * 5)
    dr_s, dlw, dk2_s, dv_s, dkk, da = _wkv_bwd(z_rkv, lw, k2, kk, a, states, invs, dy, seq)
    (dzk, dzw, dza, dzg), pg = _rowwise_bwd(
        _fn_prep, prep_rows, prep_params, 2, [[dlw], [dk2_p, dk2_s], [dkk], [da], [dg]],
        name="rwkv_prep_bwd", tm=128, row_grad=[F32] * 4)
    gw["decay_base"], gd_up, gw["iclr_base"], gi_up, gg_up, gw["key_norm_scale"], gw["key_iclr_scale"] = pg
    gw["decay_up"], gw["iclr_up"], gw["gate_up"] = gd_up[:LORA_W], gi_up[:LORA_A], gg_up[:LORA_G]
    dp_r, gmix_r = _shift_bwd(proj, 0, D, mix_rkv[:, :D], [dr_p, dr_s], seq, name="shift_r_bwd")
    dp_k, gmix_k = _shift_bwd(proj, D, D, mix_rkv[:, D:2 * D], [dzk], seq, name="shift_k_bwd")
    dp_v, gmix_v = _shift_bwd(proj, 2 * D, D, mix_rkv[:, 2 * D:], [dv_p, dv_s], seq, name="shift_v_bwd")
    dp_lora, gmix_lora = _shift_bwd(proj, C_LORA, 512, mix_lora, [jnp.concatenate([dzw, dza, dzg], axis=1)], seq,
                                    name="shift_lora_bwd")
    gw["shift_mix"] = jnp.concatenate([gmix_r, gmix_k, gmix_v, _unpad_lora(gmix_lora)], axis=1)
    dproj = [dp_r, dp_k, dp_v, dq, dk, dv, dzga, dzgb, dp_lora]
    gw["w_in_p"] = _mm_cat_tn(dproj, h1, name="mm_in_dw", after=gw["rel_bias"])
    token = comm.send_late(gw)
    dh1 = _mm_cat_nn(dproj, w_in, name="mm_in_dx", after=token)
    (grad_x,), (gw["g_pre_mix"],) = _rowwise_bwd(_fn_pre, [_win(x)], [g_pre_mix], 0, [[dh1]], name="pre_mix_bwd",
                                                 tm=256, row_grad=[F32], add_to={0: dx0})
    return loss, grad_x, gw


_COL_SHARDED = ("w_in", "decay_up", "iclr_up", "gate_up", "w_o_mem", "w_ffn_in")
_ROW_SHARDED = ("w_branch_a", "w_branch_b", "w_out", "w_q_mem", "w_kv_mem", "w_ffn_out")
_TRANSPOSED = ("w_in", "w_ffn_in")
_FIRST = ("w_in", "decay_up", "iclr_up", "gate_up")
_REST = ("w_o_mem", "w_ffn_in", "w_branch_a", "w_branch_b", "w_out", "w_q_mem", "w_kv_mem", "w_ffn_out")
_REPLICATED = ("g_pre_mix", "g_post_mix", "shift_mix", "decay_base", "iclr_base", "key_norm_scale", "key_iclr_scale",
               "bonus_scale", "lnx_w", "lnx_b", "rel_bias", "g_pre_cross", "g_post_cross", "g_mem", "g_pre_ffn",
               "g_post_ffn")
_WEIGHTS = ("g_pre_mix", "g_post_mix", "w_in", "shift_mix", "decay_base", "decay_up", "iclr_base", "iclr_up", "gate_up",
            "key_norm_scale", "key_iclr_scale", "bonus_scale", "lnx_w", "lnx_b", "rel_bias", "w_branch_a", "w_branch_b",
            "w_out", "g_pre_cross", "g_post_cross", "g_mem", "w_q_mem", "w_kv_mem", "w_o_mem", "g_pre_ffn", "g_post_ffn",
            "w_ffn_in", "w_ffn_out")
_PACK_ROWS = 8 * ((sum({"shift_mix": 3360, "bonus_scale": 1024, "rel_bias": 3072}.get(n, D) for n in _REPLICATED)
                   + 1 + 8 * LANE - 1) // (8 * LANE))


def _pack(vals):
    flat = jnp.concatenate([v.reshape(-1).astype(F32) for v in vals])
    return jnp.pad(flat, (0, _PACK_ROWS * LANE - flat.shape[0])).reshape(_PACK_ROWS, LANE)


def _unpack(packed, shapes):
    flat, out, pos = packed.reshape(-1), [], 0
    for s in shapes:
        n = math.prod(s)
        out.append(flat[pos:pos + n].reshape(s))
        pos += n
    return out


def _step(args, seq, n_mem):
    names = ("x", "mem") + _WEIGHTS + ("loss_target",) + tuple("m_" + n for n in _WEIGHTS) + tuple("v_" + n for n in _WEIGHTS)
    given = dict(zip(names, args))
    nb = given["x"].shape[0]
    x = given["x"].reshape(nb * seq, D)
    mem = given["mem"].reshape(nb * n_mem, D)
    target = given["loss_target"].reshape(nb * seq, D)
    def local(name, prefix=""):
        a = given[prefix + name][0]
        return a.T if name in _TRANSPOSED else a

    shard = {n: local(n) for n in _COL_SHARDED + _ROW_SHARDED}
    stacked = _ROW_SHARDED + _TRANSPOSED
    out = {}

    def wire(name):
        return shard[name].astype(BF16)

    def full(name, g):
        return g.reshape(-1, g.shape[-1]) if name in stacked else _cols_to_full(g)

    def as_lanes(a):
        return a.reshape(a.shape[:-2] + (a.shape[-2] * a.shape[-1] // LANE, LANE))

    def blocks_of(name, g):
        if name in _TRANSPOSED:
            return as_lanes(g.reshape((N_DEV,) + shard[name].shape)).astype(BF16)
        return (g.reshape((N_DEV,) + shard[name].shape) if name in stacked else _full_to_cols(g)).astype(BF16)

    def update(names, landed, after=None):
        done = []
        for n, parts in zip(names, landed):
            if n in _TRANSPOSED:
                res = _adamw(parts, as_lanes(shard[n]), as_lanes(local(n, "m_")), as_lanes(local(n, "v_")),
                             name="adamw_" + n, after=after, tr=1024)
                res = [r.reshape(shard[n].shape).T for r in res]
            else:
                res = _adamw(parts, shard[n], local(n, "m_"), local(n, "v_"), name="adamw_" + n, after=after)
            for kind, r in zip(("grad_", "delta_", "new_m_", "new_v_"), res):
                out[kind + n] = r[None]
            done.append(res[0])
        return done


    class Exchanges:
        def __init__(self):
            srcs = [wire(n) for n in _FIRST]
            self.first, self.begun = _exchange_start(srcs, False, srcs[0], name="gather_first_start",
                                                     dists=_SIBLING_AND_SAME_CORE)

        def first_weights(self, after):
            got = _exchange_wait(self.first, after, [wire(n) for n in _FIRST], name="gather_first_wait")
            relayed = _relay_to_sibling(got, name="gather_first_relay")
            pos = _mesh_pos()
            for j, d in enumerate(_OTHER_CHIPS):
                slot = _flat(_peer(pos, d | 1))
                got = [lax.dynamic_update_slice_in_dim(g, r[j][None], slot, 0) for g, r in zip(got, relayed)]
            self.rest, self.first_token = _exchange_start(
                [wire(n) for n in _REST], False, got[0], name="gather_rest_start")
            first = {n: full(n, g) for n, g in zip(_FIRST, got)}
            first["w_in_p"] = _permute_in(first.pop("w_in"), 0)
            return first

        def late_weights(self, after):
            got = _exchange_wait(self.rest, [after], [wire(n) for n in _REST], name="gather_rest_wait")
            return {n: full(n, g) for n, g in zip(_REST, got)}

        def send_early(self, gw):
            self.early_blocks = [blocks_of(n, gw[n]) for n in _REST]
            self.early, token = _exchange_start(self.early_blocks, True, self.early_blocks[-1], name="scatter_rest_start")
            return token

        def send_late(self, gw):
            me = _flat(_mesh_pos())
            own = [lax.dynamic_index_in_dim(b, me, 0, keepdims=False) for b in self.early_blocks]
            landed = _exchange_wait(self.early, [gw["w_in_p"]], own, name="scatter_rest_wait")
            grads = {**gw, "w_in": _unpermute_in(gw["w_in_p"], 0)}
            self.late_blocks = [blocks_of(n, grads[n]) for n in _FIRST]
            self.late, token = _exchange_start(self.late_blocks, True, landed[0], name="scatter_first_start")
            self.updated = update(_REST, landed, after=token)
            return token

        def finish(self, after):
            me = _flat(_mesh_pos())
            own = [lax.dynamic_index_in_dim(b, me, 0, keepdims=False) for b in self.late_blocks]
            update(_FIRST, _exchange_wait(self.late, [*after, *self.updated], own, name="scatter_first_wait"))

    comm = Exchanges()
    wt = {n: given[n][0] for n in _REPLICATED}
    loss_tile, grad_x, gw = _local_step(x, mem, target, wt, seq, n_mem, comm)
    rep_shapes = [given[n].shape for n in _REPLICATED]
    packed, _ = lax.optimization_barrier((_pack([gw[n] for n in _REPLICATED] + [loss_tile[0, 0]]), tuple(comm.updated)))
    small = _exchange([packed], False, name="gather_small")[0]
    zero = jnp.zeros((), F32)
    res = _adamw(small, *[_pack([given[p + n] for n in _REPLICATED] + [zero]) for p in ("", "m_", "v_")],
                 name="adamw_small", tr=_PACK_ROWS)
    for kind, r in zip(("grad_", "delta_", "new_m_", "new_v_"), res):
        for n, val in zip(_REPLICATED, _unpack(r, rep_shapes)):
            out[kind + n] = val
    loss = res[0].reshape(-1)[sum(math.prod(s) for s in rep_shapes)]
    comm.finish([grad_x, res[0]])
    grad_x = grad_x.reshape(nb, seq, D)
    return (loss, grad_x, *[out[k + n] for k in ("grad_", "delta_", "new_m_", "new_v_") for n in _WEIGHTS])


def kernel(x, mem, g_pre_mix, g_post_mix, w_in, shift_mix, decay_base, decay_up, iclr_base, iclr_up, gate_up, key_norm_scale, key_iclr_scale, bonus_scale, lnx_w, lnx_b, rel_bias, w_branch_a, w_branch_b, w_out, g_pre_cross, g_post_cross, g_mem, w_q_mem, w_kv_mem, w_o_mem, g_pre_ffn, g_post_ffn, w_ffn_in, w_ffn_out, loss_target, m_g_pre_mix, m_g_post_mix, m_w_in, m_shift_mix, m_decay_base, m_decay_up, m_iclr_base, m_iclr_up, m_gate_up, m_key_norm_scale, m_key_iclr_scale, m_bonus_scale, m_lnx_w, m_lnx_b, m_rel_bias, m_w_branch_a, m_w_branch_b, m_w_out, m_g_pre_cross, m_g_post_cross, m_g_mem, m_w_q_mem, m_w_kv_mem, m_w_o_mem, m_g_pre_ffn, m_g_post_ffn, m_w_ffn_in, m_w_ffn_out, v_g_pre_mix, v_g_post_mix, v_w_in, v_shift_mix, v_decay_base, v_decay_up, v_iclr_base, v_iclr_up, v_gate_up, v_key_norm_scale, v_key_iclr_scale, v_bonus_scale, v_lnx_w, v_lnx_b, v_rel_bias, v_w_branch_a, v_w_branch_b, v_w_out, v_g_pre_cross, v_g_post_cross, v_g_mem, v_w_q_mem, v_w_kv_mem, v_w_o_mem, v_g_pre_ffn, v_g_post_ffn, v_w_ffn_in, v_w_ffn_out):
    args = (x, mem, g_pre_mix, g_post_mix, w_in, shift_mix, decay_base, decay_up, iclr_base, iclr_up, gate_up, key_norm_scale, key_iclr_scale, bonus_scale, lnx_w, lnx_b, rel_bias, w_branch_a, w_branch_b, w_out, g_pre_cross, g_post_cross, g_mem, w_q_mem, w_kv_mem, w_o_mem, g_pre_ffn, g_post_ffn, w_ffn_in, w_ffn_out, loss_target, m_g_pre_mix, m_g_post_mix, m_w_in, m_shift_mix, m_decay_base, m_decay_up, m_iclr_base, m_iclr_up, m_gate_up, m_key_norm_scale, m_key_iclr_scale, m_bonus_scale, m_lnx_w, m_lnx_b, m_rel_bias, m_w_branch_a, m_w_branch_b, m_w_out, m_g_pre_cross, m_g_post_cross, m_g_mem, m_w_q_mem, m_w_kv_mem, m_w_o_mem, m_g_pre_ffn, m_g_post_ffn, m_w_ffn_in, m_w_ffn_out, v_g_pre_mix, v_g_post_mix, v_w_in, v_shift_mix, v_decay_base, v_decay_up, v_iclr_base, v_iclr_up, v_gate_up, v_key_norm_scale, v_key_iclr_scale, v_bonus_scale, v_lnx_w, v_lnx_b, v_rel_bias, v_w_branch_a, v_w_branch_b, v_w_out, v_g_pre_cross, v_g_post_cross, v_g_mem, v_w_q_mem, v_w_kv_mem, v_w_o_mem, v_g_pre_ffn, v_g_post_ffn, v_w_ffn_in, v_w_ffn_out)
    return _step(args, x.shape[1], mem.shape[1])
```

```python
import functools
import math

import jax
import jax.numpy as jnp
from jax import lax
from jax.experimental import pallas as pl
from jax.experimental.pallas import tpu as pltpu

F32 = jnp.float32
BF16 = jnp.bfloat16

N_DEV = 8
D = 1024
HEAD = 64
N_HEADS = D // HEAD
LANE = 128
N_PAIRS = D // LANE
CHUNK = 64
LEFT = 8 * CHUNK
BAND = LEFT + CHUNK
REL_CLIP = 128
REL_TABLE = CHUNK + REL_CLIP
MEM_WIDTH = D // 2
MEM_HEADS = 4
FFN = 2816
LORA_W, LORA_A, LORA_G = 64, 64, 160
P_WIDTH = 3 * D + 3 * D + 2 * D + 128 + 128 + 256
C_Q, C_GA, C_LORA = 3 * D, 6 * D, 8 * D
NORM_EPS = 1e-6
GROUP_NORM_EPS = 64e-5
MASK_VALUE = -1e30
ADAM_LR, ADAM_B1, ADAM_B2, ADAM_EPS, ADAM_WD, ADAM_STEP = 0.001, 0.9, 0.999, 1e-08, 0.01, 10
VMEM_LIMIT = 56 * 1024 * 1024


def _cp(*sem):
    return pltpu.CompilerParams(dimension_semantics=sem, vmem_limit_bytes=VMEM_LIMIT)


_NN, _NT, _TN = ((1,), (0,)), ((1,), (1,)), ((0,), (0,))


def _dot_raw(a, b, dims):
    return lax.dot_general(a.astype(BF16), b.astype(BF16), (dims, ((), ())), preferred_element_type=F32)


@functools.partial(jax.custom_vjp, nondiff_argnums=(2,))
def _dot_dims(a, b, dims):
    return _dot_raw(a, b, dims)


def _dot_dims_fwd(a, b, dims):
    return _dot_raw(a, b, dims), (a, b)


def _dot_dims_bwd(dims, res, g):
    a, b = res
    if dims == _NN:
        da, db = _dot_raw(g, b, _NT), _dot_raw(a, g, _TN)
    elif dims == _NT:
        da, db = _dot_raw(g, b, _NN), _dot_raw(g, a, _TN)
    else:
        da, db = _dot_raw(b, g, _NT), _dot_raw(a, g, _NN)
    return da.astype(a.dtype), db.astype(b.dtype)


_dot_dims.defvjp(_dot_dims_fwd, _dot_dims_bwd)


def _dot(a, b, dims=_NN):
    return _dot_dims(a, b, dims)


def _dot_nt(a, b):
    return _dot_dims(a, b, _NT)


def _dot_tn(a, b):
    return _dot_dims(a, b, _TN)


def _split(x, terms):
    parts, rest = [], x.astype(F32)
    for _ in range(terms):
        p = rest.astype(BF16)
        parts.append(p)
        rest = rest - p.astype(F32)
    return parts


def _dot_split_a(a, b, terms=2):
    out = None
    for p in _split(a, terms):
        t = _dot(p, b)
        out = t if out is None else out + t
    return out


def _dot_split_b(a, b, terms=3):
    out = None
    for p in _split(b, terms):
        t = _dot(a, p)
        out = t if out is None else out + t
    return out


def _dot_hi(a, b, dims=_NN):
    ah, al = _split(a, 2)
    bh, bl = _split(b, 2)
    return _dot(ah, bh, dims) + (_dot(ah, bl, dims) + _dot(al, bh, dims))


MM_VMEM_BUDGET = 30 * 1024 * 1024
MM_HBM_BPS = 3.2e12
MM_MXU_FPS = 8.5e14
MM_STEP_S = 0.35e-6


def _divisors(n, align, cap):
    out = [d for d in range(align, min(n, cap) + 1, align) if n % d == 0]
    return out or [n]


def _mm_tiles(m, n, k, ea, eb, eo, ta):
    best = None
    for tm in _divisors(m, LANE if ta else 8, 2048):
        for tn in _divisors(n, LANE, 2048):
            for tk in _divisors(k, LANE, 2048):
                nk = k // tk
                vmem = 2 * (tm * tk * ea + tk * tn * eb + tm * tn * eo) + (tm * tn * 4 if nk > 1 else 0)
                if vmem > MM_VMEM_BUDGET:
                    continue
                dma = (tm * tk * ea if (nk > 1 or n // tn == 1) else tm * tk * ea * tn / n) + tk * tn * eb + tm * tn * eo / nk
                step = max(2.0 * tm * tn * tk / MM_MXU_FPS, dma / MM_HBM_BPS) + MM_STEP_S
                cost = (m // tm) * (n // tn) * nk * step
                if best is None or cost < best[0]:
                    best = (cost, tm, tn, tk)
    return best[1:]


def _mm(a, b, *, name, ta=False, tb=False, out_dtype=F32, tm=None, tn=None, tk=None, split_a=1, after=None):
    m, k = (a.shape[1], a.shape[0]) if ta else a.shape
    n, kb = (b.shape[0], b.shape[1]) if tb else (b.shape[1], b.shape[0])
    assert k == kb, (a.shape, b.shape, ta, tb)
    if tm is None:
        tm, tn, tk = _mm_tiles(m, n, k, a.dtype.itemsize, b.dtype.itemsize, jnp.dtype(out_dtype).itemsize, ta)
    assert m % tm == 0 and n % tn == 0 and k % tk == 0, (m, n, k, tm, tn, tk)
    nk = k // tk
    dims = ((0 if ta else 1,), (1 if tb else 0,))

    n_after = 0 if after is None else 1

    def body(a_ref, b_ref, *rest):
        o_ref, scratch = rest[n_after], rest[n_after + 1:]
        prod = None
        for p in _split(a_ref[...], split_a) if split_a > 1 else [a_ref[...]]:
            t = _dot_raw(p, b_ref[...], dims)
            prod = t if prod is None else prod + t
        if nk == 1:
            o_ref[...] = prod.astype(o_ref.dtype)
            return
        acc_ref, kk = scratch[0], pl.program_id(2)

        @pl.when(kk == 0)
        def _():
            acc_ref[...] = prod

        @pl.when(kk > 0)
        def _():
            acc_ref[...] += prod

        @pl.when(kk == nk - 1)
        def _():
            o_ref[...] = acc_ref[...].astype(o_ref.dtype)

    a_spec = pl.BlockSpec((tk, tm), lambda i, j, q: (q, i)) if ta else pl.BlockSpec((tm, tk), lambda i, j, q: (i, q))
    b_spec = pl.BlockSpec((tn, tk), lambda i, j, q: (j, q)) if tb else pl.BlockSpec((tk, tn), lambda i, j, q: (q, j))
    return pl.pallas_call(
        body, name=name, grid=(m // tm, n // tn, nk),
        in_specs=[a_spec, b_spec] + [pl.BlockSpec(memory_space=pl.ANY)] * n_after,
        out_specs=pl.BlockSpec((tm, tn), lambda i, j, q: (i, j)),
        out_shape=jax.ShapeDtypeStruct((m, n), out_dtype),
        scratch_shapes=[pltpu.VMEM((tm, tn), F32)] if nk > 1 else [],
        compiler_params=_cp("parallel", "parallel", "arbitrary"),
    )(a, b, *([] if after is None else [after]))


def _piece_steps(pieces, tile):
    counts = [p.shape[1] // tile for p in pieces]
    assert all(p.shape[1] % tile == 0 for p in pieces)
    return [(sum(counts[:i]), c) for i, c in enumerate(counts)], sum(counts)


def _mm_cat_nn(pieces, w, *, name, after=None, tm=2048, tk=256):
    t, n = pieces[0].shape[0], w.shape[1]
    tm = min(tm, t)
    spans, nk = _piece_steps(pieces, tk)
    npc = len(pieces)
    n_after = 0 if after is None else 1

    def body(*refs):
        w_ref, o_ref, acc_ref = refs[npc], refs[npc + 1 + n_after], refs[npc + 2 + n_after]
        q = pl.program_id(1)

        @pl.when(q == 0)
        def _():
            acc_ref[...] = jnp.zeros_like(acc_ref)

        for p_ref, (first, count) in zip(refs[:npc], spans):
            @pl.when(jnp.logical_and(q >= first, q < first + count))
            def _(p_ref=p_ref):
                acc_ref[...] += _dot_raw(p_ref[...], w_ref[...], _NN)

        @pl.when(q == nk - 1)
        def _():
            o_ref[...] = acc_ref[...].astype(o_ref.dtype)

    def piece_spec(first, count):
        return pl.BlockSpec((tm, tk), lambda i, q: (i, jnp.clip(q - first, 0, count - 1)))

    return pl.pallas_call(
        body, name=name, grid=(t // tm, nk),
        in_specs=[piece_spec(*s) for s in spans] + [pl.BlockSpec((tk, n), lambda i, q: (q, 0))]
        + [pl.BlockSpec(memory_space=pl.ANY)] * n_after,
        out_specs=pl.BlockSpec((tm, n), lambda i, q: (i, 0)),
        out_shape=jax.ShapeDtypeStruct((t, n), BF16),
        scratch_shapes=[pltpu.VMEM((tm, n), F32)],
        compiler_params=_cp("parallel", "arbitrary"),
    )(*pieces, w, *([] if after is None else [after]))


def _mm_cat_tn(pieces, a, *, name, after=None, tk=1024, tn=512):
    t, m = a.shape
    tk = min(tk, t)
    spans, nj = _piece_steps(pieces, tn)
    npc, nk = len(pieces), t // tk
    n_after = 0 if after is None else 1

    def body(a_ref, *refs):
        o_ref, acc_ref = refs[npc + n_after], refs[npc + 1 + n_after]
        j, q = pl.program_id(0), pl.program_id(1)

        @pl.when(q == 0)
        def _():
            acc_ref[...] = jnp.zeros_like(acc_ref)

        for p_ref, (first, count) in zip(refs[:npc], spans):
            @pl.when(jnp.logical_and(j >= first, j < first + count))
            def _(p_ref=p_ref):
                acc_ref[...] += _dot_raw(p_ref[...], a_ref[...], _TN)

        @pl.when(q == nk - 1)
        def _():
            o_ref[...] = acc_ref[...].astype(o_ref.dtype)

    def piece_spec(first, count):
        def index(j, q):
            mine = jnp.logical_and(j >= first, j < first + count)
            return jnp.where(mine, q, 0), jnp.clip(j - first, 0, count - 1)
        return pl.BlockSpec((tk, tn), index)

    return pl.pallas_call(
        body, name=name, grid=(nj, nk),
        in_specs=[pl.BlockSpec((tk, m), lambda j, q: (q, 0))] + [piece_spec(*s) for s in spans]
        + [pl.BlockSpec(memory_space=pl.ANY)] * n_after,
        out_specs=pl.BlockSpec((tn, m), lambda j, q: (j, 0)),
        out_shape=jax.ShapeDtypeStruct((nj * tn, m), BF16),
        scratch_shapes=[pltpu.VMEM((tn, m), F32)],
        compiler_params=_cp("parallel", "arbitrary"),
    )(a, *pieces, *([] if after is None else [after]))


def _win(arr, start=0, width=None):
    width = arr.shape[1] if width is None else width
    assert start % width == 0
    return (arr, start // width, width)


def _row_specs(rows, tm):
    return [pl.BlockSpec((tm, w), functools.partial(lambda i, cb: (i, cb), cb=cb)) for (_, cb, w) in rows]


def _full_spec(p):
    nd = p.ndim
    return pl.BlockSpec(p.shape, lambda i, nd=nd: (0,) * nd)


def _rowwise(fn, rows, params, outs, *, name, tm, after=None):
    t = rows[0][0].shape[0]
    tm = min(tm, t)
    assert t % tm == 0
    nr, npar = len(rows), len(params)
    n_after = 0 if after is None else 1

    def body(*refs):
        vals = [r[...] for r in refs[:nr + npar]]
        res = fn(*vals)
        for o_ref, r in zip(refs[nr + npar + n_after:], res):
            o_ref[...] = r.astype(o_ref.dtype)

    return pl.pallas_call(
        body, name=name, grid=(t // tm,),
        in_specs=_row_specs(rows, tm) + [_full_spec(p) for p in params] + [pl.BlockSpec(memory_space=pl.ANY)] * n_after,
        out_specs=[pl.BlockSpec((tm, w), lambda i: (i, 0)) for (w, _) in outs],
        out_shape=[jax.ShapeDtypeStruct((t, w), dt) for (w, dt) in outs],
        compiler_params=_cp("parallel"),
    )(*[r[0] for r in rows], *params, *([] if after is None else [after]))


def _rowwise_bwd(fn, rows, params, n_const, cots, *, name, tm, row_grad, add_to=None, packed=False):
    t = rows[0][0].shape[0]
    tm = min(tm, t)
    assert t % tm == 0
    nr, npar = len(rows), len(params)
    ndp = npar - n_const
    add_to = add_to or {}
    add_idx = sorted(add_to)
    flat_cots = [c for group in cots for c in group]
    kept = [i for i in range(nr) if row_grad[i] is not None]

    def body(*refs):
        pos = 0
        row_v = [r[...] for r in refs[pos:pos + nr]]; pos += nr
        par_v = [r[...] for r in refs[pos:pos + npar]]; pos += npar
        cot_v = [r[...] for r in refs[pos:pos + len(flat_cots)]]; pos += len(flat_cots)
        add_v = [r[...] for r in refs[pos:pos + len(add_idx)]]; pos += len(add_idx)
        if packed:
            offs = [sum(rows[i][2] for i in kept[:q]) for q in range(len(kept))]
            rg_refs = [refs[pos].at[:, o:o + rows[i][2]] for o, i in zip(offs, kept)]; pos += 1
        else:
            rg_refs = refs[pos:pos + len(kept)]; pos += len(kept)
        pg_refs = refs[pos:pos + ndp]

        consts = par_v[ndp:]
        res, vjp = jax.vjp(lambda *args: tuple(fn(*args, *consts)), *row_v, *par_v[:ndp])
        cot_in, q = [], 0
        for j, group in enumerate(cots):
            c = None
            for _ in group:
                cv = cot_v[q].astype(F32); q += 1
                c = cv if c is None else c + cv
            c = jnp.zeros(res[j].shape, F32) if c is None else c
            cot_in.append(c.astype(res[j].dtype))
        grads = vjp(tuple(cot_in))
        for ref, i in zip(rg_refs, kept):
            g = grads[i].astype(F32)
            if i in add_to:
                g = g + add_v[add_idx.index(i)].astype(F32)
            ref[...] = g.astype(ref.dtype)

        @pl.when(pl.program_id(0) == 0)
        def _():
            for ref in pg_refs:
                ref[...] = jnp.zeros_like(ref)

        for ref, g in zip(pg_refs, grads[nr:]):
            ref[...] += g.astype(F32)

    cot_specs = [pl.BlockSpec((tm, c.shape[1]), lambda i: (i, 0)) for c in flat_cots]
    add_specs = [pl.BlockSpec((tm, add_to[i].shape[1]), lambda i_: (i_, 0)) for i in add_idx]
    widths = [sum(rows[i][2] for i in kept)] if packed else [rows[i][2] for i in kept]
    n_rg = len(widths)
    out_specs = [pl.BlockSpec((tm, w), lambda i_: (i_, 0)) for w in widths] + [_full_spec(p) for p in params[:ndp]]
    out_shape = [jax.ShapeDtypeStruct((t, w), row_grad[kept[q]]) for q, w in enumerate(widths)] + [
        jax.ShapeDtypeStruct(p.shape, F32) for p in params[:ndp]]
    res = pl.pallas_call(
        body, name=name, grid=(t // tm,),
        in_specs=_row_specs(rows, tm) + [_full_spec(p) for p in params] + cot_specs + add_specs,
        out_specs=out_specs, out_shape=out_shape,
        compiler_params=_cp("arbitrary"),
    )(*[r[0] for r in rows], *params, *flat_cots, *[add_to[i] for i in add_idx])
    return list(res[:n_rg]), list(res[n_rg:])


def _rms(x, g):
    xf = x.astype(F32)
    return xf * lax.rsqrt(jnp.mean(xf * xf, axis=-1, keepdims=True) + NORM_EPS) * g


def _softplus(x):
    return jnp.maximum(x, 0.0) + jnp.log(1.0 + jnp.exp(-jnp.abs(x)))


def _fn_pre(x, g):
    return (_rms(x, g).astype(BF16),)


def _fn_res(x, u, g_post):
    return (x + _rms(u, g_post),)


def _fn_res_pre(x, u, g_post, g_pre):
    xn = x + _rms(u, g_post)
    return xn, _rms(xn, g_pre).astype(BF16)


def _fn_mix(zga, zgb, ya, yb):
    return ((jax.nn.sigmoid(zga) * ya + jax.nn.sigmoid(zgb) * yb).astype(BF16),)


def _fn_swiglu(gate, up):
    gate, up = gate.astype(F32), up.astype(F32)
    return ((gate * jax.nn.sigmoid(gate) * up).astype(BF16),)


def _fn_prep(zk, zw, za, zg, decay_base, d_up, iclr_base, i_up, g_up, kns, kis, e_hd, e_dh):
    w_log = -_softplus(-(decay_base + _dot(jnp.tanh(zw), d_up))) - 0.5
    lw = -jnp.exp(w_log)
    a = jax.nn.sigmoid(iclr_base + _dot(za, i_up))
    g = _dot(jax.nn.sigmoid(zg), g_up)
    kn = zk * kns
    ss = _dot_split_a(kn * kn, e_dh)
    inv = lax.rsqrt(jnp.maximum(ss, 1e-24))
    kk = kn * _dot_split_a(inv, e_hd)
    k2 = zk * (1.0 + (a - 1.0) * kis)
    return lw, k2, kk, a, g


def _fn_post(y, r, k2, v, g, lnx_w, lnx_b, bonus, e_hd, e_dh):
    mu = _dot_split_a(_dot_split_a(y, e_dh) * (1.0 / HEAD), e_hd)
    yc = y - mu
    var = _dot_split_a(yc * yc, e_dh) * (1.0 / HEAD)
    yn = yc * _dot_split_a(lax.rsqrt(var + GROUP_NORM_EPS), e_hd)
    bs = _dot_split_a(_dot_split_a(r * k2 * bonus, e_dh), e_hd)
    return (((yn * lnx_w + lnx_b + bs * v) * g).astype(BF16),)


def _shift_fwd(p, col0, ncols, mix, seq, *, name, cw=256):
    t = p.shape[0]
    assert col0 % cw == 0 and ncols % cw == 0 and t % seq == 0
    cb0 = col0 // cw

    def body(p_ref, m_ref, z_ref):
        pv = p_ref[...]
        row = lax.broadcasted_iota(jnp.int32, pv.shape, 0)
        prev = jnp.where(row == 0, 0.0, pltpu.roll(pv, 1, axis=0))
        z_ref[...] = pv + (prev - pv) * m_ref[...]

    return pl.pallas_call(
        body, name=name, grid=(t // seq, ncols // cw),
        in_specs=[pl.BlockSpec((seq, cw), lambda b, c: (b, c + cb0)), pl.BlockSpec((1, cw), lambda b, c: (0, c))],
        out_specs=pl.BlockSpec((seq, cw), lambda b, c: (b, c)),
        out_shape=jax.ShapeDtypeStruct((t, ncols), F32),
        compiler_params=_cp("parallel", "parallel"),
    )(p, mix)


def _shift_bwd(p, col0, ncols, mix, dz_parts, seq, *, name, cw=256):
    t = p.shape[0]
    cb0 = col0 // cw
    n = len(dz_parts)

    def body(*refs):
        p_ref, m_ref = refs[:2]
        dp_ref, dm_ref = refs[2 + n:]
        dz = refs[2][...].astype(F32)
        for r in refs[3:2 + n]:
            dz = dz + r[...].astype(F32)
        pv = p_ref[...]
        mixv = m_ref[...]
        row = lax.broadcasted_iota(jnp.int32, pv.shape, 0)
        prev = jnp.where(row == 0, 0.0, pltpu.roll(pv, 1, axis=0))
        u = dz * mixv
        nxt = jnp.where(row == seq - 1, 0.0, pltpu.roll(u, seq - 1, axis=0))
        dp_ref[...] = (dz - u + nxt).astype(dp_ref.dtype)

        @pl.when(pl.program_id(1) == 0)
        def _():
            dm_ref[...] = jnp.zeros_like(dm_ref)

        dm_ref[...] += jnp.sum(dz * (prev - pv), axis=0, keepdims=True)

    return pl.pallas_call(
        body, name=name, grid=(ncols // cw, t // seq),
        in_specs=[pl.BlockSpec((seq, cw), lambda c, b: (b, c + cb0)), pl.BlockSpec((1, cw), lambda c, b: (0, c))]
        + [pl.BlockSpec((seq, cw), lambda c, b: (b, c))] * n,
        out_specs=[pl.BlockSpec((seq, cw), lambda c, b: (b, c)), pl.BlockSpec((1, cw), lambda c, b: (0, c))],
        out_shape=[jax.ShapeDtypeStruct((t, ncols), BF16), jax.ShapeDtypeStruct((1, ncols), F32)],
        compiler_params=_cp("parallel", "arbitrary"),
    )(p, mix, *dz_parts)


def _each(f, *lists):
    return [f(*xs) for xs in zip(*lists)]


def _tri_inv(low):
    c = low[0].shape[0]
    ti = lax.broadcasted_iota(jnp.int32, (c, c), 0)
    si = lax.broadcasted_iota(jnp.int32, (c, c), 1)
    eye = (ti == si).astype(F32)
    inside = (ti // 4) == (si // 4)
    base = [jnp.where(inside, m, 0.0) for m in low]
    acc = _each(lambda m: _dot(eye - m, eye + _dot(m, m)), base)
    size = 8
    while size <= c:
        wider = (ti // size) == (si // size)
        keep = jnp.logical_and(wider, jnp.logical_not(inside))
        acc = _each(lambda p, m: p - _dot(_dot(p, jnp.where(keep, m, 0.0)), p), acc, low)
        inside, size = wider, size * 2
    return acc


def _stack_rows(a, b):
    return jnp.concatenate([a, b], axis=0)


@jax.custom_vjp
def _split_rows(x):
    h = x.shape[0] // 2
    return x[:h], x[h:]


def _split_rows_fwd(x):
    return _split_rows(x), None


def _split_rows_bwd(_, g):
    return (jnp.concatenate(g, axis=0),)


_split_rows.defvjp(_split_rows_fwd, _split_rows_bwd)


def _masked_halves(stacked, top_mask, bottom_mask):
    halves = _each(_split_rows, stacked)
    return ([jnp.where(top_mask, t, 0.0) for t, _ in halves], [jnp.where(bottom_mask, b, 0.0) for _, b in halves])


@jax.custom_vjp
def _tri_inv_known(low, inv):
    return inv


def _tri_inv_known_fwd(low, inv):
    return inv, inv


def _tri_inv_known_bwd(inv, g):
    dlow = _each(lambda t, gg: -_dot(_dot(t, gg, _TN), t, _NT), inv, g)
    return dlow, _each(jnp.zeros_like, inv)


_tri_inv_known.defvjp(_tri_inv_known_fwd, _tri_inv_known_bwd)


def _wkv_chunk(s0, r, lw, k, v, kk, a, inv=None):
    c = r[0].shape[0]
    ti = lax.broadcasted_iota(jnp.int32, (c, c), 0)
    si = lax.broadcasted_iota(jnp.int32, (c, c), 1)
    incl, strict = ti >= si, ti > si
    tri = incl.astype(F32)
    cum = _each(lambda x: _dot_split_b(tri, x, 3), lw)
    eg = _each(jnp.exp, cum)
    egp = _each(lambda cs, x: jnp.exp(cs - x), cum, lw)
    ei = _each(lambda cs: jnp.exp(-cs), cum)
    rh, kkh, kt = _each(jnp.multiply, r, eg), _each(jnp.multiply, kk, egp), _each(jnp.multiply, k, ei)
    bt = _each(lambda p, q, e: (p * q) * e, a, kk, ei)
    both = _each(_stack_rows, kkh, rh)
    on_b, on_k, on_s = _each(_dot_nt, both, bt), _each(_dot_nt, both, kt), _each(_dot_nt, both, s0)
    lb, mb = _masked_halves(on_b, strict, incl)
    lk, mk = _masked_halves(on_k, strict, incl)
    on_s = _each(_split_rows, on_s)
    on_v = _each(lambda p, q, x: _split_rows(_dot(_stack_rows(p, q), x)), lk, mk, v)
    rhs = _each(lambda p, q: p[0] + q[0], on_s, on_v)
    inv = _tri_inv(lb) if inv is None else _tri_inv_known(lb, inv)
    u = _each(lambda t, x: -_dot(t, x), inv, rhs)
    y = _each(lambda p, m1, uu, q: p[1] + _dot(m1, uu) + q[1], on_s, mb, u, on_v)
    s1 = _each(lambda s, uu, x, b, kq, w: (s + _dot_tn(_stack_rows(uu, x), _stack_rows(b, kq)))
               * jnp.exp(jnp.sum(w, axis=0, keepdims=True)), s0, u, v, bt, kt, lw)
    return y, s1, inv


WKV_HEADS = 16
WKV_COLS = WKV_HEADS * HEAD
WKV_GROUPS = N_HEADS // WKV_HEADS


def _head_cols(ref):
    return [ref[:, h * HEAD:(h + 1) * HEAD] for h in range(ref.shape[1] // HEAD)]


def _wkv_specs(seq, rev):
    nc = seq // CHUNK

    def rows(col0):
        cb0 = col0 // WKV_COLS
        if rev:
            return pl.BlockSpec((CHUNK, WKV_COLS), lambda b, h, c: (b * nc + nc - 1 - c, cb0 + h))
        return pl.BlockSpec((CHUNK, WKV_COLS), lambda b, h, c: (b * nc + c, cb0 + h))

    if rev:
        st = pl.BlockSpec((1, 1, WKV_HEADS, HEAD, HEAD), lambda b, h, c: (b * WKV_GROUPS + h, nc - 1 - c, 0, 0, 0))
    else:
        st = pl.BlockSpec((1, 1, WKV_HEADS, HEAD, HEAD), lambda b, h, c: (b * WKV_GROUPS + h, c, 0, 0, 0))
    return rows, st


def _wkv_fwd(z_rkv, lw, k2, kk, a, seq):
    t = z_rkv.shape[0]
    nb, nc = t // seq, seq // CHUNK
    rows, st = _wkv_specs(seq, False)

    def body(r_ref, v_ref, lw_ref, k_ref, kk_ref, a_ref, y_ref, st_ref, inv_ref, s_scr):
        @pl.when(pl.program_id(2) == 0)
        def _():
            s_scr[...] = jnp.zeros_like(s_scr)

        s0 = [s_scr[h] for h in range(WKV_HEADS)]
        y, s1, inv = _wkv_chunk(s0, *[_head_cols(ref) for ref in (r_ref, lw_ref, k_ref, v_ref, kk_ref, a_ref)])
        for h in range(WKV_HEADS):
            st_ref[0, 0, h] = s0[h]
            inv_ref[0, 0, h] = inv[h]
            y_ref[:, h * HEAD:(h + 1) * HEAD] = y[h]
            s_scr[h] = s1[h]

    per_chunk = jax.ShapeDtypeStruct((nb * WKV_GROUPS, nc, WKV_HEADS, HEAD, HEAD), F32)
    return pl.pallas_call(
        body, name="wkv_fwd", grid=(nb, WKV_GROUPS, nc),
        in_specs=[rows(0), rows(2 * D), rows(0), rows(0), rows(0), rows(0)],
        out_specs=[rows(0), st, st],
        out_shape=[jax.ShapeDtypeStruct((t, D), F32), per_chunk, per_chunk],
        scratch_shapes=[pltpu.VMEM((WKV_HEADS, HEAD, HEAD), F32)],
        compiler_params=_cp("parallel", "parallel", "arbitrary"),
    )(z_rkv, z_rkv, lw, k2, kk, a)


def _wkv_bwd(z_rkv, lw, k2, kk, a, states, invs, dy, seq):
    t = z_rkv.shape[0]
    nb, nc = t // seq, seq // CHUNK
    rows, st = _wkv_specs(seq, True)

    def body(r_ref, v_ref, lw_ref, k_ref, kk_ref, a_ref, st_ref, inv_ref, dy_ref,
             dr_ref, dlw_ref, dk_ref, dv_ref, dkk_ref, da_ref, ds_scr):
        @pl.when(pl.program_id(2) == 0)
        def _():
            ds_scr[...] = jnp.zeros_like(ds_scr)

        s0 = [st_ref[0, 0, h] for h in range(WKV_HEADS)]
        inv = [inv_ref[0, 0, h] for h in range(WKV_HEADS)]
        _, vjp = jax.vjp(lambda *args: _wkv_chunk(*args, inv=inv)[:2],
                         s0, *[_head_cols(ref) for ref in (r_ref, lw_ref, k_ref, v_ref, kk_ref, a_ref)])
        grads = vjp(([x.astype(F32) for x in _head_cols(dy_ref)], [ds_scr[h] for h in range(WKV_HEADS)]))
        for h in range(WKV_HEADS):
            ds_scr[h] = grads[0][h]
            for ref, g in zip((dr_ref, dlw_ref, dk_ref, dv_ref, dkk_ref, da_ref), grads[1:]):
                ref[:, h * HEAD:(h + 1) * HEAD] = g[h]

    return pl.pallas_call(
        body, name="wkv_bwd", grid=(nb, WKV_GROUPS, nc),
        in_specs=[rows(0), rows(2 * D), rows(0), rows(0), rows(0), rows(0), st, st, rows(0)],
        out_specs=[rows(0)] * 6,
        out_shape=[jax.ShapeDtypeStruct((t, D), F32)] * 6,
        scratch_shapes=[pltpu.VMEM((WKV_HEADS, HEAD, HEAD), F32)],
        compiler_params=_cp("parallel", "parallel", "arbitrary"),
    )(z_rkv, z_rkv, lw, k2, kk, a, states, invs, dy)


def _softmax(s):
    e = jnp.exp(s - jnp.max(s, axis=-1, keepdims=True))
    return e * (1.0 / jnp.sum(e, axis=-1, keepdims=True))


ATT_HEADS = 8
ATT_COLS = ATT_HEADS * HEAD
ATT_GROUPS = N_HEADS // ATT_HEADS


def _attn_chunk(q, kb, vb, bias, valid):
    s = _each(lambda x, y, z: jnp.where(valid, _dot_nt(x * (HEAD ** -0.5), y) + z, MASK_VALUE), q, kb, bias)
    return _each(_dot, _each(_softmax, s), vb)


def _pad_fill(pad_ref, src_ref):
    pad_ref[0:LEFT, :] = jnp.zeros((LEFT, pad_ref.shape[1]), pad_ref.dtype)
    pad_ref[LEFT:, :] = src_ref[...].astype(pad_ref.dtype)


def _band_heads(pad_ref, start):
    return [pad_ref[pl.ds(start, BAND), h * HEAD:(h + 1) * HEAD].astype(F32) for h in range(ATT_HEADS)]


def _band_valid(c):
    return (c * CHUNK - LEFT + lax.broadcasted_iota(jnp.int32, (1, BAND), 1)) >= 0


def _attn_fwd(proj, bias, seq):
    t = proj.shape[0]
    nb, nc = t // seq, seq // CHUNK
    cq = C_Q // ATT_COLS

    def body(q_ref, k_ref, v_ref, b_ref, o_ref, kpad, vpad):
        c = pl.program_id(2)

        @pl.when(c == 0)
        def _():
            _pad_fill(kpad, k_ref)
            _pad_fill(vpad, v_ref)

        start = pl.multiple_of(c * CHUNK, CHUNK)
        o = _attn_chunk(_head_cols(q_ref), _band_heads(kpad, start), _band_heads(vpad, start),
                        [b_ref[h] for h in range(ATT_HEADS)], _band_valid(c))
        for h in range(ATT_HEADS):
            o_ref[:, h * HEAD:(h + 1) * HEAD] = o[h].astype(o_ref.dtype)

    return pl.pallas_call(
        body, name="attn_fwd", grid=(ATT_GROUPS, nb, nc),
        in_specs=[pl.BlockSpec((CHUNK, ATT_COLS), lambda h, b, c: (b * nc + c, cq + h)),
                  pl.BlockSpec((seq, ATT_COLS), lambda h, b, c: (b, cq + ATT_GROUPS + h)),
                  pl.BlockSpec((seq, ATT_COLS), lambda h, b, c: (b, cq + 2 * ATT_GROUPS + h)),
                  pl.BlockSpec((ATT_HEADS, CHUNK, BAND), lambda h, b, c: (h, 0, 0))],
        out_specs=pl.BlockSpec((CHUNK, ATT_COLS), lambda h, b, c: (b * nc + c, h)),
        out_shape=jax.ShapeDtypeStruct((t, D), BF16),
        scratch_shapes=[pltpu.VMEM((seq + LEFT, ATT_COLS), BF16)] * 2,
        compiler_params=_cp("parallel", "arbitrary", "arbitrary"),
    )(proj, proj, proj, bias)


def _attn_bwd(proj, bias, do, seq):
    t = proj.shape[0]
    nb, nc = t // seq, seq // CHUNK
    cq = C_Q // ATT_COLS

    def body(q_ref, k_ref, v_ref, b_ref, do_ref, dq_ref, dk_ref, dv_ref, db_ref, kpad, vpad, dkpad, dvpad):
        b, c = pl.program_id(1), pl.program_id(2)

        @pl.when(c == 0)
        def _():
            _pad_fill(kpad, k_ref)
            _pad_fill(vpad, v_ref)
            dkpad[...] = jnp.zeros_like(dkpad)
            dvpad[...] = jnp.zeros_like(dvpad)

        @pl.when(jnp.logical_and(b == 0, c == 0))
        def _():
            db_ref[...] = jnp.zeros_like(db_ref)

        start = pl.multiple_of(c * CHUNK, CHUNK)
        _, vjp = jax.vjp(functools.partial(_attn_chunk, valid=_band_valid(c)),
                         _head_cols(q_ref), _band_heads(kpad, start), _band_heads(vpad, start),
                         [b_ref[h] for h in range(ATT_HEADS)])
        dq, dkb, dvb, dbias = vjp([x.astype(F32) for x in _head_cols(do_ref)])
        for h in range(ATT_HEADS):
            sl = slice(h * HEAD, (h + 1) * HEAD)
            dq_ref[:, sl] = dq[h].astype(dq_ref.dtype)
            dkpad[pl.ds(start, BAND), sl] += dkb[h].astype(F32)
            dvpad[pl.ds(start, BAND), sl] += dvb[h].astype(F32)
            db_ref[h] += dbias[h]

        @pl.when(c == nc - 1)
        def _():
            dk_ref[...] = dkpad[LEFT:, :].astype(dk_ref.dtype)
            dv_ref[...] = dvpad[LEFT:, :].astype(dv_ref.dtype)

    kv_out = pl.BlockSpec((seq, ATT_COLS), lambda h, b, c: (b, h))
    return pl.pallas_call(
        body, name="attn_bwd", grid=(ATT_GROUPS, nb, nc),
        in_specs=[pl.BlockSpec((CHUNK, ATT_COLS), lambda h, b, c: (b * nc + c, cq + h)),
                  pl.BlockSpec((seq, ATT_COLS), lambda h, b, c: (b, cq + ATT_GROUPS + h)),
                  pl.BlockSpec((seq, ATT_COLS), lambda h, b, c: (b, cq + 2 * ATT_GROUPS + h)),
                  pl.BlockSpec((ATT_HEADS, CHUNK, BAND), lambda h, b, c: (h, 0, 0)),
                  pl.BlockSpec((CHUNK, ATT_COLS), lambda h, b, c: (b * nc + c, h))],
        out_specs=[pl.BlockSpec((CHUNK, ATT_COLS), lambda h, b, c: (b * nc + c, h)), kv_out, kv_out,
                   pl.BlockSpec((ATT_HEADS, CHUNK, BAND), lambda h, b, c: (h, 0, 0))],
        out_shape=[jax.ShapeDtypeStruct((t, D), BF16)] * 3 + [jax.ShapeDtypeStruct((N_HEADS, CHUNK, BAND), F32)],
        scratch_shapes=[pltpu.VMEM((seq + LEFT, ATT_COLS), BF16)] * 2 + [pltpu.VMEM((seq + LEFT, ATT_COLS), F32)] * 2,
        compiler_params=_cp("parallel", "arbitrary", "arbitrary"),
    )(proj, proj, proj, bias, do)


def _xattn_tile(q, k, v):
    s = _dot_nt(q, k) * ((MEM_WIDTH // MEM_HEADS) ** -0.5)
    return _dot(_softmax(s), v)


def _xattn_fwd(qm, kvm, seq, n_mem, tq=512):
    t = qm.shape[0]
    tq = min(tq, seq)
    nb, nq = t // seq, seq // tq

    def body(q_ref, k_ref, v_ref, o_ref):
        o_ref[...] = _xattn_tile(q_ref[...], k_ref[...], v_ref[...]).astype(o_ref.dtype)

    return pl.pallas_call(
        body, name="xattn_fwd", grid=(nb, MEM_HEADS, nq),
        in_specs=[pl.BlockSpec((tq, LANE), lambda b, h, i: (b * nq + i, h)),
                  pl.BlockSpec((n_mem, LANE), lambda b, h, i: (b, h)),
                  pl.BlockSpec((n_mem, LANE), lambda b, h, i: (b, MEM_HEADS + h))],
        out_specs=pl.BlockSpec((tq, LANE), lambda b, h, i: (b * nq + i, h)),
        out_shape=jax.ShapeDtypeStruct((t, MEM_WIDTH), BF16),
        compiler_params=_cp("parallel", "parallel", "parallel"),
    )(qm, kvm, kvm)


def _xattn_bwd(qm, kvm, do, seq, n_mem, tq=512):
    t = qm.shape[0]
    tq = min(tq, seq)
    nb, nq = t // seq, seq // tq

    def body(q_ref, k_ref, v_ref, do_ref, dq_ref, dkv_ref, dk_acc, dv_acc):
        i = pl.program_id(2)

        @pl.when(i == 0)
        def _():
            dk_acc[...] = jnp.zeros_like(dk_acc)
            dv_acc[...] = jnp.zeros_like(dv_acc)

        _, vjp = jax.vjp(_xattn_tile, q_ref[...], k_ref[...], v_ref[...])
        dq, dk, dv = vjp(do_ref[...].astype(F32))
        dq_ref[...] = dq.astype(dq_ref.dtype)
        dk_acc[...] += dk
        dv_acc[...] += dv

        @pl.when(i == nq - 1)
        def _():
            dkv_ref[0] = dk_acc[...].astype(dkv_ref.dtype)
            dkv_ref[1] = dv_acc[...].astype(dkv_ref.dtype)

    dq, dkv = pl.pallas_call(
        body, name="xattn_bwd", grid=(nb, MEM_HEADS, nq),
        in_specs=[pl.BlockSpec((tq, LANE), lambda b, h, i: (b * nq + i, h)),
                  pl.BlockSpec((n_mem, LANE), lambda b, h, i: (b, h)),
                  pl.BlockSpec((n_mem, LANE), lambda b, h, i: (b, MEM_HEADS + h)),
                  pl.BlockSpec((tq, LANE), lambda b, h, i: (b * nq + i, h))],
        out_specs=[pl.BlockSpec((tq, LANE), lambda b, h, i: (b * nq + i, h)),
                   pl.BlockSpec((2, n_mem, LANE), lambda b, h, i: (0, b, h))],
        out_shape=[jax.ShapeDtypeStruct((t, MEM_WIDTH), BF16), jax.ShapeDtypeStruct((2, nb * n_mem, MEM_WIDTH), BF16)],
        scratch_shapes=[pltpu.VMEM((n_mem, LANE), F32)] * 2,
        compiler_params=_cp("parallel", "parallel", "arbitrary"),
    )(qm, kvm, kvm, do)
    return dq, jnp.concatenate([dkv[0], dkv[1]], axis=1)


def _loss_head(x, u, g_post, target, tm=256):
    t, d = x.shape
    tm = min(tm, t)

    def tile_loss(xv, uv, gv, tv):
        diff = _fn_res(xv, uv, gv)[0] - tv
        return 0.5 * jnp.sum(jnp.mean(diff * diff, axis=-1, keepdims=True), axis=0, keepdims=True)

    def body(x_ref, u_ref, g_ref, t_ref, l_ref, dx_ref, du_ref, dg_ref):
        @pl.when(pl.program_id(0) == 0)
        def _():
            l_ref[...] = jnp.zeros_like(l_ref)
            dg_ref[...] = jnp.zeros_like(dg_ref)

        tv = t_ref[...]
        part, vjp = jax.vjp(lambda xv, uv, gv: tile_loss(xv, uv, gv, tv), x_ref[...], u_ref[...], g_ref[...])
        dx, du, dg = vjp(jnp.ones((1, 1), F32))
        l_ref[...] += part
        dx_ref[...] = dx
        du_ref[...] = du.astype(du_ref.dtype)
        dg_ref[...] += dg

    rows = pl.BlockSpec((tm, d), lambda i: (i, 0))
    vec = pl.BlockSpec((1, d), lambda i: (0, 0))
    return pl.pallas_call(
        body, name="loss_head", grid=(t // tm,),
        in_specs=[rows, rows, vec, rows],
        out_specs=[pl.BlockSpec((8, LANE), lambda i: (0, 0)), rows, rows, vec],
        out_shape=[jax.ShapeDtypeStruct((8, LANE), F32), jax.ShapeDtypeStruct((t, d), F32),
                   jax.ShapeDtypeStruct((t, d), BF16), jax.ShapeDtypeStruct((1, d), F32)],
        compiler_params=_cp("arbitrary"),
    )(x, u, g_post, target)


def _mesh_pos():
    return lax.axis_index("x"), lax.axis_index("y"), lax.axis_index("c")


def _peer(pos, d):
    x, y, c = pos
    return ((1 - x) if d & 4 else x, (1 - y) if d & 2 else y, (1 - c) if d & 1 else c)


def _flat(pos):
    return 4 * pos[0] + 2 * pos[1] + pos[2]


def _exchange(arrays, scatter, *, name):
    n = len(arrays)
    shapes = [a.shape[1:] if scatter else a.shape for a in arrays]

    def body(*refs):
        ins, outs = refs[:n], refs[n:2 * n]
        send, recv, loc = refs[2 * n:]
        pos = _mesh_pos()
        me = _flat(pos)
        pending = []
        for i in range(n):
            own = pltpu.make_async_copy(ins[i].at[me] if scatter else ins[i], outs[i].at[me], loc.at[i])
            own.start()
            pending.append(own)
            for d in range(1, N_DEV):
                peer = _peer(pos, d)
                src = ins[i].at[_flat(peer)] if scatter else ins[i]
                out_cp = pltpu.make_async_remote_copy(
                    src_ref=src, dst_ref=outs[i].at[me], send_sem=send.at[i, d - 1], recv_sem=recv.at[i, d - 1],
                    device_id=peer, device_id_type=pl.DeviceIdType.MESH)
                out_cp.start()
                pending.append(out_cp)
        for i in range(n):
            own = pending[i * N_DEV]
            for d in range(1, N_DEV):
                peer = _peer(pos, d)
                src = ins[i].at[_flat(peer)] if scatter else ins[i]
                pending[i * N_DEV + d].wait_send()
                pltpu.make_async_remote_copy(
                    src_ref=src, dst_ref=outs[i].at[_flat(peer)], send_sem=send.at[i, d - 1], recv_sem=recv.at[i, d - 1],
                    device_id=peer, device_id_type=pl.DeviceIdType.MESH).wait_recv()
            own.wait()

    hbm = pl.BlockSpec(memory_space=pltpu.HBM)
    return pl.pallas_call(
        body, name=name,
        in_specs=[hbm] * n, out_specs=[hbm] * n,
        out_shape=[jax.ShapeDtypeStruct((N_DEV,) + tuple(s), a.dtype) for s, a in zip(shapes, arrays)],
        scratch_shapes=[pltpu.SemaphoreType.DMA((n, N_DEV - 1)), pltpu.SemaphoreType.DMA((n, N_DEV - 1)),
                        pltpu.SemaphoreType.DMA((n,))],
    )(*arrays)


_HBM = pl.BlockSpec(memory_space=pltpu.HBM)
_SEM = pl.BlockSpec(memory_space=pltpu.SEMAPHORE)
_DATAFLOW = pltpu.SideEffectType.DATAFLOW_SIDE_EFFECTING


_ALL_PEERS = tuple(range(1, N_DEV))
_SIBLING_AND_SAME_CORE = (1, 2, 4, 6)


def _remote_copies(ins, lands, send, recv, scatter, dists):
    pos = _mesh_pos()
    me = _flat(pos)
    out = []
    for i in range(len(ins)):
        for j, d in enumerate(dists):
            peer = _peer(pos, d)
            src = ins[i].at[_flat(peer)] if scatter else ins[i]
            pair = i * len(dists) + j
            sems = dict(send_sem=send.at[pair], recv_sem=recv.at[pair], device_id=peer,
                        device_id_type=pl.DeviceIdType.MESH)
            out.append((pltpu.make_async_remote_copy(src_ref=src, dst_ref=lands[i].at[me], **sems),
                        pltpu.make_async_remote_copy(src_ref=src, dst_ref=lands[i].at[_flat(peer)], **sems)))
    return out


def _exchange_start(arrays, scatter, after, *, name, dists=_ALL_PEERS):
    n = len(arrays)
    shapes = [a.shape[1:] if scatter else a.shape for a in arrays]
    lands = [pltpu.with_memory_space_constraint(lax.empty((N_DEV,) + tuple(s), a.dtype), pltpu.HBM)
             for s, a in zip(shapes, arrays)]
    srcs = [pltpu.with_memory_space_constraint(a, pltpu.HBM) for a in arrays]

    def body(*refs):
        ins, land_refs = refs[:n], refs[n:2 * n]
        send, recv, token = refs[2 * n + 1], refs[2 * n + 2], refs[-1]
        for going, _ in _remote_copies(ins, land_refs, send, recv, scatter, dists):
            going.start()
        token[...] = jnp.zeros_like(token)

    sems = pltpu.SemaphoreType.DMA((n * len(dists),))
    res = pl.pallas_call(
        body, name=name,
        out_shape=(sems, sems, *[pltpu.HBM(a.shape, a.dtype) for a in srcs + lands], jax.ShapeDtypeStruct((8, LANE), F32)),
        in_specs=[_HBM] * (2 * n) + [pl.BlockSpec(memory_space=pl.ANY)],
        out_specs=(_SEM, _SEM, *[_HBM] * (2 * n), pl.BlockSpec(memory_space=pltpu.VMEM)),
        input_output_aliases={i: 2 + i for i in range(2 * n)},
        compiler_params=pltpu.CompilerParams(has_side_effects=_DATAFLOW),
    )(*srcs, *lands, after)
    return (n, scatter, dists, res[0], res[1], list(res[2:2 + 2 * n])), res[-1]


def _exchange_wait(handle, after, own, *, name):
    n, scatter, dists, send, recv, thru = handle

    def body(*refs):
        ins, land_refs = refs[:n], refs[n:2 * n]
        for going, coming in _remote_copies(ins, land_refs, refs[2 * n], refs[2 * n + 1], scatter, dists):
            going.wait_send()
            coming.wait_recv()

    res = pl.pallas_call(
        body, name=name,
        out_shape=tuple(pltpu.HBM(a.shape, a.dtype) for a in thru),
        in_specs=[_HBM] * (2 * n) + [_SEM, _SEM] + [pl.BlockSpec(memory_space=pl.ANY)] * len(after),
        out_specs=tuple([_HBM] * (2 * n)),
        input_output_aliases={i: i for i in range(2 * n)},
        compiler_params=pltpu.CompilerParams(has_side_effects=_DATAFLOW),
    )(*thru, send, recv, *after)
    me = _flat(_mesh_pos())
    return [lax.dynamic_update_slice_in_dim(land, o[None].astype(land.dtype), me, 0) for land, o in zip(res[n:], own)]


_OTHER_CHIPS = (2, 4, 6)


def _relay_to_sibling(gathered, *, name):
    n, k = len(gathered), len(_OTHER_CHIPS)

    def body(*refs):
        ins, outs = refs[:n], refs[n:2 * n]
        send, recv = refs[2 * n:]
        pos = _mesh_pos()
        copies = []
        for i in range(n):
            for j, d in enumerate(_OTHER_CHIPS):
                cp = pltpu.make_async_remote_copy(
                    src_ref=ins[i].at[_flat(_peer(pos, d))], dst_ref=outs[i].at[j],
                    send_sem=send.at[i * k + j], recv_sem=recv.at[i * k + j],
                    device_id=_peer(pos, 1), device_id_type=pl.DeviceIdType.MESH)
                cp.start()
                copies.append(cp)
        for cp in copies:
            cp.wait()

    return pl.pallas_call(
        body, name=name, in_specs=[_HBM] * n, out_specs=[_HBM] * n,
        out_shape=[jax.ShapeDtypeStruct((k,) + g.shape[1:], g.dtype) for g in gathered],
        scratch_shapes=[pltpu.SemaphoreType.DMA((n * k,)), pltpu.SemaphoreType.DMA((n * k,))],
    )(*gathered)


def _adamw(parts, w, m, v, *, name, tr=128, after=None):
    r, c = w.shape
    align = 8 * 4 // parts.dtype.itemsize
    row_tiles = [d for d in range(align, min(tr, r) + 1, align) if r % d == 0]
    tr, tc = (max(row_tiles), c) if row_tiles else (r, LANE)
    assert c % tc == 0
    n_after = 0 if after is None else 1

    def body(p_ref, w_ref, m_ref, v_ref, *rest):
        g_ref, d_ref, nm_ref, nv_ref = rest[n_after:]
        g = p_ref[0].astype(F32)
        for j in range(1, N_DEV):
            g = g + p_ref[j].astype(F32)
        m2 = ADAM_B1 * m_ref[...] + (1.0 - ADAM_B1) * g
        v2 = ADAM_B2 * v_ref[...] + (1.0 - ADAM_B2) * (g * g)
        m_hat = m2 / (1.0 - ADAM_B1 ** ADAM_STEP)
        v_hat = v2 / (1.0 - ADAM_B2 ** ADAM_STEP)
        g_ref[...] = g
        d_ref[...] = -ADAM_LR * (m_hat / (jnp.sqrt(v_hat) + ADAM_EPS) + ADAM_WD * w_ref[...])
        nm_ref[...] = m2
        nv_ref[...] = v2

    spec = pl.BlockSpec((tr, tc), lambda i, j: (i, j))
    return pl.pallas_call(
        body, name=name, grid=(r // tr, c // tc),
        in_specs=[pl.BlockSpec((N_DEV, tr, tc), lambda i, j: (0, i, j)), spec, spec, spec]
        + [pl.BlockSpec(memory_space=pl.ANY)] * n_after,
        out_specs=[spec] * 4, out_shape=[jax.ShapeDtypeStruct((r, c), F32)] * 4,
        compiler_params=_cp("parallel", "parallel"),
    )(parts, w, m, v, *([] if after is None else [after]))


def _cols_to_full(g):
    return jnp.transpose(g, (1, 0, 2)).reshape(g.shape[1], N_DEV * g.shape[2])


def _full_to_cols(w):
    r, c = w.shape
    return jnp.transpose(w.reshape(r, N_DEV, c // N_DEV), (1, 0, 2))


def _cut(a, lo, hi, axis):
    return lax.slice_in_dim(a, lo, hi, axis=axis)


def _pad_to(a, size, axis):
    pads = [(0, 0)] * a.ndim
    pads[axis] = (0, size - a.shape[axis])
    return jnp.pad(a, pads)


def _pad_lora(w, axis=1):
    return jnp.concatenate([
        _pad_to(_cut(w, 0, LORA_W, axis), 128, axis), _pad_to(_cut(w, LORA_W, LORA_W + LORA_A, axis), 128, axis),
        _pad_to(_cut(w, LORA_W + LORA_A, w.shape[axis], axis), 256, axis)], axis=axis)


def _unpad_lora(wp, axis=1):
    return jnp.concatenate([_cut(wp, 0, LORA_W, axis), _cut(wp, 128, 128 + LORA_A, axis),
                            _cut(wp, 256, 256 + LORA_G, axis)], axis=axis)


def _permute_in(w, axis):
    rk = 3 * D
    lo = rk + LORA_W + LORA_A + LORA_G
    return jnp.concatenate([_cut(w, 0, rk, axis), _cut(w, lo, w.shape[axis], axis), _pad_lora(_cut(w, rk, lo, axis), axis)],
                           axis=axis)


def _unpermute_in(wp, axis):
    return jnp.concatenate([_cut(wp, 0, 3 * D, axis), _unpad_lora(_cut(wp, C_LORA, P_WIDTH, axis), axis),
                            _cut(wp, 3 * D, C_LORA, axis)], axis=axis)


def _rel_index():
    dist = jnp.arange(CHUNK)[:, None] - jnp.arange(BAND)[None, :] + LEFT
    return (jnp.minimum(dist, REL_CLIP) + (CHUNK - 1)).reshape(-1)


def _local_step(x, mem, target, wt, seq, n_mem, comm):
    t = x.shape[0]
    row = lambda a: a.reshape(1, -1).astype(F32)
    g_pre_mix, g_post_mix = row(wt["g_pre_mix"]), row(wt["g_post_mix"])
    g_pre_cross, g_post_cross, g_mem = row(wt["g_pre_cross"]), row(wt["g_post_cross"]), row(wt["g_mem"])
    g_pre_ffn, g_post_ffn = row(wt["g_pre_ffn"]), row(wt["g_post_ffn"])
    mix = row(wt["shift_mix"])
    mix_rkv, mix_lora = mix[:, :3 * D], _pad_lora(mix[:, 3 * D:])
    decay_base, iclr_base = row(wt["decay_base"]), row(wt["iclr_base"])
    kns, kis = row(wt["key_norm_scale"]), row(wt["key_iclr_scale"])
    lnx_w, lnx_b, bonus = row(wt["lnx_w"]), row(wt["lnx_b"]), row(wt["bonus_scale"])
    e_dh = (jnp.arange(D)[:, None] // HEAD == jnp.arange(N_HEADS)[None, :]).astype(F32)
    e_hd = e_dh.T
    onehot = (jnp.arange(REL_TABLE)[:, None] == _rel_index()[None, :]).astype(BF16)

    begun = comm.begun
    (h1,) = _rowwise(_fn_pre, [_win(x)], [g_pre_mix], [(D, BF16)], name="pre_mix", tm=512, after=begun)
    (mn,) = _rowwise(_fn_pre, [_win(mem)], [g_mem], [(D, BF16)], name="pre_mem", tm=512, after=begun)
    bias = _mm(wt["rel_bias"].astype(F32), onehot, name="mm_bias", split_a=3, after=begun).reshape(N_HEADS, CHUNK, BAND)
    wt = {**wt, **comm.first_weights([h1, mn, bias])}
    w_in = wt["w_in_p"]
    d_up = jnp.pad(wt["decay_up"].astype(F32), ((0, 128 - LORA_W), (0, 0)))
    i_up = jnp.pad(wt["iclr_up"].astype(F32), ((0, 128 - LORA_A), (0, 0)))
    g_up = jnp.pad(wt["gate_up"].astype(F32), ((0, 256 - LORA_G), (0, 0)))
    proj = _mm(h1, w_in, tb=True, name="mm_in", after=comm.first_token)
    z_rkv = _shift_fwd(proj, 0, 3 * D, mix_rkv, seq, name="shift_rkv")
    z_lora = _shift_fwd(proj, C_LORA, 512, mix_lora, seq, name="shift_lora")
    prep_rows = [_win(z_rkv, D, D), _win(z_lora, 0, 128), _win(z_lora, 128, 128), _win(z_lora, 256, 256)]
    prep_params = [decay_base, d_up, iclr_base, i_up, g_up, kns, kis, e_hd, e_dh]
    lw, k2, kk, a, g = _rowwise(_fn_prep, prep_rows, prep_params, [(D, F32)] * 5, name="rwkv_prep", tm=256)
    y, states, invs = _wkv_fwd(z_rkv, lw, k2, kk, a, seq)
    post_rows = [_win(y), _win(z_rkv, 0, D), _win(k2), _win(z_rkv, 2 * D, D), _win(g)]
    post_params = [lnx_w, lnx_b, bonus, e_hd, e_dh]
    (y_a,) = _rowwise(_fn_post, post_rows, post_params, [(D, BF16)], name="rwkv_post", tm=256)
    y_b = _attn_fwd(proj, bias, seq)
    wt = {**wt, **comm.late_weights(y_b)}
    ya_p = _mm(y_a, wt["w_branch_a"], name="mm_a")
    yb_p = _mm(y_b, wt["w_branch_b"], name="mm_b")
    mix_rows = [_win(proj, C_GA, D), _win(proj, C_GA + D, D), _win(ya_p), _win(yb_p)]
    (mixed,) = _rowwise(_fn_mix, mix_rows, [], [(D, BF16)], name="gate_mix", tm=512)
    mo = _mm(mixed, wt["w_out"], name="mm_out")
    x1, h2 = _rowwise(_fn_res_pre, [_win(x), _win(mo)], [g_post_mix, g_pre_cross], [(D, F32), (D, BF16)],
                      name="res_mix", tm=512)
    qm = _mm(h2, wt["w_q_mem"], name="mm_q")
    kvm = _mm(mn, wt["w_kv_mem"], name="mm_kv")
    om = _xattn_fwd(qm, kvm, seq, n_mem)
    co = _mm(om, wt["w_o_mem"], name="mm_o")
    x2, h3 = _rowwise(_fn_res_pre, [_win(x1), _win(co)], [g_post_cross, g_pre_ffn], [(D, F32), (D, BF16)],
                      name="res_cross", tm=512)
    gu = _mm(h3, wt["w_ffn_in"], tb=True, name="mm_ffn_in", out_dtype=BF16)
    (act,) = _rowwise(_fn_swiglu, [_win(gu, 0, FFN), _win(gu, FFN, FFN)], [], [(FFN, BF16)], name="swiglu", tm=256)
    ff = _mm(act, wt["w_ffn_out"], name="mm_ffn_out")

    gw = {}
    loss, dx2, dff, gw["g_post_ffn"] = _loss_head(x2, ff, g_post_ffn, target)
    dact = _mm(dff, wt["w_ffn_out"], tb=True, name="mm_ffn_out_dx", out_dtype=BF16)
    gw["w_ffn_out"] = _mm(act, dff, ta=True, name="mm_ffn_out_dw", out_dtype=BF16)
    (dgu,), _ = _rowwise_bwd(_fn_swiglu, [_win(gu, 0, FFN), _win(gu, FFN, FFN)], [], 0, [[dact]],
                             name="swiglu_bwd", tm=256, row_grad=[BF16, BF16], packed=True)
    dh3 = _mm(dgu, wt["w_ffn_in"], name="mm_ffn_in_dx", out_dtype=BF16)
    gw["w_ffn_in"] = _mm(dgu, h3, ta=True, name="mm_ffn_in_dw", out_dtype=BF16)
    (dx1, dco), (gw["g_post_cross"], gw["g_pre_ffn"]) = _rowwise_bwd(
        _fn_res_pre, [_win(x1), _win(co)], [g_post_cross, g_pre_ffn], 0, [[dx2], [dh3]],
        name="res_cross_bwd", tm=256, row_grad=[F32, BF16])
    dom = _mm(dco, wt["w_o_mem"], tb=True, name="mm_o_dx", out_dtype=BF16)
    gw["w_o_mem"] = _mm(om, dco, ta=True, name="mm_o_dw", out_dtype=BF16)
    dqm, dkvm = _xattn_bwd(qm, kvm, dom, seq, n_mem)
    dh2 = _mm(dqm, wt["w_q_mem"], tb=True, name="mm_q_dx", out_dtype=BF16)
    gw["w_q_mem"] = _mm(h2, dqm, ta=True, name="mm_q_dw", out_dtype=BF16)
    dmn = _mm(dkvm, wt["w_kv_mem"], tb=True, name="mm_kv_dx", out_dtype=BF16)
    gw["w_kv_mem"] = _mm(mn, dkvm, ta=True, name="mm_kv_dw", out_dtype=BF16)
    _, (gw["g_mem"],) = _rowwise_bwd(_fn_pre, [_win(mem)], [g_mem], 0, [[dmn]], name="pre_mem_bwd", tm=256,
                                     row_grad=[None])
    (dx0, dmo), (gw["g_post_mix"], gw["g_pre_cross"]) = _rowwise_bwd(
        _fn_res_pre, [_win(x), _win(mo)], [g_post_mix, g_pre_cross], 0, [[dx1], [dh2]],
        name="res_mix_bwd", tm=256, row_grad=[F32, BF16])
    dmixed = _mm(dmo, wt["w_out"], tb=True, name="mm_out_dx", out_dtype=BF16)
    gw["w_out"] = _mm(mixed, dmo, ta=True, name="mm_out_dw", out_dtype=BF16)
    (dzga, dzgb, dya_p, dyb_p), _ = _rowwise_bwd(_fn_mix, mix_rows, [], 0, [[dmixed]], name="gate_mix_bwd", tm=256,
                                                 row_grad=[BF16] * 4)
    gw["w_branch_a"] = _mm(y_a, dya_p, ta=True, name="mm_a_dw", out_dtype=BF16)
    gw["w_branch_b"] = _mm(y_b, dyb_p, ta=True, name="mm_b_dw", out_dtype=BF16)
    token = comm.send_early(gw)
    dy_a = _mm(dya_p, wt["w_branch_a"], tb=True, name="mm_a_dx", out_dtype=BF16, after=token)
    dy_b = _mm(dyb_p, wt["w_branch_b"], tb=True, name="mm_b_dx", out_dtype=BF16, after=token)
    dq, dk, dv, dbias = _attn_bwd(proj, bias, dy_b, seq)
    gw["rel_bias"] = _mm(dbias.reshape(N_HEADS, CHUNK * BAND), onehot, tb=True, name="mm_bias_dw", split_a=2)
    (dy, dr_p, dk2_p, dv_p, dg), (gw["lnx_w"], gw["lnx_b"], gw["bonus_scale"]) = _rowwise_bwd(
        _fn_post, post_rows, post_params, 2, [[dy_a]], name="rwkv_post_bwd", tm=128, row_grad=[F32] * 5)
    dr_s, dlw, dk2_s, dv_s, dkk, da = _wkv_bwd(z_rkv, lw, k2, kk, a, states, invs, dy, seq)
    (dzk, dzw, dza, dzg), pg = _rowwise_bwd(
        _fn_prep, prep_rows, prep_params, 2, [[dlw], [dk2_p, dk2_s], [dkk], [da], [dg]],
        name="rwkv_prep_bwd", tm=128, row_grad=[F32] * 4)
    gw["decay_base"], gd_up, gw["iclr_base"], gi_up, gg_up, gw["key_norm_scale"], gw["key_iclr_scale"] = pg
    gw["decay_up"], gw["iclr_up"], gw["gate_up"] = gd_up[:LORA_W], gi_up[:LORA_A], gg_up[:LORA_G]
    dp_r, gmix_r = _shift_bwd(proj, 0, D, mix_rkv[:, :D], [dr_p, dr_s], seq, name="shift_r_bwd")
    dp_k, gmix_k = _shift_bwd(proj, D, D, mix_rkv[:, D:2 * D], [dzk], seq, name="shift_k_bwd")
    dp_v, gmix_v = _shift_bwd(proj, 2 * D, D, mix_rkv[:, 2 * D:], [dv_p, dv_s], seq, name="shift_v_bwd")
    dp_lora, gmix_lora = _shift_bwd(proj, C_LORA, 512, mix_lora, [jnp.concatenate([dzw, dza, dzg], axis=1)], seq,
                                    name="shift_lora_bwd")
    gw["shift_mix"] = jnp.concatenate([gmix_r, gmix_k, gmix_v, _unpad_lora(gmix_lora)], axis=1)
    dproj = [dp_r, dp_k, dp_v, dq, dk, dv, dzga, dzgb, dp_lora]
    gw["w_in_p"] = _mm_cat_tn(dproj, h1, name="mm_in_dw", after=gw["rel_bias"])
    token = comm.send_late(gw)
    dh1 = _mm_cat_nn(dproj, w_in, name="mm_in_dx", after=token)
    (grad_x,), (gw["g_pre_mix"],) = _rowwise_bwd(_fn_pre, [_win(x)], [g_pre_mix], 0, [[dh1]], name="pre_mix_bwd",
                                                 tm=256, row_grad=[F32], add_to={0: dx0})
    return loss, grad_x, gw


_COL_SHARDED = ("w_in", "decay_up", "iclr_up", "gate_up", "w_o_mem", "w_ffn_in")
_ROW_SHARDED = ("w_branch_a", "w_branch_b", "w_out", "w_q_mem", "w_kv_mem", "w_ffn_out")
_TRANSPOSED = ("w_in", "w_ffn_in")
_FIRST = ("w_in", "decay_up", "iclr_up", "gate_up")
_REST = ("w_o_mem", "w_ffn_in", "w_branch_a", "w_branch_b", "w_out", "w_q_mem", "w_kv_mem", "w_ffn_out")
_REPLICATED = ("g_pre_mix", "g_post_mix", "shift_mix", "decay_base", "iclr_base", "key_norm_scale", "key_iclr_scale",
               "bonus_scale", "lnx_w", "lnx_b", "rel_bias", "g_pre_cross", "g_post_cross", "g_mem", "g_pre_ffn",
               "g_post_ffn")
_WEIGHTS = ("g_pre_mix", "g_post_mix", "w_in", "shift_mix", "decay_base", "decay_up", "iclr_base", "iclr_up", "gate_up",
            "key_norm_scale", "key_iclr_scale", "bonus_scale", "lnx_w", "lnx_b", "rel_bias", "w_branch_a", "w_branch_b",
            "w_out", "g_pre_cross", "g_post_cross", "g_mem", "w_q_mem", "w_kv_mem", "w_o_mem", "g_pre_ffn", "g_post_ffn",
            "w_ffn_in", "w_ffn_out")
_PACK_ROWS = 8 * ((sum({"shift_mix": 3360, "bonus_scale": 1024, "rel_bias": 3072}.get(n, D) for n in _REPLICATED)
                   + 1 + 8 * LANE - 1) // (8 * LANE))


def _pack(vals):
    flat = jnp.concatenate([v.reshape(-1).astype(F32) for v in vals])
    return jnp.pad(flat, (0, _PACK_ROWS * LANE - flat.shape[0])).reshape(_PACK_ROWS, LANE)


def _unpack(packed, shapes):
    flat, out, pos = packed.reshape(-1), [], 0
    for s in shapes:
        n = math.prod(s)
        out.append(flat[pos:pos + n].reshape(s))
        pos += n
    return out


def _step(args, seq, n_mem):
    names = ("x", "mem") + _WEIGHTS + ("loss_target",) + tuple("m_" + n for n in _WEIGHTS) + tuple("v_" + n for n in _WEIGHTS)
    given = dict(zip(names, args))
    nb = given["x"].shape[0]
    x = given["x"].reshape(nb * seq, D)
    mem = given["mem"].reshape(nb * n_mem, D)
    target = given["loss_target"].reshape(nb * seq, D)
    def local(name, prefix=""):
        a = given[prefix + name][0]
        return a.T if name in _TRANSPOSED else a

    shard = {n: local(n) for n in _COL_SHARDED + _ROW_SHARDED}
    stacked = _ROW_SHARDED + _TRANSPOSED
    out = {}

    def wire(name):
        return shard[name].astype(BF16)

    def full(name, g):
        return g.reshape(-1, g.shape[-1]) if name in stacked else _cols_to_full(g)

    def blocks_of(name, g):
        return (g.reshape((N_DEV,) + shard[name].shape) if name in stacked else _full_to_cols(g)).astype(BF16)

    def update(names, landed, after=None):
        done = []
        for n, parts in zip(names, landed):
            res = _adamw(parts, shard[n], local(n, "m_"), local(n, "v_"), name="adamw_" + n, after=after)
            for kind, r in zip(("grad_", "delta_", "new_m_", "new_v_"), res):
                out[kind + n] = (r.T if n in _TRANSPOSED else r)[None]
            done.append(res[0])
        return done


    class Exchanges:
        def __init__(self):
            srcs = [wire(n) for n in _FIRST]
            self.first, self.begun = _exchange_start(srcs, False, srcs[0], name="gather_first_start",
                                                     dists=_SIBLING_AND_SAME_CORE)

        def first_weights(self, after):
            got = _exchange_wait(self.first, after, [wire(n) for n in _FIRST], name="gather_first_wait")
            relayed = _relay_to_sibling(got, name="gather_first_relay")
            pos = _mesh_pos()
            for j, d in enumerate(_OTHER_CHIPS):
                slot = _flat(_peer(pos, d | 1))
                got = [lax.dynamic_update_slice_in_dim(g, r[j][None], slot, 0) for g, r in zip(got, relayed)]
            self.rest, self.first_token = _exchange_start(
                [wire(n) for n in _REST], False, got[0], name="gather_rest_start")
            first = {n: full(n, g) for n, g in zip(_FIRST, got)}
            first["w_in_p"] = _permute_in(first.pop("w_in"), 0)
            return first

        def late_weights(self, after):
            got = _exchange_wait(self.rest, [after], [wire(n) for n in _REST], name="gather_rest_wait")
            return {n: full(n, g) for n, g in zip(_REST, got)}

        def send_early(self, gw):
            self.early_blocks = [blocks_of(n, gw[n]) for n in _REST]
            self.early, token = _exchange_start(self.early_blocks, True, self.early_blocks[-1], name="scatter_rest_start")
            return token

        def send_late(self, gw):
            me = _flat(_mesh_pos())
            own = [lax.dynamic_index_in_dim(b, me, 0, keepdims=False) for b in self.early_blocks]
            landed = _exchange_wait(self.early, [gw["w_in_p"]], own, name="scatter_rest_wait")
            grads = {**gw, "w_in": _unpermute_in(gw["w_in_p"], 0)}
            self.late_blocks = [blocks_of(n, grads[n]) for n in _FIRST]
            self.late, token = _exchange_start(self.late_blocks, True, landed[0], name="scatter_first_start")
            self.updated = update(_REST, landed, after=token)
            return token

        def finish(self, after):
            me = _flat(_mesh_pos())
            own = [lax.dynamic_index_in_dim(b, me, 0, keepdims=False) for b in self.late_blocks]
            update(_FIRST, _exchange_wait(self.late, [*after, *self.updated], own, name="scatter_first_wait"))

    comm = Exchanges()
    wt = {n: given[n][0] for n in _REPLICATED}
    loss_tile, grad_x, gw = _local_step(x, mem, target, wt, seq, n_mem, comm)
    rep_shapes = [given[n].shape for n in _REPLICATED]
    packed, _ = lax.optimization_barrier((_pack([gw[n] for n in _REPLICATED] + [loss_tile[0, 0]]), tuple(comm.updated)))
    small = _exchange([packed], False, name="gather_small")[0]
    zero = jnp.zeros((), F32)
    res = _adamw(small, *[_pack([given[p + n] for n in _REPLICATED] + [zero]) for p in ("", "m_", "v_")],
                 name="adamw_small", tr=_PACK_ROWS)
    for kind, r in zip(("grad_", "delta_", "new_m_", "new_v_"), res):
        for n, val in zip(_REPLICATED, _unpack(r, rep_shapes)):
            out[kind + n] = val
    loss = res[0].reshape(-1)[sum(math.prod(s) for s in rep_shapes)]
    comm.finish([grad_x, res[0]])
    grad_x = grad_x.reshape(nb, seq, D)
    return (loss, grad_x, *[out[k + n] for k in ("grad_", "delta_", "new_m_", "new_v_") for n in _WEIGHTS])


def kernel(x, mem, g_pre_mix, g_post_mix, w_in, shift_mix, decay_base, decay_up, iclr_base, iclr_up, gate_up, key_norm_scale, key_iclr_scale, bonus_scale, lnx_w, lnx_b, rel_bias, w_branch_a, w_branch_b, w_out, g_pre_cross, g_post_cross, g_mem, w_q_mem, w_kv_mem, w_o_mem, g_pre_ffn, g_post_ffn, w_ffn_in, w_ffn_out, loss_target, m_g_pre_mix, m_g_post_mix, m_w_in, m_shift_mix, m_decay_base, m_decay_up, m_iclr_base, m_iclr_up, m_gate_up, m_key_norm_scale, m_key_iclr_scale, m_bonus_scale, m_lnx_w, m_lnx_b, m_rel_bias, m_w_branch_a, m_w_branch_b, m_w_out, m_g_pre_cross, m_g_post_cross, m_g_mem, m_w_q_mem, m_w_kv_mem, m_w_o_mem, m_g_pre_ffn, m_g_post_ffn, m_w_ffn_in, m_w_ffn_out, v_g_pre_mix, v_g_post_mix, v_w_in, v_shift_mix, v_decay_base, v_decay_up, v_iclr_base, v_iclr_up, v_gate_up, v_key_norm_scale, v_key_iclr_scale, v_bonus_scale, v_lnx_w, v_lnx_b, v_rel_bias, v_w_branch_a, v_w_branch_b, v_w_out, v_g_pre_cross, v_g_post_cross, v_g_mem, v_w_q_mem, v_w_kv_mem, v_w_o_mem, v_g_pre_ffn, v_g_post_ffn, v_w_ffn_in, v_w_ffn_out):
    args = (x, mem, g_pre_mix, g_post_mix, w_in, shift_mix, decay_base, decay_up, iclr_base, iclr_up, gate_up, key_norm_scale, key_iclr_scale, bonus_scale, lnx_w, lnx_b, rel_bias, w_branch_a, w_branch_b, w_out, g_pre_cross, g_post_cross, g_mem, w_q_mem, w_kv_mem, w_o_mem, g_pre_ffn, g_post_ffn, w_ffn_in, w_ffn_out, loss_target, m_g_pre_mix, m_g_post_mix, m_w_in, m_shift_mix, m_decay_base, m_decay_up, m_iclr_base, m_iclr_up, m_gate_up, m_key_norm_scale, m_key_iclr_scale, m_bonus_scale, m_lnx_w, m_lnx_b, m_rel_bias, m_w_branch_a, m_w_branch_b, m_w_out, m_g_pre_cross, m_g_post_cross, m_g_mem, m_w_q_mem, m_w_kv_mem, m_w_o_mem, m_g_pre_ffn, m_g_post_ffn, m_w_ffn_in, m_w_ffn_out, v_g_pre_mix, v_g_post_mix, v_w_in, v_shift_mix, v_decay_base, v_decay_up, v_iclr_base, v_iclr_up, v_gate_up, v_key_norm_scale, v_key_iclr_scale, v_bonus_scale, v_lnx_w, v_lnx_b, v_rel_bias, v_w_branch_a, v_w_branch_b, v_w_out, v_g_pre_cross, v_g_post_cross, v_g_mem, v_w_q_mem, v_w_kv_mem, v_w_o_mem, v_g_pre_ffn, v_g_post_ffn, v_w_ffn_in, v_w_ffn_out)
    return _step(args, x.shape[1], mem.shape[1])
```

```python
import functools
import math

import jax
import jax.numpy as jnp
from jax import lax
from jax.experimental import pallas as pl
from jax.experimental.pallas import tpu as pltpu

F32 = jnp.float32
BF16 = jnp.bfloat16

N_DEV = 8
D = 1024
HEAD = 64
N_HEADS = D // HEAD
LANE = 128
N_PAIRS = D // LANE
CHUNK = 64
LEFT = 8 * CHUNK
BAND = LEFT + CHUNK
REL_CLIP = 128
REL_TABLE = CHUNK + REL_CLIP
MEM_WIDTH = D // 2
MEM_HEADS = 4
FFN = 2816
LORA_W, LORA_A, LORA_G = 64, 64, 160
P_WIDTH = 3 * D + 3 * D + 2 * D + 128 + 128 + 256
C_Q, C_GA, C_LORA = 3 * D, 6 * D, 8 * D
NORM_EPS = 1e-6
GROUP_NORM_EPS = 64e-5
MASK_VALUE = -1e30
ADAM_LR, ADAM_B1, ADAM_B2, ADAM_EPS, ADAM_WD, ADAM_STEP = 0.001, 0.9, 0.999, 1e-08, 0.01, 10
VMEM_LIMIT = 56 * 1024 * 1024


def _cp(*sem):
    return pltpu.CompilerParams(dimension_semantics=sem, vmem_limit_bytes=VMEM_LIMIT)


_NN, _NT, _TN = ((1,), (0,)), ((1,), (1,)), ((0,), (0,))


def _dot_raw(a, b, dims):
    return lax.dot_general(a.astype(BF16), b.astype(BF16), (dims, ((), ())), preferred_element_type=F32)


@functools.partial(jax.custom_vjp, nondiff_argnums=(2,))
def _dot_dims(a, b, dims):
    return _dot_raw(a, b, dims)


def _dot_dims_fwd(a, b, dims):
    return _dot_raw(a, b, dims), (a, b)


def _dot_dims_bwd(dims, res, g):
    a, b = res
    if dims == _NN:
        da, db = _dot_raw(g, b, _NT), _dot_raw(a, g, _TN)
    elif dims == _NT:
        da, db = _dot_raw(g, b, _NN), _dot_raw(g, a, _TN)
    else:
        da, db = _dot_raw(b, g, _NT), _dot_raw(a, g, _NN)
    return da.astype(a.dtype), db.astype(b.dtype)


_dot_dims.defvjp(_dot_dims_fwd, _dot_dims_bwd)


def _dot(a, b, dims=_NN):
    return _dot_dims(a, b, dims)


def _dot_nt(a, b):
    return _dot_dims(a, b, _NT)


def _dot_tn(a, b):
    return _dot_dims(a, b, _TN)


def _split(x, terms):
    parts, rest = [], x.astype(F32)
    for _ in range(terms):
        p = rest.astype(BF16)
        parts.append(p)
        rest = rest - p.astype(F32)
    return parts


def _dot_split_a(a, b, terms=2):
    out = None
    for p in _split(a, terms):
        t = _dot(p, b)
        out = t if out is None else out + t
    return out


def _dot_split_b(a, b, terms=3):
    out = None
    for p in _split(b, terms):
        t = _dot(a, p)
        out = t if out is None else out + t
    return out


MM_VMEM_BUDGET = 30 * 1024 * 1024
MM_HBM_BPS = 3.2e12
MM_MXU_FPS = 8.5e14
MM_STEP_S = 0.35e-6


def _divisors(n, align, cap):
    out = [d for d in range(align, min(n, cap) + 1, align) if n % d == 0]
    return out or [n]


def _mm_tiles(m, n, k, ea, eb, eo, ta):
    best = None
    for tm in _divisors(m, LANE if ta else 8, 2048):
        for tn in _divisors(n, LANE, 2048):
            for tk in _divisors(k, LANE, 2048):
                nk = k // tk
                vmem = 2 * (tm * tk * ea + tk * tn * eb + tm * tn * eo) + (tm * tn * 4 if nk > 1 else 0)
                if vmem > MM_VMEM_BUDGET:
                    continue
                dma = (tm * tk * ea if (nk > 1 or n // tn == 1) else tm * tk * ea * tn / n) + tk * tn * eb + tm * tn * eo / nk
                step = max(2.0 * tm * tn * tk / MM_MXU_FPS, dma / MM_HBM_BPS) + MM_STEP_S
                cost = (m // tm) * (n // tn) * nk * step
                if best is None or cost < best[0]:
                    best = (cost, tm, tn, tk)
    return best[1:]


def _mm(a, b, *, name, ta=False, tb=False, out_dtype=F32, tm=None, tn=None, tk=None, split_a=1, after=None):
    m, k = (a.shape[1], a.shape[0]) if ta else a.shape
    n, kb = (b.shape[0], b.shape[1]) if tb else (b.shape[1], b.shape[0])
    assert k == kb, (a.shape, b.shape, ta, tb)
    if tm is None:
        tm, tn, tk = _mm_tiles(m, n, k, a.dtype.itemsize, b.dtype.itemsize, jnp.dtype(out_dtype).itemsize, ta)
    assert m % tm == 0 and n % tn == 0 and k % tk == 0, (m, n, k, tm, tn, tk)
    nk = k // tk
    dims = ((0 if ta else 1,), (1 if tb else 0,))

    n_after = 0 if after is None else 1

    def body(a_ref, b_ref, *rest):
        o_ref, scratch = rest[n_after], rest[n_after + 1:]
        prod = None
        for p in _split(a_ref[...], split_a) if split_a > 1 else [a_ref[...]]:
            t = _dot_raw(p, b_ref[...], dims)
            prod = t if prod is None else prod + t
        if nk == 1:
            o_ref[...] = prod.astype(o_ref.dtype)
            return
        acc_ref, kk = scratch[0], pl.program_id(2)

        @pl.when(kk == 0)
        def _():
            acc_ref[...] = prod

        @pl.when(kk > 0)
        def _():
            acc_ref[...] += prod

        @pl.when(kk == nk - 1)
        def _():
            o_ref[...] = acc_ref[...].astype(o_ref.dtype)

    a_spec = pl.BlockSpec((tk, tm), lambda i, j, q: (q, i)) if ta else pl.BlockSpec((tm, tk), lambda i, j, q: (i, q))
    b_spec = pl.BlockSpec((tn, tk), lambda i, j, q: (j, q)) if tb else pl.BlockSpec((tk, tn), lambda i, j, q: (q, j))
    return pl.pallas_call(
        body, name=name, grid=(m // tm, n // tn, nk),
        in_specs=[a_spec, b_spec] + [pl.BlockSpec(memory_space=pl.ANY)] * n_after,
        out_specs=pl.BlockSpec((tm, tn), lambda i, j, q: (i, j)),
        out_shape=jax.ShapeDtypeStruct((m, n), out_dtype),
        scratch_shapes=[pltpu.VMEM((tm, tn), F32)] if nk > 1 else [],
        compiler_params=_cp("parallel", "parallel", "arbitrary"),
    )(a, b, *([] if after is None else [after]))


def _piece_steps(pieces, tile):
    counts = [p.shape[1] // tile for p in pieces]
    assert all(p.shape[1] % tile == 0 for p in pieces)
    return [(sum(counts[:i]), c) for i, c in enumerate(counts)], sum(counts)


def _mm_cat_nn(pieces, w, *, name, after=None, tm=2048, tk=256):
    t, n = pieces[0].shape[0], w.shape[1]
    tm = min(tm, t)
    spans, nk = _piece_steps(pieces, tk)
    npc = len(pieces)
    n_after = 0 if after is None else 1

    def body(*refs):
        w_ref, o_ref, acc_ref = refs[npc], refs[npc + 1 + n_after], refs[npc + 2 + n_after]
        q = pl.program_id(1)

        @pl.when(q == 0)
        def _():
            acc_ref[...] = jnp.zeros_like(acc_ref)

        for p_ref, (first, count) in zip(refs[:npc], spans):
            @pl.when(jnp.logical_and(q >= first, q < first + count))
            def _(p_ref=p_ref):
                acc_ref[...] += _dot_raw(p_ref[...], w_ref[...], _NN)

        @pl.when(q == nk - 1)
        def _():
            o_ref[...] = acc_ref[...].astype(o_ref.dtype)

    def piece_spec(first, count):
        return pl.BlockSpec((tm, tk), lambda i, q: (i, jnp.clip(q - first, 0, count - 1)))

    return pl.pallas_call(
        body, name=name, grid=(t // tm, nk),
        in_specs=[piece_spec(*s) for s in spans] + [pl.BlockSpec((tk, n), lambda i, q: (q, 0))]
        + [pl.BlockSpec(memory_space=pl.ANY)] * n_after,
        out_specs=pl.BlockSpec((tm, n), lambda i, q: (i, 0)),
        out_shape=jax.ShapeDtypeStruct((t, n), BF16),
        scratch_shapes=[pltpu.VMEM((tm, n), F32)],
        compiler_params=_cp("parallel", "arbitrary"),
    )(*pieces, w, *([] if after is None else [after]))


def _mm_cat_tn(pieces, a, *, name, after=None, tk=1024, tn=512):
    t, m = a.shape
    tk = min(tk, t)
    spans, nj = _piece_steps(pieces, tn)
    npc, nk = len(pieces), t // tk
    n_after = 0 if after is None else 1

    def body(a_ref, *refs):
        o_ref, acc_ref = refs[npc + n_after], refs[npc + 1 + n_after]
        j, q = pl.program_id(0), pl.program_id(1)

        @pl.when(q == 0)
        def _():
            acc_ref[...] = jnp.zeros_like(acc_ref)

        for p_ref, (first, count) in zip(refs[:npc], spans):
            @pl.when(jnp.logical_and(j >= first, j < first + count))
            def _(p_ref=p_ref):
                acc_ref[...] += _dot_raw(p_ref[...], a_ref[...], _TN)

        @pl.when(q == nk - 1)
        def _():
            o_ref[...] = acc_ref[...].astype(o_ref.dtype)

    def piece_spec(first, count):
        def index(j, q):
            mine = jnp.logical_and(j >= first, j < first + count)
            return jnp.where(mine, q, 0), jnp.clip(j - first, 0, count - 1)
        return pl.BlockSpec((tk, tn), index)

    return pl.pallas_call(
        body, name=name, grid=(nj, nk),
        in_specs=[pl.BlockSpec((tk, m), lambda j, q: (q, 0))] + [piece_spec(*s) for s in spans]
        + [pl.BlockSpec(memory_space=pl.ANY)] * n_after,
        out_specs=pl.BlockSpec((tn, m), lambda j, q: (j, 0)),
        out_shape=jax.ShapeDtypeStruct((nj * tn, m), BF16),
        scratch_shapes=[pltpu.VMEM((tn, m), F32)],
        compiler_params=_cp("parallel", "arbitrary"),
    )(a, *pieces, *([] if after is None else [after]))


def _win(arr, start=0, width=None):
    width = arr.shape[1] if width is None else width
    assert start % width == 0
    return (arr, start // width, width)


def _row_specs(rows, tm):
    return [pl.BlockSpec((tm, w), functools.partial(lambda i, cb: (i, cb), cb=cb)) for (_, cb, w) in rows]


def _full_spec(p):
    nd = p.ndim
    return pl.BlockSpec(p.shape, lambda i, nd=nd: (0,) * nd)


def _rowwise(fn, rows, params, outs, *, name, tm, after=None):
    t = rows[0][0].shape[0]
    tm = min(tm, t)
    assert t % tm == 0
    nr, npar = len(rows), len(params)
    n_after = 0 if after is None else 1

    def body(*refs):
        vals = [r[...] for r in refs[:nr + npar]]
        res = fn(*vals)
        for o_ref, r in zip(refs[nr + npar + n_after:], res):
            o_ref[...] = r.astype(o_ref.dtype)

    return pl.pallas_call(
        body, name=name, grid=(t // tm,),
        in_specs=_row_specs(rows, tm) + [_full_spec(p) for p in params] + [pl.BlockSpec(memory_space=pl.ANY)] * n_after,
        out_specs=[pl.BlockSpec((tm, w), lambda i: (i, 0)) for (w, _) in outs],
        out_shape=[jax.ShapeDtypeStruct((t, w), dt) for (w, dt) in outs],
        compiler_params=_cp("parallel"),
    )(*[r[0] for r in rows], *params, *([] if after is None else [after]))


def _rowwise_bwd(fn, rows, params, n_const, cots, *, name, tm, row_grad, add_to=None, packed=False):
    t = rows[0][0].shape[0]
    tm = min(tm, t)
    assert t % tm == 0
    nr, npar = len(rows), len(params)
    ndp = npar - n_const
    add_to = add_to or {}
    add_idx = sorted(add_to)
    flat_cots = [c for group in cots for c in group]
    kept = [i for i in range(nr) if row_grad[i] is not None]

    def body(*refs):
        pos = 0
        row_v = [r[...] for r in refs[pos:pos + nr]]; pos += nr
        par_v = [r[...] for r in refs[pos:pos + npar]]; pos += npar
        cot_v = [r[...] for r in refs[pos:pos + len(flat_cots)]]; pos += len(flat_cots)
        add_v = [r[...] for r in refs[pos:pos + len(add_idx)]]; pos += len(add_idx)
        if packed:
            offs = [sum(rows[i][2] for i in kept[:q]) for q in range(len(kept))]
            rg_refs = [refs[pos].at[:, o:o + rows[i][2]] for o, i in zip(offs, kept)]; pos += 1
        else:
            rg_refs = refs[pos:pos + len(kept)]; pos += len(kept)
        pg_refs = refs[pos:pos + ndp]

        consts = par_v[ndp:]
        res, vjp = jax.vjp(lambda *args: tuple(fn(*args, *consts)), *row_v, *par_v[:ndp])
        cot_in, q = [], 0
        for j, group in enumerate(cots):
            c = None
            for _ in group:
                cv = cot_v[q].astype(F32); q += 1
                c = cv if c is None else c + cv
            c = jnp.zeros(res[j].shape, F32) if c is None else c
            cot_in.append(c.astype(res[j].dtype))
        grads = vjp(tuple(cot_in))
        for ref, i in zip(rg_refs, kept):
            g = grads[i].astype(F32)
            if i in add_to:
                g = g + add_v[add_idx.index(i)].astype(F32)
            ref[...] = g.astype(ref.dtype)

        @pl.when(pl.program_id(0) == 0)
        def _():
            for ref in pg_refs:
                ref[...] = jnp.zeros_like(ref)

        for ref, g in zip(pg_refs, grads[nr:]):
            ref[...] += g.astype(F32)

    cot_specs = [pl.BlockSpec((tm, c.shape[1]), lambda i: (i, 0)) for c in flat_cots]
    add_specs = [pl.BlockSpec((tm, add_to[i].shape[1]), lambda i_: (i_, 0)) for i in add_idx]
    widths = [sum(rows[i][2] for i in kept)] if packed else [rows[i][2] for i in kept]
    n_rg = len(widths)
    out_specs = [pl.BlockSpec((tm, w), lambda i_: (i_, 0)) for w in widths] + [_full_spec(p) for p in params[:ndp]]
    out_shape = [jax.ShapeDtypeStruct((t, w), row_grad[kept[q]]) for q, w in enumerate(widths)] + [
        jax.ShapeDtypeStruct(p.shape, F32) for p in params[:ndp]]
    res = pl.pallas_call(
        body, name=name, grid=(t // tm,),
        in_specs=_row_specs(rows, tm) + [_full_spec(p) for p in params] + cot_specs + add_specs,
        out_specs=out_specs, out_shape=out_shape,
        compiler_params=_cp("arbitrary"),
    )(*[r[0] for r in rows], *params, *flat_cots, *[add_to[i] for i in add_idx])
    return list(res[:n_rg]), list(res[n_rg:])


def _rms(x, g):
    xf = x.astype(F32)
    return xf * lax.rsqrt(jnp.mean(xf * xf, axis=-1, keepdims=True) + NORM_EPS) * g


def _softplus(x):
    return jnp.maximum(x, 0.0) + jnp.log(1.0 + jnp.exp(-jnp.abs(x)))


def _fn_pre(x, g):
    return (_rms(x, g).astype(BF16),)


def _fn_res(x, u, g_post):
    return (x + _rms(u, g_post),)


def _fn_res_pre(x, u, g_post, g_pre):
    xn = x + _rms(u, g_post)
    return xn, _rms(xn, g_pre).astype(BF16)


def _fn_mix(zga, zgb, ya, yb):
    return ((jax.nn.sigmoid(zga) * ya + jax.nn.sigmoid(zgb) * yb).astype(BF16),)


def _fn_swiglu(gate, up):
    gate, up = gate.astype(F32), up.astype(F32)
    return ((gate * jax.nn.sigmoid(gate) * up).astype(BF16),)


def _fn_prep(zk, zw, za, zg, decay_base, d_up, iclr_base, i_up, g_up, kns, kis, e_hd, e_dh):
    w_log = -_softplus(-(decay_base + _dot(jnp.tanh(zw), d_up))) - 0.5
    lw = -jnp.exp(w_log)
    a = jax.nn.sigmoid(iclr_base + _dot(za, i_up))
    g = _dot(jax.nn.sigmoid(zg), g_up)
    kn = zk * kns
    ss = _dot(kn * kn, e_dh)
    inv = lax.rsqrt(jnp.maximum(ss, 1e-24))
    kk = kn * _dot_split_a(inv, e_hd)
    k2 = zk * (1.0 + (a - 1.0) * kis)
    return lw, k2, kk, a, g


def _fn_post(y, r, k2, v, g, lnx_w, lnx_b, bonus, e_hd, e_dh):
    mu = _dot_split_a(_dot(y, e_dh) * (1.0 / HEAD), e_hd)
    yc = y - mu
    var = _dot(yc * yc, e_dh) * (1.0 / HEAD)
    yn = yc * _dot_split_a(lax.rsqrt(var + GROUP_NORM_EPS), e_hd)
    bs = _dot_split_a(_dot(r * k2 * bonus, e_dh), e_hd)
    return (((yn * lnx_w + lnx_b + bs * v) * g).astype(BF16),)


def _shift_fwd(p, col0, ncols, mix, seq, *, name, cw=256):
    t = p.shape[0]
    assert col0 % cw == 0 and ncols % cw == 0 and t % seq == 0
    cb0 = col0 // cw

    def body(p_ref, m_ref, z_ref):
        pv = p_ref[...]
        row = lax.broadcasted_iota(jnp.int32, pv.shape, 0)
        prev = jnp.where(row == 0, 0.0, pltpu.roll(pv, 1, axis=0))
        z_ref[...] = pv + (prev - pv) * m_ref[...]

    return pl.pallas_call(
        body, name=name, grid=(t // seq, ncols // cw),
        in_specs=[pl.BlockSpec((seq, cw), lambda b, c: (b, c + cb0)), pl.BlockSpec((1, cw), lambda b, c: (0, c))],
        out_specs=pl.BlockSpec((seq, cw), lambda b, c: (b, c)),
        out_shape=jax.ShapeDtypeStruct((t, ncols), F32),
        compiler_params=_cp("parallel", "parallel"),
    )(p, mix)


def _shift_bwd(p, col0, ncols, mix, dz_parts, seq, *, name, cw=256):
    t = p.shape[0]
    cb0 = col0 // cw
    n = len(dz_parts)

    def body(*refs):
        p_ref, m_ref = refs[:2]
        dp_ref, dm_ref = refs[2 + n:]
        dz = refs[2][...].astype(F32)
        for r in refs[3:2 + n]:
            dz = dz + r[...].astype(F32)
        pv = p_ref[...]
        mixv = m_ref[...]
        row = lax.broadcasted_iota(jnp.int32, pv.shape, 0)
        prev = jnp.where(row == 0, 0.0, pltpu.roll(pv, 1, axis=0))
        u = dz * mixv
        nxt = jnp.where(row == seq - 1, 0.0, pltpu.roll(u, seq - 1, axis=0))
        dp_ref[...] = (dz - u + nxt).astype(dp_ref.dtype)

        @pl.when(pl.program_id(1) == 0)
        def _():
            dm_ref[...] = jnp.zeros_like(dm_ref)

        dm_ref[...] += jnp.sum(dz * (prev - pv), axis=0, keepdims=True)

    return pl.pallas_call(
        body, name=name, grid=(ncols // cw, t // seq),
        in_specs=[pl.BlockSpec((seq, cw), lambda c, b: (b, c + cb0)), pl.BlockSpec((1, cw), lambda c, b: (0, c))]
        + [pl.BlockSpec((seq, cw), lambda c, b: (b, c))] * n,
        out_specs=[pl.BlockSpec((seq, cw), lambda c, b: (b, c)), pl.BlockSpec((1, cw), lambda c, b: (0, c))],
        out_shape=[jax.ShapeDtypeStruct((t, ncols), BF16), jax.ShapeDtypeStruct((1, ncols), F32)],
        compiler_params=_cp("parallel", "arbitrary"),
    )(p, mix, *dz_parts)


def _each(f, *lists):
    return [f(*xs) for xs in zip(*lists)]


def _tri_inv(low):
    c = low[0].shape[0]
    ti = lax.broadcasted_iota(jnp.int32, (c, c), 0)
    si = lax.broadcasted_iota(jnp.int32, (c, c), 1)
    eye = (ti == si).astype(F32)
    inside = (ti // 4) == (si // 4)
    base = [jnp.where(inside, m, 0.0) for m in low]
    acc = _each(lambda m: _dot(eye - m, eye + _dot(m, m)), base)
    size = 8
    while size <= c:
        wider = (ti // size) == (si // size)
        keep = jnp.logical_and(wider, jnp.logical_not(inside))
        acc = _each(lambda p, m: p - _dot(_dot(p, jnp.where(keep, m, 0.0)), p), acc, low)
        inside, size = wider, size * 2
    return acc


def _stack_rows(a, b):
    return jnp.concatenate([a, b], axis=0)


@jax.custom_vjp
def _split_rows(x):
    h = x.shape[0] // 2
    return x[:h], x[h:]


def _split_rows_fwd(x):
    return _split_rows(x), None


def _split_rows_bwd(_, g):
    return (jnp.concatenate(g, axis=0),)


_split_rows.defvjp(_split_rows_fwd, _split_rows_bwd)


def _masked_halves(stacked, top_mask, bottom_mask):
    halves = _each(_split_rows, stacked)
    return ([jnp.where(top_mask, t, 0.0) for t, _ in halves], [jnp.where(bottom_mask, b, 0.0) for _, b in halves])


@jax.custom_vjp
def _tri_inv_known(low, inv):
    return inv


def _tri_inv_known_fwd(low, inv):
    return inv, inv


def _tri_inv_known_bwd(inv, g):
    dlow = _each(lambda t, gg: -_dot(_dot(t, gg, _TN), t, _NT), inv, g)
    return dlow, _each(jnp.zeros_like, inv)


_tri_inv_known.defvjp(_tri_inv_known_fwd, _tri_inv_known_bwd)


def _wkv_chunk(s0, r, lw, k, v, kk, a, inv=None):
    c = r[0].shape[0]
    ti = lax.broadcasted_iota(jnp.int32, (c, c), 0)
    si = lax.broadcasted_iota(jnp.int32, (c, c), 1)
    incl, strict = ti >= si, ti > si
    tri = incl.astype(F32)
    cum = _each(lambda x: _dot_split_b(tri, x, 3), lw)
    eg = _each(jnp.exp, cum)
    egp = _each(lambda cs, x: jnp.exp(cs - x), cum, lw)
    ei = _each(lambda cs: jnp.exp(-cs), cum)
    rh, kkh, kt = _each(jnp.multiply, r, eg), _each(jnp.multiply, kk, egp), _each(jnp.multiply, k, ei)
    bt = _each(lambda p, q, e: (p * q) * e, a, kk, ei)
    both = _each(_stack_rows, kkh, rh)
    on_b, on_k, on_s = _each(_dot_nt, both, bt), _each(_dot_nt, both, kt), _each(_dot_nt, both, s0)
    lb, mb = _masked_halves(on_b, strict, incl)
    lk, mk = _masked_halves(on_k, strict, incl)
    on_s = _each(_split_rows, on_s)
    on_v = _each(lambda p, q, x: _split_rows(_dot(_stack_rows(p, q), x)), lk, mk, v)
    rhs = _each(lambda p, q: p[0] + q[0], on_s, on_v)
    inv = _tri_inv(lb) if inv is None else _tri_inv_known(lb, inv)
    u = _each(lambda t, x: -_dot(t, x), inv, rhs)
    y = _each(lambda p, m1, uu, q: p[1] + _dot(m1, uu) + q[1], on_s, mb, u, on_v)
    s1 = _each(lambda s, uu, x, b, kq, w: (s + _dot_tn(_stack_rows(uu, x), _stack_rows(b, kq)))
               * jnp.exp(jnp.sum(w, axis=0, keepdims=True)), s0, u, v, bt, kt, lw)
    return y, s1, inv


WKV_HEADS = 16
WKV_COLS = WKV_HEADS * HEAD
WKV_GROUPS = N_HEADS // WKV_HEADS


def _head_cols(ref):
    return [ref[:, h * HEAD:(h + 1) * HEAD] for h in range(ref.shape[1] // HEAD)]


def _wkv_specs(seq, rev):
    nc = seq // CHUNK

    def rows(col0):
        cb0 = col0 // WKV_COLS
        if rev:
            return pl.BlockSpec((CHUNK, WKV_COLS), lambda b, h, c: (b * nc + nc - 1 - c, cb0 + h))
        return pl.BlockSpec((CHUNK, WKV_COLS), lambda b, h, c: (b * nc + c, cb0 + h))

    if rev:
        st = pl.BlockSpec((1, 1, WKV_HEADS, HEAD, HEAD), lambda b, h, c: (b * WKV_GROUPS + h, nc - 1 - c, 0, 0, 0))
    else:
        st = pl.BlockSpec((1, 1, WKV_HEADS, HEAD, HEAD), lambda b, h, c: (b * WKV_GROUPS + h, c, 0, 0, 0))
    return rows, st


def _wkv_fwd(z_rkv, lw, k2, kk, a, seq):
    t = z_rkv.shape[0]
    nb, nc = t // seq, seq // CHUNK
    rows, st = _wkv_specs(seq, False)

    def body(r_ref, v_ref, lw_ref, k_ref, kk_ref, a_ref, y_ref, st_ref, inv_ref, s_scr):
        @pl.when(pl.program_id(2) == 0)
        def _():
            s_scr[...] = jnp.zeros_like(s_scr)

        s0 = [s_scr[h] for h in range(WKV_HEADS)]
        y, s1, inv = _wkv_chunk(s0, *[_head_cols(ref) for ref in (r_ref, lw_ref, k_ref, v_ref, kk_ref, a_ref)])
        for h in range(WKV_HEADS):
            st_ref[0, 0, h] = s0[h]
            inv_ref[0, 0, h] = inv[h]
            y_ref[:, h * HEAD:(h + 1) * HEAD] = y[h]
            s_scr[h] = s1[h]

    per_chunk = jax.ShapeDtypeStruct((nb * WKV_GROUPS, nc, WKV_HEADS, HEAD, HEAD), F32)
    return pl.pallas_call(
        body, name="wkv_fwd", grid=(nb, WKV_GROUPS, nc),
        in_specs=[rows(0), rows(2 * D), rows(0), rows(0), rows(0), rows(0)],
        out_specs=[rows(0), st, st],
        out_shape=[jax.ShapeDtypeStruct((t, D), F32), per_chunk, per_chunk],
        scratch_shapes=[pltpu.VMEM((WKV_HEADS, HEAD, HEAD), F32)],
        compiler_params=_cp("parallel", "parallel", "arbitrary"),
    )(z_rkv, z_rkv, lw, k2, kk, a)


def _wkv_bwd(z_rkv, lw, k2, kk, a, states, invs, dy, seq):
    t = z_rkv.shape[0]
    nb, nc = t // seq, seq // CHUNK
    rows, st = _wkv_specs(seq, True)

    def body(r_ref, v_ref, lw_ref, k_ref, kk_ref, a_ref, st_ref, inv_ref, dy_ref,
             dr_ref, dlw_ref, dk_ref, dv_ref, dkk_ref, da_ref, ds_scr):
        @pl.when(pl.program_id(2) == 0)
        def _():
            ds_scr[...] = jnp.zeros_like(ds_scr)

        s0 = [st_ref[0, 0, h] for h in range(WKV_HEADS)]
        inv = [inv_ref[0, 0, h] for h in range(WKV_HEADS)]
        _, vjp = jax.vjp(lambda *args: _wkv_chunk(*args, inv=inv)[:2],
                         s0, *[_head_cols(ref) for ref in (r_ref, lw_ref, k_ref, v_ref, kk_ref, a_ref)])
        grads = vjp(([x.astype(F32) for x in _head_cols(dy_ref)], [ds_scr[h] for h in range(WKV_HEADS)]))
        for h in range(WKV_HEADS):
            ds_scr[h] = grads[0][h]
            for ref, g in zip((dr_ref, dlw_ref, dk_ref, dv_ref, dkk_ref, da_ref), grads[1:]):
                ref[:, h * HEAD:(h + 1) * HEAD] = g[h]

    return pl.pallas_call(
        body, name="wkv_bwd", grid=(nb, WKV_GROUPS, nc),
        in_specs=[rows(0), rows(2 * D), rows(0), rows(0), rows(0), rows(0), st, st, rows(0)],
        out_specs=[rows(0)] * 6,
        out_shape=[jax.ShapeDtypeStruct((t, D), F32)] * 6,
        scratch_shapes=[pltpu.VMEM((WKV_HEADS, HEAD, HEAD), F32)],
        compiler_params=_cp("parallel", "parallel", "arbitrary"),
    )(z_rkv, z_rkv, lw, k2, kk, a, states, invs, dy)


def _softmax(s):
    e = jnp.exp(s - jnp.max(s, axis=-1, keepdims=True))
    return e * (1.0 / jnp.sum(e, axis=-1, keepdims=True))


ATT_HEADS = 8
ATT_COLS = ATT_HEADS * HEAD
ATT_GROUPS = N_HEADS // ATT_HEADS


def _attn_chunk(q, kb, vb, bias, valid):
    s = _each(lambda x, y, z: jnp.where(valid, _dot_nt(x * (HEAD ** -0.5), y) + z, MASK_VALUE), q, kb, bias)
    return _each(_dot, _each(_softmax, s), vb)


def _pad_fill(pad_ref, src_ref):
    pad_ref[0:LEFT, :] = jnp.zeros((LEFT, pad_ref.shape[1]), pad_ref.dtype)
    pad_ref[LEFT:, :] = src_ref[...].astype(pad_ref.dtype)


def _band_heads(pad_ref, start):
    return [pad_ref[pl.ds(start, BAND), h * HEAD:(h + 1) * HEAD].astype(F32) for h in range(ATT_HEADS)]


def _band_valid(c):
    return (c * CHUNK - LEFT + lax.broadcasted_iota(jnp.int32, (1, BAND), 1)) >= 0


def _attn_fwd(proj, bias, seq):
    t = proj.shape[0]
    nb, nc = t // seq, seq // CHUNK
    cq = C_Q // ATT_COLS

    def body(q_ref, k_ref, v_ref, b_ref, o_ref, kpad, vpad):
        c = pl.program_id(2)

        @pl.when(c == 0)
        def _():
            _pad_fill(kpad, k_ref)
            _pad_fill(vpad, v_ref)

        start = pl.multiple_of(c * CHUNK, CHUNK)
        o = _attn_chunk(_head_cols(q_ref), _band_heads(kpad, start), _band_heads(vpad, start),
                        [b_ref[h] for h in range(ATT_HEADS)], _band_valid(c))
        for h in range(ATT_HEADS):
            o_ref[:, h * HEAD:(h + 1) * HEAD] = o[h].astype(o_ref.dtype)

    return pl.pallas_call(
        body, name="attn_fwd", grid=(ATT_GROUPS, nb, nc),
        in_specs=[pl.BlockSpec((CHUNK, ATT_COLS), lambda h, b, c: (b * nc + c, cq + h)),
                  pl.BlockSpec((seq, ATT_COLS), lambda h, b, c: (b, cq + ATT_GROUPS + h)),
                  pl.BlockSpec((seq, ATT_COLS), lambda h, b, c: (b, cq + 2 * ATT_GROUPS + h)),
                  pl.BlockSpec((ATT_HEADS, CHUNK, BAND), lambda h, b, c: (h, 0, 0))],
        out_specs=pl.BlockSpec((CHUNK, ATT_COLS), lambda h, b, c: (b * nc + c, h)),
        out_shape=jax.ShapeDtypeStruct((t, D), BF16),
        scratch_shapes=[pltpu.VMEM((seq + LEFT, ATT_COLS), BF16)] * 2,
        compiler_params=_cp("parallel", "arbitrary", "arbitrary"),
    )(proj, proj, proj, bias)


def _attn_bwd(proj, bias, do, seq):
    t = proj.shape[0]
    nb, nc = t // seq, seq // CHUNK
    cq = C_Q // ATT_COLS

    def body(q_ref, k_ref, v_ref, b_ref, do_ref, dq_ref, dk_ref, dv_ref, db_ref, kpad, vpad, dkpad, dvpad):
        b, c = pl.program_id(1), pl.program_id(2)

        @pl.when(c == 0)
        def _():
            _pad_fill(kpad, k_ref)
            _pad_fill(vpad, v_ref)
            dkpad[...] = jnp.zeros_like(dkpad)
            dvpad[...] = jnp.zeros_like(dvpad)

        @pl.when(jnp.logical_and(b == 0, c == 0))
        def _():
            db_ref[...] = jnp.zeros_like(db_ref)

        start = pl.multiple_of(c * CHUNK, CHUNK)
        _, vjp = jax.vjp(functools.partial(_attn_chunk, valid=_band_valid(c)),
                         _head_cols(q_ref), _band_heads(kpad, start), _band_heads(vpad, start),
                         [b_ref[h] for h in range(ATT_HEADS)])
        dq, dkb, dvb, dbias = vjp([x.astype(F32) for x in _head_cols(do_ref)])
        for h in range(ATT_HEADS):
            sl = slice(h * HEAD, (h + 1) * HEAD)
            dq_ref[:, sl] = dq[h].astype(dq_ref.dtype)
            dkpad[pl.ds(start, BAND), sl] += dkb[h].astype(F32)
            dvpad[pl.ds(start, BAND), sl] += dvb[h].astype(F32)
            db_ref[h] += dbias[h]

        @pl.when(c == nc - 1)
        def _():
            dk_ref[...] = dkpad[LEFT:, :].astype(dk_ref.dtype)
            dv_ref[...] = dvpad[LEFT:, :].astype(dv_ref.dtype)

    kv_out = pl.BlockSpec((seq, ATT_COLS), lambda h, b, c: (b, h))
    return pl.pallas_call(
        body, name="attn_bwd", grid=(ATT_GROUPS, nb, nc),
        in_specs=[pl.BlockSpec((CHUNK, ATT_COLS), lambda h, b, c: (b * nc + c, cq + h)),
                  pl.BlockSpec((seq, ATT_COLS), lambda h, b, c: (b, cq + ATT_GROUPS + h)),
                  pl.BlockSpec((seq, ATT_COLS), lambda h, b, c: (b, cq + 2 * ATT_GROUPS + h)),
                  pl.BlockSpec((ATT_HEADS, CHUNK, BAND), lambda h, b, c: (h, 0, 0)),
                  pl.BlockSpec((CHUNK, ATT_COLS), lambda h, b, c: (b * nc + c, h))],
        out_specs=[pl.BlockSpec((CHUNK, ATT_COLS), lambda h, b, c: (b * nc + c, h)), kv_out, kv_out,
                   pl.BlockSpec((ATT_HEADS, CHUNK, BAND), lambda h, b, c: (h, 0, 0))],
        out_shape=[jax.ShapeDtypeStruct((t, D), BF16)] * 3 + [jax.ShapeDtypeStruct((N_HEADS, CHUNK, BAND), F32)],
        scratch_shapes=[pltpu.VMEM((seq + LEFT, ATT_COLS), BF16)] * 2 + [pltpu.VMEM((seq + LEFT, ATT_COLS), F32)] * 2,
        compiler_params=_cp("parallel", "arbitrary", "arbitrary"),
    )(proj, proj, proj, bias, do)


def _xattn_tile(q, k, v):
    s = _dot_nt(q, k) * ((MEM_WIDTH // MEM_HEADS) ** -0.5)
    return _dot(_softmax(s), v)


def _xattn_fwd(qm, kvm, seq, n_mem, tq=512):
    t = qm.shape[0]
    tq = min(tq, seq)
    nb, nq = t // seq, seq // tq

    def body(q_ref, k_ref, v_ref, o_ref):
        o_ref[...] = _xattn_tile(q_ref[...], k_ref[...], v_ref[...]).astype(o_ref.dtype)

    return pl.pallas_call(
        body, name="xattn_fwd", grid=(nb, MEM_HEADS, nq),
        in_specs=[pl.BlockSpec((tq, LANE), lambda b, h, i: (b * nq + i, h)),
                  pl.BlockSpec((n_mem, LANE), lambda b, h, i: (b, h)),
                  pl.BlockSpec((n_mem, LANE), lambda b, h, i: (b, MEM_HEADS + h))],
        out_specs=pl.BlockSpec((tq, LANE), lambda b, h, i: (b * nq + i, h)),
        out_shape=jax.ShapeDtypeStruct((t, MEM_WIDTH), BF16),
        compiler_params=_cp("parallel", "parallel", "parallel"),
    )(qm, kvm, kvm)


def _xattn_bwd(qm, kvm, do, seq, n_mem, tq=512):
    t = qm.shape[0]
    tq = min(tq, seq)
    nb, nq = t // seq, seq // tq

    def body(q_ref, k_ref, v_ref, do_ref, dq_ref, dkv_ref, dk_acc, dv_acc):
        i = pl.program_id(2)

        @pl.when(i == 0)
        def _():
            dk_acc[...] = jnp.zeros_like(dk_acc)
            dv_acc[...] = jnp.zeros_like(dv_acc)

        _, vjp = jax.vjp(_xattn_tile, q_ref[...], k_ref[...], v_ref[...])
        dq, dk, dv = vjp(do_ref[...].astype(F32))
        dq_ref[...] = dq.astype(dq_ref.dtype)
        dk_acc[...] += dk
        dv_acc[...] += dv

        @pl.when(i == nq - 1)
        def _():
            dkv_ref[0] = dk_acc[...].astype(dkv_ref.dtype)
            dkv_ref[1] = dv_acc[...].astype(dkv_ref.dtype)

    dq, dkv = pl.pallas_call(
        body, name="xattn_bwd", grid=(nb, MEM_HEADS, nq),
        in_specs=[pl.BlockSpec((tq, LANE), lambda b, h, i: (b * nq + i, h)),
                  pl.BlockSpec((n_mem, LANE), lambda b, h, i: (b, h)),
                  pl.BlockSpec((n_mem, LANE), lambda b, h, i: (b, MEM_HEADS + h)),
                  pl.BlockSpec((tq, LANE), lambda b, h, i: (b * nq + i, h))],
        out_specs=[pl.BlockSpec((tq, LANE), lambda b, h, i: (b * nq + i, h)),
                   pl.BlockSpec((2, n_mem, LANE), lambda b, h, i: (0, b, h))],
        out_shape=[jax.ShapeDtypeStruct((t, MEM_WIDTH), BF16), jax.ShapeDtypeStruct((2, nb * n_mem, MEM_WIDTH), BF16)],
        scratch_shapes=[pltpu.VMEM((n_mem, LANE), F32)] * 2,
        compiler_params=_cp("parallel", "parallel", "arbitrary"),
    )(qm, kvm, kvm, do)
    return dq, jnp.concatenate([dkv[0], dkv[1]], axis=1)


def _loss_head(x, u, g_post, target, tm=256):
    t, d = x.shape
    tm = min(tm, t)

    def tile_loss(xv, uv, gv, tv):
        diff = _fn_res(xv, uv, gv)[0] - tv
        return 0.5 * jnp.sum(jnp.mean(diff * diff, axis=-1, keepdims=True), axis=0, keepdims=True)

    def body(x_ref, u_ref, g_ref, t_ref, l_ref, dx_ref, du_ref, dg_ref):
        @pl.when(pl.program_id(0) == 0)
        def _():
            l_ref[...] = jnp.zeros_like(l_ref)
            dg_ref[...] = jnp.zeros_like(dg_ref)

        tv = t_ref[...]
        part, vjp = jax.vjp(lambda xv, uv, gv: tile_loss(xv, uv, gv, tv), x_ref[...], u_ref[...], g_ref[...])
        dx, du, dg = vjp(jnp.ones((1, 1), F32))
        l_ref[...] += part
        dx_ref[...] = dx
        du_ref[...] = du.astype(du_ref.dtype)
        dg_ref[...] += dg

    rows = pl.BlockSpec((tm, d), lambda i: (i, 0))
    vec = pl.BlockSpec((1, d), lambda i: (0, 0))
    return pl.pallas_call(
        body, name="loss_head", grid=(t // tm,),
        in_specs=[rows, rows, vec, rows],
        out_specs=[pl.BlockSpec((8, LANE), lambda i: (0, 0)), rows, rows, vec],
        out_shape=[jax.ShapeDtypeStruct((8, LANE), F32), jax.ShapeDtypeStruct((t, d), F32),
                   jax.ShapeDtypeStruct((t, d), BF16), jax.ShapeDtypeStruct((1, d), F32)],
        compiler_params=_cp("arbitrary"),
    )(x, u, g_post, target)


def _mesh_pos():
    return lax.axis_index("x"), lax.axis_index("y"), lax.axis_index("c")


def _peer(pos, d):
    x, y, c = pos
    return ((1 - x) if d & 4 else x, (1 - y) if d & 2 else y, (1 - c) if d & 1 else c)


def _flat(pos):
    return 4 * pos[0] + 2 * pos[1] + pos[2]


def _exchange(arrays, scatter, *, name):
    n = len(arrays)
    shapes = [a.shape[1:] if scatter else a.shape for a in arrays]

    def body(*refs):
        ins, outs = refs[:n], refs[n:2 * n]
        send, recv, loc = refs[2 * n:]
        pos = _mesh_pos()
        me = _flat(pos)
        pending = []
        for i in range(n):
            own = pltpu.make_async_copy(ins[i].at[me] if scatter else ins[i], outs[i].at[me], loc.at[i])
            own.start()
            pending.append(own)
            for d in range(1, N_DEV):
                peer = _peer(pos, d)
                src = ins[i].at[_flat(peer)] if scatter else ins[i]
                out_cp = pltpu.make_async_remote_copy(
                    src_ref=src, dst_ref=outs[i].at[me], send_sem=send.at[i, d - 1], recv_sem=recv.at[i, d - 1],
                    device_id=peer, device_id_type=pl.DeviceIdType.MESH)
                out_cp.start()
                pending.append(out_cp)
        for i in range(n):
            own = pending[i * N_DEV]
            for d in range(1, N_DEV):
                peer = _peer(pos, d)
                src = ins[i].at[_flat(peer)] if scatter else ins[i]
                pending[i * N_DEV + d].wait_send()
                pltpu.make_async_remote_copy(
                    src_ref=src, dst_ref=outs[i].at[_flat(peer)], send_sem=send.at[i, d - 1], recv_sem=recv.at[i, d - 1],
                    device_id=peer, device_id_type=pl.DeviceIdType.MESH).wait_recv()
            own.wait()

    hbm = pl.BlockSpec(memory_space=pltpu.HBM)
    return pl.pallas_call(
        body, name=name,
        in_specs=[hbm] * n, out_specs=[hbm] * n,
        out_shape=[jax.ShapeDtypeStruct((N_DEV,) + tuple(s), a.dtype) for s, a in zip(shapes, arrays)],
        scratch_shapes=[pltpu.SemaphoreType.DMA((n, N_DEV - 1)), pltpu.SemaphoreType.DMA((n, N_DEV - 1)),
                        pltpu.SemaphoreType.DMA((n,))],
    )(*arrays)


_HBM = pl.BlockSpec(memory_space=pltpu.HBM)
_SEM = pl.BlockSpec(memory_space=pltpu.SEMAPHORE)
_DATAFLOW = pltpu.SideEffectType.DATAFLOW_SIDE_EFFECTING


_ALL_PEERS = tuple(range(1, N_DEV))
_SIBLING_AND_SAME_CORE = (1, 2, 4, 6)


def _remote_copies(ins, lands, send, recv, scatter, dists):
    pos = _mesh_pos()
    me = _flat(pos)
    out = []
    for i in range(len(ins)):
        for j, d in enumerate(dists):
            peer = _peer(pos, d)
            src = ins[i].at[_flat(peer)] if scatter else ins[i]
            pair = i * len(dists) + j
            sems = dict(send_sem=send.at[pair], recv_sem=recv.at[pair], device_id=peer,
                        device_id_type=pl.DeviceIdType.MESH)
            out.append((pltpu.make_async_remote_copy(src_ref=src, dst_ref=lands[i].at[me], **sems),
                        pltpu.make_async_remote_copy(src_ref=src, dst_ref=lands[i].at[_flat(peer)], **sems)))
    return out


def _exchange_start(arrays, scatter, after, *, name, dists=_ALL_PEERS):
    n = len(arrays)
    shapes = [a.shape[1:] if scatter else a.shape for a in arrays]
    lands = [pltpu.with_memory_space_constraint(lax.empty((N_DEV,) + tuple(s), a.dtype), pltpu.HBM)
             for s, a in zip(shapes, arrays)]
    srcs = [pltpu.with_memory_space_constraint(a, pltpu.HBM) for a in arrays]

    def body(*refs):
        ins, land_refs = refs[:n], refs[n:2 * n]
        send, recv, token = refs[2 * n + 1], refs[2 * n + 2], refs[-1]
        for going, _ in _remote_copies(ins, land_refs, send, recv, scatter, dists):
            going.start()
        token[...] = jnp.zeros_like(token)

    sems = pltpu.SemaphoreType.DMA((n * len(dists),))
    res = pl.pallas_call(
        body, name=name,
        out_shape=(sems, sems, *[pltpu.HBM(a.shape, a.dtype) for a in srcs + lands], jax.ShapeDtypeStruct((8, LANE), F32)),
        in_specs=[_HBM] * (2 * n) + [pl.BlockSpec(memory_space=pl.ANY)],
        out_specs=(_SEM, _SEM, *[_HBM] * (2 * n), pl.BlockSpec(memory_space=pltpu.VMEM)),
        input_output_aliases={i: 2 + i for i in range(2 * n)},
        compiler_params=pltpu.CompilerParams(has_side_effects=_DATAFLOW),
    )(*srcs, *lands, after)
    return (n, scatter, dists, res[0], res[1], list(res[2:2 + 2 * n])), res[-1]


def _exchange_wait(handle, after, own, *, name):
    n, scatter, dists, send, recv, thru = handle

    def body(*refs):
        ins, land_refs = refs[:n], refs[n:2 * n]
        for going, coming in _remote_copies(ins, land_refs, refs[2 * n], refs[2 * n + 1], scatter, dists):
            going.wait_send()
            coming.wait_recv()

    res = pl.pallas_call(
        body, name=name,
        out_shape=tuple(pltpu.HBM(a.shape, a.dtype) for a in thru),
        in_specs=[_HBM] * (2 * n) + [_SEM, _SEM] + [pl.BlockSpec(memory_space=pl.ANY)] * len(after),
        out_specs=tuple([_HBM] * (2 * n)),
        input_output_aliases={i: i for i in range(2 * n)},
        compiler_params=pltpu.CompilerParams(has_side_effects=_DATAFLOW),
    )(*thru, send, recv, *after)
    me = _flat(_mesh_pos())
    return [lax.dynamic_update_slice_in_dim(land, o[None].astype(land.dtype), me, 0) for land, o in zip(res[n:], own)]


_OTHER_CHIPS = (2, 4, 6)


def _relay_to_sibling(gathered, *, name):
    n, k = len(gathered), len(_OTHER_CHIPS)

    def body(*refs):
        ins, outs = refs[:n], refs[n:2 * n]
        send, recv = refs[2 * n:]
        pos = _mesh_pos()
        copies = []
        for i in range(n):
            for j, d in enumerate(_OTHER_CHIPS):
                cp = pltpu.make_async_remote_copy(
                    src_ref=ins[i].at[_flat(_peer(pos, d))], dst_ref=outs[i].at[j],
                    send_sem=send.at[i * k + j], recv_sem=recv.at[i * k + j],
                    device_id=_peer(pos, 1), device_id_type=pl.DeviceIdType.MESH)
                cp.start()
                copies.append(cp)
        for cp in copies:
            cp.wait()

    return pl.pallas_call(
        body, name=name, in_specs=[_HBM] * n, out_specs=[_HBM] * n,
        out_shape=[jax.ShapeDtypeStruct((k,) + g.shape[1:], g.dtype) for g in gathered],
        scratch_shapes=[pltpu.SemaphoreType.DMA((n * k,)), pltpu.SemaphoreType.DMA((n * k,))],
    )(*gathered)


def _adamw(parts, w, m, v, *, name, tr=128, after=None):
    r, c = w.shape
    align = 8 * 4 // parts.dtype.itemsize
    row_tiles = [d for d in range(align, min(tr, r) + 1, align) if r % d == 0]
    tr, tc = (max(row_tiles), c) if row_tiles else (r, LANE)
    assert c % tc == 0
    n_after = 0 if after is None else 1

    def body(p_ref, w_ref, m_ref, v_ref, *rest):
        g_ref, d_ref, nm_ref, nv_ref = rest[n_after:]
        g = p_ref[0].astype(F32)
        for j in range(1, N_DEV):
            g = g + p_ref[j].astype(F32)
        m2 = ADAM_B1 * m_ref[...] + (1.0 - ADAM_B1) * g
        v2 = ADAM_B2 * v_ref[...] + (1.0 - ADAM_B2) * (g * g)
        m_hat = m2 / (1.0 - ADAM_B1 ** ADAM_STEP)
        v_hat = v2 / (1.0 - ADAM_B2 ** ADAM_STEP)
        g_ref[...] = g
        d_ref[...] = -ADAM_LR * (m_hat / (jnp.sqrt(v_hat) + ADAM_EPS) + ADAM_WD * w_ref[...])
        nm_ref[...] = m2
        nv_ref[...] = v2

    spec = pl.BlockSpec((tr, tc), lambda i, j: (i, j))
    return pl.pallas_call(
        body, name=name, grid=(r // tr, c // tc),
        in_specs=[pl.BlockSpec((N_DEV, tr, tc), lambda i, j: (0, i, j)), spec, spec, spec]
        + [pl.BlockSpec(memory_space=pl.ANY)] * n_after,
        out_specs=[spec] * 4, out_shape=[jax.ShapeDtypeStruct((r, c), F32)] * 4,
        compiler_params=_cp("parallel", "parallel"),
    )(parts, w, m, v, *([] if after is None else [after]))


def _cols_to_full(g):
    return jnp.transpose(g, (1, 0, 2)).reshape(g.shape[1], N_DEV * g.shape[2])


def _full_to_cols(w):
    r, c = w.shape
    return jnp.transpose(w.reshape(r, N_DEV, c // N_DEV), (1, 0, 2))


def _cut(a, lo, hi, axis):
    return lax.slice_in_dim(a, lo, hi, axis=axis)


def _pad_to(a, size, axis):
    pads = [(0, 0)] * a.ndim
    pads[axis] = (0, size - a.shape[axis])
    return jnp.pad(a, pads)


def _pad_lora(w, axis=1):
    return jnp.concatenate([
        _pad_to(_cut(w, 0, LORA_W, axis), 128, axis), _pad_to(_cut(w, LORA_W, LORA_W + LORA_A, axis), 128, axis),
        _pad_to(_cut(w, LORA_W + LORA_A, w.shape[axis], axis), 256, axis)], axis=axis)


def _unpad_lora(wp, axis=1):
    return jnp.concatenate([_cut(wp, 0, LORA_W, axis), _cut(wp, 128, 128 + LORA_A, axis),
                            _cut(wp, 256, 256 + LORA_G, axis)], axis=axis)


def _permute_in(w, axis):
    rk = 3 * D
    lo = rk + LORA_W + LORA_A + LORA_G
    return jnp.concatenate([_cut(w, 0, rk, axis), _cut(w, lo, w.shape[axis], axis), _pad_lora(_cut(w, rk, lo, axis), axis)],
                           axis=axis)


def _unpermute_in(wp, axis):
    return jnp.concatenate([_cut(wp, 0, 3 * D, axis), _unpad_lora(_cut(wp, C_LORA, P_WIDTH, axis), axis),
                            _cut(wp, 3 * D, C_LORA, axis)], axis=axis)


def _rel_index():
    dist = jnp.arange(CHUNK)[:, None] - jnp.arange(BAND)[None, :] + LEFT
    return (jnp.minimum(dist, REL_CLIP) + (CHUNK - 1)).reshape(-1)


def _local_step(x, mem, target, wt, seq, n_mem, comm):
    t = x.shape[0]
    row = lambda a: a.reshape(1, -1).astype(F32)
    g_pre_mix, g_post_mix = row(wt["g_pre_mix"]), row(wt["g_post_mix"])
    g_pre_cross, g_post_cross, g_mem = row(wt["g_pre_cross"]), row(wt["g_post_cross"]), row(wt["g_mem"])
    g_pre_ffn, g_post_ffn = row(wt["g_pre_ffn"]), row(wt["g_post_ffn"])
    mix = row(wt["shift_mix"])
    mix_rkv, mix_lora = mix[:, :3 * D], _pad_lora(mix[:, 3 * D:])
    decay_base, iclr_base = row(wt["decay_base"]), row(wt["iclr_base"])
    kns, kis = row(wt["key_norm_scale"]), row(wt["key_iclr_scale"])
    lnx_w, lnx_b, bonus = row(wt["lnx_w"]), row(wt["lnx_b"]), row(wt["bonus_scale"])
    e_dh = (jnp.arange(D)[:, None] // HEAD == jnp.arange(N_HEADS)[None, :]).astype(F32)
    e_hd = e_dh.T
    onehot = (jnp.arange(REL_TABLE)[:, None] == _rel_index()[None, :]).astype(BF16)

    begun = comm.begun
    (h1,) = _rowwise(_fn_pre, [_win(x)], [g_pre_mix], [(D, BF16)], name="pre_mix", tm=512, after=begun)
    (mn,) = _rowwise(_fn_pre, [_win(mem)], [g_mem], [(D, BF16)], name="pre_mem", tm=512, after=begun)
    bias = _mm(wt["rel_bias"].astype(F32), onehot, name="mm_bias", split_a=3, after=begun).reshape(N_HEADS, CHUNK, BAND)
    wt = {**wt, **comm.first_weights([h1, mn, bias])}
    w_in = wt["w_in_p"]
    d_up = jnp.pad(wt["decay_up"].astype(F32), ((0, 128 - LORA_W), (0, 0)))
    i_up = jnp.pad(wt["iclr_up"].astype(F32), ((0, 128 - LORA_A), (0, 0)))
    g_up = jnp.pad(wt["gate_up"].astype(F32), ((0, 256 - LORA_G), (0, 0)))
    proj = _mm(h1, w_in, tb=True, name="mm_in", after=comm.first_token)
    z_rkv = _shift_fwd(proj, 0, 3 * D, mix_rkv, seq, name="shift_rkv")
    z_lora = _shift_fwd(proj, C_LORA, 512, mix_lora, seq, name="shift_lora")
    prep_rows = [_win(z_rkv, D, D), _win(z_lora, 0, 128), _win(z_lora, 128, 128), _win(z_lora, 256, 256)]
    prep_params = [decay_base, d_up, iclr_base, i_up, g_up, kns, kis, e_hd, e_dh]
    lw, k2, kk, a, g = _rowwise(_fn_prep, prep_rows, prep_params, [(D, F32)] * 5, name="rwkv_prep", tm=256)
    y, states, invs = _wkv_fwd(z_rkv, lw, k2, kk, a, seq)
    post_rows = [_win(y), _win(z_rkv, 0, D), _win(k2), _win(z_rkv, 2 * D, D), _win(g)]
    post_params = [lnx_w, lnx_b, bonus, e_hd, e_dh]
    (y_a,) = _rowwise(_fn_post, post_rows, post_params, [(D, BF16)], name="rwkv_post", tm=256)
    y_b = _attn_fwd(proj, bias, seq)
    wt = {**wt, **comm.late_weights(y_b)}
    ya_p = _mm(y_a, wt["w_branch_a"], name="mm_a")
    yb_p = _mm(y_b, wt["w_branch_b"], name="mm_b")
    mix_rows = [_win(proj, C_GA, D), _win(proj, C_GA + D, D), _win(ya_p), _win(yb_p)]
    (mixed,) = _rowwise(_fn_mix, mix_rows, [], [(D, BF16)], name="gate_mix", tm=512)
    mo = _mm(mixed, wt["w_out"], name="mm_out")
    x1, h2 = _rowwise(_fn_res_pre, [_win(x), _win(mo)], [g_post_mix, g_pre_cross], [(D, F32), (D, BF16)],
                      name="res_mix", tm=512)
    qm = _mm(h2, wt["w_q_mem"], name="mm_q")
    kvm = _mm(mn, wt["w_kv_mem"], name="mm_kv")
    om = _xattn_fwd(qm, kvm, seq, n_mem)
    co = _mm(om, wt["w_o_mem"], name="mm_o")
    x2, h3 = _rowwise(_fn_res_pre, [_win(x1), _win(co)], [g_post_cross, g_pre_ffn], [(D, F32), (D, BF16)],
                      name="res_cross", tm=512)
    gu = _mm(h3, wt["w_ffn_in"], tb=True, name="mm_ffn_in", out_dtype=BF16)
    (act,) = _rowwise(_fn_swiglu, [_win(gu, 0, FFN), _win(gu, FFN, FFN)], [], [(FFN, BF16)], name="swiglu", tm=256)
    ff = _mm(act, wt["w_ffn_out"], name="mm_ffn_out")

    gw = {}
    loss, dx2, dff, gw["g_post_ffn"] = _loss_head(x2, ff, g_post_ffn, target)
    dact = _mm(dff, wt["w_ffn_out"], tb=True, name="mm_ffn_out_dx", out_dtype=BF16)
    gw["w_ffn_out"] = _mm(act, dff, ta=True, name="mm_ffn_out_dw", out_dtype=BF16)
    (dgu,), _ = _rowwise_bwd(_fn_swiglu, [_win(gu, 0, FFN), _win(gu, FFN, FFN)], [], 0, [[dact]],
                             name="swiglu_bwd", tm=256, row_grad=[BF16, BF16], packed=True)
    dh3 = _mm(dgu, wt["w_ffn_in"], name="mm_ffn_in_dx", out_dtype=BF16)
    gw["w_ffn_in"] = _mm(dgu, h3, ta=True, name="mm_ffn_in_dw", out_dtype=BF16)
    (dx1, dco), (gw["g_post_cross"], gw["g_pre_ffn"]) = _rowwise_bwd(
        _fn_res_pre, [_win(x1), _win(co)], [g_post_cross, g_pre_ffn], 0, [[dx2], [dh3]],
        name="res_cross_bwd", tm=256, row_grad=[F32, BF16])
    dom = _mm(dco, wt["w_o_mem"], tb=True, name="mm_o_dx", out_dtype=BF16)
    gw["w_o_mem"] = _mm(om, dco, ta=True, name="mm_o_dw", out_dtype=BF16)
    dqm, dkvm = _xattn_bwd(qm, kvm, dom, seq, n_mem)
    dh2 = _mm(dqm, wt["w_q_mem"], tb=True, name="mm_q_dx", out_dtype=BF16)
    gw["w_q_mem"] = _mm(h2, dqm, ta=True, name="mm_q_dw", out_dtype=BF16)
    dmn = _mm(dkvm, wt["w_kv_mem"], tb=True, name="mm_kv_dx", out_dtype=BF16)
    gw["w_kv_mem"] = _mm(mn, dkvm, ta=True, name="mm_kv_dw", out_dtype=BF16)
    _, (gw["g_mem"],) = _rowwise_bwd(_fn_pre, [_win(mem)], [g_mem], 0, [[dmn]], name="pre_mem_bwd", tm=256,
                                     row_grad=[None])
    (dx0, dmo), (gw["g_post_mix"], gw["g_pre_cross"]) = _rowwise_bwd(
        _fn_res_pre, [_win(x), _win(mo)], [g_post_mix, g_pre_cross], 0, [[dx1], [dh2]],
        name="res_mix_bwd", tm=256, row_grad=[F32, BF16])
    dmixed = _mm(dmo, wt["w_out"], tb=True, name="mm_out_dx", out_dtype=BF16)
    gw["w_out"] = _mm(mixed, dmo, ta=True, name="mm_out_dw", out_dtype=BF16)
    (dzga, dzgb, dya_p, dyb_p), _ = _rowwise_bwd(_fn_mix, mix_rows, [], 0, [[dmixed]], name="gate_mix_bwd", tm=256,
                                                 row_grad=[BF16] * 4)
    gw["w_branch_a"] = _mm(y_a, dya_p, ta=True, name="mm_a_dw", out_dtype=BF16)
    gw["w_branch_b"] = _mm(y_b, dyb_p, ta=True, name="mm_b_dw", out_dtype=BF16)
    token = comm.send_early(gw)
    dy_a = _mm(dya_p, wt["w_branch_a"], tb=True, name="mm_a_dx", out_dtype=BF16, after=token)
    dy_b = _mm(dyb_p, wt["w_branch_b"], tb=True, name="mm_b_dx", out_dtype=BF16, after=token)
    dq, dk, dv, dbias = _attn_bwd(proj, bias, dy_b, seq)
    gw["rel_bias"] = _mm(dbias.reshape(N_HEADS, CHUNK * BAND), onehot, tb=True, name="mm_bias_dw", split_a=2)
    (dy, dr_p, dk2_p, dv_p, dg), (gw["lnx_w"], gw["lnx_b"], gw["bonus_scale"]) = _rowwise_bwd(
        _fn_post, post_rows, post_params, 2, [[dy_a]], name="rwkv_post_bwd", tm=128, row_grad=[F32] * 5)
    dr_s, dlw, dk2_s, dv_s, dkk, da = _wkv_bwd(z_rkv, lw, k2, kk, a, states, invs, dy, seq)
    (dzk, dzw, dza, dzg), pg = _rowwise_bwd(
        _fn_prep, prep_rows, prep_params, 2, [[dlw], [dk2_p, dk2_s], [dkk], [da], [dg]],
        name="rwkv_prep_bwd", tm=128, row_grad=[F32] * 4)
    gw["decay_base"], gd_up, gw["iclr_base"], gi_up, gg_up, gw["key_norm_scale"], gw["key_iclr_scale"] = pg
    gw["decay_up"], gw["iclr_up"], gw["gate_up"] = gd_up[:LORA_W], gi_up[:LORA_A], gg_up[:LORA_G]
    dp_r, gmix_r = _shift_bwd(proj, 0, D, mix_rkv[:, :D], [dr_p, dr_s], seq, name="shift_r_bwd")
    dp_k, gmix_k = _shift_bwd(proj, D, D, mix_rkv[:, D:2 * D], [dzk], seq, name="shift_k_bwd")
    dp_v, gmix_v = _shift_bwd(proj, 2 * D, D, mix_rkv[:, 2 * D:], [dv_p, dv_s], seq, name="shift_v_bwd")
    dp_lora, gmix_lora = _shift_bwd(proj, C_LORA, 512, mix_lora, [jnp.concatenate([dzw, dza, dzg], axis=1)], seq,
                                    name="shift_lora_bwd")
    gw["shift_mix"] = jnp.concatenate([gmix_r, gmix_k, gmix_v, _unpad_lora(gmix_lora)], axis=1)
    dproj = [dp_r, dp_k, dp_v, dq, dk, dv, dzga, dzgb, dp_lora]
    gw["w_in_p"] = _mm_cat_tn(dproj, h1, name="mm_in_dw", after=gw["rel_bias"])
    token = comm.send_late(gw)
    dh1 = _mm_cat_nn(dproj, w_in, name="mm_in_dx", after=token)
    (grad_x,), (gw["g_pre_mix"],) = _rowwise_bwd(_fn_pre, [_win(x)], [g_pre_mix], 0, [[dh1]], name="pre_mix_bwd",
                                                 tm=256, row_grad=[F32], add_to={0: dx0})
    return loss, grad_x, gw


_COL_SHARDED = ("w_in", "decay_up", "iclr_up", "gate_up", "w_o_mem", "w_ffn_in")
_ROW_SHARDED = ("w_branch_a", "w_branch_b", "w_out", "w_q_mem", "w_kv_mem", "w_ffn_out")
_TRANSPOSED = ("w_in", "w_ffn_in")
_FIRST = ("w_in", "decay_up", "iclr_up", "gate_up")
_REST = ("w_o_mem", "w_ffn_in", "w_branch_a", "w_branch_b", "w_out", "w_q_mem", "w_kv_mem", "w_ffn_out")
_REPLICATED = ("g_pre_mix", "g_post_mix", "shift_mix", "decay_base", "iclr_base", "key_norm_scale", "key_iclr_scale",
               "bonus_scale", "lnx_w", "lnx_b", "rel_bias", "g_pre_cross", "g_post_cross", "g_mem", "g_pre_ffn",
               "g_post_ffn")
_WEIGHTS = ("g_pre_mix", "g_post_mix", "w_in", "shift_mix", "decay_base", "decay_up", "iclr_base", "iclr_up", "gate_up",
            "key_norm_scale", "key_iclr_scale", "bonus_scale", "lnx_w", "lnx_b", "rel_bias", "w_branch_a", "w_branch_b",
            "w_out", "g_pre_cross", "g_post_cross", "g_mem", "w_q_mem", "w_kv_mem", "w_o_mem", "g_pre_ffn", "g_post_ffn",
            "w_ffn_in", "w_ffn_out")
_PACK_ROWS = 8 * ((sum({"shift_mix": 3360, "bonus_scale": 1024, "rel_bias": 3072}.get(n, D) for n in _REPLICATED)
                   + 1 + 8 * LANE - 1) // (8 * LANE))


def _pack(vals):
    flat = jnp.concatenate([v.reshape(-1).astype(F32) for v in vals])
    return jnp.pad(flat, (0, _PACK_ROWS * LANE - flat.shape[0])).reshape(_PACK_ROWS, LANE)


def _unpack(packed, shapes):
    flat, out, pos = packed.reshape(-1), [], 0
    for s in shapes:
        n = math.prod(s)
        out.append(flat[pos:pos + n].reshape(s))
        pos += n
    return out


def _step(args, seq, n_mem):
    names = ("x", "mem") + _WEIGHTS + ("loss_target",) + tuple("m_" + n for n in _WEIGHTS) + tuple("v_" + n for n in _WEIGHTS)
    given = dict(zip(names, args))
    nb = given["x"].shape[0]
    x = given["x"].reshape(nb * seq, D)
    mem = given["mem"].reshape(nb * n_mem, D)
    target = given["loss_target"].reshape(nb * seq, D)
    def local(name, prefix=""):
        a = given[prefix + name][0]
        return a.T if name in _TRANSPOSED else a

    shard = {n: local(n) for n in _COL_SHARDED + _ROW_SHARDED}
    stacked = _ROW_SHARDED + _TRANSPOSED
    out = {}

    def wire(name):
        return shard[name].astype(BF16)

    def full(name, g):
        return g.reshape(-1, g.shape[-1]) if name in stacked else _cols_to_full(g)

    def blocks_of(name, g):
        return (g.reshape((N_DEV,) + shard[name].shape) if name in stacked else _full_to_cols(g)).astype(BF16)

    def update(names, landed, after=None):
        done = []
        for n, parts in zip(names, landed):
            res = _adamw(parts, shard[n], local(n, "m_"), local(n, "v_"), name="adamw_" + n, after=after)
            for kind, r in zip(("grad_", "delta_", "new_m_", "new_v_"), res):
                out[kind + n] = (r.T if n in _TRANSPOSED else r)[None]
            done.append(res[0])
        return done


    class Exchanges:
        def __init__(self):
            srcs = [wire(n) for n in _FIRST]
            self.first, self.begun = _exchange_start(srcs, False, srcs[0], name="gather_first_start",
                                                     dists=_SIBLING_AND_SAME_CORE)

        def first_weights(self, after):
            got = _exchange_wait(self.first, after, [wire(n) for n in _FIRST], name="gather_first_wait")
            relayed = _relay_to_sibling(got, name="gather_first_relay")
            pos = _mesh_pos()
            for j, d in enumerate(_OTHER_CHIPS):
                slot = _flat(_peer(pos, d | 1))
                got = [lax.dynamic_update_slice_in_dim(g, r[j][None], slot, 0) for g, r in zip(got, relayed)]
            self.rest, self.first_token = _exchange_start(
                [wire(n) for n in _REST], False, got[0], name="gather_rest_start")
            first = {n: full(n, g) for n, g in zip(_FIRST, got)}
            first["w_in_p"] = _permute_in(first.pop("w_in"), 0)
            return first

        def late_weights(self, after):
            got = _exchange_wait(self.rest, [after], [wire(n) for n in _REST], name="gather_rest_wait")
            return {n: full(n, g) for n, g in zip(_REST, got)}

        def send_early(self, gw):
            self.early_blocks = [blocks_of(n, gw[n]) for n in _REST]
            self.early, token = _exchange_start(self.early_blocks, True, self.early_blocks[-1], name="scatter_rest_start")
            return token

        def send_late(self, gw):
            me = _flat(_mesh_pos())
            own = [lax.dynamic_index_in_dim(b, me, 0, keepdims=False) for b in self.early_blocks]
            landed = _exchange_wait(self.early, [gw["w_in_p"]], own, name="scatter_rest_wait")
            grads = {**gw, "w_in": _unpermute_in(gw["w_in_p"], 0)}
            self.late_blocks = [blocks_of(n, grads[n]) for n in _FIRST]
            self.late, token = _exchange_start(self.late_blocks, True, landed[0], name="scatter_first_start")
            self.updated = update(_REST, landed, after=token)
            return token

        def finish(self, after):
            me = _flat(_mesh_pos())
            own = [lax.dynamic_index_in_dim(b, me, 0, keepdims=False) for b in self.late_blocks]
            update(_FIRST, _exchange_wait(self.late, [*after, *self.updated], own, name="scatter_first_wait"))

    comm = Exchanges()
    wt = {n: given[n][0] for n in _REPLICATED}
    loss_tile, grad_x, gw = _local_step(x, mem, target, wt, seq, n_mem, comm)
    rep_shapes = [given[n].shape for n in _REPLICATED]
    packed, _ = lax.optimization_barrier((_pack([gw[n] for n in _REPLICATED] + [loss_tile[0, 0]]), tuple(comm.updated)))
    small = _exchange([packed], False, name="gather_small")[0]
    zero = jnp.zeros((), F32)
    res = _adamw(small, *[_pack([given[p + n] for n in _REPLICATED] + [zero]) for p in ("", "m_", "v_")],
                 name="adamw_small", tr=_PACK_ROWS)
    for kind, r in zip(("grad_", "delta_", "new_m_", "new_v_"), res):
        for n, val in zip(_REPLICATED, _unpack(r, rep_shapes)):
            out[kind + n] = val
    loss = res[0].reshape(-1)[sum(math.prod(s) for s in rep_shapes)]
    comm.finish([grad_x, res[0]])
    grad_x = grad_x.reshape(nb, seq, D)
    return (loss, grad_x, *[out[k + n] for k in ("grad_", "delta_", "new_m_", "new_v_") for n in _WEIGHTS])


def kernel(x, mem, g_pre_mix, g_post_mix, w_in, shift_mix, decay_base, decay_up, iclr_base, iclr_up, gate_up, key_norm_scale, key_iclr_scale, bonus_scale, lnx_w, lnx_b, rel_bias, w_branch_a, w_branch_b, w_out, g_pre_cross, g_post_cross, g_mem, w_q_mem, w_kv_mem, w_o_mem, g_pre_ffn, g_post_ffn, w_ffn_in, w_ffn_out, loss_target, m_g_pre_mix, m_g_post_mix, m_w_in, m_shift_mix, m_decay_base, m_decay_up, m_iclr_base, m_iclr_up, m_gate_up, m_key_norm_scale, m_key_iclr_scale, m_bonus_scale, m_lnx_w, m_lnx_b, m_rel_bias, m_w_branch_a, m_w_branch_b, m_w_out, m_g_pre_cross, m_g_post_cross, m_g_mem, m_w_q_mem, m_w_kv_mem, m_w_o_mem, m_g_pre_ffn, m_g_post_ffn, m_w_ffn_in, m_w_ffn_out, v_g_pre_mix, v_g_post_mix, v_w_in, v_shift_mix, v_decay_base, v_decay_up, v_iclr_base, v_iclr_up, v_gate_up, v_key_norm_scale, v_key_iclr_scale, v_bonus_scale, v_lnx_w, v_lnx_b, v_rel_bias, v_w_branch_a, v_w_branch_b, v_w_out, v_g_pre_cross, v_g_post_cross, v_g_mem, v_w_q_mem, v_w_kv_mem, v_w_o_mem, v_g_pre_ffn, v_g_post_ffn, v_w_ffn_in, v_w_ffn_out):
    args = (x, mem, g_pre_mix, g_post_mix, w_in, shift_mix, decay_base, decay_up, iclr_base, iclr_up, gate_up, key_norm_scale, key_iclr_scale, bonus_scale, lnx_w, lnx_b, rel_bias, w_branch_a, w_branch_b, w_out, g_pre_cross, g_post_cross, g_mem, w_q_mem, w_kv_mem, w_o_mem, g_pre_ffn, g_post_ffn, w_ffn_in, w_ffn_out, loss_target, m_g_pre_mix, m_g_post_mix, m_w_in, m_shift_mix, m_decay_base, m_decay_up, m_iclr_base, m_iclr_up, m_gate_up, m_key_norm_scale, m_key_iclr_scale, m_bonus_scale, m_lnx_w, m_lnx_b, m_rel_bias, m_w_branch_a, m_w_branch_b, m_w_out, m_g_pre_cross, m_g_post_cross, m_g_mem, m_w_q_mem, m_w_kv_mem, m_w_o_mem, m_g_pre_ffn, m_g_post_ffn, m_w_ffn_in, m_w_ffn_out, v_g_pre_mix, v_g_post_mix, v_w_in, v_shift_mix, v_decay_base, v_decay_up, v_iclr_base, v_iclr_up, v_gate_up, v_key_norm_scale, v_key_iclr_scale, v_bonus_scale, v_lnx_w, v_lnx_b, v_rel_bias, v_w_branch_a, v_w_branch_b, v_w_out, v_g_pre_cross, v_g_post_cross, v_g_mem, v_w_q_mem, v_w_kv_mem, v_w_o_mem, v_g_pre_ffn, v_g_post_ffn, v_w_ffn_in, v_w_ffn_out)
    return _step(args, x.shape[1], mem.shape[1])
```

```python
import functools
import math

import jax
import jax.numpy as jnp
from jax import lax
from jax.experimental import pallas as pl
from jax.experimental.pallas import tpu as pltpu

F32 = jnp.float32
BF16 = jnp.bfloat16

N_DEV = 8
D = 1024
HEAD = 64
N_HEADS = D // HEAD
LANE = 128
CHUNK = 64
LEFT = 8 * CHUNK
BAND = LEFT + CHUNK
REL_CLIP = 128
REL_TABLE = CHUNK + REL_CLIP
MEM_WIDTH = D // 2
MEM_HEADS = 4
FFN = 2816
LORA_W, LORA_A, LORA_G = 64, 64, 160
P_WIDTH = 3 * D + 3 * D + 2 * D + 128 + 128 + 256
C_Q, C_GA, C_LORA = 3 * D, 6 * D, 8 * D
NORM_EPS = 1e-6
GROUP_NORM_EPS = 64e-5
MASK_VALUE = -1e30
ADAM_LR, ADAM_B1, ADAM_B2, ADAM_EPS, ADAM_WD, ADAM_STEP = 0.001, 0.9, 0.999, 1e-08, 0.01, 10
VMEM_LIMIT = 56 * 1024 * 1024


def _cp(*sem):
    return pltpu.CompilerParams(dimension_semantics=sem, vmem_limit_bytes=VMEM_LIMIT)


_NN, _NT, _TN = ((1,), (0,)), ((1,), (1,)), ((0,), (0,))


def _dot_raw(a, b, dims):
    return lax.dot_general(a.astype(BF16), b.astype(BF16), (dims, ((), ())), preferred_element_type=F32)


@functools.partial(jax.custom_vjp, nondiff_argnums=(2,))
def _dot_dims(a, b, dims):
    return _dot_raw(a, b, dims)


def _dot_dims_fwd(a, b, dims):
    return _dot_raw(a, b, dims), (a, b)


def _dot_dims_bwd(dims, res, g):
    a, b = res
    if dims == _NN:
        da, db = _dot_raw(g, b, _NT), _dot_raw(a, g, _TN)
    elif dims == _NT:
        da, db = _dot_raw(g, b, _NN), _dot_raw(g, a, _TN)
    else:
        da, db = _dot_raw(b, g, _NT), _dot_raw(a, g, _NN)
    return da.astype(a.dtype), db.astype(b.dtype)


_dot_dims.defvjp(_dot_dims_fwd, _dot_dims_bwd)


def _dot(a, b, dims=_NN):
    return _dot_dims(a, b, dims)


def _dot_nt(a, b):
    return _dot_dims(a, b, _NT)


def _dot_tn(a, b):
    return _dot_dims(a, b, _TN)


def _split(x, terms):
    parts, rest = [], x.astype(F32)
    for _ in range(terms):
        p = rest.astype(BF16)
        parts.append(p)
        rest = rest - p.astype(F32)
    return parts


def _dot_split_a(a, b, terms=2):
    out = None
    for p in _split(a, terms):
        t = _dot(p, b)
        out = t if out is None else out + t
    return out


def _dot_split_b(a, b, terms=3):
    out = None
    for p in _split(b, terms):
        t = _dot(a, p)
        out = t if out is None else out + t
    return out


MM_VMEM_BUDGET = 30 * 1024 * 1024
MM_HBM_BPS = 3.2e12
MM_MXU_FPS = 8.5e14
MM_STEP_S = 0.35e-6


def _divisors(n, align, cap):
    out = [d for d in range(align, min(n, cap) + 1, align) if n % d == 0]
    return out or [n]


def _mm_tiles(m, n, k, ea, eb, eo, ta):
    best = None
    for tm in _divisors(m, LANE if ta else 8, 2048):
        for tn in _divisors(n, LANE, 2048):
            for tk in _divisors(k, LANE, 2048):
                nk = k // tk
                vmem = 2 * (tm * tk * ea + tk * tn * eb + tm * tn * eo) + (tm * tn * 4 if nk > 1 else 0)
                if vmem > MM_VMEM_BUDGET:
                    continue
                dma = (tm * tk * ea if (nk > 1 or n // tn == 1) else tm * tk * ea * tn / n) + tk * tn * eb + tm * tn * eo / nk
                step = max(2.0 * tm * tn * tk / MM_MXU_FPS, dma / MM_HBM_BPS) + MM_STEP_S
                cost = (m // tm) * (n // tn) * nk * step
                if best is None or cost < best[0]:
                    best = (cost, tm, tn, tk)
    return best[1:]


def _mm(a, b, *, name, ta=False, tb=False, out_dtype=F32, tm=None, tn=None, tk=None, split_a=1, after=None):
    m, k = (a.shape[1], a.shape[0]) if ta else a.shape
    n, kb = (b.shape[0], b.shape[1]) if tb else (b.shape[1], b.shape[0])
    assert k == kb, (a.shape, b.shape, ta, tb)
    if tm is None:
        tm, tn, tk = _mm_tiles(m, n, k, a.dtype.itemsize, b.dtype.itemsize, jnp.dtype(out_dtype).itemsize, ta)
    assert m % tm == 0 and n % tn == 0 and k % tk == 0, (m, n, k, tm, tn, tk)
    nk = k // tk
    dims = ((0 if ta else 1,), (1 if tb else 0,))

    n_after = 0 if after is None else 1

    def body(a_ref, b_ref, *rest):
        o_ref, scratch = rest[n_after], rest[n_after + 1:]
        prod = None
        for p in _split(a_ref[...], split_a) if split_a > 1 else [a_ref[...]]:
            t = _dot_raw(p, b_ref[...], dims)
            prod = t if prod is None else prod + t
        if nk == 1:
            o_ref[...] = prod.astype(o_ref.dtype)
            return
        acc_ref, kk = scratch[0], pl.program_id(2)

        @pl.when(kk == 0)
        def _():
            acc_ref[...] = prod

        @pl.when(kk > 0)
        def _():
            acc_ref[...] += prod

        @pl.when(kk == nk - 1)
        def _():
            o_ref[...] = acc_ref[...].astype(o_ref.dtype)

    a_spec = pl.BlockSpec((tk, tm), lambda i, j, q: (q, i)) if ta else pl.BlockSpec((tm, tk), lambda i, j, q: (i, q))
    b_spec = pl.BlockSpec((tn, tk), lambda i, j, q: (j, q)) if tb else pl.BlockSpec((tk, tn), lambda i, j, q: (q, j))
    return pl.pallas_call(
        body, name=name, grid=(m // tm, n // tn, nk),
        in_specs=[a_spec, b_spec] + [pl.BlockSpec(memory_space=pl.ANY)] * n_after,
        out_specs=pl.BlockSpec((tm, tn), lambda i, j, q: (i, j)),
        out_shape=jax.ShapeDtypeStruct((m, n), out_dtype),
        scratch_shapes=[pltpu.VMEM((tm, tn), F32)] if nk > 1 else [],
        compiler_params=_cp("parallel", "parallel", "arbitrary"),
    )(a, b, *([] if after is None else [after]))


def _piece_steps(pieces, tile):
    counts = [p.shape[1] // tile for p in pieces]
    assert all(p.shape[1] % tile == 0 for p in pieces)
    return [(sum(counts[:i]), c) for i, c in enumerate(counts)], sum(counts)


def _mm_cat_nn(pieces, w, *, name, after=None, tm=2048, tk=256):
    t, n = pieces[0].shape[0], w.shape[1]
    tm = min(tm, t)
    spans, nk = _piece_steps(pieces, tk)
    npc = len(pieces)
    n_after = 0 if after is None else 1

    def body(*refs):
        w_ref, o_ref, acc_ref = refs[npc], refs[npc + 1 + n_after], refs[npc + 2 + n_after]
        q = pl.program_id(1)

        @pl.when(q == 0)
        def _():
            acc_ref[...] = jnp.zeros_like(acc_ref)

        for p_ref, (first, count) in zip(refs[:npc], spans):
            @pl.when(jnp.logical_and(q >= first, q < first + count))
            def _(p_ref=p_ref):
                acc_ref[...] += _dot_raw(p_ref[...], w_ref[...], _NN)

        @pl.when(q == nk - 1)
        def _():
            o_ref[...] = acc_ref[...].astype(o_ref.dtype)

    def piece_spec(first, count):
        return pl.BlockSpec((tm, tk), lambda i, q: (i, jnp.clip(q - first, 0, count - 1)))

    return pl.pallas_call(
        body, name=name, grid=(t // tm, nk),
        in_specs=[piece_spec(*s) for s in spans] + [pl.BlockSpec((tk, n), lambda i, q: (q, 0))]
        + [pl.BlockSpec(memory_space=pl.ANY)] * n_after,
        out_specs=pl.BlockSpec((tm, n), lambda i, q: (i, 0)),
        out_shape=jax.ShapeDtypeStruct((t, n), BF16),
        scratch_shapes=[pltpu.VMEM((tm, n), F32)],
        compiler_params=_cp("parallel", "arbitrary"),
    )(*pieces, w, *([] if after is None else [after]))


def _mm_cat_tn(pieces, a, *, name, after=None, tk=1024, tn=512):
    t, m = a.shape
    tk = min(tk, t)
    spans, nj = _piece_steps(pieces, tn)
    npc, nk = len(pieces), t // tk
    n_after = 0 if after is None else 1

    def body(a_ref, *refs):
        o_ref, acc_ref = refs[npc + n_after], refs[npc + 1 + n_after]
        j, q = pl.program_id(0), pl.program_id(1)

        @pl.when(q == 0)
        def _():
            acc_ref[...] = jnp.zeros_like(acc_ref)

        for p_ref, (first, count) in zip(refs[:npc], spans):
            @pl.when(jnp.logical_and(j >= first, j < first + count))
            def _(p_ref=p_ref):
                acc_ref[...] += _dot_raw(p_ref[...], a_ref[...], _TN)

        @pl.when(q == nk - 1)
        def _():
            o_ref[...] = acc_ref[...].astype(o_ref.dtype)

    def piece_spec(first, count):
        def index(j, q):
            mine = jnp.logical_and(j >= first, j < first + count)
            return jnp.where(mine, q, 0), jnp.clip(j - first, 0, count - 1)
        return pl.BlockSpec((tk, tn), index)

    return pl.pallas_call(
        body, name=name, grid=(nj, nk),
        in_specs=[pl.BlockSpec((tk, m), lambda j, q: (q, 0))] + [piece_spec(*s) for s in spans]
        + [pl.BlockSpec(memory_space=pl.ANY)] * n_after,
        out_specs=pl.BlockSpec((tn, m), lambda j, q: (j, 0)),
        out_shape=jax.ShapeDtypeStruct((nj * tn, m), BF16),
        scratch_shapes=[pltpu.VMEM((tn, m), F32)],
        compiler_params=_cp("parallel", "arbitrary"),
    )(a, *pieces, *([] if after is None else [after]))


def _win(arr, start=0, width=None):
    width = arr.shape[1] if width is None else width
    assert start % width == 0
    return (arr, start // width, width)


def _row_specs(rows, tm):
    return [pl.BlockSpec((tm, w), functools.partial(lambda i, cb: (i, cb), cb=cb)) for (_, cb, w) in rows]


def _full_spec(p):
    nd = p.ndim
    return pl.BlockSpec(p.shape, lambda i, nd=nd: (0,) * nd)


def _rowwise(fn, rows, params, outs, *, name, tm, after=None):
    t = rows[0][0].shape[0]
    tm = min(tm, t)
    assert t % tm == 0
    nr, npar = len(rows), len(params)
    n_after = 0 if after is None else 1

    def body(*refs):
        vals = [r[...] for r in refs[:nr + npar]]
        res = fn(*vals)
        for o_ref, r in zip(refs[nr + npar + n_after:], res):
            o_ref[...] = r.astype(o_ref.dtype)

    return pl.pallas_call(
        body, name=name, grid=(t // tm,),
        in_specs=_row_specs(rows, tm) + [_full_spec(p) for p in params] + [pl.BlockSpec(memory_space=pl.ANY)] * n_after,
        out_specs=[pl.BlockSpec((tm, w), lambda i: (i, 0)) for (w, _) in outs],
        out_shape=[jax.ShapeDtypeStruct((t, w), dt) for (w, dt) in outs],
        compiler_params=_cp("parallel"),
    )(*[r[0] for r in rows], *params, *([] if after is None else [after]))


def _rowwise_bwd(fn, rows, params, n_const, cots, *, name, tm, row_grad, add_to=None, packed=False):
    t = rows[0][0].shape[0]
    tm = min(tm, t)
    assert t % tm == 0
    nr, npar = len(rows), len(params)
    ndp = npar - n_const
    add_to = add_to or {}
    add_idx = sorted(add_to)
    flat_cots = [c for group in cots for c in group]
    kept = [i for i in range(nr) if row_grad[i] is not None]

    def body(*refs):
        pos = 0
        row_v = [r[...] for r in refs[pos:pos + nr]]; pos += nr
        par_v = [r[...] for r in refs[pos:pos + npar]]; pos += npar
        cot_v = [r[...] for r in refs[pos:pos + len(flat_cots)]]; pos += len(flat_cots)
        add_v = [r[...] for r in refs[pos:pos + len(add_idx)]]; pos += len(add_idx)
        if packed:
            offs = [sum(rows[i][2] for i in kept[:q]) for q in range(len(kept))]
            rg_refs = [refs[pos].at[:, o:o + rows[i][2]] for o, i in zip(offs, kept)]; pos += 1
        else:
            rg_refs = refs[pos:pos + len(kept)]; pos += len(kept)
        pg_refs = refs[pos:pos + ndp]

        consts = par_v[ndp:]
        res, vjp = jax.vjp(lambda *args: tuple(fn(*args, *consts)), *row_v, *par_v[:ndp])
        cot_in, q = [], 0
        for j, group in enumerate(cots):
            c = None
            for _ in group:
                cv = cot_v[q].astype(F32); q += 1
                c = cv if c is None else c + cv
            c = jnp.zeros(res[j].shape, F32) if c is None else c
            cot_in.append(c.astype(res[j].dtype))
        grads = vjp(tuple(cot_in))
        for ref, i in zip(rg_refs, kept):
            g = grads[i].astype(F32)
            if i in add_to:
                g = g + add_v[add_idx.index(i)].astype(F32)
            ref[...] = g.astype(ref.dtype)

        @pl.when(pl.program_id(0) == 0)
        def _():
            for ref in pg_refs:
                ref[...] = jnp.zeros_like(ref)

        for ref, g in zip(pg_refs, grads[nr:]):
            ref[...] += g.astype(F32)

    cot_specs = [pl.BlockSpec((tm, c.shape[1]), lambda i: (i, 0)) for c in flat_cots]
    add_specs = [pl.BlockSpec((tm, add_to[i].shape[1]), lambda i_: (i_, 0)) for i in add_idx]
    widths = [sum(rows[i][2] for i in kept)] if packed else [rows[i][2] for i in kept]
    n_rg = len(widths)
    out_specs = [pl.BlockSpec((tm, w), lambda i_: (i_, 0)) for w in widths] + [_full_spec(p) for p in params[:ndp]]
    out_shape = [jax.ShapeDtypeStruct((t, w), row_grad[kept[q]]) for q, w in enumerate(widths)] + [
        jax.ShapeDtypeStruct(p.shape, F32) for p in params[:ndp]]
    res = pl.pallas_call(
        body, name=name, grid=(t // tm,),
        in_specs=_row_specs(rows, tm) + [_full_spec(p) for p in params] + cot_specs + add_specs,
        out_specs=out_specs, out_shape=out_shape,
        compiler_params=_cp("arbitrary"),
    )(*[r[0] for r in rows], *params, *flat_cots, *[add_to[i] for i in add_idx])
    return list(res[:n_rg]), list(res[n_rg:])


def _rms(x, g):
    xf = x.astype(F32)
    return xf * lax.rsqrt(jnp.mean(xf * xf, axis=-1, keepdims=True) + NORM_EPS) * g


def _softplus(x):
    return jnp.maximum(x, 0.0) + jnp.log(1.0 + jnp.exp(-jnp.abs(x)))


def _fn_pre(x, g):
    return (_rms(x, g).astype(BF16),)


def _fn_res(x, u, g_post):
    return (x + _rms(u, g_post),)


def _fn_res_pre(x, u, g_post, g_pre):
    xn = x + _rms(u, g_post)
    return xn, _rms(xn, g_pre).astype(BF16)


def _fn_mix(zga, zgb, ya, yb):
    return ((jax.nn.sigmoid(zga) * ya + jax.nn.sigmoid(zgb) * yb).astype(BF16),)


def _fn_swiglu(gate, up):
    gate, up = gate.astype(F32), up.astype(F32)
    return ((gate * jax.nn.sigmoid(gate) * up).astype(BF16),)


def _fn_prep(zk, zw, za, zg, decay_base, d_up, iclr_base, i_up, g_up, kns, kis, e_hd, e_dh):
    w_log = -_softplus(-(decay_base + _dot(jnp.tanh(zw), d_up))) - 0.5
    lw = -jnp.exp(w_log)
    a = jax.nn.sigmoid(iclr_base + _dot(za, i_up))
    g = _dot(jax.nn.sigmoid(zg), g_up)
    kn = zk * kns
    ss = _dot(kn * kn, e_dh)
    inv = lax.rsqrt(jnp.maximum(ss, 1e-24))
    kk = kn * _dot_split_a(inv, e_hd)
    k2 = zk * (1.0 + (a - 1.0) * kis)
    return lw, k2, kk, a, g


def _fn_post(y, r, k2, v, g, lnx_w, lnx_b, bonus, e_hd, e_dh):
    mu = _dot_split_a(_dot(y, e_dh) * (1.0 / HEAD), e_hd)
    yc = y - mu
    var = _dot(yc * yc, e_dh) * (1.0 / HEAD)
    yn = yc * _dot_split_a(lax.rsqrt(var + GROUP_NORM_EPS), e_hd)
    bs = _dot_split_a(_dot(r * k2 * bonus, e_dh), e_hd)
    return (((yn * lnx_w + lnx_b + bs * v) * g).astype(BF16),)


def _shift_fwd(p, col0, ncols, mix, seq, *, name, cw=256):
    t = p.shape[0]
    assert col0 % cw == 0 and ncols % cw == 0 and t % seq == 0
    cb0 = col0 // cw

    def body(p_ref, m_ref, z_ref):
        pv = p_ref[...]
        row = lax.broadcasted_iota(jnp.int32, pv.shape, 0)
        prev = jnp.where(row == 0, 0.0, pltpu.roll(pv, 1, axis=0))
        z_ref[...] = pv + (prev - pv) * m_ref[...]

    return pl.pallas_call(
        body, name=name, grid=(t // seq, ncols // cw),
        in_specs=[pl.BlockSpec((seq, cw), lambda b, c: (b, c + cb0)), pl.BlockSpec((1, cw), lambda b, c: (0, c))],
        out_specs=pl.BlockSpec((seq, cw), lambda b, c: (b, c)),
        out_shape=jax.ShapeDtypeStruct((t, ncols), F32),
        compiler_params=_cp("parallel", "parallel"),
    )(p, mix)


def _shift_bwd(p, col0, ncols, mix, dz_parts, seq, *, name, cw=256):
    t = p.shape[0]
    cb0 = col0 // cw
    n = len(dz_parts)

    def body(*refs):
        p_ref, m_ref = refs[:2]
        dp_ref, dm_ref = refs[2 + n:]
        dz = refs[2][...].astype(F32)
        for r in refs[3:2 + n]:
            dz = dz + r[...].astype(F32)
        pv = p_ref[...]
        mixv = m_ref[...]
        row = lax.broadcasted_iota(jnp.int32, pv.shape, 0)
        prev = jnp.where(row == 0, 0.0, pltpu.roll(pv, 1, axis=0))
        u = dz * mixv
        nxt = jnp.where(row == seq - 1, 0.0, pltpu.roll(u, seq - 1, axis=0))
        dp_ref[...] = (dz - u + nxt).astype(dp_ref.dtype)

        @pl.when(pl.program_id(1) == 0)
        def _():
            dm_ref[...] = jnp.zeros_like(dm_ref)

        dm_ref[...] += jnp.sum(dz * (prev - pv), axis=0, keepdims=True)

    return pl.pallas_call(
        body, name=name, grid=(ncols // cw, t // seq),
        in_specs=[pl.BlockSpec((seq, cw), lambda c, b: (b, c + cb0)), pl.BlockSpec((1, cw), lambda c, b: (0, c))]
        + [pl.BlockSpec((seq, cw), lambda c, b: (b, c))] * n,
        out_specs=[pl.BlockSpec((seq, cw), lambda c, b: (b, c)), pl.BlockSpec((1, cw), lambda c, b: (0, c))],
        out_shape=[jax.ShapeDtypeStruct((t, ncols), BF16), jax.ShapeDtypeStruct((1, ncols), F32)],
        compiler_params=_cp("parallel", "arbitrary"),
    )(p, mix, *dz_parts)


def _each(f, *lists):
    return [f(*xs) for xs in zip(*lists)]


def _tri_inv(low):
    c = low[0].shape[0]
    ti = lax.broadcasted_iota(jnp.int32, (c, c), 0)
    si = lax.broadcasted_iota(jnp.int32, (c, c), 1)
    eye = (ti == si).astype(F32)
    inside = (ti // 4) == (si // 4)
    base = [jnp.where(inside, m, 0.0) for m in low]
    acc = _each(lambda m: _dot(eye - m, eye + _dot(m, m)), base)
    size = 8
    while size <= c:
        wider = (ti // size) == (si // size)
        keep = jnp.logical_and(wider, jnp.logical_not(inside))
        acc = _each(lambda p, m: p - _dot(_dot(p, jnp.where(keep, m, 0.0)), p), acc, low)
        inside, size = wider, size * 2
    return acc


def _stack_rows(a, b):
    return jnp.concatenate([a, b], axis=0)


@jax.custom_vjp
def _split_rows(x):
    h = x.shape[0] // 2
    return x[:h], x[h:]


def _split_rows_fwd(x):
    return _split_rows(x), None


def _split_rows_bwd(_, g):
    return (jnp.concatenate(g, axis=0),)


_split_rows.defvjp(_split_rows_fwd, _split_rows_bwd)


def _masked_halves(stacked, top_mask, bottom_mask):
    halves = _each(_split_rows, stacked)
    return ([jnp.where(top_mask, t, 0.0) for t, _ in halves], [jnp.where(bottom_mask, b, 0.0) for _, b in halves])


@jax.custom_vjp
def _tri_inv_known(low, inv):
    return inv


def _tri_inv_known_fwd(low, inv):
    return inv, inv


def _tri_inv_known_bwd(inv, g):
    dlow = _each(lambda t, gg: -_dot(_dot(t, gg, _TN), t, _NT), inv, g)
    return dlow, _each(jnp.zeros_like, inv)


_tri_inv_known.defvjp(_tri_inv_known_fwd, _tri_inv_known_bwd)


def _wkv_chunk(s0, r, lw, k, v, kk, a, inv=None):
    c = r[0].shape[0]
    ti = lax.broadcasted_iota(jnp.int32, (c, c), 0)
    si = lax.broadcasted_iota(jnp.int32, (c, c), 1)
    incl, strict = ti >= si, ti > si
    tri = incl.astype(F32)
    cum = _each(lambda x: _dot_split_b(tri, x, 3), lw)
    eg = _each(jnp.exp, cum)
    egp = _each(lambda cs, x: jnp.exp(cs - x), cum, lw)
    ei = _each(lambda cs: jnp.exp(-cs), cum)
    rh, kkh, kt = _each(jnp.multiply, r, eg), _each(jnp.multiply, kk, egp), _each(jnp.multiply, k, ei)
    bt = _each(lambda p, q, e: (p * q) * e, a, kk, ei)
    both = _each(_stack_rows, kkh, rh)
    on_b, on_k, on_s = _each(_dot_nt, both, bt), _each(_dot_nt, both, kt), _each(_dot_nt, both, s0)
    lb, mb = _masked_halves(on_b, strict, incl)
    lk, mk = _masked_halves(on_k, strict, incl)
    on_s = _each(_split_rows, on_s)
    on_v = _each(lambda p, q, x: _split_rows(_dot(_stack_rows(p, q), x)), lk, mk, v)
    rhs = _each(lambda p, q: p[0] + q[0], on_s, on_v)
    inv = _tri_inv(lb) if inv is None else _tri_inv_known(lb, inv)
    u = _each(lambda t, x: -_dot(t, x), inv, rhs)
    y = _each(lambda p, m1, uu, q: p[1] + _dot(m1, uu) + q[1], on_s, mb, u, on_v)
    s1 = _each(lambda s, uu, x, b, kq, w: (s + _dot_tn(_stack_rows(uu, x), _stack_rows(b, kq)))
               * jnp.exp(jnp.sum(w, axis=0, keepdims=True)), s0, u, v, bt, kt, lw)
    return y, s1, inv


WKV_HEADS = 16
WKV_COLS = WKV_HEADS * HEAD
WKV_GROUPS = N_HEADS // WKV_HEADS


def _head_cols(ref):
    return [ref[:, h * HEAD:(h + 1) * HEAD] for h in range(ref.shape[1] // HEAD)]


def _wkv_specs(seq, rev):
    nc = seq // CHUNK

    def rows(col0):
        cb0 = col0 // WKV_COLS
        if rev:
            return pl.BlockSpec((CHUNK, WKV_COLS), lambda b, h, c: (b * nc + nc - 1 - c, cb0 + h))
        return pl.BlockSpec((CHUNK, WKV_COLS), lambda b, h, c: (b * nc + c, cb0 + h))

    if rev:
        st = pl.BlockSpec((1, 1, WKV_HEADS, HEAD, HEAD), lambda b, h, c: (b * WKV_GROUPS + h, nc - 1 - c, 0, 0, 0))
    else:
        st = pl.BlockSpec((1, 1, WKV_HEADS, HEAD, HEAD), lambda b, h, c: (b * WKV_GROUPS + h, c, 0, 0, 0))
    return rows, st


def _wkv_fwd(z_rkv, lw, k2, kk, a, seq):
    t = z_rkv.shape[0]
    nb, nc = t // seq, seq // CHUNK
    rows, st = _wkv_specs(seq, False)

    def body(r_ref, v_ref, lw_ref, k_ref, kk_ref, a_ref, y_ref, st_ref, inv_ref, s_scr):
        @pl.when(pl.program_id(2) == 0)
        def _():
            s_scr[...] = jnp.zeros_like(s_scr)

        s0 = [s_scr[h] for h in range(WKV_HEADS)]
        y, s1, inv = _wkv_chunk(s0, *[_head_cols(ref) for ref in (r_ref, lw_ref, k_ref, v_ref, kk_ref, a_ref)])
        for h in range(WKV_HEADS):
            st_ref[0, 0, h] = s0[h]
            inv_ref[0, 0, h] = inv[h]
            y_ref[:, h * HEAD:(h + 1) * HEAD] = y[h]
            s_scr[h] = s1[h]

    per_chunk = jax.ShapeDtypeStruct((nb * WKV_GROUPS, nc, WKV_HEADS, HEAD, HEAD), F32)
    return pl.pallas_call(
        body, name="wkv_fwd", grid=(nb, WKV_GROUPS, nc),
        in_specs=[rows(0), rows(2 * D), rows(0), rows(0), rows(0), rows(0)],
        out_specs=[rows(0), st, st],
        out_shape=[jax.ShapeDtypeStruct((t, D), F32), per_chunk, per_chunk],
        scratch_shapes=[pltpu.VMEM((WKV_HEADS, HEAD, HEAD), F32)],
        compiler_params=_cp("parallel", "parallel", "arbitrary"),
    )(z_rkv, z_rkv, lw, k2, kk, a)


def _wkv_bwd(z_rkv, lw, k2, kk, a, states, invs, dy, seq):
    t = z_rkv.shape[0]
    nb, nc = t // seq, seq // CHUNK
    rows, st = _wkv_specs(seq, True)

    def body(r_ref, v_ref, lw_ref, k_ref, kk_ref, a_ref, st_ref, inv_ref, dy_ref,
             dr_ref, dlw_ref, dk_ref, dv_ref, dkk_ref, da_ref, ds_scr):
        @pl.when(pl.program_id(2) == 0)
        def _():
            ds_scr[...] = jnp.zeros_like(ds_scr)

        s0 = [st_ref[0, 0, h] for h in range(WKV_HEADS)]
        inv = [inv_ref[0, 0, h] for h in range(WKV_HEADS)]
        _, vjp = jax.vjp(lambda *args: _wkv_chunk(*args, inv=inv)[:2],
                         s0, *[_head_cols(ref) for ref in (r_ref, lw_ref, k_ref, v_ref, kk_ref, a_ref)])
        grads = vjp(([x.astype(F32) for x in _head_cols(dy_ref)], [ds_scr[h] for h in range(WKV_HEADS)]))
        for h in range(WKV_HEADS):
            ds_scr[h] = grads[0][h]
            for ref, g in zip((dr_ref, dlw_ref, dk_ref, dv_ref, dkk_ref, da_ref), grads[1:]):
                ref[:, h * HEAD:(h + 1) * HEAD] = g[h].astype(ref.dtype)

    return pl.pallas_call(
        body, name="wkv_bwd", grid=(nb, WKV_GROUPS, nc),
        in_specs=[rows(0), rows(2 * D), rows(0), rows(0), rows(0), rows(0), st, st, rows(0)],
        out_specs=[rows(0)] * 6,
        out_shape=[jax.ShapeDtypeStruct((t, D), BF16)] * 6,
        scratch_shapes=[pltpu.VMEM((WKV_HEADS, HEAD, HEAD), F32)],
        compiler_params=_cp("parallel", "parallel", "arbitrary"),
    )(z_rkv, z_rkv, lw, k2, kk, a, states, invs, dy)


def _softmax(s):
    e = jnp.exp(s - jnp.max(s, axis=-1, keepdims=True))
    return e * (1.0 / jnp.sum(e, axis=-1, keepdims=True))


ATT_HEADS = 8
ATT_COLS = ATT_HEADS * HEAD
ATT_GROUPS = N_HEADS // ATT_HEADS


def _attn_chunk(q, kb, vb, bias, valid):
    s = _each(lambda x, y, z: jnp.where(valid, _dot_nt(x * (HEAD ** -0.5), y) + z, MASK_VALUE), q, kb, bias)
    return _each(_dot, _each(_softmax, s), vb)


def _pad_fill(pad_ref, src_ref):
    pad_ref[0:LEFT, :] = jnp.zeros((LEFT, pad_ref.shape[1]), pad_ref.dtype)
    pad_ref[LEFT:, :] = src_ref[...].astype(pad_ref.dtype)


def _band_heads(pad_ref, start):
    return [pad_ref[pl.ds(start, BAND), h * HEAD:(h + 1) * HEAD].astype(F32) for h in range(ATT_HEADS)]


def _band_valid(c):
    return (c * CHUNK - LEFT + lax.broadcasted_iota(jnp.int32, (1, BAND), 1)) >= 0


def _bias_spec():
    return pl.BlockSpec((ATT_HEADS, CHUNK, BAND), lambda h, b, c: (h, 0, 0))


def _attn_fwd(proj, bias, seq):
    t = proj.shape[0]
    nb, nc = t // seq, seq // CHUNK
    cq = C_Q // ATT_COLS

    def body(q_ref, k_ref, v_ref, b_ref, o_ref, kpad, vpad):
        c = pl.program_id(2)

        @pl.when(c == 0)
        def _():
            _pad_fill(kpad, k_ref)
            _pad_fill(vpad, v_ref)

        start = pl.multiple_of(c * CHUNK, CHUNK)
        o = _attn_chunk(_head_cols(q_ref), _band_heads(kpad, start), _band_heads(vpad, start),
                        [b_ref[h] for h in range(ATT_HEADS)], _band_valid(c))
        for h in range(ATT_HEADS):
            o_ref[:, h * HEAD:(h + 1) * HEAD] = o[h].astype(o_ref.dtype)

    return pl.pallas_call(
        body, name="attn_fwd", grid=(ATT_GROUPS, nb, nc),
        in_specs=[pl.BlockSpec((CHUNK, ATT_COLS), lambda h, b, c: (b * nc + c, cq + h)),
                  pl.BlockSpec((seq, ATT_COLS), lambda h, b, c: (b, cq + ATT_GROUPS + h)),
                  pl.BlockSpec((seq, ATT_COLS), lambda h, b, c: (b, cq + 2 * ATT_GROUPS + h)),
                  _bias_spec()],
        out_specs=pl.BlockSpec((CHUNK, ATT_COLS), lambda h, b, c: (b * nc + c, h)),
        out_shape=jax.ShapeDtypeStruct((t, D), BF16),
        scratch_shapes=[pltpu.VMEM((seq + LEFT, ATT_COLS), BF16)] * 2,
        compiler_params=_cp("parallel", "arbitrary", "arbitrary"),
    )(proj, proj, proj, bias)


def _attn_bwd(proj, bias, do, seq):
    t = proj.shape[0]
    nb, nc = t // seq, seq // CHUNK
    cq = C_Q // ATT_COLS

    def body(q_ref, k_ref, v_ref, b_ref, do_ref, dq_ref, dk_ref, dv_ref, db_ref, kpad, vpad, dkpad, dvpad):
        b, c = pl.program_id(1), pl.program_id(2)

        @pl.when(c == 0)
        def _():
            _pad_fill(kpad, k_ref)
            _pad_fill(vpad, v_ref)
            dkpad[...] = jnp.zeros_like(dkpad)
            dvpad[...] = jnp.zeros_like(dvpad)

        @pl.when(jnp.logical_and(b == 0, c == 0))
        def _():
            db_ref[...] = jnp.zeros_like(db_ref)

        start = pl.multiple_of(c * CHUNK, CHUNK)
        _, vjp = jax.vjp(functools.partial(_attn_chunk, valid=_band_valid(c)),
                         _head_cols(q_ref), _band_heads(kpad, start), _band_heads(vpad, start),
                         [b_ref[h] for h in range(ATT_HEADS)])
        dq, dkb, dvb, dbias = vjp([x.astype(F32) for x in _head_cols(do_ref)])
        for h in range(ATT_HEADS):
            sl = slice(h * HEAD, (h + 1) * HEAD)
            dq_ref[:, sl] = dq[h].astype(dq_ref.dtype)
            dkpad[pl.ds(start, BAND), sl] += dkb[h].astype(F32)
            dvpad[pl.ds(start, BAND), sl] += dvb[h].astype(F32)
            db_ref[h] += dbias[h]

        @pl.when(c == nc - 1)
        def _():
            dk_ref[...] = dkpad[LEFT:, :].astype(dk_ref.dtype)
            dv_ref[...] = dvpad[LEFT:, :].astype(dv_ref.dtype)

    kv_out = pl.BlockSpec((seq, ATT_COLS), lambda h, b, c: (b, h))
    return pl.pallas_call(
        body, name="attn_bwd", grid=(ATT_GROUPS, nb, nc),
        in_specs=[pl.BlockSpec((CHUNK, ATT_COLS), lambda h, b, c: (b * nc + c, cq + h)),
                  pl.BlockSpec((seq, ATT_COLS), lambda h, b, c: (b, cq + ATT_GROUPS + h)),
                  pl.BlockSpec((seq, ATT_COLS), lambda h, b, c: (b, cq + 2 * ATT_GROUPS + h)),
                  _bias_spec(),
                  pl.BlockSpec((CHUNK, ATT_COLS), lambda h, b, c: (b * nc + c, h))],
        out_specs=[pl.BlockSpec((CHUNK, ATT_COLS), lambda h, b, c: (b * nc + c, h)), kv_out, kv_out,
                   pl.BlockSpec((ATT_HEADS, CHUNK, BAND), lambda h, b, c: (h, 0, 0))],
        out_shape=[jax.ShapeDtypeStruct((t, D), BF16)] * 3 + [jax.ShapeDtypeStruct((N_HEADS, CHUNK, BAND), F32)],
        scratch_shapes=[pltpu.VMEM((seq + LEFT, ATT_COLS), BF16)] * 2 + [pltpu.VMEM((seq + LEFT, ATT_COLS), F32)] * 2,
        compiler_params=_cp("parallel", "arbitrary", "arbitrary"),
    )(proj, proj, proj, bias, do)


def _xattn_tile(q, k, v):
    s = _dot_nt(q, k) * ((MEM_WIDTH // MEM_HEADS) ** -0.5)
    return _dot(_softmax(s), v)


def _xattn_fwd(qm, kvm, seq, n_mem, tq=512):
    t = qm.shape[0]
    tq = min(tq, seq)
    nb, nq = t // seq, seq // tq

    def body(q_ref, k_ref, v_ref, o_ref):
        o_ref[...] = _xattn_tile(q_ref[...], k_ref[...], v_ref[...]).astype(o_ref.dtype)

    return pl.pallas_call(
        body, name="xattn_fwd", grid=(nb, MEM_HEADS, nq),
        in_specs=[pl.BlockSpec((tq, LANE), lambda b, h, i: (b * nq + i, h)),
                  pl.BlockSpec((n_mem, LANE), lambda b, h, i: (b, h)),
                  pl.BlockSpec((n_mem, LANE), lambda b, h, i: (b, MEM_HEADS + h))],
        out_specs=pl.BlockSpec((tq, LANE), lambda b, h, i: (b * nq + i, h)),
        out_shape=jax.ShapeDtypeStruct((t, MEM_WIDTH), BF16),
        compiler_params=_cp("parallel", "parallel", "parallel"),
    )(qm, kvm, kvm)


def _xattn_bwd(qm, kvm, do, seq, n_mem, tq=512):
    t = qm.shape[0]
    tq = min(tq, seq)
    nb, nq = t // seq, seq // tq

    def body(q_ref, k_ref, v_ref, do_ref, dq_ref, dkv_ref, dk_acc, dv_acc):
        i = pl.program_id(2)

        @pl.when(i == 0)
        def _():
            dk_acc[...] = jnp.zeros_like(dk_acc)
            dv_acc[...] = jnp.zeros_like(dv_acc)

        _, vjp = jax.vjp(_xattn_tile, q_ref[...], k_ref[...], v_ref[...])
        dq, dk, dv = vjp(do_ref[...].astype(F32))
        dq_ref[...] = dq.astype(dq_ref.dtype)
        dk_acc[...] += dk
        dv_acc[...] += dv

        @pl.when(i == nq - 1)
        def _():
            dkv_ref[0] = dk_acc[...].astype(dkv_ref.dtype)
            dkv_ref[1] = dv_acc[...].astype(dkv_ref.dtype)

    dq, dkv = pl.pallas_call(
        body, name="xattn_bwd", grid=(nb, MEM_HEADS, nq),
        in_specs=[pl.BlockSpec((tq, LANE), lambda b, h, i: (b * nq + i, h)),
                  pl.BlockSpec((n_mem, LANE), lambda b, h, i: (b, h)),
                  pl.BlockSpec((n_mem, LANE), lambda b, h, i: (b, MEM_HEADS + h)),
                  pl.BlockSpec((tq, LANE), lambda b, h, i: (b * nq + i, h))],
        out_specs=[pl.BlockSpec((tq, LANE), lambda b, h, i: (b * nq + i, h)),
                   pl.BlockSpec((2, n_mem, LANE), lambda b, h, i: (0, b, h))],
        out_shape=[jax.ShapeDtypeStruct((t, MEM_WIDTH), BF16), jax.ShapeDtypeStruct((2, nb * n_mem, MEM_WIDTH), BF16)],
        scratch_shapes=[pltpu.VMEM((n_mem, LANE), F32)] * 2,
        compiler_params=_cp("parallel", "parallel", "arbitrary"),
    )(qm, kvm, kvm, do)
    return dq, jnp.concatenate([dkv[0], dkv[1]], axis=1)


def _loss_head(x, u, g_post, target, tm=256):
    t, d = x.shape
    tm = min(tm, t)

    def tile_loss(xv, uv, gv, tv):
        diff = _fn_res(xv, uv, gv)[0] - tv
        return 0.5 * jnp.sum(jnp.mean(diff * diff, axis=-1, keepdims=True), axis=0, keepdims=True)

    def body(x_ref, u_ref, g_ref, t_ref, l_ref, dx_ref, du_ref, dg_ref):
        @pl.when(pl.program_id(0) == 0)
        def _():
            l_ref[...] = jnp.zeros_like(l_ref)
            dg_ref[...] = jnp.zeros_like(dg_ref)

        tv = t_ref[...]
        part, vjp = jax.vjp(lambda xv, uv, gv: tile_loss(xv, uv, gv, tv), x_ref[...], u_ref[...], g_ref[...])
        dx, du, dg = vjp(jnp.ones((1, 1), F32))
        l_ref[...] += part
        dx_ref[...] = dx
        du_ref[...] = du.astype(du_ref.dtype)
        dg_ref[...] += dg

    rows = pl.BlockSpec((tm, d), lambda i: (i, 0))
    vec = pl.BlockSpec((1, d), lambda i: (0, 0))
    return pl.pallas_call(
        body, name="loss_head", grid=(t // tm,),
        in_specs=[rows, rows, vec, rows],
        out_specs=[pl.BlockSpec((8, LANE), lambda i: (0, 0)), rows, rows, vec],
        out_shape=[jax.ShapeDtypeStruct((8, LANE), F32), jax.ShapeDtypeStruct((t, d), F32),
                   jax.ShapeDtypeStruct((t, d), BF16), jax.ShapeDtypeStruct((1, d), F32)],
        compiler_params=_cp("arbitrary"),
    )(x, u, g_post, target)


def _mesh_pos():
    return lax.axis_index("x"), lax.axis_index("y"), lax.axis_index("c")


def _peer(pos, d):
    x, y, c = pos
    return ((1 - x) if d & 4 else x, (1 - y) if d & 2 else y, (1 - c) if d & 1 else c)


def _flat(pos):
    return 4 * pos[0] + 2 * pos[1] + pos[2]


def _exchange(arrays, scatter, *, name):
    n = len(arrays)
    shapes = [a.shape[1:] if scatter else a.shape for a in arrays]

    def body(*refs):
        ins, outs = refs[:n], refs[n:2 * n]
        send, recv, loc = refs[2 * n:]
        pos = _mesh_pos()
        me = _flat(pos)
        pending = []
        for i in range(n):
            own = pltpu.make_async_copy(ins[i].at[me] if scatter else ins[i], outs[i].at[me], loc.at[i])
            own.start()
            pending.append(own)
            for d in range(1, N_DEV):
                peer = _peer(pos, d)
                src = ins[i].at[_flat(peer)] if scatter else ins[i]
                out_cp = pltpu.make_async_remote_copy(
                    src_ref=src, dst_ref=outs[i].at[me], send_sem=send.at[i, d - 1], recv_sem=recv.at[i, d - 1],
                    device_id=peer, device_id_type=pl.DeviceIdType.MESH)
                out_cp.start()
                pending.append(out_cp)
        for i in range(n):
            own = pending[i * N_DEV]
            for d in range(1, N_DEV):
                peer = _peer(pos, d)
                src = ins[i].at[_flat(peer)] if scatter else ins[i]
                pending[i * N_DEV + d].wait_send()
                pltpu.make_async_remote_copy(
                    src_ref=src, dst_ref=outs[i].at[_flat(peer)], send_sem=send.at[i, d - 1], recv_sem=recv.at[i, d - 1],
                    device_id=peer, device_id_type=pl.DeviceIdType.MESH).wait_recv()
            own.wait()

    hbm = pl.BlockSpec(memory_space=pltpu.HBM)
    return pl.pallas_call(
        body, name=name,
        in_specs=[hbm] * n, out_specs=[hbm] * n,
        out_shape=[jax.ShapeDtypeStruct((N_DEV,) + tuple(s), a.dtype) for s, a in zip(shapes, arrays)],
        scratch_shapes=[pltpu.SemaphoreType.DMA((n, N_DEV - 1)), pltpu.SemaphoreType.DMA((n, N_DEV - 1)),
                        pltpu.SemaphoreType.DMA((n,))],
    )(*arrays)


_HBM = pl.BlockSpec(memory_space=pltpu.HBM)
_SEM = pl.BlockSpec(memory_space=pltpu.SEMAPHORE)
_DATAFLOW = pltpu.SideEffectType.DATAFLOW_SIDE_EFFECTING


_ALL_PEERS = tuple(range(1, N_DEV))
_SIBLING_AND_SAME_CORE = (1, 2, 4, 6)


def _remote_copies(ins, lands, send, recv, scatter, dists):
    pos = _mesh_pos()
    me = _flat(pos)
    out = []
    for i in range(len(ins)):
        for j, d in enumerate(dists):
            peer = _peer(pos, d)
            src = ins[i].at[_flat(peer)] if scatter else ins[i]
            pair = i * len(dists) + j
            sems = dict(send_sem=send.at[pair], recv_sem=recv.at[pair], device_id=peer,
                        device_id_type=pl.DeviceIdType.MESH)
            out.append((pltpu.make_async_remote_copy(src_ref=src, dst_ref=lands[i].at[me], **sems),
                        pltpu.make_async_remote_copy(src_ref=src, dst_ref=lands[i].at[_flat(peer)], **sems)))
    return out


def _exchange_start(arrays, scatter, after, *, name, dists=_ALL_PEERS):
    n = len(arrays)
    shapes = [a.shape[1:] if scatter else a.shape for a in arrays]
    lands = [pltpu.with_memory_space_constraint(lax.empty((N_DEV,) + tuple(s), a.dtype), pltpu.HBM)
             for s, a in zip(shapes, arrays)]
    srcs = [pltpu.with_memory_space_constraint(a, pltpu.HBM) for a in arrays]

    def body(*refs):
        ins, land_refs = refs[:n], refs[n:2 * n]
        send, recv, token = refs[2 * n + 1], refs[2 * n + 2], refs[-1]
        for going, _ in _remote_copies(ins, land_refs, send, recv, scatter, dists):
            going.start()
        token[...] = jnp.zeros_like(token)

    sems = pltpu.SemaphoreType.DMA((n * len(dists),))
    res = pl.pallas_call(
        body, name=name,
        out_shape=(sems, sems, *[pltpu.HBM(a.shape, a.dtype) for a in srcs + lands], jax.ShapeDtypeStruct((8, LANE), F32)),
        in_specs=[_HBM] * (2 * n) + [pl.BlockSpec(memory_space=pl.ANY)],
        out_specs=(_SEM, _SEM, *[_HBM] * (2 * n), pl.BlockSpec(memory_space=pltpu.VMEM)),
        input_output_aliases={i: 2 + i for i in range(2 * n)},
        compiler_params=pltpu.CompilerParams(has_side_effects=_DATAFLOW),
    )(*srcs, *lands, after)
    return (n, scatter, dists, res[0], res[1], list(res[2:2 + 2 * n])), res[-1]


def _exchange_wait(handle, after, own, *, name):
    n, scatter, dists, send, recv, thru = handle

    def body(*refs):
        ins, land_refs = refs[:n], refs[n:2 * n]
        for going, coming in _remote_copies(ins, land_refs, refs[2 * n], refs[2 * n + 1], scatter, dists):
            going.wait_send()
            coming.wait_recv()

    res = pl.pallas_call(
        body, name=name,
        out_shape=tuple(pltpu.HBM(a.shape, a.dtype) for a in thru),
        in_specs=[_HBM] * (2 * n) + [_SEM, _SEM] + [pl.BlockSpec(memory_space=pl.ANY)] * len(after),
        out_specs=tuple([_HBM] * (2 * n)),
        input_output_aliases={i: i for i in range(2 * n)},
        compiler_params=pltpu.CompilerParams(has_side_effects=_DATAFLOW),
    )(*thru, send, recv, *after)
    me = _flat(_mesh_pos())
    return [lax.dynamic_update_slice_in_dim(land, o[None].astype(land.dtype), me, 0) for land, o in zip(res[n:], own)]


_OTHER_CHIPS = (2, 4, 6)


def _relay_to_sibling(gathered, *, name):
    n, k = len(gathered), len(_OTHER_CHIPS)

    def body(*refs):
        ins, outs = refs[:n], refs[n:2 * n]
        send, recv = refs[2 * n:]
        pos = _mesh_pos()
        copies = []
        for i in range(n):
            for j, d in enumerate(_OTHER_CHIPS):
                cp = pltpu.make_async_remote_copy(
                    src_ref=ins[i].at[_flat(_peer(pos, d))], dst_ref=outs[i].at[j],
                    send_sem=send.at[i * k + j], recv_sem=recv.at[i * k + j],
                    device_id=_peer(pos, 1), device_id_type=pl.DeviceIdType.MESH)
                cp.start()
                copies.append(cp)
        for cp in copies:
            cp.wait()

    return pl.pallas_call(
        body, name=name, in_specs=[_HBM] * n, out_specs=[_HBM] * n,
        out_shape=[jax.ShapeDtypeStruct((k,) + g.shape[1:], g.dtype) for g in gathered],
        scratch_shapes=[pltpu.SemaphoreType.DMA((n * k,)), pltpu.SemaphoreType.DMA((n * k,))],
    )(*gathered)


def _adamw(parts, w, m, v, *, name, tr=128, after=None):
    r, c = w.shape
    align = 8 * 4 // parts.dtype.itemsize
    row_tiles = [d for d in range(align, min(tr, r) + 1, align) if r % d == 0]
    tr, tc = (max(row_tiles), c) if row_tiles else (r, LANE)
    assert c % tc == 0
    n_after = 0 if after is None else 1

    def body(p_ref, w_ref, m_ref, v_ref, *rest):
        g_ref, d_ref, nm_ref, nv_ref = rest[n_after:]
        g = p_ref[0].astype(F32)
        for j in range(1, N_DEV):
            g = g + p_ref[j].astype(F32)
        m2 = ADAM_B1 * m_ref[...] + (1.0 - ADAM_B1) * g
        v2 = ADAM_B2 * v_ref[...] + (1.0 - ADAM_B2) * (g * g)
        m_hat = m2 / (1.0 - ADAM_B1 ** ADAM_STEP)
        v_hat = v2 / (1.0 - ADAM_B2 ** ADAM_STEP)
        g_ref[...] = g
        d_ref[...] = -ADAM_LR * (m_hat / (jnp.sqrt(v_hat) + ADAM_EPS) + ADAM_WD * w_ref[...])
        nm_ref[...] = m2
        nv_ref[...] = v2

    spec = pl.BlockSpec((tr, tc), lambda i, j: (i, j))
    return pl.pallas_call(
        body, name=name, grid=(r // tr, c // tc),
        in_specs=[pl.BlockSpec((N_DEV, tr, tc), lambda i, j: (0, i, j)), spec, spec, spec]
        + [pl.BlockSpec(memory_space=pl.ANY)] * n_after,
        out_specs=[spec] * 4, out_shape=[jax.ShapeDtypeStruct((r, c), F32)] * 4,
        compiler_params=_cp("parallel", "parallel"),
    )(parts, w, m, v, *([] if after is None else [after]))


def _cols_to_full(g):
    return jnp.transpose(g, (1, 0, 2)).reshape(g.shape[1], N_DEV * g.shape[2])


def _full_to_cols(w):
    r, c = w.shape
    return jnp.transpose(w.reshape(r, N_DEV, c // N_DEV), (1, 0, 2))


def _cut(a, lo, hi, axis):
    return lax.slice_in_dim(a, lo, hi, axis=axis)


def _pad_to(a, size, axis):
    pads = [(0, 0)] * a.ndim
    pads[axis] = (0, size - a.shape[axis])
    return jnp.pad(a, pads)


def _pad_lora(w, axis=1):
    return jnp.concatenate([
        _pad_to(_cut(w, 0, LORA_W, axis), 128, axis), _pad_to(_cut(w, LORA_W, LORA_W + LORA_A, axis), 128, axis),
        _pad_to(_cut(w, LORA_W + LORA_A, w.shape[axis], axis), 256, axis)], axis=axis)


def _unpad_lora(wp, axis=1):
    return jnp.concatenate([_cut(wp, 0, LORA_W, axis), _cut(wp, 128, 128 + LORA_A, axis),
                            _cut(wp, 256, 256 + LORA_G, axis)], axis=axis)


def _permute_in(w, axis):
    rk = 3 * D
    lo = rk + LORA_W + LORA_A + LORA_G
    return jnp.concatenate([_cut(w, 0, rk, axis), _cut(w, lo, w.shape[axis], axis), _pad_lora(_cut(w, rk, lo, axis), axis)],
                           axis=axis)


def _unpermute_in(wp, axis):
    return jnp.concatenate([_cut(wp, 0, 3 * D, axis), _unpad_lora(_cut(wp, C_LORA, P_WIDTH, axis), axis),
                            _cut(wp, 3 * D, C_LORA, axis)], axis=axis)


def _rel_index():
    dist = jnp.arange(CHUNK)[:, None] - jnp.arange(BAND)[None, :] + LEFT
    return (jnp.minimum(dist, REL_CLIP) + (CHUNK - 1)).reshape(-1)


def _local_step(x, mem, target, wt, seq, n_mem, comm):
    t = x.shape[0]
    row = lambda a: a.reshape(1, -1).astype(F32)
    g_pre_mix, g_post_mix = row(wt["g_pre_mix"]), row(wt["g_post_mix"])
    g_pre_cross, g_post_cross, g_mem = row(wt["g_pre_cross"]), row(wt["g_post_cross"]), row(wt["g_mem"])
    g_pre_ffn, g_post_ffn = row(wt["g_pre_ffn"]), row(wt["g_post_ffn"])
    mix = row(wt["shift_mix"])
    mix_rkv, mix_lora = mix[:, :3 * D], _pad_lora(mix[:, 3 * D:])
    decay_base, iclr_base = row(wt["decay_base"]), row(wt["iclr_base"])
    kns, kis = row(wt["key_norm_scale"]), row(wt["key_iclr_scale"])
    lnx_w, lnx_b, bonus = row(wt["lnx_w"]), row(wt["lnx_b"]), row(wt["bonus_scale"])
    e_dh = (jnp.arange(D)[:, None] // HEAD == jnp.arange(N_HEADS)[None, :]).astype(F32)
    e_hd = e_dh.T
    onehot = (jnp.arange(REL_TABLE)[:, None] == _rel_index()[None, :]).astype(BF16)

    begun = comm.begun
    (h1,) = _rowwise(_fn_pre, [_win(x)], [g_pre_mix], [(D, BF16)], name="pre_mix", tm=512, after=begun)
    (mn,) = _rowwise(_fn_pre, [_win(mem)], [g_mem], [(D, BF16)], name="pre_mem", tm=512, after=begun)
    bias = _mm(wt["rel_bias"].astype(F32), onehot, name="mm_bias", split_a=3, after=begun).reshape(N_HEADS, CHUNK, BAND)
    wt = {**wt, **comm.first_weights([h1, mn, bias])}
    w_in = wt["w_in_p"]
    d_up = jnp.pad(wt["decay_up"].astype(F32), ((0, 128 - LORA_W), (0, 0)))
    i_up = jnp.pad(wt["iclr_up"].astype(F32), ((0, 128 - LORA_A), (0, 0)))
    g_up = jnp.pad(wt["gate_up"].astype(F32), ((0, 256 - LORA_G), (0, 0)))
    proj = _mm(h1, w_in, tb=True, name="mm_in", after=comm.first_token)
    z_rkv = _shift_fwd(proj, 0, 3 * D, mix_rkv, seq, name="shift_rkv")
    z_lora = _shift_fwd(proj, C_LORA, 512, mix_lora, seq, name="shift_lora")
    prep_rows = [_win(z_rkv, D, D), _win(z_lora, 0, 128), _win(z_lora, 128, 128), _win(z_lora, 256, 256)]
    prep_params = [decay_base, d_up, iclr_base, i_up, g_up, kns, kis, e_hd, e_dh]
    lw, k2, kk, a, g = _rowwise(_fn_prep, prep_rows, prep_params, [(D, F32)] * 5, name="rwkv_prep", tm=256)
    y, states, invs = _wkv_fwd(z_rkv, lw, k2, kk, a, seq)
    post_rows = [_win(y), _win(z_rkv, 0, D), _win(k2), _win(z_rkv, 2 * D, D), _win(g)]
    post_params = [lnx_w, lnx_b, bonus, e_hd, e_dh]
    (y_a,) = _rowwise(_fn_post, post_rows, post_params, [(D, BF16)], name="rwkv_post", tm=256)
    y_b = _attn_fwd(proj, bias, seq)
    wt = {**wt, **comm.late_weights(y_b)}
    ya_p = _mm(y_a, wt["w_branch_a"], name="mm_a")
    yb_p = _mm(y_b, wt["w_branch_b"], name="mm_b")
    mix_rows = [_win(proj, C_GA, D), _win(proj, C_GA + D, D), _win(ya_p), _win(yb_p)]
    (mixed,) = _rowwise(_fn_mix, mix_rows, [], [(D, BF16)], name="gate_mix", tm=512)
    mo = _mm(mixed, wt["w_out"], name="mm_out")
    x1, h2 = _rowwise(_fn_res_pre, [_win(x), _win(mo)], [g_post_mix, g_pre_cross], [(D, F32), (D, BF16)],
                      name="res_mix", tm=512)
    qm = _mm(h2, wt["w_q_mem"], name="mm_q")
    kvm = _mm(mn, wt["w_kv_mem"], name="mm_kv")
    om = _xattn_fwd(qm, kvm, seq, n_mem)
    co = _mm(om, wt["w_o_mem"], name="mm_o")
    x2, h3 = _rowwise(_fn_res_pre, [_win(x1), _win(co)], [g_post_cross, g_pre_ffn], [(D, F32), (D, BF16)],
                      name="res_cross", tm=512)
    gu = _mm(h3, wt["w_ffn_in"], tb=True, name="mm_ffn_in", out_dtype=BF16)
    (act,) = _rowwise(_fn_swiglu, [_win(gu, 0, FFN), _win(gu, FFN, FFN)], [], [(FFN, BF16)], name="swiglu", tm=256)
    ff = _mm(act, wt["w_ffn_out"], name="mm_ffn_out")

    gw = {}
    loss, dx2, dff, gw["g_post_ffn"] = _loss_head(x2, ff, g_post_ffn, target)
    dact = _mm(dff, wt["w_ffn_out"], tb=True, name="mm_ffn_out_dx", out_dtype=BF16)
    gw["w_ffn_out"] = _mm(act, dff, ta=True, name="mm_ffn_out_dw", out_dtype=BF16)
    (dgu,), _ = _rowwise_bwd(_fn_swiglu, [_win(gu, 0, FFN), _win(gu, FFN, FFN)], [], 0, [[dact]],
                             name="swiglu_bwd", tm=256, row_grad=[BF16, BF16], packed=True)
    dh3 = _mm(dgu, wt["w_ffn_in"], name="mm_ffn_in_dx", out_dtype=BF16)
    gw["w_ffn_in"] = _mm(dgu, h3, ta=True, name="mm_ffn_in_dw", out_dtype=BF16)
    (dx1, dco), (gw["g_post_cross"], gw["g_pre_ffn"]) = _rowwise_bwd(
        _fn_res_pre, [_win(x1), _win(co)], [g_post_cross, g_pre_ffn], 0, [[dx2], [dh3]],
        name="res_cross_bwd", tm=256, row_grad=[F32, BF16])
    dom = _mm(dco, wt["w_o_mem"], tb=True, name="mm_o_dx", out_dtype=BF16)
    gw["w_o_mem"] = _mm(om, dco, ta=True, name="mm_o_dw", out_dtype=BF16)
    dqm, dkvm = _xattn_bwd(qm, kvm, dom, seq, n_mem)
    dh2 = _mm(dqm, wt["w_q_mem"], tb=True, name="mm_q_dx", out_dtype=BF16)
    gw["w_q_mem"] = _mm(h2, dqm, ta=True, name="mm_q_dw", out_dtype=BF16)
    dmn = _mm(dkvm, wt["w_kv_mem"], tb=True, name="mm_kv_dx", out_dtype=BF16)
    gw["w_kv_mem"] = _mm(mn, dkvm, ta=True, name="mm_kv_dw", out_dtype=BF16)
    _, (gw["g_mem"],) = _rowwise_bwd(_fn_pre, [_win(mem)], [g_mem], 0, [[dmn]], name="pre_mem_bwd", tm=256,
                                     row_grad=[None])
    (dx0, dmo), (gw["g_post_mix"], gw["g_pre_cross"]) = _rowwise_bwd(
        _fn_res_pre, [_win(x), _win(mo)], [g_post_mix, g_pre_cross], 0, [[dx1], [dh2]],
        name="res_mix_bwd", tm=256, row_grad=[F32, BF16])
    dmixed = _mm(dmo, wt["w_out"], tb=True, name="mm_out_dx", out_dtype=BF16)
    gw["w_out"] = _mm(mixed, dmo, ta=True, name="mm_out_dw", out_dtype=BF16)
    (dzga, dzgb, dya_p, dyb_p), _ = _rowwise_bwd(_fn_mix, mix_rows, [], 0, [[dmixed]], name="gate_mix_bwd", tm=256,
                                                 row_grad=[BF16] * 4)
    gw["w_branch_a"] = _mm(y_a, dya_p, ta=True, name="mm_a_dw", out_dtype=BF16)
    gw["w_branch_b"] = _mm(y_b, dyb_p, ta=True, name="mm_b_dw", out_dtype=BF16)
    token = comm.send_early(gw)
    dy_a = _mm(dya_p, wt["w_branch_a"], tb=True, name="mm_a_dx", out_dtype=BF16, after=token)
    dy_b = _mm(dyb_p, wt["w_branch_b"], tb=True, name="mm_b_dx", out_dtype=BF16, after=token)
    dq, dk, dv, dbias = _attn_bwd(proj, bias, dy_b, seq)
    gw["rel_bias"] = _mm(dbias.reshape(N_HEADS, CHUNK * BAND), onehot, tb=True, name="mm_bias_dw", split_a=2)
    (dy, dr_p, dk2_p, dv_p, dg), (gw["lnx_w"], gw["lnx_b"], gw["bonus_scale"]) = _rowwise_bwd(
        _fn_post, post_rows, post_params, 2, [[dy_a]], name="rwkv_post_bwd", tm=128, row_grad=[BF16] * 5)
    dr_s, dlw, dk2_s, dv_s, dkk, da = _wkv_bwd(z_rkv, lw, k2, kk, a, states, invs, dy, seq)
    (dzk, dzw, dza, dzg), pg = _rowwise_bwd(
        _fn_prep, prep_rows, prep_params, 2, [[dlw], [dk2_p, dk2_s], [dkk], [da], [dg]],
        name="rwkv_prep_bwd", tm=128, row_grad=[BF16] * 4)
    gw["decay_base"], gd_up, gw["iclr_base"], gi_up, gg_up, gw["key_norm_scale"], gw["key_iclr_scale"] = pg
    gw["decay_up"], gw["iclr_up"], gw["gate_up"] = gd_up[:LORA_W], gi_up[:LORA_A], gg_up[:LORA_G]
    dp_r, gmix_r = _shift_bwd(proj, 0, D, mix_rkv[:, :D], [dr_p, dr_s], seq, name="shift_r_bwd")
    dp_k, gmix_k = _shift_bwd(proj, D, D, mix_rkv[:, D:2 * D], [dzk], seq, name="shift_k_bwd")
    dp_v, gmix_v = _shift_bwd(proj, 2 * D, D, mix_rkv[:, 2 * D:], [dv_p, dv_s], seq, name="shift_v_bwd")
    dp_lora, gmix_lora = _shift_bwd(proj, C_LORA, 512, mix_lora, [jnp.concatenate([dzw, dza, dzg], axis=1)], seq,
                                    name="shift_lora_bwd")
    gw["shift_mix"] = jnp.concatenate([gmix_r, gmix_k, gmix_v, _unpad_lora(gmix_lora)], axis=1)
    dproj = [dp_r, dp_k, dp_v, dq, dk, dv, dzga, dzgb, dp_lora]
    gw["w_in_p"] = _mm_cat_tn(dproj, h1, name="mm_in_dw", after=gw["rel_bias"])
    token = comm.send_late(gw)
    dh1 = _mm_cat_nn(dproj, w_in, name="mm_in_dx", after=token)
    (grad_x,), (gw["g_pre_mix"],) = _rowwise_bwd(_fn_pre, [_win(x)], [g_pre_mix], 0, [[dh1]], name="pre_mix_bwd",
                                                 tm=256, row_grad=[F32], add_to={0: dx0})
    return loss, grad_x, gw


_COL_SHARDED = ("w_in", "decay_up", "iclr_up", "gate_up", "w_o_mem", "w_ffn_in")
_ROW_SHARDED = ("w_branch_a", "w_branch_b", "w_out", "w_q_mem", "w_kv_mem", "w_ffn_out")
_TRANSPOSED = ("w_in", "w_ffn_in")
_FIRST = ("w_in", "decay_up", "iclr_up", "gate_up")
_REST = ("w_o_mem", "w_ffn_in", "w_branch_a", "w_branch_b", "w_out", "w_q_mem", "w_kv_mem", "w_ffn_out")
_REPLICATED = ("g_pre_mix", "g_post_mix", "shift_mix", "decay_base", "iclr_base", "key_norm_scale", "key_iclr_scale",
               "bonus_scale", "lnx_w", "lnx_b", "rel_bias", "g_pre_cross", "g_post_cross", "g_mem", "g_pre_ffn",
               "g_post_ffn")
_WEIGHTS = ("g_pre_mix", "g_post_mix", "w_in", "shift_mix", "decay_base", "decay_up", "iclr_base", "iclr_up", "gate_up",
            "key_norm_scale", "key_iclr_scale", "bonus_scale", "lnx_w", "lnx_b", "rel_bias", "w_branch_a", "w_branch_b",
            "w_out", "g_pre_cross", "g_post_cross", "g_mem", "w_q_mem", "w_kv_mem", "w_o_mem", "g_pre_ffn", "g_post_ffn",
            "w_ffn_in", "w_ffn_out")
_PACK_ROWS = 8 * ((sum({"shift_mix": 3360, "bonus_scale": 1024, "rel_bias": 3072}.get(n, D) for n in _REPLICATED)
                   + 1 + 8 * LANE - 1) // (8 * LANE))


def _pack(vals):
    flat = jnp.concatenate([v.reshape(-1).astype(F32) for v in vals])
    return jnp.pad(flat, (0, _PACK_ROWS * LANE - flat.shape[0])).reshape(_PACK_ROWS, LANE)


def _unpack(packed, shapes):
    flat, out, pos = packed.reshape(-1), [], 0
    for s in shapes:
        n = math.prod(s)
        out.append(flat[pos:pos + n].reshape(s))
        pos += n
    return out


def _step(args, seq, n_mem):
    names = ("x", "mem") + _WEIGHTS + ("loss_target",) + tuple("m_" + n for n in _WEIGHTS) + tuple("v_" + n for n in _WEIGHTS)
    given = dict(zip(names, args))
    nb = given["x"].shape[0]
    x = given["x"].reshape(nb * seq, D)
    mem = given["mem"].reshape(nb * n_mem, D)
    target = given["loss_target"].reshape(nb * seq, D)
    def local(name, prefix=""):
        a = given[prefix + name][0]
        return a.T if name in _TRANSPOSED else a

    shard = {n: local(n) for n in _COL_SHARDED + _ROW_SHARDED}
    stacked = _ROW_SHARDED + _TRANSPOSED
    out = {}

    def wire(name):
        return shard[name].astype(BF16)

    def full(name, g):
        return g.reshape(-1, g.shape[-1]) if name in stacked else _cols_to_full(g)

    def blocks_of(name, g):
        return (g.reshape((N_DEV,) + shard[name].shape) if name in stacked else _full_to_cols(g)).astype(BF16)

    def update(names, landed, after=None):
        done = []
        for n, parts in zip(names, landed):
            res = _adamw(parts, shard[n], local(n, "m_"), local(n, "v_"), name="adamw_" + n, after=after)
            for kind, r in zip(("grad_", "delta_", "new_m_", "new_v_"), res):
                out[kind + n] = (r.T if n in _TRANSPOSED else r)[None]
            done.append(res[0])
        return done


    class Exchanges:
        def __init__(self):
            srcs = [wire(n) for n in _FIRST]
            self.first, self.begun = _exchange_start(srcs, False, srcs[0], name="gather_first_start",
                                                     dists=_SIBLING_AND_SAME_CORE)

        def first_weights(self, after):
            got = _exchange_wait(self.first, after, [wire(n) for n in _FIRST], name="gather_first_wait")
            relayed = _relay_to_sibling(got, name="gather_first_relay")
            pos = _mesh_pos()
            for j, d in enumerate(_OTHER_CHIPS):
                slot = _flat(_peer(pos, d | 1))
                got = [lax.dynamic_update_slice_in_dim(g, r[j][None], slot, 0) for g, r in zip(got, relayed)]
            self.rest, self.first_token = _exchange_start(
                [wire(n) for n in _REST], False, got[0], name="gather_rest_start")
            first = {n: full(n, g) for n, g in zip(_FIRST, got)}
            first["w_in_p"] = _permute_in(first.pop("w_in"), 0)
            return first

        def late_weights(self, after):
            got = _exchange_wait(self.rest, [after], [wire(n) for n in _REST], name="gather_rest_wait")
            return {n: full(n, g) for n, g in zip(_REST, got)}

        def send_early(self, gw):
            self.early_blocks = [blocks_of(n, gw[n]) for n in _REST]
            self.early, token = _exchange_start(self.early_blocks, True, self.early_blocks[-1], name="scatter_rest_start")
            return token

        def send_late(self, gw):
            me = _flat(_mesh_pos())
            own = [lax.dynamic_index_in_dim(b, me, 0, keepdims=False) for b in self.early_blocks]
            landed = _exchange_wait(self.early, [gw["w_in_p"]], own, name="scatter_rest_wait")
            grads = {**gw, "w_in": _unpermute_in(gw["w_in_p"], 0)}
            self.late_blocks = [blocks_of(n, grads[n]) for n in _FIRST]
            self.late, token = _exchange_start(self.late_blocks, True, landed[0], name="scatter_first_start")
            self.updated = update(_REST, landed, after=token)
            return token

        def finish(self, after):
            me = _flat(_mesh_pos())
            own = [lax.dynamic_index_in_dim(b, me, 0, keepdims=False) for b in self.late_blocks]
            update(_FIRST, _exchange_wait(self.late, [*after, *self.updated], own, name="scatter_first_wait"))

    comm = Exchanges()
    wt = {n: given[n][0] for n in _REPLICATED}
    loss_tile, grad_x, gw = _local_step(x, mem, target, wt, seq, n_mem, comm)
    rep_shapes = [given[n].shape for n in _REPLICATED]
    packed, _ = lax.optimization_barrier((_pack([gw[n] for n in _REPLICATED] + [loss_tile[0, 0]]), tuple(comm.updated)))
    small = _exchange([packed], False, name="gather_small")[0]
    zero = jnp.zeros((), F32)
    res = _adamw(small, *[_pack([given[p + n] for n in _REPLICATED] + [zero]) for p in ("", "m_", "v_")],
                 name="adamw_small", tr=_PACK_ROWS)
    for kind, r in zip(("grad_", "delta_", "new_m_", "new_v_"), res):
        for n, val in zip(_REPLICATED, _unpack(r, rep_shapes)):
            out[kind + n] = val
    loss = res[0].reshape(-1)[sum(math.prod(s) for s in rep_shapes)]
    comm.finish([grad_x, res[0]])
    grad_x = grad_x.reshape(nb, seq, D)
    return (loss, grad_x, *[out[k + n] for k in ("grad_", "delta_", "new_m_", "new_v_") for n in _WEIGHTS])


def kernel(x, mem, g_pre_mix, g_post_mix, w_in, shift_mix, decay_base, decay_up, iclr_base, iclr_up, gate_up, key_norm_scale, key_iclr_scale, bonus_scale, lnx_w, lnx_b, rel_bias, w_branch_a, w_branch_b, w_out, g_pre_cross, g_post_cross, g_mem, w_q_mem, w_kv_mem, w_o_mem, g_pre_ffn, g_post_ffn, w_ffn_in, w_ffn_out, loss_target, m_g_pre_mix, m_g_post_mix, m_w_in, m_shift_mix, m_decay_base, m_decay_up, m_iclr_base, m_iclr_up, m_gate_up, m_key_norm_scale, m_key_iclr_scale, m_bonus_scale, m_lnx_w, m_lnx_b, m_rel_bias, m_w_branch_a, m_w_branch_b, m_w_out, m_g_pre_cross, m_g_post_cross, m_g_mem, m_w_q_mem, m_w_kv_mem, m_w_o_mem, m_g_pre_ffn, m_g_post_ffn, m_w_ffn_in, m_w_ffn_out, v_g_pre_mix, v_g_post_mix, v_w_in, v_shift_mix, v_decay_base, v_decay_up, v_iclr_base, v_iclr_up, v_gate_up, v_key_norm_scale, v_key_iclr_scale, v_bonus_scale, v_lnx_w, v_lnx_b, v_rel_bias, v_w_branch_a, v_w_branch_b, v_w_out, v_g_pre_cross, v_g_post_cross, v_g_mem, v_w_q_mem, v_w_kv_mem, v_w_o_mem, v_g_pre_ffn, v_g_post_ffn, v_w_ffn_in, v_w_ffn_out):
    args = (x, mem, g_pre_mix, g_post_mix, w_in, shift_mix, decay_base, decay_up, iclr_base, iclr_up, gate_up, key_norm_scale, key_iclr_scale, bonus_scale, lnx_w, lnx_b, rel_bias, w_branch_a, w_branch_b, w_out, g_pre_cross, g_post_cross, g_mem, w_q_mem, w_kv_mem, w_o_mem, g_pre_ffn, g_post_ffn, w_ffn_in, w_ffn_out, loss_target, m_g_pre_mix, m_g_post_mix, m_w_in, m_shift_mix, m_decay_base, m_decay_up, m_iclr_base, m_iclr_up, m_gate_up, m_key_norm_scale, m_key_iclr_scale, m_bonus_scale, m_lnx_w, m_lnx_b, m_rel_bias, m_w_branch_a, m_w_branch_b, m_w_out, m_g_pre_cross, m_g_post_cross, m_g_mem, m_w_q_mem, m_w_kv_mem, m_w_o_mem, m_g_pre_ffn, m_g_post_ffn, m_w_ffn_in, m_w_ffn_out, v_g_pre_mix, v_g_post_mix, v_w_in, v_shift_mix, v_decay_base, v_decay_up, v_iclr_base, v_iclr_up, v_gate_up, v_key_norm_scale, v_key_iclr_scale, v_bonus_scale, v_lnx_w, v_lnx_b, v_rel_bias, v_w_branch_a, v_w_branch_b, v_w_out, v_g_pre_cross, v_g_post_cross, v_g_mem, v_w_q_mem, v_w_kv_mem, v_w_o_mem, v_g_pre_ffn, v_g_post_ffn, v_w_ffn_in, v_w_ffn_out)
    return _step(args, x.shape[1], mem.shape[1])
```

```python
import functools
import math

import jax
import jax.numpy as jnp
from jax import lax
from jax.experimental import pallas as pl
from jax.experimental.pallas import tpu as pltpu

F32 = jnp.float32
BF16 = jnp.bfloat16

N_DEV = 8
D = 1024
HEAD = 64
N_HEADS = D // HEAD
LANE = 128
CHUNK = 64
LEFT = 8 * CHUNK
BAND = LEFT + CHUNK
REL_CLIP = 128
REL_TABLE = CHUNK + REL_CLIP
MEM_WIDTH = D // 2
MEM_HEADS = 4
FFN = 2816
LORA_W, LORA_A, LORA_G = 64, 64, 160
P_WIDTH = 3 * D + 3 * D + 2 * D + 128 + 128 + 256
C_Q, C_GA, C_LORA = 3 * D, 6 * D, 8 * D
NORM_EPS = 1e-6
GROUP_NORM_EPS = 64e-5
MASK_VALUE = -1e30
ADAM_LR, ADAM_B1, ADAM_B2, ADAM_EPS, ADAM_WD, ADAM_STEP = 0.001, 0.9, 0.999, 1e-08, 0.01, 10
VMEM_LIMIT = 56 * 1024 * 1024


def _cp(*sem):
    return pltpu.CompilerParams(dimension_semantics=sem, vmem_limit_bytes=VMEM_LIMIT)


_NN, _NT, _TN = ((1,), (0,)), ((1,), (1,)), ((0,), (0,))


def _dot_raw(a, b, dims):
    return lax.dot_general(a.astype(BF16), b.astype(BF16), (dims, ((), ())), preferred_element_type=F32)


@functools.partial(jax.custom_vjp, nondiff_argnums=(2,))
def _dot_dims(a, b, dims):
    return _dot_raw(a, b, dims)


def _dot_dims_fwd(a, b, dims):
    return _dot_raw(a, b, dims), (a, b)


def _dot_dims_bwd(dims, res, g):
    a, b = res
    if dims == _NN:
        da, db = _dot_raw(g, b, _NT), _dot_raw(a, g, _TN)
    elif dims == _NT:
        da, db = _dot_raw(g, b, _NN), _dot_raw(g, a, _TN)
    else:
        da, db = _dot_raw(b, g, _NT), _dot_raw(a, g, _NN)
    return da.astype(a.dtype), db.astype(b.dtype)


_dot_dims.defvjp(_dot_dims_fwd, _dot_dims_bwd)


def _dot(a, b, dims=_NN):
    return _dot_dims(a, b, dims)


def _dot_nt(a, b):
    return _dot_dims(a, b, _NT)


def _dot_tn(a, b):
    return _dot_dims(a, b, _TN)


def _split(x, terms):
    parts, rest = [], x.astype(F32)
    for _ in range(terms):
        p = rest.astype(BF16)
        parts.append(p)
        rest = rest - p.astype(F32)
    return parts


def _dot_split_a(a, b, terms=2):
    out = None
    for p in _split(a, terms):
        t = _dot(p, b)
        out = t if out is None else out + t
    return out


def _dot_split_b(a, b, terms=3):
    out = None
    for p in _split(b, terms):
        t = _dot(a, p)
        out = t if out is None else out + t
    return out


MM_VMEM_BUDGET = 30 * 1024 * 1024
MM_HBM_BPS = 3.2e12
MM_MXU_FPS = 8.5e14
MM_STEP_S = 0.35e-6


def _divisors(n, align, cap):
    out = [d for d in range(align, min(n, cap) + 1, align) if n % d == 0]
    return out or [n]


def _mm_tiles(m, n, k, ea, eb, eo, ta):
    best = None
    for tm in _divisors(m, LANE if ta else 8, 2048):
        for tn in _divisors(n, LANE, 2048):
            for tk in _divisors(k, LANE, 2048):
                nk = k // tk
                vmem = 2 * (tm * tk * ea + tk * tn * eb + tm * tn * eo) + (tm * tn * 4 if nk > 1 else 0)
                if vmem > MM_VMEM_BUDGET:
                    continue
                dma = (tm * tk * ea if (nk > 1 or n // tn == 1) else tm * tk * ea * tn / n) + tk * tn * eb + tm * tn * eo / nk
                step = max(2.0 * tm * tn * tk / MM_MXU_FPS, dma / MM_HBM_BPS) + MM_STEP_S
                cost = (m // tm) * (n // tn) * nk * step
                if best is None or cost < best[0]:
                    best = (cost, tm, tn, tk)
    return best[1:]


def _mm(a, b, *, name, ta=False, tb=False, out_dtype=F32, tm=None, tn=None, tk=None, split_a=1, after=None):
    m, k = (a.shape[1], a.shape[0]) if ta else a.shape
    n, kb = (b.shape[0], b.shape[1]) if tb else (b.shape[1], b.shape[0])
    assert k == kb, (a.shape, b.shape, ta, tb)
    if tm is None:
        tm, tn, tk = _mm_tiles(m, n, k, a.dtype.itemsize, b.dtype.itemsize, jnp.dtype(out_dtype).itemsize, ta)
    assert m % tm == 0 and n % tn == 0 and k % tk == 0, (m, n, k, tm, tn, tk)
    nk = k // tk
    dims = ((0 if ta else 1,), (1 if tb else 0,))

    n_after = 0 if after is None else 1

    def body(a_ref, b_ref, *rest):
        o_ref, scratch = rest[n_after], rest[n_after + 1:]
        prod = None
        for p in _split(a_ref[...], split_a) if split_a > 1 else [a_ref[...]]:
            t = _dot_raw(p, b_ref[...], dims)
            prod = t if prod is None else prod + t
        if nk == 1:
            o_ref[...] = prod.astype(o_ref.dtype)
            return
        acc_ref, kk = scratch[0], pl.program_id(2)

        @pl.when(kk == 0)
        def _():
            acc_ref[...] = prod

        @pl.when(kk > 0)
        def _():
            acc_ref[...] += prod

        @pl.when(kk == nk - 1)
        def _():
            o_ref[...] = acc_ref[...].astype(o_ref.dtype)

    a_spec = pl.BlockSpec((tk, tm), lambda i, j, q: (q, i)) if ta else pl.BlockSpec((tm, tk), lambda i, j, q: (i, q))
    b_spec = pl.BlockSpec((tn, tk), lambda i, j, q: (j, q)) if tb else pl.BlockSpec((tk, tn), lambda i, j, q: (q, j))
    return pl.pallas_call(
        body, name=name, grid=(m // tm, n // tn, nk),
        in_specs=[a_spec, b_spec] + [pl.BlockSpec(memory_space=pl.ANY)] * n_after,
        out_specs=pl.BlockSpec((tm, tn), lambda i, j, q: (i, j)),
        out_shape=jax.ShapeDtypeStruct((m, n), out_dtype),
        scratch_shapes=[pltpu.VMEM((tm, tn), F32)] if nk > 1 else [],
        compiler_params=_cp("parallel", "parallel", "arbitrary"),
    )(a, b, *([] if after is None else [after]))


def _piece_steps(pieces, tile):
    counts = [p.shape[1] // tile for p in pieces]
    assert all(p.shape[1] % tile == 0 for p in pieces)
    return [(sum(counts[:i]), c) for i, c in enumerate(counts)], sum(counts)


def _mm_cat_nn(pieces, w, *, name, after=None, tm=2048, tk=256):
    t, n = pieces[0].shape[0], w.shape[1]
    tm = min(tm, t)
    spans, nk = _piece_steps(pieces, tk)
    npc = len(pieces)
    n_after = 0 if after is None else 1

    def body(*refs):
        w_ref, o_ref, acc_ref = refs[npc], refs[npc + 1 + n_after], refs[npc + 2 + n_after]
        q = pl.program_id(1)

        @pl.when(q == 0)
        def _():
            acc_ref[...] = jnp.zeros_like(acc_ref)

        for p_ref, (first, count) in zip(refs[:npc], spans):
            @pl.when(jnp.logical_and(q >= first, q < first + count))
            def _(p_ref=p_ref):
                acc_ref[...] += _dot_raw(p_ref[...], w_ref[...], _NN)

        @pl.when(q == nk - 1)
        def _():
            o_ref[...] = acc_ref[...].astype(o_ref.dtype)

    def piece_spec(first, count):
        return pl.BlockSpec((tm, tk), lambda i, q: (i, jnp.clip(q - first, 0, count - 1)))

    return pl.pallas_call(
        body, name=name, grid=(t // tm, nk),
        in_specs=[piece_spec(*s) for s in spans] + [pl.BlockSpec((tk, n), lambda i, q: (q, 0))]
        + [pl.BlockSpec(memory_space=pl.ANY)] * n_after,
        out_specs=pl.BlockSpec((tm, n), lambda i, q: (i, 0)),
        out_shape=jax.ShapeDtypeStruct((t, n), BF16),
        scratch_shapes=[pltpu.VMEM((tm, n), F32)],
        compiler_params=_cp("parallel", "arbitrary"),
    )(*pieces, w, *([] if after is None else [after]))


def _mm_cat_tn(pieces, a, *, name, after=None, tk=1024, tn=512):
    t, m = a.shape
    tk = min(tk, t)
    spans, nj = _piece_steps(pieces, tn)
    npc, nk = len(pieces), t // tk
    n_after = 0 if after is None else 1

    def body(a_ref, *refs):
        o_ref, acc_ref = refs[npc + n_after], refs[npc + 1 + n_after]
        j, q = pl.program_id(0), pl.program_id(1)

        @pl.when(q == 0)
        def _():
            acc_ref[...] = jnp.zeros_like(acc_ref)

        for p_ref, (first, count) in zip(refs[:npc], spans):
            @pl.when(jnp.logical_and(j >= first, j < first + count))
            def _(p_ref=p_ref):
                acc_ref[...] += _dot_raw(p_ref[...], a_ref[...], _TN)

        @pl.when(q == nk - 1)
        def _():
            o_ref[...] = acc_ref[...].astype(o_ref.dtype)

    def piece_spec(first, count):
        def index(j, q):
            mine = jnp.logical_and(j >= first, j < first + count)
            return jnp.where(mine, q, 0), jnp.clip(j - first, 0, count - 1)
        return pl.BlockSpec((tk, tn), index)

    return pl.pallas_call(
        body, name=name, grid=(nj, nk),
        in_specs=[pl.BlockSpec((tk, m), lambda j, q: (q, 0))] + [piece_spec(*s) for s in spans]
        + [pl.BlockSpec(memory_space=pl.ANY)] * n_after,
        out_specs=pl.BlockSpec((tn, m), lambda j, q: (j, 0)),
        out_shape=jax.ShapeDtypeStruct((nj * tn, m), BF16),
        scratch_shapes=[pltpu.VMEM((tn, m), F32)],
        compiler_params=_cp("parallel", "arbitrary"),
    )(a, *pieces, *([] if after is None else [after]))


def _win(arr, start=0, width=None):
    width = arr.shape[1] if width is None else width
    assert start % width == 0
    return (arr, start // width, width)


def _row_specs(rows, tm):
    return [pl.BlockSpec((tm, w), functools.partial(lambda i, cb: (i, cb), cb=cb)) for (_, cb, w) in rows]


def _full_spec(p):
    nd = p.ndim
    return pl.BlockSpec(p.shape, lambda i, nd=nd: (0,) * nd)


def _rowwise(fn, rows, params, outs, *, name, tm, after=None):
    t = rows[0][0].shape[0]
    tm = min(tm, t)
    assert t % tm == 0
    nr, npar = len(rows), len(params)
    n_after = 0 if after is None else 1

    def body(*refs):
        vals = [r[...] for r in refs[:nr + npar]]
        res = fn(*vals)
        for o_ref, r in zip(refs[nr + npar + n_after:], res):
            o_ref[...] = r.astype(o_ref.dtype)

    return pl.pallas_call(
        body, name=name, grid=(t // tm,),
        in_specs=_row_specs(rows, tm) + [_full_spec(p) for p in params] + [pl.BlockSpec(memory_space=pl.ANY)] * n_after,
        out_specs=[pl.BlockSpec((tm, w), lambda i: (i, 0)) for (w, _) in outs],
        out_shape=[jax.ShapeDtypeStruct((t, w), dt) for (w, dt) in outs],
        compiler_params=_cp("parallel"),
    )(*[r[0] for r in rows], *params, *([] if after is None else [after]))


def _rowwise_bwd(fn, rows, params, n_const, cots, *, name, tm, row_grad, add_to=None, packed=False):
    t = rows[0][0].shape[0]
    tm = min(tm, t)
    assert t % tm == 0
    nr, npar = len(rows), len(params)
    ndp = npar - n_const
    add_to = add_to or {}
    add_idx = sorted(add_to)
    flat_cots = [c for group in cots for c in group]
    kept = [i for i in range(nr) if row_grad[i] is not None]

    def body(*refs):
        pos = 0
        row_v = [r[...] for r in refs[pos:pos + nr]]; pos += nr
        par_v = [r[...] for r in refs[pos:pos + npar]]; pos += npar
        cot_v = [r[...] for r in refs[pos:pos + len(flat_cots)]]; pos += len(flat_cots)
        add_v = [r[...] for r in refs[pos:pos + len(add_idx)]]; pos += len(add_idx)
        if packed:
            offs = [sum(rows[i][2] for i in kept[:q]) for q in range(len(kept))]
            rg_refs = [refs[pos].at[:, o:o + rows[i][2]] for o, i in zip(offs, kept)]; pos += 1
        else:
            rg_refs = refs[pos:pos + len(kept)]; pos += len(kept)
        pg_refs = refs[pos:pos + ndp]

        consts = par_v[ndp:]
        res, vjp = jax.vjp(lambda *args: tuple(fn(*args, *consts)), *row_v, *par_v[:ndp])
        cot_in, q = [], 0
        for j, group in enumerate(cots):
            c = None
            for _ in group:
                cv = cot_v[q].astype(F32); q += 1
                c = cv if c is None else c + cv
            c = jnp.zeros(res[j].shape, F32) if c is None else c
            cot_in.append(c.astype(res[j].dtype))
        grads = vjp(tuple(cot_in))
        for ref, i in zip(rg_refs, kept):
            g = grads[i].astype(F32)
            if i in add_to:
                g = g + add_v[add_idx.index(i)].astype(F32)
            ref[...] = g.astype(ref.dtype)

        @pl.when(pl.program_id(0) == 0)
        def _():
            for ref in pg_refs:
                ref[...] = jnp.zeros_like(ref)

        for ref, g in zip(pg_refs, grads[nr:]):
            ref[...] += g.astype(F32)

    cot_specs = [pl.BlockSpec((tm, c.shape[1]), lambda i: (i, 0)) for c in flat_cots]
    add_specs = [pl.BlockSpec((tm, add_to[i].shape[1]), lambda i_: (i_, 0)) for i in add_idx]
    widths = [sum(rows[i][2] for i in kept)] if packed else [rows[i][2] for i in kept]
    n_rg = len(widths)
    out_specs = [pl.BlockSpec((tm, w), lambda i_: (i_, 0)) for w in widths] + [_full_spec(p) for p in params[:ndp]]
    out_shape = [jax.ShapeDtypeStruct((t, w), row_grad[kept[q]]) for q, w in enumerate(widths)] + [
        jax.ShapeDtypeStruct(p.shape, F32) for p in params[:ndp]]
    res = pl.pallas_call(
        body, name=name, grid=(t // tm,),
        in_specs=_row_specs(rows, tm) + [_full_spec(p) for p in params] + cot_specs + add_specs,
        out_specs=out_specs, out_shape=out_shape,
        compiler_params=_cp("arbitrary"),
    )(*[r[0] for r in rows], *params, *flat_cots, *[add_to[i] for i in add_idx])
    return list(res[:n_rg]), list(res[n_rg:])


def _rms(x, g):
    xf = x.astype(F32)
    return xf * lax.rsqrt(jnp.mean(xf * xf, axis=-1, keepdims=True) + NORM_EPS) * g


def _softplus(x):
    return jnp.maximum(x, 0.0) + jnp.log(1.0 + jnp.exp(-jnp.abs(x)))


def _fn_pre(x, g):
    return (_rms(x, g).astype(BF16),)


def _fn_res(x, u, g_post):
    return (x + _rms(u, g_post),)


def _fn_res_pre(x, u, g_post, g_pre):
    xn = x + _rms(u, g_post)
    return xn, _rms(xn, g_pre).astype(BF16)


def _fn_mix(zga, zgb, ya, yb):
    return ((jax.nn.sigmoid(zga) * ya + jax.nn.sigmoid(zgb) * yb).astype(BF16),)


def _fn_swiglu(gate, up):
    gate, up = gate.astype(F32), up.astype(F32)
    return ((gate * jax.nn.sigmoid(gate) * up).astype(BF16),)


def _fn_prep(zk, zw, za, zg, decay_base, d_up, iclr_base, i_up, g_up, kns, kis, e_hd, e_dh):
    w_log = -_softplus(-(decay_base + _dot(jnp.tanh(zw), d_up))) - 0.5
    lw = -jnp.exp(w_log)
    a = jax.nn.sigmoid(iclr_base + _dot(za, i_up))
    g = _dot(jax.nn.sigmoid(zg), g_up)
    kn = zk * kns
    ss = _dot(kn * kn, e_dh)
    inv = lax.rsqrt(jnp.maximum(ss, 1e-24))
    kk = kn * _dot_split_a(inv, e_hd)
    k2 = zk * (1.0 + (a - 1.0) * kis)
    return lw, k2, kk, a, g


def _fn_post(y, r, k2, v, g, lnx_w, lnx_b, bonus, e_hd, e_dh):
    mu = _dot_split_a(_dot(y, e_dh) * (1.0 / HEAD), e_hd)
    yc = y - mu
    var = _dot(yc * yc, e_dh) * (1.0 / HEAD)
    yn = yc * _dot_split_a(lax.rsqrt(var + GROUP_NORM_EPS), e_hd)
    bs = _dot_split_a(_dot(r * k2 * bonus, e_dh), e_hd)
    return (((yn * lnx_w + lnx_b + bs * v) * g).astype(BF16),)


def _shift_fwd(p, col0, ncols, mix, seq, *, name, cw=256):
    t = p.shape[0]
    assert col0 % cw == 0 and ncols % cw == 0 and t % seq == 0
    cb0 = col0 // cw

    def body(p_ref, m_ref, z_ref):
        pv = p_ref[...]
        row = lax.broadcasted_iota(jnp.int32, pv.shape, 0)
        prev = jnp.where(row == 0, 0.0, pltpu.roll(pv, 1, axis=0))
        z_ref[...] = pv + (prev - pv) * m_ref[...]

    return pl.pallas_call(
        body, name=name, grid=(t // seq, ncols // cw),
        in_specs=[pl.BlockSpec((seq, cw), lambda b, c: (b, c + cb0)), pl.BlockSpec((1, cw), lambda b, c: (0, c))],
        out_specs=pl.BlockSpec((seq, cw), lambda b, c: (b, c)),
        out_shape=jax.ShapeDtypeStruct((t, ncols), F32),
        compiler_params=_cp("parallel", "parallel"),
    )(p, mix)


def _shift_bwd(p, col0, ncols, mix, dz_parts, seq, *, name, cw=256):
    t = p.shape[0]
    cb0 = col0 // cw
    n = len(dz_parts)

    def body(*refs):
        p_ref, m_ref = refs[:2]
        dp_ref, dm_ref = refs[2 + n:]
        dz = refs[2][...].astype(F32)
        for r in refs[3:2 + n]:
            dz = dz + r[...].astype(F32)
        pv = p_ref[...]
        mixv = m_ref[...]
        row = lax.broadcasted_iota(jnp.int32, pv.shape, 0)
        prev = jnp.where(row == 0, 0.0, pltpu.roll(pv, 1, axis=0))
        u = dz * mixv
        nxt = jnp.where(row == seq - 1, 0.0, pltpu.roll(u, seq - 1, axis=0))
        dp_ref[...] = (dz - u + nxt).astype(dp_ref.dtype)

        @pl.when(pl.program_id(1) == 0)
        def _():
            dm_ref[...] = jnp.zeros_like(dm_ref)

        dm_ref[...] += jnp.sum(dz * (prev - pv), axis=0, keepdims=True)

    return pl.pallas_call(
        body, name=name, grid=(ncols // cw, t // seq),
        in_specs=[pl.BlockSpec((seq, cw), lambda c, b: (b, c + cb0)), pl.BlockSpec((1, cw), lambda c, b: (0, c))]
        + [pl.BlockSpec((seq, cw), lambda c, b: (b, c))] * n,
        out_specs=[pl.BlockSpec((seq, cw), lambda c, b: (b, c)), pl.BlockSpec((1, cw), lambda c, b: (0, c))],
        out_shape=[jax.ShapeDtypeStruct((t, ncols), BF16), jax.ShapeDtypeStruct((1, ncols), F32)],
        compiler_params=_cp("parallel", "arbitrary"),
    )(p, mix, *dz_parts)


def _each(f, *lists):
    return [f(*xs) for xs in zip(*lists)]


def _tri_inv(low):
    c = low[0].shape[0]
    ti = lax.broadcasted_iota(jnp.int32, (c, c), 0)
    si = lax.broadcasted_iota(jnp.int32, (c, c), 1)
    eye = (ti == si).astype(F32)
    inside = (ti // 4) == (si // 4)
    base = [jnp.where(inside, m, 0.0) for m in low]
    acc = _each(lambda m: _dot(eye - m, eye + _dot(m, m)), base)
    size = 8
    while size <= c:
        wider = (ti // size) == (si // size)
        keep = jnp.logical_and(wider, jnp.logical_not(inside))
        acc = _each(lambda p, m: p - _dot(_dot(p, jnp.where(keep, m, 0.0)), p), acc, low)
        inside, size = wider, size * 2
    return acc


def _stack_rows(a, b):
    return jnp.concatenate([a, b], axis=0)


@jax.custom_vjp
def _split_rows(x):
    h = x.shape[0] // 2
    return x[:h], x[h:]


def _split_rows_fwd(x):
    return _split_rows(x), None


def _split_rows_bwd(_, g):
    return (jnp.concatenate(g, axis=0),)


_split_rows.defvjp(_split_rows_fwd, _split_rows_bwd)


def _masked_halves(stacked, top_mask, bottom_mask):
    halves = _each(_split_rows, stacked)
    return ([jnp.where(top_mask, t, 0.0) for t, _ in halves], [jnp.where(bottom_mask, b, 0.0) for _, b in halves])


@jax.custom_vjp
def _tri_inv_known(low, inv):
    return inv


def _tri_inv_known_fwd(low, inv):
    return inv, inv


def _tri_inv_known_bwd(inv, g):
    dlow = _each(lambda t, gg: -_dot(_dot(t, gg, _TN), t, _NT), inv, g)
    return dlow, _each(jnp.zeros_like, inv)


_tri_inv_known.defvjp(_tri_inv_known_fwd, _tri_inv_known_bwd)


def _wkv_chunk(s0, r, lw, k, v, kk, a, inv=None):
    c = r[0].shape[0]
    ti = lax.broadcasted_iota(jnp.int32, (c, c), 0)
    si = lax.broadcasted_iota(jnp.int32, (c, c), 1)
    incl, strict = ti >= si, ti > si
    tri = incl.astype(F32)
    cum = _each(lambda x: _dot_split_b(tri, x, 3), lw)
    eg = _each(jnp.exp, cum)
    egp = _each(lambda cs, x: jnp.exp(cs - x), cum, lw)
    ei = _each(lambda cs: jnp.exp(-cs), cum)
    rh, kkh, kt = _each(jnp.multiply, r, eg), _each(jnp.multiply, kk, egp), _each(jnp.multiply, k, ei)
    bt = _each(lambda p, q, e: (p * q) * e, a, kk, ei)
    both = _each(_stack_rows, kkh, rh)
    on_b, on_k, on_s = _each(_dot_nt, both, bt), _each(_dot_nt, both, kt), _each(_dot_nt, both, s0)
    lb, mb = _masked_halves(on_b, strict, incl)
    lk, mk = _masked_halves(on_k, strict, incl)
    on_s = _each(_split_rows, on_s)
    on_v = _each(lambda p, q, x: _split_rows(_dot(_stack_rows(p, q), x)), lk, mk, v)
    rhs = _each(lambda p, q: p[0] + q[0], on_s, on_v)
    inv = _tri_inv(lb) if inv is None else _tri_inv_known(lb, inv)
    u = _each(lambda t, x: -_dot(t, x), inv, rhs)
    y = _each(lambda p, m1, uu, q: p[1] + _dot(m1, uu) + q[1], on_s, mb, u, on_v)
    s1 = _each(lambda s, uu, x, b, kq, w: (s + _dot_tn(_stack_rows(uu, x), _stack_rows(b, kq)))
               * jnp.exp(jnp.sum(w, axis=0, keepdims=True)), s0, u, v, bt, kt, lw)
    return y, s1, inv


WKV_HEADS = 16
WKV_COLS = WKV_HEADS * HEAD
WKV_GROUPS = N_HEADS // WKV_HEADS


def _head_cols(ref):
    return [ref[:, h * HEAD:(h + 1) * HEAD] for h in range(ref.shape[1] // HEAD)]


def _wkv_specs(seq, rev):
    nc = seq // CHUNK

    def rows(col0):
        cb0 = col0 // WKV_COLS
        if rev:
            return pl.BlockSpec((CHUNK, WKV_COLS), lambda b, h, c: (b * nc + nc - 1 - c, cb0 + h))
        return pl.BlockSpec((CHUNK, WKV_COLS), lambda b, h, c: (b * nc + c, cb0 + h))

    if rev:
        st = pl.BlockSpec((1, 1, WKV_HEADS, HEAD, HEAD), lambda b, h, c: (b * WKV_GROUPS + h, nc - 1 - c, 0, 0, 0))
    else:
        st = pl.BlockSpec((1, 1, WKV_HEADS, HEAD, HEAD), lambda b, h, c: (b * WKV_GROUPS + h, c, 0, 0, 0))
    return rows, st


def _wkv_fwd(z_rkv, lw, k2, kk, a, seq):
    t = z_rkv.shape[0]
    nb, nc = t // seq, seq // CHUNK
    rows, st = _wkv_specs(seq, False)

    def body(r_ref, v_ref, lw_ref, k_ref, kk_ref, a_ref, y_ref, st_ref, inv_ref, s_scr):
        @pl.when(pl.program_id(2) == 0)
        def _():
            s_scr[...] = jnp.zeros_like(s_scr)

        s0 = [s_scr[h] for h in range(WKV_HEADS)]
        y, s1, inv = _wkv_chunk(s0, *[_head_cols(ref) for ref in (r_ref, lw_ref, k_ref, v_ref, kk_ref, a_ref)])
        for h in range(WKV_HEADS):
            st_ref[0, 0, h] = s0[h]
            inv_ref[0, 0, h] = inv[h]
            y_ref[:, h * HEAD:(h + 1) * HEAD] = y[h]
            s_scr[h] = s1[h]

    per_chunk = jax.ShapeDtypeStruct((nb * WKV_GROUPS, nc, WKV_HEADS, HEAD, HEAD), F32)
    return pl.pallas_call(
        body, name="wkv_fwd", grid=(nb, WKV_GROUPS, nc),
        in_specs=[rows(0), rows(2 * D), rows(0), rows(0), rows(0), rows(0)],
        out_specs=[rows(0), st, st],
        out_shape=[jax.ShapeDtypeStruct((t, D), F32), per_chunk, per_chunk],
        scratch_shapes=[pltpu.VMEM((WKV_HEADS, HEAD, HEAD), F32)],
        compiler_params=_cp("parallel", "parallel", "arbitrary"),
    )(z_rkv, z_rkv, lw, k2, kk, a)


def _wkv_bwd(z_rkv, lw, k2, kk, a, states, invs, dy, seq):
    t = z_rkv.shape[0]
    nb, nc = t // seq, seq // CHUNK
    rows, st = _wkv_specs(seq, True)

    def body(r_ref, v_ref, lw_ref, k_ref, kk_ref, a_ref, st_ref, inv_ref, dy_ref,
             dr_ref, dlw_ref, dk_ref, dv_ref, dkk_ref, da_ref, ds_scr):
        @pl.when(pl.program_id(2) == 0)
        def _():
            ds_scr[...] = jnp.zeros_like(ds_scr)

        s0 = [st_ref[0, 0, h] for h in range(WKV_HEADS)]
        inv = [inv_ref[0, 0, h] for h in range(WKV_HEADS)]
        _, vjp = jax.vjp(lambda *args: _wkv_chunk(*args, inv=inv)[:2],
                         s0, *[_head_cols(ref) for ref in (r_ref, lw_ref, k_ref, v_ref, kk_ref, a_ref)])
        grads = vjp(([x.astype(F32) for x in _head_cols(dy_ref)], [ds_scr[h] for h in range(WKV_HEADS)]))
        for h in range(WKV_HEADS):
            ds_scr[h] = grads[0][h]
            for ref, g in zip((dr_ref, dlw_ref, dk_ref, dv_ref, dkk_ref, da_ref), grads[1:]):
                ref[:, h * HEAD:(h + 1) * HEAD] = g[h].astype(ref.dtype)

    return pl.pallas_call(
        body, name="wkv_bwd", grid=(nb, WKV_GROUPS, nc),
        in_specs=[rows(0), rows(2 * D), rows(0), rows(0), rows(0), rows(0), st, st, rows(0)],
        out_specs=[rows(0)] * 6,
        out_shape=[jax.ShapeDtypeStruct((t, D), BF16)] * 6,
        scratch_shapes=[pltpu.VMEM((WKV_HEADS, HEAD, HEAD), F32)],
        compiler_params=_cp("parallel", "parallel", "arbitrary"),
    )(z_rkv, z_rkv, lw, k2, kk, a, states, invs, dy)


def _softmax(s):
    e = jnp.exp(s - jnp.max(s, axis=-1, keepdims=True))
    return e * (1.0 / jnp.sum(e, axis=-1, keepdims=True))


ATT_HEADS = 8
ATT_COLS = ATT_HEADS * HEAD
ATT_GROUPS = N_HEADS // ATT_HEADS


def _attn_chunk(q, kb, vb, bias, valid):
    s = _each(lambda x, y, z: jnp.where(valid, _dot_nt(x * (HEAD ** -0.5), y) + z, MASK_VALUE), q, kb, bias)
    return _each(_dot, _each(_softmax, s), vb)


def _pad_fill(pad_ref, src_ref):
    pad_ref[0:LEFT, :] = jnp.zeros((LEFT, pad_ref.shape[1]), pad_ref.dtype)
    pad_ref[LEFT:, :] = src_ref[...].astype(pad_ref.dtype)


def _band_heads(pad_ref, start):
    return [pad_ref[pl.ds(start, BAND), h * HEAD:(h + 1) * HEAD].astype(F32) for h in range(ATT_HEADS)]


def _band_valid(c):
    return (c * CHUNK - LEFT + lax.broadcasted_iota(jnp.int32, (1, BAND), 1)) >= 0


def _bias_spec():
    return pl.BlockSpec((ATT_HEADS, CHUNK, BAND), lambda h, b, c: (h, 0, 0))


def _attn_fwd(proj, bias, seq):
    t = proj.shape[0]
    nb, nc = t // seq, seq // CHUNK
    cq = C_Q // ATT_COLS

    def body(q_ref, k_ref, v_ref, b_ref, o_ref, kpad, vpad):
        c = pl.program_id(2)

        @pl.when(c == 0)
        def _():
            _pad_fill(kpad, k_ref)
            _pad_fill(vpad, v_ref)

        start = pl.multiple_of(c * CHUNK, CHUNK)
        o = _attn_chunk(_head_cols(q_ref), _band_heads(kpad, start), _band_heads(vpad, start),
                        [b_ref[h] for h in range(ATT_HEADS)], _band_valid(c))
        for h in range(ATT_HEADS):
            o_ref[:, h * HEAD:(h + 1) * HEAD] = o[h].astype(o_ref.dtype)

    return pl.pallas_call(
        body, name="attn_fwd", grid=(ATT_GROUPS, nb, nc),
        in_specs=[pl.BlockSpec((CHUNK, ATT_COLS), lambda h, b, c: (b * nc + c, cq + h)),
                  pl.BlockSpec((seq, ATT_COLS), lambda h, b, c: (b, cq + ATT_GROUPS + h)),
                  pl.BlockSpec((seq, ATT_COLS), lambda h, b, c: (b, cq + 2 * ATT_GROUPS + h)),
                  _bias_spec()],
        out_specs=pl.BlockSpec((CHUNK, ATT_COLS), lambda h, b, c: (b * nc + c, h)),
        out_shape=jax.ShapeDtypeStruct((t, D), BF16),
        scratch_shapes=[pltpu.VMEM((seq + LEFT, ATT_COLS), BF16)] * 2,
        compiler_params=_cp("parallel", "arbitrary", "arbitrary"),
    )(proj, proj, proj, bias)


def _attn_bwd(proj, bias, do, seq):
    t = proj.shape[0]
    nb, nc = t // seq, seq // CHUNK
    cq = C_Q // ATT_COLS

    def body(q_ref, k_ref, v_ref, b_ref, do_ref, dq_ref, dk_ref, dv_ref, db_ref, kpad, vpad, dkpad, dvpad):
        b, c = pl.program_id(1), pl.program_id(2)

        @pl.when(c == 0)
        def _():
            _pad_fill(kpad, k_ref)
            _pad_fill(vpad, v_ref)
            dkpad[...] = jnp.zeros_like(dkpad)
            dvpad[...] = jnp.zeros_like(dvpad)

        @pl.when(jnp.logical_and(b == 0, c == 0))
        def _():
            db_ref[...] = jnp.zeros_like(db_ref)

        start = pl.multiple_of(c * CHUNK, CHUNK)
        _, vjp = jax.vjp(functools.partial(_attn_chunk, valid=_band_valid(c)),
                         _head_cols(q_ref), _band_heads(kpad, start), _band_heads(vpad, start),
                         [b_ref[h] for h in range(ATT_HEADS)])
        dq, dkb, dvb, dbias = vjp([x.astype(F32) for x in _head_cols(do_ref)])
        for h in range(ATT_HEADS):
            sl = slice(h * HEAD, (h + 1) * HEAD)
            dq_ref[:, sl] = dq[h].astype(dq_ref.dtype)
            dkpad[pl.ds(start, BAND), sl] += dkb[h].astype(F32)
            dvpad[pl.ds(start, BAND), sl] += dvb[h].astype(F32)
            db_ref[h] += dbias[h]

        @pl.when(c == nc - 1)
        def _():
            dk_ref[...] = dkpad[LEFT:, :].astype(dk_ref.dtype)
            dv_ref[...] = dvpad[LEFT:, :].astype(dv_ref.dtype)

    kv_out = pl.BlockSpec((seq, ATT_COLS), lambda h, b, c: (b, h))
    return pl.pallas_call(
        body, name="attn_bwd", grid=(ATT_GROUPS, nb, nc),
        in_specs=[pl.BlockSpec((CHUNK, ATT_COLS), lambda h, b, c: (b * nc + c, cq + h)),
                  pl.BlockSpec((seq, ATT_COLS), lambda h, b, c: (b, cq + ATT_GROUPS + h)),
                  pl.BlockSpec((seq, ATT_COLS), lambda h, b, c: (b, cq + 2 * ATT_GROUPS + h)),
                  _bias_spec(),
                  pl.BlockSpec((CHUNK, ATT_COLS), lambda h, b, c: (b * nc + c, h))],
        out_specs=[pl.BlockSpec((CHUNK, ATT_COLS), lambda h, b, c: (b * nc + c, h)), kv_out, kv_out,
                   pl.BlockSpec((ATT_HEADS, CHUNK, BAND), lambda h, b, c: (h, 0, 0))],
        out_shape=[jax.ShapeDtypeStruct((t, D), BF16)] * 3 + [jax.ShapeDtypeStruct((N_HEADS, CHUNK, BAND), F32)],
        scratch_shapes=[pltpu.VMEM((seq + LEFT, ATT_COLS), BF16)] * 2 + [pltpu.VMEM((seq + LEFT, ATT_COLS), F32)] * 2,
        compiler_params=_cp("parallel", "arbitrary", "arbitrary"),
    )(proj, proj, proj, bias, do)


def _xattn_tile(q, k, v):
    s = _dot_nt(q, k) * ((MEM_WIDTH // MEM_HEADS) ** -0.5)
    return _dot(_softmax(s), v)


def _xattn_fwd(qm, kvm, seq, n_mem, tq=512):
    t = qm.shape[0]
    tq = min(tq, seq)
    nb, nq = t // seq, seq // tq

    def body(q_ref, k_ref, v_ref, o_ref):
        o_ref[...] = _xattn_tile(q_ref[...], k_ref[...], v_ref[...]).astype(o_ref.dtype)

    return pl.pallas_call(
        body, name="xattn_fwd", grid=(nb, MEM_HEADS, nq),
        in_specs=[pl.BlockSpec((tq, LANE), lambda b, h, i: (b * nq + i, h)),
                  pl.BlockSpec((n_mem, LANE), lambda b, h, i: (b, h)),
                  pl.BlockSpec((n_mem, LANE), lambda b, h, i: (b, MEM_HEADS + h))],
        out_specs=pl.BlockSpec((tq, LANE), lambda b, h, i: (b * nq + i, h)),
        out_shape=jax.ShapeDtypeStruct((t, MEM_WIDTH), BF16),
        compiler_params=_cp("parallel", "parallel", "parallel"),
    )(qm, kvm, kvm)


def _xattn_bwd(qm, kvm, do, seq, n_mem, tq=512):
    t = qm.shape[0]
    tq = min(tq, seq)
    nb, nq = t // seq, seq // tq

    def body(q_ref, k_ref, v_ref, do_ref, dq_ref, dkv_ref, dk_acc, dv_acc):
        i = pl.program_id(2)

        @pl.when(i == 0)
        def _():
            dk_acc[...] = jnp.zeros_like(dk_acc)
            dv_acc[...] = jnp.zeros_like(dv_acc)

        _, vjp = jax.vjp(_xattn_tile, q_ref[...], k_ref[...], v_ref[...])
        dq, dk, dv = vjp(do_ref[...].astype(F32))
        dq_ref[...] = dq.astype(dq_ref.dtype)
        dk_acc[...] += dk
        dv_acc[...] += dv

        @pl.when(i == nq - 1)
        def _():
            dkv_ref[0] = dk_acc[...].astype(dkv_ref.dtype)
            dkv_ref[1] = dv_acc[...].astype(dkv_ref.dtype)

    dq, dkv = pl.pallas_call(
        body, name="xattn_bwd", grid=(nb, MEM_HEADS, nq),
        in_specs=[pl.BlockSpec((tq, LANE), lambda b, h, i: (b * nq + i, h)),
                  pl.BlockSpec((n_mem, LANE), lambda b, h, i: (b, h)),
                  pl.BlockSpec((n_mem, LANE), lambda b, h, i: (b, MEM_HEADS + h)),
                  pl.BlockSpec((tq, LANE), lambda b, h, i: (b * nq + i, h))],
        out_specs=[pl.BlockSpec((tq, LANE), lambda b, h, i: (b * nq + i, h)),
                   pl.BlockSpec((2, n_mem, LANE), lambda b, h, i: (0, b, h))],
        out_shape=[jax.ShapeDtypeStruct((t, MEM_WIDTH), BF16), jax.ShapeDtypeStruct((2, nb * n_mem, MEM_WIDTH), BF16)],
        scratch_shapes=[pltpu.VMEM((n_mem, LANE), F32)] * 2,
        compiler_params=_cp("parallel", "parallel", "arbitrary"),
    )(qm, kvm, kvm, do)
    return dq, jnp.concatenate([dkv[0], dkv[1]], axis=1)


def _loss_head(x, u, g_post, target, tm=256):
    t, d = x.shape
    tm = min(tm, t)

    def tile_loss(xv, uv, gv, tv):
        diff = _fn_res(xv, uv, gv)[0] - tv
        return 0.5 * jnp.sum(jnp.mean(diff * diff, axis=-1, keepdims=True), axis=0, keepdims=True)

    def body(x_ref, u_ref, g_ref, t_ref, l_ref, dx_ref, du_ref, dg_ref):
        @pl.when(pl.program_id(0) == 0)
        def _():
            l_ref[...] = jnp.zeros_like(l_ref)
            dg_ref[...] = jnp.zeros_like(dg_ref)

        tv = t_ref[...]
        part, vjp = jax.vjp(lambda xv, uv, gv: tile_loss(xv, uv, gv, tv), x_ref[...], u_ref[...], g_ref[...])
        dx, du, dg = vjp(jnp.ones((1, 1), F32))
        l_ref[...] += part
        dx_ref[...] = dx
        du_ref[...] = du.astype(du_ref.dtype)
        dg_ref[...] += dg

    rows = pl.BlockSpec((tm, d), lambda i: (i, 0))
    vec = pl.BlockSpec((1, d), lambda i: (0, 0))
    return pl.pallas_call(
        body, name="loss_head", grid=(t // tm,),
        in_specs=[rows, rows, vec, rows],
        out_specs=[pl.BlockSpec((8, LANE), lambda i: (0, 0)), rows, rows, vec],
        out_shape=[jax.ShapeDtypeStruct((8, LANE), F32), jax.ShapeDtypeStruct((t, d), F32),
                   jax.ShapeDtypeStruct((t, d), BF16), jax.ShapeDtypeStruct((1, d), F32)],
        compiler_params=_cp("arbitrary"),
    )(x, u, g_post, target)


def _mesh_pos():
    return lax.axis_index("x"), lax.axis_index("y"), lax.axis_index("c")


def _peer(pos, d):
    x, y, c = pos
    return ((1 - x) if d & 4 else x, (1 - y) if d & 2 else y, (1 - c) if d & 1 else c)


def _flat(pos):
    return 4 * pos[0] + 2 * pos[1] + pos[2]


def _exchange(arrays, scatter, *, name):
    n = len(arrays)
    shapes = [a.shape[1:] if scatter else a.shape for a in arrays]

    def body(*refs):
        ins, outs = refs[:n], refs[n:2 * n]
        send, recv, loc = refs[2 * n:]
        pos = _mesh_pos()
        me = _flat(pos)
        pending = []
        for i in range(n):
            own = pltpu.make_async_copy(ins[i].at[me] if scatter else ins[i], outs[i].at[me], loc.at[i])
            own.start()
            pending.append(own)
            for d in range(1, N_DEV):
                peer = _peer(pos, d)
                src = ins[i].at[_flat(peer)] if scatter else ins[i]
                out_cp = pltpu.make_async_remote_copy(
                    src_ref=src, dst_ref=outs[i].at[me], send_sem=send.at[i, d - 1], recv_sem=recv.at[i, d - 1],
                    device_id=peer, device_id_type=pl.DeviceIdType.MESH)
                out_cp.start()
                pending.append(out_cp)
        for i in range(n):
            own = pending[i * N_DEV]
            for d in range(1, N_DEV):
                peer = _peer(pos, d)
                src = ins[i].at[_flat(peer)] if scatter else ins[i]
                pending[i * N_DEV + d].wait_send()
                pltpu.make_async_remote_copy(
                    src_ref=src, dst_ref=outs[i].at[_flat(peer)], send_sem=send.at[i, d - 1], recv_sem=recv.at[i, d - 1],
                    device_id=peer, device_id_type=pl.DeviceIdType.MESH).wait_recv()
            own.wait()

    hbm = pl.BlockSpec(memory_space=pltpu.HBM)
    return pl.pallas_call(
        body, name=name,
        in_specs=[hbm] * n, out_specs=[hbm] * n,
        out_shape=[jax.ShapeDtypeStruct((N_DEV,) + tuple(s), a.dtype) for s, a in zip(shapes, arrays)],
        scratch_shapes=[pltpu.SemaphoreType.DMA((n, N_DEV - 1)), pltpu.SemaphoreType.DMA((n, N_DEV - 1)),
                        pltpu.SemaphoreType.DMA((n,))],
    )(*arrays)


_HBM = pl.BlockSpec(memory_space=pltpu.HBM)
_SEM = pl.BlockSpec(memory_space=pltpu.SEMAPHORE)
_DATAFLOW = pltpu.SideEffectType.DATAFLOW_SIDE_EFFECTING


_ALL_PEERS = tuple(range(1, N_DEV))
_SIBLING_AND_SAME_CORE = (1, 2, 4, 6)


def _remote_copies(ins, lands, send, recv, scatter, dists):
    pos = _mesh_pos()
    me = _flat(pos)
    out = []
    for i in range(len(ins)):
        for j, d in enumerate(dists):
            peer = _peer(pos, d)
            src = ins[i].at[_flat(peer)] if scatter else ins[i]
            pair = i * len(dists) + j
            sems = dict(send_sem=send.at[pair], recv_sem=recv.at[pair], device_id=peer,
                        device_id_type=pl.DeviceIdType.MESH)
            out.append((pltpu.make_async_remote_copy(src_ref=src, dst_ref=lands[i].at[me], **sems),
                        pltpu.make_async_remote_copy(src_ref=src, dst_ref=lands[i].at[_flat(peer)], **sems)))
    return out


def _exchange_start(arrays, scatter, after, *, name, dists=_ALL_PEERS):
    n = len(arrays)
    shapes = [a.shape[1:] if scatter else a.shape for a in arrays]
    lands = [pltpu.with_memory_space_constraint(lax.empty((N_DEV,) + tuple(s), a.dtype), pltpu.HBM)
             for s, a in zip(shapes, arrays)]
    srcs = [pltpu.with_memory_space_constraint(a, pltpu.HBM) for a in arrays]

    def body(*refs):
        ins, land_refs = refs[:n], refs[n:2 * n]
        send, recv, token = refs[2 * n + 1], refs[2 * n + 2], refs[-1]
        for going, _ in _remote_copies(ins, land_refs, send, recv, scatter, dists):
            going.start()
        token[...] = jnp.zeros_like(token)

    sems = pltpu.SemaphoreType.DMA((n * len(dists),))
    res = pl.pallas_call(
        body, name=name,
        out_shape=(sems, sems, *[pltpu.HBM(a.shape, a.dtype) for a in srcs + lands], jax.ShapeDtypeStruct((8, LANE), F32)),
        in_specs=[_HBM] * (2 * n) + [pl.BlockSpec(memory_space=pl.ANY)],
        out_specs=(_SEM, _SEM, *[_HBM] * (2 * n), pl.BlockSpec(memory_space=pltpu.VMEM)),
        input_output_aliases={i: 2 + i for i in range(2 * n)},
        compiler_params=pltpu.CompilerParams(has_side_effects=_DATAFLOW),
    )(*srcs, *lands, after)
    return (n, scatter, dists, res[0], res[1], list(res[2:2 + 2 * n])), res[-1]


def _exchange_wait(handle, after, own, *, name):
    n, scatter, dists, send, recv, thru = handle

    def body(*refs):
        ins, land_refs = refs[:n], refs[n:2 * n]
        for going, coming in _remote_copies(ins, land_refs, refs[2 * n], refs[2 * n + 1], scatter, dists):
            going.wait_send()
            coming.wait_recv()

    res = pl.pallas_call(
        body, name=name,
        out_shape=tuple(pltpu.HBM(a.shape, a.dtype) for a in thru),
        in_specs=[_HBM] * (2 * n) + [_SEM, _SEM] + [pl.BlockSpec(memory_space=pl.ANY)] * len(after),
        out_specs=tuple([_HBM] * (2 * n)),
        input_output_aliases={i: i for i in range(2 * n)},
        compiler_params=pltpu.CompilerParams(has_side_effects=_DATAFLOW),
    )(*thru, send, recv, *after)
    me = _flat(_mesh_pos())
    return [lax.dynamic_update_slice_in_dim(land, o[None].astype(land.dtype), me, 0) for land, o in zip(res[n:], own)]


_OTHER_CHIPS = (2, 4, 6)


def _relay_to_sibling(gathered, *, name):
    n, k = len(gathered), len(_OTHER_CHIPS)

    def body(*refs):
        ins, outs = refs[:n], refs[n:2 * n]
        send, recv = refs[2 * n:]
        pos = _mesh_pos()
        copies = []
        for i in range(n):
            for j, d in enumerate(_OTHER_CHIPS):
                cp = pltpu.make_async_remote_copy(
                    src_ref=ins[i].at[_flat(_peer(pos, d))], dst_ref=outs[i].at[j],
                    send_sem=send.at[i * k + j], recv_sem=recv.at[i * k + j],
                    device_id=_peer(pos, 1), device_id_type=pl.DeviceIdType.MESH)
                cp.start()
                copies.append(cp)
        for cp in copies:
            cp.wait()

    return pl.pallas_call(
        body, name=name, in_specs=[_HBM] * n, out_specs=[_HBM] * n,
        out_shape=[jax.ShapeDtypeStruct((k,) + g.shape[1:], g.dtype) for g in gathered],
        scratch_shapes=[pltpu.SemaphoreType.DMA((n * k,)), pltpu.SemaphoreType.DMA((n * k,))],
    )(*gathered)


def _adamw(parts, w, m, v, *, name, tr=128, after=None):
    r, c = w.shape
    align = 8 * 4 // parts.dtype.itemsize
    row_tiles = [d for d in range(align, min(tr, r) + 1, align) if r % d == 0]
    tr, tc = (max(row_tiles), c) if row_tiles else (r, LANE)
    assert c % tc == 0
    n_after = 0 if after is None else 1

    def body(p_ref, w_ref, m_ref, v_ref, *rest):
        g_ref, d_ref, nm_ref, nv_ref = rest[n_after:]
        g = p_ref[0].astype(F32)
        for j in range(1, N_DEV):
            g = g + p_ref[j].astype(F32)
        m2 = ADAM_B1 * m_ref[...] + (1.0 - ADAM_B1) * g
        v2 = ADAM_B2 * v_ref[...] + (1.0 - ADAM_B2) * (g * g)
        m_hat = m2 / (1.0 - ADAM_B1 ** ADAM_STEP)
        v_hat = v2 / (1.0 - ADAM_B2 ** ADAM_STEP)
        g_ref[...] = g
        d_ref[...] = -ADAM_LR * (m_hat / (jnp.sqrt(v_hat) + ADAM_EPS) + ADAM_WD * w_ref[...])
        nm_ref[...] = m2
        nv_ref[...] = v2

    spec = pl.BlockSpec((tr, tc), lambda i, j: (i, j))
    return pl.pallas_call(
        body, name=name, grid=(r // tr, c // tc),
        in_specs=[pl.BlockSpec((N_DEV, tr, tc), lambda i, j: (0, i, j)), spec, spec, spec]
        + [pl.BlockSpec(memory_space=pl.ANY)] * n_after,
        out_specs=[spec] * 4, out_shape=[jax.ShapeDtypeStruct((r, c), F32)] * 4,
        compiler_params=_cp("parallel", "parallel"),
    )(parts, w, m, v, *([] if after is None else [after]))


def _cols_to_full(g):
    return jnp.transpose(g, (1, 0, 2)).reshape(g.shape[1], N_DEV * g.shape[2])


def _full_to_cols(w):
    r, c = w.shape
    return jnp.transpose(w.reshape(r, N_DEV, c // N_DEV), (1, 0, 2))


def _cut(a, lo, hi, axis):
    return lax.slice_in_dim(a, lo, hi, axis=axis)


def _pad_to(a, size, axis):
    pads = [(0, 0)] * a.ndim
    pads[axis] = (0, size - a.shape[axis])
    return jnp.pad(a, pads)


def _pad_lora(w, axis=1):
    return jnp.concatenate([
        _pad_to(_cut(w, 0, LORA_W, axis), 128, axis), _pad_to(_cut(w, LORA_W, LORA_W + LORA_A, axis), 128, axis),
        _pad_to(_cut(w, LORA_W + LORA_A, w.shape[axis], axis), 256, axis)], axis=axis)


def _unpad_lora(wp, axis=1):
    return jnp.concatenate([_cut(wp, 0, LORA_W, axis), _cut(wp, 128, 128 + LORA_A, axis),
                            _cut(wp, 256, 256 + LORA_G, axis)], axis=axis)


def _permute_in(w, axis):
    rk = 3 * D
    lo = rk + LORA_W + LORA_A + LORA_G
    return jnp.concatenate([_cut(w, 0, rk, axis), _cut(w, lo, w.shape[axis], axis), _pad_lora(_cut(w, rk, lo, axis), axis)],
                           axis=axis)


def _unpermute_in(wp, axis):
    return jnp.concatenate([_cut(wp, 0, 3 * D, axis), _unpad_lora(_cut(wp, C_LORA, P_WIDTH, axis), axis),
                            _cut(wp, 3 * D, C_LORA, axis)], axis=axis)


def _rel_index():
    dist = jnp.arange(CHUNK)[:, None] - jnp.arange(BAND)[None, :] + LEFT
    return (jnp.minimum(dist, REL_CLIP) + (CHUNK - 1)).reshape(-1)


def _local_step(x, mem, target, wt, seq, n_mem, comm):
    t = x.shape[0]
    row = lambda a: a.reshape(1, -1).astype(F32)
    g_pre_mix, g_post_mix = row(wt["g_pre_mix"]), row(wt["g_post_mix"])
    g_pre_cross, g_post_cross, g_mem = row(wt["g_pre_cross"]), row(wt["g_post_cross"]), row(wt["g_mem"])
    g_pre_ffn, g_post_ffn = row(wt["g_pre_ffn"]), row(wt["g_post_ffn"])
    mix = row(wt["shift_mix"])
    mix_rkv, mix_lora = mix[:, :3 * D], _pad_lora(mix[:, 3 * D:])
    decay_base, iclr_base = row(wt["decay_base"]), row(wt["iclr_base"])
    kns, kis = row(wt["key_norm_scale"]), row(wt["key_iclr_scale"])
    lnx_w, lnx_b, bonus = row(wt["lnx_w"]), row(wt["lnx_b"]), row(wt["bonus_scale"])
    e_dh = (jnp.arange(D)[:, None] // HEAD == jnp.arange(N_HEADS)[None, :]).astype(F32)
    e_hd = e_dh.T
    onehot = (jnp.arange(REL_TABLE)[:, None] == _rel_index()[None, :]).astype(BF16)

    begun = comm.begun
    (h1,) = _rowwise(_fn_pre, [_win(x)], [g_pre_mix], [(D, BF16)], name="pre_mix", tm=512, after=begun)
    (mn,) = _rowwise(_fn_pre, [_win(mem)], [g_mem], [(D, BF16)], name="pre_mem", tm=512, after=begun)
    bias = _mm(wt["rel_bias"].astype(F32), onehot, name="mm_bias", split_a=3, after=begun).reshape(N_HEADS, CHUNK, BAND)
    wt = {**wt, **comm.first_weights([h1, mn, bias])}
    w_in = wt["w_in_p"]
    d_up = jnp.pad(wt["decay_up"].astype(F32), ((0, 128 - LORA_W), (0, 0)))
    i_up = jnp.pad(wt["iclr_up"].astype(F32), ((0, 128 - LORA_A), (0, 0)))
    g_up = jnp.pad(wt["gate_up"].astype(F32), ((0, 256 - LORA_G), (0, 0)))
    proj = _mm(h1, w_in, tb=True, name="mm_in", after=comm.first_token)
    z_rkv = _shift_fwd(proj, 0, 3 * D, mix_rkv, seq, name="shift_rkv")
    z_lora = _shift_fwd(proj, C_LORA, 512, mix_lora, seq, name="shift_lora")
    prep_rows = [_win(z_rkv, D, D), _win(z_lora, 0, 128), _win(z_lora, 128, 128), _win(z_lora, 256, 256)]
    prep_params = [decay_base, d_up, iclr_base, i_up, g_up, kns, kis, e_hd, e_dh]
    lw, k2, kk, a, g = _rowwise(_fn_prep, prep_rows, prep_params, [(D, F32)] * 5, name="rwkv_prep", tm=256)
    y, states, invs = _wkv_fwd(z_rkv, lw, k2, kk, a, seq)
    post_rows = [_win(y), _win(z_rkv, 0, D), _win(k2), _win(z_rkv, 2 * D, D), _win(g)]
    post_params = [lnx_w, lnx_b, bonus, e_hd, e_dh]
    (y_a,) = _rowwise(_fn_post, post_rows, post_params, [(D, BF16)], name="rwkv_post", tm=256)
    y_b = _attn_fwd(proj, bias, seq)
    wt = {**wt, **comm.late_weights(y_b)}
    ya_p = _mm(y_a, wt["w_branch_a"], name="mm_a")
    yb_p = _mm(y_b, wt["w_branch_b"], name="mm_b")
    mix_rows = [_win(proj, C_GA, D), _win(proj, C_GA + D, D), _win(ya_p), _win(yb_p)]
    (mixed,) = _rowwise(_fn_mix, mix_rows, [], [(D, BF16)], name="gate_mix", tm=512)
    mo = _mm(mixed, wt["w_out"], name="mm_out")
    x1, h2 = _rowwise(_fn_res_pre, [_win(x), _win(mo)], [g_post_mix, g_pre_cross], [(D, F32), (D, BF16)],
                      name="res_mix", tm=512)
    qm = _mm(h2, wt["w_q_mem"], name="mm_q")
    kvm = _mm(mn, wt["w_kv_mem"], name="mm_kv")
    om = _xattn_fwd(qm, kvm, seq, n_mem)
    co = _mm(om, wt["w_o_mem"], name="mm_o")
    x2, h3 = _rowwise(_fn_res_pre, [_win(x1), _win(co)], [g_post_cross, g_pre_ffn], [(D, F32), (D, BF16)],
                      name="res_cross", tm=512)
    gu = _mm(h3, wt["w_ffn_in"], tb=True, name="mm_ffn_in", out_dtype=BF16)
    (act,) = _rowwise(_fn_swiglu, [_win(gu, 0, FFN), _win(gu, FFN, FFN)], [], [(FFN, BF16)], name="swiglu", tm=256)
    ff = _mm(act, wt["w_ffn_out"], name="mm_ffn_out")

    gw = {}
    loss, dx2, dff, gw["g_post_ffn"] = _loss_head(x2, ff, g_post_ffn, target)
    dact = _mm(dff, wt["w_ffn_out"], tb=True, name="mm_ffn_out_dx", out_dtype=BF16)
    gw["w_ffn_out"] = _mm(act, dff, ta=True, name="mm_ffn_out_dw", out_dtype=BF16)
    (dgu,), _ = _rowwise_bwd(_fn_swiglu, [_win(gu, 0, FFN), _win(gu, FFN, FFN)], [], 0, [[dact]],
                             name="swiglu_bwd", tm=256, row_grad=[BF16, BF16], packed=True)
    dh3 = _mm(dgu, wt["w_ffn_in"], name="mm_ffn_in_dx", out_dtype=BF16)
    gw["w_ffn_in"] = _mm(dgu, h3, ta=True, name="mm_ffn_in_dw", out_dtype=BF16)
    (dx1, dco), (gw["g_post_cross"], gw["g_pre_ffn"]) = _rowwise_bwd(
        _fn_res_pre, [_win(x1), _win(co)], [g_post_cross, g_pre_ffn], 0, [[dx2], [dh3]],
        name="res_cross_bwd", tm=256, row_grad=[F32, BF16])
    dom = _mm(dco, wt["w_o_mem"], tb=True, name="mm_o_dx", out_dtype=BF16)
    gw["w_o_mem"] = _mm(om, dco, ta=True, name="mm_o_dw", out_dtype=BF16)
    dqm, dkvm = _xattn_bwd(qm, kvm, dom, seq, n_mem)
    dh2 = _mm(dqm, wt["w_q_mem"], tb=True, name="mm_q_dx", out_dtype=BF16)
    gw["w_q_mem"] = _mm(h2, dqm, ta=True, name="mm_q_dw", out_dtype=BF16)
    dmn = _mm(dkvm, wt["w_kv_mem"], tb=True, name="mm_kv_dx", out_dtype=BF16)
    gw["w_kv_mem"] = _mm(mn, dkvm, ta=True, name="mm_kv_dw", out_dtype=BF16)
    _, (gw["g_mem"],) = _rowwise_bwd(_fn_pre, [_win(mem)], [g_mem], 0, [[dmn]], name="pre_mem_bwd", tm=256,
                                     row_grad=[None])
    (dx0, dmo), (gw["g_post_mix"], gw["g_pre_cross"]) = _rowwise_bwd(
        _fn_res_pre, [_win(x), _win(mo)], [g_post_mix, g_pre_cross], 0, [[dx1], [dh2]],
        name="res_mix_bwd", tm=256, row_grad=[F32, BF16])
    dmixed = _mm(dmo, wt["w_out"], tb=True, name="mm_out_dx", out_dtype=BF16)
    gw["w_out"] = _mm(mixed, dmo, ta=True, name="mm_out_dw", out_dtype=BF16)
    (dzga, dzgb, dya_p, dyb_p), _ = _rowwise_bwd(_fn_mix, mix_rows, [], 0, [[dmixed]], name="gate_mix_bwd", tm=256,
                                                 row_grad=[BF16] * 4)
    gw["w_branch_a"] = _mm(y_a, dya_p, ta=True, name="mm_a_dw", out_dtype=BF16)
    gw["w_branch_b"] = _mm(y_b, dyb_p, ta=True, name="mm_b_dw", out_dtype=BF16)
    token = comm.send_early(gw)
    dy_a = _mm(dya_p, wt["w_branch_a"], tb=True, name="mm_a_dx", out_dtype=BF16, after=token)
    dy_b = _mm(dyb_p, wt["w_branch_b"], tb=True, name="mm_b_dx", out_dtype=BF16, after=token)
    dq, dk, dv, dbias = _attn_bwd(proj, bias, dy_b, seq)
    gw["rel_bias"] = _mm(dbias.reshape(N_HEADS, CHUNK * BAND), onehot, tb=True, name="mm_bias_dw", split_a=2)
    (dy, dr_p, dk2_p, dv_p, dg), (gw["lnx_w"], gw["lnx_b"], gw["bonus_scale"]) = _rowwise_bwd(
        _fn_post, post_rows, post_params, 2, [[dy_a]], name="rwkv_post_bwd", tm=512, row_grad=[BF16] * 5)
    dr_s, dlw, dk2_s, dv_s, dkk, da = _wkv_bwd(z_rkv, lw, k2, kk, a, states, invs, dy, seq)
    (dzk, dzw, dza, dzg), pg = _rowwise_bwd(
        _fn_prep, prep_rows, prep_params, 2, [[dlw], [dk2_p, dk2_s], [dkk], [da], [dg]],
        name="rwkv_prep_bwd", tm=512, row_grad=[BF16] * 4)
    gw["decay_base"], gd_up, gw["iclr_base"], gi_up, gg_up, gw["key_norm_scale"], gw["key_iclr_scale"] = pg
    gw["decay_up"], gw["iclr_up"], gw["gate_up"] = gd_up[:LORA_W], gi_up[:LORA_A], gg_up[:LORA_G]
    dp_r, gmix_r = _shift_bwd(proj, 0, D, mix_rkv[:, :D], [dr_p, dr_s], seq, name="shift_r_bwd")
    dp_k, gmix_k = _shift_bwd(proj, D, D, mix_rkv[:, D:2 * D], [dzk], seq, name="shift_k_bwd")
    dp_v, gmix_v = _shift_bwd(proj, 2 * D, D, mix_rkv[:, 2 * D:], [dv_p, dv_s], seq, name="shift_v_bwd")
    dp_lora, gmix_lora = _shift_bwd(proj, C_LORA, 512, mix_lora, [jnp.concatenate([dzw, dza, dzg], axis=1)], seq,
                                    name="shift_lora_bwd")
    gw["shift_mix"] = jnp.concatenate([gmix_r, gmix_k, gmix_v, _unpad_lora(gmix_lora)], axis=1)
    dproj = [dp_r, dp_k, dp_v, dq, dk, dv, dzga, dzgb, dp_lora]
    gw["w_in_p"] = _mm_cat_tn(dproj, h1, name="mm_in_dw", after=gw["rel_bias"])
    token = comm.send_late(gw)
    dh1 = _mm_cat_nn(dproj, w_in, name="mm_in_dx", after=token)
    (grad_x,), (gw["g_pre_mix"],) = _rowwise_bwd(_fn_pre, [_win(x)], [g_pre_mix], 0, [[dh1]], name="pre_mix_bwd",
                                                 tm=256, row_grad=[F32], add_to={0: dx0})
    return loss, grad_x, gw


_COL_SHARDED = ("w_in", "decay_up", "iclr_up", "gate_up", "w_o_mem", "w_ffn_in")
_ROW_SHARDED = ("w_branch_a", "w_branch_b", "w_out", "w_q_mem", "w_kv_mem", "w_ffn_out")
_TRANSPOSED = ("w_in", "w_ffn_in")
_FIRST = ("w_in", "decay_up", "iclr_up", "gate_up")
_REST = ("w_o_mem", "w_ffn_in", "w_branch_a", "w_branch_b", "w_out", "w_q_mem", "w_kv_mem", "w_ffn_out")
_REPLICATED = ("g_pre_mix", "g_post_mix", "shift_mix", "decay_base", "iclr_base", "key_norm_scale", "key_iclr_scale",
               "bonus_scale", "lnx_w", "lnx_b", "rel_bias", "g_pre_cross", "g_post_cross", "g_mem", "g_pre_ffn",
               "g_post_ffn")
_WEIGHTS = ("g_pre_mix", "g_post_mix", "w_in", "shift_mix", "decay_base", "decay_up", "iclr_base", "iclr_up", "gate_up",
            "key_norm_scale", "key_iclr_scale", "bonus_scale", "lnx_w", "lnx_b", "rel_bias", "w_branch_a", "w_branch_b",
            "w_out", "g_pre_cross", "g_post_cross", "g_mem", "w_q_mem", "w_kv_mem", "w_o_mem", "g_pre_ffn", "g_post_ffn",
            "w_ffn_in", "w_ffn_out")
_PACK_ROWS = 8 * ((sum({"shift_mix": 3360, "bonus_scale": 1024, "rel_bias": 3072}.get(n, D) for n in _REPLICATED)
                   + 1 + 8 * LANE - 1) // (8 * LANE))


def _pack(vals):
    flat = jnp.concatenate([v.reshape(-1).astype(F32) for v in vals])
    return jnp.pad(flat, (0, _PACK_ROWS * LANE - flat.shape[0])).reshape(_PACK_ROWS, LANE)


def _unpack(packed, shapes):
    flat, out, pos = packed.reshape(-1), [], 0
    for s in shapes:
        n = math.prod(s)
        out.append(flat[pos:pos + n].reshape(s))
        pos += n
    return out


def _step(args, seq, n_mem):
    names = ("x", "mem") + _WEIGHTS + ("loss_target",) + tuple("m_" + n for n in _WEIGHTS) + tuple("v_" + n for n in _WEIGHTS)
    given = dict(zip(names, args))
    nb = given["x"].shape[0]
    x = given["x"].reshape(nb * seq, D)
    mem = given["mem"].reshape(nb * n_mem, D)
    target = given["loss_target"].reshape(nb * seq, D)
    def local(name, prefix=""):
        a = given[prefix + name][0]
        return a.T if name in _TRANSPOSED else a

    shard = {n: local(n) for n in _COL_SHARDED + _ROW_SHARDED}
    stacked = _ROW_SHARDED + _TRANSPOSED
    out = {}

    def wire(name):
        return shard[name].astype(BF16)

    def full(name, g):
        return g.reshape(-1, g.shape[-1]) if name in stacked else _cols_to_full(g)

    def blocks_of(name, g):
        return (g.reshape((N_DEV,) + shard[name].shape) if name in stacked else _full_to_cols(g)).astype(BF16)

    def update(names, landed, after=None):
        done = []
        for n, parts in zip(names, landed):
            res = _adamw(parts, shard[n], local(n, "m_"), local(n, "v_"), name="adamw_" + n, after=after)
            for kind, r in zip(("grad_", "delta_", "new_m_", "new_v_"), res):
                out[kind + n] = (r.T if n in _TRANSPOSED else r)[None]
            done.append(res[0])
        return done


    class Exchanges:
        def __init__(self):
            srcs = [wire(n) for n in _FIRST]
            self.first, self.begun = _exchange_start(srcs, False, srcs[0], name="gather_first_start",
                                                     dists=_SIBLING_AND_SAME_CORE)

        def first_weights(self, after):
            got = _exchange_wait(self.first, after, [wire(n) for n in _FIRST], name="gather_first_wait")
            relayed = _relay_to_sibling(got, name="gather_first_relay")
            pos = _mesh_pos()
            for j, d in enumerate(_OTHER_CHIPS):
                slot = _flat(_peer(pos, d | 1))
                got = [lax.dynamic_update_slice_in_dim(g, r[j][None], slot, 0) for g, r in zip(got, relayed)]
            self.rest, self.first_token = _exchange_start(
                [wire(n) for n in _REST], False, got[0], name="gather_rest_start")
            first = {n: full(n, g) for n, g in zip(_FIRST, got)}
            first["w_in_p"] = _permute_in(first.pop("w_in"), 0)
            return first

        def late_weights(self, after):
            got = _exchange_wait(self.rest, [after], [wire(n) for n in _REST], name="gather_rest_wait")
            return {n: full(n, g) for n, g in zip(_REST, got)}

        def send_early(self, gw):
            self.early_blocks = [blocks_of(n, gw[n]) for n in _REST]
            self.early, token = _exchange_start(self.early_blocks, True, self.early_blocks[-1], name="scatter_rest_start")
            return token

        def send_late(self, gw):
            me = _flat(_mesh_pos())
            own = [lax.dynamic_index_in_dim(b, me, 0, keepdims=False) for b in self.early_blocks]
            landed = _exchange_wait(self.early, [gw["w_in_p"]], own, name="scatter_rest_wait")
            grads = {**gw, "w_in": _unpermute_in(gw["w_in_p"], 0)}
            self.late_blocks = [blocks_of(n, grads[n]) for n in _FIRST]
            self.late, token = _exchange_start(self.late_blocks, True, landed[0], name="scatter_first_start")
            self.updated = update(_REST, landed, after=token)
            return token

        def finish(self, after):
            me = _flat(_mesh_pos())
            own = [lax.dynamic_index_in_dim(b, me, 0, keepdims=False) for b in self.late_blocks]
            update(_FIRST, _exchange_wait(self.late, [*after, *self.updated], own, name="scatter_first_wait"))

    comm = Exchanges()
    wt = {n: given[n][0] for n in _REPLICATED}
    loss_tile, grad_x, gw = _local_step(x, mem, target, wt, seq, n_mem, comm)
    rep_shapes = [given[n].shape for n in _REPLICATED]
    packed, _ = lax.optimization_barrier((_pack([gw[n] for n in _REPLICATED] + [loss_tile[0, 0]]), tuple(comm.updated)))
    small = _exchange([packed], False, name="gather_small")[0]
    zero = jnp.zeros((), F32)
    res = _adamw(small, *[_pack([given[p + n] for n in _REPLICATED] + [zero]) for p in ("", "m_", "v_")],
                 name="adamw_small", tr=_PACK_ROWS)
    for kind, r in zip(("grad_", "delta_", "new_m_", "new_v_"), res):
        for n, val in zip(_REPLICATED, _unpack(r, rep_shapes)):
            out[kind + n] = val
    loss = res[0].reshape(-1)[sum(math.prod(s) for s in rep_shapes)]
    comm.finish([grad_x, res[0]])
    grad_x = grad_x.reshape(nb, seq, D)
    return (loss, grad_x, *[out[k + n] for k in ("grad_", "delta_", "new_m_", "new_v_") for n in _WEIGHTS])


def kernel(x, mem, g_pre_mix, g_post_mix, w_in, shift_mix, decay_base, decay_up, iclr_base, iclr_up, gate_up, key_norm_scale, key_iclr_scale, bonus_scale, lnx_w, lnx_b, rel_bias, w_branch_a, w_branch_b, w_out, g_pre_cross, g_post_cross, g_mem, w_q_mem, w_kv_mem, w_o_mem, g_pre_ffn, g_post_ffn, w_ffn_in, w_ffn_out, loss_target, m_g_pre_mix, m_g_post_mix, m_w_in, m_shift_mix, m_decay_base, m_decay_up, m_iclr_base, m_iclr_up, m_gate_up, m_key_norm_scale, m_key_iclr_scale, m_bonus_scale, m_lnx_w, m_lnx_b, m_rel_bias, m_w_branch_a, m_w_branch_b, m_w_out, m_g_pre_cross, m_g_post_cross, m_g_mem, m_w_q_mem, m_w_kv_mem, m_w_o_mem, m_g_pre_ffn, m_g_post_ffn, m_w_ffn_in, m_w_ffn_out, v_g_pre_mix, v_g_post_mix, v_w_in, v_shift_mix, v_decay_base, v_decay_up, v_iclr_base, v_iclr_up, v_gate_up, v_key_norm_scale, v_key_iclr_scale, v_bonus_scale, v_lnx_w, v_lnx_b, v_rel_bias, v_w_branch_a, v_w_branch_b, v_w_out, v_g_pre_cross, v_g_post_cross, v_g_mem, v_w_q_mem, v_w_kv_mem, v_w_o_mem, v_g_pre_ffn, v_g_post_ffn, v_w_ffn_in, v_w_ffn_out):
    args = (x, mem, g_pre_mix, g_post_mix, w_in, shift_mix, decay_base, decay_up, iclr_base, iclr_up, gate_up, key_norm_scale, key_iclr_scale, bonus_scale, lnx_w, lnx_b, rel_bias, w_branch_a, w_branch_b, w_out, g_pre_cross, g_post_cross, g_mem, w_q_mem, w_kv_mem, w_o_mem, g_pre_ffn, g_post_ffn, w_ffn_in, w_ffn_out, loss_target, m_g_pre_mix, m_g_post_mix, m_w_in, m_shift_mix, m_decay_base, m_decay_up, m_iclr_base, m_iclr_up, m_gate_up, m_key_norm_scale, m_key_iclr_scale, m_bonus_scale, m_lnx_w, m_lnx_b, m_rel_bias, m_w_branch_a, m_w_branch_b, m_w_out, m_g_pre_cross, m_g_post_cross, m_g_mem, m_w_q_mem, m_w_kv_mem, m_w_o_mem, m_g_pre_ffn, m_g_post_ffn, m_w_ffn_in, m_w_ffn_out, v_g_pre_mix, v_g_post_mix, v_w_in, v_shift_mix, v_decay_base, v_decay_up, v_iclr_base, v_iclr_up, v_gate_up, v_key_norm_scale, v_key_iclr_scale, v_bonus_scale, v_lnx_w, v_lnx_b, v_rel_bias, v_w_branch_a, v_w_branch_b, v_w_out, v_g_pre_cross, v_g_post_cross, v_g_mem, v_w_q_mem, v_w_kv_mem, v_w_o_mem, v_g_pre_ffn, v_g_post_ffn, v_w_ffn_in, v_w_ffn_out)
    return _step(args, x.shape[1], mem.shape[1])
```

```python
import functools
import math

import jax
import jax.numpy as jnp
from jax import lax
from jax.experimental import pallas as pl
from jax.experimental.pallas import tpu as pltpu

F32 = jnp.float32
BF16 = jnp.bfloat16

N_DEV = 8
D = 1024
HEAD = 64
N_HEADS = D // HEAD
LANE = 128
CHUNK = 64
LEFT = 8 * CHUNK
BAND = LEFT + CHUNK
REL_CLIP = 128
REL_TABLE = CHUNK + REL_CLIP
MEM_WIDTH = D // 2
MEM_HEADS = 4
FFN = 2816
LORA_W, LORA_A, LORA_G = 64, 64, 160
P_WIDTH = 3 * D + 3 * D + 2 * D + 128 + 128 + 256
C_Q, C_GA, C_LORA = 3 * D, 6 * D, 8 * D
NORM_EPS = 1e-6
GROUP_NORM_EPS = 64e-5
MASK_VALUE = -1e30
ADAM_LR, ADAM_B1, ADAM_B2, ADAM_EPS, ADAM_WD, ADAM_STEP = 0.001, 0.9, 0.999, 1e-08, 0.01, 10
VMEM_LIMIT = 56 * 1024 * 1024


def _cp(*sem):
    return pltpu.CompilerParams(dimension_semantics=sem, vmem_limit_bytes=VMEM_LIMIT)


_NN, _NT, _TN = ((1,), (0,)), ((1,), (1,)), ((0,), (0,))


def _dot_raw(a, b, dims):
    return lax.dot_general(a.astype(BF16), b.astype(BF16), (dims, ((), ())), preferred_element_type=F32)


@functools.partial(jax.custom_vjp, nondiff_argnums=(2,))
def _dot_dims(a, b, dims):
    return _dot_raw(a, b, dims)


def _dot_dims_fwd(a, b, dims):
    return _dot_raw(a, b, dims), (a, b)


def _dot_dims_bwd(dims, res, g):
    a, b = res
    if dims == _NN:
        da, db = _dot_raw(g, b, _NT), _dot_raw(a, g, _TN)
    elif dims == _NT:
        da, db = _dot_raw(g, b, _NN), _dot_raw(g, a, _TN)
    else:
        da, db = _dot_raw(b, g, _NT), _dot_raw(a, g, _NN)
    return da.astype(a.dtype), db.astype(b.dtype)


_dot_dims.defvjp(_dot_dims_fwd, _dot_dims_bwd)


def _dot(a, b, dims=_NN):
    return _dot_dims(a, b, dims)


def _dot_nt(a, b):
    return _dot_dims(a, b, _NT)


def _dot_tn(a, b):
    return _dot_dims(a, b, _TN)


def _split(x, terms):
    parts, rest = [], x.astype(F32)
    for _ in range(terms):
        p = rest.astype(BF16)
        parts.append(p)
        rest = rest - p.astype(F32)
    return parts


def _dot_split_a(a, b, terms=2):
    out = None
    for p in _split(a, terms):
        t = _dot(p, b)
        out = t if out is None else out + t
    return out


def _dot_split_b(a, b, terms=3):
    out = None
    for p in _split(b, terms):
        t = _dot(a, p)
        out = t if out is None else out + t
    return out


MM_VMEM_BUDGET = 30 * 1024 * 1024
MM_HBM_BPS = 3.2e12
MM_MXU_FPS = 8.5e14
MM_STEP_S = 0.35e-6


def _divisors(n, align, cap):
    out = [d for d in range(align, min(n, cap) + 1, align) if n % d == 0]
    return out or [n]


def _mm_tiles(m, n, k, ea, eb, eo, ta):
    best = None
    for tm in _divisors(m, LANE if ta else 8, 2048):
        for tn in _divisors(n, LANE, 2048):
            for tk in _divisors(k, LANE, 2048):
                nk = k // tk
                vmem = 2 * (tm * tk * ea + tk * tn * eb + tm * tn * eo) + (tm * tn * 4 if nk > 1 else 0)
                if vmem > MM_VMEM_BUDGET:
                    continue
                dma = (tm * tk * ea if (nk > 1 or n // tn == 1) else tm * tk * ea * tn / n) + tk * tn * eb + tm * tn * eo / nk
                step = max(2.0 * tm * tn * tk / MM_MXU_FPS, dma / MM_HBM_BPS) + MM_STEP_S
                cost = (m // tm) * (n // tn) * nk * step
                if best is None or cost < best[0]:
                    best = (cost, tm, tn, tk)
    return best[1:]


def _mm(a, b, *, name, ta=False, tb=False, out_dtype=F32, tm=None, tn=None, tk=None, split_a=1, after=None):
    m, k = (a.shape[1], a.shape[0]) if ta else a.shape
    n, kb = (b.shape[0], b.shape[1]) if tb else (b.shape[1], b.shape[0])
    assert k == kb, (a.shape, b.shape, ta, tb)
    if tm is None:
        tm, tn, tk = _mm_tiles(m, n, k, a.dtype.itemsize, b.dtype.itemsize, jnp.dtype(out_dtype).itemsize, ta)
    assert m % tm == 0 and n % tn == 0 and k % tk == 0, (m, n, k, tm, tn, tk)
    nk = k // tk
    dims = ((0 if ta else 1,), (1 if tb else 0,))

    n_after = 0 if after is None else 1

    def body(a_ref, b_ref, *rest):
        o_ref, scratch = rest[n_after], rest[n_after + 1:]
        prod = None
        for p in _split(a_ref[...], split_a) if split_a > 1 else [a_ref[...]]:
            t = _dot_raw(p, b_ref[...], dims)
            prod = t if prod is None else prod + t
        if nk == 1:
            o_ref[...] = prod.astype(o_ref.dtype)
            return
        acc_ref, kk = scratch[0], pl.program_id(2)

        @pl.when(kk == 0)
        def _():
            acc_ref[...] = prod

        @pl.when(kk > 0)
        def _():
            acc_ref[...] += prod

        @pl.when(kk == nk - 1)
        def _():
            o_ref[...] = acc_ref[...].astype(o_ref.dtype)

    a_spec = pl.BlockSpec((tk, tm), lambda i, j, q: (q, i)) if ta else pl.BlockSpec((tm, tk), lambda i, j, q: (i, q))
    b_spec = pl.BlockSpec((tn, tk), lambda i, j, q: (j, q)) if tb else pl.BlockSpec((tk, tn), lambda i, j, q: (q, j))
    return pl.pallas_call(
        body, name=name, grid=(m // tm, n // tn, nk),
        in_specs=[a_spec, b_spec] + [pl.BlockSpec(memory_space=pl.ANY)] * n_after,
        out_specs=pl.BlockSpec((tm, tn), lambda i, j, q: (i, j)),
        out_shape=jax.ShapeDtypeStruct((m, n), out_dtype),
        scratch_shapes=[pltpu.VMEM((tm, tn), F32)] if nk > 1 else [],
        compiler_params=_cp("parallel", "parallel", "arbitrary"),
    )(a, b, *([] if after is None else [after]))


def _piece_steps(pieces, tile):
    counts = [p.shape[1] // tile for p in pieces]
    assert all(p.shape[1] % tile == 0 for p in pieces)
    return [(sum(counts[:i]), c) for i, c in enumerate(counts)], sum(counts)


def _mm_cat_nn(pieces, w, *, name, after=None, tm=2048, tk=256):
    t, n = pieces[0].shape[0], w.shape[1]
    tm = min(tm, t)
    spans, nk = _piece_steps(pieces, tk)
    npc = len(pieces)
    n_after = 0 if after is None else 1

    def body(*refs):
        w_ref, o_ref, acc_ref = refs[npc], refs[npc + 1 + n_after], refs[npc + 2 + n_after]
        q = pl.program_id(1)

        @pl.when(q == 0)
        def _():
            acc_ref[...] = jnp.zeros_like(acc_ref)

        for p_ref, (first, count) in zip(refs[:npc], spans):
            @pl.when(jnp.logical_and(q >= first, q < first + count))
            def _(p_ref=p_ref):
                acc_ref[...] += _dot_raw(p_ref[...], w_ref[...], _NN)

        @pl.when(q == nk - 1)
        def _():
            o_ref[...] = acc_ref[...].astype(o_ref.dtype)

    def piece_spec(first, count):
        return pl.BlockSpec((tm, tk), lambda i, q: (i, jnp.clip(q - first, 0, count - 1)))

    return pl.pallas_call(
        body, name=name, grid=(t // tm, nk),
        in_specs=[piece_spec(*s) for s in spans] + [pl.BlockSpec((tk, n), lambda i, q: (q, 0))]
        + [pl.BlockSpec(memory_space=pl.ANY)] * n_after,
        out_specs=pl.BlockSpec((tm, n), lambda i, q: (i, 0)),
        out_shape=jax.ShapeDtypeStruct((t, n), BF16),
        scratch_shapes=[pltpu.VMEM((tm, n), F32)],
        compiler_params=_cp("parallel", "arbitrary"),
    )(*pieces, w, *([] if after is None else [after]))


def _mm_cat_tn(pieces, a, *, name, after=None, tk=1024, tn=512):
    t, m = a.shape
    tk = min(tk, t)
    spans, nj = _piece_steps(pieces, tn)
    npc, nk = len(pieces), t // tk
    n_after = 0 if after is None else 1

    def body(a_ref, *refs):
        o_ref, acc_ref = refs[npc + n_after], refs[npc + 1 + n_after]
        j, q = pl.program_id(0), pl.program_id(1)

        @pl.when(q == 0)
        def _():
            acc_ref[...] = jnp.zeros_like(acc_ref)

        for p_ref, (first, count) in zip(refs[:npc], spans):
            @pl.when(jnp.logical_and(j >= first, j < first + count))
            def _(p_ref=p_ref):
                acc_ref[...] += _dot_raw(p_ref[...], a_ref[...], _TN)

        @pl.when(q == nk - 1)
        def _():
            o_ref[...] = acc_ref[...].astype(o_ref.dtype)

    def piece_spec(first, count):
        def index(j, q):
            mine = jnp.logical_and(j >= first, j < first + count)
            return jnp.where(mine, q, 0), jnp.clip(j - first, 0, count - 1)
        return pl.BlockSpec((tk, tn), index)

    return pl.pallas_call(
        body, name=name, grid=(nj, nk),
        in_specs=[pl.BlockSpec((tk, m), lambda j, q: (q, 0))] + [piece_spec(*s) for s in spans]
        + [pl.BlockSpec(memory_space=pl.ANY)] * n_after,
        out_specs=pl.BlockSpec((tn, m), lambda j, q: (j, 0)),
        out_shape=jax.ShapeDtypeStruct((nj * tn, m), BF16),
        scratch_shapes=[pltpu.VMEM((tn, m), F32)],
        compiler_params=_cp("parallel", "arbitrary"),
    )(a, *pieces, *([] if after is None else [after]))


def _win(arr, start=0, width=None):
    width = arr.shape[1] if width is None else width
    assert start % width == 0
    return (arr, start // width, width)


def _row_specs(rows, tm):
    return [pl.BlockSpec((tm, w), functools.partial(lambda i, cb: (i, cb), cb=cb)) for (_, cb, w) in rows]


def _full_spec(p):
    nd = p.ndim
    return pl.BlockSpec(p.shape, lambda i, nd=nd: (0,) * nd)


def _rowwise(fn, rows, params, outs, *, name, tm, after=None):
    t = rows[0][0].shape[0]
    tm = min(tm, t)
    assert t % tm == 0
    nr, npar = len(rows), len(params)
    n_after = 0 if after is None else 1

    def body(*refs):
        vals = [r[...] for r in refs[:nr + npar]]
        res = fn(*vals)
        for o_ref, r in zip(refs[nr + npar + n_after:], res):
            o_ref[...] = r.astype(o_ref.dtype)

    return pl.pallas_call(
        body, name=name, grid=(t // tm,),
        in_specs=_row_specs(rows, tm) + [_full_spec(p) for p in params] + [pl.BlockSpec(memory_space=pl.ANY)] * n_after,
        out_specs=[pl.BlockSpec((tm, w), lambda i: (i, 0)) for (w, _) in outs],
        out_shape=[jax.ShapeDtypeStruct((t, w), dt) for (w, dt) in outs],
        compiler_params=_cp("parallel"),
    )(*[r[0] for r in rows], *params, *([] if after is None else [after]))


def _rowwise_bwd(fn, rows, params, n_const, cots, *, name, tm, row_grad, add_to=None, packed=False):
    t = rows[0][0].shape[0]
    tm = min(tm, t)
    assert t % tm == 0
    nr, npar = len(rows), len(params)
    ndp = npar - n_const
    add_to = add_to or {}
    add_idx = sorted(add_to)
    flat_cots = [c for group in cots for c in group]
    kept = [i for i in range(nr) if row_grad[i] is not None]

    def body(*refs):
        pos = 0
        row_v = [r[...] for r in refs[pos:pos + nr]]; pos += nr
        par_v = [r[...] for r in refs[pos:pos + npar]]; pos += npar
        cot_v = [r[...] for r in refs[pos:pos + len(flat_cots)]]; pos += len(flat_cots)
        add_v = [r[...] for r in refs[pos:pos + len(add_idx)]]; pos += len(add_idx)
        if packed:
            offs = [sum(rows[i][2] for i in kept[:q]) for q in range(len(kept))]
            rg_refs = [refs[pos].at[:, o:o + rows[i][2]] for o, i in zip(offs, kept)]; pos += 1
        else:
            rg_refs = refs[pos:pos + len(kept)]; pos += len(kept)
        pg_refs = refs[pos:pos + ndp]

        consts = par_v[ndp:]
        res, vjp = jax.vjp(lambda *args: tuple(fn(*args, *consts)), *row_v, *par_v[:ndp])
        cot_in, q = [], 0
        for j, group in enumerate(cots):
            c = None
            for _ in group:
                cv = cot_v[q].astype(F32); q += 1
                c = cv if c is None else c + cv
            c = jnp.zeros(res[j].shape, F32) if c is None else c
            cot_in.append(c.astype(res[j].dtype))
        grads = vjp(tuple(cot_in))
        for ref, i in zip(rg_refs, kept):
            g = grads[i].astype(F32)
            if i in add_to:
                g = g + add_v[add_idx.index(i)].astype(F32)
            ref[...] = g.astype(ref.dtype)

        @pl.when(pl.program_id(0) == 0)
        def _():
            for ref in pg_refs:
                ref[...] = jnp.zeros_like(ref)

        for ref, g in zip(pg_refs, grads[nr:]):
            ref[...] += g.astype(F32)

    cot_specs = [pl.BlockSpec((tm, c.shape[1]), lambda i: (i, 0)) for c in flat_cots]
    add_specs = [pl.BlockSpec((tm, add_to[i].shape[1]), lambda i_: (i_, 0)) for i in add_idx]
    widths = [sum(rows[i][2] for i in kept)] if packed else [rows[i][2] for i in kept]
    n_rg = len(widths)
    out_specs = [pl.BlockSpec((tm, w), lambda i_: (i_, 0)) for w in widths] + [_full_spec(p) for p in params[:ndp]]
    out_shape = [jax.ShapeDtypeStruct((t, w), row_grad[kept[q]]) for q, w in enumerate(widths)] + [
        jax.ShapeDtypeStruct(p.shape, F32) for p in params[:ndp]]
    res = pl.pallas_call(
        body, name=name, grid=(t // tm,),
        in_specs=_row_specs(rows, tm) + [_full_spec(p) for p in params] + cot_specs + add_specs,
        out_specs=out_specs, out_shape=out_shape,
        compiler_params=_cp("arbitrary"),
    )(*[r[0] for r in rows], *params, *flat_cots, *[add_to[i] for i in add_idx])
    return list(res[:n_rg]), list(res[n_rg:])


def _rms(x, g):
    xf = x.astype(F32)
    return xf * lax.rsqrt(jnp.mean(xf * xf, axis=-1, keepdims=True) + NORM_EPS) * g


def _softplus(x):
    return jnp.maximum(x, 0.0) + jnp.log(1.0 + jnp.exp(-jnp.abs(x)))


def _fn_pre(x, g):
    return (_rms(x, g).astype(BF16),)


def _fn_res(x, u, g_post):
    return (x + _rms(u, g_post),)


def _fn_res_pre(x, u, g_post, g_pre):
    xn = x + _rms(u, g_post)
    return xn, _rms(xn, g_pre).astype(BF16)


def _fn_mix(zga, zgb, ya, yb):
    return ((jax.nn.sigmoid(zga) * ya + jax.nn.sigmoid(zgb) * yb).astype(BF16),)


def _fn_swiglu(gate, up):
    gate, up = gate.astype(F32), up.astype(F32)
    return ((gate * jax.nn.sigmoid(gate) * up).astype(BF16),)


def _fn_prep(zk, zw, za, zg, decay_base, d_up, iclr_base, i_up, g_up, kns, kis, e_hd, e_dh):
    w_log = -_softplus(-(decay_base + _dot(jnp.tanh(zw), d_up))) - 0.5
    lw = -jnp.exp(w_log)
    a = jax.nn.sigmoid(iclr_base + _dot(za, i_up))
    g = _dot(jax.nn.sigmoid(zg), g_up)
    kn = zk * kns
    ss = _dot(kn * kn, e_dh)
    inv = lax.rsqrt(jnp.maximum(ss, 1e-24))
    kk = kn * _dot_split_a(inv, e_hd)
    k2 = zk * (1.0 + (a - 1.0) * kis)
    return lw, k2, kk, a, g


def _fn_post(y, r, k2, v, g, lnx_w, lnx_b, bonus, e_hd, e_dh):
    mu = _dot_split_a(_dot(y, e_dh) * (1.0 / HEAD), e_hd)
    yc = y - mu
    var = _dot(yc * yc, e_dh) * (1.0 / HEAD)
    yn = yc * _dot_split_a(lax.rsqrt(var + GROUP_NORM_EPS), e_hd)
    bs = _dot_split_a(_dot(r * k2 * bonus, e_dh), e_hd)
    return (((yn * lnx_w + lnx_b + bs * v) * g).astype(BF16),)


def _shift_fwd(p, col0, ncols, mix, seq, *, name, cw=256):
    t = p.shape[0]
    assert col0 % cw == 0 and ncols % cw == 0 and t % seq == 0
    cb0 = col0 // cw

    def body(p_ref, m_ref, z_ref):
        pv = p_ref[...]
        row = lax.broadcasted_iota(jnp.int32, pv.shape, 0)
        prev = jnp.where(row == 0, 0.0, pltpu.roll(pv, 1, axis=0))
        z_ref[...] = pv + (prev - pv) * m_ref[...]

    return pl.pallas_call(
        body, name=name, grid=(t // seq, ncols // cw),
        in_specs=[pl.BlockSpec((seq, cw), lambda b, c: (b, c + cb0)), pl.BlockSpec((1, cw), lambda b, c: (0, c))],
        out_specs=pl.BlockSpec((seq, cw), lambda b, c: (b, c)),
        out_shape=jax.ShapeDtypeStruct((t, ncols), F32),
        compiler_params=_cp("parallel", "parallel"),
    )(p, mix)


def _shift_bwd(p, col0, ncols, mix, dz_parts, seq, *, name, cw=256):
    t = p.shape[0]
    cb0 = col0 // cw
    n = len(dz_parts)

    def body(*refs):
        p_ref, m_ref = refs[:2]
        dp_ref, dm_ref = refs[2 + n:]
        dz = refs[2][...].astype(F32)
        for r in refs[3:2 + n]:
            dz = dz + r[...].astype(F32)
        pv = p_ref[...]
        mixv = m_ref[...]
        row = lax.broadcasted_iota(jnp.int32, pv.shape, 0)
        prev = jnp.where(row == 0, 0.0, pltpu.roll(pv, 1, axis=0))
        u = dz * mixv
        nxt = jnp.where(row == seq - 1, 0.0, pltpu.roll(u, seq - 1, axis=0))
        dp_ref[...] = (dz - u + nxt).astype(dp_ref.dtype)

        @pl.when(pl.program_id(1) == 0)
        def _():
            dm_ref[...] = jnp.zeros_like(dm_ref)

        dm_ref[...] += jnp.sum(dz * (prev - pv), axis=0, keepdims=True)

    return pl.pallas_call(
        body, name=name, grid=(ncols // cw, t // seq),
        in_specs=[pl.BlockSpec((seq, cw), lambda c, b: (b, c + cb0)), pl.BlockSpec((1, cw), lambda c, b: (0, c))]
        + [pl.BlockSpec((seq, cw), lambda c, b: (b, c))] * n,
        out_specs=[pl.BlockSpec((seq, cw), lambda c, b: (b, c)), pl.BlockSpec((1, cw), lambda c, b: (0, c))],
        out_shape=[jax.ShapeDtypeStruct((t, ncols), BF16), jax.ShapeDtypeStruct((1, ncols), F32)],
        compiler_params=_cp("parallel", "arbitrary"),
    )(p, mix, *dz_parts)


def _each(f, *lists):
    return [f(*xs) for xs in zip(*lists)]


def _tri_inv(low):
    c = low[0].shape[0]
    ti = lax.broadcasted_iota(jnp.int32, (c, c), 0)
    si = lax.broadcasted_iota(jnp.int32, (c, c), 1)
    eye = (ti == si).astype(F32)
    inside = (ti // 4) == (si // 4)
    base = [jnp.where(inside, m, 0.0) for m in low]
    acc = _each(lambda m: _dot(eye - m, eye + _dot(m, m)), base)
    size = 8
    while size <= c:
        wider = (ti // size) == (si // size)
        keep = jnp.logical_and(wider, jnp.logical_not(inside))
        acc = _each(lambda p, m: p - _dot(_dot(p, jnp.where(keep, m, 0.0)), p), acc, low)
        inside, size = wider, size * 2
    return acc


def _stack_rows(a, b):
    return jnp.concatenate([a, b], axis=0)


@jax.custom_vjp
def _split_rows(x):
    h = x.shape[0] // 2
    return x[:h], x[h:]


def _split_rows_fwd(x):
    return _split_rows(x), None


def _split_rows_bwd(_, g):
    return (jnp.concatenate(g, axis=0),)


_split_rows.defvjp(_split_rows_fwd, _split_rows_bwd)


def _masked_halves(stacked, top_mask, bottom_mask):
    halves = _each(_split_rows, stacked)
    return ([jnp.where(top_mask, t, 0.0) for t, _ in halves], [jnp.where(bottom_mask, b, 0.0) for _, b in halves])


@jax.custom_vjp
def _tri_inv_known(low, inv):
    return inv


def _tri_inv_known_fwd(low, inv):
    return inv, inv


def _tri_inv_known_bwd(inv, g):
    dlow = _each(lambda t, gg: -_dot(_dot(t, gg, _TN), t, _NT), inv, g)
    return dlow, _each(jnp.zeros_like, inv)


_tri_inv_known.defvjp(_tri_inv_known_fwd, _tri_inv_known_bwd)


def _wkv_chunk(s0, r, lw, k, v, kk, a, inv=None):
    c = r[0].shape[0]
    ti = lax.broadcasted_iota(jnp.int32, (c, c), 0)
    si = lax.broadcasted_iota(jnp.int32, (c, c), 1)
    incl, strict = ti >= si, ti > si
    tri = incl.astype(F32)
    cum = _each(lambda x: _dot_split_b(tri, x, 3), lw)
    eg = _each(jnp.exp, cum)
    egp = _each(lambda cs, x: jnp.exp(cs - x), cum, lw)
    ei = _each(lambda cs: jnp.exp(-cs), cum)
    rh, kkh, kt = _each(jnp.multiply, r, eg), _each(jnp.multiply, kk, egp), _each(jnp.multiply, k, ei)
    bt = _each(lambda p, q, e: (p * q) * e, a, kk, ei)
    both = _each(_stack_rows, kkh, rh)
    on_b, on_k, on_s = _each(_dot_nt, both, bt), _each(_dot_nt, both, kt), _each(_dot_nt, both, s0)
    lb, mb = _masked_halves(on_b, strict, incl)
    lk, mk = _masked_halves(on_k, strict, incl)
    on_s = _each(_split_rows, on_s)
    on_v = _each(lambda p, q, x: _split_rows(_dot(_stack_rows(p, q), x)), lk, mk, v)
    rhs = _each(lambda p, q: p[0] + q[0], on_s, on_v)
    inv = _tri_inv(lb) if inv is None else _tri_inv_known(lb, inv)
    u = _each(lambda t, x: -_dot(t, x), inv, rhs)
    y = _each(lambda p, m1, uu, q: p[1] + _dot(m1, uu) + q[1], on_s, mb, u, on_v)
    s1 = _each(lambda s, uu, x, b, kq, w: (s + _dot_tn(_stack_rows(uu, x), _stack_rows(b, kq)))
               * jnp.exp(jnp.sum(w, axis=0, keepdims=True)), s0, u, v, bt, kt, lw)
    return y, s1, inv


WKV_HEADS = 16
WKV_COLS = WKV_HEADS * HEAD
WKV_GROUPS = N_HEADS // WKV_HEADS


def _head_cols(ref):
    return [ref[:, h * HEAD:(h + 1) * HEAD] for h in range(ref.shape[1] // HEAD)]


def _wkv_specs(seq, rev):
    nc = seq // CHUNK

    def rows(col0):
        cb0 = col0 // WKV_COLS
        if rev:
            return pl.BlockSpec((CHUNK, WKV_COLS), lambda b, h, c: (b * nc + nc - 1 - c, cb0 + h))
        return pl.BlockSpec((CHUNK, WKV_COLS), lambda b, h, c: (b * nc + c, cb0 + h))

    if rev:
        st = pl.BlockSpec((1, 1, WKV_HEADS, HEAD, HEAD), lambda b, h, c: (b * WKV_GROUPS + h, nc - 1 - c, 0, 0, 0))
    else:
        st = pl.BlockSpec((1, 1, WKV_HEADS, HEAD, HEAD), lambda b, h, c: (b * WKV_GROUPS + h, c, 0, 0, 0))
    return rows, st


def _wkv_fwd(z_rkv, lw, k2, kk, a, seq):
    t = z_rkv.shape[0]
    nb, nc = t // seq, seq // CHUNK
    rows, st = _wkv_specs(seq, False)

    def body(r_ref, v_ref, lw_ref, k_ref, kk_ref, a_ref, y_ref, st_ref, inv_ref, s_scr):
        @pl.when(pl.program_id(2) == 0)
        def _():
            s_scr[...] = jnp.zeros_like(s_scr)

        s0 = [s_scr[h] for h in range(WKV_HEADS)]
        y, s1, inv = _wkv_chunk(s0, *[_head_cols(ref) for ref in (r_ref, lw_ref, k_ref, v_ref, kk_ref, a_ref)])
        for h in range(WKV_HEADS):
            st_ref[0, 0, h] = s0[h]
            inv_ref[0, 0, h] = inv[h]
            y_ref[:, h * HEAD:(h + 1) * HEAD] = y[h]
            s_scr[h] = s1[h]

    per_chunk = jax.ShapeDtypeStruct((nb * WKV_GROUPS, nc, WKV_HEADS, HEAD, HEAD), F32)
    return pl.pallas_call(
        body, name="wkv_fwd", grid=(nb, WKV_GROUPS, nc),
        in_specs=[rows(0), rows(2 * D), rows(0), rows(0), rows(0), rows(0)],
        out_specs=[rows(0), st, st],
        out_shape=[jax.ShapeDtypeStruct((t, D), F32), per_chunk, per_chunk],
        scratch_shapes=[pltpu.VMEM((WKV_HEADS, HEAD, HEAD), F32)],
        compiler_params=_cp("parallel", "parallel", "arbitrary"),
    )(z_rkv, z_rkv, lw, k2, kk, a)


def _wkv_bwd(z_rkv, lw, k2, kk, a, states, invs, dy, seq):
    t = z_rkv.shape[0]
    nb, nc = t // seq, seq // CHUNK
    rows, st = _wkv_specs(seq, True)

    def body(r_ref, v_ref, lw_ref, k_ref, kk_ref, a_ref, st_ref, inv_ref, dy_ref,
             dr_ref, dlw_ref, dk_ref, dv_ref, dkk_ref, da_ref, ds_scr):
        @pl.when(pl.program_id(2) == 0)
        def _():
            ds_scr[...] = jnp.zeros_like(ds_scr)

        s0 = [st_ref[0, 0, h] for h in range(WKV_HEADS)]
        inv = [inv_ref[0, 0, h] for h in range(WKV_HEADS)]
        _, vjp = jax.vjp(lambda *args: _wkv_chunk(*args, inv=inv)[:2],
                         s0, *[_head_cols(ref) for ref in (r_ref, lw_ref, k_ref, v_ref, kk_ref, a_ref)])
        grads = vjp(([x.astype(F32) for x in _head_cols(dy_ref)], [ds_scr[h] for h in range(WKV_HEADS)]))
        for h in range(WKV_HEADS):
            ds_scr[h] = grads[0][h]
            for ref, g in zip((dr_ref, dlw_ref, dk_ref, dv_ref, dkk_ref, da_ref), grads[1:]):
                ref[:, h * HEAD:(h + 1) * HEAD] = g[h].astype(ref.dtype)

    return pl.pallas_call(
        body, name="wkv_bwd", grid=(nb, WKV_GROUPS, nc),
        in_specs=[rows(0), rows(2 * D), rows(0), rows(0), rows(0), rows(0), st, st, rows(0)],
        out_specs=[rows(0)] * 6,
        out_shape=[jax.ShapeDtypeStruct((t, D), BF16)] * 6,
        scratch_shapes=[pltpu.VMEM((WKV_HEADS, HEAD, HEAD), F32)],
        compiler_params=_cp("parallel", "parallel", "arbitrary"),
    )(z_rkv, z_rkv, lw, k2, kk, a, states, invs, dy)


def _softmax(s):
    e = jnp.exp(s - jnp.max(s, axis=-1, keepdims=True))
    return e * (1.0 / jnp.sum(e, axis=-1, keepdims=True))


ATT_HEADS = 8
ATT_COLS = ATT_HEADS * HEAD
ATT_GROUPS = N_HEADS // ATT_HEADS


def _attn_chunk(q, kb, vb, bias, valid):
    s = _each(lambda x, y, z: jnp.where(valid, _dot_nt(x * (HEAD ** -0.5), y) + z, MASK_VALUE), q, kb, bias)
    return _each(_dot, _each(_softmax, s), vb)


def _pad_fill(pad_ref, src_ref):
    pad_ref[0:LEFT, :] = jnp.zeros((LEFT, pad_ref.shape[1]), pad_ref.dtype)
    pad_ref[LEFT:, :] = src_ref[...].astype(pad_ref.dtype)


def _band_heads(pad_ref, start):
    return [pad_ref[pl.ds(start, BAND), h * HEAD:(h + 1) * HEAD].astype(F32) for h in range(ATT_HEADS)]


def _band_valid(c):
    return (c * CHUNK - LEFT + lax.broadcasted_iota(jnp.int32, (1, BAND), 1)) >= 0


def _bias_spec():
    return pl.BlockSpec((ATT_HEADS, CHUNK, BAND), lambda h, b, c: (h, 0, 0))


def _attn_fwd(proj, bias, seq):
    t = proj.shape[0]
    nb, nc = t // seq, seq // CHUNK
    cq = C_Q // ATT_COLS

    def body(q_ref, k_ref, v_ref, b_ref, o_ref, kpad, vpad):
        c = pl.program_id(2)

        @pl.when(c == 0)
        def _():
            _pad_fill(kpad, k_ref)
            _pad_fill(vpad, v_ref)

        start = pl.multiple_of(c * CHUNK, CHUNK)
        o = _attn_chunk(_head_cols(q_ref), _band_heads(kpad, start), _band_heads(vpad, start),
                        [b_ref[h] for h in range(ATT_HEADS)], _band_valid(c))
        for h in range(ATT_HEADS):
            o_ref[:, h * HEAD:(h + 1) * HEAD] = o[h].astype(o_ref.dtype)

    return pl.pallas_call(
        body, name="attn_fwd", grid=(ATT_GROUPS, nb, nc),
        in_specs=[pl.BlockSpec((CHUNK, ATT_COLS), lambda h, b, c: (b * nc + c, cq + h)),
                  pl.BlockSpec((seq, ATT_COLS), lambda h, b, c: (b, cq + ATT_GROUPS + h)),
                  pl.BlockSpec((seq, ATT_COLS), lambda h, b, c: (b, cq + 2 * ATT_GROUPS + h)),
                  _bias_spec()],
        out_specs=pl.BlockSpec((CHUNK, ATT_COLS), lambda h, b, c: (b * nc + c, h)),
        out_shape=jax.ShapeDtypeStruct((t, D), BF16),
        scratch_shapes=[pltpu.VMEM((seq + LEFT, ATT_COLS), BF16)] * 2,
        compiler_params=_cp("parallel", "arbitrary", "arbitrary"),
    )(proj, proj, proj, bias)


def _attn_bwd(proj, bias, do, seq):
    t = proj.shape[0]
    nb, nc = t // seq, seq // CHUNK
    cq = C_Q // ATT_COLS

    def body(q_ref, k_ref, v_ref, b_ref, do_ref, dq_ref, dk_ref, dv_ref, db_ref, kpad, vpad, dkpad, dvpad):
        b, c = pl.program_id(1), pl.program_id(2)

        @pl.when(c == 0)
        def _():
            _pad_fill(kpad, k_ref)
            _pad_fill(vpad, v_ref)
            dkpad[...] = jnp.zeros_like(dkpad)
            dvpad[...] = jnp.zeros_like(dvpad)

        @pl.when(jnp.logical_and(b == 0, c == 0))
        def _():
            db_ref[...] = jnp.zeros_like(db_ref)

        start = pl.multiple_of(c * CHUNK, CHUNK)
        _, vjp = jax.vjp(functools.partial(_attn_chunk, valid=_band_valid(c)),
                         _head_cols(q_ref), _band_heads(kpad, start), _band_heads(vpad, start),
                         [b_ref[h] for h in range(ATT_HEADS)])
        dq, dkb, dvb, dbias = vjp([x.astype(F32) for x in _head_cols(do_ref)])
        for h in range(ATT_HEADS):
            sl = slice(h * HEAD, (h + 1) * HEAD)
            dq_ref[:, sl] = dq[h].astype(dq_ref.dtype)
            dkpad[pl.ds(start, BAND), sl] += dkb[h].astype(F32)
            dvpad[pl.ds(start, BAND), sl] += dvb[h].astype(F32)
            db_ref[h] += dbias[h]

        @pl.when(c == nc - 1)
        def _():
            dk_ref[...] = dkpad[LEFT:, :].astype(dk_ref.dtype)
            dv_ref[...] = dvpad[LEFT:, :].astype(dv_ref.dtype)

    kv_out = pl.BlockSpec((seq, ATT_COLS), lambda h, b, c: (b, h))
    return pl.pallas_call(
        body, name="attn_bwd", grid=(ATT_GROUPS, nb, nc),
        in_specs=[pl.BlockSpec((CHUNK, ATT_COLS), lambda h, b, c: (b * nc + c, cq + h)),
                  pl.BlockSpec((seq, ATT_COLS), lambda h, b, c: (b, cq + ATT_GROUPS + h)),
                  pl.BlockSpec((seq, ATT_COLS), lambda h, b, c: (b, cq + 2 * ATT_GROUPS + h)),
                  _bias_spec(),
                  pl.BlockSpec((CHUNK, ATT_COLS), lambda h, b, c: (b * nc + c, h))],
        out_specs=[pl.BlockSpec((CHUNK, ATT_COLS), lambda h, b, c: (b * nc + c, h)), kv_out, kv_out,
                   pl.BlockSpec((ATT_HEADS, CHUNK, BAND), lambda h, b, c: (h, 0, 0))],
        out_shape=[jax.ShapeDtypeStruct((t, D), BF16)] * 3 + [jax.ShapeDtypeStruct((N_HEADS, CHUNK, BAND), F32)],
        scratch_shapes=[pltpu.VMEM((seq + LEFT, ATT_COLS), BF16)] * 2 + [pltpu.VMEM((seq + LEFT, ATT_COLS), F32)] * 2,
        compiler_params=_cp("parallel", "arbitrary", "arbitrary"),
    )(proj, proj, proj, bias, do)


def _xattn_tile(q, k, v):
    s = _dot_nt(q, k) * ((MEM_WIDTH // MEM_HEADS) ** -0.5)
    return _dot(_softmax(s), v)


def _xattn_fwd(qm, kvm, seq, n_mem, tq=1024):
    t = qm.shape[0]
    tq = min(tq, seq)
    nb, nq = t // seq, seq // tq

    def body(q_ref, k_ref, v_ref, o_ref):
        o_ref[...] = _xattn_tile(q_ref[...], k_ref[...], v_ref[...]).astype(o_ref.dtype)

    return pl.pallas_call(
        body, name="xattn_fwd", grid=(nb, MEM_HEADS, nq),
        in_specs=[pl.BlockSpec((tq, LANE), lambda b, h, i: (b * nq + i, h)),
                  pl.BlockSpec((n_mem, LANE), lambda b, h, i: (b, h)),
                  pl.BlockSpec((n_mem, LANE), lambda b, h, i: (b, MEM_HEADS + h))],
        out_specs=pl.BlockSpec((tq, LANE), lambda b, h, i: (b * nq + i, h)),
        out_shape=jax.ShapeDtypeStruct((t, MEM_WIDTH), BF16),
        compiler_params=_cp("parallel", "parallel", "parallel"),
    )(qm, kvm, kvm)


def _xattn_bwd(qm, kvm, do, seq, n_mem, tq=1024):
    t = qm.shape[0]
    tq = min(tq, seq)
    nb, nq = t // seq, seq // tq

    def body(q_ref, k_ref, v_ref, do_ref, dq_ref, dkv_ref, dk_acc, dv_acc):
        i = pl.program_id(2)

        @pl.when(i == 0)
        def _():
            dk_acc[...] = jnp.zeros_like(dk_acc)
            dv_acc[...] = jnp.zeros_like(dv_acc)

        _, vjp = jax.vjp(_xattn_tile, q_ref[...], k_ref[...], v_ref[...])
        dq, dk, dv = vjp(do_ref[...].astype(F32))
        dq_ref[...] = dq.astype(dq_ref.dtype)
        dk_acc[...] += dk
        dv_acc[...] += dv

        @pl.when(i == nq - 1)
        def _():
            dkv_ref[0] = dk_acc[...].astype(dkv_ref.dtype)
            dkv_ref[1] = dv_acc[...].astype(dkv_ref.dtype)

    dq, dkv = pl.pallas_call(
        body, name="xattn_bwd", grid=(nb, MEM_HEADS, nq),
        in_specs=[pl.BlockSpec((tq, LANE), lambda b, h, i: (b * nq + i, h)),
                  pl.BlockSpec((n_mem, LANE), lambda b, h, i: (b, h)),
                  pl.BlockSpec((n_mem, LANE), lambda b, h, i: (b, MEM_HEADS + h)),
                  pl.BlockSpec((tq, LANE), lambda b, h, i: (b * nq + i, h))],
        out_specs=[pl.BlockSpec((tq, LANE), lambda b, h, i: (b * nq + i, h)),
                   pl.BlockSpec((2, n_mem, LANE), lambda b, h, i: (0, b, h))],
        out_shape=[jax.ShapeDtypeStruct((t, MEM_WIDTH), BF16), jax.ShapeDtypeStruct((2, nb * n_mem, MEM_WIDTH), BF16)],
        scratch_shapes=[pltpu.VMEM((n_mem, LANE), F32)] * 2,
        compiler_params=_cp("parallel", "parallel", "arbitrary"),
    )(qm, kvm, kvm, do)
    return dq, jnp.concatenate([dkv[0], dkv[1]], axis=1)


def _loss_head(x, u, g_post, target, tm=512):
    t, d = x.shape
    tm = min(tm, t)

    def tile_loss(xv, uv, gv, tv):
        diff = _fn_res(xv, uv, gv)[0] - tv
        return 0.5 * jnp.sum(jnp.mean(diff * diff, axis=-1, keepdims=True), axis=0, keepdims=True)

    def body(x_ref, u_ref, g_ref, t_ref, l_ref, dx_ref, du_ref, dg_ref):
        @pl.when(pl.program_id(0) == 0)
        def _():
            l_ref[...] = jnp.zeros_like(l_ref)
            dg_ref[...] = jnp.zeros_like(dg_ref)

        tv = t_ref[...]
        part, vjp = jax.vjp(lambda xv, uv, gv: tile_loss(xv, uv, gv, tv), x_ref[...], u_ref[...], g_ref[...])
        dx, du, dg = vjp(jnp.ones((1, 1), F32))
        l_ref[...] += part
        dx_ref[...] = dx
        du_ref[...] = du.astype(du_ref.dtype)
        dg_ref[...] += dg

    rows = pl.BlockSpec((tm, d), lambda i: (i, 0))
    vec = pl.BlockSpec((1, d), lambda i: (0, 0))
    return pl.pallas_call(
        body, name="loss_head", grid=(t // tm,),
        in_specs=[rows, rows, vec, rows],
        out_specs=[pl.BlockSpec((8, LANE), lambda i: (0, 0)), rows, rows, vec],
        out_shape=[jax.ShapeDtypeStruct((8, LANE), F32), jax.ShapeDtypeStruct((t, d), F32),
                   jax.ShapeDtypeStruct((t, d), BF16), jax.ShapeDtypeStruct((1, d), F32)],
        compiler_params=_cp("arbitrary"),
    )(x, u, g_post, target)


def _mesh_pos():
    return lax.axis_index("x"), lax.axis_index("y"), lax.axis_index("c")


def _peer(pos, d):
    x, y, c = pos
    return ((1 - x) if d & 4 else x, (1 - y) if d & 2 else y, (1 - c) if d & 1 else c)


def _flat(pos):
    return 4 * pos[0] + 2 * pos[1] + pos[2]


def _exchange(arrays, scatter, *, name):
    n = len(arrays)
    shapes = [a.shape[1:] if scatter else a.shape for a in arrays]

    def body(*refs):
        ins, outs = refs[:n], refs[n:2 * n]
        send, recv, loc = refs[2 * n:]
        pos = _mesh_pos()
        me = _flat(pos)
        pending = []
        for i in range(n):
            own = pltpu.make_async_copy(ins[i].at[me] if scatter else ins[i], outs[i].at[me], loc.at[i])
            own.start()
            pending.append(own)
            for d in range(1, N_DEV):
                peer = _peer(pos, d)
                src = ins[i].at[_flat(peer)] if scatter else ins[i]
                out_cp = pltpu.make_async_remote_copy(
                    src_ref=src, dst_ref=outs[i].at[me], send_sem=send.at[i, d - 1], recv_sem=recv.at[i, d - 1],
                    device_id=peer, device_id_type=pl.DeviceIdType.MESH)
                out_cp.start()
                pending.append(out_cp)
        for i in range(n):
            own = pending[i * N_DEV]
            for d in range(1, N_DEV):
                peer = _peer(pos, d)
                src = ins[i].at[_flat(peer)] if scatter else ins[i]
                pending[i * N_DEV + d].wait_send()
                pltpu.make_async_remote_copy(
                    src_ref=src, dst_ref=outs[i].at[_flat(peer)], send_sem=send.at[i, d - 1], recv_sem=recv.at[i, d - 1],
                    device_id=peer, device_id_type=pl.DeviceIdType.MESH).wait_recv()
            own.wait()

    hbm = pl.BlockSpec(memory_space=pltpu.HBM)
    return pl.pallas_call(
        body, name=name,
        in_specs=[hbm] * n, out_specs=[hbm] * n,
        out_shape=[jax.ShapeDtypeStruct((N_DEV,) + tuple(s), a.dtype) for s, a in zip(shapes, arrays)],
        scratch_shapes=[pltpu.SemaphoreType.DMA((n, N_DEV - 1)), pltpu.SemaphoreType.DMA((n, N_DEV - 1)),
                        pltpu.SemaphoreType.DMA((n,))],
    )(*arrays)


_HBM = pl.BlockSpec(memory_space=pltpu.HBM)
_SEM = pl.BlockSpec(memory_space=pltpu.SEMAPHORE)
_DATAFLOW = pltpu.SideEffectType.DATAFLOW_SIDE_EFFECTING


_ALL_PEERS = tuple(range(1, N_DEV))
_SIBLING_AND_SAME_CORE = (1, 2, 4, 6)


def _remote_copies(ins, lands, send, recv, scatter, dists):
    pos = _mesh_pos()
    me = _flat(pos)
    out = []
    for i in range(len(ins)):
        for j, d in enumerate(dists):
            peer = _peer(pos, d)
            src = ins[i].at[_flat(peer)] if scatter else ins[i]
            pair = i * len(dists) + j
            sems = dict(send_sem=send.at[pair], recv_sem=recv.at[pair], device_id=peer,
                        device_id_type=pl.DeviceIdType.MESH)
            out.append((pltpu.make_async_remote_copy(src_ref=src, dst_ref=lands[i].at[me], **sems),
                        pltpu.make_async_remote_copy(src_ref=src, dst_ref=lands[i].at[_flat(peer)], **sems)))
    return out


def _exchange_start(arrays, scatter, after, *, name, dists=_ALL_PEERS):
    n = len(arrays)
    shapes = [a.shape[1:] if scatter else a.shape for a in arrays]
    lands = [pltpu.with_memory_space_constraint(lax.empty((N_DEV,) + tuple(s), a.dtype), pltpu.HBM)
             for s, a in zip(shapes, arrays)]
    srcs = [pltpu.with_memory_space_constraint(a, pltpu.HBM) for a in arrays]

    def body(*refs):
        ins, land_refs = refs[:n], refs[n:2 * n]
        send, recv, token = refs[2 * n + 1], refs[2 * n + 2], refs[-1]
        for going, _ in _remote_copies(ins, land_refs, send, recv, scatter, dists):
            going.start()
        token[...] = jnp.zeros_like(token)

    sems = pltpu.SemaphoreType.DMA((n * len(dists),))
    res = pl.pallas_call(
        body, name=name,
        out_shape=(sems, sems, *[pltpu.HBM(a.shape, a.dtype) for a in srcs + lands], jax.ShapeDtypeStruct((8, LANE), F32)),
        in_specs=[_HBM] * (2 * n) + [pl.BlockSpec(memory_space=pl.ANY)],
        out_specs=(_SEM, _SEM, *[_HBM] * (2 * n), pl.BlockSpec(memory_space=pltpu.VMEM)),
        input_output_aliases={i: 2 + i for i in range(2 * n)},
        compiler_params=pltpu.CompilerParams(has_side_effects=_DATAFLOW),
    )(*srcs, *lands, after)
    return (n, scatter, dists, res[0], res[1], list(res[2:2 + 2 * n])), res[-1]


def _exchange_wait(handle, after, own, *, name):
    n, scatter, dists, send, recv, thru = handle

    def body(*refs):
        ins, land_refs = refs[:n], refs[n:2 * n]
        for going, coming in _remote_copies(ins, land_refs, refs[2 * n], refs[2 * n + 1], scatter, dists):
            going.wait_send()
            coming.wait_recv()

    res = pl.pallas_call(
        body, name=name,
        out_shape=tuple(pltpu.HBM(a.shape, a.dtype) for a in thru),
        in_specs=[_HBM] * (2 * n) + [_SEM, _SEM] + [pl.BlockSpec(memory_space=pl.ANY)] * len(after),
        out_specs=tuple([_HBM] * (2 * n)),
        input_output_aliases={i: i for i in range(2 * n)},
        compiler_params=pltpu.CompilerParams(has_side_effects=_DATAFLOW),
    )(*thru, send, recv, *after)
    me = _flat(_mesh_pos())
    return [lax.dynamic_update_slice_in_dim(land, o[None].astype(land.dtype), me, 0) for land, o in zip(res[n:], own)]


_OTHER_CHIPS = (2, 4, 6)


def _relay_to_sibling(gathered, *, name):
    n, k = len(gathered), len(_OTHER_CHIPS)

    def body(*refs):
        ins, outs = refs[:n], refs[n:2 * n]
        send, recv = refs[2 * n:]
        pos = _mesh_pos()
        copies = []
        for i in range(n):
            for j, d in enumerate(_OTHER_CHIPS):
                cp = pltpu.make_async_remote_copy(
                    src_ref=ins[i].at[_flat(_peer(pos, d))], dst_ref=outs[i].at[j],
                    send_sem=send.at[i * k + j], recv_sem=recv.at[i * k + j],
                    device_id=_peer(pos, 1), device_id_type=pl.DeviceIdType.MESH)
                cp.start()
                copies.append(cp)
        for cp in copies:
            cp.wait()

    return pl.pallas_call(
        body, name=name, in_specs=[_HBM] * n, out_specs=[_HBM] * n,
        out_shape=[jax.ShapeDtypeStruct((k,) + g.shape[1:], g.dtype) for g in gathered],
        scratch_shapes=[pltpu.SemaphoreType.DMA((n * k,)), pltpu.SemaphoreType.DMA((n * k,))],
    )(*gathered)


def _adamw(parts, w, m, v, *, name, tr=128, after=None):
    r, c = w.shape
    align = 8 * 4 // parts.dtype.itemsize
    row_tiles = [d for d in range(align, min(tr, r) + 1, align) if r % d == 0]
    tr, tc = (max(row_tiles), c) if row_tiles else (r, LANE)
    assert c % tc == 0
    n_after = 0 if after is None else 1

    def body(p_ref, w_ref, m_ref, v_ref, *rest):
        g_ref, d_ref, nm_ref, nv_ref = rest[n_after:]
        g = p_ref[0].astype(F32)
        for j in range(1, N_DEV):
            g = g + p_ref[j].astype(F32)
        m2 = ADAM_B1 * m_ref[...] + (1.0 - ADAM_B1) * g
        v2 = ADAM_B2 * v_ref[...] + (1.0 - ADAM_B2) * (g * g)
        m_hat = m2 / (1.0 - ADAM_B1 ** ADAM_STEP)
        v_hat = v2 / (1.0 - ADAM_B2 ** ADAM_STEP)
        g_ref[...] = g
        d_ref[...] = -ADAM_LR * (m_hat / (jnp.sqrt(v_hat) + ADAM_EPS) + ADAM_WD * w_ref[...])
        nm_ref[...] = m2
        nv_ref[...] = v2

    spec = pl.BlockSpec((tr, tc), lambda i, j: (i, j))
    return pl.pallas_call(
        body, name=name, grid=(r // tr, c // tc),
        in_specs=[pl.BlockSpec((N_DEV, tr, tc), lambda i, j: (0, i, j)), spec, spec, spec]
        + [pl.BlockSpec(memory_space=pl.ANY)] * n_after,
        out_specs=[spec] * 4, out_shape=[jax.ShapeDtypeStruct((r, c), F32)] * 4,
        compiler_params=_cp("parallel", "parallel"),
    )(parts, w, m, v, *([] if after is None else [after]))


def _cols_to_full(g):
    return jnp.transpose(g, (1, 0, 2)).reshape(g.shape[1], N_DEV * g.shape[2])


def _full_to_cols(w):
    r, c = w.shape
    return jnp.transpose(w.reshape(r, N_DEV, c // N_DEV), (1, 0, 2))


def _cut(a, lo, hi, axis):
    return lax.slice_in_dim(a, lo, hi, axis=axis)


def _pad_to(a, size, axis):
    pads = [(0, 0)] * a.ndim
    pads[axis] = (0, size - a.shape[axis])
    return jnp.pad(a, pads)


def _pad_lora(w, axis=1):
    return jnp.concatenate([
        _pad_to(_cut(w, 0, LORA_W, axis), 128, axis), _pad_to(_cut(w, LORA_W, LORA_W + LORA_A, axis), 128, axis),
        _pad_to(_cut(w, LORA_W + LORA_A, w.shape[axis], axis), 256, axis)], axis=axis)


def _unpad_lora(wp, axis=1):
    return jnp.concatenate([_cut(wp, 0, LORA_W, axis), _cut(wp, 128, 128 + LORA_A, axis),
                            _cut(wp, 256, 256 + LORA_G, axis)], axis=axis)


def _permute_in(w, axis):
    rk = 3 * D
    lo = rk + LORA_W + LORA_A + LORA_G
    return jnp.concatenate([_cut(w, 0, rk, axis), _cut(w, lo, w.shape[axis], axis), _pad_lora(_cut(w, rk, lo, axis), axis)],
                           axis=axis)


def _unpermute_in(wp, axis):
    return jnp.concatenate([_cut(wp, 0, 3 * D, axis), _unpad_lora(_cut(wp, C_LORA, P_WIDTH, axis), axis),
                            _cut(wp, 3 * D, C_LORA, axis)], axis=axis)


def _rel_index():
    dist = jnp.arange(CHUNK)[:, None] - jnp.arange(BAND)[None, :] + LEFT
    return (jnp.minimum(dist, REL_CLIP) + (CHUNK - 1)).reshape(-1)


def _local_step(x, mem, target, wt, seq, n_mem, comm):
    t = x.shape[0]
    row = lambda a: a.reshape(1, -1).astype(F32)
    g_pre_mix, g_post_mix = row(wt["g_pre_mix"]), row(wt["g_post_mix"])
    g_pre_cross, g_post_cross, g_mem = row(wt["g_pre_cross"]), row(wt["g_post_cross"]), row(wt["g_mem"])
    g_pre_ffn, g_post_ffn = row(wt["g_pre_ffn"]), row(wt["g_post_ffn"])
    mix = row(wt["shift_mix"])
    mix_rkv, mix_lora = mix[:, :3 * D], _pad_lora(mix[:, 3 * D:])
    decay_base, iclr_base = row(wt["decay_base"]), row(wt["iclr_base"])
    kns, kis = row(wt["key_norm_scale"]), row(wt["key_iclr_scale"])
    lnx_w, lnx_b, bonus = row(wt["lnx_w"]), row(wt["lnx_b"]), row(wt["bonus_scale"])
    e_dh = (jnp.arange(D)[:, None] // HEAD == jnp.arange(N_HEADS)[None, :]).astype(F32)
    e_hd = e_dh.T
    onehot = (jnp.arange(REL_TABLE)[:, None] == _rel_index()[None, :]).astype(BF16)

    begun = comm.begun
    (h1,) = _rowwise(_fn_pre, [_win(x)], [g_pre_mix], [(D, BF16)], name="pre_mix", tm=512, after=begun)
    (mn,) = _rowwise(_fn_pre, [_win(mem)], [g_mem], [(D, BF16)], name="pre_mem", tm=512, after=begun)
    bias = _mm(wt["rel_bias"].astype(F32), onehot, name="mm_bias", split_a=3, after=begun).reshape(N_HEADS, CHUNK, BAND)
    wt = {**wt, **comm.first_weights([h1, mn, bias])}
    w_in = wt["w_in_p"]
    d_up = jnp.pad(wt["decay_up"].astype(F32), ((0, 128 - LORA_W), (0, 0)))
    i_up = jnp.pad(wt["iclr_up"].astype(F32), ((0, 128 - LORA_A), (0, 0)))
    g_up = jnp.pad(wt["gate_up"].astype(F32), ((0, 256 - LORA_G), (0, 0)))
    proj = _mm(h1, w_in, tb=True, name="mm_in", after=comm.first_token)
    z_rkv = _shift_fwd(proj, 0, 3 * D, mix_rkv, seq, name="shift_rkv")
    z_lora = _shift_fwd(proj, C_LORA, 512, mix_lora, seq, name="shift_lora")
    prep_rows = [_win(z_rkv, D, D), _win(z_lora, 0, 128), _win(z_lora, 128, 128), _win(z_lora, 256, 256)]
    prep_params = [decay_base, d_up, iclr_base, i_up, g_up, kns, kis, e_hd, e_dh]
    lw, k2, kk, a, g = _rowwise(_fn_prep, prep_rows, prep_params, [(D, F32)] * 5, name="rwkv_prep", tm=256)
    y, states, invs = _wkv_fwd(z_rkv, lw, k2, kk, a, seq)
    post_rows = [_win(y), _win(z_rkv, 0, D), _win(k2), _win(z_rkv, 2 * D, D), _win(g)]
    post_params = [lnx_w, lnx_b, bonus, e_hd, e_dh]
    (y_a,) = _rowwise(_fn_post, post_rows, post_params, [(D, BF16)], name="rwkv_post", tm=256)
    y_b = _attn_fwd(proj, bias, seq)
    wt = {**wt, **comm.late_weights(y_b)}
    ya_p = _mm(y_a, wt["w_branch_a"], name="mm_a")
    yb_p = _mm(y_b, wt["w_branch_b"], name="mm_b")
    mix_rows = [_win(proj, C_GA, D), _win(proj, C_GA + D, D), _win(ya_p), _win(yb_p)]
    (mixed,) = _rowwise(_fn_mix, mix_rows, [], [(D, BF16)], name="gate_mix", tm=512)
    mo = _mm(mixed, wt["w_out"], name="mm_out")
    x1, h2 = _rowwise(_fn_res_pre, [_win(x), _win(mo)], [g_post_mix, g_pre_cross], [(D, F32), (D, BF16)],
                      name="res_mix", tm=512)
    qm = _mm(h2, wt["w_q_mem"], name="mm_q")
    kvm = _mm(mn, wt["w_kv_mem"], name="mm_kv")
    om = _xattn_fwd(qm, kvm, seq, n_mem)
    co = _mm(om, wt["w_o_mem"], name="mm_o")
    x2, h3 = _rowwise(_fn_res_pre, [_win(x1), _win(co)], [g_post_cross, g_pre_ffn], [(D, F32), (D, BF16)],
                      name="res_cross", tm=512)
    gu = _mm(h3, wt["w_ffn_in"], tb=True, name="mm_ffn_in", out_dtype=BF16)
    (act,) = _rowwise(_fn_swiglu, [_win(gu, 0, FFN), _win(gu, FFN, FFN)], [], [(FFN, BF16)], name="swiglu", tm=256)
    ff = _mm(act, wt["w_ffn_out"], name="mm_ffn_out")

    gw = {}
    loss, dx2, dff, gw["g_post_ffn"] = _loss_head(x2, ff, g_post_ffn, target)
    dact = _mm(dff, wt["w_ffn_out"], tb=True, name="mm_ffn_out_dx", out_dtype=BF16)
    gw["w_ffn_out"] = _mm(act, dff, ta=True, name="mm_ffn_out_dw", out_dtype=BF16)
    (dgu,), _ = _rowwise_bwd(_fn_swiglu, [_win(gu, 0, FFN), _win(gu, FFN, FFN)], [], 0, [[dact]],
                             name="swiglu_bwd", tm=256, row_grad=[BF16, BF16], packed=True)
    dh3 = _mm(dgu, wt["w_ffn_in"], name="mm_ffn_in_dx", out_dtype=BF16)
    gw["w_ffn_in"] = _mm(dgu, h3, ta=True, name="mm_ffn_in_dw", out_dtype=BF16)
    (dx1, dco), (gw["g_post_cross"], gw["g_pre_ffn"]) = _rowwise_bwd(
        _fn_res_pre, [_win(x1), _win(co)], [g_post_cross, g_pre_ffn], 0, [[dx2], [dh3]],
        name="res_cross_bwd", tm=512, row_grad=[F32, BF16])
    dom = _mm(dco, wt["w_o_mem"], tb=True, name="mm_o_dx", out_dtype=BF16)
    gw["w_o_mem"] = _mm(om, dco, ta=True, name="mm_o_dw", out_dtype=BF16)
    dqm, dkvm = _xattn_bwd(qm, kvm, dom, seq, n_mem)
    dh2 = _mm(dqm, wt["w_q_mem"], tb=True, name="mm_q_dx", out_dtype=BF16)
    gw["w_q_mem"] = _mm(h2, dqm, ta=True, name="mm_q_dw", out_dtype=BF16)
    dmn = _mm(dkvm, wt["w_kv_mem"], tb=True, name="mm_kv_dx", out_dtype=BF16)
    gw["w_kv_mem"] = _mm(mn, dkvm, ta=True, name="mm_kv_dw", out_dtype=BF16)
    _, (gw["g_mem"],) = _rowwise_bwd(_fn_pre, [_win(mem)], [g_mem], 0, [[dmn]], name="pre_mem_bwd", tm=256,
                                     row_grad=[None])
    (dx0, dmo), (gw["g_post_mix"], gw["g_pre_cross"]) = _rowwise_bwd(
        _fn_res_pre, [_win(x), _win(mo)], [g_post_mix, g_pre_cross], 0, [[dx1], [dh2]],
        name="res_mix_bwd", tm=512, row_grad=[F32, BF16])
    dmixed = _mm(dmo, wt["w_out"], tb=True, name="mm_out_dx", out_dtype=BF16)
    gw["w_out"] = _mm(mixed, dmo, ta=True, name="mm_out_dw", out_dtype=BF16)
    (dzga, dzgb, dya_p, dyb_p), _ = _rowwise_bwd(_fn_mix, mix_rows, [], 0, [[dmixed]], name="gate_mix_bwd", tm=512,
                                                 row_grad=[BF16] * 4)
    gw["w_branch_a"] = _mm(y_a, dya_p, ta=True, name="mm_a_dw", out_dtype=BF16)
    gw["w_branch_b"] = _mm(y_b, dyb_p, ta=True, name="mm_b_dw", out_dtype=BF16)
    token = comm.send_early(gw)
    dy_a = _mm(dya_p, wt["w_branch_a"], tb=True, name="mm_a_dx", out_dtype=BF16, after=token)
    dy_b = _mm(dyb_p, wt["w_branch_b"], tb=True, name="mm_b_dx", out_dtype=BF16, after=token)
    dq, dk, dv, dbias = _attn_bwd(proj, bias, dy_b, seq)
    gw["rel_bias"] = _mm(dbias.reshape(N_HEADS, CHUNK * BAND), onehot, tb=True, name="mm_bias_dw", split_a=2)
    (dy, dr_p, dk2_p, dv_p, dg), (gw["lnx_w"], gw["lnx_b"], gw["bonus_scale"]) = _rowwise_bwd(
        _fn_post, post_rows, post_params, 2, [[dy_a]], name="rwkv_post_bwd", tm=512, row_grad=[BF16] * 5)
    dr_s, dlw, dk2_s, dv_s, dkk, da = _wkv_bwd(z_rkv, lw, k2, kk, a, states, invs, dy, seq)
    (dzk, dzw, dza, dzg), pg = _rowwise_bwd(
        _fn_prep, prep_rows, prep_params, 2, [[dlw], [dk2_p, dk2_s], [dkk], [da], [dg]],
        name="rwkv_prep_bwd", tm=512, row_grad=[BF16] * 4)
    gw["decay_base"], gd_up, gw["iclr_base"], gi_up, gg_up, gw["key_norm_scale"], gw["key_iclr_scale"] = pg
    gw["decay_up"], gw["iclr_up"], gw["gate_up"] = gd_up[:LORA_W], gi_up[:LORA_A], gg_up[:LORA_G]
    dp_r, gmix_r = _shift_bwd(proj, 0, D, mix_rkv[:, :D], [dr_p, dr_s], seq, name="shift_r_bwd")
    dp_k, gmix_k = _shift_bwd(proj, D, D, mix_rkv[:, D:2 * D], [dzk], seq, name="shift_k_bwd")
    dp_v, gmix_v = _shift_bwd(proj, 2 * D, D, mix_rkv[:, 2 * D:], [dv_p, dv_s], seq, name="shift_v_bwd")
    dp_lora, gmix_lora = _shift_bwd(proj, C_LORA, 512, mix_lora, [jnp.concatenate([dzw, dza, dzg], axis=1)], seq,
                                    name="shift_lora_bwd")
    gw["shift_mix"] = jnp.concatenate([gmix_r, gmix_k, gmix_v, _unpad_lora(gmix_lora)], axis=1)
    dproj = [dp_r, dp_k, dp_v, dq, dk, dv, dzga, dzgb, dp_lora]
    gw["w_in_p"] = _mm_cat_tn(dproj, h1, name="mm_in_dw", after=gw["rel_bias"])
    token = comm.send_late(gw)
    dh1 = _mm_cat_nn(dproj, w_in, name="mm_in_dx", after=token)
    (grad_x,), (gw["g_pre_mix"],) = _rowwise_bwd(_fn_pre, [_win(x)], [g_pre_mix], 0, [[dh1]], name="pre_mix_bwd",
                                                 tm=512, row_grad=[F32], add_to={0: dx0})
    return loss, grad_x, gw


_COL_SHARDED = ("w_in", "decay_up", "iclr_up", "gate_up", "w_o_mem", "w_ffn_in")
_ROW_SHARDED = ("w_branch_a", "w_branch_b", "w_out", "w_q_mem", "w_kv_mem", "w_ffn_out")
_TRANSPOSED = ("w_in", "w_ffn_in")
_FIRST = ("w_in", "decay_up", "iclr_up", "gate_up")
_REST = ("w_o_mem", "w_ffn_in", "w_branch_a", "w_branch_b", "w_out", "w_q_mem", "w_kv_mem", "w_ffn_out")
_REPLICATED = ("g_pre_mix", "g_post_mix", "shift_mix", "decay_base", "iclr_base", "key_norm_scale", "key_iclr_scale",
               "bonus_scale", "lnx_w", "lnx_b", "rel_bias", "g_pre_cross", "g_post_cross", "g_mem", "g_pre_ffn",
               "g_post_ffn")
_WEIGHTS = ("g_pre_mix", "g_post_mix", "w_in", "shift_mix", "decay_base", "decay_up", "iclr_base", "iclr_up", "gate_up",
            "key_norm_scale", "key_iclr_scale", "bonus_scale", "lnx_w", "lnx_b", "rel_bias", "w_branch_a", "w_branch_b",
            "w_out", "g_pre_cross", "g_post_cross", "g_mem", "w_q_mem", "w_kv_mem", "w_o_mem", "g_pre_ffn", "g_post_ffn",
            "w_ffn_in", "w_ffn_out")
_PACK_ROWS = 8 * ((sum({"shift_mix": 3360, "bonus_scale": 1024, "rel_bias": 3072}.get(n, D) for n in _REPLICATED)
                   + 1 + 8 * LANE - 1) // (8 * LANE))


def _pack(vals):
    flat = jnp.concatenate([v.reshape(-1).astype(F32) for v in vals])
    return jnp.pad(flat, (0, _PACK_ROWS * LANE - flat.shape[0])).reshape(_PACK_ROWS, LANE)


def _unpack(packed, shapes):
    flat, out, pos = packed.reshape(-1), [], 0
    for s in shapes:
        n = math.prod(s)
        out.append(flat[pos:pos + n].reshape(s))
        pos += n
    return out


def _step(args, seq, n_mem):
    names = ("x", "mem") + _WEIGHTS + ("loss_target",) + tuple("m_" + n for n in _WEIGHTS) + tuple("v_" + n for n in _WEIGHTS)
    given = dict(zip(names, args))
    nb = given["x"].shape[0]
    x = given["x"].reshape(nb * seq, D)
    mem = given["mem"].reshape(nb * n_mem, D)
    target = given["loss_target"].reshape(nb * seq, D)
    def local(name, prefix=""):
        a = given[prefix + name][0]
        return a.T if name in _TRANSPOSED else a

    shard = {n: local(n) for n in _COL_SHARDED + _ROW_SHARDED}
    stacked = _ROW_SHARDED + _TRANSPOSED
    out = {}

    def wire(name):
        return shard[name].astype(BF16)

    def full(name, g):
        return g.reshape(-1, g.shape[-1]) if name in stacked else _cols_to_full(g)

    def blocks_of(name, g):
        return (g.reshape((N_DEV,) + shard[name].shape) if name in stacked else _full_to_cols(g)).astype(BF16)

    def update(names, landed, after=None):
        done = []
        for n, parts in zip(names, landed):
            res = _adamw(parts, shard[n], local(n, "m_"), local(n, "v_"), name="adamw_" + n, after=after)
            for kind, r in zip(("grad_", "delta_", "new_m_", "new_v_"), res):
                out[kind + n] = (r.T if n in _TRANSPOSED else r)[None]
            done.append(res[0])
        return done


    class Exchanges:
        def __init__(self):
            srcs = [wire(n) for n in _FIRST]
            self.first, self.begun = _exchange_start(srcs, False, srcs[0], name="gather_first_start",
                                                     dists=_SIBLING_AND_SAME_CORE)

        def first_weights(self, after):
            got = _exchange_wait(self.first, after, [wire(n) for n in _FIRST], name="gather_first_wait")
            relayed = _relay_to_sibling(got, name="gather_first_relay")
            pos = _mesh_pos()
            for j, d in enumerate(_OTHER_CHIPS):
                slot = _flat(_peer(pos, d | 1))
                got = [lax.dynamic_update_slice_in_dim(g, r[j][None], slot, 0) for g, r in zip(got, relayed)]
            self.rest, self.first_token = _exchange_start(
                [wire(n) for n in _REST], False, got[0], name="gather_rest_start")
            first = {n: full(n, g) for n, g in zip(_FIRST, got)}
            first["w_in_p"] = _permute_in(first.pop("w_in"), 0)
            return first

        def late_weights(self, after):
            got = _exchange_wait(self.rest, [after], [wire(n) for n in _REST], name="gather_rest_wait")
            return {n: full(n, g) for n, g in zip(_REST, got)}

        def send_early(self, gw):
            self.early_blocks = [blocks_of(n, gw[n]) for n in _REST]
            self.early, token = _exchange_start(self.early_blocks, True, self.early_blocks[-1], name="scatter_rest_start")
            return token

        def send_late(self, gw):
            me = _flat(_mesh_pos())
            own = [lax.dynamic_index_in_dim(b, me, 0, keepdims=False) for b in self.early_blocks]
            landed = _exchange_wait(self.early, [gw["w_in_p"]], own, name="scatter_rest_wait")
            grads = {**gw, "w_in": _unpermute_in(gw["w_in_p"], 0)}
            self.late_blocks = [blocks_of(n, grads[n]) for n in _FIRST]
            self.late, token = _exchange_start(self.late_blocks, True, landed[0], name="scatter_first_start")
            self.updated = update(_REST, landed, after=token)
            return token

        def finish(self, after):
            me = _flat(_mesh_pos())
            own = [lax.dynamic_index_in_dim(b, me, 0, keepdims=False) for b in self.late_blocks]
            update(_FIRST, _exchange_wait(self.late, [*after, *self.updated], own, name="scatter_first_wait"))

    comm = Exchanges()
    wt = {n: given[n][0] for n in _REPLICATED}
    loss_tile, grad_x, gw = _local_step(x, mem, target, wt, seq, n_mem, comm)
    rep_shapes = [given[n].shape for n in _REPLICATED]
    packed, _ = lax.optimization_barrier((_pack([gw[n] for n in _REPLICATED] + [loss_tile[0, 0]]), tuple(comm.updated)))
    small = _exchange([packed], False, name="gather_small")[0]
    zero = jnp.zeros((), F32)
    res = _adamw(small, *[_pack([given[p + n] for n in _REPLICATED] + [zero]) for p in ("", "m_", "v_")],
                 name="adamw_small", tr=_PACK_ROWS)
    for kind, r in zip(("grad_", "delta_", "new_m_", "new_v_"), res):
        for n, val in zip(_REPLICATED, _unpack(r, rep_shapes)):
            out[kind + n] = val
    loss = res[0].reshape(-1)[sum(math.prod(s) for s in rep_shapes)]
    comm.finish([grad_x, res[0]])
    grad_x = grad_x.reshape(nb, seq, D)
    return (loss, grad_x, *[out[k + n] for k in ("grad_", "delta_", "new_m_", "new_v_") for n in _WEIGHTS])


def kernel(x, mem, g_pre_mix, g_post_mix, w_in, shift_mix, decay_base, decay_up, iclr_base, iclr_up, gate_up, key_norm_scale, key_iclr_scale, bonus_scale, lnx_w, lnx_b, rel_bias, w_branch_a, w_branch_b, w_out, g_pre_cross, g_post_cross, g_mem, w_q_mem, w_kv_mem, w_o_mem, g_pre_ffn, g_post_ffn, w_ffn_in, w_ffn_out, loss_target, m_g_pre_mix, m_g_post_mix, m_w_in, m_shift_mix, m_decay_base, m_decay_up, m_iclr_base, m_iclr_up, m_gate_up, m_key_norm_scale, m_key_iclr_scale, m_bonus_scale, m_lnx_w, m_lnx_b, m_rel_bias, m_w_branch_a, m_w_branch_b, m_w_out, m_g_pre_cross, m_g_post_cross, m_g_mem, m_w_q_mem, m_w_kv_mem, m_w_o_mem, m_g_pre_ffn, m_g_post_ffn, m_w_ffn_in, m_w_ffn_out, v_g_pre_mix, v_g_post_mix, v_w_in, v_shift_mix, v_decay_base, v_decay_up, v_iclr_base, v_iclr_up, v_gate_up, v_key_norm_scale, v_key_iclr_scale, v_bonus_scale, v_lnx_w, v_lnx_b, v_rel_bias, v_w_branch_a, v_w_branch_b, v_w_out, v_g_pre_cross, v_g_post_cross, v_g_mem, v_w_q_mem, v_w_kv_mem, v_w_o_mem, v_g_pre_ffn, v_g_post_ffn, v_w_ffn_in, v_w_ffn_out):
    args = (x, mem, g_pre_mix, g_post_mix, w_in, shift_mix, decay_base, decay_up, iclr_base, iclr_up, gate_up, key_norm_scale, key_iclr_scale, bonus_scale, lnx_w, lnx_b, rel_bias, w_branch_a, w_branch_b, w_out, g_pre_cross, g_post_cross, g_mem, w_q_mem, w_kv_mem, w_o_mem, g_pre_ffn, g_post_ffn, w_ffn_in, w_ffn_out, loss_target, m_g_pre_mix, m_g_post_mix, m_w_in, m_shift_mix, m_decay_base, m_decay_up, m_iclr_base, m_iclr_up, m_gate_up, m_key_norm_scale, m_key_iclr_scale, m_bonus_scale, m_lnx_w, m_lnx_b, m_rel_bias, m_w_branch_a, m_w_branch_b, m_w_out, m_g_pre_cross, m_g_post_cross, m_g_mem, m_w_q_mem, m_w_kv_mem, m_w_o_mem, m_g_pre_ffn, m_g_post_ffn, m_w_ffn_in, m_w_ffn_out, v_g_pre_mix, v_g_post_mix, v_w_in, v_shift_mix, v_decay_base, v_decay_up, v_iclr_base, v_iclr_up, v_gate_up, v_key_norm_scale, v_key_iclr_scale, v_bonus_scale, v_lnx_w, v_lnx_b, v_rel_bias, v_w_branch_a, v_w_branch_b, v_w_out, v_g_pre_cross, v_g_post_cross, v_g_mem, v_w_q_mem, v_w_kv_mem, v_w_o_mem, v_g_pre_ffn, v_g_post_ffn, v_w_ffn_in, v_w_ffn_out)
    return _step(args, x.shape[1], mem.shape[1])
```

```python
import functools
import math

import jax
import jax.numpy as jnp
from jax import lax
from jax.experimental import pallas as pl
from jax.experimental.pallas import tpu as pltpu

F32 = jnp.float32
BF16 = jnp.bfloat16

N_DEV = 8
D = 1024
HEAD = 64
N_HEADS = D // HEAD
LANE = 128
CHUNK = 64
LEFT = 8 * CHUNK
BAND = LEFT + CHUNK
REL_CLIP = 128
REL_TABLE = CHUNK + REL_CLIP
MEM_WIDTH = D // 2
MEM_HEADS = 4
FFN = 2816
LORA_W, LORA_A, LORA_G = 64, 64, 160
P_WIDTH = 3 * D + 3 * D + 2 * D + 128 + 128 + 256
C_Q, C_GA, C_LORA = 3 * D, 6 * D, 8 * D
NORM_EPS = 1e-6
GROUP_NORM_EPS = 64e-5
MASK_VALUE = -1e30
ADAM_LR, ADAM_B1, ADAM_B2, ADAM_EPS, ADAM_WD, ADAM_STEP = 0.001, 0.9, 0.999, 1e-08, 0.01, 10
VMEM_LIMIT = 56 * 1024 * 1024


def _cp(*sem):
    return pltpu.CompilerParams(dimension_semantics=sem, vmem_limit_bytes=VMEM_LIMIT)


_NN, _NT, _TN = ((1,), (0,)), ((1,), (1,)), ((0,), (0,))


def _dot_raw(a, b, dims):
    return lax.dot_general(a.astype(BF16), b.astype(BF16), (dims, ((), ())), preferred_element_type=F32)


@functools.partial(jax.custom_vjp, nondiff_argnums=(2,))
def _dot_dims(a, b, dims):
    return _dot_raw(a, b, dims)


def _dot_dims_fwd(a, b, dims):
    return _dot_raw(a, b, dims), (a, b)


def _dot_dims_bwd(dims, res, g):
    a, b = res
    if dims == _NN:
        da, db = _dot_raw(g, b, _NT), _dot_raw(a, g, _TN)
    elif dims == _NT:
        da, db = _dot_raw(g, b, _NN), _dot_raw(g, a, _TN)
    else:
        da, db = _dot_raw(b, g, _NT), _dot_raw(a, g, _NN)
    return da.astype(a.dtype), db.astype(b.dtype)


_dot_dims.defvjp(_dot_dims_fwd, _dot_dims_bwd)


def _dot(a, b, dims=_NN):
    return _dot_dims(a, b, dims)


def _dot_nt(a, b):
    return _dot_dims(a, b, _NT)


def _dot_tn(a, b):
    return _dot_dims(a, b, _TN)


def _split(x, terms):
    parts, rest = [], x.astype(F32)
    for _ in range(terms):
        p = rest.astype(BF16)
        parts.append(p)
        rest = rest - p.astype(F32)
    return parts


def _dot_split_a(a, b, terms=2):
    out = None
    for p in _split(a, terms):
        t = _dot(p, b)
        out = t if out is None else out + t
    return out


def _dot_split_b(a, b, terms=3):
    out = None
    for p in _split(b, terms):
        t = _dot(a, p)
        out = t if out is None else out + t
    return out


MM_VMEM_BUDGET = 30 * 1024 * 1024
MM_HBM_BPS = 3.2e12
MM_MXU_FPS = 8.5e14
MM_STEP_S = 0.35e-6


def _divisors(n, align, cap):
    out = [d for d in range(align, min(n, cap) + 1, align) if n % d == 0]
    return out or [n]


def _mm_tiles(m, n, k, ea, eb, eo, ta):
    best = None
    for tm in _divisors(m, LANE if ta else 8, 2048):
        for tn in _divisors(n, LANE, 2048):
            for tk in _divisors(k, LANE, 2048):
                nk = k // tk
                vmem = 2 * (tm * tk * ea + tk * tn * eb + tm * tn * eo) + (tm * tn * 4 if nk > 1 else 0)
                if vmem > MM_VMEM_BUDGET:
                    continue
                dma = (tm * tk * ea if (nk > 1 or n // tn == 1) else tm * tk * ea * tn / n) + tk * tn * eb + tm * tn * eo / nk
                step = max(2.0 * tm * tn * tk / MM_MXU_FPS, dma / MM_HBM_BPS) + MM_STEP_S
                cost = (m // tm) * (n // tn) * nk * step
                if best is None or cost < best[0]:
                    best = (cost, tm, tn, tk)
    return best[1:]


def _mm(a, b, *, name, ta=False, tb=False, out_dtype=F32, tm=None, tn=None, tk=None, split_a=1, after=None):
    m, k = (a.shape[1], a.shape[0]) if ta else a.shape
    n, kb = (b.shape[0], b.shape[1]) if tb else (b.shape[1], b.shape[0])
    assert k == kb, (a.shape, b.shape, ta, tb)
    if tm is None:
        tm, tn, tk = _mm_tiles(m, n, k, a.dtype.itemsize, b.dtype.itemsize, jnp.dtype(out_dtype).itemsize, ta)
    assert m % tm == 0 and n % tn == 0 and k % tk == 0, (m, n, k, tm, tn, tk)
    nk = k // tk
    dims = ((0 if ta else 1,), (1 if tb else 0,))

    n_after = 0 if after is None else 1

    def body(a_ref, b_ref, *rest):
        o_ref, scratch = rest[n_after], rest[n_after + 1:]
        prod = None
        for p in _split(a_ref[...], split_a) if split_a > 1 else [a_ref[...]]:
            t = _dot_raw(p, b_ref[...], dims)
            prod = t if prod is None else prod + t
        if nk == 1:
            o_ref[...] = prod.astype(o_ref.dtype)
            return
        acc_ref, kk = scratch[0], pl.program_id(2)

        @pl.when(kk == 0)
        def _():
            acc_ref[...] = prod

        @pl.when(kk > 0)
        def _():
            acc_ref[...] += prod

        @pl.when(kk == nk - 1)
        def _():
            o_ref[...] = acc_ref[...].astype(o_ref.dtype)

    a_spec = pl.BlockSpec((tk, tm), lambda i, j, q: (q, i)) if ta else pl.BlockSpec((tm, tk), lambda i, j, q: (i, q))
    b_spec = pl.BlockSpec((tn, tk), lambda i, j, q: (j, q)) if tb else pl.BlockSpec((tk, tn), lambda i, j, q: (q, j))
    return pl.pallas_call(
        body, name=name, grid=(m // tm, n // tn, nk),
        in_specs=[a_spec, b_spec] + [pl.BlockSpec(memory_space=pl.ANY)] * n_after,
        out_specs=pl.BlockSpec((tm, tn), lambda i, j, q: (i, j)),
        out_shape=jax.ShapeDtypeStruct((m, n), out_dtype),
        scratch_shapes=[pltpu.VMEM((tm, tn), F32)] if nk > 1 else [],
        compiler_params=_cp("parallel", "parallel", "arbitrary"),
    )(a, b, *([] if after is None else [after]))


def _piece_steps(pieces, tile):
    counts = [p.shape[1] // tile for p in pieces]
    assert all(p.shape[1] % tile == 0 for p in pieces)
    return [(sum(counts[:i]), c) for i, c in enumerate(counts)], sum(counts)


def _mm_cat_nn(pieces, w, *, name, after=None, tm=2048, tk=256):
    t, n = pieces[0].shape[0], w.shape[1]
    tm = min(tm, t)
    spans, nk = _piece_steps(pieces, tk)
    npc = len(pieces)
    n_after = 0 if after is None else 1

    def body(*refs):
        w_ref, o_ref, acc_ref = refs[npc], refs[npc + 1 + n_after], refs[npc + 2 + n_after]
        q = pl.program_id(1)

        @pl.when(q == 0)
        def _():
            acc_ref[...] = jnp.zeros_like(acc_ref)

        for p_ref, (first, count) in zip(refs[:npc], spans):
            @pl.when(jnp.logical_and(q >= first, q < first + count))
            def _(p_ref=p_ref):
                acc_ref[...] += _dot_raw(p_ref[...], w_ref[...], _NN)

        @pl.when(q == nk - 1)
        def _():
            o_ref[...] = acc_ref[...].astype(o_ref.dtype)

    def piece_spec(first, count):
        return pl.BlockSpec((tm, tk), lambda i, q: (i, jnp.clip(q - first, 0, count - 1)))

    return pl.pallas_call(
        body, name=name, grid=(t // tm, nk),
        in_specs=[piece_spec(*s) for s in spans] + [pl.BlockSpec((tk, n), lambda i, q: (q, 0))]
        + [pl.BlockSpec(memory_space=pl.ANY)] * n_after,
        out_specs=pl.BlockSpec((tm, n), lambda i, q: (i, 0)),
        out_shape=jax.ShapeDtypeStruct((t, n), BF16),
        scratch_shapes=[pltpu.VMEM((tm, n), F32)],
        compiler_params=_cp("parallel", "arbitrary"),
    )(*pieces, w, *([] if after is None else [after]))


def _mm_cat_tn(pieces, a, *, name, after=None, tk=1024, tn=512):
    t, m = a.shape
    tk = min(tk, t)
    spans, nj = _piece_steps(pieces, tn)
    npc, nk = len(pieces), t // tk
    n_after = 0 if after is None else 1

    def body(a_ref, *refs):
        o_ref, acc_ref = refs[npc + n_after], refs[npc + 1 + n_after]
        j, q = pl.program_id(0), pl.program_id(1)

        @pl.when(q == 0)
        def _():
            acc_ref[...] = jnp.zeros_like(acc_ref)

        for p_ref, (first, count) in zip(refs[:npc], spans):
            @pl.when(jnp.logical_and(j >= first, j < first + count))
            def _(p_ref=p_ref):
                acc_ref[...] += _dot_raw(p_ref[...], a_ref[...], _TN)

        @pl.when(q == nk - 1)
        def _():
            o_ref[...] = acc_ref[...].astype(o_ref.dtype)

    def piece_spec(first, count):
        def index(j, q):
            mine = jnp.logical_and(j >= first, j < first + count)
            return jnp.where(mine, q, 0), jnp.clip(j - first, 0, count - 1)
        return pl.BlockSpec((tk, tn), index)

    return pl.pallas_call(
        body, name=name, grid=(nj, nk),
        in_specs=[pl.BlockSpec((tk, m), lambda j, q: (q, 0))] + [piece_spec(*s) for s in spans]
        + [pl.BlockSpec(memory_space=pl.ANY)] * n_after,
        out_specs=pl.BlockSpec((tn, m), lambda j, q: (j, 0)),
        out_shape=jax.ShapeDtypeStruct((nj * tn, m), BF16),
        scratch_shapes=[pltpu.VMEM((tn, m), F32)],
        compiler_params=_cp("parallel", "arbitrary"),
    )(a, *pieces, *([] if after is None else [after]))


def _win(arr, start=0, width=None):
    width = arr.shape[1] if width is None else width
    assert start % width == 0
    return (arr, start // width, width)


def _row_specs(rows, tm):
    return [pl.BlockSpec((tm, w), functools.partial(lambda i, cb: (i, cb), cb=cb)) for (_, cb, w) in rows]


def _full_spec(p):
    nd = p.ndim
    return pl.BlockSpec(p.shape, lambda i, nd=nd: (0,) * nd)


def _rowwise(fn, rows, params, outs, *, name, tm, after=None):
    t = rows[0][0].shape[0]
    tm = min(tm, t)
    assert t % tm == 0
    nr, npar = len(rows), len(params)
    n_after = 0 if after is None else 1

    def body(*refs):
        vals = [r[...] for r in refs[:nr + npar]]
        res = fn(*vals)
        for o_ref, r in zip(refs[nr + npar + n_after:], res):
            o_ref[...] = r.astype(o_ref.dtype)

    return pl.pallas_call(
        body, name=name, grid=(t // tm,),
        in_specs=_row_specs(rows, tm) + [_full_spec(p) for p in params] + [pl.BlockSpec(memory_space=pl.ANY)] * n_after,
        out_specs=[pl.BlockSpec((tm, w), lambda i: (i, 0)) for (w, _) in outs],
        out_shape=[jax.ShapeDtypeStruct((t, w), dt) for (w, dt) in outs],
        compiler_params=_cp("parallel"),
    )(*[r[0] for r in rows], *params, *([] if after is None else [after]))


def _rowwise_bwd(fn, rows, params, n_const, cots, *, name, tm, row_grad, add_to=None, packed=False):
    t = rows[0][0].shape[0]
    tm = min(tm, t)
    assert t % tm == 0
    nr, npar = len(rows), len(params)
    ndp = npar - n_const
    add_to = add_to or {}
    add_idx = sorted(add_to)
    flat_cots = [c for group in cots for c in group]
    kept = [i for i in range(nr) if row_grad[i] is not None]

    def body(*refs):
        pos = 0
        row_v = [r[...] for r in refs[pos:pos + nr]]; pos += nr
        par_v = [r[...] for r in refs[pos:pos + npar]]; pos += npar
        cot_v = [r[...] for r in refs[pos:pos + len(flat_cots)]]; pos += len(flat_cots)
        add_v = [r[...] for r in refs[pos:pos + len(add_idx)]]; pos += len(add_idx)
        if packed:
            offs = [sum(rows[i][2] for i in kept[:q]) for q in range(len(kept))]
            rg_refs = [refs[pos].at[:, o:o + rows[i][2]] for o, i in zip(offs, kept)]; pos += 1
        else:
            rg_refs = refs[pos:pos + len(kept)]; pos += len(kept)
        pg_refs = refs[pos:pos + ndp]

        consts = par_v[ndp:]
        res, vjp = jax.vjp(lambda *args: tuple(fn(*args, *consts)), *row_v, *par_v[:ndp])
        cot_in, q = [], 0
        for j, group in enumerate(cots):
            c = None
            for _ in group:
                cv = cot_v[q].astype(F32); q += 1
                c = cv if c is None else c + cv
            c = jnp.zeros(res[j].shape, F32) if c is None else c
            cot_in.append(c.astype(res[j].dtype))
        grads = vjp(tuple(cot_in))
        for ref, i in zip(rg_refs, kept):
            g = grads[i].astype(F32)
            if i in add_to:
                g = g + add_v[add_idx.index(i)].astype(F32)
            ref[...] = g.astype(ref.dtype)

        @pl.when(pl.program_id(0) == 0)
        def _():
            for ref in pg_refs:
                ref[...] = jnp.zeros_like(ref)

        for ref, g in zip(pg_refs, grads[nr:]):
            ref[...] += g.astype(F32)

    cot_specs = [pl.BlockSpec((tm, c.shape[1]), lambda i: (i, 0)) for c in flat_cots]
    add_specs = [pl.BlockSpec((tm, add_to[i].shape[1]), lambda i_: (i_, 0)) for i in add_idx]
    widths = [sum(rows[i][2] for i in kept)] if packed else [rows[i][2] for i in kept]
    n_rg = len(widths)
    out_specs = [pl.BlockSpec((tm, w), lambda i_: (i_, 0)) for w in widths] + [_full_spec(p) for p in params[:ndp]]
    out_shape = [jax.ShapeDtypeStruct((t, w), row_grad[kept[q]]) for q, w in enumerate(widths)] + [
        jax.ShapeDtypeStruct(p.shape, F32) for p in params[:ndp]]
    res = pl.pallas_call(
        body, name=name, grid=(t // tm,),
        in_specs=_row_specs(rows, tm) + [_full_spec(p) for p in params] + cot_specs + add_specs,
        out_specs=out_specs, out_shape=out_shape,
        compiler_params=_cp("arbitrary"),
    )(*[r[0] for r in rows], *params, *flat_cots, *[add_to[i] for i in add_idx])
    return list(res[:n_rg]), list(res[n_rg:])


def _rms(x, g):
    xf = x.astype(F32)
    return xf * lax.rsqrt(jnp.mean(xf * xf, axis=-1, keepdims=True) + NORM_EPS) * g


def _softplus(x):
    return jnp.maximum(x, 0.0) + jnp.log(1.0 + jnp.exp(-jnp.abs(x)))


def _fn_pre(x, g):
    return (_rms(x, g).astype(BF16),)


def _fn_res(x, u, g_post):
    return (x + _rms(u, g_post),)


def _fn_res_pre(x, u, g_post, g_pre):
    xn = x + _rms(u, g_post)
    return xn, _rms(xn, g_pre).astype(BF16)


def _fn_mix(zga, zgb, ya, yb):
    return ((jax.nn.sigmoid(zga) * ya + jax.nn.sigmoid(zgb) * yb).astype(BF16),)


def _fn_swiglu(gate, up):
    gate, up = gate.astype(F32), up.astype(F32)
    return ((gate * jax.nn.sigmoid(gate) * up).astype(BF16),)


def _fn_prep(zk, zw, za, zg, decay_base, d_up, iclr_base, i_up, g_up, kns, kis, e_hd, e_dh):
    w_log = -_softplus(-(decay_base + _dot(jnp.tanh(zw), d_up))) - 0.5
    lw = -jnp.exp(w_log)
    a = jax.nn.sigmoid(iclr_base + _dot(za, i_up))
    g = _dot(jax.nn.sigmoid(zg), g_up)
    kn = zk * kns
    ss = _dot(kn * kn, e_dh)
    inv = lax.rsqrt(jnp.maximum(ss, 1e-24))
    kk = kn * _dot_split_a(inv, e_hd)
    k2 = zk * (1.0 + (a - 1.0) * kis)
    return lw, k2, kk, a, g


def _fn_post(y, r, k2, v, g, lnx_w, lnx_b, bonus, e_hd, e_dh):
    mu = _dot_split_a(_dot(y, e_dh) * (1.0 / HEAD), e_hd)
    yc = y - mu
    var = _dot(yc * yc, e_dh) * (1.0 / HEAD)
    yn = yc * _dot_split_a(lax.rsqrt(var + GROUP_NORM_EPS), e_hd)
    bs = _dot_split_a(_dot(r * k2 * bonus, e_dh), e_hd)
    return (((yn * lnx_w + lnx_b + bs * v) * g).astype(BF16),)


def _shift_fwd(p, col0, ncols, mix, seq, *, name, cw=256):
    t = p.shape[0]
    assert col0 % cw == 0 and ncols % cw == 0 and t % seq == 0
    cb0 = col0 // cw

    def body(p_ref, m_ref, z_ref):
        pv = p_ref[...]
        row = lax.broadcasted_iota(jnp.int32, pv.shape, 0)
        prev = jnp.where(row == 0, 0.0, pltpu.roll(pv, 1, axis=0))
        z_ref[...] = pv + (prev - pv) * m_ref[...]

    return pl.pallas_call(
        body, name=name, grid=(t // seq, ncols // cw),
        in_specs=[pl.BlockSpec((seq, cw), lambda b, c: (b, c + cb0)), pl.BlockSpec((1, cw), lambda b, c: (0, c))],
        out_specs=pl.BlockSpec((seq, cw), lambda b, c: (b, c)),
        out_shape=jax.ShapeDtypeStruct((t, ncols), F32),
        compiler_params=_cp("parallel", "parallel"),
    )(p, mix)


def _shift_bwd(p, col0, ncols, mix, dz_parts, seq, *, name, cw=256):
    t = p.shape[0]
    cb0 = col0 // cw
    n = len(dz_parts)

    def body(*refs):
        p_ref, m_ref = refs[:2]
        dp_ref, dm_ref = refs[2 + n:]
        dz = refs[2][...].astype(F32)
        for r in refs[3:2 + n]:
            dz = dz + r[...].astype(F32)
        pv = p_ref[...]
        mixv = m_ref[...]
        row = lax.broadcasted_iota(jnp.int32, pv.shape, 0)
        prev = jnp.where(row == 0, 0.0, pltpu.roll(pv, 1, axis=0))
        u = dz * mixv
        nxt = jnp.where(row == seq - 1, 0.0, pltpu.roll(u, seq - 1, axis=0))
        dp_ref[...] = (dz - u + nxt).astype(dp_ref.dtype)

        @pl.when(pl.program_id(1) == 0)
        def _():
            dm_ref[...] = jnp.zeros_like(dm_ref)

        dm_ref[...] += jnp.sum(dz * (prev - pv), axis=0, keepdims=True)

    return pl.pallas_call(
        body, name=name, grid=(ncols // cw, t // seq),
        in_specs=[pl.BlockSpec((seq, cw), lambda c, b: (b, c + cb0)), pl.BlockSpec((1, cw), lambda c, b: (0, c))]
        + [pl.BlockSpec((seq, cw), lambda c, b: (b, c))] * n,
        out_specs=[pl.BlockSpec((seq, cw), lambda c, b: (b, c)), pl.BlockSpec((1, cw), lambda c, b: (0, c))],
        out_shape=[jax.ShapeDtypeStruct((t, ncols), BF16), jax.ShapeDtypeStruct((1, ncols), F32)],
        compiler_params=_cp("parallel", "arbitrary"),
    )(p, mix, *dz_parts)


def _each(f, *lists):
    return [f(*xs) for xs in zip(*lists)]


def _tri_inv(low):
    c = low[0].shape[0]
    ti = lax.broadcasted_iota(jnp.int32, (c, c), 0)
    si = lax.broadcasted_iota(jnp.int32, (c, c), 1)
    eye = (ti == si).astype(F32)
    inside = (ti // 4) == (si // 4)
    base = [jnp.where(inside, m, 0.0) for m in low]
    acc = _each(lambda m: _dot(eye - m, eye + _dot(m, m)), base)
    size = 8
    while size <= c:
        wider = (ti // size) == (si // size)
        keep = jnp.logical_and(wider, jnp.logical_not(inside))
        acc = _each(lambda p, m: p - _dot(_dot(p, jnp.where(keep, m, 0.0)), p), acc, low)
        inside, size = wider, size * 2
    return acc


def _stack_rows(a, b):
    return jnp.concatenate([a, b], axis=0)


@jax.custom_vjp
def _split_rows(x):
    h = x.shape[0] // 2
    return x[:h], x[h:]


def _split_rows_fwd(x):
    return _split_rows(x), None


def _split_rows_bwd(_, g):
    return (jnp.concatenate(g, axis=0),)


_split_rows.defvjp(_split_rows_fwd, _split_rows_bwd)


def _masked_halves(stacked, top_mask, bottom_mask):
    halves = _each(_split_rows, stacked)
    return ([jnp.where(top_mask, t, 0.0) for t, _ in halves], [jnp.where(bottom_mask, b, 0.0) for _, b in halves])


@jax.custom_vjp
def _tri_inv_known(low, inv):
    return inv


def _tri_inv_known_fwd(low, inv):
    return inv, inv


def _tri_inv_known_bwd(inv, g):
    dlow = _each(lambda t, gg: -_dot(_dot(t, gg, _TN), t, _NT), inv, g)
    return dlow, _each(jnp.zeros_like, inv)


_tri_inv_known.defvjp(_tri_inv_known_fwd, _tri_inv_known_bwd)


def _wkv_chunk(s0, r, lw, k, v, kk, a, inv=None):
    c = r[0].shape[0]
    ti = lax.broadcasted_iota(jnp.int32, (c, c), 0)
    si = lax.broadcasted_iota(jnp.int32, (c, c), 1)
    incl, strict = ti >= si, ti > si
    tri = incl.astype(F32)
    cum = _each(lambda x: _dot_split_b(tri, x, 3), lw)
    eg = _each(jnp.exp, cum)
    egp = _each(lambda cs, x: jnp.exp(cs - x), cum, lw)
    ei = _each(lambda cs: jnp.exp(-cs), cum)
    rh, kkh, kt = _each(jnp.multiply, r, eg), _each(jnp.multiply, kk, egp), _each(jnp.multiply, k, ei)
    bt = _each(lambda p, q, e: (p * q) * e, a, kk, ei)
    both = _each(_stack_rows, kkh, rh)
    on_b, on_k, on_s = _each(_dot_nt, both, bt), _each(_dot_nt, both, kt), _each(_dot_nt, both, s0)
    lb, mb = _masked_halves(on_b, strict, incl)
    lk, mk = _masked_halves(on_k, strict, incl)
    on_s = _each(_split_rows, on_s)
    on_v = _each(lambda p, q, x: _split_rows(_dot(_stack_rows(p, q), x)), lk, mk, v)
    rhs = _each(lambda p, q: p[0] + q[0], on_s, on_v)
    inv = _tri_inv(lb) if inv is None else _tri_inv_known(lb, inv)
    u = _each(lambda t, x: -_dot(t, x), inv, rhs)
    y = _each(lambda p, m1, uu, q: p[1] + _dot(m1, uu) + q[1], on_s, mb, u, on_v)
    s1 = _each(lambda s, uu, x, b, kq, w: (s + _dot_tn(_stack_rows(uu, x), _stack_rows(b, kq)))
               * jnp.exp(jnp.sum(w, axis=0, keepdims=True)), s0, u, v, bt, kt, lw)
    return y, s1, inv


WKV_HEADS = 16
WKV_COLS = WKV_HEADS * HEAD
WKV_GROUPS = N_HEADS // WKV_HEADS


def _head_cols(ref):
    return [ref[:, h * HEAD:(h + 1) * HEAD] for h in range(ref.shape[1] // HEAD)]


def _wkv_specs(seq, rev):
    nc = seq // CHUNK

    def rows(col0):
        cb0 = col0 // WKV_COLS
        if rev:
            return pl.BlockSpec((CHUNK, WKV_COLS), lambda b, h, c: (b * nc + nc - 1 - c, cb0 + h))
        return pl.BlockSpec((CHUNK, WKV_COLS), lambda b, h, c: (b * nc + c, cb0 + h))

    if rev:
        st = pl.BlockSpec((1, 1, WKV_HEADS, HEAD, HEAD), lambda b, h, c: (b * WKV_GROUPS + h, nc - 1 - c, 0, 0, 0))
    else:
        st = pl.BlockSpec((1, 1, WKV_HEADS, HEAD, HEAD), lambda b, h, c: (b * WKV_GROUPS + h, c, 0, 0, 0))
    return rows, st


def _wkv_fwd(z_rkv, lw, k2, kk, a, seq):
    t = z_rkv.shape[0]
    nb, nc = t // seq, seq // CHUNK
    rows, st = _wkv_specs(seq, False)

    def body(r_ref, v_ref, lw_ref, k_ref, kk_ref, a_ref, y_ref, st_ref, inv_ref, s_scr):
        @pl.when(pl.program_id(2) == 0)
        def _():
            s_scr[...] = jnp.zeros_like(s_scr)

        s0 = [s_scr[h] for h in range(WKV_HEADS)]
        y, s1, inv = _wkv_chunk(s0, *[_head_cols(ref) for ref in (r_ref, lw_ref, k_ref, v_ref, kk_ref, a_ref)])
        for h in range(WKV_HEADS):
            st_ref[0, 0, h] = s0[h]
            inv_ref[0, 0, h] = inv[h]
            y_ref[:, h * HEAD:(h + 1) * HEAD] = y[h]
            s_scr[h] = s1[h]

    per_chunk = jax.ShapeDtypeStruct((nb * WKV_GROUPS, nc, WKV_HEADS, HEAD, HEAD), F32)
    return pl.pallas_call(
        body, name="wkv_fwd", grid=(nb, WKV_GROUPS, nc),
        in_specs=[rows(0), rows(2 * D), rows(0), rows(0), rows(0), rows(0)],
        out_specs=[rows(0), st, st],
        out_shape=[jax.ShapeDtypeStruct((t, D), F32), per_chunk, per_chunk],
        scratch_shapes=[pltpu.VMEM((WKV_HEADS, HEAD, HEAD), F32)],
        compiler_params=_cp("parallel", "parallel", "arbitrary"),
    )(z_rkv, z_rkv, lw, k2, kk, a)


def _wkv_bwd(z_rkv, lw, k2, kk, a, states, invs, dy, seq):
    t = z_rkv.shape[0]
    nb, nc = t // seq, seq // CHUNK
    rows, st = _wkv_specs(seq, True)

    def body(r_ref, v_ref, lw_ref, k_ref, kk_ref, a_ref, st_ref, inv_ref, dy_ref,
             dr_ref, dlw_ref, dk_ref, dv_ref, dkk_ref, da_ref, ds_scr):
        @pl.when(pl.program_id(2) == 0)
        def _():
            ds_scr[...] = jnp.zeros_like(ds_scr)

        s0 = [st_ref[0, 0, h] for h in range(WKV_HEADS)]
        inv = [inv_ref[0, 0, h] for h in range(WKV_HEADS)]
        _, vjp = jax.vjp(lambda *args: _wkv_chunk(*args, inv=inv)[:2],
                         s0, *[_head_cols(ref) for ref in (r_ref, lw_ref, k_ref, v_ref, kk_ref, a_ref)])
        grads = vjp(([x.astype(F32) for x in _head_cols(dy_ref)], [ds_scr[h] for h in range(WKV_HEADS)]))
        for h in range(WKV_HEADS):
            ds_scr[h] = grads[0][h]
            for ref, g in zip((dr_ref, dlw_ref, dk_ref, dv_ref, dkk_ref, da_ref), grads[1:]):
                ref[:, h * HEAD:(h + 1) * HEAD] = g[h].astype(ref.dtype)

    return pl.pallas_call(
        body, name="wkv_bwd", grid=(nb, WKV_GROUPS, nc),
        in_specs=[rows(0), rows(2 * D), rows(0), rows(0), rows(0), rows(0), st, st, rows(0)],
        out_specs=[rows(0)] * 6,
        out_shape=[jax.ShapeDtypeStruct((t, D), BF16)] * 6,
        scratch_shapes=[pltpu.VMEM((WKV_HEADS, HEAD, HEAD), F32)],
        compiler_params=_cp("parallel", "parallel", "arbitrary"),
    )(z_rkv, z_rkv, lw, k2, kk, a, states, invs, dy)


def _softmax(s):
    e = jnp.exp(s - jnp.max(s, axis=-1, keepdims=True))
    return e * (1.0 / jnp.sum(e, axis=-1, keepdims=True))


ATT_FWD_HEADS = 16
ATT_HEADS = 8
ATT_COLS = ATT_HEADS * HEAD
ATT_GROUPS = N_HEADS // ATT_HEADS


def _attn_chunk(q, kb, vb, bias, valid):
    s = _each(lambda x, y, z: jnp.where(valid, _dot_nt(x * (HEAD ** -0.5), y) + z, MASK_VALUE), q, kb, bias)
    return _each(_dot, _each(_softmax, s), vb)


def _pad_fill(pad_ref, src_ref):
    pad_ref[0:LEFT, :] = jnp.zeros((LEFT, pad_ref.shape[1]), pad_ref.dtype)
    pad_ref[LEFT:, :] = src_ref[...].astype(pad_ref.dtype)


def _band_heads(pad_ref, start):
    return [pad_ref[pl.ds(start, BAND), h * HEAD:(h + 1) * HEAD].astype(F32) for h in range(pad_ref.shape[1] // HEAD)]


def _band_valid(c):
    return (c * CHUNK - LEFT + lax.broadcasted_iota(jnp.int32, (1, BAND), 1)) >= 0


def _bias_spec():
    return pl.BlockSpec((ATT_HEADS, CHUNK, BAND), lambda h, b, c: (h, 0, 0))


def _attn_fwd(proj, bias, seq):
    t = proj.shape[0]
    nb, nc = t // seq, seq // CHUNK
    heads = ATT_FWD_HEADS
    cols, groups = heads * HEAD, N_HEADS // heads
    cq = C_Q // cols

    def body(q_ref, k_ref, v_ref, b_ref, o_ref, kpad, vpad):
        c = pl.program_id(2)

        @pl.when(c == 0)
        def _():
            _pad_fill(kpad, k_ref)
            _pad_fill(vpad, v_ref)

        start = pl.multiple_of(c * CHUNK, CHUNK)
        o = _attn_chunk(_head_cols(q_ref), _band_heads(kpad, start), _band_heads(vpad, start),
                        [b_ref[h] for h in range(heads)], _band_valid(c))
        for h in range(heads):
            o_ref[:, h * HEAD:(h + 1) * HEAD] = o[h].astype(o_ref.dtype)

    return pl.pallas_call(
        body, name="attn_fwd", grid=(groups, nb, nc),
        in_specs=[pl.BlockSpec((CHUNK, cols), lambda h, b, c: (b * nc + c, cq + h)),
                  pl.BlockSpec((seq, cols), lambda h, b, c: (b, cq + groups + h)),
                  pl.BlockSpec((seq, cols), lambda h, b, c: (b, cq + 2 * groups + h)),
                  pl.BlockSpec((heads, CHUNK, BAND), lambda h, b, c: (h, 0, 0))],
        out_specs=pl.BlockSpec((CHUNK, cols), lambda h, b, c: (b * nc + c, h)),
        out_shape=jax.ShapeDtypeStruct((t, D), BF16),
        scratch_shapes=[pltpu.VMEM((seq + LEFT, cols), BF16)] * 2,
        compiler_params=_cp("parallel", "arbitrary", "arbitrary"),
    )(proj, proj, proj, bias)


def _attn_bwd(proj, bias, do, seq):
    t = proj.shape[0]
    nb, nc = t // seq, seq // CHUNK
    cq = C_Q // ATT_COLS

    def body(q_ref, k_ref, v_ref, b_ref, do_ref, dq_ref, dk_ref, dv_ref, db_ref, kpad, vpad, dkpad, dvpad):
        b, c = pl.program_id(1), pl.program_id(2)

        @pl.when(c == 0)
        def _():
            _pad_fill(kpad, k_ref)
            _pad_fill(vpad, v_ref)
            dkpad[...] = jnp.zeros_like(dkpad)
            dvpad[...] = jnp.zeros_like(dvpad)

        @pl.when(jnp.logical_and(b == 0, c == 0))
        def _():
            db_ref[...] = jnp.zeros_like(db_ref)

        start = pl.multiple_of(c * CHUNK, CHUNK)
        _, vjp = jax.vjp(functools.partial(_attn_chunk, valid=_band_valid(c)),
                         _head_cols(q_ref), _band_heads(kpad, start), _band_heads(vpad, start),
                         [b_ref[h] for h in range(ATT_HEADS)])
        dq, dkb, dvb, dbias = vjp([x.astype(F32) for x in _head_cols(do_ref)])
        for h in range(ATT_HEADS):
            sl = slice(h * HEAD, (h + 1) * HEAD)
            dq_ref[:, sl] = dq[h].astype(dq_ref.dtype)
            dkpad[pl.ds(start, BAND), sl] += dkb[h].astype(F32)
            dvpad[pl.ds(start, BAND), sl] += dvb[h].astype(F32)
            db_ref[h] += dbias[h]

        @pl.when(c == nc - 1)
        def _():
            dk_ref[...] = dkpad[LEFT:, :].astype(dk_ref.dtype)
            dv_ref[...] = dvpad[LEFT:, :].astype(dv_ref.dtype)

    kv_out = pl.BlockSpec((seq, ATT_COLS), lambda h, b, c: (b, h))
    return pl.pallas_call(
        body, name="attn_bwd", grid=(ATT_GROUPS, nb, nc),
        in_specs=[pl.BlockSpec((CHUNK, ATT_COLS), lambda h, b, c: (b * nc + c, cq + h)),
                  pl.BlockSpec((seq, ATT_COLS), lambda h, b, c: (b, cq + ATT_GROUPS + h)),
                  pl.BlockSpec((seq, ATT_COLS), lambda h, b, c: (b, cq + 2 * ATT_GROUPS + h)),
                  _bias_spec(),
                  pl.BlockSpec((CHUNK, ATT_COLS), lambda h, b, c: (b * nc + c, h))],
        out_specs=[pl.BlockSpec((CHUNK, ATT_COLS), lambda h, b, c: (b * nc + c, h)), kv_out, kv_out,
                   pl.BlockSpec((ATT_HEADS, CHUNK, BAND), lambda h, b, c: (h, 0, 0))],
        out_shape=[jax.ShapeDtypeStruct((t, D), BF16)] * 3 + [jax.ShapeDtypeStruct((N_HEADS, CHUNK, BAND), F32)],
        scratch_shapes=[pltpu.VMEM((seq + LEFT, ATT_COLS), BF16)] * 2 + [pltpu.VMEM((seq + LEFT, ATT_COLS), F32)] * 2,
        compiler_params=_cp("parallel", "arbitrary", "arbitrary"),
    )(proj, proj, proj, bias, do)


def _xattn_tile(q, k, v):
    s = _dot_nt(q, k) * ((MEM_WIDTH // MEM_HEADS) ** -0.5)
    return _dot(_softmax(s), v)


def _xattn_fwd(qm, kvm, seq, n_mem, tq=1024):
    t = qm.shape[0]
    tq = min(tq, seq)
    nb, nq = t // seq, seq // tq

    def body(q_ref, k_ref, v_ref, o_ref):
        o_ref[...] = _xattn_tile(q_ref[...], k_ref[...], v_ref[...]).astype(o_ref.dtype)

    return pl.pallas_call(
        body, name="xattn_fwd", grid=(nb, MEM_HEADS, nq),
        in_specs=[pl.BlockSpec((tq, LANE), lambda b, h, i: (b * nq + i, h)),
                  pl.BlockSpec((n_mem, LANE), lambda b, h, i: (b, h)),
                  pl.BlockSpec((n_mem, LANE), lambda b, h, i: (b, MEM_HEADS + h))],
        out_specs=pl.BlockSpec((tq, LANE), lambda b, h, i: (b * nq + i, h)),
        out_shape=jax.ShapeDtypeStruct((t, MEM_WIDTH), BF16),
        compiler_params=_cp("parallel", "parallel", "parallel"),
    )(qm, kvm, kvm)


def _xattn_bwd(qm, kvm, do, seq, n_mem, tq=1024):
    t = qm.shape[0]
    tq = min(tq, seq)
    nb, nq = t // seq, seq // tq

    def body(q_ref, k_ref, v_ref, do_ref, dq_ref, dkv_ref, dk_acc, dv_acc):
        i = pl.program_id(2)

        @pl.when(i == 0)
        def _():
            dk_acc[...] = jnp.zeros_like(dk_acc)
            dv_acc[...] = jnp.zeros_like(dv_acc)

        _, vjp = jax.vjp(_xattn_tile, q_ref[...], k_ref[...], v_ref[...])
        dq, dk, dv = vjp(do_ref[...].astype(F32))
        dq_ref[...] = dq.astype(dq_ref.dtype)
        dk_acc[...] += dk
        dv_acc[...] += dv

        @pl.when(i == nq - 1)
        def _():
            dkv_ref[0] = dk_acc[...].astype(dkv_ref.dtype)
            dkv_ref[1] = dv_acc[...].astype(dkv_ref.dtype)

    dq, dkv = pl.pallas_call(
        body, name="xattn_bwd", grid=(nb, MEM_HEADS, nq),
        in_specs=[pl.BlockSpec((tq, LANE), lambda b, h, i: (b * nq + i, h)),
                  pl.BlockSpec((n_mem, LANE), lambda b, h, i: (b, h)),
                  pl.BlockSpec((n_mem, LANE), lambda b, h, i: (b, MEM_HEADS + h)),
                  pl.BlockSpec((tq, LANE), lambda b, h, i: (b * nq + i, h))],
        out_specs=[pl.BlockSpec((tq, LANE), lambda b, h, i: (b * nq + i, h)),
                   pl.BlockSpec((2, n_mem, LANE), lambda b, h, i: (0, b, h))],
        out_shape=[jax.ShapeDtypeStruct((t, MEM_WIDTH), BF16), jax.ShapeDtypeStruct((2, nb * n_mem, MEM_WIDTH), BF16)],
        scratch_shapes=[pltpu.VMEM((n_mem, LANE), F32)] * 2,
        compiler_params=_cp("parallel", "parallel", "arbitrary"),
    )(qm, kvm, kvm, do)
    return dq, jnp.concatenate([dkv[0], dkv[1]], axis=1)


def _loss_head(x, u, g_post, target, tm=512):
    t, d = x.shape
    tm = min(tm, t)

    def tile_loss(xv, uv, gv, tv):
        diff = _fn_res(xv, uv, gv)[0] - tv
        return 0.5 * jnp.sum(jnp.mean(diff * diff, axis=-1, keepdims=True), axis=0, keepdims=True)

    def body(x_ref, u_ref, g_ref, t_ref, l_ref, dx_ref, du_ref, dg_ref):
        @pl.when(pl.program_id(0) == 0)
        def _():
            l_ref[...] = jnp.zeros_like(l_ref)
            dg_ref[...] = jnp.zeros_like(dg_ref)

        tv = t_ref[...]
        part, vjp = jax.vjp(lambda xv, uv, gv: tile_loss(xv, uv, gv, tv), x_ref[...], u_ref[...], g_ref[...])
        dx, du, dg = vjp(jnp.ones((1, 1), F32))
        l_ref[...] += part
        dx_ref[...] = dx
        du_ref[...] = du.astype(du_ref.dtype)
        dg_ref[...] += dg

    rows = pl.BlockSpec((tm, d), lambda i: (i, 0))
    vec = pl.BlockSpec((1, d), lambda i: (0, 0))
    return pl.pallas_call(
        body, name="loss_head", grid=(t // tm,),
        in_specs=[rows, rows, vec, rows],
        out_specs=[pl.BlockSpec((8, LANE), lambda i: (0, 0)), rows, rows, vec],
        out_shape=[jax.ShapeDtypeStruct((8, LANE), F32), jax.ShapeDtypeStruct((t, d), F32),
                   jax.ShapeDtypeStruct((t, d), BF16), jax.ShapeDtypeStruct((1, d), F32)],
        compiler_params=_cp("arbitrary"),
    )(x, u, g_post, target)


def _mesh_pos():
    return lax.axis_index("x"), lax.axis_index("y"), lax.axis_index("c")


def _peer(pos, d):
    x, y, c = pos
    return ((1 - x) if d & 4 else x, (1 - y) if d & 2 else y, (1 - c) if d & 1 else c)


def _flat(pos):
    return 4 * pos[0] + 2 * pos[1] + pos[2]


def _exchange(arrays, scatter, *, name):
    n = len(arrays)
    shapes = [a.shape[1:] if scatter else a.shape for a in arrays]

    def body(*refs):
        ins, outs = refs[:n], refs[n:2 * n]
        send, recv, loc = refs[2 * n:]
        pos = _mesh_pos()
        me = _flat(pos)
        pending = []
        for i in range(n):
            own = pltpu.make_async_copy(ins[i].at[me] if scatter else ins[i], outs[i].at[me], loc.at[i])
            own.start()
            pending.append(own)
            for d in range(1, N_DEV):
                peer = _peer(pos, d)
                src = ins[i].at[_flat(peer)] if scatter else ins[i]
                out_cp = pltpu.make_async_remote_copy(
                    src_ref=src, dst_ref=outs[i].at[me], send_sem=send.at[i, d - 1], recv_sem=recv.at[i, d - 1],
                    device_id=peer, device_id_type=pl.DeviceIdType.MESH)
                out_cp.start()
                pending.append(out_cp)
        for i in range(n):
            own = pending[i * N_DEV]
            for d in range(1, N_DEV):
                peer = _peer(pos, d)
                src = ins[i].at[_flat(peer)] if scatter else ins[i]
                pending[i * N_DEV + d].wait_send()
                pltpu.make_async_remote_copy(
                    src_ref=src, dst_ref=outs[i].at[_flat(peer)], send_sem=send.at[i, d - 1], recv_sem=recv.at[i, d - 1],
                    device_id=peer, device_id_type=pl.DeviceIdType.MESH).wait_recv()
            own.wait()

    hbm = pl.BlockSpec(memory_space=pltpu.HBM)
    return pl.pallas_call(
        body, name=name,
        in_specs=[hbm] * n, out_specs=[hbm] * n,
        out_shape=[jax.ShapeDtypeStruct((N_DEV,) + tuple(s), a.dtype) for s, a in zip(shapes, arrays)],
        scratch_shapes=[pltpu.SemaphoreType.DMA((n, N_DEV - 1)), pltpu.SemaphoreType.DMA((n, N_DEV - 1)),
                        pltpu.SemaphoreType.DMA((n,))],
    )(*arrays)


_HBM = pl.BlockSpec(memory_space=pltpu.HBM)
_SEM = pl.BlockSpec(memory_space=pltpu.SEMAPHORE)
_DATAFLOW = pltpu.SideEffectType.DATAFLOW_SIDE_EFFECTING


_ALL_PEERS = tuple(range(1, N_DEV))
_SIBLING_AND_SAME_CORE = (1, 2, 4, 6)


def _remote_copies(ins, lands, send, recv, scatter, dists):
    pos = _mesh_pos()
    me = _flat(pos)
    out = []
    for i in range(len(ins)):
        for j, d in enumerate(dists):
            peer = _peer(pos, d)
            src = ins[i].at[_flat(peer)] if scatter else ins[i]
            pair = i * len(dists) + j
            sems = dict(send_sem=send.at[pair], recv_sem=recv.at[pair], device_id=peer,
                        device_id_type=pl.DeviceIdType.MESH)
            out.append((pltpu.make_async_remote_copy(src_ref=src, dst_ref=lands[i].at[me], **sems),
                        pltpu.make_async_remote_copy(src_ref=src, dst_ref=lands[i].at[_flat(peer)], **sems)))
    return out


def _exchange_start(arrays, scatter, after, *, name, dists=_ALL_PEERS):
    n = len(arrays)
    shapes = [a.shape[1:] if scatter else a.shape for a in arrays]
    lands = [pltpu.with_memory_space_constraint(lax.empty((N_DEV,) + tuple(s), a.dtype), pltpu.HBM)
             for s, a in zip(shapes, arrays)]
    srcs = [pltpu.with_memory_space_constraint(a, pltpu.HBM) for a in arrays]

    def body(*refs):
        ins, land_refs = refs[:n], refs[n:2 * n]
        send, recv, token = refs[2 * n + 1], refs[2 * n + 2], refs[-1]
        for going, _ in _remote_copies(ins, land_refs, send, recv, scatter, dists):
            going.start()
        token[...] = jnp.zeros_like(token)

    sems = pltpu.SemaphoreType.DMA((n * len(dists),))
    res = pl.pallas_call(
        body, name=name,
        out_shape=(sems, sems, *[pltpu.HBM(a.shape, a.dtype) for a in srcs + lands], jax.ShapeDtypeStruct((8, LANE), F32)),
        in_specs=[_HBM] * (2 * n) + [pl.BlockSpec(memory_space=pl.ANY)],
        out_specs=(_SEM, _SEM, *[_HBM] * (2 * n), pl.BlockSpec(memory_space=pltpu.VMEM)),
        input_output_aliases={i: 2 + i for i in range(2 * n)},
        compiler_params=pltpu.CompilerParams(has_side_effects=_DATAFLOW),
    )(*srcs, *lands, after)
    return (n, scatter, dists, res[0], res[1], list(res[2:2 + 2 * n])), res[-1]


def _exchange_wait(handle, after, own, *, name):
    n, scatter, dists, send, recv, thru = handle

    def body(*refs):
        ins, land_refs = refs[:n], refs[n:2 * n]
        for going, coming in _remote_copies(ins, land_refs, refs[2 * n], refs[2 * n + 1], scatter, dists):
            going.wait_send()
            coming.wait_recv()

    res = pl.pallas_call(
        body, name=name,
        out_shape=tuple(pltpu.HBM(a.shape, a.dtype) for a in thru),
        in_specs=[_HBM] * (2 * n) + [_SEM, _SEM] + [pl.BlockSpec(memory_space=pl.ANY)] * len(after),
        out_specs=tuple([_HBM] * (2 * n)),
        input_output_aliases={i: i for i in range(2 * n)},
        compiler_params=pltpu.CompilerParams(has_side_effects=_DATAFLOW),
    )(*thru, send, recv, *after)
    me = _flat(_mesh_pos())
    return [lax.dynamic_update_slice_in_dim(land, o[None].astype(land.dtype), me, 0) for land, o in zip(res[n:], own)]


_OTHER_CHIPS = (2, 4, 6)


def _relay_to_sibling(gathered, *, name):
    n, k = len(gathered), len(_OTHER_CHIPS)

    def body(*refs):
        ins, outs = refs[:n], refs[n:2 * n]
        send, recv = refs[2 * n:]
        pos = _mesh_pos()
        copies = []
        for i in range(n):
            for j, d in enumerate(_OTHER_CHIPS):
                cp = pltpu.make_async_remote_copy(
                    src_ref=ins[i].at[_flat(_peer(pos, d))], dst_ref=outs[i].at[j],
                    send_sem=send.at[i * k + j], recv_sem=recv.at[i * k + j],
                    device_id=_peer(pos, 1), device_id_type=pl.DeviceIdType.MESH)
                cp.start()
                copies.append(cp)
        for cp in copies:
            cp.wait()

    return pl.pallas_call(
        body, name=name, in_specs=[_HBM] * n, out_specs=[_HBM] * n,
        out_shape=[jax.ShapeDtypeStruct((k,) + g.shape[1:], g.dtype) for g in gathered],
        scratch_shapes=[pltpu.SemaphoreType.DMA((n * k,)), pltpu.SemaphoreType.DMA((n * k,))],
    )(*gathered)


def _adamw(parts, w, m, v, *, name, tr=128, after=None):
    r, c = w.shape
    align = 8 * 4 // parts.dtype.itemsize
    row_tiles = [d for d in range(align, min(tr, r) + 1, align) if r % d == 0]
    tr, tc = (max(row_tiles), c) if row_tiles else (r, LANE)
    assert c % tc == 0
    n_after = 0 if after is None else 1

    def body(p_ref, w_ref, m_ref, v_ref, *rest):
        g_ref, d_ref, nm_ref, nv_ref = rest[n_after:]
        g = p_ref[0].astype(F32)
        for j in range(1, N_DEV):
            g = g + p_ref[j].astype(F32)
        m2 = ADAM_B1 * m_ref[...] + (1.0 - ADAM_B1) * g
        v2 = ADAM_B2 * v_ref[...] + (1.0 - ADAM_B2) * (g * g)
        m_hat = m2 / (1.0 - ADAM_B1 ** ADAM_STEP)
        v_hat = v2 / (1.0 - ADAM_B2 ** ADAM_STEP)
        g_ref[...] = g
        d_ref[...] = -ADAM_LR * (m_hat / (jnp.sqrt(v_hat) + ADAM_EPS) + ADAM_WD * w_ref[...])
        nm_ref[...] = m2
        nv_ref[...] = v2

    spec = pl.BlockSpec((tr, tc), lambda i, j: (i, j))
    return pl.pallas_call(
        body, name=name, grid=(r // tr, c // tc),
        in_specs=[pl.BlockSpec((N_DEV, tr, tc), lambda i, j: (0, i, j)), spec, spec, spec]
        + [pl.BlockSpec(memory_space=pl.ANY)] * n_after,
        out_specs=[spec] * 4, out_shape=[jax.ShapeDtypeStruct((r, c), F32)] * 4,
        compiler_params=_cp("parallel", "parallel"),
    )(parts, w, m, v, *([] if after is None else [after]))


def _cols_to_full(g):
    return jnp.transpose(g, (1, 0, 2)).reshape(g.shape[1], N_DEV * g.shape[2])


def _full_to_cols(w):
    r, c = w.shape
    return jnp.transpose(w.reshape(r, N_DEV, c // N_DEV), (1, 0, 2))


def _cut(a, lo, hi, axis):
    return lax.slice_in_dim(a, lo, hi, axis=axis)


def _pad_to(a, size, axis):
    pads = [(0, 0)] * a.ndim
    pads[axis] = (0, size - a.shape[axis])
    return jnp.pad(a, pads)


def _pad_lora(w, axis=1):
    return jnp.concatenate([
        _pad_to(_cut(w, 0, LORA_W, axis), 128, axis), _pad_to(_cut(w, LORA_W, LORA_W + LORA_A, axis), 128, axis),
        _pad_to(_cut(w, LORA_W + LORA_A, w.shape[axis], axis), 256, axis)], axis=axis)


def _unpad_lora(wp, axis=1):
    return jnp.concatenate([_cut(wp, 0, LORA_W, axis), _cut(wp, 128, 128 + LORA_A, axis),
                            _cut(wp, 256, 256 + LORA_G, axis)], axis=axis)


def _permute_in(w, axis):
    rk = 3 * D
    lo = rk + LORA_W + LORA_A + LORA_G
    return jnp.concatenate([_cut(w, 0, rk, axis), _cut(w, lo, w.shape[axis], axis), _pad_lora(_cut(w, rk, lo, axis), axis)],
                           axis=axis)


def _unpermute_in(wp, axis):
    return jnp.concatenate([_cut(wp, 0, 3 * D, axis), _unpad_lora(_cut(wp, C_LORA, P_WIDTH, axis), axis),
                            _cut(wp, 3 * D, C_LORA, axis)], axis=axis)


def _rel_index():
    dist = jnp.arange(CHUNK)[:, None] - jnp.arange(BAND)[None, :] + LEFT
    return (jnp.minimum(dist, REL_CLIP) + (CHUNK - 1)).reshape(-1)


def _local_step(x, mem, target, wt, seq, n_mem, comm):
    t = x.shape[0]
    row = lambda a: a.reshape(1, -1).astype(F32)
    g_pre_mix, g_post_mix = row(wt["g_pre_mix"]), row(wt["g_post_mix"])
    g_pre_cross, g_post_cross, g_mem = row(wt["g_pre_cross"]), row(wt["g_post_cross"]), row(wt["g_mem"])
    g_pre_ffn, g_post_ffn = row(wt["g_pre_ffn"]), row(wt["g_post_ffn"])
    mix = row(wt["shift_mix"])
    mix_rkv, mix_lora = mix[:, :3 * D], _pad_lora(mix[:, 3 * D:])
    decay_base, iclr_base = row(wt["decay_base"]), row(wt["iclr_base"])
    kns, kis = row(wt["key_norm_scale"]), row(wt["key_iclr_scale"])
    lnx_w, lnx_b, bonus = row(wt["lnx_w"]), row(wt["lnx_b"]), row(wt["bonus_scale"])
    e_dh = (jnp.arange(D)[:, None] // HEAD == jnp.arange(N_HEADS)[None, :]).astype(F32)
    e_hd = e_dh.T
    onehot = (jnp.arange(REL_TABLE)[:, None] == _rel_index()[None, :]).astype(BF16)

    begun = comm.begun
    (h1,) = _rowwise(_fn_pre, [_win(x)], [g_pre_mix], [(D, BF16)], name="pre_mix", tm=512, after=begun)
    (mn,) = _rowwise(_fn_pre, [_win(mem)], [g_mem], [(D, BF16)], name="pre_mem", tm=512, after=begun)
    bias = _mm(wt["rel_bias"].astype(F32), onehot, name="mm_bias", split_a=3, after=begun).reshape(N_HEADS, CHUNK, BAND)
    wt = {**wt, **comm.first_weights([h1, mn, bias])}
    w_in = wt["w_in_p"]
    d_up = jnp.pad(wt["decay_up"].astype(F32), ((0, 128 - LORA_W), (0, 0)))
    i_up = jnp.pad(wt["iclr_up"].astype(F32), ((0, 128 - LORA_A), (0, 0)))
    g_up = jnp.pad(wt["gate_up"].astype(F32), ((0, 256 - LORA_G), (0, 0)))
    proj = _mm(h1, w_in, tb=True, name="mm_in", after=comm.first_token)
    z_rkv = _shift_fwd(proj, 0, 3 * D, mix_rkv, seq, name="shift_rkv")
    z_lora = _shift_fwd(proj, C_LORA, 512, mix_lora, seq, name="shift_lora")
    prep_rows = [_win(z_rkv, D, D), _win(z_lora, 0, 128), _win(z_lora, 128, 128), _win(z_lora, 256, 256)]
    prep_params = [decay_base, d_up, iclr_base, i_up, g_up, kns, kis, e_hd, e_dh]
    lw, k2, kk, a, g = _rowwise(_fn_prep, prep_rows, prep_params, [(D, F32)] * 5, name="rwkv_prep", tm=256)
    y, states, invs = _wkv_fwd(z_rkv, lw, k2, kk, a, seq)
    post_rows = [_win(y), _win(z_rkv, 0, D), _win(k2), _win(z_rkv, 2 * D, D), _win(g)]
    post_params = [lnx_w, lnx_b, bonus, e_hd, e_dh]
    (y_a,) = _rowwise(_fn_post, post_rows, post_params, [(D, BF16)], name="rwkv_post", tm=256)
    y_b = _attn_fwd(proj, bias, seq)
    wt = {**wt, **comm.late_weights(y_b)}
    ya_p = _mm(y_a, wt["w_branch_a"], name="mm_a")
    yb_p = _mm(y_b, wt["w_branch_b"], name="mm_b")
    mix_rows = [_win(proj, C_GA, D), _win(proj, C_GA + D, D), _win(ya_p), _win(yb_p)]
    (mixed,) = _rowwise(_fn_mix, mix_rows, [], [(D, BF16)], name="gate_mix", tm=512)
    mo = _mm(mixed, wt["w_out"], name="mm_out")
    x1, h2 = _rowwise(_fn_res_pre, [_win(x), _win(mo)], [g_post_mix, g_pre_cross], [(D, F32), (D, BF16)],
                      name="res_mix", tm=512)
    qm = _mm(h2, wt["w_q_mem"], name="mm_q")
    kvm = _mm(mn, wt["w_kv_mem"], name="mm_kv")
    om = _xattn_fwd(qm, kvm, seq, n_mem)
    co = _mm(om, wt["w_o_mem"], name="mm_o")
    x2, h3 = _rowwise(_fn_res_pre, [_win(x1), _win(co)], [g_post_cross, g_pre_ffn], [(D, F32), (D, BF16)],
                      name="res_cross", tm=512)
    gu = _mm(h3, wt["w_ffn_in"], tb=True, name="mm_ffn_in", out_dtype=BF16)
    (act,) = _rowwise(_fn_swiglu, [_win(gu, 0, FFN), _win(gu, FFN, FFN)], [], [(FFN, BF16)], name="swiglu", tm=256)
    ff = _mm(act, wt["w_ffn_out"], name="mm_ffn_out")

    gw = {}
    loss, dx2, dff, gw["g_post_ffn"] = _loss_head(x2, ff, g_post_ffn, target)
    dact = _mm(dff, wt["w_ffn_out"], tb=True, name="mm_ffn_out_dx", out_dtype=BF16)
    gw["w_ffn_out"] = _mm(act, dff, ta=True, name="mm_ffn_out_dw", out_dtype=BF16)
    (dgu,), _ = _rowwise_bwd(_fn_swiglu, [_win(gu, 0, FFN), _win(gu, FFN, FFN)], [], 0, [[dact]],
                             name="swiglu_bwd", tm=256, row_grad=[BF16, BF16], packed=True)
    dh3 = _mm(dgu, wt["w_ffn_in"], name="mm_ffn_in_dx", out_dtype=BF16)
    gw["w_ffn_in"] = _mm(dgu, h3, ta=True, name="mm_ffn_in_dw", out_dtype=BF16)
    (dx1, dco), (gw["g_post_cross"], gw["g_pre_ffn"]) = _rowwise_bwd(
        _fn_res_pre, [_win(x1), _win(co)], [g_post_cross, g_pre_ffn], 0, [[dx2], [dh3]],
        name="res_cross_bwd", tm=512, row_grad=[F32, BF16])
    dom = _mm(dco, wt["w_o_mem"], tb=True, name="mm_o_dx", out_dtype=BF16)
    gw["w_o_mem"] = _mm(om, dco, ta=True, name="mm_o_dw", out_dtype=BF16)
    dqm, dkvm = _xattn_bwd(qm, kvm, dom, seq, n_mem)
    dh2 = _mm(dqm, wt["w_q_mem"], tb=True, name="mm_q_dx", out_dtype=BF16)
    gw["w_q_mem"] = _mm(h2, dqm, ta=True, name="mm_q_dw", out_dtype=BF16)
    dmn = _mm(dkvm, wt["w_kv_mem"], tb=True, name="mm_kv_dx", out_dtype=BF16)
    gw["w_kv_mem"] = _mm(mn, dkvm, ta=True, name="mm_kv_dw", out_dtype=BF16)
    _, (gw["g_mem"],) = _rowwise_bwd(_fn_pre, [_win(mem)], [g_mem], 0, [[dmn]], name="pre_mem_bwd", tm=256,
                                     row_grad=[None])
    (dx0, dmo), (gw["g_post_mix"], gw["g_pre_cross"]) = _rowwise_bwd(
        _fn_res_pre, [_win(x), _win(mo)], [g_post_mix, g_pre_cross], 0, [[dx1], [dh2]],
        name="res_mix_bwd", tm=512, row_grad=[F32, BF16])
    dmixed = _mm(dmo, wt["w_out"], tb=True, name="mm_out_dx", out_dtype=BF16)
    gw["w_out"] = _mm(mixed, dmo, ta=True, name="mm_out_dw", out_dtype=BF16)
    (dzga, dzgb, dya_p, dyb_p), _ = _rowwise_bwd(_fn_mix, mix_rows, [], 0, [[dmixed]], name="gate_mix_bwd", tm=512,
                                                 row_grad=[BF16] * 4)
    gw["w_branch_a"] = _mm(y_a, dya_p, ta=True, name="mm_a_dw", out_dtype=BF16)
    gw["w_branch_b"] = _mm(y_b, dyb_p, ta=True, name="mm_b_dw", out_dtype=BF16)
    token = comm.send_early(gw)
    dy_a = _mm(dya_p, wt["w_branch_a"], tb=True, name="mm_a_dx", out_dtype=BF16, after=token)
    dy_b = _mm(dyb_p, wt["w_branch_b"], tb=True, name="mm_b_dx", out_dtype=BF16, after=token)
    dq, dk, dv, dbias = _attn_bwd(proj, bias, dy_b, seq)
    gw["rel_bias"] = _mm(dbias.reshape(N_HEADS, CHUNK * BAND), onehot, tb=True, name="mm_bias_dw", split_a=2)
    (dy, dr_p, dk2_p, dv_p, dg), (gw["lnx_w"], gw["lnx_b"], gw["bonus_scale"]) = _rowwise_bwd(
        _fn_post, post_rows, post_params, 2, [[dy_a]], name="rwkv_post_bwd", tm=512, row_grad=[BF16] * 5)
    dr_s, dlw, dk2_s, dv_s, dkk, da = _wkv_bwd(z_rkv, lw, k2, kk, a, states, invs, dy, seq)
    (dzk, dzw, dza, dzg), pg = _rowwise_bwd(
        _fn_prep, prep_rows, prep_params, 2, [[dlw], [dk2_p, dk2_s], [dkk], [da], [dg]],
        name="rwkv_prep_bwd", tm=512, row_grad=[BF16] * 4)
    gw["decay_base"], gd_up, gw["iclr_base"], gi_up, gg_up, gw["key_norm_scale"], gw["key_iclr_scale"] = pg
    gw["decay_up"], gw["iclr_up"], gw["gate_up"] = gd_up[:LORA_W], gi_up[:LORA_A], gg_up[:LORA_G]
    dp_r, gmix_r = _shift_bwd(proj, 0, D, mix_rkv[:, :D], [dr_p, dr_s], seq, name="shift_r_bwd")
    dp_k, gmix_k = _shift_bwd(proj, D, D, mix_rkv[:, D:2 * D], [dzk], seq, name="shift_k_bwd")
    dp_v, gmix_v = _shift_bwd(proj, 2 * D, D, mix_rkv[:, 2 * D:], [dv_p, dv_s], seq, name="shift_v_bwd")
    dp_lora, gmix_lora = _shift_bwd(proj, C_LORA, 512, mix_lora, [jnp.concatenate([dzw, dza, dzg], axis=1)], seq,
                                    name="shift_lora_bwd")
    gw["shift_mix"] = jnp.concatenate([gmix_r, gmix_k, gmix_v, _unpad_lora(gmix_lora)], axis=1)
    dproj = [dp_r, dp_k, dp_v, dq, dk, dv, dzga, dzgb, dp_lora]
    gw["w_in_p"] = _mm_cat_tn(dproj, h1, name="mm_in_dw", after=gw["rel_bias"])
    token = comm.send_late(gw)
    dh1 = _mm_cat_nn(dproj, w_in, name="mm_in_dx", after=token)
    (grad_x,), (gw["g_pre_mix"],) = _rowwise_bwd(_fn_pre, [_win(x)], [g_pre_mix], 0, [[dh1]], name="pre_mix_bwd",
                                                 tm=512, row_grad=[F32], add_to={0: dx0})
    return loss, grad_x, gw


_COL_SHARDED = ("w_in", "decay_up", "iclr_up", "gate_up", "w_o_mem", "w_ffn_in")
_ROW_SHARDED = ("w_branch_a", "w_branch_b", "w_out", "w_q_mem", "w_kv_mem", "w_ffn_out")
_TRANSPOSED = ("w_in", "w_ffn_in")
_FIRST = ("w_in", "decay_up", "iclr_up", "gate_up")
_REST = ("w_o_mem", "w_ffn_in", "w_branch_a", "w_branch_b", "w_out", "w_q_mem", "w_kv_mem", "w_ffn_out")
_REPLICATED = ("g_pre_mix", "g_post_mix", "shift_mix", "decay_base", "iclr_base", "key_norm_scale", "key_iclr_scale",
               "bonus_scale", "lnx_w", "lnx_b", "rel_bias", "g_pre_cross", "g_post_cross", "g_mem", "g_pre_ffn",
               "g_post_ffn")
_WEIGHTS = ("g_pre_mix", "g_post_mix", "w_in", "shift_mix", "decay_base", "decay_up", "iclr_base", "iclr_up", "gate_up",
            "key_norm_scale", "key_iclr_scale", "bonus_scale", "lnx_w", "lnx_b", "rel_bias", "w_branch_a", "w_branch_b",
            "w_out", "g_pre_cross", "g_post_cross", "g_mem", "w_q_mem", "w_kv_mem", "w_o_mem", "g_pre_ffn", "g_post_ffn",
            "w_ffn_in", "w_ffn_out")
_PACK_ROWS = 8 * ((sum({"shift_mix": 3360, "bonus_scale": 1024, "rel_bias": 3072}.get(n, D) for n in _REPLICATED)
                   + 1 + 8 * LANE - 1) // (8 * LANE))


def _pack(vals):
    flat = jnp.concatenate([v.reshape(-1).astype(F32) for v in vals])
    return jnp.pad(flat, (0, _PACK_ROWS * LANE - flat.shape[0])).reshape(_PACK_ROWS, LANE)


def _unpack(packed, shapes):
    flat, out, pos = packed.reshape(-1), [], 0
    for s in shapes:
        n = math.prod(s)
        out.append(flat[pos:pos + n].reshape(s))
        pos += n
    return out


def _step(args, seq, n_mem):
    names = ("x", "mem") + _WEIGHTS + ("loss_target",) + tuple("m_" + n for n in _WEIGHTS) + tuple("v_" + n for n in _WEIGHTS)
    given = dict(zip(names, args))
    nb = given["x"].shape[0]
    x = given["x"].reshape(nb * seq, D)
    mem = given["mem"].reshape(nb * n_mem, D)
    target = given["loss_target"].reshape(nb * seq, D)
    def local(name, prefix=""):
        a = given[prefix + name][0]
        return a.T if name in _TRANSPOSED else a

    shard = {n: local(n) for n in _COL_SHARDED + _ROW_SHARDED}
    stacked = _ROW_SHARDED + _TRANSPOSED
    out = {}

    def wire(name):
        return shard[name].astype(BF16)

    def full(name, g):
        return g.reshape(-1, g.shape[-1]) if name in stacked else _cols_to_full(g)

    def blocks_of(name, g):
        return (g.reshape((N_DEV,) + shard[name].shape) if name in stacked else _full_to_cols(g)).astype(BF16)

    def update(names, landed, after=None):
        done = []
        for n, parts in zip(names, landed):
            res = _adamw(parts, shard[n], local(n, "m_"), local(n, "v_"), name="adamw_" + n, after=after)
            for kind, r in zip(("grad_", "delta_", "new_m_", "new_v_"), res):
                out[kind + n] = (r.T if n in _TRANSPOSED else r)[None]
            done.append(res[0])
        return done


    class Exchanges:
        def __init__(self):
            srcs = [wire(n) for n in _FIRST]
            self.first, self.begun = _exchange_start(srcs, False, srcs[0], name="gather_first_start",
                                                     dists=_SIBLING_AND_SAME_CORE)

        def first_weights(self, after):
            got = _exchange_wait(self.first, after, [wire(n) for n in _FIRST], name="gather_first_wait")
            relayed = _relay_to_sibling(got, name="gather_first_relay")
            pos = _mesh_pos()
            for j, d in enumerate(_OTHER_CHIPS):
                slot = _flat(_peer(pos, d | 1))
                got = [lax.dynamic_update_slice_in_dim(g, r[j][None], slot, 0) for g, r in zip(got, relayed)]
            self.rest, self.first_token = _exchange_start(
                [wire(n) for n in _REST], False, got[0], name="gather_rest_start")
            first = {n: full(n, g) for n, g in zip(_FIRST, got)}
            first["w_in_p"] = _permute_in(first.pop("w_in"), 0)
            return first

        def late_weights(self, after):
            got = _exchange_wait(self.rest, [after], [wire(n) for n in _REST], name="gather_rest_wait")
            return {n: full(n, g) for n, g in zip(_REST, got)}

        def send_early(self, gw):
            self.early_blocks = [blocks_of(n, gw[n]) for n in _REST]
            self.early, token = _exchange_start(self.early_blocks, True, self.early_blocks[-1], name="scatter_rest_start")
            return token

        def send_late(self, gw):
            me = _flat(_mesh_pos())
            own = [lax.dynamic_index_in_dim(b, me, 0, keepdims=False) for b in self.early_blocks]
            landed = _exchange_wait(self.early, [gw["w_in_p"]], own, name="scatter_rest_wait")
            grads = {**gw, "w_in": _unpermute_in(gw["w_in_p"], 0)}
            self.late_blocks = [blocks_of(n, grads[n]) for n in _FIRST]
            self.late, token = _exchange_start(self.late_blocks, True, landed[0], name="scatter_first_start")
            self.updated = update(_REST, landed, after=token)
            return token

        def finish(self, after):
            me = _flat(_mesh_pos())
            own = [lax.dynamic_index_in_dim(b, me, 0, keepdims=False) for b in self.late_blocks]
            update(_FIRST, _exchange_wait(self.late, [*after, *self.updated], own, name="scatter_first_wait"))

    comm = Exchanges()
    wt = {n: given[n][0] for n in _REPLICATED}
    loss_tile, grad_x, gw = _local_step(x, mem, target, wt, seq, n_mem, comm)
    rep_shapes = [given[n].shape for n in _REPLICATED]
    packed, _ = lax.optimization_barrier((_pack([gw[n] for n in _REPLICATED] + [loss_tile[0, 0]]), tuple(comm.updated)))
    small = _exchange([packed], False, name="gather_small")[0]
    zero = jnp.zeros((), F32)
    res = _adamw(small, *[_pack([given[p + n] for n in _REPLICATED] + [zero]) for p in ("", "m_", "v_")],
                 name="adamw_small", tr=_PACK_ROWS)
    for kind, r in zip(("grad_", "delta_", "new_m_", "new_v_"), res):
        for n, val in zip(_REPLICATED, _unpack(r, rep_shapes)):
            out[kind + n] = val
    loss = res[0].reshape(-1)[sum(math.prod(s) for s in rep_shapes)]
    comm.finish([grad_x, res[0]])
    grad_x = grad_x.reshape(nb, seq, D)
    return (loss, grad_x, *[out[k + n] for k in ("grad_", "delta_", "new_m_", "new_v_") for n in _WEIGHTS])


def kernel(x, mem, g_pre_mix, g_post_mix, w_in, shift_mix, decay_base, decay_up, iclr_base, iclr_up, gate_up, key_norm_scale, key_iclr_scale, bonus_scale, lnx_w, lnx_b, rel_bias, w_branch_a, w_branch_b, w_out, g_pre_cross, g_post_cross, g_mem, w_q_mem, w_kv_mem, w_o_mem, g_pre_ffn, g_post_ffn, w_ffn_in, w_ffn_out, loss_target, m_g_pre_mix, m_g_post_mix, m_w_in, m_shift_mix, m_decay_base, m_decay_up, m_iclr_base, m_iclr_up, m_gate_up, m_key_norm_scale, m_key_iclr_scale, m_bonus_scale, m_lnx_w, m_lnx_b, m_rel_bias, m_w_branch_a, m_w_branch_b, m_w_out, m_g_pre_cross, m_g_post_cross, m_g_mem, m_w_q_mem, m_w_kv_mem, m_w_o_mem, m_g_pre_ffn, m_g_post_ffn, m_w_ffn_in, m_w_ffn_out, v_g_pre_mix, v_g_post_mix, v_w_in, v_shift_mix, v_decay_base, v_decay_up, v_iclr_base, v_iclr_up, v_gate_up, v_key_norm_scale, v_key_iclr_scale, v_bonus_scale, v_lnx_w, v_lnx_b, v_rel_bias, v_w_branch_a, v_w_branch_b, v_w_out, v_g_pre_cross, v_g_post_cross, v_g_mem, v_w_q_mem, v_w_kv_mem, v_w_o_mem, v_g_pre_ffn, v_g_post_ffn, v_w_ffn_in, v_w_ffn_out):
    args = (x, mem, g_pre_mix, g_post_mix, w_in, shift_mix, decay_base, decay_up, iclr_base, iclr_up, gate_up, key_norm_scale, key_iclr_scale, bonus_scale, lnx_w, lnx_b, rel_bias, w_branch_a, w_branch_b, w_out, g_pre_cross, g_post_cross, g_mem, w_q_mem, w_kv_mem, w_o_mem, g_pre_ffn, g_post_ffn, w_ffn_in, w_ffn_out, loss_target, m_g_pre_mix, m_g_post_mix, m_w_in, m_shift_mix, m_decay_base, m_decay_up, m_iclr_base, m_iclr_up, m_gate_up, m_key_norm_scale, m_key_iclr_scale, m_bonus_scale, m_lnx_w, m_lnx_b, m_rel_bias, m_w_branch_a, m_w_branch_b, m_w_out, m_g_pre_cross, m_g_post_cross, m_g_mem, m_w_q_mem, m_w_kv_mem, m_w_o_mem, m_g_pre_ffn, m_g_post_ffn, m_w_ffn_in, m_w_ffn_out, v_g_pre_mix, v_g_post_mix, v_w_in, v_shift_mix, v_decay_base, v_decay_up, v_iclr_base, v_iclr_up, v_gate_up, v_key_norm_scale, v_key_iclr_scale, v_bonus_scale, v_lnx_w, v_lnx_b, v_rel_bias, v_w_branch_a, v_w_branch_b, v_w_out, v_g_pre_cross, v_g_post_cross, v_g_mem, v_w_q_mem, v_w_kv_mem, v_w_o_mem, v_g_pre_ffn, v_g_post_ffn, v_w_ffn_in, v_w_ffn_out)
    return _step(args, x.shape[1], mem.shape[1])
```

```python
import functools
import math

import jax
import jax.numpy as jnp
from jax import lax
from jax.experimental import pallas as pl
from jax.experimental.pallas import tpu as pltpu

F32 = jnp.float32
BF16 = jnp.bfloat16

N_DEV = 8
D = 1024
HEAD = 64
N_HEADS = D // HEAD
LANE = 128
CHUNK = 64
LEFT = 8 * CHUNK
BAND = LEFT + CHUNK
REL_CLIP = 128
REL_TABLE = CHUNK + REL_CLIP
MEM_WIDTH = D // 2
MEM_HEADS = 4
FFN = 2816
LORA_W, LORA_A, LORA_G = 64, 64, 160
P_WIDTH = 3 * D + 3 * D + 2 * D + 128 + 128 + 256
C_Q, C_GA, C_LORA = 3 * D, 6 * D, 8 * D
NORM_EPS = 1e-6
GROUP_NORM_EPS = 64e-5
MASK_VALUE = -1e30
ADAM_LR, ADAM_B1, ADAM_B2, ADAM_EPS, ADAM_WD, ADAM_STEP = 0.001, 0.9, 0.999, 1e-08, 0.01, 10
VMEM_LIMIT = 56 * 1024 * 1024


def _cp(*sem):
    return pltpu.CompilerParams(dimension_semantics=sem, vmem_limit_bytes=VMEM_LIMIT)


_NN, _NT, _TN = ((1,), (0,)), ((1,), (1,)), ((0,), (0,))


def _dot_raw(a, b, dims):
    return lax.dot_general(a.astype(BF16), b.astype(BF16), (dims, ((), ())), preferred_element_type=F32)


@functools.partial(jax.custom_vjp, nondiff_argnums=(2,))
def _dot_dims(a, b, dims):
    return _dot_raw(a, b, dims)


def _dot_dims_fwd(a, b, dims):
    return _dot_raw(a, b, dims), (a, b)


def _dot_dims_bwd(dims, res, g):
    a, b = res
    if dims == _NN:
        da, db = _dot_raw(g, b, _NT), _dot_raw(a, g, _TN)
    elif dims == _NT:
        da, db = _dot_raw(g, b, _NN), _dot_raw(g, a, _TN)
    else:
        da, db = _dot_raw(b, g, _NT), _dot_raw(a, g, _NN)
    return da.astype(a.dtype), db.astype(b.dtype)


_dot_dims.defvjp(_dot_dims_fwd, _dot_dims_bwd)


def _dot(a, b, dims=_NN):
    return _dot_dims(a, b, dims)


def _dot_nt(a, b):
    return _dot_dims(a, b, _NT)


def _dot_tn(a, b):
    return _dot_dims(a, b, _TN)


def _split(x, terms):
    parts, rest = [], x.astype(F32)
    for _ in range(terms):
        p = rest.astype(BF16)
        parts.append(p)
        rest = rest - p.astype(F32)
    return parts


def _dot_split_a(a, b, terms=2):
    out = None
    for p in _split(a, terms):
        t = _dot(p, b)
        out = t if out is None else out + t
    return out


def _dot_split_b(a, b, terms=3):
    out = None
    for p in _split(b, terms):
        t = _dot(a, p)
        out = t if out is None else out + t
    return out


MM_VMEM_BUDGET = 30 * 1024 * 1024
MM_HBM_BPS = 3.2e12
MM_MXU_FPS = 8.5e14
MM_STEP_S = 0.35e-6


def _divisors(n, align, cap):
    out = [d for d in range(align, min(n, cap) + 1, align) if n % d == 0]
    return out or [n]


def _mm_tiles(m, n, k, ea, eb, eo, ta):
    best = None
    for tm in _divisors(m, LANE if ta else 8, 2048):
        for tn in _divisors(n, LANE, 2048):
            for tk in _divisors(k, LANE, 2048):
                nk = k // tk
                vmem = 2 * (tm * tk * ea + tk * tn * eb + tm * tn * eo) + (tm * tn * 4 if nk > 1 else 0)
                if vmem > MM_VMEM_BUDGET:
                    continue
                dma = (tm * tk * ea if (nk > 1 or n // tn == 1) else tm * tk * ea * tn / n) + tk * tn * eb + tm * tn * eo / nk
                step = max(2.0 * tm * tn * tk / MM_MXU_FPS, dma / MM_HBM_BPS) + MM_STEP_S
                edges = (tm * tk * ea + tk * tn * eb + tm * tn * eo) / MM_HBM_BPS
                cost = (m // tm) * (n // tn) * nk * step + edges
                if best is None or cost < best[0]:
                    best = (cost, tm, tn, tk)
    return best[1:]


def _mm(a, b, *, name, ta=False, tb=False, out_dtype=F32, tm=None, tn=None, tk=None, split_a=1, after=None):
    m, k = (a.shape[1], a.shape[0]) if ta else a.shape
    n, kb = (b.shape[0], b.shape[1]) if tb else (b.shape[1], b.shape[0])
    assert k == kb, (a.shape, b.shape, ta, tb)
    if tm is None:
        tm, tn, tk = _mm_tiles(m, n, k, a.dtype.itemsize, b.dtype.itemsize, jnp.dtype(out_dtype).itemsize, ta)
    assert m % tm == 0 and n % tn == 0 and k % tk == 0, (m, n, k, tm, tn, tk)
    nk = k // tk
    dims = ((0 if ta else 1,), (1 if tb else 0,))

    n_after = 0 if after is None else 1

    def body(a_ref, b_ref, *rest):
        o_ref, scratch = rest[n_after], rest[n_after + 1:]
        prod = None
        for p in _split(a_ref[...], split_a) if split_a > 1 else [a_ref[...]]:
            t = _dot_raw(p, b_ref[...], dims)
            prod = t if prod is None else prod + t
        if nk == 1:
            o_ref[...] = prod.astype(o_ref.dtype)
            return
        acc_ref, kk = scratch[0], pl.program_id(2)

        @pl.when(kk == 0)
        def _():
            acc_ref[...] = prod

        @pl.when(kk > 0)
        def _():
            acc_ref[...] += prod

        @pl.when(kk == nk - 1)
        def _():
            o_ref[...] = acc_ref[...].astype(o_ref.dtype)

    a_spec = pl.BlockSpec((tk, tm), lambda i, j, q: (q, i)) if ta else pl.BlockSpec((tm, tk), lambda i, j, q: (i, q))
    b_spec = pl.BlockSpec((tn, tk), lambda i, j, q: (j, q)) if tb else pl.BlockSpec((tk, tn), lambda i, j, q: (q, j))
    return pl.pallas_call(
        body, name=name, grid=(m // tm, n // tn, nk),
        in_specs=[a_spec, b_spec] + [pl.BlockSpec(memory_space=pl.ANY)] * n_after,
        out_specs=pl.BlockSpec((tm, tn), lambda i, j, q: (i, j)),
        out_shape=jax.ShapeDtypeStruct((m, n), out_dtype),
        scratch_shapes=[pltpu.VMEM((tm, tn), F32)] if nk > 1 else [],
        compiler_params=_cp("parallel", "parallel", "arbitrary"),
    )(a, b, *([] if after is None else [after]))


def _piece_steps(pieces, tile):
    counts = [p.shape[1] // tile for p in pieces]
    assert all(p.shape[1] % tile == 0 for p in pieces)
    return [(sum(counts[:i]), c) for i, c in enumerate(counts)], sum(counts)


def _mm_cat_nn(pieces, w, *, name, after=None, tm=2048, tk=256):
    t, n = pieces[0].shape[0], w.shape[1]
    tm = min(tm, t)
    spans, nk = _piece_steps(pieces, tk)
    npc = len(pieces)
    n_after = 0 if after is None else 1

    def body(*refs):
        w_ref, o_ref, acc_ref = refs[npc], refs[npc + 1 + n_after], refs[npc + 2 + n_after]
        q = pl.program_id(1)

        @pl.when(q == 0)
        def _():
            acc_ref[...] = jnp.zeros_like(acc_ref)

        for p_ref, (first, count) in zip(refs[:npc], spans):
            @pl.when(jnp.logical_and(q >= first, q < first + count))
            def _(p_ref=p_ref):
                acc_ref[...] += _dot_raw(p_ref[...], w_ref[...], _NN)

        @pl.when(q == nk - 1)
        def _():
            o_ref[...] = acc_ref[...].astype(o_ref.dtype)

    def piece_spec(first, count):
        return pl.BlockSpec((tm, tk), lambda i, q: (i, jnp.clip(q - first, 0, count - 1)))

    return pl.pallas_call(
        body, name=name, grid=(t // tm, nk),
        in_specs=[piece_spec(*s) for s in spans] + [pl.BlockSpec((tk, n), lambda i, q: (q, 0))]
        + [pl.BlockSpec(memory_space=pl.ANY)] * n_after,
        out_specs=pl.BlockSpec((tm, n), lambda i, q: (i, 0)),
        out_shape=jax.ShapeDtypeStruct((t, n), BF16),
        scratch_shapes=[pltpu.VMEM((tm, n), F32)],
        compiler_params=_cp("parallel", "arbitrary"),
    )(*pieces, w, *([] if after is None else [after]))


def _mm_cat_tn(pieces, a, *, name, after=None, tk=1024, tn=512):
    t, m = a.shape
    tk = min(tk, t)
    spans, nj = _piece_steps(pieces, tn)
    npc, nk = len(pieces), t // tk
    n_after = 0 if after is None else 1

    def body(a_ref, *refs):
        o_ref, acc_ref = refs[npc + n_after], refs[npc + 1 + n_after]
        j, q = pl.program_id(0), pl.program_id(1)

        @pl.when(q == 0)
        def _():
            acc_ref[...] = jnp.zeros_like(acc_ref)

        for p_ref, (first, count) in zip(refs[:npc], spans):
            @pl.when(jnp.logical_and(j >= first, j < first + count))
            def _(p_ref=p_ref):
                acc_ref[...] += _dot_raw(p_ref[...], a_ref[...], _TN)

        @pl.when(q == nk - 1)
        def _():
            o_ref[...] = acc_ref[...].astype(o_ref.dtype)

    def piece_spec(first, count):
        def index(j, q):
            mine = jnp.logical_and(j >= first, j < first + count)
            return jnp.where(mine, q, 0), jnp.clip(j - first, 0, count - 1)
        return pl.BlockSpec((tk, tn), index)

    return pl.pallas_call(
        body, name=name, grid=(nj, nk),
        in_specs=[pl.BlockSpec((tk, m), lambda j, q: (q, 0))] + [piece_spec(*s) for s in spans]
        + [pl.BlockSpec(memory_space=pl.ANY)] * n_after,
        out_specs=pl.BlockSpec((tn, m), lambda j, q: (j, 0)),
        out_shape=jax.ShapeDtypeStruct((nj * tn, m), BF16),
        scratch_shapes=[pltpu.VMEM((tn, m), F32)],
        compiler_params=_cp("parallel", "arbitrary"),
    )(a, *pieces, *([] if after is None else [after]))


def _win(arr, start=0, width=None):
    width = arr.shape[1] if width is None else width
    assert start % width == 0
    return (arr, start // width, width)


def _row_specs(rows, tm):
    return [pl.BlockSpec((tm, w), functools.partial(lambda i, cb: (i, cb), cb=cb)) for (_, cb, w) in rows]


def _full_spec(p):
    nd = p.ndim
    return pl.BlockSpec(p.shape, lambda i, nd=nd: (0,) * nd)


def _rowwise(fn, rows, params, outs, *, name, tm, after=None):
    t = rows[0][0].shape[0]
    tm = min(tm, t)
    assert t % tm == 0
    nr, npar = len(rows), len(params)
    n_after = 0 if after is None else 1

    def body(*refs):
        vals = [r[...] for r in refs[:nr + npar]]
        res = fn(*vals)
        for o_ref, r in zip(refs[nr + npar + n_after:], res):
            o_ref[...] = r.astype(o_ref.dtype)

    return pl.pallas_call(
        body, name=name, grid=(t // tm,),
        in_specs=_row_specs(rows, tm) + [_full_spec(p) for p in params] + [pl.BlockSpec(memory_space=pl.ANY)] * n_after,
        out_specs=[pl.BlockSpec((tm, w), lambda i: (i, 0)) for (w, _) in outs],
        out_shape=[jax.ShapeDtypeStruct((t, w), dt) for (w, dt) in outs],
        compiler_params=_cp("parallel"),
    )(*[r[0] for r in rows], *params, *([] if after is None else [after]))


def _rowwise_bwd(fn, rows, params, n_const, cots, *, name, tm, row_grad, add_to=None, packed=False):
    t = rows[0][0].shape[0]
    tm = min(tm, t)
    assert t % tm == 0
    nr, npar = len(rows), len(params)
    ndp = npar - n_const
    add_to = add_to or {}
    add_idx = sorted(add_to)
    flat_cots = [c for group in cots for c in group]
    kept = [i for i in range(nr) if row_grad[i] is not None]

    def body(*refs):
        pos = 0
        row_v = [r[...] for r in refs[pos:pos + nr]]; pos += nr
        par_v = [r[...] for r in refs[pos:pos + npar]]; pos += npar
        cot_v = [r[...] for r in refs[pos:pos + len(flat_cots)]]; pos += len(flat_cots)
        add_v = [r[...] for r in refs[pos:pos + len(add_idx)]]; pos += len(add_idx)
        if packed:
            offs = [sum(rows[i][2] for i in kept[:q]) for q in range(len(kept))]
            rg_refs = [refs[pos].at[:, o:o + rows[i][2]] for o, i in zip(offs, kept)]; pos += 1
        else:
            rg_refs = refs[pos:pos + len(kept)]; pos += len(kept)
        pg_refs = refs[pos:pos + ndp]

        consts = par_v[ndp:]
        res, vjp = jax.vjp(lambda *args: tuple(fn(*args, *consts)), *row_v, *par_v[:ndp])
        cot_in, q = [], 0
        for j, group in enumerate(cots):
            c = None
            for _ in group:
                cv = cot_v[q].astype(F32); q += 1
                c = cv if c is None else c + cv
            c = jnp.zeros(res[j].shape, F32) if c is None else c
            cot_in.append(c.astype(res[j].dtype))
        grads = vjp(tuple(cot_in))
        for ref, i in zip(rg_refs, kept):
            g = grads[i].astype(F32)
            if i in add_to:
                g = g + add_v[add_idx.index(i)].astype(F32)
            ref[...] = g.astype(ref.dtype)

        @pl.when(pl.program_id(0) == 0)
        def _():
            for ref in pg_refs:
                ref[...] = jnp.zeros_like(ref)

        for ref, g in zip(pg_refs, grads[nr:]):
            ref[...] += g.astype(F32)

    cot_specs = [pl.BlockSpec((tm, c.shape[1]), lambda i: (i, 0)) for c in flat_cots]
    add_specs = [pl.BlockSpec((tm, add_to[i].shape[1]), lambda i_: (i_, 0)) for i in add_idx]
    widths = [sum(rows[i][2] for i in kept)] if packed else [rows[i][2] for i in kept]
    n_rg = len(widths)
    out_specs = [pl.BlockSpec((tm, w), lambda i_: (i_, 0)) for w in widths] + [_full_spec(p) for p in params[:ndp]]
    out_shape = [jax.ShapeDtypeStruct((t, w), row_grad[kept[q]]) for q, w in enumerate(widths)] + [
        jax.ShapeDtypeStruct(p.shape, F32) for p in params[:ndp]]
    res = pl.pallas_call(
        body, name=name, grid=(t // tm,),
        in_specs=_row_specs(rows, tm) + [_full_spec(p) for p in params] + cot_specs + add_specs,
        out_specs=out_specs, out_shape=out_shape,
        compiler_params=_cp("arbitrary"),
    )(*[r[0] for r in rows], *params, *flat_cots, *[add_to[i] for i in add_idx])
    return list(res[:n_rg]), list(res[n_rg:])


def _rms(x, g):
    xf = x.astype(F32)
    return xf * lax.rsqrt(jnp.mean(xf * xf, axis=-1, keepdims=True) + NORM_EPS) * g


def _softplus(x):
    return jnp.maximum(x, 0.0) + jnp.log(1.0 + jnp.exp(-jnp.abs(x)))


def _fn_pre(x, g):
    return (_rms(x, g).astype(BF16),)


def _fn_res(x, u, g_post):
    return (x + _rms(u, g_post),)


def _fn_res_pre(x, u, g_post, g_pre):
    xn = x + _rms(u, g_post)
    return xn, _rms(xn, g_pre).astype(BF16)


def _fn_mix(zga, zgb, ya, yb):
    return ((jax.nn.sigmoid(zga) * ya + jax.nn.sigmoid(zgb) * yb).astype(BF16),)


def _fn_swiglu(gate, up):
    gate, up = gate.astype(F32), up.astype(F32)
    return ((gate * jax.nn.sigmoid(gate) * up).astype(BF16),)


def _fn_prep(zk, zw, za, zg, decay_base, d_up, iclr_base, i_up, g_up, kns, kis, e_hd, e_dh):
    w_log = -_softplus(-(decay_base + _dot(jnp.tanh(zw), d_up))) - 0.5
    lw = -jnp.exp(w_log)
    a = jax.nn.sigmoid(iclr_base + _dot(za, i_up))
    g = _dot(jax.nn.sigmoid(zg), g_up)
    kn = zk * kns
    ss = _dot(kn * kn, e_dh)
    inv = lax.rsqrt(jnp.maximum(ss, 1e-24))
    kk = kn * _dot_split_a(inv, e_hd)
    k2 = zk * (1.0 + (a - 1.0) * kis)
    return lw, k2, kk, a, g


def _fn_post(y, r, k2, v, g, lnx_w, lnx_b, bonus, e_hd, e_dh):
    mu = _dot_split_a(_dot(y, e_dh) * (1.0 / HEAD), e_hd)
    yc = y - mu
    var = _dot(yc * yc, e_dh) * (1.0 / HEAD)
    yn = yc * _dot_split_a(lax.rsqrt(var + GROUP_NORM_EPS), e_hd)
    bs = _dot_split_a(_dot(r * k2 * bonus, e_dh), e_hd)
    return (((yn * lnx_w + lnx_b + bs * v) * g).astype(BF16),)


def _shift_fwd(p, col0, ncols, mix, seq, *, name, cw=256):
    t = p.shape[0]
    assert col0 % cw == 0 and ncols % cw == 0 and t % seq == 0
    cb0 = col0 // cw

    def body(p_ref, m_ref, z_ref):
        pv = p_ref[...]
        row = lax.broadcasted_iota(jnp.int32, pv.shape, 0)
        prev = jnp.where(row == 0, 0.0, pltpu.roll(pv, 1, axis=0))
        z_ref[...] = pv + (prev - pv) * m_ref[...]

    return pl.pallas_call(
        body, name=name, grid=(t // seq, ncols // cw),
        in_specs=[pl.BlockSpec((seq, cw), lambda b, c: (b, c + cb0)), pl.BlockSpec((1, cw), lambda b, c: (0, c))],
        out_specs=pl.BlockSpec((seq, cw), lambda b, c: (b, c)),
        out_shape=jax.ShapeDtypeStruct((t, ncols), F32),
        compiler_params=_cp("parallel", "parallel"),
    )(p, mix)


def _shift_bwd(p, col0, ncols, mix, dz_parts, seq, *, name, cw=256):
    t = p.shape[0]
    cb0 = col0 // cw
    n = len(dz_parts)

    def body(*refs):
        p_ref, m_ref = refs[:2]
        dp_ref, dm_ref = refs[2 + n:]
        dz = refs[2][...].astype(F32)
        for r in refs[3:2 + n]:
            dz = dz + r[...].astype(F32)
        pv = p_ref[...]
        mixv = m_ref[...]
        row = lax.broadcasted_iota(jnp.int32, pv.shape, 0)
        prev = jnp.where(row == 0, 0.0, pltpu.roll(pv, 1, axis=0))
        u = dz * mixv
        nxt = jnp.where(row == seq - 1, 0.0, pltpu.roll(u, seq - 1, axis=0))
        dp_ref[...] = (dz - u + nxt).astype(dp_ref.dtype)

        @pl.when(pl.program_id(1) == 0)
        def _():
            dm_ref[...] = jnp.zeros_like(dm_ref)

        dm_ref[...] += jnp.sum(dz * (prev - pv), axis=0, keepdims=True)

    return pl.pallas_call(
        body, name=name, grid=(ncols // cw, t // seq),
        in_specs=[pl.BlockSpec((seq, cw), lambda c, b: (b, c + cb0)), pl.BlockSpec((1, cw), lambda c, b: (0, c))]
        + [pl.BlockSpec((seq, cw), lambda c, b: (b, c))] * n,
        out_specs=[pl.BlockSpec((seq, cw), lambda c, b: (b, c)), pl.BlockSpec((1, cw), lambda c, b: (0, c))],
        out_shape=[jax.ShapeDtypeStruct((t, ncols), BF16), jax.ShapeDtypeStruct((1, ncols), F32)],
        compiler_params=_cp("parallel", "arbitrary"),
    )(p, mix, *dz_parts)


def _each(f, *lists):
    return [f(*xs) for xs in zip(*lists)]


def _tri_inv(low):
    c = low[0].shape[0]
    ti = lax.broadcasted_iota(jnp.int32, (c, c), 0)
    si = lax.broadcasted_iota(jnp.int32, (c, c), 1)
    eye = (ti == si).astype(F32)
    inside = (ti // 4) == (si // 4)
    base = [jnp.where(inside, m, 0.0) for m in low]
    acc = _each(lambda m: _dot(eye - m, eye + _dot(m, m)), base)
    size = 8
    while size <= c:
        wider = (ti // size) == (si // size)
        keep = jnp.logical_and(wider, jnp.logical_not(inside))
        acc = _each(lambda p, m: p - _dot(_dot(p, jnp.where(keep, m, 0.0)), p), acc, low)
        inside, size = wider, size * 2
    return acc


def _stack_rows(a, b):
    return jnp.concatenate([a, b], axis=0)


@jax.custom_vjp
def _split_rows(x):
    h = x.shape[0] // 2
    return x[:h], x[h:]


def _split_rows_fwd(x):
    return _split_rows(x), None


def _split_rows_bwd(_, g):
    return (jnp.concatenate(g, axis=0),)


_split_rows.defvjp(_split_rows_fwd, _split_rows_bwd)


def _masked_halves(stacked, top_mask, bottom_mask):
    halves = _each(_split_rows, stacked)
    return ([jnp.where(top_mask, t, 0.0) for t, _ in halves], [jnp.where(bottom_mask, b, 0.0) for _, b in halves])


@jax.custom_vjp
def _tri_inv_known(low, inv):
    return inv


def _tri_inv_known_fwd(low, inv):
    return inv, inv


def _tri_inv_known_bwd(inv, g):
    dlow = _each(lambda t, gg: -_dot(_dot(t, gg, _TN), t, _NT), inv, g)
    return dlow, _each(jnp.zeros_like, inv)


_tri_inv_known.defvjp(_tri_inv_known_fwd, _tri_inv_known_bwd)


def _wkv_chunk(s0, r, lw, k, v, kk, a, inv=None):
    c = r[0].shape[0]
    ti = lax.broadcasted_iota(jnp.int32, (c, c), 0)
    si = lax.broadcasted_iota(jnp.int32, (c, c), 1)
    incl, strict = ti >= si, ti > si
    tri = incl.astype(F32)
    cum = _each(lambda x: _dot_split_b(tri, x, 3), lw)
    eg = _each(jnp.exp, cum)
    egp = _each(lambda cs, x: jnp.exp(cs - x), cum, lw)
    ei = _each(lambda cs: jnp.exp(-cs), cum)
    rh, kkh, kt = _each(jnp.multiply, r, eg), _each(jnp.multiply, kk, egp), _each(jnp.multiply, k, ei)
    bt = _each(lambda p, q, e: (p * q) * e, a, kk, ei)
    both = _each(_stack_rows, kkh, rh)
    on_b, on_k, on_s = _each(_dot_nt, both, bt), _each(_dot_nt, both, kt), _each(_dot_nt, both, s0)
    lb, mb = _masked_halves(on_b, strict, incl)
    lk, mk = _masked_halves(on_k, strict, incl)
    on_s = _each(_split_rows, on_s)
    on_v = _each(lambda p, q, x: _split_rows(_dot(_stack_rows(p, q), x)), lk, mk, v)
    rhs = _each(lambda p, q: p[0] + q[0], on_s, on_v)
    inv = _tri_inv(lb) if inv is None else _tri_inv_known(lb, inv)
    u = _each(lambda t, x: -_dot(t, x), inv, rhs)
    y = _each(lambda p, m1, uu, q: p[1] + _dot(m1, uu) + q[1], on_s, mb, u, on_v)
    s1 = _each(lambda s, uu, x, b, kq, w: (s + _dot_tn(_stack_rows(uu, x), _stack_rows(b, kq)))
               * jnp.exp(jnp.sum(w, axis=0, keepdims=True)), s0, u, v, bt, kt, lw)
    return y, s1, inv


WKV_HEADS = 16
WKV_COLS = WKV_HEADS * HEAD
WKV_GROUPS = N_HEADS // WKV_HEADS


def _head_cols(ref):
    return [ref[:, h * HEAD:(h + 1) * HEAD] for h in range(ref.shape[1] // HEAD)]


def _wkv_specs(seq, rev):
    nc = seq // CHUNK

    def rows(col0):
        cb0 = col0 // WKV_COLS
        if rev:
            return pl.BlockSpec((CHUNK, WKV_COLS), lambda b, h, c: (b * nc + nc - 1 - c, cb0 + h))
        return pl.BlockSpec((CHUNK, WKV_COLS), lambda b, h, c: (b * nc + c, cb0 + h))

    if rev:
        st = pl.BlockSpec((1, 1, WKV_HEADS, HEAD, HEAD), lambda b, h, c: (b * WKV_GROUPS + h, nc - 1 - c, 0, 0, 0))
    else:
        st = pl.BlockSpec((1, 1, WKV_HEADS, HEAD, HEAD), lambda b, h, c: (b * WKV_GROUPS + h, c, 0, 0, 0))
    return rows, st


def _wkv_fwd(z_rkv, lw, k2, kk, a, seq):
    t = z_rkv.shape[0]
    nb, nc = t // seq, seq // CHUNK
    rows, st = _wkv_specs(seq, False)

    def body(r_ref, v_ref, lw_ref, k_ref, kk_ref, a_ref, y_ref, st_ref, inv_ref, s_scr):
        @pl.when(pl.program_id(2) == 0)
        def _():
            s_scr[...] = jnp.zeros_like(s_scr)

        s0 = [s_scr[h] for h in range(WKV_HEADS)]
        y, s1, inv = _wkv_chunk(s0, *[_head_cols(ref) for ref in (r_ref, lw_ref, k_ref, v_ref, kk_ref, a_ref)])
        for h in range(WKV_HEADS):
            st_ref[0, 0, h] = s0[h]
            inv_ref[0, 0, h] = inv[h]
            y_ref[:, h * HEAD:(h + 1) * HEAD] = y[h]
            s_scr[h] = s1[h]

    per_chunk = jax.ShapeDtypeStruct((nb * WKV_GROUPS, nc, WKV_HEADS, HEAD, HEAD), F32)
    return pl.pallas_call(
        body, name="wkv_fwd", grid=(nb, WKV_GROUPS, nc),
        in_specs=[rows(0), rows(2 * D), rows(0), rows(0), rows(0), rows(0)],
        out_specs=[rows(0), st, st],
        out_shape=[jax.ShapeDtypeStruct((t, D), F32), per_chunk, per_chunk],
        scratch_shapes=[pltpu.VMEM((WKV_HEADS, HEAD, HEAD), F32)],
        compiler_params=_cp("parallel", "parallel", "arbitrary"),
    )(z_rkv, z_rkv, lw, k2, kk, a)


def _wkv_bwd(z_rkv, lw, k2, kk, a, states, invs, dy, seq):
    t = z_rkv.shape[0]
    nb, nc = t // seq, seq // CHUNK
    rows, st = _wkv_specs(seq, True)

    def body(r_ref, v_ref, lw_ref, k_ref, kk_ref, a_ref, st_ref, inv_ref, dy_ref,
             dr_ref, dlw_ref, dk_ref, dv_ref, dkk_ref, da_ref, ds_scr):
        @pl.when(pl.program_id(2) == 0)
        def _():
            ds_scr[...] = jnp.zeros_like(ds_scr)

        s0 = [st_ref[0, 0, h] for h in range(WKV_HEADS)]
        inv = [inv_ref[0, 0, h] for h in range(WKV_HEADS)]
        _, vjp = jax.vjp(lambda *args: _wkv_chunk(*args, inv=inv)[:2],
                         s0, *[_head_cols(ref) for ref in (r_ref, lw_ref, k_ref, v_ref, kk_ref, a_ref)])
        grads = vjp(([x.astype(F32) for x in _head_cols(dy_ref)], [ds_scr[h] for h in range(WKV_HEADS)]))
        for h in range(WKV_HEADS):
            ds_scr[h] = grads[0][h]
            for ref, g in zip((dr_ref, dlw_ref, dk_ref, dv_ref, dkk_ref, da_ref), grads[1:]):
                ref[:, h * HEAD:(h + 1) * HEAD] = g[h].astype(ref.dtype)

    return pl.pallas_call(
        body, name="wkv_bwd", grid=(nb, WKV_GROUPS, nc),
        in_specs=[rows(0), rows(2 * D), rows(0), rows(0), rows(0), rows(0), st, st, rows(0)],
        out_specs=[rows(0)] * 6,
        out_shape=[jax.ShapeDtypeStruct((t, D), BF16)] * 6,
        scratch_shapes=[pltpu.VMEM((WKV_HEADS, HEAD, HEAD), F32)],
        compiler_params=_cp("parallel", "parallel", "arbitrary"),
    )(z_rkv, z_rkv, lw, k2, kk, a, states, invs, dy)


def _softmax(s):
    e = jnp.exp(s - jnp.max(s, axis=-1, keepdims=True))
    return e * (1.0 / jnp.sum(e, axis=-1, keepdims=True))


ATT_FWD_HEADS = 16
ATT_HEADS = 8
ATT_COLS = ATT_HEADS * HEAD
ATT_GROUPS = N_HEADS // ATT_HEADS


def _attn_chunk(q, kb, vb, bias, valid):
    s = _each(lambda x, y, z: jnp.where(valid, _dot_nt(x * (HEAD ** -0.5), y) + z, MASK_VALUE), q, kb, bias)
    return _each(_dot, _each(_softmax, s), vb)


def _pad_fill(pad_ref, src_ref):
    pad_ref[0:LEFT, :] = jnp.zeros((LEFT, pad_ref.shape[1]), pad_ref.dtype)
    pad_ref[LEFT:, :] = src_ref[...].astype(pad_ref.dtype)


def _band_heads(pad_ref, start):
    return [pad_ref[pl.ds(start, BAND), h * HEAD:(h + 1) * HEAD].astype(F32) for h in range(pad_ref.shape[1] // HEAD)]


def _band_valid(c):
    return (c * CHUNK - LEFT + lax.broadcasted_iota(jnp.int32, (1, BAND), 1)) >= 0


def _bias_spec():
    return pl.BlockSpec((ATT_HEADS, CHUNK, BAND), lambda h, b, c: (h, 0, 0))


def _attn_fwd(proj, bias, seq):
    t = proj.shape[0]
    nb, nc = t // seq, seq // CHUNK
    heads = ATT_FWD_HEADS
    cols, groups = heads * HEAD, N_HEADS // heads
    cq = C_Q // cols

    def body(q_ref, k_ref, v_ref, b_ref, o_ref, kpad, vpad):
        c = pl.program_id(2)

        @pl.when(c == 0)
        def _():
            _pad_fill(kpad, k_ref)
            _pad_fill(vpad, v_ref)

        start = pl.multiple_of(c * CHUNK, CHUNK)
        o = _attn_chunk(_head_cols(q_ref), _band_heads(kpad, start), _band_heads(vpad, start),
                        [b_ref[h] for h in range(heads)], _band_valid(c))
        for h in range(heads):
            o_ref[:, h * HEAD:(h + 1) * HEAD] = o[h].astype(o_ref.dtype)

    return pl.pallas_call(
        body, name="attn_fwd", grid=(groups, nb, nc),
        in_specs=[pl.BlockSpec((CHUNK, cols), lambda h, b, c: (b * nc + c, cq + h)),
                  pl.BlockSpec((seq, cols), lambda h, b, c: (b, cq + groups + h)),
                  pl.BlockSpec((seq, cols), lambda h, b, c: (b, cq + 2 * groups + h)),
                  pl.BlockSpec((heads, CHUNK, BAND), lambda h, b, c: (h, 0, 0))],
        out_specs=pl.BlockSpec((CHUNK, cols), lambda h, b, c: (b * nc + c, h)),
        out_shape=jax.ShapeDtypeStruct((t, D), BF16),
        scratch_shapes=[pltpu.VMEM((seq + LEFT, cols), BF16)] * 2,
        compiler_params=_cp("parallel", "arbitrary", "arbitrary"),
    )(proj, proj, proj, bias)


def _attn_bwd(proj, bias, do, seq):
    t = proj.shape[0]
    nb, nc = t // seq, seq // CHUNK
    cq = C_Q // ATT_COLS

    def body(q_ref, k_ref, v_ref, b_ref, do_ref, dq_ref, dk_ref, dv_ref, db_ref, kpad, vpad, dkpad, dvpad):
        b, c = pl.program_id(1), pl.program_id(2)

        @pl.when(c == 0)
        def _():
            _pad_fill(kpad, k_ref)
            _pad_fill(vpad, v_ref)
            dkpad[...] = jnp.zeros_like(dkpad)
            dvpad[...] = jnp.zeros_like(dvpad)

        @pl.when(jnp.logical_and(b == 0, c == 0))
        def _():
            db_ref[...] = jnp.zeros_like(db_ref)

        start = pl.multiple_of(c * CHUNK, CHUNK)
        _, vjp = jax.vjp(functools.partial(_attn_chunk, valid=_band_valid(c)),
                         _head_cols(q_ref), _band_heads(kpad, start), _band_heads(vpad, start),
                         [b_ref[h] for h in range(ATT_HEADS)])
        dq, dkb, dvb, dbias = vjp([x.astype(F32) for x in _head_cols(do_ref)])
        for h in range(ATT_HEADS):
            sl = slice(h * HEAD, (h + 1) * HEAD)
            dq_ref[:, sl] = dq[h].astype(dq_ref.dtype)
            dkpad[pl.ds(start, BAND), sl] += dkb[h].astype(F32)
            dvpad[pl.ds(start, BAND), sl] += dvb[h].astype(F32)
            db_ref[h] += dbias[h]

        @pl.when(c == nc - 1)
        def _():
            dk_ref[...] = dkpad[LEFT:, :].astype(dk_ref.dtype)
            dv_ref[...] = dvpad[LEFT:, :].astype(dv_ref.dtype)

    kv_out = pl.BlockSpec((seq, ATT_COLS), lambda h, b, c: (b, h))
    return pl.pallas_call(
        body, name="attn_bwd", grid=(ATT_GROUPS, nb, nc),
        in_specs=[pl.BlockSpec((CHUNK, ATT_COLS), lambda h, b, c: (b * nc + c, cq + h)),
                  pl.BlockSpec((seq, ATT_COLS), lambda h, b, c: (b, cq + ATT_GROUPS + h)),
                  pl.BlockSpec((seq, ATT_COLS), lambda h, b, c: (b, cq + 2 * ATT_GROUPS + h)),
                  _bias_spec(),
                  pl.BlockSpec((CHUNK, ATT_COLS), lambda h, b, c: (b * nc + c, h))],
        out_specs=[pl.BlockSpec((CHUNK, ATT_COLS), lambda h, b, c: (b * nc + c, h)), kv_out, kv_out,
                   pl.BlockSpec((ATT_HEADS, CHUNK, BAND), lambda h, b, c: (h, 0, 0))],
        out_shape=[jax.ShapeDtypeStruct((t, D), BF16)] * 3 + [jax.ShapeDtypeStruct((N_HEADS, CHUNK, BAND), F32)],
        scratch_shapes=[pltpu.VMEM((seq + LEFT, ATT_COLS), BF16)] * 2 + [pltpu.VMEM((seq + LEFT, ATT_COLS), F32)] * 2,
        compiler_params=_cp("parallel", "arbitrary", "arbitrary"),
    )(proj, proj, proj, bias, do)


def _xattn_tile(q, k, v):
    s = _dot_nt(q, k) * ((MEM_WIDTH // MEM_HEADS) ** -0.5)
    return _dot(_softmax(s), v)


def _xattn_fwd(qm, kvm, seq, n_mem, tq=1024):
    t = qm.shape[0]
    tq = min(tq, seq)
    nb, nq = t // seq, seq // tq

    def body(q_ref, k_ref, v_ref, o_ref):
        o_ref[...] = _xattn_tile(q_ref[...], k_ref[...], v_ref[...]).astype(o_ref.dtype)

    return pl.pallas_call(
        body, name="xattn_fwd", grid=(nb, MEM_HEADS, nq),
        in_specs=[pl.BlockSpec((tq, LANE), lambda b, h, i: (b * nq + i, h)),
                  pl.BlockSpec((n_mem, LANE), lambda b, h, i: (b, h)),
                  pl.BlockSpec((n_mem, LANE), lambda b, h, i: (b, MEM_HEADS + h))],
        out_specs=pl.BlockSpec((tq, LANE), lambda b, h, i: (b * nq + i, h)),
        out_shape=jax.ShapeDtypeStruct((t, MEM_WIDTH), BF16),
        compiler_params=_cp("parallel", "parallel", "parallel"),
    )(qm, kvm, kvm)


def _xattn_bwd(qm, kvm, do, seq, n_mem, tq=1024):
    t = qm.shape[0]
    tq = min(tq, seq)
    nb, nq = t // seq, seq // tq

    def body(q_ref, k_ref, v_ref, do_ref, dq_ref, dkv_ref, dk_acc, dv_acc):
        i = pl.program_id(2)

        @pl.when(i == 0)
        def _():
            dk_acc[...] = jnp.zeros_like(dk_acc)
            dv_acc[...] = jnp.zeros_like(dv_acc)

        _, vjp = jax.vjp(_xattn_tile, q_ref[...], k_ref[...], v_ref[...])
        dq, dk, dv = vjp(do_ref[...].astype(F32))
        dq_ref[...] = dq.astype(dq_ref.dtype)
        dk_acc[...] += dk
        dv_acc[...] += dv

        @pl.when(i == nq - 1)
        def _():
            dkv_ref[0] = dk_acc[...].astype(dkv_ref.dtype)
            dkv_ref[1] = dv_acc[...].astype(dkv_ref.dtype)

    dq, dkv = pl.pallas_call(
        body, name="xattn_bwd", grid=(nb, MEM_HEADS, nq),
        in_specs=[pl.BlockSpec((tq, LANE), lambda b, h, i: (b * nq + i, h)),
                  pl.BlockSpec((n_mem, LANE), lambda b, h, i: (b, h)),
                  pl.BlockSpec((n_mem, LANE), lambda b, h, i: (b, MEM_HEADS + h)),
                  pl.BlockSpec((tq, LANE), lambda b, h, i: (b * nq + i, h))],
        out_specs=[pl.BlockSpec((tq, LANE), lambda b, h, i: (b * nq + i, h)),
                   pl.BlockSpec((2, n_mem, LANE), lambda b, h, i: (0, b, h))],
        out_shape=[jax.ShapeDtypeStruct((t, MEM_WIDTH), BF16), jax.ShapeDtypeStruct((2, nb * n_mem, MEM_WIDTH), BF16)],
        scratch_shapes=[pltpu.VMEM((n_mem, LANE), F32)] * 2,
        compiler_params=_cp("parallel", "parallel", "arbitrary"),
    )(qm, kvm, kvm, do)
    return dq, jnp.concatenate([dkv[0], dkv[1]], axis=1)


def _loss_head(x, u, g_post, target, tm=512):
    t, d = x.shape
    tm = min(tm, t)

    def tile_loss(xv, uv, gv, tv):
        diff = _fn_res(xv, uv, gv)[0] - tv
        return 0.5 * jnp.sum(jnp.mean(diff * diff, axis=-1, keepdims=True), axis=0, keepdims=True)

    def body(x_ref, u_ref, g_ref, t_ref, l_ref, dx_ref, du_ref, dg_ref):
        @pl.when(pl.program_id(0) == 0)
        def _():
            l_ref[...] = jnp.zeros_like(l_ref)
            dg_ref[...] = jnp.zeros_like(dg_ref)

        tv = t_ref[...]
        part, vjp = jax.vjp(lambda xv, uv, gv: tile_loss(xv, uv, gv, tv), x_ref[...], u_ref[...], g_ref[...])
        dx, du, dg = vjp(jnp.ones((1, 1), F32))
        l_ref[...] += part
        dx_ref[...] = dx
        du_ref[...] = du.astype(du_ref.dtype)
        dg_ref[...] += dg

    rows = pl.BlockSpec((tm, d), lambda i: (i, 0))
    vec = pl.BlockSpec((1, d), lambda i: (0, 0))
    return pl.pallas_call(
        body, name="loss_head", grid=(t // tm,),
        in_specs=[rows, rows, vec, rows],
        out_specs=[pl.BlockSpec((8, LANE), lambda i: (0, 0)), rows, rows, vec],
        out_shape=[jax.ShapeDtypeStruct((8, LANE), F32), jax.ShapeDtypeStruct((t, d), F32),
                   jax.ShapeDtypeStruct((t, d), BF16), jax.ShapeDtypeStruct((1, d), F32)],
        compiler_params=_cp("arbitrary"),
    )(x, u, g_post, target)


def _mesh_pos():
    return lax.axis_index("x"), lax.axis_index("y"), lax.axis_index("c")


def _peer(pos, d):
    x, y, c = pos
    return ((1 - x) if d & 4 else x, (1 - y) if d & 2 else y, (1 - c) if d & 1 else c)


def _flat(pos):
    return 4 * pos[0] + 2 * pos[1] + pos[2]


def _exchange(arrays, scatter, *, name):
    n = len(arrays)
    shapes = [a.shape[1:] if scatter else a.shape for a in arrays]

    def body(*refs):
        ins, outs = refs[:n], refs[n:2 * n]
        send, recv, loc = refs[2 * n:]
        pos = _mesh_pos()
        me = _flat(pos)
        pending = []
        for i in range(n):
            own = pltpu.make_async_copy(ins[i].at[me] if scatter else ins[i], outs[i].at[me], loc.at[i])
            own.start()
            pending.append(own)
            for d in range(1, N_DEV):
                peer = _peer(pos, d)
                src = ins[i].at[_flat(peer)] if scatter else ins[i]
                out_cp = pltpu.make_async_remote_copy(
                    src_ref=src, dst_ref=outs[i].at[me], send_sem=send.at[i, d - 1], recv_sem=recv.at[i, d - 1],
                    device_id=peer, device_id_type=pl.DeviceIdType.MESH)
                out_cp.start()
                pending.append(out_cp)
        for i in range(n):
            own = pending[i * N_DEV]
            for d in range(1, N_DEV):
                peer = _peer(pos, d)
                src = ins[i].at[_flat(peer)] if scatter else ins[i]
                pending[i * N_DEV + d].wait_send()
                pltpu.make_async_remote_copy(
                    src_ref=src, dst_ref=outs[i].at[_flat(peer)], send_sem=send.at[i, d - 1], recv_sem=recv.at[i, d - 1],
                    device_id=peer, device_id_type=pl.DeviceIdType.MESH).wait_recv()
            own.wait()

    hbm = pl.BlockSpec(memory_space=pltpu.HBM)
    return pl.pallas_call(
        body, name=name,
        in_specs=[hbm] * n, out_specs=[hbm] * n,
        out_shape=[jax.ShapeDtypeStruct((N_DEV,) + tuple(s), a.dtype) for s, a in zip(shapes, arrays)],
        scratch_shapes=[pltpu.SemaphoreType.DMA((n, N_DEV - 1)), pltpu.SemaphoreType.DMA((n, N_DEV - 1)),
                        pltpu.SemaphoreType.DMA((n,))],
    )(*arrays)


_HBM = pl.BlockSpec(memory_space=pltpu.HBM)
_SEM = pl.BlockSpec(memory_space=pltpu.SEMAPHORE)
_DATAFLOW = pltpu.SideEffectType.DATAFLOW_SIDE_EFFECTING


_ALL_PEERS = tuple(range(1, N_DEV))
_SIBLING_AND_SAME_CORE = (1, 2, 4, 6)


def _remote_copies(ins, lands, send, recv, scatter, dists):
    pos = _mesh_pos()
    me = _flat(pos)
    out = []
    for i in range(len(ins)):
        for j, d in enumerate(dists):
            peer = _peer(pos, d)
            src = ins[i].at[_flat(peer)] if scatter else ins[i]
            pair = i * len(dists) + j
            sems = dict(send_sem=send.at[pair], recv_sem=recv.at[pair], device_id=peer,
                        device_id_type=pl.DeviceIdType.MESH)
            out.append((pltpu.make_async_remote_copy(src_ref=src, dst_ref=lands[i].at[me], **sems),
                        pltpu.make_async_remote_copy(src_ref=src, dst_ref=lands[i].at[_flat(peer)], **sems)))
    return out


def _exchange_start(arrays, scatter, after, *, name, dists=_ALL_PEERS):
    n = len(arrays)
    shapes = [a.shape[1:] if scatter else a.shape for a in arrays]
    lands = [pltpu.with_memory_space_constraint(lax.empty((N_DEV,) + tuple(s), a.dtype), pltpu.HBM)
             for s, a in zip(shapes, arrays)]
    srcs = [pltpu.with_memory_space_constraint(a, pltpu.HBM) for a in arrays]

    def body(*refs):
        ins, land_refs = refs[:n], refs[n:2 * n]
        send, recv, token = refs[2 * n + 1], refs[2 * n + 2], refs[-1]
        for going, _ in _remote_copies(ins, land_refs, send, recv, scatter, dists):
            going.start()
        token[...] = jnp.zeros_like(token)

    sems = pltpu.SemaphoreType.DMA((n * len(dists),))
    res = pl.pallas_call(
        body, name=name,
        out_shape=(sems, sems, *[pltpu.HBM(a.shape, a.dtype) for a in srcs + lands], jax.ShapeDtypeStruct((8, LANE), F32)),
        in_specs=[_HBM] * (2 * n) + [pl.BlockSpec(memory_space=pl.ANY)],
        out_specs=(_SEM, _SEM, *[_HBM] * (2 * n), pl.BlockSpec(memory_space=pltpu.VMEM)),
        input_output_aliases={i: 2 + i for i in range(2 * n)},
        compiler_params=pltpu.CompilerParams(has_side_effects=_DATAFLOW),
    )(*srcs, *lands, after)
    return (n, scatter, dists, res[0], res[1], list(res[2:2 + 2 * n])), res[-1]


def _exchange_wait(handle, after, own, *, name):
    n, scatter, dists, send, recv, thru = handle

    def body(*refs):
        ins, land_refs = refs[:n], refs[n:2 * n]
        for going, coming in _remote_copies(ins, land_refs, refs[2 * n], refs[2 * n + 1], scatter, dists):
            going.wait_send()
            coming.wait_recv()

    res = pl.pallas_call(
        body, name=name,
        out_shape=tuple(pltpu.HBM(a.shape, a.dtype) for a in thru),
        in_specs=[_HBM] * (2 * n) + [_SEM, _SEM] + [pl.BlockSpec(memory_space=pl.ANY)] * len(after),
        out_specs=tuple([_HBM] * (2 * n)),
        input_output_aliases={i: i for i in range(2 * n)},
        compiler_params=pltpu.CompilerParams(has_side_effects=_DATAFLOW),
    )(*thru, send, recv, *after)
    me = _flat(_mesh_pos())
    return [lax.dynamic_update_slice_in_dim(land, o[None].astype(land.dtype), me, 0) for land, o in zip(res[n:], own)]


_OTHER_CHIPS = (2, 4, 6)


def _relay_to_sibling(gathered, *, name):
    n, k = len(gathered), len(_OTHER_CHIPS)

    def body(*refs):
        ins, outs = refs[:n], refs[n:2 * n]
        send, recv = refs[2 * n:]
        pos = _mesh_pos()
        copies = []
        for i in range(n):
            for j, d in enumerate(_OTHER_CHIPS):
                cp = pltpu.make_async_remote_copy(
                    src_ref=ins[i].at[_flat(_peer(pos, d))], dst_ref=outs[i].at[j],
                    send_sem=send.at[i * k + j], recv_sem=recv.at[i * k + j],
                    device_id=_peer(pos, 1), device_id_type=pl.DeviceIdType.MESH)
                cp.start()
                copies.append(cp)
        for cp in copies:
            cp.wait()

    return pl.pallas_call(
        body, name=name, in_specs=[_HBM] * n, out_specs=[_HBM] * n,
        out_shape=[jax.ShapeDtypeStruct((k,) + g.shape[1:], g.dtype) for g in gathered],
        scratch_shapes=[pltpu.SemaphoreType.DMA((n * k,)), pltpu.SemaphoreType.DMA((n * k,))],
    )(*gathered)


def _adamw(parts, w, m, v, *, name, tr=128, after=None):
    r, c = w.shape
    align = 8 * 4 // parts.dtype.itemsize
    row_tiles = [d for d in range(align, min(tr, r) + 1, align) if r % d == 0]
    tr, tc = (max(row_tiles), c) if row_tiles else (r, LANE)
    assert c % tc == 0
    n_after = 0 if after is None else 1

    def body(p_ref, w_ref, m_ref, v_ref, *rest):
        g_ref, d_ref, nm_ref, nv_ref = rest[n_after:]
        g = p_ref[0].astype(F32)
        for j in range(1, N_DEV):
            g = g + p_ref[j].astype(F32)
        m2 = ADAM_B1 * m_ref[...] + (1.0 - ADAM_B1) * g
        v2 = ADAM_B2 * v_ref[...] + (1.0 - ADAM_B2) * (g * g)
        m_hat = m2 / (1.0 - ADAM_B1 ** ADAM_STEP)
        v_hat = v2 / (1.0 - ADAM_B2 ** ADAM_STEP)
        g_ref[...] = g
        d_ref[...] = -ADAM_LR * (m_hat / (jnp.sqrt(v_hat) + ADAM_EPS) + ADAM_WD * w_ref[...])
        nm_ref[...] = m2
        nv_ref[...] = v2

    spec = pl.BlockSpec((tr, tc), lambda i, j: (i, j))
    return pl.pallas_call(
        body, name=name, grid=(r // tr, c // tc),
        in_specs=[pl.BlockSpec((N_DEV, tr, tc), lambda i, j: (0, i, j)), spec, spec, spec]
        + [pl.BlockSpec(memory_space=pl.ANY)] * n_after,
        out_specs=[spec] * 4, out_shape=[jax.ShapeDtypeStruct((r, c), F32)] * 4,
        compiler_params=_cp("parallel", "parallel"),
    )(parts, w, m, v, *([] if after is None else [after]))


def _cols_to_full(g):
    return jnp.transpose(g, (1, 0, 2)).reshape(g.shape[1], N_DEV * g.shape[2])


def _full_to_cols(w):
    r, c = w.shape
    return jnp.transpose(w.reshape(r, N_DEV, c // N_DEV), (1, 0, 2))


def _cut(a, lo, hi, axis):
    return lax.slice_in_dim(a, lo, hi, axis=axis)


def _pad_to(a, size, axis):
    pads = [(0, 0)] * a.ndim
    pads[axis] = (0, size - a.shape[axis])
    return jnp.pad(a, pads)


def _pad_lora(w, axis=1):
    return jnp.concatenate([
        _pad_to(_cut(w, 0, LORA_W, axis), 128, axis), _pad_to(_cut(w, LORA_W, LORA_W + LORA_A, axis), 128, axis),
        _pad_to(_cut(w, LORA_W + LORA_A, w.shape[axis], axis), 256, axis)], axis=axis)


def _unpad_lora(wp, axis=1):
    return jnp.concatenate([_cut(wp, 0, LORA_W, axis), _cut(wp, 128, 128 + LORA_A, axis),
                            _cut(wp, 256, 256 + LORA_G, axis)], axis=axis)


def _permute_in(w, axis):
    rk = 3 * D
    lo = rk + LORA_W + LORA_A + LORA_G
    return jnp.concatenate([_cut(w, 0, rk, axis), _cut(w, lo, w.shape[axis], axis), _pad_lora(_cut(w, rk, lo, axis), axis)],
                           axis=axis)


def _unpermute_in(wp, axis):
    return jnp.concatenate([_cut(wp, 0, 3 * D, axis), _unpad_lora(_cut(wp, C_LORA, P_WIDTH, axis), axis),
                            _cut(wp, 3 * D, C_LORA, axis)], axis=axis)


def _rel_index():
    dist = jnp.arange(CHUNK)[:, None] - jnp.arange(BAND)[None, :] + LEFT
    return (jnp.minimum(dist, REL_CLIP) + (CHUNK - 1)).reshape(-1)


def _local_step(x, mem, target, wt, seq, n_mem, comm):
    t = x.shape[0]
    row = lambda a: a.reshape(1, -1).astype(F32)
    g_pre_mix, g_post_mix = row(wt["g_pre_mix"]), row(wt["g_post_mix"])
    g_pre_cross, g_post_cross, g_mem = row(wt["g_pre_cross"]), row(wt["g_post_cross"]), row(wt["g_mem"])
    g_pre_ffn, g_post_ffn = row(wt["g_pre_ffn"]), row(wt["g_post_ffn"])
    mix = row(wt["shift_mix"])
    mix_rkv, mix_lora = mix[:, :3 * D], _pad_lora(mix[:, 3 * D:])
    decay_base, iclr_base = row(wt["decay_base"]), row(wt["iclr_base"])
    kns, kis = row(wt["key_norm_scale"]), row(wt["key_iclr_scale"])
    lnx_w, lnx_b, bonus = row(wt["lnx_w"]), row(wt["lnx_b"]), row(wt["bonus_scale"])
    e_dh = (jnp.arange(D)[:, None] // HEAD == jnp.arange(N_HEADS)[None, :]).astype(F32)
    e_hd = e_dh.T
    onehot = (jnp.arange(REL_TABLE)[:, None] == _rel_index()[None, :]).astype(BF16)

    begun = comm.begun
    (h1,) = _rowwise(_fn_pre, [_win(x)], [g_pre_mix], [(D, BF16)], name="pre_mix", tm=512, after=begun)
    (mn,) = _rowwise(_fn_pre, [_win(mem)], [g_mem], [(D, BF16)], name="pre_mem", tm=512, after=begun)
    bias = _mm(wt["rel_bias"].astype(F32), onehot, name="mm_bias", split_a=3, after=begun).reshape(N_HEADS, CHUNK, BAND)
    wt = {**wt, **comm.first_weights([h1, mn, bias])}
    w_in = wt["w_in_p"]
    d_up = jnp.pad(wt["decay_up"].astype(F32), ((0, 128 - LORA_W), (0, 0)))
    i_up = jnp.pad(wt["iclr_up"].astype(F32), ((0, 128 - LORA_A), (0, 0)))
    g_up = jnp.pad(wt["gate_up"].astype(F32), ((0, 256 - LORA_G), (0, 0)))
    proj = _mm(h1, w_in, tb=True, name="mm_in", after=comm.first_token)
    z_rkv = _shift_fwd(proj, 0, 3 * D, mix_rkv, seq, name="shift_rkv")
    z_lora = _shift_fwd(proj, C_LORA, 512, mix_lora, seq, name="shift_lora")
    prep_rows = [_win(z_rkv, D, D), _win(z_lora, 0, 128), _win(z_lora, 128, 128), _win(z_lora, 256, 256)]
    prep_params = [decay_base, d_up, iclr_base, i_up, g_up, kns, kis, e_hd, e_dh]
    lw, k2, kk, a, g = _rowwise(_fn_prep, prep_rows, prep_params, [(D, F32)] * 5, name="rwkv_prep", tm=256)
    y, states, invs = _wkv_fwd(z_rkv, lw, k2, kk, a, seq)
    post_rows = [_win(y), _win(z_rkv, 0, D), _win(k2), _win(z_rkv, 2 * D, D), _win(g)]
    post_params = [lnx_w, lnx_b, bonus, e_hd, e_dh]
    (y_a,) = _rowwise(_fn_post, post_rows, post_params, [(D, BF16)], name="rwkv_post", tm=256)
    y_b = _attn_fwd(proj, bias, seq)
    wt = {**wt, **comm.late_weights(y_b)}
    ya_p = _mm(y_a, wt["w_branch_a"], name="mm_a")
    yb_p = _mm(y_b, wt["w_branch_b"], name="mm_b")
    mix_rows = [_win(proj, C_GA, D), _win(proj, C_GA + D, D), _win(ya_p), _win(yb_p)]
    (mixed,) = _rowwise(_fn_mix, mix_rows, [], [(D, BF16)], name="gate_mix", tm=512)
    mo = _mm(mixed, wt["w_out"], name="mm_out")
    x1, h2 = _rowwise(_fn_res_pre, [_win(x), _win(mo)], [g_post_mix, g_pre_cross], [(D, F32), (D, BF16)],
                      name="res_mix", tm=512)
    qm = _mm(h2, wt["w_q_mem"], name="mm_q")
    kvm = _mm(mn, wt["w_kv_mem"], name="mm_kv")
    om = _xattn_fwd(qm, kvm, seq, n_mem)
    co = _mm(om, wt["w_o_mem"], name="mm_o")
    x2, h3 = _rowwise(_fn_res_pre, [_win(x1), _win(co)], [g_post_cross, g_pre_ffn], [(D, F32), (D, BF16)],
                      name="res_cross", tm=512)
    gu = _mm(h3, wt["w_ffn_in"], tb=True, name="mm_ffn_in", out_dtype=BF16)
    (act,) = _rowwise(_fn_swiglu, [_win(gu, 0, FFN), _win(gu, FFN, FFN)], [], [(FFN, BF16)], name="swiglu", tm=512)
    ff = _mm(act, wt["w_ffn_out"], name="mm_ffn_out")

    gw = {}
    loss, dx2, dff, gw["g_post_ffn"] = _loss_head(x2, ff, g_post_ffn, target)
    dact = _mm(dff, wt["w_ffn_out"], tb=True, name="mm_ffn_out_dx", out_dtype=BF16)
    gw["w_ffn_out"] = _mm(act, dff, ta=True, name="mm_ffn_out_dw", out_dtype=BF16)
    (dgu,), _ = _rowwise_bwd(_fn_swiglu, [_win(gu, 0, FFN), _win(gu, FFN, FFN)], [], 0, [[dact]],
                             name="swiglu_bwd", tm=512, row_grad=[BF16, BF16], packed=True)
    dh3 = _mm(dgu, wt["w_ffn_in"], name="mm_ffn_in_dx", out_dtype=BF16)
    gw["w_ffn_in"] = _mm(dgu, h3, ta=True, name="mm_ffn_in_dw", out_dtype=BF16)
    (dx1, dco), (gw["g_post_cross"], gw["g_pre_ffn"]) = _rowwise_bwd(
        _fn_res_pre, [_win(x1), _win(co)], [g_post_cross, g_pre_ffn], 0, [[dx2], [dh3]],
        name="res_cross_bwd", tm=512, row_grad=[F32, BF16])
    dom = _mm(dco, wt["w_o_mem"], tb=True, name="mm_o_dx", out_dtype=BF16)
    gw["w_o_mem"] = _mm(om, dco, ta=True, name="mm_o_dw", out_dtype=BF16)
    dqm, dkvm = _xattn_bwd(qm, kvm, dom, seq, n_mem)
    dh2 = _mm(dqm, wt["w_q_mem"], tb=True, name="mm_q_dx", out_dtype=BF16)
    gw["w_q_mem"] = _mm(h2, dqm, ta=True, name="mm_q_dw", out_dtype=BF16)
    dmn = _mm(dkvm, wt["w_kv_mem"], tb=True, name="mm_kv_dx", out_dtype=BF16)
    gw["w_kv_mem"] = _mm(mn, dkvm, ta=True, name="mm_kv_dw", out_dtype=BF16)
    _, (gw["g_mem"],) = _rowwise_bwd(_fn_pre, [_win(mem)], [g_mem], 0, [[dmn]], name="pre_mem_bwd", tm=256,
                                     row_grad=[None])
    (dx0, dmo), (gw["g_post_mix"], gw["g_pre_cross"]) = _rowwise_bwd(
        _fn_res_pre, [_win(x), _win(mo)], [g_post_mix, g_pre_cross], 0, [[dx1], [dh2]],
        name="res_mix_bwd", tm=512, row_grad=[F32, BF16])
    dmixed = _mm(dmo, wt["w_out"], tb=True, name="mm_out_dx", out_dtype=BF16)
    gw["w_out"] = _mm(mixed, dmo, ta=True, name="mm_out_dw", out_dtype=BF16)
    (dzga, dzgb, dya_p, dyb_p), _ = _rowwise_bwd(_fn_mix, mix_rows, [], 0, [[dmixed]], name="gate_mix_bwd", tm=512,
                                                 row_grad=[BF16] * 4)
    gw["w_branch_a"] = _mm(y_a, dya_p, ta=True, name="mm_a_dw", out_dtype=BF16)
    gw["w_branch_b"] = _mm(y_b, dyb_p, ta=True, name="mm_b_dw", out_dtype=BF16)
    token = comm.send_early(gw)
    dy_a = _mm(dya_p, wt["w_branch_a"], tb=True, name="mm_a_dx", out_dtype=BF16, after=token)
    dy_b = _mm(dyb_p, wt["w_branch_b"], tb=True, name="mm_b_dx", out_dtype=BF16, after=token)
    dq, dk, dv, dbias = _attn_bwd(proj, bias, dy_b, seq)
    gw["rel_bias"] = _mm(dbias.reshape(N_HEADS, CHUNK * BAND), onehot, tb=True, name="mm_bias_dw", split_a=2)
    (dy, dr_p, dk2_p, dv_p, dg), (gw["lnx_w"], gw["lnx_b"], gw["bonus_scale"]) = _rowwise_bwd(
        _fn_post, post_rows, post_params, 2, [[dy_a]], name="rwkv_post_bwd", tm=512, row_grad=[BF16] * 5)
    dr_s, dlw, dk2_s, dv_s, dkk, da = _wkv_bwd(z_rkv, lw, k2, kk, a, states, invs, dy, seq)
    (dzk, dzw, dza, dzg), pg = _rowwise_bwd(
        _fn_prep, prep_rows, prep_params, 2, [[dlw], [dk2_p, dk2_s], [dkk], [da], [dg]],
        name="rwkv_prep_bwd", tm=512, row_grad=[BF16] * 4)
    gw["decay_base"], gd_up, gw["iclr_base"], gi_up, gg_up, gw["key_norm_scale"], gw["key_iclr_scale"] = pg
    gw["decay_up"], gw["iclr_up"], gw["gate_up"] = gd_up[:LORA_W], gi_up[:LORA_A], gg_up[:LORA_G]
    dp_r, gmix_r = _shift_bwd(proj, 0, D, mix_rkv[:, :D], [dr_p, dr_s], seq, name="shift_r_bwd")
    dp_k, gmix_k = _shift_bwd(proj, D, D, mix_rkv[:, D:2 * D], [dzk], seq, name="shift_k_bwd")
    dp_v, gmix_v = _shift_bwd(proj, 2 * D, D, mix_rkv[:, 2 * D:], [dv_p, dv_s], seq, name="shift_v_bwd")
    dp_lora, gmix_lora = _shift_bwd(proj, C_LORA, 512, mix_lora, [jnp.concatenate([dzw, dza, dzg], axis=1)], seq,
                                    name="shift_lora_bwd")
    gw["shift_mix"] = jnp.concatenate([gmix_r, gmix_k, gmix_v, _unpad_lora(gmix_lora)], axis=1)
    dproj = [dp_r, dp_k, dp_v, dq, dk, dv, dzga, dzgb, dp_lora]
    gw["w_in_p"] = _mm_cat_tn(dproj, h1, name="mm_in_dw", after=gw["rel_bias"])
    token = comm.send_late(gw)
    dh1 = _mm_cat_nn(dproj, w_in, name="mm_in_dx", after=token)
    (grad_x,), (gw["g_pre_mix"],) = _rowwise_bwd(_fn_pre, [_win(x)], [g_pre_mix], 0, [[dh1]], name="pre_mix_bwd",
                                                 tm=512, row_grad=[F32], add_to={0: dx0})
    return loss, grad_x, gw


_COL_SHARDED = ("w_in", "decay_up", "iclr_up", "gate_up", "w_o_mem", "w_ffn_in")
_ROW_SHARDED = ("w_branch_a", "w_branch_b", "w_out", "w_q_mem", "w_kv_mem", "w_ffn_out")
_TRANSPOSED = ("w_in", "w_ffn_in")
_FIRST = ("w_in", "decay_up", "iclr_up", "gate_up")
_REST = ("w_o_mem", "w_ffn_in", "w_branch_a", "w_branch_b", "w_out", "w_q_mem", "w_kv_mem", "w_ffn_out")
_REPLICATED = ("g_pre_mix", "g_post_mix", "shift_mix", "decay_base", "iclr_base", "key_norm_scale", "key_iclr_scale",
               "bonus_scale", "lnx_w", "lnx_b", "rel_bias", "g_pre_cross", "g_post_cross", "g_mem", "g_pre_ffn",
               "g_post_ffn")
_WEIGHTS = ("g_pre_mix", "g_post_mix", "w_in", "shift_mix", "decay_base", "decay_up", "iclr_base", "iclr_up", "gate_up",
            "key_norm_scale", "key_iclr_scale", "bonus_scale", "lnx_w", "lnx_b", "rel_bias", "w_branch_a", "w_branch_b",
            "w_out", "g_pre_cross", "g_post_cross", "g_mem", "w_q_mem", "w_kv_mem", "w_o_mem", "g_pre_ffn", "g_post_ffn",
            "w_ffn_in", "w_ffn_out")
_PACK_ROWS = 8 * ((sum({"shift_mix": 3360, "bonus_scale": 1024, "rel_bias": 3072}.get(n, D) for n in _REPLICATED)
                   + 1 + 8 * LANE - 1) // (8 * LANE))


def _pack(vals):
    flat = jnp.concatenate([v.reshape(-1).astype(F32) for v in vals])
    return jnp.pad(flat, (0, _PACK_ROWS * LANE - flat.shape[0])).reshape(_PACK_ROWS, LANE)


def _unpack(packed, shapes):
    flat, out, pos = packed.reshape(-1), [], 0
    for s in shapes:
        n = math.prod(s)
        out.append(flat[pos:pos + n].reshape(s))
        pos += n
    return out


def _step(args, seq, n_mem):
    names = ("x", "mem") + _WEIGHTS + ("loss_target",) + tuple("m_" + n for n in _WEIGHTS) + tuple("v_" + n for n in _WEIGHTS)
    given = dict(zip(names, args))
    nb = given["x"].shape[0]
    x = given["x"].reshape(nb * seq, D)
    mem = given["mem"].reshape(nb * n_mem, D)
    target = given["loss_target"].reshape(nb * seq, D)
    def local(name, prefix=""):
        a = given[prefix + name][0]
        return a.T if name in _TRANSPOSED else a

    shard = {n: local(n) for n in _COL_SHARDED + _ROW_SHARDED}
    stacked = _ROW_SHARDED + _TRANSPOSED
    out = {}

    def wire(name):
        return shard[name].astype(BF16)

    def full(name, g):
        return g.reshape(-1, g.shape[-1]) if name in stacked else _cols_to_full(g)

    def blocks_of(name, g):
        return (g.reshape((N_DEV,) + shard[name].shape) if name in stacked else _full_to_cols(g)).astype(BF16)

    def update(names, landed, after=None):
        done = []
        for n, parts in zip(names, landed):
            res = _adamw(parts, shard[n], local(n, "m_"), local(n, "v_"), name="adamw_" + n, after=after)
            for kind, r in zip(("grad_", "delta_", "new_m_", "new_v_"), res):
                out[kind + n] = (r.T if n in _TRANSPOSED else r)[None]
            done.append(res[0])
        return done


    class Exchanges:
        def __init__(self):
            srcs = [wire(n) for n in _FIRST]
            self.first, self.begun = _exchange_start(srcs, False, srcs[0], name="gather_first_start",
                                                     dists=_SIBLING_AND_SAME_CORE)

        def first_weights(self, after):
            got = _exchange_wait(self.first, after, [wire(n) for n in _FIRST], name="gather_first_wait")
            relayed = _relay_to_sibling(got, name="gather_first_relay")
            pos = _mesh_pos()
            for j, d in enumerate(_OTHER_CHIPS):
                slot = _flat(_peer(pos, d | 1))
                got = [lax.dynamic_update_slice_in_dim(g, r[j][None], slot, 0) for g, r in zip(got, relayed)]
            self.rest, self.first_token = _exchange_start(
                [wire(n) for n in _REST], False, got[0], name="gather_rest_start")
            first = {n: full(n, g) for n, g in zip(_FIRST, got)}
            first["w_in_p"] = _permute_in(first.pop("w_in"), 0)
            return first

        def late_weights(self, after):
            got = _exchange_wait(self.rest, [after], [wire(n) for n in _REST], name="gather_rest_wait")
            return {n: full(n, g) for n, g in zip(_REST, got)}

        def send_early(self, gw):
            self.early_blocks = [blocks_of(n, gw[n]) for n in _REST]
            self.early, token = _exchange_start(self.early_blocks, True, self.early_blocks[-1], name="scatter_rest_start")
            return token

        def send_late(self, gw):
            me = _flat(_mesh_pos())
            own = [lax.dynamic_index_in_dim(b, me, 0, keepdims=False) for b in self.early_blocks]
            landed = _exchange_wait(self.early, [gw["w_in_p"]], own, name="scatter_rest_wait")
            grads = {**gw, "w_in": _unpermute_in(gw["w_in_p"], 0)}
            self.late_blocks = [blocks_of(n, grads[n]) for n in _FIRST]
            self.late, token = _exchange_start(self.late_blocks, True, landed[0], name="scatter_first_start")
            self.updated = update(_REST, landed, after=token)
            return token

        def finish(self, after):
            me = _flat(_mesh_pos())
            own = [lax.dynamic_index_in_dim(b, me, 0, keepdims=False) for b in self.late_blocks]
            update(_FIRST, _exchange_wait(self.late, [*after, *self.updated], own, name="scatter_first_wait"))

    comm = Exchanges()
    wt = {n: given[n][0] for n in _REPLICATED}
    loss_tile, grad_x, gw = _local_step(x, mem, target, wt, seq, n_mem, comm)
    rep_shapes = [given[n].shape for n in _REPLICATED]
    packed, _ = lax.optimization_barrier((_pack([gw[n] for n in _REPLICATED] + [loss_tile[0, 0]]), tuple(comm.updated)))
    small = _exchange([packed], False, name="gather_small")[0]
    zero = jnp.zeros((), F32)
    res = _adamw(small, *[_pack([given[p + n] for n in _REPLICATED] + [zero]) for p in ("", "m_", "v_")],
                 name="adamw_small", tr=_PACK_ROWS)
    for kind, r in zip(("grad_", "delta_", "new_m_", "new_v_"), res):
        for n, val in zip(_REPLICATED, _unpack(r, rep_shapes)):
            out[kind + n] = val
    loss = res[0].reshape(-1)[sum(math.prod(s) for s in rep_shapes)]
    comm.finish([grad_x, res[0]])
    grad_x = grad_x.reshape(nb, seq, D)
    return (loss, grad_x, *[out[k + n] for k in ("grad_", "delta_", "new_m_", "new_v_") for n in _WEIGHTS])


def kernel(x, mem, g_pre_mix, g_post_mix, w_in, shift_mix, decay_base, decay_up, iclr_base, iclr_up, gate_up, key_norm_scale, key_iclr_scale, bonus_scale, lnx_w, lnx_b, rel_bias, w_branch_a, w_branch_b, w_out, g_pre_cross, g_post_cross, g_mem, w_q_mem, w_kv_mem, w_o_mem, g_pre_ffn, g_post_ffn, w_ffn_in, w_ffn_out, loss_target, m_g_pre_mix, m_g_post_mix, m_w_in, m_shift_mix, m_decay_base, m_decay_up, m_iclr_base, m_iclr_up, m_gate_up, m_key_norm_scale, m_key_iclr_scale, m_bonus_scale, m_lnx_w, m_lnx_b, m_rel_bias, m_w_branch_a, m_w_branch_b, m_w_out, m_g_pre_cross, m_g_post_cross, m_g_mem, m_w_q_mem, m_w_kv_mem, m_w_o_mem, m_g_pre_ffn, m_g_post_ffn, m_w_ffn_in, m_w_ffn_out, v_g_pre_mix, v_g_post_mix, v_w_in, v_shift_mix, v_decay_base, v_decay_up, v_iclr_base, v_iclr_up, v_gate_up, v_key_norm_scale, v_key_iclr_scale, v_bonus_scale, v_lnx_w, v_lnx_b, v_rel_bias, v_w_branch_a, v_w_branch_b, v_w_out, v_g_pre_cross, v_g_post_cross, v_g_mem, v_w_q_mem, v_w_kv_mem, v_w_o_mem, v_g_pre_ffn, v_g_post_ffn, v_w_ffn_in, v_w_ffn_out):
    args = (x, mem, g_pre_mix, g_post_mix, w_in, shift_mix, decay_base, decay_up, iclr_base, iclr_up, gate_up, key_norm_scale, key_iclr_scale, bonus_scale, lnx_w, lnx_b, rel_bias, w_branch_a, w_branch_b, w_out, g_pre_cross, g_post_cross, g_mem, w_q_mem, w_kv_mem, w_o_mem, g_pre_ffn, g_post_ffn, w_ffn_in, w_ffn_out, loss_target, m_g_pre_mix, m_g_post_mix, m_w_in, m_shift_mix, m_decay_base, m_decay_up, m_iclr_base, m_iclr_up, m_gate_up, m_key_norm_scale, m_key_iclr_scale, m_bonus_scale, m_lnx_w, m_lnx_b, m_rel_bias, m_w_branch_a, m_w_branch_b, m_w_out, m_g_pre_cross, m_g_post_cross, m_g_mem, m_w_q_mem, m_w_kv_mem, m_w_o_mem, m_g_pre_ffn, m_g_post_ffn, m_w_ffn_in, m_w_ffn_out, v_g_pre_mix, v_g_post_mix, v_w_in, v_shift_mix, v_decay_base, v_decay_up, v_iclr_base, v_iclr_up, v_gate_up, v_key_norm_scale, v_key_iclr_scale, v_bonus_scale, v_lnx_w, v_lnx_b, v_rel_bias, v_w_branch_a, v_w_branch_b, v_w_out, v_g_pre_cross, v_g_post_cross, v_g_mem, v_w_q_mem, v_w_kv_mem, v_w_o_mem, v_g_pre_ffn, v_g_post_ffn, v_w_ffn_in, v_w_ffn_out)
    return _step(args, x.shape[1], mem.shape[1])
```

```python
import functools
import math

import jax
import jax.numpy as jnp
from jax import lax
from jax.experimental import pallas as pl
from jax.experimental.pallas import tpu as pltpu

F32 = jnp.float32
BF16 = jnp.bfloat16

N_DEV = 8
D = 1024
HEAD = 64
N_HEADS = D // HEAD
LANE = 128
CHUNK = 64
LEFT = 8 * CHUNK
BAND = LEFT + CHUNK
REL_CLIP = 128
REL_TABLE = CHUNK + REL_CLIP
MEM_WIDTH = D // 2
MEM_HEADS = 4
FFN = 2816
LORA_W, LORA_A, LORA_G = 64, 64, 160
P_WIDTH = 3 * D + 3 * D + 2 * D + 128 + 128 + 256
C_Q, C_GA, C_LORA = 3 * D, 6 * D, 8 * D
NORM_EPS = 1e-6
GROUP_NORM_EPS = 64e-5
MASK_VALUE = -1e30
ADAM_LR, ADAM_B1, ADAM_B2, ADAM_EPS, ADAM_WD, ADAM_STEP = 0.001, 0.9, 0.999, 1e-08, 0.01, 10
VMEM_LIMIT = 56 * 1024 * 1024


def _cp(*sem):
    return pltpu.CompilerParams(dimension_semantics=sem, vmem_limit_bytes=VMEM_LIMIT)


_NN, _NT, _TN = ((1,), (0,)), ((1,), (1,)), ((0,), (0,))


def _dot_raw(a, b, dims):
    return lax.dot_general(a.astype(BF16), b.astype(BF16), (dims, ((), ())), preferred_element_type=F32)


@functools.partial(jax.custom_vjp, nondiff_argnums=(2,))
def _dot_dims(a, b, dims):
    return _dot_raw(a, b, dims)


def _dot_dims_fwd(a, b, dims):
    return _dot_raw(a, b, dims), (a, b)


def _dot_dims_bwd(dims, res, g):
    a, b = res
    if dims == _NN:
        da, db = _dot_raw(g, b, _NT), _dot_raw(a, g, _TN)
    elif dims == _NT:
        da, db = _dot_raw(g, b, _NN), _dot_raw(g, a, _TN)
    else:
        da, db = _dot_raw(b, g, _NT), _dot_raw(a, g, _NN)
    return da.astype(a.dtype), db.astype(b.dtype)


_dot_dims.defvjp(_dot_dims_fwd, _dot_dims_bwd)


def _dot(a, b, dims=_NN):
    return _dot_dims(a, b, dims)


def _dot_nt(a, b):
    return _dot_dims(a, b, _NT)


def _dot_tn(a, b):
    return _dot_dims(a, b, _TN)


def _split(x, terms):
    parts, rest = [], x.astype(F32)
    for _ in range(terms):
        p = rest.astype(BF16)
        parts.append(p)
        rest = rest - p.astype(F32)
    return parts


def _dot_split_a(a, b, terms=2):
    out = None
    for p in _split(a, terms):
        t = _dot(p, b)
        out = t if out is None else out + t
    return out


def _dot_split_b(a, b, terms=3):
    out = None
    for p in _split(b, terms):
        t = _dot(a, p)
        out = t if out is None else out + t
    return out


MM_VMEM_BUDGET = 30 * 1024 * 1024
MM_HBM_BPS = 3.2e12
MM_MXU_FPS = 8.5e14
MM_STEP_S = 0.35e-6


def _divisors(n, align, cap):
    out = [d for d in range(align, min(n, cap) + 1, align) if n % d == 0]
    return out or [n]


def _mm_tiles(m, n, k, ea, eb, eo, ta):
    best = None
    for tm in _divisors(m, LANE if ta else 8, 2048):
        for tn in _divisors(n, LANE, 2048):
            for tk in _divisors(k, LANE, 2048):
                nk = k // tk
                vmem = 2 * (tm * tk * ea + tk * tn * eb + tm * tn * eo) + (tm * tn * 4 if nk > 1 else 0)
                if vmem > MM_VMEM_BUDGET:
                    continue
                dma = (tm * tk * ea if (nk > 1 or n // tn == 1) else tm * tk * ea * tn / n) + tk * tn * eb + tm * tn * eo / nk
                step = max(2.0 * tm * tn * tk / MM_MXU_FPS, dma / MM_HBM_BPS) + MM_STEP_S
                cost = (m // tm) * (n // tn) * nk * step
                if best is None or cost < best[0]:
                    best = (cost, tm, tn, tk)
    return best[1:]


def _mm(a, b, *, name, ta=False, tb=False, out_dtype=F32, tm=None, tn=None, tk=None, split_a=1, after=None):
    m, k = (a.shape[1], a.shape[0]) if ta else a.shape
    n, kb = (b.shape[0], b.shape[1]) if tb else (b.shape[1], b.shape[0])
    assert k == kb, (a.shape, b.shape, ta, tb)
    if tm is None:
        tm, tn, tk = _mm_tiles(m, n, k, a.dtype.itemsize, b.dtype.itemsize, jnp.dtype(out_dtype).itemsize, ta)
    assert m % tm == 0 and n % tn == 0 and k % tk == 0, (m, n, k, tm, tn, tk)
    nk = k // tk
    dims = ((0 if ta else 1,), (1 if tb else 0,))

    n_after = 0 if after is None else 1

    def body(a_ref, b_ref, *rest):
        o_ref, scratch = rest[n_after], rest[n_after + 1:]
        prod = None
        for p in _split(a_ref[...], split_a) if split_a > 1 else [a_ref[...]]:
            t = _dot_raw(p, b_ref[...], dims)
            prod = t if prod is None else prod + t
        if nk == 1:
            o_ref[...] = prod.astype(o_ref.dtype)
            return
        acc_ref, kk = scratch[0], pl.program_id(2)

        @pl.when(kk == 0)
        def _():
            acc_ref[...] = prod

        @pl.when(kk > 0)
        def _():
            acc_ref[...] += prod

        @pl.when(kk == nk - 1)
        def _():
            o_ref[...] = acc_ref[...].astype(o_ref.dtype)

    a_spec = pl.BlockSpec((tk, tm), lambda i, j, q: (q, i)) if ta else pl.BlockSpec((tm, tk), lambda i, j, q: (i, q))
    b_spec = pl.BlockSpec((tn, tk), lambda i, j, q: (j, q)) if tb else pl.BlockSpec((tk, tn), lambda i, j, q: (q, j))
    return pl.pallas_call(
        body, name=name, grid=(m // tm, n // tn, nk),
        in_specs=[a_spec, b_spec] + [pl.BlockSpec(memory_space=pl.ANY)] * n_after,
        out_specs=pl.BlockSpec((tm, tn), lambda i, j, q: (i, j)),
        out_shape=jax.ShapeDtypeStruct((m, n), out_dtype),
        scratch_shapes=[pltpu.VMEM((tm, tn), F32)] if nk > 1 else [],
        compiler_params=_cp("parallel", "parallel", "arbitrary"),
    )(a, b, *([] if after is None else [after]))


def _piece_steps(pieces, tile):
    counts = [p.shape[1] // tile for p in pieces]
    assert all(p.shape[1] % tile == 0 for p in pieces)
    return [(sum(counts[:i]), c) for i, c in enumerate(counts)], sum(counts)


def _mm_cat_nn(pieces, w, *, name, after=None, tm=2048, tk=256):
    t, n = pieces[0].shape[0], w.shape[1]
    tm = min(tm, t)
    spans, nk = _piece_steps(pieces, tk)
    npc = len(pieces)
    n_after = 0 if after is None else 1

    def body(*refs):
        w_ref, o_ref, acc_ref = refs[npc], refs[npc + 1 + n_after], refs[npc + 2 + n_after]
        q = pl.program_id(1)

        @pl.when(q == 0)
        def _():
            acc_ref[...] = jnp.zeros_like(acc_ref)

        for p_ref, (first, count) in zip(refs[:npc], spans):
            @pl.when(jnp.logical_and(q >= first, q < first + count))
            def _(p_ref=p_ref):
                acc_ref[...] += _dot_raw(p_ref[...], w_ref[...], _NN)

        @pl.when(q == nk - 1)
        def _():
            o_ref[...] = acc_ref[...].astype(o_ref.dtype)

    def piece_spec(first, count):
        return pl.BlockSpec((tm, tk), lambda i, q: (i, jnp.clip(q - first, 0, count - 1)))

    return pl.pallas_call(
        body, name=name, grid=(t // tm, nk),
        in_specs=[piece_spec(*s) for s in spans] + [pl.BlockSpec((tk, n), lambda i, q: (q, 0))]
        + [pl.BlockSpec(memory_space=pl.ANY)] * n_after,
        out_specs=pl.BlockSpec((tm, n), lambda i, q: (i, 0)),
        out_shape=jax.ShapeDtypeStruct((t, n), BF16),
        scratch_shapes=[pltpu.VMEM((tm, n), F32)],
        compiler_params=_cp("parallel", "arbitrary"),
    )(*pieces, w, *([] if after is None else [after]))


def _mm_cat_tn(pieces, a, *, name, after=None, tk=1024, tn=512):
    t, m = a.shape
    tk = min(tk, t)
    spans, nj = _piece_steps(pieces, tn)
    npc, nk = len(pieces), t // tk
    n_after = 0 if after is None else 1

    def body(a_ref, *refs):
        o_ref, acc_ref = refs[npc + n_after], refs[npc + 1 + n_after]
        j, q = pl.program_id(0), pl.program_id(1)

        @pl.when(q == 0)
        def _():
            acc_ref[...] = jnp.zeros_like(acc_ref)

        for p_ref, (first, count) in zip(refs[:npc], spans):
            @pl.when(jnp.logical_and(j >= first, j < first + count))
            def _(p_ref=p_ref):
                acc_ref[...] += _dot_raw(p_ref[...], a_ref[...], _TN)

        @pl.when(q == nk - 1)
        def _():
            o_ref[...] = acc_ref[...].astype(o_ref.dtype)

    def piece_spec(first, count):
        def index(j, q):
            mine = jnp.logical_and(j >= first, j < first + count)
            return jnp.where(mine, q, 0), jnp.clip(j - first, 0, count - 1)
        return pl.BlockSpec((tk, tn), index)

    return pl.pallas_call(
        body, name=name, grid=(nj, nk),
        in_specs=[pl.BlockSpec((tk, m), lambda j, q: (q, 0))] + [piece_spec(*s) for s in spans]
        + [pl.BlockSpec(memory_space=pl.ANY)] * n_after,
        out_specs=pl.BlockSpec((tn, m), lambda j, q: (j, 0)),
        out_shape=jax.ShapeDtypeStruct((nj * tn, m), BF16),
        scratch_shapes=[pltpu.VMEM((tn, m), F32)],
        compiler_params=_cp("parallel", "arbitrary"),
    )(a, *pieces, *([] if after is None else [after]))


def _win(arr, start=0, width=None):
    width = arr.shape[1] if width is None else width
    assert start % width == 0
    return (arr, start // width, width)


def _row_specs(rows, tm):
    return [pl.BlockSpec((tm, w), functools.partial(lambda i, cb: (i, cb), cb=cb)) for (_, cb, w) in rows]


def _full_spec(p):
    nd = p.ndim
    return pl.BlockSpec(p.shape, lambda i, nd=nd: (0,) * nd)


def _rowwise(fn, rows, params, outs, *, name, tm, after=None):
    t = rows[0][0].shape[0]
    tm = min(tm, t)
    assert t % tm == 0
    nr, npar = len(rows), len(params)
    n_after = 0 if after is None else 1

    def body(*refs):
        vals = [r[...] for r in refs[:nr + npar]]
        res = fn(*vals)
        for o_ref, r in zip(refs[nr + npar + n_after:], res):
            o_ref[...] = r.astype(o_ref.dtype)

    return pl.pallas_call(
        body, name=name, grid=(t // tm,),
        in_specs=_row_specs(rows, tm) + [_full_spec(p) for p in params] + [pl.BlockSpec(memory_space=pl.ANY)] * n_after,
        out_specs=[pl.BlockSpec((tm, w), lambda i: (i, 0)) for (w, _) in outs],
        out_shape=[jax.ShapeDtypeStruct((t, w), dt) for (w, dt) in outs],
        compiler_params=_cp("parallel"),
    )(*[r[0] for r in rows], *params, *([] if after is None else [after]))


def _rowwise_bwd(fn, rows, params, n_const, cots, *, name, tm, row_grad, add_to=None, packed=False):
    t = rows[0][0].shape[0]
    tm = min(tm, t)
    assert t % tm == 0
    nr, npar = len(rows), len(params)
    ndp = npar - n_const
    add_to = add_to or {}
    add_idx = sorted(add_to)
    flat_cots = [c for group in cots for c in group]
    kept = [i for i in range(nr) if row_grad[i] is not None]

    def body(*refs):
        pos = 0
        row_v = [r[...] for r in refs[pos:pos + nr]]; pos += nr
        par_v = [r[...] for r in refs[pos:pos + npar]]; pos += npar
        cot_v = [r[...] for r in refs[pos:pos + len(flat_cots)]]; pos += len(flat_cots)
        add_v = [r[...] for r in refs[pos:pos + len(add_idx)]]; pos += len(add_idx)
        if packed:
            offs = [sum(rows[i][2] for i in kept[:q]) for q in range(len(kept))]
            rg_refs = [refs[pos].at[:, o:o + rows[i][2]] for o, i in zip(offs, kept)]; pos += 1
        else:
            rg_refs = refs[pos:pos + len(kept)]; pos += len(kept)
        pg_refs = refs[pos:pos + ndp]

        consts = par_v[ndp:]
        res, vjp = jax.vjp(lambda *args: tuple(fn(*args, *consts)), *row_v, *par_v[:ndp])
        cot_in, q = [], 0
        for j, group in enumerate(cots):
            c = None
            for _ in group:
                cv = cot_v[q].astype(F32); q += 1
                c = cv if c is None else c + cv
            c = jnp.zeros(res[j].shape, F32) if c is None else c
            cot_in.append(c.astype(res[j].dtype))
        grads = vjp(tuple(cot_in))
        for ref, i in zip(rg_refs, kept):
            g = grads[i].astype(F32)
            if i in add_to:
                g = g + add_v[add_idx.index(i)].astype(F32)
            ref[...] = g.astype(ref.dtype)

        @pl.when(pl.program_id(0) == 0)
        def _():
            for ref in pg_refs:
                ref[...] = jnp.zeros_like(ref)

        for ref, g in zip(pg_refs, grads[nr:]):
            ref[...] += g.astype(F32)

    cot_specs = [pl.BlockSpec((tm, c.shape[1]), lambda i: (i, 0)) for c in flat_cots]
    add_specs = [pl.BlockSpec((tm, add_to[i].shape[1]), lambda i_: (i_, 0)) for i in add_idx]
    widths = [sum(rows[i][2] for i in kept)] if packed else [rows[i][2] for i in kept]
    n_rg = len(widths)
    out_specs = [pl.BlockSpec((tm, w), lambda i_: (i_, 0)) for w in widths] + [_full_spec(p) for p in params[:ndp]]
    out_shape = [jax.ShapeDtypeStruct((t, w), row_grad[kept[q]]) for q, w in enumerate(widths)] + [
        jax.ShapeDtypeStruct(p.shape, F32) for p in params[:ndp]]
    res = pl.pallas_call(
        body, name=name, grid=(t // tm,),
        in_specs=_row_specs(rows, tm) + [_full_spec(p) for p in params] + cot_specs + add_specs,
        out_specs=out_specs, out_shape=out_shape,
        compiler_params=_cp("arbitrary"),
    )(*[r[0] for r in rows], *params, *flat_cots, *[add_to[i] for i in add_idx])
    return list(res[:n_rg]), list(res[n_rg:])


def _rms(x, g):
    xf = x.astype(F32)
    return xf * lax.rsqrt(jnp.mean(xf * xf, axis=-1, keepdims=True) + NORM_EPS) * g


def _softplus(x):
    return jnp.maximum(x, 0.0) + jnp.log(1.0 + jnp.exp(-jnp.abs(x)))


def _fn_pre(x, g):
    return (_rms(x, g).astype(BF16),)


def _fn_res(x, u, g_post):
    return (x + _rms(u, g_post),)


def _fn_res_pre(x, u, g_post, g_pre):
    xn = x + _rms(u, g_post)
    return xn, _rms(xn, g_pre).astype(BF16)


def _fn_mix(zga, zgb, ya, yb):
    return ((jax.nn.sigmoid(zga) * ya + jax.nn.sigmoid(zgb) * yb).astype(BF16),)


def _fn_swiglu(gate, up):
    gate, up = gate.astype(F32), up.astype(F32)
    return ((gate * jax.nn.sigmoid(gate) * up).astype(BF16),)


def _fn_prep(zk, zw, za, zg, decay_base, d_up, iclr_base, i_up, g_up, kns, kis, e_hd, e_dh):
    w_log = -_softplus(-(decay_base + _dot(jnp.tanh(zw), d_up))) - 0.5
    lw = -jnp.exp(w_log)
    a = jax.nn.sigmoid(iclr_base + _dot(za, i_up))
    g = _dot(jax.nn.sigmoid(zg), g_up)
    kn = zk * kns
    ss = _dot(kn * kn, e_dh)
    inv = lax.rsqrt(jnp.maximum(ss, 1e-24))
    kk = kn * _dot_split_a(inv, e_hd)
    k2 = zk * (1.0 + (a - 1.0) * kis)
    return lw, k2, kk, a, g


def _fn_post(y, r, k2, v, g, lnx_w, lnx_b, bonus, e_hd, e_dh):
    mu = _dot_split_a(_dot(y, e_dh) * (1.0 / HEAD), e_hd)
    yc = y - mu
    var = _dot(yc * yc, e_dh) * (1.0 / HEAD)
    yn = yc * _dot_split_a(lax.rsqrt(var + GROUP_NORM_EPS), e_hd)
    bs = _dot_split_a(_dot(r * k2 * bonus, e_dh), e_hd)
    return (((yn * lnx_w + lnx_b + bs * v) * g).astype(BF16),)


def _shift_fwd(p, col0, ncols, mix, seq, *, name, cw=256):
    t = p.shape[0]
    assert col0 % cw == 0 and ncols % cw == 0 and t % seq == 0
    cb0 = col0 // cw

    def body(p_ref, m_ref, z_ref):
        pv = p_ref[...]
        row = lax.broadcasted_iota(jnp.int32, pv.shape, 0)
        prev = jnp.where(row == 0, 0.0, pltpu.roll(pv, 1, axis=0))
        z_ref[...] = pv + (prev - pv) * m_ref[...]

    return pl.pallas_call(
        body, name=name, grid=(t // seq, ncols // cw),
        in_specs=[pl.BlockSpec((seq, cw), lambda b, c: (b, c + cb0)), pl.BlockSpec((1, cw), lambda b, c: (0, c))],
        out_specs=pl.BlockSpec((seq, cw), lambda b, c: (b, c)),
        out_shape=jax.ShapeDtypeStruct((t, ncols), F32),
        compiler_params=_cp("parallel", "parallel"),
    )(p, mix)


def _shift_bwd(p, col0, ncols, mix, dz_parts, seq, *, name, cw=256):
    t = p.shape[0]
    cb0 = col0 // cw
    n = len(dz_parts)

    def body(*refs):
        p_ref, m_ref = refs[:2]
        dp_ref, dm_ref = refs[2 + n:]
        dz = refs[2][...].astype(F32)
        for r in refs[3:2 + n]:
            dz = dz + r[...].astype(F32)
        pv = p_ref[...]
        mixv = m_ref[...]
        row = lax.broadcasted_iota(jnp.int32, pv.shape, 0)
        prev = jnp.where(row == 0, 0.0, pltpu.roll(pv, 1, axis=0))
        u = dz * mixv
        nxt = jnp.where(row == seq - 1, 0.0, pltpu.roll(u, seq - 1, axis=0))
        dp_ref[...] = (dz - u + nxt).astype(dp_ref.dtype)

        @pl.when(pl.program_id(1) == 0)
        def _():
            dm_ref[...] = jnp.zeros_like(dm_ref)

        dm_ref[...] += jnp.sum(dz * (prev - pv), axis=0, keepdims=True)

    return pl.pallas_call(
        body, name=name, grid=(ncols // cw, t // seq),
        in_specs=[pl.BlockSpec((seq, cw), lambda c, b: (b, c + cb0)), pl.BlockSpec((1, cw), lambda c, b: (0, c))]
        + [pl.BlockSpec((seq, cw), lambda c, b: (b, c))] * n,
        out_specs=[pl.BlockSpec((seq, cw), lambda c, b: (b, c)), pl.BlockSpec((1, cw), lambda c, b: (0, c))],
        out_shape=[jax.ShapeDtypeStruct((t, ncols), BF16), jax.ShapeDtypeStruct((1, ncols), F32)],
        compiler_params=_cp("parallel", "arbitrary"),
    )(p, mix, *dz_parts)


def _each(f, *lists):
    return [f(*xs) for xs in zip(*lists)]


def _tri_inv(low):
    c = low[0].shape[0]
    ti = lax.broadcasted_iota(jnp.int32, (c, c), 0)
    si = lax.broadcasted_iota(jnp.int32, (c, c), 1)
    eye = (ti == si).astype(F32)
    inside = (ti // 4) == (si // 4)
    base = [jnp.where(inside, m, 0.0) for m in low]
    acc = _each(lambda m: _dot(eye - m, eye + _dot(m, m)), base)
    size = 8
    while size <= c:
        wider = (ti // size) == (si // size)
        keep = jnp.logical_and(wider, jnp.logical_not(inside))
        acc = _each(lambda p, m: p - _dot(_dot(p, jnp.where(keep, m, 0.0)), p), acc, low)
        inside, size = wider, size * 2
    return acc


def _stack_rows(a, b):
    return jnp.concatenate([a, b], axis=0)


@jax.custom_vjp
def _split_rows(x):
    h = x.shape[0] // 2
    return x[:h], x[h:]


def _split_rows_fwd(x):
    return _split_rows(x), None


def _split_rows_bwd(_, g):
    return (jnp.concatenate(g, axis=0),)


_split_rows.defvjp(_split_rows_fwd, _split_rows_bwd)


def _masked_halves(stacked, top_mask, bottom_mask):
    halves = _each(_split_rows, stacked)
    return ([jnp.where(top_mask, t, 0.0) for t, _ in halves], [jnp.where(bottom_mask, b, 0.0) for _, b in halves])


@jax.custom_vjp
def _tri_inv_known(low, inv):
    return inv


def _tri_inv_known_fwd(low, inv):
    return inv, inv


def _tri_inv_known_bwd(inv, g):
    dlow = _each(lambda t, gg: -_dot(_dot(t, gg, _TN), t, _NT), inv, g)
    return dlow, _each(jnp.zeros_like, inv)


_tri_inv_known.defvjp(_tri_inv_known_fwd, _tri_inv_known_bwd)


def _wkv_chunk(s0, r, lw, k, v, kk, a, inv=None):
    c = r[0].shape[0]
    ti = lax.broadcasted_iota(jnp.int32, (c, c), 0)
    si = lax.broadcasted_iota(jnp.int32, (c, c), 1)
    incl, strict = ti >= si, ti > si
    tri = incl.astype(F32)
    cum = _each(lambda x: _dot_split_b(tri, x, 3), lw)
    eg = _each(jnp.exp, cum)
    egp = _each(lambda cs, x: jnp.exp(cs - x), cum, lw)
    ei = _each(lambda cs: jnp.exp(-cs), cum)
    rh, kkh, kt = _each(jnp.multiply, r, eg), _each(jnp.multiply, kk, egp), _each(jnp.multiply, k, ei)
    bt = _each(lambda p, q, e: (p * q) * e, a, kk, ei)
    both = _each(_stack_rows, kkh, rh)
    on_b, on_k, on_s = _each(_dot_nt, both, bt), _each(_dot_nt, both, kt), _each(_dot_nt, both, s0)
    lb, mb = _masked_halves(on_b, strict, incl)
    lk, mk = _masked_halves(on_k, strict, incl)
    on_s = _each(_split_rows, on_s)
    on_v = _each(lambda p, q, x: _split_rows(_dot(_stack_rows(p, q), x)), lk, mk, v)
    rhs = _each(lambda p, q: p[0] + q[0], on_s, on_v)
    inv = _tri_inv(lb) if inv is None else _tri_inv_known(lb, inv)
    u = _each(lambda t, x: -_dot(t, x), inv, rhs)
    y = _each(lambda p, m1, uu, q: p[1] + _dot(m1, uu) + q[1], on_s, mb, u, on_v)
    s1 = _each(lambda s, uu, x, b, kq, w: (s + _dot_tn(_stack_rows(uu, x), _stack_rows(b, kq)))
               * jnp.exp(jnp.sum(w, axis=0, keepdims=True)), s0, u, v, bt, kt, lw)
    return y, s1, inv


WKV_HEADS = 16
WKV_COLS = WKV_HEADS * HEAD
WKV_GROUPS = N_HEADS // WKV_HEADS


def _head_cols(ref):
    return [ref[:, h * HEAD:(h + 1) * HEAD] for h in range(ref.shape[1] // HEAD)]


def _wkv_specs(seq, rev):
    nc = seq // CHUNK

    def rows(col0):
        cb0 = col0 // WKV_COLS
        if rev:
            return pl.BlockSpec((CHUNK, WKV_COLS), lambda b, h, c: (b * nc + nc - 1 - c, cb0 + h))
        return pl.BlockSpec((CHUNK, WKV_COLS), lambda b, h, c: (b * nc + c, cb0 + h))

    if rev:
        st = pl.BlockSpec((1, 1, WKV_HEADS, HEAD, HEAD), lambda b, h, c: (b * WKV_GROUPS + h, nc - 1 - c, 0, 0, 0))
    else:
        st = pl.BlockSpec((1, 1, WKV_HEADS, HEAD, HEAD), lambda b, h, c: (b * WKV_GROUPS + h, c, 0, 0, 0))
    return rows, st


def _wkv_fwd(z_rkv, lw, k2, kk, a, seq):
    t = z_rkv.shape[0]
    nb, nc = t // seq, seq // CHUNK
    rows, st = _wkv_specs(seq, False)

    def body(r_ref, v_ref, lw_ref, k_ref, kk_ref, a_ref, y_ref, st_ref, inv_ref, s_scr):
        @pl.when(pl.program_id(2) == 0)
        def _():
            s_scr[...] = jnp.zeros_like(s_scr)

        s0 = [s_scr[h] for h in range(WKV_HEADS)]
        y, s1, inv = _wkv_chunk(s0, *[_head_cols(ref) for ref in (r_ref, lw_ref, k_ref, v_ref, kk_ref, a_ref)])
        for h in range(WKV_HEADS):
            st_ref[0, 0, h] = s0[h]
            inv_ref[0, 0, h] = inv[h]
            y_ref[:, h * HEAD:(h + 1) * HEAD] = y[h]
            s_scr[h] = s1[h]

    per_chunk = jax.ShapeDtypeStruct((nb * WKV_GROUPS, nc, WKV_HEADS, HEAD, HEAD), F32)
    return pl.pallas_call(
        body, name="wkv_fwd", grid=(nb, WKV_GROUPS, nc),
        in_specs=[rows(0), rows(2 * D), rows(0), rows(0), rows(0), rows(0)],
        out_specs=[rows(0), st, st],
        out_shape=[jax.ShapeDtypeStruct((t, D), F32), per_chunk, per_chunk],
        scratch_shapes=[pltpu.VMEM((WKV_HEADS, HEAD, HEAD), F32)],
        compiler_params=_cp("parallel", "parallel", "arbitrary"),
    )(z_rkv, z_rkv, lw, k2, kk, a)


def _wkv_bwd(z_rkv, lw, k2, kk, a, states, invs, dy, seq):
    t = z_rkv.shape[0]
    nb, nc = t // seq, seq // CHUNK
    rows, st = _wkv_specs(seq, True)

    def body(r_ref, v_ref, lw_ref, k_ref, kk_ref, a_ref, st_ref, inv_ref, dy_ref,
             dr_ref, dlw_ref, dk_ref, dv_ref, dkk_ref, da_ref, ds_scr):
        @pl.when(pl.program_id(2) == 0)
        def _():
            ds_scr[...] = jnp.zeros_like(ds_scr)

        s0 = [st_ref[0, 0, h] for h in range(WKV_HEADS)]
        inv = [inv_ref[0, 0, h] for h in range(WKV_HEADS)]
        _, vjp = jax.vjp(lambda *args: _wkv_chunk(*args, inv=inv)[:2],
                         s0, *[_head_cols(ref) for ref in (r_ref, lw_ref, k_ref, v_ref, kk_ref, a_ref)])
        grads = vjp(([x.astype(F32) for x in _head_cols(dy_ref)], [ds_scr[h] for h in range(WKV_HEADS)]))
        for h in range(WKV_HEADS):
            ds_scr[h] = grads[0][h]
            for ref, g in zip((dr_ref, dlw_ref, dk_ref, dv_ref, dkk_ref, da_ref), grads[1:]):
                ref[:, h * HEAD:(h + 1) * HEAD] = g[h].astype(ref.dtype)

    return pl.pallas_call(
        body, name="wkv_bwd", grid=(nb, WKV_GROUPS, nc),
        in_specs=[rows(0), rows(2 * D), rows(0), rows(0), rows(0), rows(0), st, st, rows(0)],
        out_specs=[rows(0)] * 6,
        out_shape=[jax.ShapeDtypeStruct((t, D), BF16)] * 6,
        scratch_shapes=[pltpu.VMEM((WKV_HEADS, HEAD, HEAD), F32)],
        compiler_params=_cp("parallel", "parallel", "arbitrary"),
    )(z_rkv, z_rkv, lw, k2, kk, a, states, invs, dy)


def _softmax(s):
    e = jnp.exp(s - jnp.max(s, axis=-1, keepdims=True))
    return e * (1.0 / jnp.sum(e, axis=-1, keepdims=True))


ATT_FWD_HEADS = 16
ATT_HEADS = 8
ATT_COLS = ATT_HEADS * HEAD
ATT_GROUPS = N_HEADS // ATT_HEADS


def _attn_chunk(q, kb, vb, bias, valid):
    s = _each(lambda x, y, z: jnp.where(valid, _dot_nt(x * (HEAD ** -0.5), y) + z, MASK_VALUE), q, kb, bias)
    return _each(_dot, _each(_softmax, s), vb)


def _pad_fill(pad_ref, src_ref):
    pad_ref[0:LEFT, :] = jnp.zeros((LEFT, pad_ref.shape[1]), pad_ref.dtype)
    pad_ref[LEFT:, :] = src_ref[...].astype(pad_ref.dtype)


def _band_heads(pad_ref, start):
    return [pad_ref[pl.ds(start, BAND), h * HEAD:(h + 1) * HEAD] for h in range(pad_ref.shape[1] // HEAD)]


def _band_valid(c):
    return (c * CHUNK - LEFT + lax.broadcasted_iota(jnp.int32, (1, BAND), 1)) >= 0


def _bias_spec():
    return pl.BlockSpec((ATT_HEADS, CHUNK, BAND), lambda h, b, c: (h, 0, 0))


def _attn_fwd(proj, bias, seq):
    t = proj.shape[0]
    nb, nc = t // seq, seq // CHUNK
    heads = ATT_FWD_HEADS
    cols, groups = heads * HEAD, N_HEADS // heads
    cq = C_Q // cols

    def body(q_ref, k_ref, v_ref, b_ref, o_ref, kpad, vpad):
        c = pl.program_id(2)

        @pl.when(c == 0)
        def _():
            _pad_fill(kpad, k_ref)
            _pad_fill(vpad, v_ref)

        start = pl.multiple_of(c * CHUNK, CHUNK)
        o = _attn_chunk(_head_cols(q_ref), _band_heads(kpad, start), _band_heads(vpad, start),
                        [b_ref[h] for h in range(heads)], _band_valid(c))
        for h in range(heads):
            o_ref[:, h * HEAD:(h + 1) * HEAD] = o[h].astype(o_ref.dtype)

    return pl.pallas_call(
        body, name="attn_fwd", grid=(groups, nb, nc),
        in_specs=[pl.BlockSpec((CHUNK, cols), lambda h, b, c: (b * nc + c, cq + h)),
                  pl.BlockSpec((seq, cols), lambda h, b, c: (b, cq + groups + h)),
                  pl.BlockSpec((seq, cols), lambda h, b, c: (b, cq + 2 * groups + h)),
                  pl.BlockSpec((heads, CHUNK, BAND), lambda h, b, c: (h, 0, 0))],
        out_specs=pl.BlockSpec((CHUNK, cols), lambda h, b, c: (b * nc + c, h)),
        out_shape=jax.ShapeDtypeStruct((t, D), BF16),
        scratch_shapes=[pltpu.VMEM((seq + LEFT, cols), BF16)] * 2,
        compiler_params=_cp("parallel", "arbitrary", "arbitrary"),
    )(proj, proj, proj, bias)


def _attn_bwd(proj, bias, do, seq):
    t = proj.shape[0]
    nb, nc = t // seq, seq // CHUNK
    cq = C_Q // ATT_COLS

    def body(q_ref, k_ref, v_ref, b_ref, do_ref, dq_ref, dk_ref, dv_ref, db_ref, kpad, vpad, dkpad, dvpad):
        b, c = pl.program_id(1), pl.program_id(2)

        @pl.when(c == 0)
        def _():
            _pad_fill(kpad, k_ref)
            _pad_fill(vpad, v_ref)
            dkpad[...] = jnp.zeros_like(dkpad)
            dvpad[...] = jnp.zeros_like(dvpad)

        @pl.when(jnp.logical_and(b == 0, c == 0))
        def _():
            db_ref[...] = jnp.zeros_like(db_ref)

        start = pl.multiple_of(c * CHUNK, CHUNK)
        _, vjp = jax.vjp(functools.partial(_attn_chunk, valid=_band_valid(c)),
                         _head_cols(q_ref), _band_heads(kpad, start), _band_heads(vpad, start),
                         [b_ref[h] for h in range(ATT_HEADS)])
        dq, dkb, dvb, dbias = vjp([x.astype(F32) for x in _head_cols(do_ref)])
        for h in range(ATT_HEADS):
            sl = slice(h * HEAD, (h + 1) * HEAD)
            dq_ref[:, sl] = dq[h].astype(dq_ref.dtype)
            dkpad[pl.ds(start, BAND), sl] += dkb[h].astype(F32)
            dvpad[pl.ds(start, BAND), sl] += dvb[h].astype(F32)
            db_ref[h] += dbias[h]

        @pl.when(c == nc - 1)
        def _():
            dk_ref[...] = dkpad[LEFT:, :].astype(dk_ref.dtype)
            dv_ref[...] = dvpad[LEFT:, :].astype(dv_ref.dtype)

    kv_out = pl.BlockSpec((seq, ATT_COLS), lambda h, b, c: (b, h))
    return pl.pallas_call(
        body, name="attn_bwd", grid=(ATT_GROUPS, nb, nc),
        in_specs=[pl.BlockSpec((CHUNK, ATT_COLS), lambda h, b, c: (b * nc + c, cq + h)),
                  pl.BlockSpec((seq, ATT_COLS), lambda h, b, c: (b, cq + ATT_GROUPS + h)),
                  pl.BlockSpec((seq, ATT_COLS), lambda h, b, c: (b, cq + 2 * ATT_GROUPS + h)),
                  _bias_spec(),
                  pl.BlockSpec((CHUNK, ATT_COLS), lambda h, b, c: (b * nc + c, h))],
        out_specs=[pl.BlockSpec((CHUNK, ATT_COLS), lambda h, b, c: (b * nc + c, h)), kv_out, kv_out,
                   pl.BlockSpec((ATT_HEADS, CHUNK, BAND), lambda h, b, c: (h, 0, 0))],
        out_shape=[jax.ShapeDtypeStruct((t, D), BF16)] * 3 + [jax.ShapeDtypeStruct((N_HEADS, CHUNK, BAND), F32)],
        scratch_shapes=[pltpu.VMEM((seq + LEFT, ATT_COLS), BF16)] * 2 + [pltpu.VMEM((seq + LEFT, ATT_COLS), F32)] * 2,
        compiler_params=_cp("parallel", "arbitrary", "arbitrary"),
    )(proj, proj, proj, bias, do)


def _xattn_tile(q, k, v):
    s = _dot_nt(q, k) * ((MEM_WIDTH // MEM_HEADS) ** -0.5)
    return _dot(_softmax(s), v)


def _xattn_fwd(qm, kvm, seq, n_mem, tq=1024):
    t = qm.shape[0]
    tq = min(tq, seq)
    nb, nq = t // seq, seq // tq

    def body(q_ref, k_ref, v_ref, o_ref):
        o_ref[...] = _xattn_tile(q_ref[...], k_ref[...], v_ref[...]).astype(o_ref.dtype)

    return pl.pallas_call(
        body, name="xattn_fwd", grid=(nb, MEM_HEADS, nq),
        in_specs=[pl.BlockSpec((tq, LANE), lambda b, h, i: (b * nq + i, h)),
                  pl.BlockSpec((n_mem, LANE), lambda b, h, i: (b, h)),
                  pl.BlockSpec((n_mem, LANE), lambda b, h, i: (b, MEM_HEADS + h))],
        out_specs=pl.BlockSpec((tq, LANE), lambda b, h, i: (b * nq + i, h)),
        out_shape=jax.ShapeDtypeStruct((t, MEM_WIDTH), BF16),
        compiler_params=_cp("parallel", "parallel", "parallel"),
    )(qm, kvm, kvm)


def _xattn_bwd(qm, kvm, do, seq, n_mem, tq=1024):
    t = qm.shape[0]
    tq = min(tq, seq)
    nb, nq = t // seq, seq // tq

    def body(q_ref, k_ref, v_ref, do_ref, dq_ref, dkv_ref, dk_acc, dv_acc):
        i = pl.program_id(2)

        @pl.when(i == 0)
        def _():
            dk_acc[...] = jnp.zeros_like(dk_acc)
            dv_acc[...] = jnp.zeros_like(dv_acc)

        _, vjp = jax.vjp(_xattn_tile, q_ref[...], k_ref[...], v_ref[...])
        dq, dk, dv = vjp(do_ref[...].astype(F32))
        dq_ref[...] = dq.astype(dq_ref.dtype)
        dk_acc[...] += dk
        dv_acc[...] += dv

        @pl.when(i == nq - 1)
        def _():
            dkv_ref[0] = dk_acc[...].astype(dkv_ref.dtype)
            dkv_ref[1] = dv_acc[...].astype(dkv_ref.dtype)

    dq, dkv = pl.pallas_call(
        body, name="xattn_bwd", grid=(nb, MEM_HEADS, nq),
        in_specs=[pl.BlockSpec((tq, LANE), lambda b, h, i: (b * nq + i, h)),
                  pl.BlockSpec((n_mem, LANE), lambda b, h, i: (b, h)),
                  pl.BlockSpec((n_mem, LANE), lambda b, h, i: (b, MEM_HEADS + h)),
                  pl.BlockSpec((tq, LANE), lambda b, h, i: (b * nq + i, h))],
        out_specs=[pl.BlockSpec((tq, LANE), lambda b, h, i: (b * nq + i, h)),
                   pl.BlockSpec((2, n_mem, LANE), lambda b, h, i: (0, b, h))],
        out_shape=[jax.ShapeDtypeStruct((t, MEM_WIDTH), BF16), jax.ShapeDtypeStruct((2, nb * n_mem, MEM_WIDTH), BF16)],
        scratch_shapes=[pltpu.VMEM((n_mem, LANE), F32)] * 2,
        compiler_params=_cp("parallel", "parallel", "arbitrary"),
    )(qm, kvm, kvm, do)
    return dq, jnp.concatenate([dkv[0], dkv[1]], axis=1)


def _loss_head(x, u, g_post, target, tm=512):
    t, d = x.shape
    tm = min(tm, t)

    def tile_loss(xv, uv, gv, tv):
        diff = _fn_res(xv, uv, gv)[0] - tv
        return 0.5 * jnp.sum(jnp.mean(diff * diff, axis=-1, keepdims=True), axis=0, keepdims=True)

    def body(x_ref, u_ref, g_ref, t_ref, l_ref, dx_ref, du_ref, dg_ref):
        @pl.when(pl.program_id(0) == 0)
        def _():
            l_ref[...] = jnp.zeros_like(l_ref)
            dg_ref[...] = jnp.zeros_like(dg_ref)

        tv = t_ref[...]
        part, vjp = jax.vjp(lambda xv, uv, gv: tile_loss(xv, uv, gv, tv), x_ref[...], u_ref[...], g_ref[...])
        dx, du, dg = vjp(jnp.ones((1, 1), F32))
        l_ref[...] += part
        dx_ref[...] = dx
        du_ref[...] = du.astype(du_ref.dtype)
        dg_ref[...] += dg

    rows = pl.BlockSpec((tm, d), lambda i: (i, 0))
    vec = pl.BlockSpec((1, d), lambda i: (0, 0))
    return pl.pallas_call(
        body, name="loss_head", grid=(t // tm,),
        in_specs=[rows, rows, vec, rows],
        out_specs=[pl.BlockSpec((8, LANE), lambda i: (0, 0)), rows, rows, vec],
        out_shape=[jax.ShapeDtypeStruct((8, LANE), F32), jax.ShapeDtypeStruct((t, d), F32),
                   jax.ShapeDtypeStruct((t, d), BF16), jax.ShapeDtypeStruct((1, d), F32)],
        compiler_params=_cp("arbitrary"),
    )(x, u, g_post, target)


def _mesh_pos():
    return lax.axis_index("x"), lax.axis_index("y"), lax.axis_index("c")


def _peer(pos, d):
    x, y, c = pos
    return ((1 - x) if d & 4 else x, (1 - y) if d & 2 else y, (1 - c) if d & 1 else c)


def _flat(pos):
    return 4 * pos[0] + 2 * pos[1] + pos[2]


def _exchange(arrays, scatter, *, name):
    n = len(arrays)
    shapes = [a.shape[1:] if scatter else a.shape for a in arrays]

    def body(*refs):
        ins, outs = refs[:n], refs[n:2 * n]
        send, recv, loc = refs[2 * n:]
        pos = _mesh_pos()
        me = _flat(pos)
        pending = []
        for i in range(n):
            own = pltpu.make_async_copy(ins[i].at[me] if scatter else ins[i], outs[i].at[me], loc.at[i])
            own.start()
            pending.append(own)
            for d in range(1, N_DEV):
                peer = _peer(pos, d)
                src = ins[i].at[_flat(peer)] if scatter else ins[i]
                out_cp = pltpu.make_async_remote_copy(
                    src_ref=src, dst_ref=outs[i].at[me], send_sem=send.at[i, d - 1], recv_sem=recv.at[i, d - 1],
                    device_id=peer, device_id_type=pl.DeviceIdType.MESH)
                out_cp.start()
                pending.append(out_cp)
        for i in range(n):
            own = pending[i * N_DEV]
            for d in range(1, N_DEV):
                peer = _peer(pos, d)
                src = ins[i].at[_flat(peer)] if scatter else ins[i]
                pending[i * N_DEV + d].wait_send()
                pltpu.make_async_remote_copy(
                    src_ref=src, dst_ref=outs[i].at[_flat(peer)], send_sem=send.at[i, d - 1], recv_sem=recv.at[i, d - 1],
                    device_id=peer, device_id_type=pl.DeviceIdType.MESH).wait_recv()
            own.wait()

    hbm = pl.BlockSpec(memory_space=pltpu.HBM)
    return pl.pallas_call(
        body, name=name,
        in_specs=[hbm] * n, out_specs=[hbm] * n,
        out_shape=[jax.ShapeDtypeStruct((N_DEV,) + tuple(s), a.dtype) for s, a in zip(shapes, arrays)],
        scratch_shapes=[pltpu.SemaphoreType.DMA((n, N_DEV - 1)), pltpu.SemaphoreType.DMA((n, N_DEV - 1)),
                        pltpu.SemaphoreType.DMA((n,))],
    )(*arrays)


_HBM = pl.BlockSpec(memory_space=pltpu.HBM)
_SEM = pl.BlockSpec(memory_space=pltpu.SEMAPHORE)
_DATAFLOW = pltpu.SideEffectType.DATAFLOW_SIDE_EFFECTING


_ALL_PEERS = tuple(range(1, N_DEV))
_SIBLING_AND_SAME_CORE = (1, 2, 4, 6)


def _remote_copies(ins, lands, send, recv, scatter, dists):
    pos = _mesh_pos()
    me = _flat(pos)
    out = []
    for i in range(len(ins)):
        for j, d in enumerate(dists):
            peer = _peer(pos, d)
            src = ins[i].at[_flat(peer)] if scatter else ins[i]
            pair = i * len(dists) + j
            sems = dict(send_sem=send.at[pair], recv_sem=recv.at[pair], device_id=peer,
                        device_id_type=pl.DeviceIdType.MESH)
            out.append((pltpu.make_async_remote_copy(src_ref=src, dst_ref=lands[i].at[me], **sems),
                        pltpu.make_async_remote_copy(src_ref=src, dst_ref=lands[i].at[_flat(peer)], **sems)))
    return out


def _exchange_start(arrays, scatter, after, *, name, dists=_ALL_PEERS):
    n = len(arrays)
    shapes = [a.shape[1:] if scatter else a.shape for a in arrays]
    lands = [pltpu.with_memory_space_constraint(lax.empty((N_DEV,) + tuple(s), a.dtype), pltpu.HBM)
             for s, a in zip(shapes, arrays)]
    srcs = [pltpu.with_memory_space_constraint(a, pltpu.HBM) for a in arrays]

    def body(*refs):
        ins, land_refs = refs[:n], refs[n:2 * n]
        send, recv, token = refs[2 * n + 1], refs[2 * n + 2], refs[-1]
        for going, _ in _remote_copies(ins, land_refs, send, recv, scatter, dists):
            going.start()
        token[...] = jnp.zeros_like(token)

    sems = pltpu.SemaphoreType.DMA((n * len(dists),))
    res = pl.pallas_call(
        body, name=name,
        out_shape=(sems, sems, *[pltpu.HBM(a.shape, a.dtype) for a in srcs + lands], jax.ShapeDtypeStruct((8, LANE), F32)),
        in_specs=[_HBM] * (2 * n) + [pl.BlockSpec(memory_space=pl.ANY)],
        out_specs=(_SEM, _SEM, *[_HBM] * (2 * n), pl.BlockSpec(memory_space=pltpu.VMEM)),
        input_output_aliases={i: 2 + i for i in range(2 * n)},
        compiler_params=pltpu.CompilerParams(has_side_effects=_DATAFLOW),
    )(*srcs, *lands, after)
    return (n, scatter, dists, res[0], res[1], list(res[2:2 + 2 * n])), res[-1]


def _exchange_wait(handle, after, own, *, name):
    n, scatter, dists, send, recv, thru = handle

    def body(*refs):
        ins, land_refs = refs[:n], refs[n:2 * n]
        for going, coming in _remote_copies(ins, land_refs, refs[2 * n], refs[2 * n + 1], scatter, dists):
            going.wait_send()
            coming.wait_recv()

    res = pl.pallas_call(
        body, name=name,
        out_shape=tuple(pltpu.HBM(a.shape, a.dtype) for a in thru),
        in_specs=[_HBM] * (2 * n) + [_SEM, _SEM] + [pl.BlockSpec(memory_space=pl.ANY)] * len(after),
        out_specs=tuple([_HBM] * (2 * n)),
        input_output_aliases={i: i for i in range(2 * n)},
        compiler_params=pltpu.CompilerParams(has_side_effects=_DATAFLOW),
    )(*thru, send, recv, *after)
    me = _flat(_mesh_pos())
    return [lax.dynamic_update_slice_in_dim(land, o[None].astype(land.dtype), me, 0) for land, o in zip(res[n:], own)]


_OTHER_CHIPS = (2, 4, 6)


def _relay_to_sibling(gathered, *, name):
    n, k = len(gathered), len(_OTHER_CHIPS)

    def body(*refs):
        ins, outs = refs[:n], refs[n:2 * n]
        send, recv = refs[2 * n:]
        pos = _mesh_pos()
        copies = []
        for i in range(n):
            for j, d in enumerate(_OTHER_CHIPS):
                cp = pltpu.make_async_remote_copy(
                    src_ref=ins[i].at[_flat(_peer(pos, d))], dst_ref=outs[i].at[j],
                    send_sem=send.at[i * k + j], recv_sem=recv.at[i * k + j],
                    device_id=_peer(pos, 1), device_id_type=pl.DeviceIdType.MESH)
                cp.start()
                copies.append(cp)
        for cp in copies:
            cp.wait()

    return pl.pallas_call(
        body, name=name, in_specs=[_HBM] * n, out_specs=[_HBM] * n,
        out_shape=[jax.ShapeDtypeStruct((k,) + g.shape[1:], g.dtype) for g in gathered],
        scratch_shapes=[pltpu.SemaphoreType.DMA((n * k,)), pltpu.SemaphoreType.DMA((n * k,))],
    )(*gathered)


def _adamw(parts, w, m, v, *, name, tr=128, after=None):
    r, c = w.shape
    align = 8 * 4 // parts.dtype.itemsize
    row_tiles = [d for d in range(align, min(tr, r) + 1, align) if r % d == 0]
    tr, tc = (max(row_tiles), c) if row_tiles else (r, LANE)
    assert c % tc == 0
    n_after = 0 if after is None else 1

    def body(p_ref, w_ref, m_ref, v_ref, *rest):
        g_ref, d_ref, nm_ref, nv_ref = rest[n_after:]
        g = p_ref[0].astype(F32)
        for j in range(1, N_DEV):
            g = g + p_ref[j].astype(F32)
        m2 = ADAM_B1 * m_ref[...] + (1.0 - ADAM_B1) * g
        v2 = ADAM_B2 * v_ref[...] + (1.0 - ADAM_B2) * (g * g)
        m_hat = m2 / (1.0 - ADAM_B1 ** ADAM_STEP)
        v_hat = v2 / (1.0 - ADAM_B2 ** ADAM_STEP)
        g_ref[...] = g
        d_ref[...] = -ADAM_LR * (m_hat / (jnp.sqrt(v_hat) + ADAM_EPS) + ADAM_WD * w_ref[...])
        nm_ref[...] = m2
        nv_ref[...] = v2

    spec = pl.BlockSpec((tr, tc), lambda i, j: (i, j))
    return pl.pallas_call(
        body, name=name, grid=(r // tr, c // tc),
        in_specs=[pl.BlockSpec((N_DEV, tr, tc), lambda i, j: (0, i, j)), spec, spec, spec]
        + [pl.BlockSpec(memory_space=pl.ANY)] * n_after,
        out_specs=[spec] * 4, out_shape=[jax.ShapeDtypeStruct((r, c), F32)] * 4,
        compiler_params=_cp("parallel", "parallel"),
    )(parts, w, m, v, *([] if after is None else [after]))


def _cols_to_full(g):
    return jnp.transpose(g, (1, 0, 2)).reshape(g.shape[1], N_DEV * g.shape[2])


def _full_to_cols(w):
    r, c = w.shape
    return jnp.transpose(w.reshape(r, N_DEV, c // N_DEV), (1, 0, 2))


def _cut(a, lo, hi, axis):
    return lax.slice_in_dim(a, lo, hi, axis=axis)


def _pad_to(a, size, axis):
    pads = [(0, 0)] * a.ndim
    pads[axis] = (0, size - a.shape[axis])
    return jnp.pad(a, pads)


def _pad_lora(w, axis=1):
    return jnp.concatenate([
        _pad_to(_cut(w, 0, LORA_W, axis), 128, axis), _pad_to(_cut(w, LORA_W, LORA_W + LORA_A, axis), 128, axis),
        _pad_to(_cut(w, LORA_W + LORA_A, w.shape[axis], axis), 256, axis)], axis=axis)


def _unpad_lora(wp, axis=1):
    return jnp.concatenate([_cut(wp, 0, LORA_W, axis), _cut(wp, 128, 128 + LORA_A, axis),
                            _cut(wp, 256, 256 + LORA_G, axis)], axis=axis)


def _permute_in(w, axis):
    rk = 3 * D
    lo = rk + LORA_W + LORA_A + LORA_G
    return jnp.concatenate([_cut(w, 0, rk, axis), _cut(w, lo, w.shape[axis], axis), _pad_lora(_cut(w, rk, lo, axis), axis)],
                           axis=axis)


def _unpermute_in(wp, axis):
    return jnp.concatenate([_cut(wp, 0, 3 * D, axis), _unpad_lora(_cut(wp, C_LORA, P_WIDTH, axis), axis),
                            _cut(wp, 3 * D, C_LORA, axis)], axis=axis)


def _rel_index():
    dist = jnp.arange(CHUNK)[:, None] - jnp.arange(BAND)[None, :] + LEFT
    return (jnp.minimum(dist, REL_CLIP) + (CHUNK - 1)).reshape(-1)


def _local_step(x, mem, target, wt, seq, n_mem, comm):
    t = x.shape[0]
    row = lambda a: a.reshape(1, -1).astype(F32)
    g_pre_mix, g_post_mix = row(wt["g_pre_mix"]), row(wt["g_post_mix"])
    g_pre_cross, g_post_cross, g_mem = row(wt["g_pre_cross"]), row(wt["g_post_cross"]), row(wt["g_mem"])
    g_pre_ffn, g_post_ffn = row(wt["g_pre_ffn"]), row(wt["g_post_ffn"])
    mix = row(wt["shift_mix"])
    mix_rkv, mix_lora = mix[:, :3 * D], _pad_lora(mix[:, 3 * D:])
    decay_base, iclr_base = row(wt["decay_base"]), row(wt["iclr_base"])
    kns, kis = row(wt["key_norm_scale"]), row(wt["key_iclr_scale"])
    lnx_w, lnx_b, bonus = row(wt["lnx_w"]), row(wt["lnx_b"]), row(wt["bonus_scale"])
    e_dh = (jnp.arange(D)[:, None] // HEAD == jnp.arange(N_HEADS)[None, :]).astype(F32)
    e_hd = e_dh.T
    onehot = (jnp.arange(REL_TABLE)[:, None] == _rel_index()[None, :]).astype(BF16)

    begun = comm.begun
    (h1,) = _rowwise(_fn_pre, [_win(x)], [g_pre_mix], [(D, BF16)], name="pre_mix", tm=512, after=begun)
    (mn,) = _rowwise(_fn_pre, [_win(mem)], [g_mem], [(D, BF16)], name="pre_mem", tm=512, after=begun)
    bias = _mm(wt["rel_bias"].astype(F32), onehot, name="mm_bias", split_a=3, after=begun).reshape(N_HEADS, CHUNK, BAND)
    wt = {**wt, **comm.first_weights([h1, mn, bias])}
    w_in = wt["w_in_p"]
    d_up = jnp.pad(wt["decay_up"].astype(F32), ((0, 128 - LORA_W), (0, 0)))
    i_up = jnp.pad(wt["iclr_up"].astype(F32), ((0, 128 - LORA_A), (0, 0)))
    g_up = jnp.pad(wt["gate_up"].astype(F32), ((0, 256 - LORA_G), (0, 0)))
    proj = _mm(h1, w_in, tb=True, name="mm_in", after=comm.first_token)
    z_rkv = _shift_fwd(proj, 0, 3 * D, mix_rkv, seq, name="shift_rkv")
    z_lora = _shift_fwd(proj, C_LORA, 512, mix_lora, seq, name="shift_lora")
    prep_rows = [_win(z_rkv, D, D), _win(z_lora, 0, 128), _win(z_lora, 128, 128), _win(z_lora, 256, 256)]
    prep_params = [decay_base, d_up, iclr_base, i_up, g_up, kns, kis, e_hd, e_dh]
    lw, k2, kk, a, g = _rowwise(_fn_prep, prep_rows, prep_params, [(D, F32)] * 5, name="rwkv_prep", tm=256)
    y, states, invs = _wkv_fwd(z_rkv, lw, k2, kk, a, seq)
    post_rows = [_win(y), _win(z_rkv, 0, D), _win(k2), _win(z_rkv, 2 * D, D), _win(g)]
    post_params = [lnx_w, lnx_b, bonus, e_hd, e_dh]
    (y_a,) = _rowwise(_fn_post, post_rows, post_params, [(D, BF16)], name="rwkv_post", tm=256)
    y_b = _attn_fwd(proj, bias, seq)
    wt = {**wt, **comm.late_weights(y_b)}
    ya_p = _mm(y_a, wt["w_branch_a"], name="mm_a")
    yb_p = _mm(y_b, wt["w_branch_b"], name="mm_b")
    mix_rows = [_win(proj, C_GA, D), _win(proj, C_GA + D, D), _win(ya_p), _win(yb_p)]
    (mixed,) = _rowwise(_fn_mix, mix_rows, [], [(D, BF16)], name="gate_mix", tm=512)
    mo = _mm(mixed, wt["w_out"], name="mm_out")
    x1, h2 = _rowwise(_fn_res_pre, [_win(x), _win(mo)], [g_post_mix, g_pre_cross], [(D, F32), (D, BF16)],
                      name="res_mix", tm=512)
    qm = _mm(h2, wt["w_q_mem"], name="mm_q")
    kvm = _mm(mn, wt["w_kv_mem"], name="mm_kv")
    om = _xattn_fwd(qm, kvm, seq, n_mem)
    co = _mm(om, wt["w_o_mem"], name="mm_o")
    x2, h3 = _rowwise(_fn_res_pre, [_win(x1), _win(co)], [g_post_cross, g_pre_ffn], [(D, F32), (D, BF16)],
                      name="res_cross", tm=512)
    gu = _mm(h3, wt["w_ffn_in"], tb=True, name="mm_ffn_in", out_dtype=BF16)
    (act,) = _rowwise(_fn_swiglu, [_win(gu, 0, FFN), _win(gu, FFN, FFN)], [], [(FFN, BF16)], name="swiglu", tm=512)
    ff = _mm(act, wt["w_ffn_out"], name="mm_ffn_out")

    gw = {}
    loss, dx2, dff, gw["g_post_ffn"] = _loss_head(x2, ff, g_post_ffn, target)
    dact = _mm(dff, wt["w_ffn_out"], tb=True, name="mm_ffn_out_dx", out_dtype=BF16)
    gw["w_ffn_out"] = _mm(act, dff, ta=True, name="mm_ffn_out_dw", out_dtype=BF16)
    (dgu,), _ = _rowwise_bwd(_fn_swiglu, [_win(gu, 0, FFN), _win(gu, FFN, FFN)], [], 0, [[dact]],
                             name="swiglu_bwd", tm=512, row_grad=[BF16, BF16], packed=True)
    dh3 = _mm(dgu, wt["w_ffn_in"], name="mm_ffn_in_dx", out_dtype=BF16)
    gw["w_ffn_in"] = _mm(dgu, h3, ta=True, name="mm_ffn_in_dw", out_dtype=BF16)
    (dx1, dco), (gw["g_post_cross"], gw["g_pre_ffn"]) = _rowwise_bwd(
        _fn_res_pre, [_win(x1), _win(co)], [g_post_cross, g_pre_ffn], 0, [[dx2], [dh3]],
        name="res_cross_bwd", tm=512, row_grad=[F32, BF16])
    dom = _mm(dco, wt["w_o_mem"], tb=True, name="mm_o_dx", out_dtype=BF16)
    gw["w_o_mem"] = _mm(om, dco, ta=True, name="mm_o_dw", out_dtype=BF16)
    dqm, dkvm = _xattn_bwd(qm, kvm, dom, seq, n_mem)
    dh2 = _mm(dqm, wt["w_q_mem"], tb=True, name="mm_q_dx", out_dtype=BF16)
    gw["w_q_mem"] = _mm(h2, dqm, ta=True, name="mm_q_dw", out_dtype=BF16)
    dmn = _mm(dkvm, wt["w_kv_mem"], tb=True, name="mm_kv_dx", out_dtype=BF16)
    gw["w_kv_mem"] = _mm(mn, dkvm, ta=True, name="mm_kv_dw", out_dtype=BF16)
    _, (gw["g_mem"],) = _rowwise_bwd(_fn_pre, [_win(mem)], [g_mem], 0, [[dmn]], name="pre_mem_bwd", tm=256,
                                     row_grad=[None])
    (dx0, dmo), (gw["g_post_mix"], gw["g_pre_cross"]) = _rowwise_bwd(
        _fn_res_pre, [_win(x), _win(mo)], [g_post_mix, g_pre_cross], 0, [[dx1], [dh2]],
        name="res_mix_bwd", tm=512, row_grad=[F32, BF16])
    dmixed = _mm(dmo, wt["w_out"], tb=True, name="mm_out_dx", out_dtype=BF16)
    gw["w_out"] = _mm(mixed, dmo, ta=True, name="mm_out_dw", out_dtype=BF16)
    (dzga, dzgb, dya_p, dyb_p), _ = _rowwise_bwd(_fn_mix, mix_rows, [], 0, [[dmixed]], name="gate_mix_bwd", tm=512,
                                                 row_grad=[BF16] * 4)
    gw["w_branch_a"] = _mm(y_a, dya_p, ta=True, name="mm_a_dw", out_dtype=BF16)
    gw["w_branch_b"] = _mm(y_b, dyb_p, ta=True, name="mm_b_dw", out_dtype=BF16)
    token = comm.send_early(gw)
    dy_a = _mm(dya_p, wt["w_branch_a"], tb=True, name="mm_a_dx", out_dtype=BF16, after=token)
    dy_b = _mm(dyb_p, wt["w_branch_b"], tb=True, name="mm_b_dx", out_dtype=BF16, after=token)
    dq, dk, dv, dbias = _attn_bwd(proj, bias, dy_b, seq)
    gw["rel_bias"] = _mm(dbias.reshape(N_HEADS, CHUNK * BAND), onehot, tb=True, name="mm_bias_dw", split_a=2)
    (dy, dr_p, dk2_p, dv_p, dg), (gw["lnx_w"], gw["lnx_b"], gw["bonus_scale"]) = _rowwise_bwd(
        _fn_post, post_rows, post_params, 2, [[dy_a]], name="rwkv_post_bwd", tm=512, row_grad=[BF16] * 5)
    dr_s, dlw, dk2_s, dv_s, dkk, da = _wkv_bwd(z_rkv, lw, k2, kk, a, states, invs, dy, seq)
    (dzk, dzw, dza, dzg), pg = _rowwise_bwd(
        _fn_prep, prep_rows, prep_params, 2, [[dlw], [dk2_p, dk2_s], [dkk], [da], [dg]],
        name="rwkv_prep_bwd", tm=512, row_grad=[BF16] * 4)
    gw["decay_base"], gd_up, gw["iclr_base"], gi_up, gg_up, gw["key_norm_scale"], gw["key_iclr_scale"] = pg
    gw["decay_up"], gw["iclr_up"], gw["gate_up"] = gd_up[:LORA_W], gi_up[:LORA_A], gg_up[:LORA_G]
    dp_r, gmix_r = _shift_bwd(proj, 0, D, mix_rkv[:, :D], [dr_p, dr_s], seq, name="shift_r_bwd")
    dp_k, gmix_k = _shift_bwd(proj, D, D, mix_rkv[:, D:2 * D], [dzk], seq, name="shift_k_bwd")
    dp_v, gmix_v = _shift_bwd(proj, 2 * D, D, mix_rkv[:, 2 * D:], [dv_p, dv_s], seq, name="shift_v_bwd")
    dp_lora, gmix_lora = _shift_bwd(proj, C_LORA, 512, mix_lora, [jnp.concatenate([dzw, dza, dzg], axis=1)], seq,
                                    name="shift_lora_bwd")
    gw["shift_mix"] = jnp.concatenate([gmix_r, gmix_k, gmix_v, _unpad_lora(gmix_lora)], axis=1)
    dproj = [dp_r, dp_k, dp_v, dq, dk, dv, dzga, dzgb, dp_lora]
    gw["w_in_p"] = _mm_cat_tn(dproj, h1, name="mm_in_dw", after=gw["rel_bias"])
    token = comm.send_late(gw)
    dh1 = _mm_cat_nn(dproj, w_in, name="mm_in_dx", after=token)
    (grad_x,), (gw["g_pre_mix"],) = _rowwise_bwd(_fn_pre, [_win(x)], [g_pre_mix], 0, [[dh1]], name="pre_mix_bwd",
                                                 tm=512, row_grad=[F32], add_to={0: dx0})
    return loss, grad_x, gw


_COL_SHARDED = ("w_in", "decay_up", "iclr_up", "gate_up", "w_o_mem", "w_ffn_in")
_ROW_SHARDED = ("w_branch_a", "w_branch_b", "w_out", "w_q_mem", "w_kv_mem", "w_ffn_out")
_TRANSPOSED = ("w_in", "w_ffn_in")
_FIRST = ("w_in", "decay_up", "iclr_up", "gate_up")
_REST = ("w_o_mem", "w_ffn_in", "w_branch_a", "w_branch_b", "w_out", "w_q_mem", "w_kv_mem", "w_ffn_out")
_REPLICATED = ("g_pre_mix", "g_post_mix", "shift_mix", "decay_base", "iclr_base", "key_norm_scale", "key_iclr_scale",
               "bonus_scale", "lnx_w", "lnx_b", "rel_bias", "g_pre_cross", "g_post_cross", "g_mem", "g_pre_ffn",
               "g_post_ffn")
_WEIGHTS = ("g_pre_mix", "g_post_mix", "w_in", "shift_mix", "decay_base", "decay_up", "iclr_base", "iclr_up", "gate_up",
            "key_norm_scale", "key_iclr_scale", "bonus_scale", "lnx_w", "lnx_b", "rel_bias", "w_branch_a", "w_branch_b",
            "w_out", "g_pre_cross", "g_post_cross", "g_mem", "w_q_mem", "w_kv_mem", "w_o_mem", "g_pre_ffn", "g_post_ffn",
            "w_ffn_in", "w_ffn_out")
_PACK_ROWS = 8 * ((sum({"shift_mix": 3360, "bonus_scale": 1024, "rel_bias": 3072}.get(n, D) for n in _REPLICATED)
                   + 1 + 8 * LANE - 1) // (8 * LANE))


def _pack(vals):
    flat = jnp.concatenate([v.reshape(-1).astype(F32) for v in vals])
    return jnp.pad(flat, (0, _PACK_ROWS * LANE - flat.shape[0])).reshape(_PACK_ROWS, LANE)


def _unpack(packed, shapes):
    flat, out, pos = packed.reshape(-1), [], 0
    for s in shapes:
        n = math.prod(s)
        out.append(flat[pos:pos + n].reshape(s))
        pos += n
    return out


def _step(args, seq, n_mem):
    names = ("x", "mem") + _WEIGHTS + ("loss_target",) + tuple("m_" + n for n in _WEIGHTS) + tuple("v_" + n for n in _WEIGHTS)
    given = dict(zip(names, args))
    nb = given["x"].shape[0]
    x = given["x"].reshape(nb * seq, D)
    mem = given["mem"].reshape(nb * n_mem, D)
    target = given["loss_target"].reshape(nb * seq, D)
    def local(name, prefix=""):
        a = given[prefix + name][0]
        return a.T if name in _TRANSPOSED else a

    shard = {n: local(n) for n in _COL_SHARDED + _ROW_SHARDED}
    stacked = _ROW_SHARDED + _TRANSPOSED
    out = {}

    def wire(name):
        return shard[name].astype(BF16)

    def full(name, g):
        return g.reshape(-1, g.shape[-1]) if name in stacked else _cols_to_full(g)

    def blocks_of(name, g):
        return (g.reshape((N_DEV,) + shard[name].shape) if name in stacked else _full_to_cols(g)).astype(BF16)

    def update(names, landed, after=None):
        done = []
        for n, parts in zip(names, landed):
            res = _adamw(parts, shard[n], local(n, "m_"), local(n, "v_"), name="adamw_" + n, after=after)
            for kind, r in zip(("grad_", "delta_", "new_m_", "new_v_"), res):
                out[kind + n] = (r.T if n in _TRANSPOSED else r)[None]
            done.append(res[0])
        return done


    class Exchanges:
        def __init__(self):
            srcs = [wire(n) for n in _FIRST]
            self.first, self.begun = _exchange_start(srcs, False, srcs[0], name="gather_first_start",
                                                     dists=_SIBLING_AND_SAME_CORE)

        def first_weights(self, after):
            got = _exchange_wait(self.first, after, [wire(n) for n in _FIRST], name="gather_first_wait")
            relayed = _relay_to_sibling(got, name="gather_first_relay")
            pos = _mesh_pos()
            for j, d in enumerate(_OTHER_CHIPS):
                slot = _flat(_peer(pos, d | 1))
                got = [lax.dynamic_update_slice_in_dim(g, r[j][None], slot, 0) for g, r in zip(got, relayed)]
            self.rest, self.first_token = _exchange_start(
                [wire(n) for n in _REST], False, got[0], name="gather_rest_start")
            first = {n: full(n, g) for n, g in zip(_FIRST, got)}
            first["w_in_p"] = _permute_in(first.pop("w_in"), 0)
            return first

        def late_weights(self, after):
            got = _exchange_wait(self.rest, [after], [wire(n) for n in _REST], name="gather_rest_wait")
            return {n: full(n, g) for n, g in zip(_REST, got)}

        def send_early(self, gw):
            self.early_blocks = [blocks_of(n, gw[n]) for n in _REST]
            self.early, token = _exchange_start(self.early_blocks, True, self.early_blocks[-1], name="scatter_rest_start")
            return token

        def send_late(self, gw):
            me = _flat(_mesh_pos())
            own = [lax.dynamic_index_in_dim(b, me, 0, keepdims=False) for b in self.early_blocks]
            landed = _exchange_wait(self.early, [gw["w_in_p"]], own, name="scatter_rest_wait")
            grads = {**gw, "w_in": _unpermute_in(gw["w_in_p"], 0)}
            self.late_blocks = [blocks_of(n, grads[n]) for n in _FIRST]
            self.late, token = _exchange_start(self.late_blocks, True, landed[0], name="scatter_first_start")
            self.updated = update(_REST, landed, after=token)
            return token

        def finish(self, after):
            me = _flat(_mesh_pos())
            own = [lax.dynamic_index_in_dim(b, me, 0, keepdims=False) for b in self.late_blocks]
            update(_FIRST, _exchange_wait(self.late, [*after, *self.updated], own, name="scatter_first_wait"))

    comm = Exchanges()
    wt = {n: given[n][0] for n in _REPLICATED}
    loss_tile, grad_x, gw = _local_step(x, mem, target, wt, seq, n_mem, comm)
    rep_shapes = [given[n].shape for n in _REPLICATED]
    packed, _ = lax.optimization_barrier((_pack([gw[n] for n in _REPLICATED] + [loss_tile[0, 0]]), tuple(comm.updated)))
    small = _exchange([packed], False, name="gather_small")[0]
    zero = jnp.zeros((), F32)
    res = _adamw(small, *[_pack([given[p + n] for n in _REPLICATED] + [zero]) for p in ("", "m_", "v_")],
                 name="adamw_small", tr=_PACK_ROWS)
    for kind, r in zip(("grad_", "delta_", "new_m_", "new_v_"), res):
        for n, val in zip(_REPLICATED, _unpack(r, rep_shapes)):
            out[kind + n] = val
    loss = res[0].reshape(-1)[sum(math.prod(s) for s in rep_shapes)]
    comm.finish([grad_x, res[0]])
    grad_x = grad_x.reshape(nb, seq, D)
    return (loss, grad_x, *[out[k + n] for k in ("grad_", "delta_", "new_m_", "new_v_") for n in _WEIGHTS])


def kernel(x, mem, g_pre_mix, g_post_mix, w_in, shift_mix, decay_base, decay_up, iclr_base, iclr_up, gate_up, key_norm_scale, key_iclr_scale, bonus_scale, lnx_w, lnx_b, rel_bias, w_branch_a, w_branch_b, w_out, g_pre_cross, g_post_cross, g_mem, w_q_mem, w_kv_mem, w_o_mem, g_pre_ffn, g_post_ffn, w_ffn_in, w_ffn_out, loss_target, m_g_pre_mix, m_g_post_mix, m_w_in, m_shift_mix, m_decay_base, m_decay_up, m_iclr_base, m_iclr_up, m_gate_up, m_key_norm_scale, m_key_iclr_scale, m_bonus_scale, m_lnx_w, m_lnx_b, m_rel_bias, m_w_branch_a, m_w_branch_b, m_w_out, m_g_pre_cross, m_g_post_cross, m_g_mem, m_w_q_mem, m_w_kv_mem, m_w_o_mem, m_g_pre_ffn, m_g_post_ffn, m_w_ffn_in, m_w_ffn_out, v_g_pre_mix, v_g_post_mix, v_w_in, v_shift_mix, v_decay_base, v_decay_up, v_iclr_base, v_iclr_up, v_gate_up, v_key_norm_scale, v_key_iclr_scale, v_bonus_scale, v_lnx_w, v_lnx_b, v_rel_bias, v_w_branch_a, v_w_branch_b, v_w_out, v_g_pre_cross, v_g_post_cross, v_g_mem, v_w_q_mem, v_w_kv_mem, v_w_o_mem, v_g_pre_ffn, v_g_post_ffn, v_w_ffn_in, v_w_ffn_out):
    args = (x, mem, g_pre_mix, g_post_mix, w_in, shift_mix, decay_base, decay_up, iclr_base, iclr_up, gate_up, key_norm_scale, key_iclr_scale, bonus_scale, lnx_w, lnx_b, rel_bias, w_branch_a, w_branch_b, w_out, g_pre_cross, g_post_cross, g_mem, w_q_mem, w_kv_mem, w_o_mem, g_pre_ffn, g_post_ffn, w_ffn_in, w_ffn_out, loss_target, m_g_pre_mix, m_g_post_mix, m_w_in, m_shift_mix, m_decay_base, m_decay_up, m_iclr_base, m_iclr_up, m_gate_up, m_key_norm_scale, m_key_iclr_scale, m_bonus_scale, m_lnx_w, m_lnx_b, m_rel_bias, m_w_branch_a, m_w_branch_b, m_w_out, m_g_pre_cross, m_g_post_cross, m_g_mem, m_w_q_mem, m_w_kv_mem, m_w_o_mem, m_g_pre_ffn, m_g_post_ffn, m_w_ffn_in, m_w_ffn_out, v_g_pre_mix, v_g_post_mix, v_w_in, v_shift_mix, v_decay_base, v_decay_up, v_iclr_base, v_iclr_up, v_gate_up, v_key_norm_scale, v_key_iclr_scale, v_bonus_scale, v_lnx_w, v_lnx_b, v_rel_bias, v_w_branch_a, v_w_branch_b, v_w_out, v_g_pre_cross, v_g_post_cross, v_g_mem, v_w_q_mem, v_w_kv_mem, v_w_o_mem, v_g_pre_ffn, v_g_post_ffn, v_w_ffn_in, v_w_ffn_out)
    return _step(args, x.shape[1], mem.shape[1])
```

```python
import functools
import math

import jax
import jax.numpy as jnp
from jax import lax
from jax.experimental import pallas as pl
from jax.experimental.pallas import tpu as pltpu

F32 = jnp.float32
BF16 = jnp.bfloat16

N_DEV = 8
D = 1024
HEAD = 64
N_HEADS = D // HEAD
LANE = 128
CHUNK = 64
LEFT = 8 * CHUNK
BAND = LEFT + CHUNK
REL_CLIP = 128
REL_TABLE = CHUNK + REL_CLIP
MEM_WIDTH = D // 2
MEM_HEADS = 4
FFN = 2816
LORA_W, LORA_A, LORA_G = 64, 64, 160
P_WIDTH = 3 * D + 3 * D + 2 * D + 128 + 128 + 256
C_Q, C_GA, C_LORA = 3 * D, 6 * D, 8 * D
NORM_EPS = 1e-6
GROUP_NORM_EPS = 64e-5
MASK_VALUE = -1e30
ADAM_LR, ADAM_B1, ADAM_B2, ADAM_EPS, ADAM_WD, ADAM_STEP = 0.001, 0.9, 0.999, 1e-08, 0.01, 10
VMEM_LIMIT = 56 * 1024 * 1024


def _cp(*sem):
    return pltpu.CompilerParams(dimension_semantics=sem, vmem_limit_bytes=VMEM_LIMIT)


_NN, _NT, _TN = ((1,), (0,)), ((1,), (1,)), ((0,), (0,))


def _dot_raw(a, b, dims):
    return lax.dot_general(a.astype(BF16), b.astype(BF16), (dims, ((), ())), preferred_element_type=F32)


@functools.partial(jax.custom_vjp, nondiff_argnums=(2,))
def _dot_dims(a, b, dims):
    return _dot_raw(a, b, dims)


def _dot_dims_fwd(a, b, dims):
    return _dot_raw(a, b, dims), (a, b)


def _dot_dims_bwd(dims, res, g):
    a, b = res
    if dims == _NN:
        da, db = _dot_raw(g, b, _NT), _dot_raw(a, g, _TN)
    elif dims == _NT:
        da, db = _dot_raw(g, b, _NN), _dot_raw(g, a, _TN)
    else:
        da, db = _dot_raw(b, g, _NT), _dot_raw(a, g, _NN)
    return da.astype(a.dtype), db.astype(b.dtype)


_dot_dims.defvjp(_dot_dims_fwd, _dot_dims_bwd)


def _dot(a, b, dims=_NN):
    return _dot_dims(a, b, dims)


def _dot_nt(a, b):
    return _dot_dims(a, b, _NT)


def _dot_tn(a, b):
    return _dot_dims(a, b, _TN)


def _split(x, terms):
    parts, rest = [], x.astype(F32)
    for _ in range(terms):
        p = rest.astype(BF16)
        parts.append(p)
        rest = rest - p.astype(F32)
    return parts


def _dot_split_a(a, b, terms=2):
    out = None
    for p in _split(a, terms):
        t = _dot(p, b)
        out = t if out is None else out + t
    return out


def _dot_split_b(a, b, terms=3):
    out = None
    for p in _split(b, terms):
        t = _dot(a, p)
        out = t if out is None else out + t
    return out


MM_VMEM_BUDGET = 30 * 1024 * 1024
MM_HBM_BPS = 3.2e12
MM_MXU_FPS = 8.5e14
MM_STEP_S = 0.35e-6


def _divisors(n, align, cap):
    out = [d for d in range(align, min(n, cap) + 1, align) if n % d == 0]
    return out or [n]


def _mm_tiles(m, n, k, ea, eb, eo, ta):
    best = None
    for tm in _divisors(m, LANE if ta else 8, 2048):
        for tn in _divisors(n, LANE, 2048):
            for tk in _divisors(k, LANE, 2048):
                nk = k // tk
                vmem = 2 * (tm * tk * ea + tk * tn * eb + tm * tn * eo) + (tm * tn * 4 if nk > 1 else 0)
                if vmem > MM_VMEM_BUDGET:
                    continue
                dma = (tm * tk * ea if (nk > 1 or n // tn == 1) else tm * tk * ea * tn / n) + tk * tn * eb + tm * tn * eo / nk
                step = max(2.0 * tm * tn * tk / MM_MXU_FPS, dma / MM_HBM_BPS) + MM_STEP_S
                cost = (m // tm) * (n // tn) * nk * step
                if best is None or cost < best[0]:
                    best = (cost, tm, tn, tk)
    return best[1:]


def _mm(a, b, *, name, ta=False, tb=False, out_dtype=F32, tm=None, tn=None, tk=None, split_a=1, after=None):
    m, k = (a.shape[1], a.shape[0]) if ta else a.shape
    n, kb = (b.shape[0], b.shape[1]) if tb else (b.shape[1], b.shape[0])
    assert k == kb, (a.shape, b.shape, ta, tb)
    if tm is None:
        tm, tn, tk = _mm_tiles(m, n, k, a.dtype.itemsize, b.dtype.itemsize, jnp.dtype(out_dtype).itemsize, ta)
    assert m % tm == 0 and n % tn == 0 and k % tk == 0, (m, n, k, tm, tn, tk)
    nk = k // tk
    dims = ((0 if ta else 1,), (1 if tb else 0,))

    n_after = 0 if after is None else 1

    def body(a_ref, b_ref, *rest):
        o_ref, scratch = rest[n_after], rest[n_after + 1:]
        prod = None
        for p in _split(a_ref[...], split_a) if split_a > 1 else [a_ref[...]]:
            t = _dot_raw(p, b_ref[...], dims)
            prod = t if prod is None else prod + t
        if nk == 1:
            o_ref[...] = prod.astype(o_ref.dtype)
            return
        acc_ref, kk = scratch[0], pl.program_id(2)

        @pl.when(kk == 0)
        def _():
            acc_ref[...] = prod

        @pl.when(kk > 0)
        def _():
            acc_ref[...] += prod

        @pl.when(kk == nk - 1)
        def _():
            o_ref[...] = acc_ref[...].astype(o_ref.dtype)

    a_spec = pl.BlockSpec((tk, tm), lambda i, j, q: (q, i)) if ta else pl.BlockSpec((tm, tk), lambda i, j, q: (i, q))
    b_spec = pl.BlockSpec((tn, tk), lambda i, j, q: (j, q)) if tb else pl.BlockSpec((tk, tn), lambda i, j, q: (q, j))
    return pl.pallas_call(
        body, name=name, grid=(m // tm, n // tn, nk),
        in_specs=[a_spec, b_spec] + [pl.BlockSpec(memory_space=pl.ANY)] * n_after,
        out_specs=pl.BlockSpec((tm, tn), lambda i, j, q: (i, j)),
        out_shape=jax.ShapeDtypeStruct((m, n), out_dtype),
        scratch_shapes=[pltpu.VMEM((tm, tn), F32)] if nk > 1 else [],
        compiler_params=_cp("parallel", "parallel", "arbitrary"),
    )(a, b, *([] if after is None else [after]))


def _piece_steps(pieces, tile):
    counts = [p.shape[1] // tile for p in pieces]
    assert all(p.shape[1] % tile == 0 for p in pieces)
    return [(sum(counts[:i]), c) for i, c in enumerate(counts)], sum(counts)


def _mm_cat_nn(pieces, w, *, name, after=None, tm=2048, tk=256):
    t, n = pieces[0].shape[0], w.shape[1]
    tm = min(tm, t)
    spans, nk = _piece_steps(pieces, tk)
    npc = len(pieces)
    n_after = 0 if after is None else 1

    def body(*refs):
        w_ref, o_ref, acc_ref = refs[npc], refs[npc + 1 + n_after], refs[npc + 2 + n_after]
        q = pl.program_id(1)

        @pl.when(q == 0)
        def _():
            acc_ref[...] = jnp.zeros_like(acc_ref)

        for p_ref, (first, count) in zip(refs[:npc], spans):
            @pl.when(jnp.logical_and(q >= first, q < first + count))
            def _(p_ref=p_ref):
                acc_ref[...] += _dot_raw(p_ref[...], w_ref[...], _NN)

        @pl.when(q == nk - 1)
        def _():
            o_ref[...] = acc_ref[...].astype(o_ref.dtype)

    def piece_spec(first, count):
        return pl.BlockSpec((tm, tk), lambda i, q: (i, jnp.clip(q - first, 0, count - 1)))

    return pl.pallas_call(
        body, name=name, grid=(t // tm, nk),
        in_specs=[piece_spec(*s) for s in spans] + [pl.BlockSpec((tk, n), lambda i, q: (q, 0))]
        + [pl.BlockSpec(memory_space=pl.ANY)] * n_after,
        out_specs=pl.BlockSpec((tm, n), lambda i, q: (i, 0)),
        out_shape=jax.ShapeDtypeStruct((t, n), BF16),
        scratch_shapes=[pltpu.VMEM((tm, n), F32)],
        compiler_params=_cp("parallel", "arbitrary"),
    )(*pieces, w, *([] if after is None else [after]))


def _mm_cat_tn(pieces, a, *, name, after=None, tk=1024, tn=512):
    t, m = a.shape
    tk = min(tk, t)
    spans, nj = _piece_steps(pieces, tn)
    npc, nk = len(pieces), t // tk
    n_after = 0 if after is None else 1

    def body(a_ref, *refs):
        o_ref, acc_ref = refs[npc + n_after], refs[npc + 1 + n_after]
        j, q = pl.program_id(0), pl.program_id(1)

        @pl.when(q == 0)
        def _():
            acc_ref[...] = jnp.zeros_like(acc_ref)

        for p_ref, (first, count) in zip(refs[:npc], spans):
            @pl.when(jnp.logical_and(j >= first, j < first + count))
            def _(p_ref=p_ref):
                acc_ref[...] += _dot_raw(p_ref[...], a_ref[...], _TN)

        @pl.when(q == nk - 1)
        def _():
            o_ref[...] = acc_ref[...].astype(o_ref.dtype)

    def piece_spec(first, count):
        def index(j, q):
            mine = jnp.logical_and(j >= first, j < first + count)
            return jnp.where(mine, q, 0), jnp.clip(j - first, 0, count - 1)
        return pl.BlockSpec((tk, tn), index)

    return pl.pallas_call(
        body, name=name, grid=(nj, nk),
        in_specs=[pl.BlockSpec((tk, m), lambda j, q: (q, 0))] + [piece_spec(*s) for s in spans]
        + [pl.BlockSpec(memory_space=pl.ANY)] * n_after,
        out_specs=pl.BlockSpec((tn, m), lambda j, q: (j, 0)),
        out_shape=jax.ShapeDtypeStruct((nj * tn, m), BF16),
        scratch_shapes=[pltpu.VMEM((tn, m), F32)],
        compiler_params=_cp("parallel", "arbitrary"),
    )(a, *pieces, *([] if after is None else [after]))


def _win(arr, start=0, width=None):
    width = arr.shape[1] if width is None else width
    assert start % width == 0
    return (arr, start // width, width)


def _row_specs(rows, tm):
    return [pl.BlockSpec((tm, w), functools.partial(lambda i, cb: (i, cb), cb=cb)) for (_, cb, w) in rows]


def _full_spec(p):
    nd = p.ndim
    return pl.BlockSpec(p.shape, lambda i, nd=nd: (0,) * nd)


def _rowwise(fn, rows, params, outs, *, name, tm, after=None):
    t = rows[0][0].shape[0]
    tm = min(tm, t)
    assert t % tm == 0
    nr, npar = len(rows), len(params)
    n_after = 0 if after is None else 1

    def body(*refs):
        vals = [r[...] for r in refs[:nr + npar]]
        res = fn(*vals)
        for o_ref, r in zip(refs[nr + npar + n_after:], res):
            o_ref[...] = r.astype(o_ref.dtype)

    return pl.pallas_call(
        body, name=name, grid=(t // tm,),
        in_specs=_row_specs(rows, tm) + [_full_spec(p) for p in params] + [pl.BlockSpec(memory_space=pl.ANY)] * n_after,
        out_specs=[pl.BlockSpec((tm, w), lambda i: (i, 0)) for (w, _) in outs],
        out_shape=[jax.ShapeDtypeStruct((t, w), dt) for (w, dt) in outs],
        compiler_params=_cp("parallel"),
    )(*[r[0] for r in rows], *params, *([] if after is None else [after]))


def _rowwise_bwd(fn, rows, params, n_const, cots, *, name, tm, row_grad, add_to=None, packed=False):
    t = rows[0][0].shape[0]
    tm = min(tm, t)
    assert t % tm == 0
    nr, npar = len(rows), len(params)
    ndp = npar - n_const
    add_to = add_to or {}
    add_idx = sorted(add_to)
    flat_cots = [c for group in cots for c in group]
    kept = [i for i in range(nr) if row_grad[i] is not None]

    def body(*refs):
        pos = 0
        row_v = [r[...] for r in refs[pos:pos + nr]]; pos += nr
        par_v = [r[...] for r in refs[pos:pos + npar]]; pos += npar
        cot_v = [r[...] for r in refs[pos:pos + len(flat_cots)]]; pos += len(flat_cots)
        add_v = [r[...] for r in refs[pos:pos + len(add_idx)]]; pos += len(add_idx)
        if packed:
            offs = [sum(rows[i][2] for i in kept[:q]) for q in range(len(kept))]
            rg_refs = [refs[pos].at[:, o:o + rows[i][2]] for o, i in zip(offs, kept)]; pos += 1
        else:
            rg_refs = refs[pos:pos + len(kept)]; pos += len(kept)
        pg_refs = refs[pos:pos + ndp]

        consts = par_v[ndp:]
        res, vjp = jax.vjp(lambda *args: tuple(fn(*args, *consts)), *row_v, *par_v[:ndp])
        cot_in, q = [], 0
        for j, group in enumerate(cots):
            c = None
            for _ in group:
                cv = cot_v[q].astype(F32); q += 1
                c = cv if c is None else c + cv
            c = jnp.zeros(res[j].shape, F32) if c is None else c
            cot_in.append(c.astype(res[j].dtype))
        grads = vjp(tuple(cot_in))
        for ref, i in zip(rg_refs, kept):
            g = grads[i].astype(F32)
            if i in add_to:
                g = g + add_v[add_idx.index(i)].astype(F32)
            ref[...] = g.astype(ref.dtype)

        @pl.when(pl.program_id(0) == 0)
        def _():
            for ref in pg_refs:
                ref[...] = jnp.zeros_like(ref)

        for ref, g in zip(pg_refs, grads[nr:]):
            ref[...] += g.astype(F32)

    cot_specs = [pl.BlockSpec((tm, c.shape[1]), lambda i: (i, 0)) for c in flat_cots]
    add_specs = [pl.BlockSpec((tm, add_to[i].shape[1]), lambda i_: (i_, 0)) for i in add_idx]
    widths = [sum(rows[i][2] for i in kept)] if packed else [rows[i][2] for i in kept]
    n_rg = len(widths)
    out_specs = [pl.BlockSpec((tm, w), lambda i_: (i_, 0)) for w in widths] + [_full_spec(p) for p in params[:ndp]]
    out_shape = [jax.ShapeDtypeStruct((t, w), row_grad[kept[q]]) for q, w in enumerate(widths)] + [
        jax.ShapeDtypeStruct(p.shape, F32) for p in params[:ndp]]
    res = pl.pallas_call(
        body, name=name, grid=(t // tm,),
        in_specs=_row_specs(rows, tm) + [_full_spec(p) for p in params] + cot_specs + add_specs,
        out_specs=out_specs, out_shape=out_shape,
        compiler_params=_cp("arbitrary"),
    )(*[r[0] for r in rows], *params, *flat_cots, *[add_to[i] for i in add_idx])
    return list(res[:n_rg]), list(res[n_rg:])


def _rms(x, g):
    xf = x.astype(F32)
    return xf * lax.rsqrt(jnp.mean(xf * xf, axis=-1, keepdims=True) + NORM_EPS) * g


def _softplus(x):
    return jnp.maximum(x, 0.0) + jnp.log(1.0 + jnp.exp(-jnp.abs(x)))


def _fn_pre(x, g):
    return (_rms(x, g).astype(BF16),)


def _fn_res(x, u, g_post):
    return (x + _rms(u, g_post),)


def _fn_res_pre(x, u, g_post, g_pre):
    xn = x + _rms(u, g_post)
    return xn, _rms(xn, g_pre).astype(BF16)


def _fn_mix(zga, zgb, ya, yb):
    return ((jax.nn.sigmoid(zga) * ya + jax.nn.sigmoid(zgb) * yb).astype(BF16),)


def _fn_swiglu(gate, up):
    gate, up = gate.astype(F32), up.astype(F32)
    return ((gate * jax.nn.sigmoid(gate) * up).astype(BF16),)


def _fn_prep(zk, zw, za, zg, decay_base, d_up, iclr_base, i_up, g_up, kns, kis, e_hd, e_dh):
    w_log = -_softplus(-(decay_base + _dot(jnp.tanh(zw), d_up))) - 0.5
    lw = -jnp.exp(w_log)
    a = jax.nn.sigmoid(iclr_base + _dot(za, i_up))
    g = _dot(jax.nn.sigmoid(zg), g_up)
    kn = zk * kns
    ss = _dot(kn * kn, e_dh)
    inv = lax.rsqrt(jnp.maximum(ss, 1e-24))
    kk = kn * _dot_split_a(inv, e_hd)
    k2 = zk * (1.0 + (a - 1.0) * kis)
    return lw, k2, kk, a, g


def _fn_post(y, r, k2, v, g, lnx_w, lnx_b, bonus, e_hd, e_dh):
    mu = _dot_split_a(_dot(y, e_dh) * (1.0 / HEAD), e_hd)
    yc = y - mu
    var = _dot(yc * yc, e_dh) * (1.0 / HEAD)
    yn = yc * _dot_split_a(lax.rsqrt(var + GROUP_NORM_EPS), e_hd)
    bs = _dot_split_a(_dot(r * k2 * bonus, e_dh), e_hd)
    return (((yn * lnx_w + lnx_b + bs * v) * g).astype(BF16),)


def _shift_fwd(p, col0, ncols, mix, seq, *, name, cw=256):
    t = p.shape[0]
    assert col0 % cw == 0 and ncols % cw == 0 and t % seq == 0
    cb0 = col0 // cw

    def body(p_ref, m_ref, z_ref):
        pv = p_ref[...]
        row = lax.broadcasted_iota(jnp.int32, pv.shape, 0)
        prev = jnp.where(row == 0, 0.0, pltpu.roll(pv, 1, axis=0))
        z_ref[...] = pv + (prev - pv) * m_ref[...]

    return pl.pallas_call(
        body, name=name, grid=(t // seq, ncols // cw),
        in_specs=[pl.BlockSpec((seq, cw), lambda b, c: (b, c + cb0)), pl.BlockSpec((1, cw), lambda b, c: (0, c))],
        out_specs=pl.BlockSpec((seq, cw), lambda b, c: (b, c)),
        out_shape=jax.ShapeDtypeStruct((t, ncols), F32),
        compiler_params=_cp("parallel", "parallel"),
    )(p, mix)


def _shift_bwd(p, col0, ncols, mix, dz_parts, seq, *, name, cw=256):
    t = p.shape[0]
    cb0 = col0 // cw
    n = len(dz_parts)

    def body(*refs):
        p_ref, m_ref = refs[:2]
        dp_ref, dm_ref = refs[2 + n:]
        dz = refs[2][...].astype(F32)
        for r in refs[3:2 + n]:
            dz = dz + r[...].astype(F32)
        pv = p_ref[...]
        mixv = m_ref[...]
        row = lax.broadcasted_iota(jnp.int32, pv.shape, 0)
        prev = jnp.where(row == 0, 0.0, pltpu.roll(pv, 1, axis=0))
        u = dz * mixv
        nxt = jnp.where(row == seq - 1, 0.0, pltpu.roll(u, seq - 1, axis=0))
        dp_ref[...] = (dz - u + nxt).astype(dp_ref.dtype)

        @pl.when(pl.program_id(1) == 0)
        def _():
            dm_ref[...] = jnp.zeros_like(dm_ref)

        dm_ref[...] += jnp.sum(dz * (prev - pv), axis=0, keepdims=True)

    return pl.pallas_call(
        body, name=name, grid=(ncols // cw, t // seq),
        in_specs=[pl.BlockSpec((seq, cw), lambda c, b: (b, c + cb0)), pl.BlockSpec((1, cw), lambda c, b: (0, c))]
        + [pl.BlockSpec((seq, cw), lambda c, b: (b, c))] * n,
        out_specs=[pl.BlockSpec((seq, cw), lambda c, b: (b, c)), pl.BlockSpec((1, cw), lambda c, b: (0, c))],
        out_shape=[jax.ShapeDtypeStruct((t, ncols), BF16), jax.ShapeDtypeStruct((1, ncols), F32)],
        compiler_params=_cp("parallel", "arbitrary"),
    )(p, mix, *dz_parts)


def _each(f, *lists):
    return [f(*xs) for xs in zip(*lists)]


def _tri_inv(low):
    c = low[0].shape[0]
    ti = lax.broadcasted_iota(jnp.int32, (c, c), 0)
    si = lax.broadcasted_iota(jnp.int32, (c, c), 1)
    eye = (ti == si).astype(F32)
    inside = (ti // 4) == (si // 4)
    base = [jnp.where(inside, m, 0.0) for m in low]
    acc = _each(lambda m: _dot(eye - m, eye + _dot(m, m)), base)
    size = 8
    while size <= c:
        wider = (ti // size) == (si // size)
        keep = jnp.logical_and(wider, jnp.logical_not(inside))
        acc = _each(lambda p, m: p - _dot(_dot(p, jnp.where(keep, m, 0.0)), p), acc, low)
        inside, size = wider, size * 2
    return acc


def _stack_rows(a, b):
    return jnp.concatenate([a, b], axis=0)


@jax.custom_vjp
def _split_rows(x):
    h = x.shape[0] // 2
    return x[:h], x[h:]


def _split_rows_fwd(x):
    return _split_rows(x), None


def _split_rows_bwd(_, g):
    return (jnp.concatenate(g, axis=0),)


_split_rows.defvjp(_split_rows_fwd, _split_rows_bwd)


def _masked_halves(stacked, top_mask, bottom_mask):
    halves = _each(_split_rows, stacked)
    return ([jnp.where(top_mask, t, 0.0) for t, _ in halves], [jnp.where(bottom_mask, b, 0.0) for _, b in halves])


@jax.custom_vjp
def _tri_inv_known(low, inv):
    return inv


def _tri_inv_known_fwd(low, inv):
    return inv, inv


def _tri_inv_known_bwd(inv, g):
    dlow = _each(lambda t, gg: -_dot(_dot(t, gg, _TN), t, _NT), inv, g)
    return dlow, _each(jnp.zeros_like, inv)


_tri_inv_known.defvjp(_tri_inv_known_fwd, _tri_inv_known_bwd)


def _tri_ones(c):
    return (lax.broadcasted_iota(jnp.int32, (c, c), 0) >= lax.broadcasted_iota(jnp.int32, (c, c), 1)).astype(F32)


def _cumsum_rows(lw):
    return _dot_split_b(_tri_ones(lw.shape[0]), lw, 3)


def _wkv_chunk(s0, r, lw, cum, k, v, kk, a, inv=None):
    c = r[0].shape[0]
    ti = lax.broadcasted_iota(jnp.int32, (c, c), 0)
    si = lax.broadcasted_iota(jnp.int32, (c, c), 1)
    incl, strict = ti >= si, ti > si
    eg = _each(jnp.exp, cum)
    egp = _each(lambda cs, x: jnp.exp(cs - x), cum, lw)
    ei = _each(lambda cs: jnp.exp(-cs), cum)
    rh, kkh, kt = _each(jnp.multiply, r, eg), _each(jnp.multiply, kk, egp), _each(jnp.multiply, k, ei)
    bt = _each(lambda p, q, e: (p * q) * e, a, kk, ei)
    both = _each(_stack_rows, kkh, rh)
    on_b, on_k, on_s = _each(_dot_nt, both, bt), _each(_dot_nt, both, kt), _each(_dot_nt, both, s0)
    lb, mb = _masked_halves(on_b, strict, incl)
    lk, mk = _masked_halves(on_k, strict, incl)
    on_s = _each(_split_rows, on_s)
    on_v = _each(lambda p, q, x: _split_rows(_dot(_stack_rows(p, q), x)), lk, mk, v)
    rhs = _each(lambda p, q: p[0] + q[0], on_s, on_v)
    inv = _tri_inv(lb) if inv is None else _tri_inv_known(lb, inv)
    u = _each(lambda t, x: -_dot(t, x), inv, rhs)
    y = _each(lambda p, m1, uu, q: p[1] + _dot(m1, uu) + q[1], on_s, mb, u, on_v)
    s1 = _each(lambda s, uu, x, b, kq, w: (s + _dot_tn(_stack_rows(uu, x), _stack_rows(b, kq)))
               * jnp.exp(jnp.sum(w, axis=0, keepdims=True)), s0, u, v, bt, kt, lw)
    return y, s1, inv


WKV_HEADS = 16
WKV_COLS = WKV_HEADS * HEAD
WKV_GROUPS = N_HEADS // WKV_HEADS


def _head_cols(ref):
    return [ref[:, h * HEAD:(h + 1) * HEAD] for h in range(ref.shape[1] // HEAD)]


def _wkv_specs(seq, rev):
    nc = seq // CHUNK

    def rows(col0):
        cb0 = col0 // WKV_COLS
        if rev:
            return pl.BlockSpec((CHUNK, WKV_COLS), lambda b, h, c: (b * nc + nc - 1 - c, cb0 + h))
        return pl.BlockSpec((CHUNK, WKV_COLS), lambda b, h, c: (b * nc + c, cb0 + h))

    if rev:
        st = pl.BlockSpec((1, 1, WKV_HEADS, HEAD, HEAD), lambda b, h, c: (b * WKV_GROUPS + h, nc - 1 - c, 0, 0, 0))
    else:
        st = pl.BlockSpec((1, 1, WKV_HEADS, HEAD, HEAD), lambda b, h, c: (b * WKV_GROUPS + h, c, 0, 0, 0))
    return rows, st


def _wkv_fwd(z_rkv, lw, k2, kk, a, seq):
    t = z_rkv.shape[0]
    nb, nc = t // seq, seq // CHUNK
    rows, st = _wkv_specs(seq, False)

    def body(r_ref, v_ref, lw_ref, k_ref, kk_ref, a_ref, y_ref, st_ref, inv_ref, s_scr, cum_scr):
        @pl.when(pl.program_id(2) == 0)
        def _():
            s_scr[...] = jnp.zeros_like(s_scr)

        cum_scr[...] = _cumsum_rows(lw_ref[...])
        s0 = [s_scr[h] for h in range(WKV_HEADS)]
        y, s1, inv = _wkv_chunk(s0, *[_head_cols(ref) for ref in (r_ref, lw_ref, cum_scr, k_ref, v_ref, kk_ref, a_ref)])
        for h in range(WKV_HEADS):
            st_ref[0, 0, h] = s0[h]
            inv_ref[0, 0, h] = inv[h]
            y_ref[:, h * HEAD:(h + 1) * HEAD] = y[h]
            s_scr[h] = s1[h]

    per_chunk = jax.ShapeDtypeStruct((nb * WKV_GROUPS, nc, WKV_HEADS, HEAD, HEAD), F32)
    return pl.pallas_call(
        body, name="wkv_fwd", grid=(nb, WKV_GROUPS, nc),
        in_specs=[rows(0), rows(2 * D), rows(0), rows(0), rows(0), rows(0)],
        out_specs=[rows(0), st, st],
        out_shape=[jax.ShapeDtypeStruct((t, D), F32), per_chunk, per_chunk],
        scratch_shapes=[pltpu.VMEM((WKV_HEADS, HEAD, HEAD), F32), pltpu.VMEM((CHUNK, WKV_COLS), F32)],
        compiler_params=_cp("parallel", "parallel", "arbitrary"),
    )(z_rkv, z_rkv, lw, k2, kk, a)


def _wkv_bwd(z_rkv, lw, k2, kk, a, states, invs, dy, seq):
    t = z_rkv.shape[0]
    nb, nc = t // seq, seq // CHUNK
    rows, st = _wkv_specs(seq, True)

    def body(r_ref, v_ref, lw_ref, k_ref, kk_ref, a_ref, st_ref, inv_ref, dy_ref,
             dr_ref, dlw_ref, dk_ref, dv_ref, dkk_ref, da_ref, ds_scr, cum_scr, dlw_scr):
        @pl.when(pl.program_id(2) == 0)
        def _():
            ds_scr[...] = jnp.zeros_like(ds_scr)

        cum_scr[...] = _cumsum_rows(lw_ref[...])
        s0 = [st_ref[0, 0, h] for h in range(WKV_HEADS)]
        inv = [inv_ref[0, 0, h] for h in range(WKV_HEADS)]
        _, vjp = jax.vjp(lambda *args: _wkv_chunk(*args, inv=inv)[:2],
                         s0, *[_head_cols(ref) for ref in (r_ref, lw_ref, cum_scr, k_ref, v_ref, kk_ref, a_ref)])
        ds0, dr, dlw, dcum, dk, dv, dkk, da = vjp(
            ([x.astype(F32) for x in _head_cols(dy_ref)], [ds_scr[h] for h in range(WKV_HEADS)]))
        for h in range(WKV_HEADS):
            sl = slice(h * HEAD, (h + 1) * HEAD)
            ds_scr[h] = ds0[h]
            dlw_scr[:, sl] = dlw[h]
            cum_scr[:, sl] = dcum[h]
            for ref, g in zip((dr_ref, dk_ref, dv_ref, dkk_ref, da_ref), (dr, dk, dv, dkk, da)):
                ref[:, sl] = g[h].astype(ref.dtype)
        dlw_ref[...] = (dlw_scr[...] + _dot_raw(_tri_ones(CHUNK), cum_scr[...], _TN)).astype(dlw_ref.dtype)

    return pl.pallas_call(
        body, name="wkv_bwd", grid=(nb, WKV_GROUPS, nc),
        in_specs=[rows(0), rows(2 * D), rows(0), rows(0), rows(0), rows(0), st, st, rows(0)],
        out_specs=[rows(0)] * 6,
        out_shape=[jax.ShapeDtypeStruct((t, D), BF16)] * 6,
        scratch_shapes=[pltpu.VMEM((WKV_HEADS, HEAD, HEAD), F32)] + [pltpu.VMEM((CHUNK, WKV_COLS), F32)] * 2,
        compiler_params=_cp("parallel", "parallel", "arbitrary"),
    )(z_rkv, z_rkv, lw, k2, kk, a, states, invs, dy)


def _softmax(s):
    e = jnp.exp(s - jnp.max(s, axis=-1, keepdims=True))
    return e * (1.0 / jnp.sum(e, axis=-1, keepdims=True))


ATT_FWD_HEADS = 16
ATT_HEADS = 8
ATT_COLS = ATT_HEADS * HEAD
ATT_GROUPS = N_HEADS // ATT_HEADS


def _attn_chunk(q, kb, vb, bias, valid):
    s = _each(lambda x, y, z: jnp.where(valid, _dot_nt(x * (HEAD ** -0.5), y) + z, MASK_VALUE), q, kb, bias)
    return _each(_dot, _each(_softmax, s), vb)


def _pad_fill(pad_ref, src_ref):
    pad_ref[0:LEFT, :] = jnp.zeros((LEFT, pad_ref.shape[1]), pad_ref.dtype)
    pad_ref[LEFT:, :] = src_ref[...].astype(pad_ref.dtype)


def _band_heads(pad_ref, start):
    return [pad_ref[pl.ds(start, BAND), h * HEAD:(h + 1) * HEAD] for h in range(pad_ref.shape[1] // HEAD)]


def _band_valid(c):
    return (c * CHUNK - LEFT + lax.broadcasted_iota(jnp.int32, (1, BAND), 1)) >= 0


def _bias_spec():
    return pl.BlockSpec((ATT_HEADS, CHUNK, BAND), lambda h, b, c: (h, 0, 0))


def _attn_fwd(proj, bias, seq):
    t = proj.shape[0]
    nb, nc = t // seq, seq // CHUNK
    heads = ATT_FWD_HEADS
    cols, groups = heads * HEAD, N_HEADS // heads
    cq = C_Q // cols

    def body(q_ref, k_ref, v_ref, b_ref, o_ref, kpad, vpad):
        c = pl.program_id(2)

        @pl.when(c == 0)
        def _():
            _pad_fill(kpad, k_ref)
            _pad_fill(vpad, v_ref)

        start = pl.multiple_of(c * CHUNK, CHUNK)
        o = _attn_chunk(_head_cols(q_ref), _band_heads(kpad, start), _band_heads(vpad, start),
                        [b_ref[h] for h in range(heads)], _band_valid(c))
        for h in range(heads):
            o_ref[:, h * HEAD:(h + 1) * HEAD] = o[h].astype(o_ref.dtype)

    return pl.pallas_call(
        body, name="attn_fwd", grid=(groups, nb, nc),
        in_specs=[pl.BlockSpec((CHUNK, cols), lambda h, b, c: (b * nc + c, cq + h)),
                  pl.BlockSpec((seq, cols), lambda h, b, c: (b, cq + groups + h)),
                  pl.BlockSpec((seq, cols), lambda h, b, c: (b, cq + 2 * groups + h)),
                  pl.BlockSpec((heads, CHUNK, BAND), lambda h, b, c: (h, 0, 0))],
        out_specs=pl.BlockSpec((CHUNK, cols), lambda h, b, c: (b * nc + c, h)),
        out_shape=jax.ShapeDtypeStruct((t, D), BF16),
        scratch_shapes=[pltpu.VMEM((seq + LEFT, cols), BF16)] * 2,
        compiler_params=_cp("parallel", "arbitrary", "arbitrary"),
    )(proj, proj, proj, bias)


def _attn_bwd(proj, bias, do, seq):
    t = proj.shape[0]
    nb, nc = t // seq, seq // CHUNK
    cq = C_Q // ATT_COLS

    def body(q_ref, k_ref, v_ref, b_ref, do_ref, dq_ref, dk_ref, dv_ref, db_ref, kpad, vpad, dkpad, dvpad):
        b, c = pl.program_id(1), pl.program_id(2)

        @pl.when(c == 0)
        def _():
            _pad_fill(kpad, k_ref)
            _pad_fill(vpad, v_ref)
            dkpad[...] = jnp.zeros_like(dkpad)
            dvpad[...] = jnp.zeros_like(dvpad)

        @pl.when(jnp.logical_and(b == 0, c == 0))
        def _():
            db_ref[...] = jnp.zeros_like(db_ref)

        start = pl.multiple_of(c * CHUNK, CHUNK)
        _, vjp = jax.vjp(functools.partial(_attn_chunk, valid=_band_valid(c)),
                         _head_cols(q_ref), _band_heads(kpad, start), _band_heads(vpad, start),
                         [b_ref[h] for h in range(ATT_HEADS)])
        dq, dkb, dvb, dbias = vjp([x.astype(F32) for x in _head_cols(do_ref)])
        for h in range(ATT_HEADS):
            sl = slice(h * HEAD, (h + 1) * HEAD)
            dq_ref[:, sl] = dq[h].astype(dq_ref.dtype)
            dkpad[pl.ds(start, BAND), sl] += dkb[h].astype(F32)
            dvpad[pl.ds(start, BAND), sl] += dvb[h].astype(F32)
            db_ref[h] += dbias[h]

        @pl.when(c == nc - 1)
        def _():
            dk_ref[...] = dkpad[LEFT:, :].astype(dk_ref.dtype)
            dv_ref[...] = dvpad[LEFT:, :].astype(dv_ref.dtype)

    kv_out = pl.BlockSpec((seq, ATT_COLS), lambda h, b, c: (b, h))
    return pl.pallas_call(
        body, name="attn_bwd", grid=(ATT_GROUPS, nb, nc),
        in_specs=[pl.BlockSpec((CHUNK, ATT_COLS), lambda h, b, c: (b * nc + c, cq + h)),
                  pl.BlockSpec((seq, ATT_COLS), lambda h, b, c: (b, cq + ATT_GROUPS + h)),
                  pl.BlockSpec((seq, ATT_COLS), lambda h, b, c: (b, cq + 2 * ATT_GROUPS + h)),
                  _bias_spec(),
                  pl.BlockSpec((CHUNK, ATT_COLS), lambda h, b, c: (b * nc + c, h))],
        out_specs=[pl.BlockSpec((CHUNK, ATT_COLS), lambda h, b, c: (b * nc + c, h)), kv_out, kv_out,
                   pl.BlockSpec((ATT_HEADS, CHUNK, BAND), lambda h, b, c: (h, 0, 0))],
        out_shape=[jax.ShapeDtypeStruct((t, D), BF16)] * 3 + [jax.ShapeDtypeStruct((N_HEADS, CHUNK, BAND), F32)],
        scratch_shapes=[pltpu.VMEM((seq + LEFT, ATT_COLS), BF16)] * 2 + [pltpu.VMEM((seq + LEFT, ATT_COLS), F32)] * 2,
        compiler_params=_cp("parallel", "arbitrary", "arbitrary"),
    )(proj, proj, proj, bias, do)


def _xattn_tile(q, k, v):
    s = _dot_nt(q, k) * ((MEM_WIDTH // MEM_HEADS) ** -0.5)
    return _dot(_softmax(s), v)


def _xattn_fwd(qm, kvm, seq, n_mem, tq=1024):
    t = qm.shape[0]
    tq = min(tq, seq)
    nb, nq = t // seq, seq // tq

    def body(q_ref, k_ref, v_ref, o_ref):
        o_ref[...] = _xattn_tile(q_ref[...], k_ref[...], v_ref[...]).astype(o_ref.dtype)

    return pl.pallas_call(
        body, name="xattn_fwd", grid=(nb, MEM_HEADS, nq),
        in_specs=[pl.BlockSpec((tq, LANE), lambda b, h, i: (b * nq + i, h)),
                  pl.BlockSpec((n_mem, LANE), lambda b, h, i: (b, h)),
                  pl.BlockSpec((n_mem, LANE), lambda b, h, i: (b, MEM_HEADS + h))],
        out_specs=pl.BlockSpec((tq, LANE), lambda b, h, i: (b * nq + i, h)),
        out_shape=jax.ShapeDtypeStruct((t, MEM_WIDTH), BF16),
        compiler_params=_cp("parallel", "parallel", "parallel"),
    )(qm, kvm, kvm)


def _xattn_bwd(qm, kvm, do, seq, n_mem, tq=1024):
    t = qm.shape[0]
    tq = min(tq, seq)
    nb, nq = t // seq, seq // tq

    def body(q_ref, k_ref, v_ref, do_ref, dq_ref, dkv_ref, dk_acc, dv_acc):
        i = pl.program_id(2)

        @pl.when(i == 0)
        def _():
            dk_acc[...] = jnp.zeros_like(dk_acc)
            dv_acc[...] = jnp.zeros_like(dv_acc)

        _, vjp = jax.vjp(_xattn_tile, q_ref[...], k_ref[...], v_ref[...])
        dq, dk, dv = vjp(do_ref[...].astype(F32))
        dq_ref[...] = dq.astype(dq_ref.dtype)
        dk_acc[...] += dk
        dv_acc[...] += dv

        @pl.when(i == nq - 1)
        def _():
            dkv_ref[0] = dk_acc[...].astype(dkv_ref.dtype)
            dkv_ref[1] = dv_acc[...].astype(dkv_ref.dtype)

    dq, dkv = pl.pallas_call(
        body, name="xattn_bwd", grid=(nb, MEM_HEADS, nq),
        in_specs=[pl.BlockSpec((tq, LANE), lambda b, h, i: (b * nq + i, h)),
                  pl.BlockSpec((n_mem, LANE), lambda b, h, i: (b, h)),
                  pl.BlockSpec((n_mem, LANE), lambda b, h, i: (b, MEM_HEADS + h)),
                  pl.BlockSpec((tq, LANE), lambda b, h, i: (b * nq + i, h))],
        out_specs=[pl.BlockSpec((tq, LANE), lambda b, h, i: (b * nq + i, h)),
                   pl.BlockSpec((2, n_mem, LANE), lambda b, h, i: (0, b, h))],
        out_shape=[jax.ShapeDtypeStruct((t, MEM_WIDTH), BF16), jax.ShapeDtypeStruct((2, nb * n_mem, MEM_WIDTH), BF16)],
        scratch_shapes=[pltpu.VMEM((n_mem, LANE), F32)] * 2,
        compiler_params=_cp("parallel", "parallel", "arbitrary"),
    )(qm, kvm, kvm, do)
    return dq, jnp.concatenate([dkv[0], dkv[1]], axis=1)


def _loss_head(x, u, g_post, target, tm=512):
    t, d = x.shape
    tm = min(tm, t)

    def tile_loss(xv, uv, gv, tv):
        diff = _fn_res(xv, uv, gv)[0] - tv
        return 0.5 * jnp.sum(jnp.mean(diff * diff, axis=-1, keepdims=True), axis=0, keepdims=True)

    def body(x_ref, u_ref, g_ref, t_ref, l_ref, dx_ref, du_ref, dg_ref):
        @pl.when(pl.program_id(0) == 0)
        def _():
            l_ref[...] = jnp.zeros_like(l_ref)
            dg_ref[...] = jnp.zeros_like(dg_ref)

        tv = t_ref[...]
        part, vjp = jax.vjp(lambda xv, uv, gv: tile_loss(xv, uv, gv, tv), x_ref[...], u_ref[...], g_ref[...])
        dx, du, dg = vjp(jnp.ones((1, 1), F32))
        l_ref[...] += part
        dx_ref[...] = dx
        du_ref[...] = du.astype(du_ref.dtype)
        dg_ref[...] += dg

    rows = pl.BlockSpec((tm, d), lambda i: (i, 0))
    vec = pl.BlockSpec((1, d), lambda i: (0, 0))
    return pl.pallas_call(
        body, name="loss_head", grid=(t // tm,),
        in_specs=[rows, rows, vec, rows],
        out_specs=[pl.BlockSpec((8, LANE), lambda i: (0, 0)), rows, rows, vec],
        out_shape=[jax.ShapeDtypeStruct((8, LANE), F32), jax.ShapeDtypeStruct((t, d), F32),
                   jax.ShapeDtypeStruct((t, d), BF16), jax.ShapeDtypeStruct((1, d), F32)],
        compiler_params=_cp("arbitrary"),
    )(x, u, g_post, target)


def _mesh_pos():
    return lax.axis_index("x"), lax.axis_index("y"), lax.axis_index("c")


def _peer(pos, d):
    x, y, c = pos
    return ((1 - x) if d & 4 else x, (1 - y) if d & 2 else y, (1 - c) if d & 1 else c)


def _flat(pos):
    return 4 * pos[0] + 2 * pos[1] + pos[2]


def _exchange(arrays, scatter, *, name):
    n = len(arrays)
    shapes = [a.shape[1:] if scatter else a.shape for a in arrays]

    def body(*refs):
        ins, outs = refs[:n], refs[n:2 * n]
        send, recv, loc = refs[2 * n:]
        pos = _mesh_pos()
        me = _flat(pos)
        pending = []
        for i in range(n):
            own = pltpu.make_async_copy(ins[i].at[me] if scatter else ins[i], outs[i].at[me], loc.at[i])
            own.start()
            pending.append(own)
            for d in range(1, N_DEV):
                peer = _peer(pos, d)
                src = ins[i].at[_flat(peer)] if scatter else ins[i]
                out_cp = pltpu.make_async_remote_copy(
                    src_ref=src, dst_ref=outs[i].at[me], send_sem=send.at[i, d - 1], recv_sem=recv.at[i, d - 1],
                    device_id=peer, device_id_type=pl.DeviceIdType.MESH)
                out_cp.start()
                pending.append(out_cp)
        for i in range(n):
            own = pending[i * N_DEV]
            for d in range(1, N_DEV):
                peer = _peer(pos, d)
                src = ins[i].at[_flat(peer)] if scatter else ins[i]
                pending[i * N_DEV + d].wait_send()
                pltpu.make_async_remote_copy(
                    src_ref=src, dst_ref=outs[i].at[_flat(peer)], send_sem=send.at[i, d - 1], recv_sem=recv.at[i, d - 1],
                    device_id=peer, device_id_type=pl.DeviceIdType.MESH).wait_recv()
            own.wait()

    hbm = pl.BlockSpec(memory_space=pltpu.HBM)
    return pl.pallas_call(
        body, name=name,
        in_specs=[hbm] * n, out_specs=[hbm] * n,
        out_shape=[jax.ShapeDtypeStruct((N_DEV,) + tuple(s), a.dtype) for s, a in zip(shapes, arrays)],
        scratch_shapes=[pltpu.SemaphoreType.DMA((n, N_DEV - 1)), pltpu.SemaphoreType.DMA((n, N_DEV - 1)),
                        pltpu.SemaphoreType.DMA((n,))],
    )(*arrays)


_HBM = pl.BlockSpec(memory_space=pltpu.HBM)
_SEM = pl.BlockSpec(memory_space=pltpu.SEMAPHORE)
_DATAFLOW = pltpu.SideEffectType.DATAFLOW_SIDE_EFFECTING


_ALL_PEERS = tuple(range(1, N_DEV))
_SIBLING_AND_SAME_CORE = (1, 2, 4, 6)


def _remote_copies(ins, lands, send, recv, scatter, dists):
    pos = _mesh_pos()
    me = _flat(pos)
    out = []
    for i in range(len(ins)):
        for j, d in enumerate(dists):
            peer = _peer(pos, d)
            src = ins[i].at[_flat(peer)] if scatter else ins[i]
            pair = i * len(dists) + j
            sems = dict(send_sem=send.at[pair], recv_sem=recv.at[pair], device_id=peer,
                        device_id_type=pl.DeviceIdType.MESH)
            out.append((pltpu.make_async_remote_copy(src_ref=src, dst_ref=lands[i].at[me], **sems),
                        pltpu.make_async_remote_copy(src_ref=src, dst_ref=lands[i].at[_flat(peer)], **sems)))
    return out


def _exchange_start(arrays, scatter, after, *, name, dists=_ALL_PEERS):
    n = len(arrays)
    shapes = [a.shape[1:] if scatter else a.shape for a in arrays]
    lands = [pltpu.with_memory_space_constraint(lax.empty((N_DEV,) + tuple(s), a.dtype), pltpu.HBM)
             for s, a in zip(shapes, arrays)]
    srcs = [pltpu.with_memory_space_constraint(a, pltpu.HBM) for a in arrays]

    def body(*refs):
        ins, land_refs = refs[:n], refs[n:2 * n]
        send, recv, token = refs[2 * n + 1], refs[2 * n + 2], refs[-1]
        for going, _ in _remote_copies(ins, land_refs, send, recv, scatter, dists):
            going.start()
        token[...] = jnp.zeros_like(token)

    sems = pltpu.SemaphoreType.DMA((n * len(dists),))
    res = pl.pallas_call(
        body, name=name,
        out_shape=(sems, sems, *[pltpu.HBM(a.shape, a.dtype) for a in srcs + lands], jax.ShapeDtypeStruct((8, LANE), F32)),
        in_specs=[_HBM] * (2 * n) + [pl.BlockSpec(memory_space=pl.ANY)],
        out_specs=(_SEM, _SEM, *[_HBM] * (2 * n), pl.BlockSpec(memory_space=pltpu.VMEM)),
        input_output_aliases={i: 2 + i for i in range(2 * n)},
        compiler_params=pltpu.CompilerParams(has_side_effects=_DATAFLOW),
    )(*srcs, *lands, after)
    return (n, scatter, dists, res[0], res[1], list(res[2:2 + 2 * n])), res[-1]


def _exchange_wait(handle, after, own, *, name):
    n, scatter, dists, send, recv, thru = handle

    def body(*refs):
        ins, land_refs = refs[:n], refs[n:2 * n]
        for going, coming in _remote_copies(ins, land_refs, refs[2 * n], refs[2 * n + 1], scatter, dists):
            going.wait_send()
            coming.wait_recv()

    res = pl.pallas_call(
        body, name=name,
        out_shape=tuple(pltpu.HBM(a.shape, a.dtype) for a in thru),
        in_specs=[_HBM] * (2 * n) + [_SEM, _SEM] + [pl.BlockSpec(memory_space=pl.ANY)] * len(after),
        out_specs=tuple([_HBM] * (2 * n)),
        input_output_aliases={i: i for i in range(2 * n)},
        compiler_params=pltpu.CompilerParams(has_side_effects=_DATAFLOW),
    )(*thru, send, recv, *after)
    me = _flat(_mesh_pos())
    return [lax.dynamic_update_slice_in_dim(land, o[None].astype(land.dtype), me, 0) for land, o in zip(res[n:], own)]


_OTHER_CHIPS = (2, 4, 6)


def _relay_to_sibling(gathered, *, name):
    n, k = len(gathered), len(_OTHER_CHIPS)

    def body(*refs):
        ins, outs = refs[:n], refs[n:2 * n]
        send, recv = refs[2 * n:]
        pos = _mesh_pos()
        copies = []
        for i in range(n):
            for j, d in enumerate(_OTHER_CHIPS):
                cp = pltpu.make_async_remote_copy(
                    src_ref=ins[i].at[_flat(_peer(pos, d))], dst_ref=outs[i].at[j],
                    send_sem=send.at[i * k + j], recv_sem=recv.at[i * k + j],
                    device_id=_peer(pos, 1), device_id_type=pl.DeviceIdType.MESH)
                cp.start()
                copies.append(cp)
        for cp in copies:
            cp.wait()

    return pl.pallas_call(
        body, name=name, in_specs=[_HBM] * n, out_specs=[_HBM] * n,
        out_shape=[jax.ShapeDtypeStruct((k,) + g.shape[1:], g.dtype) for g in gathered],
        scratch_shapes=[pltpu.SemaphoreType.DMA((n * k,)), pltpu.SemaphoreType.DMA((n * k,))],
    )(*gathered)


def _adamw(parts, w, m, v, *, name, tr=128, after=None):
    r, c = w.shape
    align = 8 * 4 // parts.dtype.itemsize
    row_tiles = [d for d in range(align, min(tr, r) + 1, align) if r % d == 0]
    tr, tc = (max(row_tiles), c) if row_tiles else (r, LANE)
    assert c % tc == 0
    n_after = 0 if after is None else 1

    def body(p_ref, w_ref, m_ref, v_ref, *rest):
        g_ref, d_ref, nm_ref, nv_ref = rest[n_after:]
        g = p_ref[0].astype(F32)
        for j in range(1, N_DEV):
            g = g + p_ref[j].astype(F32)
        m2 = ADAM_B1 * m_ref[...] + (1.0 - ADAM_B1) * g
        v2 = ADAM_B2 * v_ref[...] + (1.0 - ADAM_B2) * (g * g)
        m_hat = m2 / (1.0 - ADAM_B1 ** ADAM_STEP)
        v_hat = v2 / (1.0 - ADAM_B2 ** ADAM_STEP)
        g_ref[...] = g
        d_ref[...] = -ADAM_LR * (m_hat / (jnp.sqrt(v_hat) + ADAM_EPS) + ADAM_WD * w_ref[...])
        nm_ref[...] = m2
        nv_ref[...] = v2

    spec = pl.BlockSpec((tr, tc), lambda i, j: (i, j))
    return pl.pallas_call(
        body, name=name, grid=(r // tr, c // tc),
        in_specs=[pl.BlockSpec((N_DEV, tr, tc), lambda i, j: (0, i, j)), spec, spec, spec]
        + [pl.BlockSpec(memory_space=pl.ANY)] * n_after,
        out_specs=[spec] * 4, out_shape=[jax.ShapeDtypeStruct((r, c), F32)] * 4,
        compiler_params=_cp("parallel", "parallel"),
    )(parts, w, m, v, *([] if after is None else [after]))


def _cols_to_full(g):
    return jnp.transpose(g, (1, 0, 2)).reshape(g.shape[1], N_DEV * g.shape[2])


def _full_to_cols(w):
    r, c = w.shape
    return jnp.transpose(w.reshape(r, N_DEV, c // N_DEV), (1, 0, 2))


def _cut(a, lo, hi, axis):
    return lax.slice_in_dim(a, lo, hi, axis=axis)


def _pad_to(a, size, axis):
    pads = [(0, 0)] * a.ndim
    pads[axis] = (0, size - a.shape[axis])
    return jnp.pad(a, pads)


def _pad_lora(w, axis=1):
    return jnp.concatenate([
        _pad_to(_cut(w, 0, LORA_W, axis), 128, axis), _pad_to(_cut(w, LORA_W, LORA_W + LORA_A, axis), 128, axis),
        _pad_to(_cut(w, LORA_W + LORA_A, w.shape[axis], axis), 256, axis)], axis=axis)


def _unpad_lora(wp, axis=1):
    return jnp.concatenate([_cut(wp, 0, LORA_W, axis), _cut(wp, 128, 128 + LORA_A, axis),
                            _cut(wp, 256, 256 + LORA_G, axis)], axis=axis)


def _permute_in(w, axis):
    rk = 3 * D
    lo = rk + LORA_W + LORA_A + LORA_G
    return jnp.concatenate([_cut(w, 0, rk, axis), _cut(w, lo, w.shape[axis], axis), _pad_lora(_cut(w, rk, lo, axis), axis)],
                           axis=axis)


def _unpermute_in(wp, axis):
    return jnp.concatenate([_cut(wp, 0, 3 * D, axis), _unpad_lora(_cut(wp, C_LORA, P_WIDTH, axis), axis),
                            _cut(wp, 3 * D, C_LORA, axis)], axis=axis)


def _rel_index():
    dist = jnp.arange(CHUNK)[:, None] - jnp.arange(BAND)[None, :] + LEFT
    return (jnp.minimum(dist, REL_CLIP) + (CHUNK - 1)).reshape(-1)


def _local_step(x, mem, target, wt, seq, n_mem, comm):
    t = x.shape[0]
    row = lambda a: a.reshape(1, -1).astype(F32)
    g_pre_mix, g_post_mix = row(wt["g_pre_mix"]), row(wt["g_post_mix"])
    g_pre_cross, g_post_cross, g_mem = row(wt["g_pre_cross"]), row(wt["g_post_cross"]), row(wt["g_mem"])
    g_pre_ffn, g_post_ffn = row(wt["g_pre_ffn"]), row(wt["g_post_ffn"])
    mix = row(wt["shift_mix"])
    mix_rkv, mix_lora = mix[:, :3 * D], _pad_lora(mix[:, 3 * D:])
    decay_base, iclr_base = row(wt["decay_base"]), row(wt["iclr_base"])
    kns, kis = row(wt["key_norm_scale"]), row(wt["key_iclr_scale"])
    lnx_w, lnx_b, bonus = row(wt["lnx_w"]), row(wt["lnx_b"]), row(wt["bonus_scale"])
    e_dh = (jnp.arange(D)[:, None] // HEAD == jnp.arange(N_HEADS)[None, :]).astype(F32)
    e_hd = e_dh.T
    onehot = (jnp.arange(REL_TABLE)[:, None] == _rel_index()[None, :]).astype(BF16)

    begun = comm.begun
    (h1,) = _rowwise(_fn_pre, [_win(x)], [g_pre_mix], [(D, BF16)], name="pre_mix", tm=512, after=begun)
    (mn,) = _rowwise(_fn_pre, [_win(mem)], [g_mem], [(D, BF16)], name="pre_mem", tm=512, after=begun)
    bias = _mm(wt["rel_bias"].astype(F32), onehot, name="mm_bias", split_a=3, after=begun).reshape(N_HEADS, CHUNK, BAND)
    wt = {**wt, **comm.first_weights([h1, mn, bias])}
    w_in = wt["w_in_p"]
    d_up = jnp.pad(wt["decay_up"].astype(F32), ((0, 128 - LORA_W), (0, 0)))
    i_up = jnp.pad(wt["iclr_up"].astype(F32), ((0, 128 - LORA_A), (0, 0)))
    g_up = jnp.pad(wt["gate_up"].astype(F32), ((0, 256 - LORA_G), (0, 0)))
    proj = _mm(h1, w_in, tb=True, name="mm_in", after=comm.first_token)
    z_rkv = _shift_fwd(proj, 0, 3 * D, mix_rkv, seq, name="shift_rkv")
    z_lora = _shift_fwd(proj, C_LORA, 512, mix_lora, seq, name="shift_lora")
    prep_rows = [_win(z_rkv, D, D), _win(z_lora, 0, 128), _win(z_lora, 128, 128), _win(z_lora, 256, 256)]
    prep_params = [decay_base, d_up, iclr_base, i_up, g_up, kns, kis, e_hd, e_dh]
    lw, k2, kk, a, g = _rowwise(_fn_prep, prep_rows, prep_params, [(D, F32)] * 5, name="rwkv_prep", tm=256)
    y, states, invs = _wkv_fwd(z_rkv, lw, k2, kk, a, seq)
    post_rows = [_win(y), _win(z_rkv, 0, D), _win(k2), _win(z_rkv, 2 * D, D), _win(g)]
    post_params = [lnx_w, lnx_b, bonus, e_hd, e_dh]
    (y_a,) = _rowwise(_fn_post, post_rows, post_params, [(D, BF16)], name="rwkv_post", tm=256)
    y_b = _attn_fwd(proj, bias, seq)
    wt = {**wt, **comm.late_weights(y_b)}
    ya_p = _mm(y_a, wt["w_branch_a"], name="mm_a")
    yb_p = _mm(y_b, wt["w_branch_b"], name="mm_b")
    mix_rows = [_win(proj, C_GA, D), _win(proj, C_GA + D, D), _win(ya_p), _win(yb_p)]
    (mixed,) = _rowwise(_fn_mix, mix_rows, [], [(D, BF16)], name="gate_mix", tm=512)
    mo = _mm(mixed, wt["w_out"], name="mm_out")
    x1, h2 = _rowwise(_fn_res_pre, [_win(x), _win(mo)], [g_post_mix, g_pre_cross], [(D, F32), (D, BF16)],
                      name="res_mix", tm=512)
    qm = _mm(h2, wt["w_q_mem"], name="mm_q")
    kvm = _mm(mn, wt["w_kv_mem"], name="mm_kv")
    om = _xattn_fwd(qm, kvm, seq, n_mem)
    co = _mm(om, wt["w_o_mem"], name="mm_o")
    x2, h3 = _rowwise(_fn_res_pre, [_win(x1), _win(co)], [g_post_cross, g_pre_ffn], [(D, F32), (D, BF16)],
                      name="res_cross", tm=512)
    gu = _mm(h3, wt["w_ffn_in"], tb=True, name="mm_ffn_in", out_dtype=BF16)
    (act,) = _rowwise(_fn_swiglu, [_win(gu, 0, FFN), _win(gu, FFN, FFN)], [], [(FFN, BF16)], name="swiglu", tm=512)
    ff = _mm(act, wt["w_ffn_out"], name="mm_ffn_out")

    gw = {}
    loss, dx2, dff, gw["g_post_ffn"] = _loss_head(x2, ff, g_post_ffn, target)
    dact = _mm(dff, wt["w_ffn_out"], tb=True, name="mm_ffn_out_dx", out_dtype=BF16)
    gw["w_ffn_out"] = _mm(act, dff, ta=True, name="mm_ffn_out_dw", out_dtype=BF16)
    (dgu,), _ = _rowwise_bwd(_fn_swiglu, [_win(gu, 0, FFN), _win(gu, FFN, FFN)], [], 0, [[dact]],
                             name="swiglu_bwd", tm=512, row_grad=[BF16, BF16], packed=True)
    dh3 = _mm(dgu, wt["w_ffn_in"], name="mm_ffn_in_dx", out_dtype=BF16)
    gw["w_ffn_in"] = _mm(dgu, h3, ta=True, name="mm_ffn_in_dw", out_dtype=BF16)
    (dx1, dco), (gw["g_post_cross"], gw["g_pre_ffn"]) = _rowwise_bwd(
        _fn_res_pre, [_win(x1), _win(co)], [g_post_cross, g_pre_ffn], 0, [[dx2], [dh3]],
        name="res_cross_bwd", tm=512, row_grad=[F32, BF16])
    dom = _mm(dco, wt["w_o_mem"], tb=True, name="mm_o_dx", out_dtype=BF16)
    gw["w_o_mem"] = _mm(om, dco, ta=True, name="mm_o_dw", out_dtype=BF16)
    dqm, dkvm = _xattn_bwd(qm, kvm, dom, seq, n_mem)
    dh2 = _mm(dqm, wt["w_q_mem"], tb=True, name="mm_q_dx", out_dtype=BF16)
    gw["w_q_mem"] = _mm(h2, dqm, ta=True, name="mm_q_dw", out_dtype=BF16)
    dmn = _mm(dkvm, wt["w_kv_mem"], tb=True, name="mm_kv_dx", out_dtype=BF16)
    gw["w_kv_mem"] = _mm(mn, dkvm, ta=True, name="mm_kv_dw", out_dtype=BF16)
    _, (gw["g_mem"],) = _rowwise_bwd(_fn_pre, [_win(mem)], [g_mem], 0, [[dmn]], name="pre_mem_bwd", tm=256,
                                     row_grad=[None])
    (dx0, dmo), (gw["g_post_mix"], gw["g_pre_cross"]) = _rowwise_bwd(
        _fn_res_pre, [_win(x), _win(mo)], [g_post_mix, g_pre_cross], 0, [[dx1], [dh2]],
        name="res_mix_bwd", tm=512, row_grad=[F32, BF16])
    dmixed = _mm(dmo, wt["w_out"], tb=True, name="mm_out_dx", out_dtype=BF16)
    gw["w_out"] = _mm(mixed, dmo, ta=True, name="mm_out_dw", out_dtype=BF16)
    (dzga, dzgb, dya_p, dyb_p), _ = _rowwise_bwd(_fn_mix, mix_rows, [], 0, [[dmixed]], name="gate_mix_bwd", tm=512,
                                                 row_grad=[BF16] * 4)
    gw["w_branch_a"] = _mm(y_a, dya_p, ta=True, name="mm_a_dw", out_dtype=BF16)
    gw["w_branch_b"] = _mm(y_b, dyb_p, ta=True, name="mm_b_dw", out_dtype=BF16)
    token = comm.send_early(gw)
    dy_a = _mm(dya_p, wt["w_branch_a"], tb=True, name="mm_a_dx", out_dtype=BF16, after=token)
    dy_b = _mm(dyb_p, wt["w_branch_b"], tb=True, name="mm_b_dx", out_dtype=BF16, after=token)
    dq, dk, dv, dbias = _attn_bwd(proj, bias, dy_b, seq)
    gw["rel_bias"] = _mm(dbias.reshape(N_HEADS, CHUNK * BAND), onehot, tb=True, name="mm_bias_dw", split_a=2)
    (dy, dr_p, dk2_p, dv_p, dg), (gw["lnx_w"], gw["lnx_b"], gw["bonus_scale"]) = _rowwise_bwd(
        _fn_post, post_rows, post_params, 2, [[dy_a]], name="rwkv_post_bwd", tm=512, row_grad=[BF16] * 5)
    dr_s, dlw, dk2_s, dv_s, dkk, da = _wkv_bwd(z_rkv, lw, k2, kk, a, states, invs, dy, seq)
    (dzk, dzw, dza, dzg), pg = _rowwise_bwd(
        _fn_prep, prep_rows, prep_params, 2, [[dlw], [dk2_p, dk2_s], [dkk], [da], [dg]],
        name="rwkv_prep_bwd", tm=512, row_grad=[BF16] * 4)
    gw["decay_base"], gd_up, gw["iclr_base"], gi_up, gg_up, gw["key_norm_scale"], gw["key_iclr_scale"] = pg
    gw["decay_up"], gw["iclr_up"], gw["gate_up"] = gd_up[:LORA_W], gi_up[:LORA_A], gg_up[:LORA_G]
    dp_r, gmix_r = _shift_bwd(proj, 0, D, mix_rkv[:, :D], [dr_p, dr_s], seq, name="shift_r_bwd")
    dp_k, gmix_k = _shift_bwd(proj, D, D, mix_rkv[:, D:2 * D], [dzk], seq, name="shift_k_bwd")
    dp_v, gmix_v = _shift_bwd(proj, 2 * D, D, mix_rkv[:, 2 * D:], [dv_p, dv_s], seq, name="shift_v_bwd")
    dp_lora, gmix_lora = _shift_bwd(proj, C_LORA, 512, mix_lora, [jnp.concatenate([dzw, dza, dzg], axis=1)], seq,
                                    name="shift_lora_bwd")
    gw["shift_mix"] = jnp.concatenate([gmix_r, gmix_k, gmix_v, _unpad_lora(gmix_lora)], axis=1)
    dproj = [dp_r, dp_k, dp_v, dq, dk, dv, dzga, dzgb, dp_lora]
    gw["w_in_p"] = _mm_cat_tn(dproj, h1, name="mm_in_dw", after=gw["rel_bias"])
    token = comm.send_late(gw)
    dh1 = _mm_cat_nn(dproj, w_in, name="mm_in_dx", after=token)
    (grad_x,), (gw["g_pre_mix"],) = _rowwise_bwd(_fn_pre, [_win(x)], [g_pre_mix], 0, [[dh1]], name="pre_mix_bwd",
                                                 tm=512, row_grad=[F32], add_to={0: dx0})
    return loss, grad_x, gw


_COL_SHARDED = ("w_in", "decay_up", "iclr_up", "gate_up", "w_o_mem", "w_ffn_in")
_ROW_SHARDED = ("w_branch_a", "w_branch_b", "w_out", "w_q_mem", "w_kv_mem", "w_ffn_out")
_TRANSPOSED = ("w_in", "w_ffn_in")
_FIRST = ("w_in", "decay_up", "iclr_up", "gate_up")
_REST = ("w_o_mem", "w_ffn_in", "w_branch_a", "w_branch_b", "w_out", "w_q_mem", "w_kv_mem", "w_ffn_out")
_REPLICATED = ("g_pre_mix", "g_post_mix", "shift_mix", "decay_base", "iclr_base", "key_norm_scale", "key_iclr_scale",
               "bonus_scale", "lnx_w", "lnx_b", "rel_bias", "g_pre_cross", "g_post_cross", "g_mem", "g_pre_ffn",
               "g_post_ffn")
_WEIGHTS = ("g_pre_mix", "g_post_mix", "w_in", "shift_mix", "decay_base", "decay_up", "iclr_base", "iclr_up", "gate_up",
            "key_norm_scale", "key_iclr_scale", "bonus_scale", "lnx_w", "lnx_b", "rel_bias", "w_branch_a", "w_branch_b",
            "w_out", "g_pre_cross", "g_post_cross", "g_mem", "w_q_mem", "w_kv_mem", "w_o_mem", "g_pre_ffn", "g_post_ffn",
            "w_ffn_in", "w_ffn_out")
_PACK_ROWS = 8 * ((sum({"shift_mix": 3360, "bonus_scale": 1024, "rel_bias": 3072}.get(n, D) for n in _REPLICATED)
                   + 1 + 8 * LANE - 1) // (8 * LANE))


def _pack(vals):
    flat = jnp.concatenate([v.reshape(-1).astype(F32) for v in vals])
    return jnp.pad(flat, (0, _PACK_ROWS * LANE - flat.shape[0])).reshape(_PACK_ROWS, LANE)


def _unpack(packed, shapes):
    flat, out, pos = packed.reshape(-1), [], 0
    for s in shapes:
        n = math.prod(s)
        out.append(flat[pos:pos + n].reshape(s))
        pos += n
    return out


def _step(args, seq, n_mem):
    names = ("x", "mem") + _WEIGHTS + ("loss_target",) + tuple("m_" + n for n in _WEIGHTS) + tuple("v_" + n for n in _WEIGHTS)
    given = dict(zip(names, args))
    nb = given["x"].shape[0]
    x = given["x"].reshape(nb * seq, D)
    mem = given["mem"].reshape(nb * n_mem, D)
    target = given["loss_target"].reshape(nb * seq, D)
    def local(name, prefix=""):
        a = given[prefix + name][0]
        return a.T if name in _TRANSPOSED else a

    shard = {n: local(n) for n in _COL_SHARDED + _ROW_SHARDED}
    stacked = _ROW_SHARDED + _TRANSPOSED
    out = {}

    def wire(name):
        return shard[name].astype(BF16)

    def full(name, g):
        return g.reshape(-1, g.shape[-1]) if name in stacked else _cols_to_full(g)

    def blocks_of(name, g):
        return (g.reshape((N_DEV,) + shard[name].shape) if name in stacked else _full_to_cols(g)).astype(BF16)

    def update(names, landed, after=None):
        done = []
        for n, parts in zip(names, landed):
            res = _adamw(parts, shard[n], local(n, "m_"), local(n, "v_"), name="adamw_" + n, after=after)
            for kind, r in zip(("grad_", "delta_", "new_m_", "new_v_"), res):
                out[kind + n] = (r.T if n in _TRANSPOSED else r)[None]
            done.append(res[0])
        return done


    class Exchanges:
        def __init__(self):
            srcs = [wire(n) for n in _FIRST]
            self.first, self.begun = _exchange_start(srcs, False, srcs[0], name="gather_first_start",
                                                     dists=_SIBLING_AND_SAME_CORE)

        def first_weights(self, after):
            got = _exchange_wait(self.first, after, [wire(n) for n in _FIRST], name="gather_first_wait")
            relayed = _relay_to_sibling(got, name="gather_first_relay")
            pos = _mesh_pos()
            for j, d in enumerate(_OTHER_CHIPS):
                slot = _flat(_peer(pos, d | 1))
                got = [lax.dynamic_update_slice_in_dim(g, r[j][None], slot, 0) for g, r in zip(got, relayed)]
            self.rest, self.first_token = _exchange_start(
                [wire(n) for n in _REST], False, got[0], name="gather_rest_start")
            first = {n: full(n, g) for n, g in zip(_FIRST, got)}
            first["w_in_p"] = _permute_in(first.pop("w_in"), 0)
            return first

        def late_weights(self, after):
            got = _exchange_wait(self.rest, [after], [wire(n) for n in _REST], name="gather_rest_wait")
            return {n: full(n, g) for n, g in zip(_REST, got)}

        def send_early(self, gw):
            self.early_blocks = [blocks_of(n, gw[n]) for n in _REST]
            self.early, token = _exchange_start(self.early_blocks, True, self.early_blocks[-1], name="scatter_rest_start")
            return token

        def send_late(self, gw):
            me = _flat(_mesh_pos())
            own = [lax.dynamic_index_in_dim(b, me, 0, keepdims=False) for b in self.early_blocks]
            landed = _exchange_wait(self.early, [gw["w_in_p"]], own, name="scatter_rest_wait")
            grads = {**gw, "w_in": _unpermute_in(gw["w_in_p"], 0)}
            self.late_blocks = [blocks_of(n, grads[n]) for n in _FIRST]
            self.late, token = _exchange_start(self.late_blocks, True, landed[0], name="scatter_first_start")
            self.updated = update(_REST, landed, after=token)
            return token

        def finish(self, after):
            me = _flat(_mesh_pos())
            own = [lax.dynamic_index_in_dim(b, me, 0, keepdims=False) for b in self.late_blocks]
            update(_FIRST, _exchange_wait(self.late, [*after, *self.updated], own, name="scatter_first_wait"))

    comm = Exchanges()
    wt = {n: given[n][0] for n in _REPLICATED}
    loss_tile, grad_x, gw = _local_step(x, mem, target, wt, seq, n_mem, comm)
    rep_shapes = [given[n].shape for n in _REPLICATED]
    packed, _ = lax.optimization_barrier((_pack([gw[n] for n in _REPLICATED] + [loss_tile[0, 0]]), tuple(comm.updated)))
    small = _exchange([packed], False, name="gather_small")[0]
    zero = jnp.zeros((), F32)
    res = _adamw(small, *[_pack([given[p + n] for n in _REPLICATED] + [zero]) for p in ("", "m_", "v_")],
                 name="adamw_small", tr=_PACK_ROWS)
    for kind, r in zip(("grad_", "delta_", "new_m_", "new_v_"), res):
        for n, val in zip(_REPLICATED, _unpack(r, rep_shapes)):
            out[kind + n] = val
    loss = res[0].reshape(-1)[sum(math.prod(s) for s in rep_shapes)]
    comm.finish([grad_x, res[0]])
    grad_x = grad_x.reshape(nb, seq, D)
    return (loss, grad_x, *[out[k + n] for k in ("grad_", "delta_", "new_m_", "new_v_") for n in _WEIGHTS])


def kernel(x, mem, g_pre_mix, g_post_mix, w_in, shift_mix, decay_base, decay_up, iclr_base, iclr_up, gate_up, key_norm_scale, key_iclr_scale, bonus_scale, lnx_w, lnx_b, rel_bias, w_branch_a, w_branch_b, w_out, g_pre_cross, g_post_cross, g_mem, w_q_mem, w_kv_mem, w_o_mem, g_pre_ffn, g_post_ffn, w_ffn_in, w_ffn_out, loss_target, m_g_pre_mix, m_g_post_mix, m_w_in, m_shift_mix, m_decay_base, m_decay_up, m_iclr_base, m_iclr_up, m_gate_up, m_key_norm_scale, m_key_iclr_scale, m_bonus_scale, m_lnx_w, m_lnx_b, m_rel_bias, m_w_branch_a, m_w_branch_b, m_w_out, m_g_pre_cross, m_g_post_cross, m_g_mem, m_w_q_mem, m_w_kv_mem, m_w_o_mem, m_g_pre_ffn, m_g_post_ffn, m_w_ffn_in, m_w_ffn_out, v_g_pre_mix, v_g_post_mix, v_w_in, v_shift_mix, v_decay_base, v_decay_up, v_iclr_base, v_iclr_up, v_gate_up, v_key_norm_scale, v_key_iclr_scale, v_bonus_scale, v_lnx_w, v_lnx_b, v_rel_bias, v_w_branch_a, v_w_branch_b, v_w_out, v_g_pre_cross, v_g_post_cross, v_g_mem, v_w_q_mem, v_w_kv_mem, v_w_o_mem, v_g_pre_ffn, v_g_post_ffn, v_w_ffn_in, v_w_ffn_out):
    args = (x, mem, g_pre_mix, g_post_mix, w_in, shift_mix, decay_base, decay_up, iclr_base, iclr_up, gate_up, key_norm_scale, key_iclr_scale, bonus_scale, lnx_w, lnx_b, rel_bias, w_branch_a, w_branch_b, w_out, g_pre_cross, g_post_cross, g_mem, w_q_mem, w_kv_mem, w_o_mem, g_pre_ffn, g_post_ffn, w_ffn_in, w_ffn_out, loss_target, m_g_pre_mix, m_g_post_mix, m_w_in, m_shift_mix, m_decay_base, m_decay_up, m_iclr_base, m_iclr_up, m_gate_up, m_key_norm_scale, m_key_iclr_scale, m_bonus_scale, m_lnx_w, m_lnx_b, m_rel_bias, m_w_branch_a, m_w_branch_b, m_w_out, m_g_pre_cross, m_g_post_cross, m_g_mem, m_w_q_mem, m_w_kv_mem, m_w_o_mem, m_g_pre_ffn, m_g_post_ffn, m_w_ffn_in, m_w_ffn_out, v_g_pre_mix, v_g_post_mix, v_w_in, v_shift_mix, v_decay_base, v_decay_up, v_iclr_base, v_iclr_up, v_gate_up, v_key_norm_scale, v_key_iclr_scale, v_bonus_scale, v_lnx_w, v_lnx_b, v_rel_bias, v_w_branch_a, v_w_branch_b, v_w_out, v_g_pre_cross, v_g_post_cross, v_g_mem, v_w_q_mem, v_w_kv_mem, v_w_o_mem, v_g_pre_ffn, v_g_post_ffn, v_w_ffn_in, v_w_ffn_out)
    return _step(args, x.shape[1], mem.shape[1])
```

```python
import functools
import math

import jax
import jax.numpy as jnp
from jax import lax
from jax.experimental import pallas as pl
from jax.experimental.pallas import tpu as pltpu

F32 = jnp.float32
BF16 = jnp.bfloat16

N_DEV = 8
D = 1024
HEAD = 64
N_HEADS = D // HEAD
LANE = 128
CHUNK = 64
LEFT = 8 * CHUNK
BAND = LEFT + CHUNK
REL_CLIP = 128
REL_TABLE = CHUNK + REL_CLIP
MEM_WIDTH = D // 2
MEM_HEADS = 4
FFN = 2816
LORA_W, LORA_A, LORA_G = 64, 64, 160
P_WIDTH = 3 * D + 3 * D + 2 * D + 128 + 128 + 256
C_Q, C_GA, C_LORA = 3 * D, 6 * D, 8 * D
NORM_EPS = 1e-6
GROUP_NORM_EPS = 64e-5
MASK_VALUE = -1e30
ADAM_LR, ADAM_B1, ADAM_B2, ADAM_EPS, ADAM_WD, ADAM_STEP = 0.001, 0.9, 0.999, 1e-08, 0.01, 10
VMEM_LIMIT = 56 * 1024 * 1024


def _cp(*sem):
    return pltpu.CompilerParams(dimension_semantics=sem, vmem_limit_bytes=VMEM_LIMIT)


_NN, _NT, _TN = ((1,), (0,)), ((1,), (1,)), ((0,), (0,))


def _dot_raw(a, b, dims):
    return lax.dot_general(a.astype(BF16), b.astype(BF16), (dims, ((), ())), preferred_element_type=F32)


@functools.partial(jax.custom_vjp, nondiff_argnums=(2,))
def _dot_dims(a, b, dims):
    return _dot_raw(a, b, dims)


def _dot_dims_fwd(a, b, dims):
    return _dot_raw(a, b, dims), (a, b)


def _dot_dims_bwd(dims, res, g):
    a, b = res
    if dims == _NN:
        da, db = _dot_raw(g, b, _NT), _dot_raw(a, g, _TN)
    elif dims == _NT:
        da, db = _dot_raw(g, b, _NN), _dot_raw(g, a, _TN)
    else:
        da, db = _dot_raw(b, g, _NT), _dot_raw(a, g, _NN)
    return da.astype(a.dtype), db.astype(b.dtype)


_dot_dims.defvjp(_dot_dims_fwd, _dot_dims_bwd)


def _dot(a, b, dims=_NN):
    return _dot_dims(a, b, dims)


def _dot_nt(a, b):
    return _dot_dims(a, b, _NT)


def _dot_tn(a, b):
    return _dot_dims(a, b, _TN)


def _split(x, terms):
    parts, rest = [], x.astype(F32)
    for _ in range(terms):
        p = rest.astype(BF16)
        parts.append(p)
        rest = rest - p.astype(F32)
    return parts


def _dot_split_a(a, b, terms=2):
    out = None
    for p in _split(a, terms):
        t = _dot(p, b)
        out = t if out is None else out + t
    return out


def _dot_split_b(a, b, terms=3):
    out = None
    for p in _split(b, terms):
        t = _dot(a, p)
        out = t if out is None else out + t
    return out


MM_VMEM_BUDGET = 30 * 1024 * 1024
MM_HBM_BPS = 3.2e12
MM_MXU_FPS = 8.5e14
MM_STEP_S = 0.35e-6


def _divisors(n, align, cap):
    out = [d for d in range(align, min(n, cap) + 1, align) if n % d == 0]
    return out or [n]


def _mm_tiles(m, n, k, ea, eb, eo, ta):
    best = None
    for tm in _divisors(m, LANE if ta else 8, 2048):
        for tn in _divisors(n, LANE, 2048):
            for tk in _divisors(k, LANE, 2048):
                nk = k // tk
                vmem = 2 * (tm * tk * ea + tk * tn * eb + tm * tn * eo) + (tm * tn * 4 if nk > 1 else 0)
                if vmem > MM_VMEM_BUDGET:
                    continue
                dma = (tm * tk * ea if (nk > 1 or n // tn == 1) else tm * tk * ea * tn / n) + tk * tn * eb + tm * tn * eo / nk
                step = max(2.0 * tm * tn * tk / MM_MXU_FPS, dma / MM_HBM_BPS) + MM_STEP_S
                cost = (m // tm) * (n // tn) * nk * step
                if best is None or cost < best[0]:
                    best = (cost, tm, tn, tk)
    return best[1:]


def _mm(a, b, *, name, ta=False, tb=False, out_dtype=F32, tm=None, tn=None, tk=None, split_a=1, after=None):
    m, k = (a.shape[1], a.shape[0]) if ta else a.shape
    n, kb = (b.shape[0], b.shape[1]) if tb else (b.shape[1], b.shape[0])
    assert k == kb, (a.shape, b.shape, ta, tb)
    if tm is None:
        tm, tn, tk = _mm_tiles(m, n, k, a.dtype.itemsize, b.dtype.itemsize, jnp.dtype(out_dtype).itemsize, ta)
    assert m % tm == 0 and n % tn == 0 and k % tk == 0, (m, n, k, tm, tn, tk)
    nk = k // tk
    dims = ((0 if ta else 1,), (1 if tb else 0,))

    n_after = 0 if after is None else 1

    def body(a_ref, b_ref, *rest):
        o_ref, scratch = rest[n_after], rest[n_after + 1:]
        prod = None
        for p in _split(a_ref[...], split_a) if split_a > 1 else [a_ref[...]]:
            t = _dot_raw(p, b_ref[...], dims)
            prod = t if prod is None else prod + t
        if nk == 1:
            o_ref[...] = prod.astype(o_ref.dtype)
            return
        acc_ref, kk = scratch[0], pl.program_id(2)

        @pl.when(kk == 0)
        def _():
            acc_ref[...] = prod

        @pl.when(kk > 0)
        def _():
            acc_ref[...] += prod

        @pl.when(kk == nk - 1)
        def _():
            o_ref[...] = acc_ref[...].astype(o_ref.dtype)

    a_spec = pl.BlockSpec((tk, tm), lambda i, j, q: (q, i)) if ta else pl.BlockSpec((tm, tk), lambda i, j, q: (i, q))
    b_spec = pl.BlockSpec((tn, tk), lambda i, j, q: (j, q)) if tb else pl.BlockSpec((tk, tn), lambda i, j, q: (q, j))
    return pl.pallas_call(
        body, name=name, grid=(m // tm, n // tn, nk),
        in_specs=[a_spec, b_spec] + [pl.BlockSpec(memory_space=pl.ANY)] * n_after,
        out_specs=pl.BlockSpec((tm, tn), lambda i, j, q: (i, j)),
        out_shape=jax.ShapeDtypeStruct((m, n), out_dtype),
        scratch_shapes=[pltpu.VMEM((tm, tn), F32)] if nk > 1 else [],
        compiler_params=_cp("parallel", "parallel", "arbitrary"),
    )(a, b, *([] if after is None else [after]))


def _piece_steps(pieces, tile):
    counts = [p.shape[1] // tile for p in pieces]
    assert all(p.shape[1] % tile == 0 for p in pieces)
    return [(sum(counts[:i]), c) for i, c in enumerate(counts)], sum(counts)


def _mm_cat_nn(pieces, w, *, name, after=None, tm=2048, tk=256):
    t, n = pieces[0].shape[0], w.shape[1]
    tm = min(tm, t)
    spans, nk = _piece_steps(pieces, tk)
    npc = len(pieces)
    n_after = 0 if after is None else 1

    def body(*refs):
        w_ref, o_ref, acc_ref = refs[npc], refs[npc + 1 + n_after], refs[npc + 2 + n_after]
        q = pl.program_id(1)

        @pl.when(q == 0)
        def _():
            acc_ref[...] = jnp.zeros_like(acc_ref)

        for p_ref, (first, count) in zip(refs[:npc], spans):
            @pl.when(jnp.logical_and(q >= first, q < first + count))
            def _(p_ref=p_ref):
                acc_ref[...] += _dot_raw(p_ref[...], w_ref[...], _NN)

        @pl.when(q == nk - 1)
        def _():
            o_ref[...] = acc_ref[...].astype(o_ref.dtype)

    def piece_spec(first, count):
        return pl.BlockSpec((tm, tk), lambda i, q: (i, jnp.clip(q - first, 0, count - 1)))

    return pl.pallas_call(
        body, name=name, grid=(t // tm, nk),
        in_specs=[piece_spec(*s) for s in spans] + [pl.BlockSpec((tk, n), lambda i, q: (q, 0))]
        + [pl.BlockSpec(memory_space=pl.ANY)] * n_after,
        out_specs=pl.BlockSpec((tm, n), lambda i, q: (i, 0)),
        out_shape=jax.ShapeDtypeStruct((t, n), BF16),
        scratch_shapes=[pltpu.VMEM((tm, n), F32)],
        compiler_params=_cp("parallel", "arbitrary"),
    )(*pieces, w, *([] if after is None else [after]))


def _mm_cat_tn(pieces, a, *, name, after=None, tk=1024, tn=512):
    t, m = a.shape
    tk = min(tk, t)
    spans, nj = _piece_steps(pieces, tn)
    npc, nk = len(pieces), t // tk
    n_after = 0 if after is None else 1

    def body(a_ref, *refs):
        o_ref, acc_ref = refs[npc + n_after], refs[npc + 1 + n_after]
        j, q = pl.program_id(0), pl.program_id(1)

        @pl.when(q == 0)
        def _():
            acc_ref[...] = jnp.zeros_like(acc_ref)

        for p_ref, (first, count) in zip(refs[:npc], spans):
            @pl.when(jnp.logical_and(j >= first, j < first + count))
            def _(p_ref=p_ref):
                acc_ref[...] += _dot_raw(p_ref[...], a_ref[...], _TN)

        @pl.when(q == nk - 1)
        def _():
            o_ref[...] = acc_ref[...].astype(o_ref.dtype)

    def piece_spec(first, count):
        def index(j, q):
            mine = jnp.logical_and(j >= first, j < first + count)
            return jnp.where(mine, q, 0), jnp.clip(j - first, 0, count - 1)
        return pl.BlockSpec((tk, tn), index)

    return pl.pallas_call(
        body, name=name, grid=(nj, nk),
        in_specs=[pl.BlockSpec((tk, m), lambda j, q: (q, 0))] + [piece_spec(*s) for s in spans]
        + [pl.BlockSpec(memory_space=pl.ANY)] * n_after,
        out_specs=pl.BlockSpec((tn, m), lambda j, q: (j, 0)),
        out_shape=jax.ShapeDtypeStruct((nj * tn, m), BF16),
        scratch_shapes=[pltpu.VMEM((tn, m), F32)],
        compiler_params=_cp("parallel", "arbitrary"),
    )(a, *pieces, *([] if after is None else [after]))


def _win(arr, start=0, width=None):
    width = arr.shape[1] if width is None else width
    assert start % width == 0
    return (arr, start // width, width)


def _row_specs(rows, tm):
    return [pl.BlockSpec((tm, w), functools.partial(lambda i, cb: (i, cb), cb=cb)) for (_, cb, w) in rows]


def _full_spec(p):
    nd = p.ndim
    return pl.BlockSpec(p.shape, lambda i, nd=nd: (0,) * nd)


def _rowwise(fn, rows, params, outs, *, name, tm, after=None):
    t = rows[0][0].shape[0]
    tm = min(tm, t)
    assert t % tm == 0
    nr, npar = len(rows), len(params)
    n_after = 0 if after is None else 1

    def body(*refs):
        vals = [r[...] for r in refs[:nr + npar]]
        res = fn(*vals)
        for o_ref, r in zip(refs[nr + npar + n_after:], res):
            o_ref[...] = r.astype(o_ref.dtype)

    return pl.pallas_call(
        body, name=name, grid=(t // tm,),
        in_specs=_row_specs(rows, tm) + [_full_spec(p) for p in params] + [pl.BlockSpec(memory_space=pl.ANY)] * n_after,
        out_specs=[pl.BlockSpec((tm, w), lambda i: (i, 0)) for (w, _) in outs],
        out_shape=[jax.ShapeDtypeStruct((t, w), dt) for (w, dt) in outs],
        compiler_params=_cp("parallel"),
    )(*[r[0] for r in rows], *params, *([] if after is None else [after]))


def _rowwise_bwd(fn, rows, params, n_const, cots, *, name, tm, row_grad, add_to=None, packed=False):
    t = rows[0][0].shape[0]
    tm = min(tm, t)
    assert t % tm == 0
    nr, npar = len(rows), len(params)
    ndp = npar - n_const
    add_to = add_to or {}
    add_idx = sorted(add_to)
    flat_cots = [c for group in cots for c in group]
    kept = [i for i in range(nr) if row_grad[i] is not None]

    def body(*refs):
        pos = 0
        row_v = [r[...] for r in refs[pos:pos + nr]]; pos += nr
        par_v = [r[...] for r in refs[pos:pos + npar]]; pos += npar
        cot_v = [r[...] for r in refs[pos:pos + len(flat_cots)]]; pos += len(flat_cots)
        add_v = [r[...] for r in refs[pos:pos + len(add_idx)]]; pos += len(add_idx)
        if packed:
            offs = [sum(rows[i][2] for i in kept[:q]) for q in range(len(kept))]
            rg_refs = [refs[pos].at[:, o:o + rows[i][2]] for o, i in zip(offs, kept)]; pos += 1
        else:
            rg_refs = refs[pos:pos + len(kept)]; pos += len(kept)
        pg_refs = refs[pos:pos + ndp]

        consts = par_v[ndp:]
        res, vjp = jax.vjp(lambda *args: tuple(fn(*args, *consts)), *row_v, *par_v[:ndp])
        cot_in, q = [], 0
        for j, group in enumerate(cots):
            c = None
            for _ in group:
                cv = cot_v[q].astype(F32); q += 1
                c = cv if c is None else c + cv
            c = jnp.zeros(res[j].shape, F32) if c is None else c
            cot_in.append(c.astype(res[j].dtype))
        grads = vjp(tuple(cot_in))
        for ref, i in zip(rg_refs, kept):
            g = grads[i].astype(F32)
            if i in add_to:
                g = g + add_v[add_idx.index(i)].astype(F32)
            ref[...] = g.astype(ref.dtype)

        @pl.when(pl.program_id(0) == 0)
        def _():
            for ref in pg_refs:
                ref[...] = jnp.zeros_like(ref)

        for ref, g in zip(pg_refs, grads[nr:]):
            ref[...] += g.astype(F32)

    cot_specs = [pl.BlockSpec((tm, c.shape[1]), lambda i: (i, 0)) for c in flat_cots]
    add_specs = [pl.BlockSpec((tm, add_to[i].shape[1]), lambda i_: (i_, 0)) for i in add_idx]
    widths = [sum(rows[i][2] for i in kept)] if packed else [rows[i][2] for i in kept]
    n_rg = len(widths)
    out_specs = [pl.BlockSpec((tm, w), lambda i_: (i_, 0)) for w in widths] + [_full_spec(p) for p in params[:ndp]]
    out_shape = [jax.ShapeDtypeStruct((t, w), row_grad[kept[q]]) for q, w in enumerate(widths)] + [
        jax.ShapeDtypeStruct(p.shape, F32) for p in params[:ndp]]
    res = pl.pallas_call(
        body, name=name, grid=(t // tm,),
        in_specs=_row_specs(rows, tm) + [_full_spec(p) for p in params] + cot_specs + add_specs,
        out_specs=out_specs, out_shape=out_shape,
        compiler_params=_cp("arbitrary"),
    )(*[r[0] for r in rows], *params, *flat_cots, *[add_to[i] for i in add_idx])
    return list(res[:n_rg]), list(res[n_rg:])


def _rms(x, g):
    xf = x.astype(F32)
    return xf * lax.rsqrt(jnp.mean(xf * xf, axis=-1, keepdims=True) + NORM_EPS) * g


def _softplus(x):
    return jnp.maximum(x, 0.0) + jnp.log(1.0 + jnp.exp(-jnp.abs(x)))


def _fn_pre(x, g):
    return (_rms(x, g).astype(BF16),)


def _fn_res(x, u, g_post):
    return (x + _rms(u, g_post),)


def _fn_res_pre(x, u, g_post, g_pre):
    xn = x + _rms(u, g_post)
    return xn, _rms(xn, g_pre).astype(BF16)


def _fn_mix(zga, zgb, ya, yb):
    return ((jax.nn.sigmoid(zga) * ya + jax.nn.sigmoid(zgb) * yb).astype(BF16),)


def _fn_swiglu(gate, up):
    gate, up = gate.astype(F32), up.astype(F32)
    return ((gate * jax.nn.sigmoid(gate) * up).astype(BF16),)


def _fn_prep(zk, zw, za, zg, decay_base, d_up, iclr_base, i_up, g_up, kns, kis, e_hd, e_dh):
    w_log = -_softplus(-(decay_base + _dot(jnp.tanh(zw), d_up))) - 0.5
    lw = -jnp.exp(w_log)
    a = jax.nn.sigmoid(iclr_base + _dot(za, i_up))
    g = _dot(jax.nn.sigmoid(zg), g_up)
    kn = zk * kns
    ss = _dot(kn * kn, e_dh)
    inv = lax.rsqrt(jnp.maximum(ss, 1e-24))
    kk = kn * _dot_split_a(inv, e_hd)
    k2 = zk * (1.0 + (a - 1.0) * kis)
    return lw, k2, kk, a, g


def _fn_post(y, r, k2, v, g, lnx_w, lnx_b, bonus, e_hd, e_dh):
    mu = _dot_split_a(_dot(y, e_dh) * (1.0 / HEAD), e_hd)
    yc = y - mu
    var = _dot(yc * yc, e_dh) * (1.0 / HEAD)
    yn = yc * _dot_split_a(lax.rsqrt(var + GROUP_NORM_EPS), e_hd)
    bs = _dot_split_a(_dot(r * k2 * bonus, e_dh), e_hd)
    return (((yn * lnx_w + lnx_b + bs * v) * g).astype(BF16),)


def _shift_fwd(p, col0, ncols, mix, seq, *, name, cw=256):
    t = p.shape[0]
    assert col0 % cw == 0 and ncols % cw == 0 and t % seq == 0
    cb0 = col0 // cw

    def body(p_ref, m_ref, z_ref):
        pv = p_ref[...]
        row = lax.broadcasted_iota(jnp.int32, pv.shape, 0)
        prev = jnp.where(row == 0, 0.0, pltpu.roll(pv, 1, axis=0))
        z_ref[...] = pv + (prev - pv) * m_ref[...]

    return pl.pallas_call(
        body, name=name, grid=(t // seq, ncols // cw),
        in_specs=[pl.BlockSpec((seq, cw), lambda b, c: (b, c + cb0)), pl.BlockSpec((1, cw), lambda b, c: (0, c))],
        out_specs=pl.BlockSpec((seq, cw), lambda b, c: (b, c)),
        out_shape=jax.ShapeDtypeStruct((t, ncols), F32),
        compiler_params=_cp("parallel", "parallel"),
    )(p, mix)


def _shift_bwd(p, col0, ncols, mix, dz_parts, seq, *, name, cw=256):
    t = p.shape[0]
    cb0 = col0 // cw
    n = len(dz_parts)

    def body(*refs):
        p_ref, m_ref = refs[:2]
        dp_ref, dm_ref = refs[2 + n:]
        dz = refs[2][...].astype(F32)
        for r in refs[3:2 + n]:
            dz = dz + r[...].astype(F32)
        pv = p_ref[...]
        mixv = m_ref[...]
        row = lax.broadcasted_iota(jnp.int32, pv.shape, 0)
        prev = jnp.where(row == 0, 0.0, pltpu.roll(pv, 1, axis=0))
        u = dz * mixv
        nxt = jnp.where(row == seq - 1, 0.0, pltpu.roll(u, seq - 1, axis=0))
        dp_ref[...] = (dz - u + nxt).astype(dp_ref.dtype)

        @pl.when(pl.program_id(1) == 0)
        def _():
            dm_ref[...] = jnp.zeros_like(dm_ref)

        dm_ref[...] += jnp.sum(dz * (prev - pv), axis=0, keepdims=True)

    return pl.pallas_call(
        body, name=name, grid=(ncols // cw, t // seq),
        in_specs=[pl.BlockSpec((seq, cw), lambda c, b: (b, c + cb0)), pl.BlockSpec((1, cw), lambda c, b: (0, c))]
        + [pl.BlockSpec((seq, cw), lambda c, b: (b, c))] * n,
        out_specs=[pl.BlockSpec((seq, cw), lambda c, b: (b, c)), pl.BlockSpec((1, cw), lambda c, b: (0, c))],
        out_shape=[jax.ShapeDtypeStruct((t, ncols), BF16), jax.ShapeDtypeStruct((1, ncols), F32)],
        compiler_params=_cp("parallel", "arbitrary"),
    )(p, mix, *dz_parts)


def _each(f, *lists):
    return [f(*xs) for xs in zip(*lists)]


def _tri_inv(low):
    c = low[0].shape[0]
    ti = lax.broadcasted_iota(jnp.int32, (c, c), 0)
    si = lax.broadcasted_iota(jnp.int32, (c, c), 1)
    eye = (ti == si).astype(F32)
    inside = (ti // 4) == (si // 4)
    base = [jnp.where(inside, m, 0.0) for m in low]
    acc = _each(lambda m: _dot(eye - m, eye + _dot(m, m)), base)
    size = 8
    while size <= c:
        wider = (ti // size) == (si // size)
        keep = jnp.logical_and(wider, jnp.logical_not(inside))
        acc = _each(lambda p, m: p - _dot(_dot(p, jnp.where(keep, m, 0.0)), p), acc, low)
        inside, size = wider, size * 2
    return acc


def _stack_rows(a, b):
    return jnp.concatenate([a, b], axis=0)


@jax.custom_vjp
def _split_rows(x):
    h = x.shape[0] // 2
    return x[:h], x[h:]


def _split_rows_fwd(x):
    return _split_rows(x), None


def _split_rows_bwd(_, g):
    return (jnp.concatenate(g, axis=0),)


_split_rows.defvjp(_split_rows_fwd, _split_rows_bwd)


def _masked_halves(stacked, top_mask, bottom_mask):
    halves = _each(_split_rows, stacked)
    return ([jnp.where(top_mask, t, 0.0) for t, _ in halves], [jnp.where(bottom_mask, b, 0.0) for _, b in halves])


@jax.custom_vjp
def _tri_inv_known(low, inv):
    return inv


def _tri_inv_known_fwd(low, inv):
    return inv, inv


def _tri_inv_known_bwd(inv, g):
    dlow = _each(lambda t, gg: -_dot(_dot(t, gg, _TN), t, _NT), inv, g)
    return dlow, _each(jnp.zeros_like, inv)


_tri_inv_known.defvjp(_tri_inv_known_fwd, _tri_inv_known_bwd)


def _tri_ones(c):
    return (lax.broadcasted_iota(jnp.int32, (c, c), 0) >= lax.broadcasted_iota(jnp.int32, (c, c), 1)).astype(F32)


def _cumsum_rows(lw):
    return _dot_split_b(_tri_ones(lw.shape[0]), lw, 3)


def _wkv_chunk(s0, r, lw, cum, k, v, kk, a, inv=None):
    c = r[0].shape[0]
    ti = lax.broadcasted_iota(jnp.int32, (c, c), 0)
    si = lax.broadcasted_iota(jnp.int32, (c, c), 1)
    incl, strict = ti >= si, ti > si
    eg = _each(jnp.exp, cum)
    egp = _each(lambda cs, x: jnp.exp(cs - x), cum, lw)
    ei = _each(lambda cs: jnp.exp(-cs), cum)
    rh, kkh, kt = _each(jnp.multiply, r, eg), _each(jnp.multiply, kk, egp), _each(jnp.multiply, k, ei)
    bt = _each(lambda p, q, e: (p * q) * e, a, kk, ei)
    both = _each(_stack_rows, kkh, rh)
    on_b, on_k, on_s = _each(_dot_nt, both, bt), _each(_dot_nt, both, kt), _each(_dot_nt, both, s0)
    lb, mb = _masked_halves(on_b, strict, incl)
    lk, mk = _masked_halves(on_k, strict, incl)
    on_s = _each(_split_rows, on_s)
    on_v = _each(lambda p, q, x: _split_rows(_dot(_stack_rows(p, q), x)), lk, mk, v)
    rhs = _each(lambda p, q: p[0] + q[0], on_s, on_v)
    inv = _tri_inv(lb) if inv is None else _tri_inv_known(lb, inv)
    u = _each(lambda t, x: -_dot(t, x), inv, rhs)
    y = _each(lambda p, m1, uu, q: p[1] + _dot(m1, uu) + q[1], on_s, mb, u, on_v)
    s1 = _each(lambda s, uu, x, b, kq, w: (s + _dot_tn(_stack_rows(uu, x), _stack_rows(b, kq)))
               * jnp.exp(jnp.sum(w, axis=0, keepdims=True)), s0, u, v, bt, kt, lw)
    return y, s1, inv


WKV_HEADS = 16
WKV_COLS = WKV_HEADS * HEAD
WKV_GROUPS = N_HEADS // WKV_HEADS


def _head_cols(ref):
    return [ref[:, h * HEAD:(h + 1) * HEAD] for h in range(ref.shape[1] // HEAD)]


def _wkv_specs(seq, rev):
    nc = seq // CHUNK

    def rows(col0):
        cb0 = col0 // WKV_COLS
        if rev:
            return pl.BlockSpec((CHUNK, WKV_COLS), lambda b, h, c: (b * nc + nc - 1 - c, cb0 + h))
        return pl.BlockSpec((CHUNK, WKV_COLS), lambda b, h, c: (b * nc + c, cb0 + h))

    if rev:
        st = pl.BlockSpec((1, 1, WKV_HEADS, HEAD, HEAD), lambda b, h, c: (b * WKV_GROUPS + h, nc - 1 - c, 0, 0, 0))
    else:
        st = pl.BlockSpec((1, 1, WKV_HEADS, HEAD, HEAD), lambda b, h, c: (b * WKV_GROUPS + h, c, 0, 0, 0))
    return rows, st


def _wkv_fwd(z_rkv, lw, k2, kk, a, seq):
    t = z_rkv.shape[0]
    nb, nc = t // seq, seq // CHUNK
    rows, st = _wkv_specs(seq, False)

    def body(r_ref, v_ref, lw_ref, k_ref, kk_ref, a_ref, y_ref, st_ref, inv_ref, s_scr, cum_scr):
        @pl.when(pl.program_id(2) == 0)
        def _():
            s_scr[...] = jnp.zeros_like(s_scr)

        cum_scr[...] = _cumsum_rows(lw_ref[...])
        s0 = [s_scr[h] for h in range(WKV_HEADS)]
        y, s1, inv = _wkv_chunk(s0, *[_head_cols(ref) for ref in (r_ref, lw_ref, cum_scr, k_ref, v_ref, kk_ref, a_ref)])
        for h in range(WKV_HEADS):
            st_ref[0, 0, h] = s0[h]
            inv_ref[0, 0, h] = inv[h]
            y_ref[:, h * HEAD:(h + 1) * HEAD] = y[h]
            s_scr[h] = s1[h]

    per_chunk = jax.ShapeDtypeStruct((nb * WKV_GROUPS, nc, WKV_HEADS, HEAD, HEAD), F32)
    return pl.pallas_call(
        body, name="wkv_fwd", grid=(nb, WKV_GROUPS, nc),
        in_specs=[rows(0), rows(2 * D), rows(0), rows(0), rows(0), rows(0)],
        out_specs=[rows(0), st, st],
        out_shape=[jax.ShapeDtypeStruct((t, D), F32), per_chunk, per_chunk],
        scratch_shapes=[pltpu.VMEM((WKV_HEADS, HEAD, HEAD), F32), pltpu.VMEM((CHUNK, WKV_COLS), F32)],
        compiler_params=_cp("parallel", "parallel", "arbitrary"),
    )(z_rkv, z_rkv, lw, k2, kk, a)


def _wkv_bwd(z_rkv, lw, k2, kk, a, states, invs, dy, seq):
    t = z_rkv.shape[0]
    nb, nc = t // seq, seq // CHUNK
    rows, st = _wkv_specs(seq, True)

    def body(r_ref, v_ref, lw_ref, k_ref, kk_ref, a_ref, st_ref, inv_ref, dy_ref,
             dr_ref, dlw_ref, dk_ref, dv_ref, dkk_ref, da_ref, ds_scr, cum_scr, dlw_scr):
        @pl.when(pl.program_id(2) == 0)
        def _():
            ds_scr[...] = jnp.zeros_like(ds_scr)

        cum_scr[...] = _cumsum_rows(lw_ref[...])
        s0 = [st_ref[0, 0, h] for h in range(WKV_HEADS)]
        inv = [inv_ref[0, 0, h] for h in range(WKV_HEADS)]
        _, vjp = jax.vjp(lambda *args: _wkv_chunk(*args, inv=inv)[:2],
                         s0, *[_head_cols(ref) for ref in (r_ref, lw_ref, cum_scr, k_ref, v_ref, kk_ref, a_ref)])
        ds0, dr, dlw, dcum, dk, dv, dkk, da = vjp(
            ([x.astype(F32) for x in _head_cols(dy_ref)], [ds_scr[h] for h in range(WKV_HEADS)]))
        for h in range(WKV_HEADS):
            sl = slice(h * HEAD, (h + 1) * HEAD)
            ds_scr[h] = ds0[h]
            dlw_scr[:, sl] = dlw[h]
            cum_scr[:, sl] = dcum[h]
            for ref, g in zip((dr_ref, dk_ref, dv_ref, dkk_ref, da_ref), (dr, dk, dv, dkk, da)):
                ref[:, sl] = g[h].astype(ref.dtype)
        dlw_ref[...] = (dlw_scr[...] + _dot_raw(_tri_ones(CHUNK), cum_scr[...], _TN)).astype(dlw_ref.dtype)

    return pl.pallas_call(
        body, name="wkv_bwd", grid=(nb, WKV_GROUPS, nc),
        in_specs=[rows(0), rows(2 * D), rows(0), rows(0), rows(0), rows(0), st, st, rows(0)],
        out_specs=[rows(0)] * 6,
        out_shape=[jax.ShapeDtypeStruct((t, D), BF16)] * 6,
        scratch_shapes=[pltpu.VMEM((WKV_HEADS, HEAD, HEAD), F32)] + [pltpu.VMEM((CHUNK, WKV_COLS), F32)] * 2,
        compiler_params=_cp("parallel", "parallel", "arbitrary"),
    )(z_rkv, z_rkv, lw, k2, kk, a, states, invs, dy)


def _softmax(s):
    e = jnp.exp(s - jnp.max(s, axis=-1, keepdims=True))
    return e * (1.0 / jnp.sum(e, axis=-1, keepdims=True))


ATT_FWD_HEADS = 16
ATT_HEADS = 8
ATT_COLS = ATT_HEADS * HEAD
ATT_GROUPS = N_HEADS // ATT_HEADS


def _attn_chunk(q, kb, vb, bias, valid):
    s = _each(lambda x, y, z: jnp.where(valid, _dot_nt(x * (HEAD ** -0.5), y) + z, MASK_VALUE), q, kb, bias)
    return _each(_dot, _each(_softmax, s), vb)


def _pad_fill(pad_ref, src_ref):
    pad_ref[0:LEFT, :] = jnp.zeros((LEFT, pad_ref.shape[1]), pad_ref.dtype)
    pad_ref[LEFT:, :] = src_ref[...].astype(pad_ref.dtype)


def _band_heads(pad_ref, start):
    return [pad_ref[pl.ds(start, BAND), h * HEAD:(h + 1) * HEAD] for h in range(pad_ref.shape[1] // HEAD)]


def _band_valid(c):
    return (c * CHUNK - LEFT + lax.broadcasted_iota(jnp.int32, (1, BAND), 1)) >= 0


def _bias_spec():
    return pl.BlockSpec((ATT_HEADS, CHUNK, BAND), lambda h, b, c: (h, 0, 0))


def _attn_fwd(proj, bias, seq):
    t = proj.shape[0]
    nb, nc = t // seq, seq // CHUNK
    heads = ATT_FWD_HEADS
    cols, groups = heads * HEAD, N_HEADS // heads
    cq = C_Q // cols

    def body(q_ref, k_ref, v_ref, b_ref, o_ref, kpad, vpad):
        c = pl.program_id(2)

        @pl.when(c == 0)
        def _():
            _pad_fill(kpad, k_ref)
            _pad_fill(vpad, v_ref)

        start = pl.multiple_of(c * CHUNK, CHUNK)
        o = _attn_chunk(_head_cols(q_ref), _band_heads(kpad, start), _band_heads(vpad, start),
                        [b_ref[h] for h in range(heads)], _band_valid(c))
        for h in range(heads):
            o_ref[:, h * HEAD:(h + 1) * HEAD] = o[h].astype(o_ref.dtype)

    return pl.pallas_call(
        body, name="attn_fwd", grid=(groups, nb, nc),
        in_specs=[pl.BlockSpec((CHUNK, cols), lambda h, b, c: (b * nc + c, cq + h)),
                  pl.BlockSpec((seq, cols), lambda h, b, c: (b, cq + groups + h)),
                  pl.BlockSpec((seq, cols), lambda h, b, c: (b, cq + 2 * groups + h)),
                  pl.BlockSpec((heads, CHUNK, BAND), lambda h, b, c: (h, 0, 0))],
        out_specs=pl.BlockSpec((CHUNK, cols), lambda h, b, c: (b * nc + c, h)),
        out_shape=jax.ShapeDtypeStruct((t, D), BF16),
        scratch_shapes=[pltpu.VMEM((seq + LEFT, cols), BF16)] * 2,
        compiler_params=_cp("parallel", "arbitrary", "arbitrary"),
    )(proj, proj, proj, bias)


def _attn_bwd(proj, bias, do, seq):
    t = proj.shape[0]
    nb, nc = t // seq, seq // CHUNK
    cq = C_Q // ATT_COLS

    def body(q_ref, k_ref, v_ref, b_ref, do_ref, dq_ref, dk_ref, dv_ref, db_ref, kpad, vpad, dkpad, dvpad):
        b, c = pl.program_id(1), pl.program_id(2)

        @pl.when(c == 0)
        def _():
            _pad_fill(kpad, k_ref)
            _pad_fill(vpad, v_ref)
            dkpad[...] = jnp.zeros_like(dkpad)
            dvpad[...] = jnp.zeros_like(dvpad)

        @pl.when(jnp.logical_and(b == 0, c == 0))
        def _():
            db_ref[...] = jnp.zeros_like(db_ref)

        start = pl.multiple_of(c * CHUNK, CHUNK)
        _, vjp = jax.vjp(functools.partial(_attn_chunk, valid=_band_valid(c)),
                         _head_cols(q_ref), _band_heads(kpad, start), _band_heads(vpad, start),
                         [b_ref[h] for h in range(ATT_HEADS)])
        dq, dkb, dvb, dbias = vjp([x.astype(F32) for x in _head_cols(do_ref)])
        for h in range(ATT_HEADS):
            sl = slice(h * HEAD, (h + 1) * HEAD)
            dq_ref[:, sl] = dq[h].astype(dq_ref.dtype)
            dkpad[pl.ds(start, BAND), sl] += dkb[h].astype(F32)
            dvpad[pl.ds(start, BAND), sl] += dvb[h].astype(F32)
            db_ref[h] += dbias[h]

        @pl.when(c == nc - 1)
        def _():
            dk_ref[...] = dkpad[LEFT:, :].astype(dk_ref.dtype)
            dv_ref[...] = dvpad[LEFT:, :].astype(dv_ref.dtype)

    kv_out = pl.BlockSpec((seq, ATT_COLS), lambda h, b, c: (b, h))
    return pl.pallas_call(
        body, name="attn_bwd", grid=(ATT_GROUPS, nb, nc),
        in_specs=[pl.BlockSpec((CHUNK, ATT_COLS), lambda h, b, c: (b * nc + c, cq + h)),
                  pl.BlockSpec((seq, ATT_COLS), lambda h, b, c: (b, cq + ATT_GROUPS + h)),
                  pl.BlockSpec((seq, ATT_COLS), lambda h, b, c: (b, cq + 2 * ATT_GROUPS + h)),
                  _bias_spec(),
                  pl.BlockSpec((CHUNK, ATT_COLS), lambda h, b, c: (b * nc + c, h))],
        out_specs=[pl.BlockSpec((CHUNK, ATT_COLS), lambda h, b, c: (b * nc + c, h)), kv_out, kv_out,
                   pl.BlockSpec((ATT_HEADS, CHUNK, BAND), lambda h, b, c: (h, 0, 0))],
        out_shape=[jax.ShapeDtypeStruct((t, D), BF16)] * 3 + [jax.ShapeDtypeStruct((N_HEADS, CHUNK, BAND), F32)],
        scratch_shapes=[pltpu.VMEM((seq + LEFT, ATT_COLS), BF16)] * 2 + [pltpu.VMEM((seq + LEFT, ATT_COLS), F32)] * 2,
        compiler_params=_cp("parallel", "arbitrary", "arbitrary"),
    )(proj, proj, proj, bias, do)


def _xattn_tile(q, k, v):
    s = _dot_nt(q, k) * ((MEM_WIDTH // MEM_HEADS) ** -0.5)
    return _dot(_softmax(s), v)


def _xattn_fwd(qm, kvm, seq, n_mem, tq=1024):
    t = qm.shape[0]
    tq = min(tq, seq)
    nb, nq = t // seq, seq // tq

    def body(q_ref, k_ref, v_ref, o_ref):
        o_ref[...] = _xattn_tile(q_ref[...], k_ref[...], v_ref[...]).astype(o_ref.dtype)

    return pl.pallas_call(
        body, name="xattn_fwd", grid=(nb, MEM_HEADS, nq),
        in_specs=[pl.BlockSpec((tq, LANE), lambda b, h, i: (b * nq + i, h)),
                  pl.BlockSpec((n_mem, LANE), lambda b, h, i: (b, h)),
                  pl.BlockSpec((n_mem, LANE), lambda b, h, i: (b, MEM_HEADS + h))],
        out_specs=pl.BlockSpec((tq, LANE), lambda b, h, i: (b * nq + i, h)),
        out_shape=jax.ShapeDtypeStruct((t, MEM_WIDTH), BF16),
        compiler_params=_cp("parallel", "parallel", "parallel"),
    )(qm, kvm, kvm)


def _xattn_bwd(qm, kvm, do, seq, n_mem, tq=1024):
    t = qm.shape[0]
    tq = min(tq, seq)
    nb, nq = t // seq, seq // tq

    def body(q_ref, k_ref, v_ref, do_ref, dq_ref, dkv_ref, dk_acc, dv_acc):
        i = pl.program_id(2)

        @pl.when(i == 0)
        def _():
            dk_acc[...] = jnp.zeros_like(dk_acc)
            dv_acc[...] = jnp.zeros_like(dv_acc)

        _, vjp = jax.vjp(_xattn_tile, q_ref[...], k_ref[...], v_ref[...])
        dq, dk, dv = vjp(do_ref[...].astype(F32))
        dq_ref[...] = dq.astype(dq_ref.dtype)
        dk_acc[...] += dk
        dv_acc[...] += dv

        @pl.when(i == nq - 1)
        def _():
            dkv_ref[0] = dk_acc[...].astype(dkv_ref.dtype)
            dkv_ref[1] = dv_acc[...].astype(dkv_ref.dtype)

    dq, dkv = pl.pallas_call(
        body, name="xattn_bwd", grid=(nb, MEM_HEADS, nq),
        in_specs=[pl.BlockSpec((tq, LANE), lambda b, h, i: (b * nq + i, h)),
                  pl.BlockSpec((n_mem, LANE), lambda b, h, i: (b, h)),
                  pl.BlockSpec((n_mem, LANE), lambda b, h, i: (b, MEM_HEADS + h)),
                  pl.BlockSpec((tq, LANE), lambda b, h, i: (b * nq + i, h))],
        out_specs=[pl.BlockSpec((tq, LANE), lambda b, h, i: (b * nq + i, h)),
                   pl.BlockSpec((2, n_mem, LANE), lambda b, h, i: (0, b, h))],
        out_shape=[jax.ShapeDtypeStruct((t, MEM_WIDTH), BF16), jax.ShapeDtypeStruct((2, nb * n_mem, MEM_WIDTH), BF16)],
        scratch_shapes=[pltpu.VMEM((n_mem, LANE), F32)] * 2,
        compiler_params=_cp("parallel", "parallel", "arbitrary"),
    )(qm, kvm, kvm, do)
    return dq, jnp.concatenate([dkv[0], dkv[1]], axis=1)


def _loss_head(x, u, g_post, target, tm=512):
    t, d = x.shape
    tm = min(tm, t)

    def tile_loss(xv, uv, gv, tv):
        diff = _fn_res(xv, uv, gv)[0] - tv
        return 0.5 * jnp.sum(jnp.mean(diff * diff, axis=-1, keepdims=True), axis=0, keepdims=True)

    def body(x_ref, u_ref, g_ref, t_ref, l_ref, dx_ref, du_ref, dg_ref):
        @pl.when(pl.program_id(0) == 0)
        def _():
            l_ref[...] = jnp.zeros_like(l_ref)
            dg_ref[...] = jnp.zeros_like(dg_ref)

        tv = t_ref[...]
        part, vjp = jax.vjp(lambda xv, uv, gv: tile_loss(xv, uv, gv, tv), x_ref[...], u_ref[...], g_ref[...])
        dx, du, dg = vjp(jnp.ones((1, 1), F32))
        l_ref[...] += part
        dx_ref[...] = dx
        du_ref[...] = du.astype(du_ref.dtype)
        dg_ref[...] += dg

    rows = pl.BlockSpec((tm, d), lambda i: (i, 0))
    vec = pl.BlockSpec((1, d), lambda i: (0, 0))
    return pl.pallas_call(
        body, name="loss_head", grid=(t // tm,),
        in_specs=[rows, rows, vec, rows],
        out_specs=[pl.BlockSpec((8, LANE), lambda i: (0, 0)), rows, rows, vec],
        out_shape=[jax.ShapeDtypeStruct((8, LANE), F32), jax.ShapeDtypeStruct((t, d), F32),
                   jax.ShapeDtypeStruct((t, d), BF16), jax.ShapeDtypeStruct((1, d), F32)],
        compiler_params=_cp("arbitrary"),
    )(x, u, g_post, target)


def _mesh_pos():
    return lax.axis_index("x"), lax.axis_index("y"), lax.axis_index("c")


def _peer(pos, d):
    x, y, c = pos
    return ((1 - x) if d & 4 else x, (1 - y) if d & 2 else y, (1 - c) if d & 1 else c)


def _flat(pos):
    return 4 * pos[0] + 2 * pos[1] + pos[2]


def _exchange(arrays, scatter, *, name):
    n = len(arrays)
    shapes = [a.shape[1:] if scatter else a.shape for a in arrays]

    def body(*refs):
        ins, outs = refs[:n], refs[n:2 * n]
        send, recv, loc = refs[2 * n:]
        pos = _mesh_pos()
        me = _flat(pos)
        pending = []
        for i in range(n):
            own = pltpu.make_async_copy(ins[i].at[me] if scatter else ins[i], outs[i].at[me], loc.at[i])
            own.start()
            pending.append(own)
            for d in range(1, N_DEV):
                peer = _peer(pos, d)
                src = ins[i].at[_flat(peer)] if scatter else ins[i]
                out_cp = pltpu.make_async_remote_copy(
                    src_ref=src, dst_ref=outs[i].at[me], send_sem=send.at[i, d - 1], recv_sem=recv.at[i, d - 1],
                    device_id=peer, device_id_type=pl.DeviceIdType.MESH)
                out_cp.start()
                pending.append(out_cp)
        for i in range(n):
            own = pending[i * N_DEV]
            for d in range(1, N_DEV):
                peer = _peer(pos, d)
                src = ins[i].at[_flat(peer)] if scatter else ins[i]
                pending[i * N_DEV + d].wait_send()
                pltpu.make_async_remote_copy(
                    src_ref=src, dst_ref=outs[i].at[_flat(peer)], send_sem=send.at[i, d - 1], recv_sem=recv.at[i, d - 1],
                    device_id=peer, device_id_type=pl.DeviceIdType.MESH).wait_recv()
            own.wait()

    hbm = pl.BlockSpec(memory_space=pltpu.HBM)
    return pl.pallas_call(
        body, name=name,
        in_specs=[hbm] * n, out_specs=[hbm] * n,
        out_shape=[jax.ShapeDtypeStruct((N_DEV,) + tuple(s), a.dtype) for s, a in zip(shapes, arrays)],
        scratch_shapes=[pltpu.SemaphoreType.DMA((n, N_DEV - 1)), pltpu.SemaphoreType.DMA((n, N_DEV - 1)),
                        pltpu.SemaphoreType.DMA((n,))],
    )(*arrays)


_HBM = pl.BlockSpec(memory_space=pltpu.HBM)
_SEM = pl.BlockSpec(memory_space=pltpu.SEMAPHORE)
_DATAFLOW = pltpu.SideEffectType.DATAFLOW_SIDE_EFFECTING


_ALL_PEERS = tuple(range(1, N_DEV))
_SIBLING_AND_SAME_CORE = (1, 2, 4, 6)


def _remote_copies(ins, lands, send, recv, scatter, dists):
    pos = _mesh_pos()
    me = _flat(pos)
    out = []
    for i in range(len(ins)):
        for j, d in enumerate(dists):
            peer = _peer(pos, d)
            src = ins[i].at[_flat(peer)] if scatter else ins[i]
            pair = i * len(dists) + j
            sems = dict(send_sem=send.at[pair], recv_sem=recv.at[pair], device_id=peer,
                        device_id_type=pl.DeviceIdType.MESH)
            out.append((pltpu.make_async_remote_copy(src_ref=src, dst_ref=lands[i].at[me], **sems),
                        pltpu.make_async_remote_copy(src_ref=src, dst_ref=lands[i].at[_flat(peer)], **sems)))
    return out


def _exchange_start(arrays, scatter, after, *, name, dists=_ALL_PEERS):
    n = len(arrays)
    shapes = [a.shape[1:] if scatter else a.shape for a in arrays]
    lands = [pltpu.with_memory_space_constraint(lax.empty((N_DEV,) + tuple(s), a.dtype), pltpu.HBM)
             for s, a in zip(shapes, arrays)]
    srcs = [pltpu.with_memory_space_constraint(a, pltpu.HBM) for a in arrays]

    def body(*refs):
        ins, land_refs = refs[:n], refs[n:2 * n]
        send, recv, token = refs[2 * n + 1], refs[2 * n + 2], refs[-1]
        for going, _ in _remote_copies(ins, land_refs, send, recv, scatter, dists):
            going.start()
        token[...] = jnp.zeros_like(token)

    sems = pltpu.SemaphoreType.DMA((n * len(dists),))
    res = pl.pallas_call(
        body, name=name,
        out_shape=(sems, sems, *[pltpu.HBM(a.shape, a.dtype) for a in srcs + lands], jax.ShapeDtypeStruct((8, LANE), F32)),
        in_specs=[_HBM] * (2 * n) + [pl.BlockSpec(memory_space=pl.ANY)],
        out_specs=(_SEM, _SEM, *[_HBM] * (2 * n), pl.BlockSpec(memory_space=pltpu.VMEM)),
        input_output_aliases={i: 2 + i for i in range(2 * n)},
        compiler_params=pltpu.CompilerParams(has_side_effects=_DATAFLOW),
    )(*srcs, *lands, after)
    return (n, scatter, dists, res[0], res[1], list(res[2:2 + 2 * n])), res[-1]


def _exchange_wait(handle, after, own, *, name):
    n, scatter, dists, send, recv, thru = handle

    def body(*refs):
        ins, land_refs = refs[:n], refs[n:2 * n]
        for going, coming in _remote_copies(ins, land_refs, refs[2 * n], refs[2 * n + 1], scatter, dists):
            going.wait_send()
            coming.wait_recv()

    res = pl.pallas_call(
        body, name=name,
        out_shape=tuple(pltpu.HBM(a.shape, a.dtype) for a in thru),
        in_specs=[_HBM] * (2 * n) + [_SEM, _SEM] + [pl.BlockSpec(memory_space=pl.ANY)] * len(after),
        out_specs=tuple([_HBM] * (2 * n)),
        input_output_aliases={i: i for i in range(2 * n)},
        compiler_params=pltpu.CompilerParams(has_side_effects=_DATAFLOW),
    )(*thru, send, recv, *after)
    me = _flat(_mesh_pos())
    return [lax.dynamic_update_slice_in_dim(land, o[None].astype(land.dtype), me, 0) for land, o in zip(res[n:], own)]


_OTHER_CHIPS = (2, 4, 6)


def _relay_to_sibling(gathered, *, name):
    n, k = len(gathered), len(_OTHER_CHIPS)

    def body(*refs):
        ins, outs = refs[:n], refs[n:2 * n]
        send, recv = refs[2 * n:]
        pos = _mesh_pos()
        copies = []
        for i in range(n):
            for j, d in enumerate(_OTHER_CHIPS):
                cp = pltpu.make_async_remote_copy(
                    src_ref=ins[i].at[_flat(_peer(pos, d))], dst_ref=outs[i].at[j],
                    send_sem=send.at[i * k + j], recv_sem=recv.at[i * k + j],
                    device_id=_peer(pos, 1), device_id_type=pl.DeviceIdType.MESH)
                cp.start()
                copies.append(cp)
        for cp in copies:
            cp.wait()

    return pl.pallas_call(
        body, name=name, in_specs=[_HBM] * n, out_specs=[_HBM] * n,
        out_shape=[jax.ShapeDtypeStruct((k,) + g.shape[1:], g.dtype) for g in gathered],
        scratch_shapes=[pltpu.SemaphoreType.DMA((n * k,)), pltpu.SemaphoreType.DMA((n * k,))],
    )(*gathered)


def _adamw(parts, w, m, v, *, name, tr=128, after=None):
    r, c = w.shape
    align = 8 * 4 // parts.dtype.itemsize
    row_tiles = [d for d in range(align, min(tr, r) + 1, align) if r % d == 0]
    tr, tc = (max(row_tiles), c) if row_tiles else (r, LANE)
    assert c % tc == 0
    n_after = 0 if after is None else 1

    def body(p_ref, w_ref, m_ref, v_ref, *rest):
        g_ref, d_ref, nm_ref, nv_ref = rest[n_after:]
        g = p_ref[0].astype(F32)
        for j in range(1, N_DEV):
            g = g + p_ref[j].astype(F32)
        m2 = ADAM_B1 * m_ref[...] + (1.0 - ADAM_B1) * g
        v2 = ADAM_B2 * v_ref[...] + (1.0 - ADAM_B2) * (g * g)
        m_hat = m2 / (1.0 - ADAM_B1 ** ADAM_STEP)
        v_hat = v2 / (1.0 - ADAM_B2 ** ADAM_STEP)
        g_ref[...] = g
        d_ref[...] = -ADAM_LR * (m_hat / (jnp.sqrt(v_hat) + ADAM_EPS) + ADAM_WD * w_ref[...])
        nm_ref[...] = m2
        nv_ref[...] = v2

    spec = pl.BlockSpec((tr, tc), lambda i, j: (i, j))
    return pl.pallas_call(
        body, name=name, grid=(r // tr, c // tc),
        in_specs=[pl.BlockSpec((N_DEV, tr, tc), lambda i, j: (0, i, j)), spec, spec, spec]
        + [pl.BlockSpec(memory_space=pl.ANY)] * n_after,
        out_specs=[spec] * 4, out_shape=[jax.ShapeDtypeStruct((r, c), F32)] * 4,
        compiler_params=_cp("parallel", "parallel"),
    )(parts, w, m, v, *([] if after is None else [after]))


def _cols_to_full(g):
    return jnp.transpose(g, (1, 0, 2)).reshape(g.shape[1], N_DEV * g.shape[2])


def _full_to_cols(w):
    r, c = w.shape
    return jnp.transpose(w.reshape(r, N_DEV, c // N_DEV), (1, 0, 2))


def _cut(a, lo, hi, axis):
    return lax.slice_in_dim(a, lo, hi, axis=axis)


def _pad_to(a, size, axis):
    pads = [(0, 0)] * a.ndim
    pads[axis] = (0, size - a.shape[axis])
    return jnp.pad(a, pads)


def _pad_lora(w, axis=1):
    return jnp.concatenate([
        _pad_to(_cut(w, 0, LORA_W, axis), 128, axis), _pad_to(_cut(w, LORA_W, LORA_W + LORA_A, axis), 128, axis),
        _pad_to(_cut(w, LORA_W + LORA_A, w.shape[axis], axis), 256, axis)], axis=axis)


def _unpad_lora(wp, axis=1):
    return jnp.concatenate([_cut(wp, 0, LORA_W, axis), _cut(wp, 128, 128 + LORA_A, axis),
                            _cut(wp, 256, 256 + LORA_G, axis)], axis=axis)


def _permute_in(w, axis):
    rk = 3 * D
    lo = rk + LORA_W + LORA_A + LORA_G
    return jnp.concatenate([_cut(w, 0, rk, axis), _cut(w, lo, w.shape[axis], axis), _pad_lora(_cut(w, rk, lo, axis), axis)],
                           axis=axis)


def _unpermute_in(wp, axis):
    return jnp.concatenate([_cut(wp, 0, 3 * D, axis), _unpad_lora(_cut(wp, C_LORA, P_WIDTH, axis), axis),
                            _cut(wp, 3 * D, C_LORA, axis)], axis=axis)


def _rel_index():
    dist = jnp.arange(CHUNK)[:, None] - jnp.arange(BAND)[None, :] + LEFT
    return (jnp.minimum(dist, REL_CLIP) + (CHUNK - 1)).reshape(-1)


def _local_step(x, mem, target, wt, seq, n_mem, comm):
    t = x.shape[0]
    row = lambda a: a.reshape(1, -1).astype(F32)
    g_pre_mix, g_post_mix = row(wt["g_pre_mix"]), row(wt["g_post_mix"])
    g_pre_cross, g_post_cross, g_mem = row(wt["g_pre_cross"]), row(wt["g_post_cross"]), row(wt["g_mem"])
    g_pre_ffn, g_post_ffn = row(wt["g_pre_ffn"]), row(wt["g_post_ffn"])
    mix = row(wt["shift_mix"])
    mix_rkv, mix_lora = mix[:, :3 * D], _pad_lora(mix[:, 3 * D:])
    decay_base, iclr_base = row(wt["decay_base"]), row(wt["iclr_base"])
    kns, kis = row(wt["key_norm_scale"]), row(wt["key_iclr_scale"])
    lnx_w, lnx_b, bonus = row(wt["lnx_w"]), row(wt["lnx_b"]), row(wt["bonus_scale"])
    e_dh = (jnp.arange(D)[:, None] // HEAD == jnp.arange(N_HEADS)[None, :]).astype(F32)
    e_hd = e_dh.T
    onehot = (jnp.arange(REL_TABLE)[:, None] == _rel_index()[None, :]).astype(BF16)

    begun = comm.begun
    (h1,) = _rowwise(_fn_pre, [_win(x)], [g_pre_mix], [(D, BF16)], name="pre_mix", tm=512, after=begun)
    (mn,) = _rowwise(_fn_pre, [_win(mem)], [g_mem], [(D, BF16)], name="pre_mem", tm=512, after=begun)
    bias = _mm(wt["rel_bias"].astype(F32), onehot, name="mm_bias", split_a=3, after=begun).reshape(N_HEADS, CHUNK, BAND)
    wt = {**wt, **comm.first_weights([h1, mn, bias])}
    w_in = wt["w_in_p"]
    d_up = jnp.pad(wt["decay_up"].astype(F32), ((0, 128 - LORA_W), (0, 0)))
    i_up = jnp.pad(wt["iclr_up"].astype(F32), ((0, 128 - LORA_A), (0, 0)))
    g_up = jnp.pad(wt["gate_up"].astype(F32), ((0, 256 - LORA_G), (0, 0)))
    proj = _mm(h1, w_in, tb=True, name="mm_in", after=comm.first_token)
    z_rkv = _shift_fwd(proj, 0, 3 * D, mix_rkv, seq, name="shift_rkv")
    z_lora = _shift_fwd(proj, C_LORA, 512, mix_lora, seq, name="shift_lora")
    prep_rows = [_win(z_rkv, D, D), _win(z_lora, 0, 128), _win(z_lora, 128, 128), _win(z_lora, 256, 256)]
    prep_params = [decay_base, d_up, iclr_base, i_up, g_up, kns, kis, e_hd, e_dh]
    lw, k2, kk, a, g = _rowwise(_fn_prep, prep_rows, prep_params, [(D, F32)] * 5, name="rwkv_prep", tm=256)
    y, states, invs = _wkv_fwd(z_rkv, lw, k2, kk, a, seq)
    post_rows = [_win(y), _win(z_rkv, 0, D), _win(k2), _win(z_rkv, 2 * D, D), _win(g)]
    post_params = [lnx_w, lnx_b, bonus, e_hd, e_dh]
    (y_a,) = _rowwise(_fn_post, post_rows, post_params, [(D, BF16)], name="rwkv_post", tm=256)
    y_b = _attn_fwd(proj, bias, seq)
    wt = {**wt, **comm.late_weights(y_b)}
    ya_p = _mm(y_a, wt["w_branch_a"], name="mm_a")
    yb_p = _mm(y_b, wt["w_branch_b"], name="mm_b")
    mix_rows = [_win(proj, C_GA, D), _win(proj, C_GA + D, D), _win(ya_p), _win(yb_p)]
    (mixed,) = _rowwise(_fn_mix, mix_rows, [], [(D, BF16)], name="gate_mix", tm=512)
    mo = _mm(mixed, wt["w_out"], name="mm_out")
    x1, h2 = _rowwise(_fn_res_pre, [_win(x), _win(mo)], [g_post_mix, g_pre_cross], [(D, F32), (D, BF16)],
                      name="res_mix", tm=512)
    qm = _mm(h2, wt["w_q_mem"], name="mm_q", out_dtype=BF16)
    kvm = _mm(mn, wt["w_kv_mem"], name="mm_kv", out_dtype=BF16)
    om = _xattn_fwd(qm, kvm, seq, n_mem)
    co = _mm(om, wt["w_o_mem"], name="mm_o")
    x2, h3 = _rowwise(_fn_res_pre, [_win(x1), _win(co)], [g_post_cross, g_pre_ffn], [(D, F32), (D, BF16)],
                      name="res_cross", tm=512)
    gu = _mm(h3, wt["w_ffn_in"], tb=True, name="mm_ffn_in", out_dtype=BF16)
    (act,) = _rowwise(_fn_swiglu, [_win(gu, 0, FFN), _win(gu, FFN, FFN)], [], [(FFN, BF16)], name="swiglu", tm=512)
    ff = _mm(act, wt["w_ffn_out"], name="mm_ffn_out")

    gw = {}
    loss, dx2, dff, gw["g_post_ffn"] = _loss_head(x2, ff, g_post_ffn, target)
    dact = _mm(dff, wt["w_ffn_out"], tb=True, name="mm_ffn_out_dx", out_dtype=BF16)
    gw["w_ffn_out"] = _mm(act, dff, ta=True, name="mm_ffn_out_dw", out_dtype=BF16)
    (dgu,), _ = _rowwise_bwd(_fn_swiglu, [_win(gu, 0, FFN), _win(gu, FFN, FFN)], [], 0, [[dact]],
                             name="swiglu_bwd", tm=512, row_grad=[BF16, BF16], packed=True)
    dh3 = _mm(dgu, wt["w_ffn_in"], name="mm_ffn_in_dx", out_dtype=BF16)
    gw["w_ffn_in"] = _mm(dgu, h3, ta=True, name="mm_ffn_in_dw", out_dtype=BF16)
    (dx1, dco), (gw["g_post_cross"], gw["g_pre_ffn"]) = _rowwise_bwd(
        _fn_res_pre, [_win(x1), _win(co)], [g_post_cross, g_pre_ffn], 0, [[dx2], [dh3]],
        name="res_cross_bwd", tm=512, row_grad=[F32, BF16])
    dom = _mm(dco, wt["w_o_mem"], tb=True, name="mm_o_dx", out_dtype=BF16)
    gw["w_o_mem"] = _mm(om, dco, ta=True, name="mm_o_dw", out_dtype=BF16)
    dqm, dkvm = _xattn_bwd(qm, kvm, dom, seq, n_mem)
    dh2 = _mm(dqm, wt["w_q_mem"], tb=True, name="mm_q_dx", out_dtype=BF16)
    gw["w_q_mem"] = _mm(h2, dqm, ta=True, name="mm_q_dw", out_dtype=BF16)
    dmn = _mm(dkvm, wt["w_kv_mem"], tb=True, name="mm_kv_dx", out_dtype=BF16)
    gw["w_kv_mem"] = _mm(mn, dkvm, ta=True, name="mm_kv_dw", out_dtype=BF16)
    _, (gw["g_mem"],) = _rowwise_bwd(_fn_pre, [_win(mem)], [g_mem], 0, [[dmn]], name="pre_mem_bwd", tm=256,
                                     row_grad=[None])
    (dx0, dmo), (gw["g_post_mix"], gw["g_pre_cross"]) = _rowwise_bwd(
        _fn_res_pre, [_win(x), _win(mo)], [g_post_mix, g_pre_cross], 0, [[dx1], [dh2]],
        name="res_mix_bwd", tm=512, row_grad=[F32, BF16])
    dmixed = _mm(dmo, wt["w_out"], tb=True, name="mm_out_dx", out_dtype=BF16)
    gw["w_out"] = _mm(mixed, dmo, ta=True, name="mm_out_dw", out_dtype=BF16)
    (dzga, dzgb, dya_p, dyb_p), _ = _rowwise_bwd(_fn_mix, mix_rows, [], 0, [[dmixed]], name="gate_mix_bwd", tm=512,
                                                 row_grad=[BF16] * 4)
    gw["w_branch_a"] = _mm(y_a, dya_p, ta=True, name="mm_a_dw", out_dtype=BF16)
    gw["w_branch_b"] = _mm(y_b, dyb_p, ta=True, name="mm_b_dw", out_dtype=BF16)
    token = comm.send_early(gw)
    dy_a = _mm(dya_p, wt["w_branch_a"], tb=True, name="mm_a_dx", out_dtype=BF16, after=token)
    dy_b = _mm(dyb_p, wt["w_branch_b"], tb=True, name="mm_b_dx", out_dtype=BF16, after=token)
    dq, dk, dv, dbias = _attn_bwd(proj, bias, dy_b, seq)
    gw["rel_bias"] = _mm(dbias.reshape(N_HEADS, CHUNK * BAND), onehot, tb=True, name="mm_bias_dw", split_a=2)
    (dy, dr_p, dk2_p, dv_p, dg), (gw["lnx_w"], gw["lnx_b"], gw["bonus_scale"]) = _rowwise_bwd(
        _fn_post, post_rows, post_params, 2, [[dy_a]], name="rwkv_post_bwd", tm=512, row_grad=[BF16] * 5)
    dr_s, dlw, dk2_s, dv_s, dkk, da = _wkv_bwd(z_rkv, lw, k2, kk, a, states, invs, dy, seq)
    (dzk, dzw, dza, dzg), pg = _rowwise_bwd(
        _fn_prep, prep_rows, prep_params, 2, [[dlw], [dk2_p, dk2_s], [dkk], [da], [dg]],
        name="rwkv_prep_bwd", tm=512, row_grad=[BF16] * 4)
    gw["decay_base"], gd_up, gw["iclr_base"], gi_up, gg_up, gw["key_norm_scale"], gw["key_iclr_scale"] = pg
    gw["decay_up"], gw["iclr_up"], gw["gate_up"] = gd_up[:LORA_W], gi_up[:LORA_A], gg_up[:LORA_G]
    dp_r, gmix_r = _shift_bwd(proj, 0, D, mix_rkv[:, :D], [dr_p, dr_s], seq, name="shift_r_bwd")
    dp_k, gmix_k = _shift_bwd(proj, D, D, mix_rkv[:, D:2 * D], [dzk], seq, name="shift_k_bwd")
    dp_v, gmix_v = _shift_bwd(proj, 2 * D, D, mix_rkv[:, 2 * D:], [dv_p, dv_s], seq, name="shift_v_bwd")
    dp_lora, gmix_lora = _shift_bwd(proj, C_LORA, 512, mix_lora, [jnp.concatenate([dzw, dza, dzg], axis=1)], seq,
                                    name="shift_lora_bwd")
    gw["shift_mix"] = jnp.concatenate([gmix_r, gmix_k, gmix_v, _unpad_lora(gmix_lora)], axis=1)
    dproj = [dp_r, dp_k, dp_v, dq, dk, dv, dzga, dzgb, dp_lora]
    gw["w_in_p"] = _mm_cat_tn(dproj, h1, name="mm_in_dw", after=gw["rel_bias"])
    token = comm.send_late(gw)
    dh1 = _mm_cat_nn(dproj, w_in, name="mm_in_dx", after=token)
    (grad_x,), (gw["g_pre_mix"],) = _rowwise_bwd(_fn_pre, [_win(x)], [g_pre_mix], 0, [[dh1]], name="pre_mix_bwd",
                                                 tm=512, row_grad=[F32], add_to={0: dx0})
    return loss, grad_x, gw


_COL_SHARDED = ("w_in", "decay_up", "iclr_up", "gate_up", "w_o_mem", "w_ffn_in")
_ROW_SHARDED = ("w_branch_a", "w_branch_b", "w_out", "w_q_mem", "w_kv_mem", "w_ffn_out")
_TRANSPOSED = ("w_in", "w_ffn_in")
_FIRST = ("w_in", "decay_up", "iclr_up", "gate_up")
_REST = ("w_o_mem", "w_ffn_in", "w_branch_a", "w_branch_b", "w_out", "w_q_mem", "w_kv_mem", "w_ffn_out")
_REPLICATED = ("g_pre_mix", "g_post_mix", "shift_mix", "decay_base", "iclr_base", "key_norm_scale", "key_iclr_scale",
               "bonus_scale", "lnx_w", "lnx_b", "rel_bias", "g_pre_cross", "g_post_cross", "g_mem", "g_pre_ffn",
               "g_post_ffn")
_WEIGHTS = ("g_pre_mix", "g_post_mix", "w_in", "shift_mix", "decay_base", "decay_up", "iclr_base", "iclr_up", "gate_up",
            "key_norm_scale", "key_iclr_scale", "bonus_scale", "lnx_w", "lnx_b", "rel_bias", "w_branch_a", "w_branch_b",
            "w_out", "g_pre_cross", "g_post_cross", "g_mem", "w_q_mem", "w_kv_mem", "w_o_mem", "g_pre_ffn", "g_post_ffn",
            "w_ffn_in", "w_ffn_out")
_PACK_ROWS = 8 * ((sum({"shift_mix": 3360, "bonus_scale": 1024, "rel_bias": 3072}.get(n, D) for n in _REPLICATED)
                   + 1 + 8 * LANE - 1) // (8 * LANE))


def _pack(vals):
    flat = jnp.concatenate([v.reshape(-1).astype(F32) for v in vals])
    return jnp.pad(flat, (0, _PACK_ROWS * LANE - flat.shape[0])).reshape(_PACK_ROWS, LANE)


def _unpack(packed, shapes):
    flat, out, pos = packed.reshape(-1), [], 0
    for s in shapes:
        n = math.prod(s)
        out.append(flat[pos:pos + n].reshape(s))
        pos += n
    return out


def _step(args, seq, n_mem):
    names = ("x", "mem") + _WEIGHTS + ("loss_target",) + tuple("m_" + n for n in _WEIGHTS) + tuple("v_" + n for n in _WEIGHTS)
    given = dict(zip(names, args))
    nb = given["x"].shape[0]
    x = given["x"].reshape(nb * seq, D)
    mem = given["mem"].reshape(nb * n_mem, D)
    target = given["loss_target"].reshape(nb * seq, D)
    def local(name, prefix=""):
        a = given[prefix + name][0]
        return a.T if name in _TRANSPOSED else a

    shard = {n: local(n) for n in _COL_SHARDED + _ROW_SHARDED}
    stacked = _ROW_SHARDED + _TRANSPOSED
    out = {}

    def wire(name):
        return shard[name].astype(BF16)

    def full(name, g):
        return g.reshape(-1, g.shape[-1]) if name in stacked else _cols_to_full(g)

    def blocks_of(name, g):
        return (g.reshape((N_DEV,) + shard[name].shape) if name in stacked else _full_to_cols(g)).astype(BF16)

    def update(names, landed, after=None):
        done = []
        for n, parts in zip(names, landed):
            res = _adamw(parts, shard[n], local(n, "m_"), local(n, "v_"), name="adamw_" + n, after=after)
            for kind, r in zip(("grad_", "delta_", "new_m_", "new_v_"), res):
                out[kind + n] = (r.T if n in _TRANSPOSED else r)[None]
            done.append(res[0])
        return done


    class Exchanges:
        def __init__(self):
            srcs = [wire(n) for n in _FIRST]
            self.first, self.begun = _exchange_start(srcs, False, srcs[0], name="gather_first_start",
                                                     dists=_SIBLING_AND_SAME_CORE)

        def first_weights(self, after):
            got = _exchange_wait(self.first, after, [wire(n) for n in _FIRST], name="gather_first_wait")
            relayed = _relay_to_sibling(got, name="gather_first_relay")
            pos = _mesh_pos()
            for j, d in enumerate(_OTHER_CHIPS):
                slot = _flat(_peer(pos, d | 1))
                got = [lax.dynamic_update_slice_in_dim(g, r[j][None], slot, 0) for g, r in zip(got, relayed)]
            self.rest, self.first_token = _exchange_start(
                [wire(n) for n in _REST], False, got[0], name="gather_rest_start")
            first = {n: full(n, g) for n, g in zip(_FIRST, got)}
            first["w_in_p"] = _permute_in(first.pop("w_in"), 0)
            return first

        def late_weights(self, after):
            got = _exchange_wait(self.rest, [after], [wire(n) for n in _REST], name="gather_rest_wait")
            return {n: full(n, g) for n, g in zip(_REST, got)}

        def send_early(self, gw):
            self.early_blocks = [blocks_of(n, gw[n]) for n in _REST]
            self.early, token = _exchange_start(self.early_blocks, True, self.early_blocks[-1], name="scatter_rest_start")
            return token

        def send_late(self, gw):
            me = _flat(_mesh_pos())
            own = [lax.dynamic_index_in_dim(b, me, 0, keepdims=False) for b in self.early_blocks]
            landed = _exchange_wait(self.early, [gw["w_in_p"]], own, name="scatter_rest_wait")
            grads = {**gw, "w_in": _unpermute_in(gw["w_in_p"], 0)}
            self.late_blocks = [blocks_of(n, grads[n]) for n in _FIRST]
            self.late, token = _exchange_start(self.late_blocks, True, landed[0], name="scatter_first_start")
            self.updated = update(_REST, landed, after=token)
            return token

        def finish(self, after):
            me = _flat(_mesh_pos())
            own = [lax.dynamic_index_in_dim(b, me, 0, keepdims=False) for b in self.late_blocks]
            update(_FIRST, _exchange_wait(self.late, [*after, *self.updated], own, name="scatter_first_wait"))

    comm = Exchanges()
    wt = {n: given[n][0] for n in _REPLICATED}
    loss_tile, grad_x, gw = _local_step(x, mem, target, wt, seq, n_mem, comm)
    rep_shapes = [given[n].shape for n in _REPLICATED]
    packed, _ = lax.optimization_barrier((_pack([gw[n] for n in _REPLICATED] + [loss_tile[0, 0]]), tuple(comm.updated)))
    small = _exchange([packed], False, name="gather_small")[0]
    zero = jnp.zeros((), F32)
    res = _adamw(small, *[_pack([given[p + n] for n in _REPLICATED] + [zero]) for p in ("", "m_", "v_")],
                 name="adamw_small", tr=_PACK_ROWS)
    for kind, r in zip(("grad_", "delta_", "new_m_", "new_v_"), res):
        for n, val in zip(_REPLICATED, _unpack(r, rep_shapes)):
            out[kind + n] = val
    loss = res[0].reshape(-1)[sum(math.prod(s) for s in rep_shapes)]
    comm.finish([grad_x, res[0]])
    grad_x = grad_x.reshape(nb, seq, D)
    return (loss, grad_x, *[out[k + n] for k in ("grad_", "delta_", "new_m_", "new_v_") for n in _WEIGHTS])


def kernel(x, mem, g_pre_mix, g_post_mix, w_in, shift_mix, decay_base, decay_up, iclr_base, iclr_up, gate_up, key_norm_scale, key_iclr_scale, bonus_scale, lnx_w, lnx_b, rel_bias, w_branch_a, w_branch_b, w_out, g_pre_cross, g_post_cross, g_mem, w_q_mem, w_kv_mem, w_o_mem, g_pre_ffn, g_post_ffn, w_ffn_in, w_ffn_out, loss_target, m_g_pre_mix, m_g_post_mix, m_w_in, m_shift_mix, m_decay_base, m_decay_up, m_iclr_base, m_iclr_up, m_gate_up, m_key_norm_scale, m_key_iclr_scale, m_bonus_scale, m_lnx_w, m_lnx_b, m_rel_bias, m_w_branch_a, m_w_branch_b, m_w_out, m_g_pre_cross, m_g_post_cross, m_g_mem, m_w_q_mem, m_w_kv_mem, m_w_o_mem, m_g_pre_ffn, m_g_post_ffn, m_w_ffn_in, m_w_ffn_out, v_g_pre_mix, v_g_post_mix, v_w_in, v_shift_mix, v_decay_base, v_decay_up, v_iclr_base, v_iclr_up, v_gate_up, v_key_norm_scale, v_key_iclr_scale, v_bonus_scale, v_lnx_w, v_lnx_b, v_rel_bias, v_w_branch_a, v_w_branch_b, v_w_out, v_g_pre_cross, v_g_post_cross, v_g_mem, v_w_q_mem, v_w_kv_mem, v_w_o_mem, v_g_pre_ffn, v_g_post_ffn, v_w_ffn_in, v_w_ffn_out):
    args = (x, mem, g_pre_mix, g_post_mix, w_in, shift_mix, decay_base, decay_up, iclr_base, iclr_up, gate_up, key_norm_scale, key_iclr_scale, bonus_scale, lnx_w, lnx_b, rel_bias, w_branch_a, w_branch_b, w_out, g_pre_cross, g_post_cross, g_mem, w_q_mem, w_kv_mem, w_o_mem, g_pre_ffn, g_post_ffn, w_ffn_in, w_ffn_out, loss_target, m_g_pre_mix, m_g_post_mix, m_w_in, m_shift_mix, m_decay_base, m_decay_up, m_iclr_base, m_iclr_up, m_gate_up, m_key_norm_scale, m_key_iclr_scale, m_bonus_scale, m_lnx_w, m_lnx_b, m_rel_bias, m_w_branch_a, m_w_branch_b, m_w_out, m_g_pre_cross, m_g_post_cross, m_g_mem, m_w_q_mem, m_w_kv_mem, m_w_o_mem, m_g_pre_ffn, m_g_post_ffn, m_w_ffn_in, m_w_ffn_out, v_g_pre_mix, v_g_post_mix, v_w_in, v_shift_mix, v_decay_base, v_decay_up, v_iclr_base, v_iclr_up, v_gate_up, v_key_norm_scale, v_key_iclr_scale, v_bonus_scale, v_lnx_w, v_lnx_b, v_rel_bias, v_w_branch_a, v_w_branch_b, v_w_out, v_g_pre_cross, v_g_post_cross, v_g_mem, v_w_q_mem, v_w_kv_mem, v_w_o_mem, v_g_pre_ffn, v_g_post_ffn, v_w_ffn_in, v_w_ffn_out)
    return _step(args, x.shape[1], mem.shape[1])
```

```python
import functools
import math

import jax
import jax.numpy as jnp
from jax import lax
from jax.experimental import pallas as pl
from jax.experimental.pallas import tpu as pltpu

F32 = jnp.float32
BF16 = jnp.bfloat16

N_DEV = 8
D = 1024
HEAD = 64
N_HEADS = D // HEAD
LANE = 128
CHUNK = 64
LEFT = 8 * CHUNK
BAND = LEFT + CHUNK
REL_CLIP = 128
REL_TABLE = CHUNK + REL_CLIP
MEM_WIDTH = D // 2
MEM_HEADS = 4
FFN = 2816
LORA_W, LORA_A, LORA_G = 64, 64, 160
P_WIDTH = 3 * D + 3 * D + 2 * D + 128 + 128 + 256
C_Q, C_GA, C_LORA = 3 * D, 6 * D, 8 * D
NORM_EPS = 1e-6
GROUP_NORM_EPS = 64e-5
MASK_VALUE = -1e30
ADAM_LR, ADAM_B1, ADAM_B2, ADAM_EPS, ADAM_WD, ADAM_STEP = 0.001, 0.9, 0.999, 1e-08, 0.01, 10
VMEM_LIMIT = 56 * 1024 * 1024


def _cp(*sem):
    return pltpu.CompilerParams(dimension_semantics=sem, vmem_limit_bytes=VMEM_LIMIT)


_NN, _NT, _TN = ((1,), (0,)), ((1,), (1,)), ((0,), (0,))


def _dot_raw(a, b, dims):
    return lax.dot_general(a.astype(BF16), b.astype(BF16), (dims, ((), ())), preferred_element_type=F32)


@functools.partial(jax.custom_vjp, nondiff_argnums=(2,))
def _dot_dims(a, b, dims):
    return _dot_raw(a, b, dims)


def _dot_dims_fwd(a, b, dims):
    return _dot_raw(a, b, dims), (a, b)


def _dot_dims_bwd(dims, res, g):
    a, b = res
    if dims == _NN:
        da, db = _dot_raw(g, b, _NT), _dot_raw(a, g, _TN)
    elif dims == _NT:
        da, db = _dot_raw(g, b, _NN), _dot_raw(g, a, _TN)
    else:
        da, db = _dot_raw(b, g, _NT), _dot_raw(a, g, _NN)
    return da.astype(a.dtype), db.astype(b.dtype)


_dot_dims.defvjp(_dot_dims_fwd, _dot_dims_bwd)


def _dot(a, b, dims=_NN):
    return _dot_dims(a, b, dims)


def _dot_nt(a, b):
    return _dot_dims(a, b, _NT)


def _dot_tn(a, b):
    return _dot_dims(a, b, _TN)


def _split(x, terms):
    parts, rest = [], x.astype(F32)
    for _ in range(terms):
        p = rest.astype(BF16)
        parts.append(p)
        rest = rest - p.astype(F32)
    return parts


def _dot_split_a(a, b, terms=2):
    out = None
    for p in _split(a, terms):
        t = _dot(p, b)
        out = t if out is None else out + t
    return out


def _dot_split_b(a, b, terms=3):
    out = None
    for p in _split(b, terms):
        t = _dot(a, p)
        out = t if out is None else out + t
    return out


MM_VMEM_BUDGET = 30 * 1024 * 1024
MM_HBM_BPS = 3.2e12
MM_MXU_FPS = 8.5e14
MM_STEP_S = 0.35e-6


def _divisors(n, align, cap):
    out = [d for d in range(align, min(n, cap) + 1, align) if n % d == 0]
    return out or [n]


def _mm_tiles(m, n, k, ea, eb, eo, ta):
    best = None
    for tm in _divisors(m, LANE if ta else 8, 2048):
        for tn in _divisors(n, LANE, 2048):
            for tk in _divisors(k, LANE, 2048):
                nk = k // tk
                vmem = 2 * (tm * tk * ea + tk * tn * eb + tm * tn * eo) + (tm * tn * 4 if nk > 1 else 0)
                if vmem > MM_VMEM_BUDGET:
                    continue
                dma = (tm * tk * ea if (nk > 1 or n // tn == 1) else tm * tk * ea * tn / n) + tk * tn * eb + tm * tn * eo / nk
                step = max(2.0 * tm * tn * tk / MM_MXU_FPS, dma / MM_HBM_BPS) + MM_STEP_S
                cost = (m // tm) * (n // tn) * nk * step
                if best is None or cost < best[0]:
                    best = (cost, tm, tn, tk)
    return best[1:]


def _mm(a, b, *, name, ta=False, tb=False, out_dtype=F32, tm=None, tn=None, tk=None, split_a=1, after=None):
    m, k = (a.shape[1], a.shape[0]) if ta else a.shape
    n, kb = (b.shape[0], b.shape[1]) if tb else (b.shape[1], b.shape[0])
    assert k == kb, (a.shape, b.shape, ta, tb)
    if tm is None:
        tm, tn, tk = _mm_tiles(m, n, k, a.dtype.itemsize, b.dtype.itemsize, jnp.dtype(out_dtype).itemsize, ta)
    assert m % tm == 0 and n % tn == 0 and k % tk == 0, (m, n, k, tm, tn, tk)
    nk = k // tk
    dims = ((0 if ta else 1,), (1 if tb else 0,))

    n_after = 0 if after is None else 1

    def body(a_ref, b_ref, *rest):
        o_ref, scratch = rest[n_after], rest[n_after + 1:]
        prod = None
        for p in _split(a_ref[...], split_a) if split_a > 1 else [a_ref[...]]:
            t = _dot_raw(p, b_ref[...], dims)
            prod = t if prod is None else prod + t
        if nk == 1:
            o_ref[...] = prod.astype(o_ref.dtype)
            return
        acc_ref, kk = scratch[0], pl.program_id(2)

        @pl.when(kk == 0)
        def _():
            acc_ref[...] = prod

        @pl.when(kk > 0)
        def _():
            acc_ref[...] += prod

        @pl.when(kk == nk - 1)
        def _():
            o_ref[...] = acc_ref[...].astype(o_ref.dtype)

    a_spec = pl.BlockSpec((tk, tm), lambda i, j, q: (q, i)) if ta else pl.BlockSpec((tm, tk), lambda i, j, q: (i, q))
    b_spec = pl.BlockSpec((tn, tk), lambda i, j, q: (j, q)) if tb else pl.BlockSpec((tk, tn), lambda i, j, q: (q, j))
    return pl.pallas_call(
        body, name=name, grid=(m // tm, n // tn, nk),
        in_specs=[a_spec, b_spec] + [pl.BlockSpec(memory_space=pl.ANY)] * n_after,
        out_specs=pl.BlockSpec((tm, tn), lambda i, j, q: (i, j)),
        out_shape=jax.ShapeDtypeStruct((m, n), out_dtype),
        scratch_shapes=[pltpu.VMEM((tm, tn), F32)] if nk > 1 else [],
        compiler_params=_cp("parallel", "parallel", "arbitrary"),
    )(a, b, *([] if after is None else [after]))


def _piece_steps(pieces, tile):
    counts = [p.shape[1] // tile for p in pieces]
    assert all(p.shape[1] % tile == 0 for p in pieces)
    return [(sum(counts[:i]), c) for i, c in enumerate(counts)], sum(counts)


def _mm_cat_nn(pieces, w, *, name, after=None, tm=2048, tk=256):
    t, n = pieces[0].shape[0], w.shape[1]
    tm = min(tm, t)
    spans, nk = _piece_steps(pieces, tk)
    npc = len(pieces)
    n_after = 0 if after is None else 1

    def body(*refs):
        w_ref, o_ref, acc_ref = refs[npc], refs[npc + 1 + n_after], refs[npc + 2 + n_after]
        q = pl.program_id(1)

        @pl.when(q == 0)
        def _():
            acc_ref[...] = jnp.zeros_like(acc_ref)

        for p_ref, (first, count) in zip(refs[:npc], spans):
            @pl.when(jnp.logical_and(q >= first, q < first + count))
            def _(p_ref=p_ref):
                acc_ref[...] += _dot_raw(p_ref[...], w_ref[...], _NN)

        @pl.when(q == nk - 1)
        def _():
            o_ref[...] = acc_ref[...].astype(o_ref.dtype)

    def piece_spec(first, count):
        return pl.BlockSpec((tm, tk), lambda i, q: (i, jnp.clip(q - first, 0, count - 1)))

    return pl.pallas_call(
        body, name=name, grid=(t // tm, nk),
        in_specs=[piece_spec(*s) for s in spans] + [pl.BlockSpec((tk, n), lambda i, q: (q, 0))]
        + [pl.BlockSpec(memory_space=pl.ANY)] * n_after,
        out_specs=pl.BlockSpec((tm, n), lambda i, q: (i, 0)),
        out_shape=jax.ShapeDtypeStruct((t, n), BF16),
        scratch_shapes=[pltpu.VMEM((tm, n), F32)],
        compiler_params=_cp("parallel", "arbitrary"),
    )(*pieces, w, *([] if after is None else [after]))


def _mm_cat_tn(pieces, a, *, name, after=None, tk=1024, tn=512):
    t, m = a.shape
    tk = min(tk, t)
    spans, nj = _piece_steps(pieces, tn)
    npc, nk = len(pieces), t // tk
    n_after = 0 if after is None else 1

    def body(a_ref, *refs):
        o_ref, acc_ref = refs[npc + n_after], refs[npc + 1 + n_after]
        j, q = pl.program_id(0), pl.program_id(1)

        @pl.when(q == 0)
        def _():
            acc_ref[...] = jnp.zeros_like(acc_ref)

        for p_ref, (first, count) in zip(refs[:npc], spans):
            @pl.when(jnp.logical_and(j >= first, j < first + count))
            def _(p_ref=p_ref):
                acc_ref[...] += _dot_raw(p_ref[...], a_ref[...], _TN)

        @pl.when(q == nk - 1)
        def _():
            o_ref[...] = acc_ref[...].astype(o_ref.dtype)

    def piece_spec(first, count):
        def index(j, q):
            mine = jnp.logical_and(j >= first, j < first + count)
            return jnp.where(mine, q, 0), jnp.clip(j - first, 0, count - 1)
        return pl.BlockSpec((tk, tn), index)

    return pl.pallas_call(
        body, name=name, grid=(nj, nk),
        in_specs=[pl.BlockSpec((tk, m), lambda j, q: (q, 0))] + [piece_spec(*s) for s in spans]
        + [pl.BlockSpec(memory_space=pl.ANY)] * n_after,
        out_specs=pl.BlockSpec((tn, m), lambda j, q: (j, 0)),
        out_shape=jax.ShapeDtypeStruct((nj * tn, m), BF16),
        scratch_shapes=[pltpu.VMEM((tn, m), F32)],
        compiler_params=_cp("parallel", "arbitrary"),
    )(a, *pieces, *([] if after is None else [after]))


def _win(arr, start=0, width=None):
    width = arr.shape[1] if width is None else width
    assert start % width == 0
    return (arr, start // width, width)


def _row_specs(rows, tm):
    return [pl.BlockSpec((tm, w), functools.partial(lambda i, cb: (i, cb), cb=cb)) for (_, cb, w) in rows]


def _full_spec(p):
    nd = p.ndim
    return pl.BlockSpec(p.shape, lambda i, nd=nd: (0,) * nd)


def _rowwise(fn, rows, params, outs, *, name, tm, after=None):
    t = rows[0][0].shape[0]
    tm = min(tm, t)
    assert t % tm == 0
    nr, npar = len(rows), len(params)
    n_after = 0 if after is None else 1

    def body(*refs):
        vals = [r[...] for r in refs[:nr + npar]]
        res = fn(*vals)
        for o_ref, r in zip(refs[nr + npar + n_after:], res):
            o_ref[...] = r.astype(o_ref.dtype)

    return pl.pallas_call(
        body, name=name, grid=(t // tm,),
        in_specs=_row_specs(rows, tm) + [_full_spec(p) for p in params] + [pl.BlockSpec(memory_space=pl.ANY)] * n_after,
        out_specs=[pl.BlockSpec((tm, w), lambda i: (i, 0)) for (w, _) in outs],
        out_shape=[jax.ShapeDtypeStruct((t, w), dt) for (w, dt) in outs],
        compiler_params=_cp("parallel"),
    )(*[r[0] for r in rows], *params, *([] if after is None else [after]))


def _rowwise_bwd(fn, rows, params, n_const, cots, *, name, tm, row_grad, add_to=None, packed=False):
    t = rows[0][0].shape[0]
    tm = min(tm, t)
    assert t % tm == 0
    nr, npar = len(rows), len(params)
    ndp = npar - n_const
    add_to = add_to or {}
    add_idx = sorted(add_to)
    flat_cots = [c for group in cots for c in group]
    kept = [i for i in range(nr) if row_grad[i] is not None]

    def body(*refs):
        pos = 0
        row_v = [r[...] for r in refs[pos:pos + nr]]; pos += nr
        par_v = [r[...] for r in refs[pos:pos + npar]]; pos += npar
        cot_v = [r[...] for r in refs[pos:pos + len(flat_cots)]]; pos += len(flat_cots)
        add_v = [r[...] for r in refs[pos:pos + len(add_idx)]]; pos += len(add_idx)
        if packed:
            offs = [sum(rows[i][2] for i in kept[:q]) for q in range(len(kept))]
            rg_refs = [refs[pos].at[:, o:o + rows[i][2]] for o, i in zip(offs, kept)]; pos += 1
        else:
            rg_refs = refs[pos:pos + len(kept)]; pos += len(kept)
        pg_refs = refs[pos:pos + ndp]

        consts = par_v[ndp:]
        res, vjp = jax.vjp(lambda *args: tuple(fn(*args, *consts)), *row_v, *par_v[:ndp])
        cot_in, q = [], 0
        for j, group in enumerate(cots):
            c = None
            for _ in group:
                cv = cot_v[q].astype(F32); q += 1
                c = cv if c is None else c + cv
            c = jnp.zeros(res[j].shape, F32) if c is None else c
            cot_in.append(c.astype(res[j].dtype))
        grads = vjp(tuple(cot_in))
        for ref, i in zip(rg_refs, kept):
            g = grads[i].astype(F32)
            if i in add_to:
                g = g + add_v[add_idx.index(i)].astype(F32)
            ref[...] = g.astype(ref.dtype)

        @pl.when(pl.program_id(0) == 0)
        def _():
            for ref in pg_refs:
                ref[...] = jnp.zeros_like(ref)

        for ref, g in zip(pg_refs, grads[nr:]):
            ref[...] += g.astype(F32)

    cot_specs = [pl.BlockSpec((tm, c.shape[1]), lambda i: (i, 0)) for c in flat_cots]
    add_specs = [pl.BlockSpec((tm, add_to[i].shape[1]), lambda i_: (i_, 0)) for i in add_idx]
    widths = [sum(rows[i][2] for i in kept)] if packed else [rows[i][2] for i in kept]
    n_rg = len(widths)
    out_specs = [pl.BlockSpec((tm, w), lambda i_: (i_, 0)) for w in widths] + [_full_spec(p) for p in params[:ndp]]
    out_shape = [jax.ShapeDtypeStruct((t, w), row_grad[kept[q]]) for q, w in enumerate(widths)] + [
        jax.ShapeDtypeStruct(p.shape, F32) for p in params[:ndp]]
    res = pl.pallas_call(
        body, name=name, grid=(t // tm,),
        in_specs=_row_specs(rows, tm) + [_full_spec(p) for p in params] + cot_specs + add_specs,
        out_specs=out_specs, out_shape=out_shape,
        compiler_params=_cp("arbitrary"),
    )(*[r[0] for r in rows], *params, *flat_cots, *[add_to[i] for i in add_idx])
    return list(res[:n_rg]), list(res[n_rg:])


def _rms(x, g):
    xf = x.astype(F32)
    return xf * lax.rsqrt(jnp.mean(xf * xf, axis=-1, keepdims=True) + NORM_EPS) * g


def _softplus(x):
    return jnp.maximum(x, 0.0) + jnp.log(1.0 + jnp.exp(-jnp.abs(x)))


def _fn_pre(x, g):
    return (_rms(x, g).astype(BF16),)


def _fn_res(x, u, g_post):
    return (x + _rms(u, g_post),)


def _fn_res_pre(x, u, g_post, g_pre):
    xn = x + _rms(u, g_post)
    return xn, _rms(xn, g_pre).astype(BF16)


def _fn_mix(zga, zgb, ya, yb):
    return ((jax.nn.sigmoid(zga) * ya.astype(F32) + jax.nn.sigmoid(zgb) * yb.astype(F32)).astype(BF16),)


def _fn_swiglu(gate, up):
    gate, up = gate.astype(F32), up.astype(F32)
    return ((gate * jax.nn.sigmoid(gate) * up).astype(BF16),)


def _fn_prep(zk, zw, za, zg, decay_base, d_up, iclr_base, i_up, g_up, kns, kis, e_hd, e_dh):
    w_log = -_softplus(-(decay_base + _dot(jnp.tanh(zw), d_up))) - 0.5
    lw = -jnp.exp(w_log)
    a = jax.nn.sigmoid(iclr_base + _dot(za, i_up))
    g = _dot(jax.nn.sigmoid(zg), g_up)
    kn = zk * kns
    ss = _dot(kn * kn, e_dh)
    inv = lax.rsqrt(jnp.maximum(ss, 1e-24))
    kk = kn * _dot_split_a(inv, e_hd)
    k2 = zk * (1.0 + (a - 1.0) * kis)
    return lw, k2, kk, a, g


def _fn_post(y, r, k2, v, g, lnx_w, lnx_b, bonus, e_hd, e_dh):
    mu = _dot_split_a(_dot(y, e_dh) * (1.0 / HEAD), e_hd)
    yc = y - mu
    var = _dot(yc * yc, e_dh) * (1.0 / HEAD)
    yn = yc * _dot_split_a(lax.rsqrt(var + GROUP_NORM_EPS), e_hd)
    bs = _dot_split_a(_dot(r * k2 * bonus, e_dh), e_hd)
    return (((yn * lnx_w + lnx_b + bs * v) * g).astype(BF16),)


def _shift_fwd(p, col0, ncols, mix, seq, *, name, cw=256):
    t = p.shape[0]
    assert col0 % cw == 0 and ncols % cw == 0 and t % seq == 0
    cb0 = col0 // cw

    def body(p_ref, m_ref, z_ref):
        pv = p_ref[...]
        row = lax.broadcasted_iota(jnp.int32, pv.shape, 0)
        prev = jnp.where(row == 0, 0.0, pltpu.roll(pv, 1, axis=0))
        z_ref[...] = pv + (prev - pv) * m_ref[...]

    return pl.pallas_call(
        body, name=name, grid=(t // seq, ncols // cw),
        in_specs=[pl.BlockSpec((seq, cw), lambda b, c: (b, c + cb0)), pl.BlockSpec((1, cw), lambda b, c: (0, c))],
        out_specs=pl.BlockSpec((seq, cw), lambda b, c: (b, c)),
        out_shape=jax.ShapeDtypeStruct((t, ncols), F32),
        compiler_params=_cp("parallel", "parallel"),
    )(p, mix)


def _shift_bwd(p, col0, ncols, mix, dz_parts, seq, *, name, cw=256):
    t = p.shape[0]
    cb0 = col0 // cw
    n = len(dz_parts)

    def body(*refs):
        p_ref, m_ref = refs[:2]
        dp_ref, dm_ref = refs[2 + n:]
        dz = refs[2][...].astype(F32)
        for r in refs[3:2 + n]:
            dz = dz + r[...].astype(F32)
        pv = p_ref[...]
        mixv = m_ref[...]
        row = lax.broadcasted_iota(jnp.int32, pv.shape, 0)
        prev = jnp.where(row == 0, 0.0, pltpu.roll(pv, 1, axis=0))
        u = dz * mixv
        nxt = jnp.where(row == seq - 1, 0.0, pltpu.roll(u, seq - 1, axis=0))
        dp_ref[...] = (dz - u + nxt).astype(dp_ref.dtype)

        @pl.when(pl.program_id(1) == 0)
        def _():
            dm_ref[...] = jnp.zeros_like(dm_ref)

        dm_ref[...] += jnp.sum(dz * (prev - pv), axis=0, keepdims=True)

    return pl.pallas_call(
        body, name=name, grid=(ncols // cw, t // seq),
        in_specs=[pl.BlockSpec((seq, cw), lambda c, b: (b, c + cb0)), pl.BlockSpec((1, cw), lambda c, b: (0, c))]
        + [pl.BlockSpec((seq, cw), lambda c, b: (b, c))] * n,
        out_specs=[pl.BlockSpec((seq, cw), lambda c, b: (b, c)), pl.BlockSpec((1, cw), lambda c, b: (0, c))],
        out_shape=[jax.ShapeDtypeStruct((t, ncols), BF16), jax.ShapeDtypeStruct((1, ncols), F32)],
        compiler_params=_cp("parallel", "arbitrary"),
    )(p, mix, *dz_parts)


def _each(f, *lists):
    return [f(*xs) for xs in zip(*lists)]


def _tri_inv(low):
    c = low[0].shape[0]
    ti = lax.broadcasted_iota(jnp.int32, (c, c), 0)
    si = lax.broadcasted_iota(jnp.int32, (c, c), 1)
    eye = (ti == si).astype(F32)
    inside = (ti // 4) == (si // 4)
    base = [jnp.where(inside, m, 0.0) for m in low]
    acc = _each(lambda m: _dot(eye - m, eye + _dot(m, m)), base)
    size = 8
    while size <= c:
        wider = (ti // size) == (si // size)
        keep = jnp.logical_and(wider, jnp.logical_not(inside))
        acc = _each(lambda p, m: p - _dot(_dot(p, jnp.where(keep, m, 0.0)), p), acc, low)
        inside, size = wider, size * 2
    return acc


def _stack_rows(a, b):
    return jnp.concatenate([a, b], axis=0)


@jax.custom_vjp
def _split_rows(x):
    h = x.shape[0] // 2
    return x[:h], x[h:]


def _split_rows_fwd(x):
    return _split_rows(x), None


def _split_rows_bwd(_, g):
    return (jnp.concatenate(g, axis=0),)


_split_rows.defvjp(_split_rows_fwd, _split_rows_bwd)


def _masked_halves(stacked, top_mask, bottom_mask):
    halves = _each(_split_rows, stacked)
    return ([jnp.where(top_mask, t, 0.0) for t, _ in halves], [jnp.where(bottom_mask, b, 0.0) for _, b in halves])


@jax.custom_vjp
def _tri_inv_known(low, inv):
    return inv


def _tri_inv_known_fwd(low, inv):
    return inv, inv


def _tri_inv_known_bwd(inv, g):
    dlow = _each(lambda t, gg: -_dot(_dot(t, gg, _TN), t, _NT), inv, g)
    return dlow, _each(jnp.zeros_like, inv)


_tri_inv_known.defvjp(_tri_inv_known_fwd, _tri_inv_known_bwd)


def _tri_ones(c):
    return (lax.broadcasted_iota(jnp.int32, (c, c), 0) >= lax.broadcasted_iota(jnp.int32, (c, c), 1)).astype(F32)


def _cumsum_rows(lw):
    return _dot_split_b(_tri_ones(lw.shape[0]), lw, 3)


def _wkv_chunk(s0, r, lw, cum, k, v, kk, a, inv=None):
    c = r[0].shape[0]
    ti = lax.broadcasted_iota(jnp.int32, (c, c), 0)
    si = lax.broadcasted_iota(jnp.int32, (c, c), 1)
    incl, strict = ti >= si, ti > si
    eg = _each(jnp.exp, cum)
    egp = _each(lambda cs, x: jnp.exp(cs - x), cum, lw)
    ei = _each(lambda cs: jnp.exp(-cs), cum)
    rh, kkh, kt = _each(jnp.multiply, r, eg), _each(jnp.multiply, kk, egp), _each(jnp.multiply, k, ei)
    bt = _each(lambda p, q, e: (p * q) * e, a, kk, ei)
    both = _each(_stack_rows, kkh, rh)
    on_b, on_k, on_s = _each(_dot_nt, both, bt), _each(_dot_nt, both, kt), _each(_dot_nt, both, s0)
    lb, mb = _masked_halves(on_b, strict, incl)
    lk, mk = _masked_halves(on_k, strict, incl)
    on_s = _each(_split_rows, on_s)
    on_v = _each(lambda p, q, x: _split_rows(_dot(_stack_rows(p, q), x)), lk, mk, v)
    rhs = _each(lambda p, q: p[0] + q[0], on_s, on_v)
    inv = _tri_inv(lb) if inv is None else _tri_inv_known(lb, inv)
    u = _each(lambda t, x: -_dot(t, x), inv, rhs)
    y = _each(lambda p, m1, uu, q: p[1] + _dot(m1, uu) + q[1], on_s, mb, u, on_v)
    s1 = _each(lambda s, uu, x, b, kq, w: (s + _dot_tn(_stack_rows(uu, x), _stack_rows(b, kq)))
               * jnp.exp(jnp.sum(w, axis=0, keepdims=True)), s0, u, v, bt, kt, lw)
    return y, s1, inv


WKV_HEADS = 16
WKV_COLS = WKV_HEADS * HEAD
WKV_GROUPS = N_HEADS // WKV_HEADS


def _head_cols(ref):
    return [ref[:, h * HEAD:(h + 1) * HEAD] for h in range(ref.shape[1] // HEAD)]


def _wkv_specs(seq, rev):
    nc = seq // CHUNK

    def rows(col0):
        cb0 = col0 // WKV_COLS
        if rev:
            return pl.BlockSpec((CHUNK, WKV_COLS), lambda b, h, c: (b * nc + nc - 1 - c, cb0 + h))
        return pl.BlockSpec((CHUNK, WKV_COLS), lambda b, h, c: (b * nc + c, cb0 + h))

    if rev:
        st = pl.BlockSpec((1, 1, WKV_HEADS, HEAD, HEAD), lambda b, h, c: (b * WKV_GROUPS + h, nc - 1 - c, 0, 0, 0))
    else:
        st = pl.BlockSpec((1, 1, WKV_HEADS, HEAD, HEAD), lambda b, h, c: (b * WKV_GROUPS + h, c, 0, 0, 0))
    return rows, st


def _wkv_fwd(z_rkv, lw, k2, kk, a, seq):
    t = z_rkv.shape[0]
    nb, nc = t // seq, seq // CHUNK
    rows, st = _wkv_specs(seq, False)

    def body(r_ref, v_ref, lw_ref, k_ref, kk_ref, a_ref, y_ref, st_ref, inv_ref, s_scr, cum_scr):
        @pl.when(pl.program_id(2) == 0)
        def _():
            s_scr[...] = jnp.zeros_like(s_scr)

        cum_scr[...] = _cumsum_rows(lw_ref[...])
        s0 = [s_scr[h] for h in range(WKV_HEADS)]
        y, s1, inv = _wkv_chunk(s0, *[_head_cols(ref) for ref in (r_ref, lw_ref, cum_scr, k_ref, v_ref, kk_ref, a_ref)])
        for h in range(WKV_HEADS):
            st_ref[0, 0, h] = s0[h]
            inv_ref[0, 0, h] = inv[h]
            y_ref[:, h * HEAD:(h + 1) * HEAD] = y[h]
            s_scr[h] = s1[h]

    per_chunk = jax.ShapeDtypeStruct((nb * WKV_GROUPS, nc, WKV_HEADS, HEAD, HEAD), F32)
    return pl.pallas_call(
        body, name="wkv_fwd", grid=(nb, WKV_GROUPS, nc),
        in_specs=[rows(0), rows(2 * D), rows(0), rows(0), rows(0), rows(0)],
        out_specs=[rows(0), st, st],
        out_shape=[jax.ShapeDtypeStruct((t, D), F32), per_chunk, per_chunk],
        scratch_shapes=[pltpu.VMEM((WKV_HEADS, HEAD, HEAD), F32), pltpu.VMEM((CHUNK, WKV_COLS), F32)],
        compiler_params=_cp("parallel", "parallel", "arbitrary"),
    )(z_rkv, z_rkv, lw, k2, kk, a)


def _wkv_bwd(z_rkv, lw, k2, kk, a, states, invs, dy, seq):
    t = z_rkv.shape[0]
    nb, nc = t // seq, seq // CHUNK
    rows, st = _wkv_specs(seq, True)

    def body(r_ref, v_ref, lw_ref, k_ref, kk_ref, a_ref, st_ref, inv_ref, dy_ref,
             dr_ref, dlw_ref, dk_ref, dv_ref, dkk_ref, da_ref, ds_scr, cum_scr, dlw_scr):
        @pl.when(pl.program_id(2) == 0)
        def _():
            ds_scr[...] = jnp.zeros_like(ds_scr)

        cum_scr[...] = _cumsum_rows(lw_ref[...])
        s0 = [st_ref[0, 0, h] for h in range(WKV_HEADS)]
        inv = [inv_ref[0, 0, h] for h in range(WKV_HEADS)]
        _, vjp = jax.vjp(lambda *args: _wkv_chunk(*args, inv=inv)[:2],
                         s0, *[_head_cols(ref) for ref in (r_ref, lw_ref, cum_scr, k_ref, v_ref, kk_ref, a_ref)])
        ds0, dr, dlw, dcum, dk, dv, dkk, da = vjp(
            ([x.astype(F32) for x in _head_cols(dy_ref)], [ds_scr[h] for h in range(WKV_HEADS)]))
        for h in range(WKV_HEADS):
            sl = slice(h * HEAD, (h + 1) * HEAD)
            ds_scr[h] = ds0[h]
            dlw_scr[:, sl] = dlw[h]
            cum_scr[:, sl] = dcum[h]
            for ref, g in zip((dr_ref, dk_ref, dv_ref, dkk_ref, da_ref), (dr, dk, dv, dkk, da)):
                ref[:, sl] = g[h].astype(ref.dtype)
        dlw_ref[...] = (dlw_scr[...] + _dot_raw(_tri_ones(CHUNK), cum_scr[...], _TN)).astype(dlw_ref.dtype)

    return pl.pallas_call(
        body, name="wkv_bwd", grid=(nb, WKV_GROUPS, nc),
        in_specs=[rows(0), rows(2 * D), rows(0), rows(0), rows(0), rows(0), st, st, rows(0)],
        out_specs=[rows(0)] * 6,
        out_shape=[jax.ShapeDtypeStruct((t, D), BF16)] * 6,
        scratch_shapes=[pltpu.VMEM((WKV_HEADS, HEAD, HEAD), F32)] + [pltpu.VMEM((CHUNK, WKV_COLS), F32)] * 2,
        compiler_params=_cp("parallel", "parallel", "arbitrary"),
    )(z_rkv, z_rkv, lw, k2, kk, a, states, invs, dy)


def _softmax(s):
    e = jnp.exp(s - jnp.max(s, axis=-1, keepdims=True))
    return e * (1.0 / jnp.sum(e, axis=-1, keepdims=True))


ATT_FWD_HEADS = 16
ATT_HEADS = 8
ATT_COLS = ATT_HEADS * HEAD
ATT_GROUPS = N_HEADS // ATT_HEADS


def _attn_chunk(q, kb, vb, bias, valid):
    s = _each(lambda x, y, z: jnp.where(valid, _dot_nt(x * (HEAD ** -0.5), y) + z, MASK_VALUE), q, kb, bias)
    return _each(_dot, _each(_softmax, s), vb)


def _pad_fill(pad_ref, src_ref):
    pad_ref[0:LEFT, :] = jnp.zeros((LEFT, pad_ref.shape[1]), pad_ref.dtype)
    pad_ref[LEFT:, :] = src_ref[...].astype(pad_ref.dtype)


def _band_heads(pad_ref, start):
    return [pad_ref[pl.ds(start, BAND), h * HEAD:(h + 1) * HEAD] for h in range(pad_ref.shape[1] // HEAD)]


def _band_valid(c):
    return (c * CHUNK - LEFT + lax.broadcasted_iota(jnp.int32, (1, BAND), 1)) >= 0


def _bias_spec():
    return pl.BlockSpec((ATT_HEADS, CHUNK, BAND), lambda h, b, c: (h, 0, 0))


def _attn_fwd(proj, bias, seq):
    t = proj.shape[0]
    nb, nc = t // seq, seq // CHUNK
    heads = ATT_FWD_HEADS
    cols, groups = heads * HEAD, N_HEADS // heads
    cq = C_Q // cols

    def body(q_ref, k_ref, v_ref, b_ref, o_ref, kpad, vpad):
        c = pl.program_id(2)

        @pl.when(c == 0)
        def _():
            _pad_fill(kpad, k_ref)
            _pad_fill(vpad, v_ref)

        start = pl.multiple_of(c * CHUNK, CHUNK)
        o = _attn_chunk(_head_cols(q_ref), _band_heads(kpad, start), _band_heads(vpad, start),
                        [b_ref[h] for h in range(heads)], _band_valid(c))
        for h in range(heads):
            o_ref[:, h * HEAD:(h + 1) * HEAD] = o[h].astype(o_ref.dtype)

    return pl.pallas_call(
        body, name="attn_fwd", grid=(groups, nb, nc),
        in_specs=[pl.BlockSpec((CHUNK, cols), lambda h, b, c: (b * nc + c, cq + h)),
                  pl.BlockSpec((seq, cols), lambda h, b, c: (b, cq + groups + h)),
                  pl.BlockSpec((seq, cols), lambda h, b, c: (b, cq + 2 * groups + h)),
                  pl.BlockSpec((heads, CHUNK, BAND), lambda h, b, c: (h, 0, 0))],
        out_specs=pl.BlockSpec((CHUNK, cols), lambda h, b, c: (b * nc + c, h)),
        out_shape=jax.ShapeDtypeStruct((t, D), BF16),
        scratch_shapes=[pltpu.VMEM((seq + LEFT, cols), BF16)] * 2,
        compiler_params=_cp("parallel", "arbitrary", "arbitrary"),
    )(proj, proj, proj, bias)


def _attn_bwd(proj, bias, do, seq):
    t = proj.shape[0]
    nb, nc = t // seq, seq // CHUNK
    cq = C_Q // ATT_COLS

    def body(q_ref, k_ref, v_ref, b_ref, do_ref, dq_ref, dk_ref, dv_ref, db_ref, kpad, vpad, dkpad, dvpad):
        b, c = pl.program_id(1), pl.program_id(2)

        @pl.when(c == 0)
        def _():
            _pad_fill(kpad, k_ref)
            _pad_fill(vpad, v_ref)
            dkpad[...] = jnp.zeros_like(dkpad)
            dvpad[...] = jnp.zeros_like(dvpad)

        @pl.when(jnp.logical_and(b == 0, c == 0))
        def _():
            db_ref[...] = jnp.zeros_like(db_ref)

        start = pl.multiple_of(c * CHUNK, CHUNK)
        _, vjp = jax.vjp(functools.partial(_attn_chunk, valid=_band_valid(c)),
                         _head_cols(q_ref), _band_heads(kpad, start), _band_heads(vpad, start),
                         [b_ref[h] for h in range(ATT_HEADS)])
        dq, dkb, dvb, dbias = vjp([x.astype(F32) for x in _head_cols(do_ref)])
        for h in range(ATT_HEADS):
            sl = slice(h * HEAD, (h + 1) * HEAD)
            dq_ref[:, sl] = dq[h].astype(dq_ref.dtype)
            dkpad[pl.ds(start, BAND), sl] += dkb[h].astype(F32)
            dvpad[pl.ds(start, BAND), sl] += dvb[h].astype(F32)
            db_ref[h] += dbias[h]

        @pl.when(c == nc - 1)
        def _():
            dk_ref[...] = dkpad[LEFT:, :].astype(dk_ref.dtype)
            dv_ref[...] = dvpad[LEFT:, :].astype(dv_ref.dtype)

    kv_out = pl.BlockSpec((seq, ATT_COLS), lambda h, b, c: (b, h))
    return pl.pallas_call(
        body, name="attn_bwd", grid=(ATT_GROUPS, nb, nc),
        in_specs=[pl.BlockSpec((CHUNK, ATT_COLS), lambda h, b, c: (b * nc + c, cq + h)),
                  pl.BlockSpec((seq, ATT_COLS), lambda h, b, c: (b, cq + ATT_GROUPS + h)),
                  pl.BlockSpec((seq, ATT_COLS), lambda h, b, c: (b, cq + 2 * ATT_GROUPS + h)),
                  _bias_spec(),
                  pl.BlockSpec((CHUNK, ATT_COLS), lambda h, b, c: (b * nc + c, h))],
        out_specs=[pl.BlockSpec((CHUNK, ATT_COLS), lambda h, b, c: (b * nc + c, h)), kv_out, kv_out,
                   pl.BlockSpec((ATT_HEADS, CHUNK, BAND), lambda h, b, c: (h, 0, 0))],
        out_shape=[jax.ShapeDtypeStruct((t, D), BF16)] * 3 + [jax.ShapeDtypeStruct((N_HEADS, CHUNK, BAND), F32)],
        scratch_shapes=[pltpu.VMEM((seq + LEFT, ATT_COLS), BF16)] * 2 + [pltpu.VMEM((seq + LEFT, ATT_COLS), F32)] * 2,
        compiler_params=_cp("parallel", "arbitrary", "arbitrary"),
    )(proj, proj, proj, bias, do)


def _xattn_tile(q, k, v):
    s = _dot_nt(q, k) * ((MEM_WIDTH // MEM_HEADS) ** -0.5)
    return _dot(_softmax(s), v)


def _xattn_fwd(qm, kvm, seq, n_mem, tq=1024):
    t = qm.shape[0]
    tq = min(tq, seq)
    nb, nq = t // seq, seq // tq

    def body(q_ref, k_ref, v_ref, o_ref):
        o_ref[...] = _xattn_tile(q_ref[...], k_ref[...], v_ref[...]).astype(o_ref.dtype)

    return pl.pallas_call(
        body, name="xattn_fwd", grid=(nb, MEM_HEADS, nq),
        in_specs=[pl.BlockSpec((tq, LANE), lambda b, h, i: (b * nq + i, h)),
                  pl.BlockSpec((n_mem, LANE), lambda b, h, i: (b, h)),
                  pl.BlockSpec((n_mem, LANE), lambda b, h, i: (b, MEM_HEADS + h))],
        out_specs=pl.BlockSpec((tq, LANE), lambda b, h, i: (b * nq + i, h)),
        out_shape=jax.ShapeDtypeStruct((t, MEM_WIDTH), BF16),
        compiler_params=_cp("parallel", "parallel", "parallel"),
    )(qm, kvm, kvm)


def _xattn_bwd(qm, kvm, do, seq, n_mem, tq=1024):
    t = qm.shape[0]
    tq = min(tq, seq)
    nb, nq = t // seq, seq // tq

    def body(q_ref, k_ref, v_ref, do_ref, dq_ref, dkv_ref, dk_acc, dv_acc):
        i = pl.program_id(2)

        @pl.when(i == 0)
        def _():
            dk_acc[...] = jnp.zeros_like(dk_acc)
            dv_acc[...] = jnp.zeros_like(dv_acc)

        _, vjp = jax.vjp(_xattn_tile, q_ref[...], k_ref[...], v_ref[...])
        dq, dk, dv = vjp(do_ref[...].astype(F32))
        dq_ref[...] = dq.astype(dq_ref.dtype)
        dk_acc[...] += dk
        dv_acc[...] += dv

        @pl.when(i == nq - 1)
        def _():
            dkv_ref[0] = dk_acc[...].astype(dkv_ref.dtype)
            dkv_ref[1] = dv_acc[...].astype(dkv_ref.dtype)

    dq, dkv = pl.pallas_call(
        body, name="xattn_bwd", grid=(nb, MEM_HEADS, nq),
        in_specs=[pl.BlockSpec((tq, LANE), lambda b, h, i: (b * nq + i, h)),
                  pl.BlockSpec((n_mem, LANE), lambda b, h, i: (b, h)),
                  pl.BlockSpec((n_mem, LANE), lambda b, h, i: (b, MEM_HEADS + h)),
                  pl.BlockSpec((tq, LANE), lambda b, h, i: (b * nq + i, h))],
        out_specs=[pl.BlockSpec((tq, LANE), lambda b, h, i: (b * nq + i, h)),
                   pl.BlockSpec((2, n_mem, LANE), lambda b, h, i: (0, b, h))],
        out_shape=[jax.ShapeDtypeStruct((t, MEM_WIDTH), BF16), jax.ShapeDtypeStruct((2, nb * n_mem, MEM_WIDTH), BF16)],
        scratch_shapes=[pltpu.VMEM((n_mem, LANE), F32)] * 2,
        compiler_params=_cp("parallel", "parallel", "arbitrary"),
    )(qm, kvm, kvm, do)
    return dq, jnp.concatenate([dkv[0], dkv[1]], axis=1)


def _loss_head(x, u, g_post, target, tm=512):
    t, d = x.shape
    tm = min(tm, t)

    def tile_loss(xv, uv, gv, tv):
        diff = _fn_res(xv, uv, gv)[0] - tv
        return 0.5 * jnp.sum(jnp.mean(diff * diff, axis=-1, keepdims=True), axis=0, keepdims=True)

    def body(x_ref, u_ref, g_ref, t_ref, l_ref, dx_ref, du_ref, dg_ref):
        @pl.when(pl.program_id(0) == 0)
        def _():
            l_ref[...] = jnp.zeros_like(l_ref)
            dg_ref[...] = jnp.zeros_like(dg_ref)

        tv = t_ref[...]
        part, vjp = jax.vjp(lambda xv, uv, gv: tile_loss(xv, uv, gv, tv), x_ref[...], u_ref[...], g_ref[...])
        dx, du, dg = vjp(jnp.ones((1, 1), F32))
        l_ref[...] += part
        dx_ref[...] = dx
        du_ref[...] = du.astype(du_ref.dtype)
        dg_ref[...] += dg

    rows = pl.BlockSpec((tm, d), lambda i: (i, 0))
    vec = pl.BlockSpec((1, d), lambda i: (0, 0))
    return pl.pallas_call(
        body, name="loss_head", grid=(t // tm,),
        in_specs=[rows, rows, vec, rows],
        out_specs=[pl.BlockSpec((8, LANE), lambda i: (0, 0)), rows, rows, vec],
        out_shape=[jax.ShapeDtypeStruct((8, LANE), F32), jax.ShapeDtypeStruct((t, d), F32),
                   jax.ShapeDtypeStruct((t, d), BF16), jax.ShapeDtypeStruct((1, d), F32)],
        compiler_params=_cp("arbitrary"),
    )(x, u, g_post, target)


def _mesh_pos():
    return lax.axis_index("x"), lax.axis_index("y"), lax.axis_index("c")


def _peer(pos, d):
    x, y, c = pos
    return ((1 - x) if d & 4 else x, (1 - y) if d & 2 else y, (1 - c) if d & 1 else c)


def _flat(pos):
    return 4 * pos[0] + 2 * pos[1] + pos[2]


def _exchange(arrays, scatter, *, name):
    n = len(arrays)
    shapes = [a.shape[1:] if scatter else a.shape for a in arrays]

    def body(*refs):
        ins, outs = refs[:n], refs[n:2 * n]
        send, recv, loc = refs[2 * n:]
        pos = _mesh_pos()
        me = _flat(pos)
        pending = []
        for i in range(n):
            own = pltpu.make_async_copy(ins[i].at[me] if scatter else ins[i], outs[i].at[me], loc.at[i])
            own.start()
            pending.append(own)
            for d in range(1, N_DEV):
                peer = _peer(pos, d)
                src = ins[i].at[_flat(peer)] if scatter else ins[i]
                out_cp = pltpu.make_async_remote_copy(
                    src_ref=src, dst_ref=outs[i].at[me], send_sem=send.at[i, d - 1], recv_sem=recv.at[i, d - 1],
                    device_id=peer, device_id_type=pl.DeviceIdType.MESH)
                out_cp.start()
                pending.append(out_cp)
        for i in range(n):
            own = pending[i * N_DEV]
            for d in range(1, N_DEV):
                peer = _peer(pos, d)
                src = ins[i].at[_flat(peer)] if scatter else ins[i]
                pending[i * N_DEV + d].wait_send()
                pltpu.make_async_remote_copy(
                    src_ref=src, dst_ref=outs[i].at[_flat(peer)], send_sem=send.at[i, d - 1], recv_sem=recv.at[i, d - 1],
                    device_id=peer, device_id_type=pl.DeviceIdType.MESH).wait_recv()
            own.wait()

    hbm = pl.BlockSpec(memory_space=pltpu.HBM)
    return pl.pallas_call(
        body, name=name,
        in_specs=[hbm] * n, out_specs=[hbm] * n,
        out_shape=[jax.ShapeDtypeStruct((N_DEV,) + tuple(s), a.dtype) for s, a in zip(shapes, arrays)],
        scratch_shapes=[pltpu.SemaphoreType.DMA((n, N_DEV - 1)), pltpu.SemaphoreType.DMA((n, N_DEV - 1)),
                        pltpu.SemaphoreType.DMA((n,))],
    )(*arrays)


_HBM = pl.BlockSpec(memory_space=pltpu.HBM)
_SEM = pl.BlockSpec(memory_space=pltpu.SEMAPHORE)
_DATAFLOW = pltpu.SideEffectType.DATAFLOW_SIDE_EFFECTING


_ALL_PEERS = tuple(range(1, N_DEV))
_SIBLING_AND_SAME_CORE = (1, 2, 4, 6)


def _remote_copies(ins, lands, send, recv, scatter, dists):
    pos = _mesh_pos()
    me = _flat(pos)
    out = []
    for i in range(len(ins)):
        for j, d in enumerate(dists):
            peer = _peer(pos, d)
            src = ins[i].at[_flat(peer)] if scatter else ins[i]
            pair = i * len(dists) + j
            sems = dict(send_sem=send.at[pair], recv_sem=recv.at[pair], device_id=peer,
                        device_id_type=pl.DeviceIdType.MESH)
            out.append((pltpu.make_async_remote_copy(src_ref=src, dst_ref=lands[i].at[me], **sems),
                        pltpu.make_async_remote_copy(src_ref=src, dst_ref=lands[i].at[_flat(peer)], **sems)))
    return out


def _exchange_start(arrays, scatter, after, *, name, dists=_ALL_PEERS):
    n = len(arrays)
    shapes = [a.shape[1:] if scatter else a.shape for a in arrays]
    lands = [pltpu.with_memory_space_constraint(lax.empty((N_DEV,) + tuple(s), a.dtype), pltpu.HBM)
             for s, a in zip(shapes, arrays)]
    srcs = [pltpu.with_memory_space_constraint(a, pltpu.HBM) for a in arrays]

    def body(*refs):
        ins, land_refs = refs[:n], refs[n:2 * n]
        send, recv, token = refs[2 * n + 1], refs[2 * n + 2], refs[-1]
        for going, _ in _remote_copies(ins, land_refs, send, recv, scatter, dists):
            going.start()
        token[...] = jnp.zeros_like(token)

    sems = pltpu.SemaphoreType.DMA((n * len(dists),))
    res = pl.pallas_call(
        body, name=name,
        out_shape=(sems, sems, *[pltpu.HBM(a.shape, a.dtype) for a in srcs + lands], jax.ShapeDtypeStruct((8, LANE), F32)),
        in_specs=[_HBM] * (2 * n) + [pl.BlockSpec(memory_space=pl.ANY)],
        out_specs=(_SEM, _SEM, *[_HBM] * (2 * n), pl.BlockSpec(memory_space=pltpu.VMEM)),
        input_output_aliases={i: 2 + i for i in range(2 * n)},
        compiler_params=pltpu.CompilerParams(has_side_effects=_DATAFLOW),
    )(*srcs, *lands, after)
    return (n, scatter, dists, res[0], res[1], list(res[2:2 + 2 * n])), res[-1]


def _exchange_wait(handle, after, own, *, name):
    n, scatter, dists, send, recv, thru = handle

    def body(*refs):
        ins, land_refs = refs[:n], refs[n:2 * n]
        for going, coming in _remote_copies(ins, land_refs, refs[2 * n], refs[2 * n + 1], scatter, dists):
            going.wait_send()
            coming.wait_recv()

    res = pl.pallas_call(
        body, name=name,
        out_shape=tuple(pltpu.HBM(a.shape, a.dtype) for a in thru),
        in_specs=[_HBM] * (2 * n) + [_SEM, _SEM] + [pl.BlockSpec(memory_space=pl.ANY)] * len(after),
        out_specs=tuple([_HBM] * (2 * n)),
        input_output_aliases={i: i for i in range(2 * n)},
        compiler_params=pltpu.CompilerParams(has_side_effects=_DATAFLOW),
    )(*thru, send, recv, *after)
    me = _flat(_mesh_pos())
    return [lax.dynamic_update_slice_in_dim(land, o[None].astype(land.dtype), me, 0) for land, o in zip(res[n:], own)]


_OTHER_CHIPS = (2, 4, 6)


def _relay_to_sibling(gathered, *, name):
    n, k = len(gathered), len(_OTHER_CHIPS)

    def body(*refs):
        ins, outs = refs[:n], refs[n:2 * n]
        send, recv = refs[2 * n:]
        pos = _mesh_pos()
        copies = []
        for i in range(n):
            for j, d in enumerate(_OTHER_CHIPS):
                cp = pltpu.make_async_remote_copy(
                    src_ref=ins[i].at[_flat(_peer(pos, d))], dst_ref=outs[i].at[j],
                    send_sem=send.at[i * k + j], recv_sem=recv.at[i * k + j],
                    device_id=_peer(pos, 1), device_id_type=pl.DeviceIdType.MESH)
                cp.start()
                copies.append(cp)
        for cp in copies:
            cp.wait()

    return pl.pallas_call(
        body, name=name, in_specs=[_HBM] * n, out_specs=[_HBM] * n,
        out_shape=[jax.ShapeDtypeStruct((k,) + g.shape[1:], g.dtype) for g in gathered],
        scratch_shapes=[pltpu.SemaphoreType.DMA((n * k,)), pltpu.SemaphoreType.DMA((n * k,))],
    )(*gathered)


def _adamw(parts, w, m, v, *, name, tr=128, after=None):
    r, c = w.shape
    align = 8 * 4 // parts.dtype.itemsize
    row_tiles = [d for d in range(align, min(tr, r) + 1, align) if r % d == 0]
    tr, tc = (max(row_tiles), c) if row_tiles else (r, LANE)
    assert c % tc == 0
    n_after = 0 if after is None else 1

    def body(p_ref, w_ref, m_ref, v_ref, *rest):
        g_ref, d_ref, nm_ref, nv_ref = rest[n_after:]
        g = p_ref[0].astype(F32)
        for j in range(1, N_DEV):
            g = g + p_ref[j].astype(F32)
        m2 = ADAM_B1 * m_ref[...] + (1.0 - ADAM_B1) * g
        v2 = ADAM_B2 * v_ref[...] + (1.0 - ADAM_B2) * (g * g)
        m_hat = m2 / (1.0 - ADAM_B1 ** ADAM_STEP)
        v_hat = v2 / (1.0 - ADAM_B2 ** ADAM_STEP)
        g_ref[...] = g
        d_ref[...] = -ADAM_LR * (m_hat / (jnp.sqrt(v_hat) + ADAM_EPS) + ADAM_WD * w_ref[...])
        nm_ref[...] = m2
        nv_ref[...] = v2

    spec = pl.BlockSpec((tr, tc), lambda i, j: (i, j))
    return pl.pallas_call(
        body, name=name, grid=(r // tr, c // tc),
        in_specs=[pl.BlockSpec((N_DEV, tr, tc), lambda i, j: (0, i, j)), spec, spec, spec]
        + [pl.BlockSpec(memory_space=pl.ANY)] * n_after,
        out_specs=[spec] * 4, out_shape=[jax.ShapeDtypeStruct((r, c), F32)] * 4,
        compiler_params=_cp("parallel", "parallel"),
    )(parts, w, m, v, *([] if after is None else [after]))


def _cols_to_full(g):
    return jnp.transpose(g, (1, 0, 2)).reshape(g.shape[1], N_DEV * g.shape[2])


def _full_to_cols(w):
    r, c = w.shape
    return jnp.transpose(w.reshape(r, N_DEV, c // N_DEV), (1, 0, 2))


def _cut(a, lo, hi, axis):
    return lax.slice_in_dim(a, lo, hi, axis=axis)


def _pad_to(a, size, axis):
    pads = [(0, 0)] * a.ndim
    pads[axis] = (0, size - a.shape[axis])
    return jnp.pad(a, pads)


def _pad_lora(w, axis=1):
    return jnp.concatenate([
        _pad_to(_cut(w, 0, LORA_W, axis), 128, axis), _pad_to(_cut(w, LORA_W, LORA_W + LORA_A, axis), 128, axis),
        _pad_to(_cut(w, LORA_W + LORA_A, w.shape[axis], axis), 256, axis)], axis=axis)


def _unpad_lora(wp, axis=1):
    return jnp.concatenate([_cut(wp, 0, LORA_W, axis), _cut(wp, 128, 128 + LORA_A, axis),
                            _cut(wp, 256, 256 + LORA_G, axis)], axis=axis)


def _permute_in(w, axis):
    rk = 3 * D
    lo = rk + LORA_W + LORA_A + LORA_G
    return jnp.concatenate([_cut(w, 0, rk, axis), _cut(w, lo, w.shape[axis], axis), _pad_lora(_cut(w, rk, lo, axis), axis)],
                           axis=axis)


def _unpermute_in(wp, axis):
    return jnp.concatenate([_cut(wp, 0, 3 * D, axis), _unpad_lora(_cut(wp, C_LORA, P_WIDTH, axis), axis),
                            _cut(wp, 3 * D, C_LORA, axis)], axis=axis)


def _rel_index():
    dist = jnp.arange(CHUNK)[:, None] - jnp.arange(BAND)[None, :] + LEFT
    return (jnp.minimum(dist, REL_CLIP) + (CHUNK - 1)).reshape(-1)


def _local_step(x, mem, target, wt, seq, n_mem, comm):
    t = x.shape[0]
    row = lambda a: a.reshape(1, -1).astype(F32)
    g_pre_mix, g_post_mix = row(wt["g_pre_mix"]), row(wt["g_post_mix"])
    g_pre_cross, g_post_cross, g_mem = row(wt["g_pre_cross"]), row(wt["g_post_cross"]), row(wt["g_mem"])
    g_pre_ffn, g_post_ffn = row(wt["g_pre_ffn"]), row(wt["g_post_ffn"])
    mix = row(wt["shift_mix"])
    mix_rkv, mix_lora = mix[:, :3 * D], _pad_lora(mix[:, 3 * D:])
    decay_base, iclr_base = row(wt["decay_base"]), row(wt["iclr_base"])
    kns, kis = row(wt["key_norm_scale"]), row(wt["key_iclr_scale"])
    lnx_w, lnx_b, bonus = row(wt["lnx_w"]), row(wt["lnx_b"]), row(wt["bonus_scale"])
    e_dh = (jnp.arange(D)[:, None] // HEAD == jnp.arange(N_HEADS)[None, :]).astype(F32)
    e_hd = e_dh.T
    onehot = (jnp.arange(REL_TABLE)[:, None] == _rel_index()[None, :]).astype(BF16)

    begun = comm.begun
    (h1,) = _rowwise(_fn_pre, [_win(x)], [g_pre_mix], [(D, BF16)], name="pre_mix", tm=512, after=begun)
    (mn,) = _rowwise(_fn_pre, [_win(mem)], [g_mem], [(D, BF16)], name="pre_mem", tm=512, after=begun)
    bias = _mm(wt["rel_bias"].astype(F32), onehot, name="mm_bias", split_a=3, after=begun).reshape(N_HEADS, CHUNK, BAND)
    wt = {**wt, **comm.first_weights([h1, mn, bias])}
    w_in = wt["w_in_p"]
    d_up = jnp.pad(wt["decay_up"].astype(F32), ((0, 128 - LORA_W), (0, 0)))
    i_up = jnp.pad(wt["iclr_up"].astype(F32), ((0, 128 - LORA_A), (0, 0)))
    g_up = jnp.pad(wt["gate_up"].astype(F32), ((0, 256 - LORA_G), (0, 0)))
    proj = _mm(h1, w_in, tb=True, name="mm_in", after=comm.first_token)
    z_rkv = _shift_fwd(proj, 0, 3 * D, mix_rkv, seq, name="shift_rkv")
    z_lora = _shift_fwd(proj, C_LORA, 512, mix_lora, seq, name="shift_lora")
    prep_rows = [_win(z_rkv, D, D), _win(z_lora, 0, 128), _win(z_lora, 128, 128), _win(z_lora, 256, 256)]
    prep_params = [decay_base, d_up, iclr_base, i_up, g_up, kns, kis, e_hd, e_dh]
    lw, k2, kk, a, g = _rowwise(_fn_prep, prep_rows, prep_params, [(D, F32)] * 5, name="rwkv_prep", tm=256)
    y, states, invs = _wkv_fwd(z_rkv, lw, k2, kk, a, seq)
    post_rows = [_win(y), _win(z_rkv, 0, D), _win(k2), _win(z_rkv, 2 * D, D), _win(g)]
    post_params = [lnx_w, lnx_b, bonus, e_hd, e_dh]
    (y_a,) = _rowwise(_fn_post, post_rows, post_params, [(D, BF16)], name="rwkv_post", tm=256)
    y_b = _attn_fwd(proj, bias, seq)
    wt = {**wt, **comm.late_weights(y_b)}
    ya_p = _mm(y_a, wt["w_branch_a"], name="mm_a", out_dtype=BF16)
    yb_p = _mm(y_b, wt["w_branch_b"], name="mm_b", out_dtype=BF16)
    mix_rows = [_win(proj, C_GA, D), _win(proj, C_GA + D, D), _win(ya_p), _win(yb_p)]
    (mixed,) = _rowwise(_fn_mix, mix_rows, [], [(D, BF16)], name="gate_mix", tm=512)
    mo = _mm(mixed, wt["w_out"], name="mm_out")
    x1, h2 = _rowwise(_fn_res_pre, [_win(x), _win(mo)], [g_post_mix, g_pre_cross], [(D, F32), (D, BF16)],
                      name="res_mix", tm=512)
    qm = _mm(h2, wt["w_q_mem"], name="mm_q", out_dtype=BF16)
    kvm = _mm(mn, wt["w_kv_mem"], name="mm_kv", out_dtype=BF16)
    om = _xattn_fwd(qm, kvm, seq, n_mem)
    co = _mm(om, wt["w_o_mem"], name="mm_o")
    x2, h3 = _rowwise(_fn_res_pre, [_win(x1), _win(co)], [g_post_cross, g_pre_ffn], [(D, F32), (D, BF16)],
                      name="res_cross", tm=512)
    gu = _mm(h3, wt["w_ffn_in"], tb=True, name="mm_ffn_in", out_dtype=BF16)
    (act,) = _rowwise(_fn_swiglu, [_win(gu, 0, FFN), _win(gu, FFN, FFN)], [], [(FFN, BF16)], name="swiglu", tm=512)
    ff = _mm(act, wt["w_ffn_out"], name="mm_ffn_out")

    gw = {}
    loss, dx2, dff, gw["g_post_ffn"] = _loss_head(x2, ff, g_post_ffn, target)
    dact = _mm(dff, wt["w_ffn_out"], tb=True, name="mm_ffn_out_dx", out_dtype=BF16)
    gw["w_ffn_out"] = _mm(act, dff, ta=True, name="mm_ffn_out_dw", out_dtype=BF16)
    (dgu,), _ = _rowwise_bwd(_fn_swiglu, [_win(gu, 0, FFN), _win(gu, FFN, FFN)], [], 0, [[dact]],
                             name="swiglu_bwd", tm=512, row_grad=[BF16, BF16], packed=True)
    dh3 = _mm(dgu, wt["w_ffn_in"], name="mm_ffn_in_dx", out_dtype=BF16)
    gw["w_ffn_in"] = _mm(dgu, h3, ta=True, name="mm_ffn_in_dw", out_dtype=BF16)
    (dx1, dco), (gw["g_post_cross"], gw["g_pre_ffn"]) = _rowwise_bwd(
        _fn_res_pre, [_win(x1), _win(co)], [g_post_cross, g_pre_ffn], 0, [[dx2], [dh3]],
        name="res_cross_bwd", tm=512, row_grad=[F32, BF16])
    dom = _mm(dco, wt["w_o_mem"], tb=True, name="mm_o_dx", out_dtype=BF16)
    gw["w_o_mem"] = _mm(om, dco, ta=True, name="mm_o_dw", out_dtype=BF16)
    dqm, dkvm = _xattn_bwd(qm, kvm, dom, seq, n_mem)
    dh2 = _mm(dqm, wt["w_q_mem"], tb=True, name="mm_q_dx", out_dtype=BF16)
    gw["w_q_mem"] = _mm(h2, dqm, ta=True, name="mm_q_dw", out_dtype=BF16)
    dmn = _mm(dkvm, wt["w_kv_mem"], tb=True, name="mm_kv_dx", out_dtype=BF16)
    gw["w_kv_mem"] = _mm(mn, dkvm, ta=True, name="mm_kv_dw", out_dtype=BF16)
    _, (gw["g_mem"],) = _rowwise_bwd(_fn_pre, [_win(mem)], [g_mem], 0, [[dmn]], name="pre_mem_bwd", tm=256,
                                     row_grad=[None])
    (dx0, dmo), (gw["g_post_mix"], gw["g_pre_cross"]) = _rowwise_bwd(
        _fn_res_pre, [_win(x), _win(mo)], [g_post_mix, g_pre_cross], 0, [[dx1], [dh2]],
        name="res_mix_bwd", tm=512, row_grad=[F32, BF16])
    dmixed = _mm(dmo, wt["w_out"], tb=True, name="mm_out_dx", out_dtype=BF16)
    gw["w_out"] = _mm(mixed, dmo, ta=True, name="mm_out_dw", out_dtype=BF16)
    (dzga, dzgb, dya_p, dyb_p), _ = _rowwise_bwd(_fn_mix, mix_rows, [], 0, [[dmixed]], name="gate_mix_bwd", tm=512,
                                                 row_grad=[BF16] * 4)
    gw["w_branch_a"] = _mm(y_a, dya_p, ta=True, name="mm_a_dw", out_dtype=BF16)
    gw["w_branch_b"] = _mm(y_b, dyb_p, ta=True, name="mm_b_dw", out_dtype=BF16)
    token = comm.send_early(gw)
    dy_a = _mm(dya_p, wt["w_branch_a"], tb=True, name="mm_a_dx", out_dtype=BF16, after=token)
    dy_b = _mm(dyb_p, wt["w_branch_b"], tb=True, name="mm_b_dx", out_dtype=BF16, after=token)
    dq, dk, dv, dbias = _attn_bwd(proj, bias, dy_b, seq)
    gw["rel_bias"] = _mm(dbias.reshape(N_HEADS, CHUNK * BAND), onehot, tb=True, name="mm_bias_dw", split_a=2)
    (dy, dr_p, dk2_p, dv_p, dg), (gw["lnx_w"], gw["lnx_b"], gw["bonus_scale"]) = _rowwise_bwd(
        _fn_post, post_rows, post_params, 2, [[dy_a]], name="rwkv_post_bwd", tm=512, row_grad=[BF16] * 5)
    dr_s, dlw, dk2_s, dv_s, dkk, da = _wkv_bwd(z_rkv, lw, k2, kk, a, states, invs, dy, seq)
    (dzk, dzw, dza, dzg), pg = _rowwise_bwd(
        _fn_prep, prep_rows, prep_params, 2, [[dlw], [dk2_p, dk2_s], [dkk], [da], [dg]],
        name="rwkv_prep_bwd", tm=512, row_grad=[BF16] * 4)
    gw["decay_base"], gd_up, gw["iclr_base"], gi_up, gg_up, gw["key_norm_scale"], gw["key_iclr_scale"] = pg
    gw["decay_up"], gw["iclr_up"], gw["gate_up"] = gd_up[:LORA_W], gi_up[:LORA_A], gg_up[:LORA_G]
    dp_r, gmix_r = _shift_bwd(proj, 0, D, mix_rkv[:, :D], [dr_p, dr_s], seq, name="shift_r_bwd")
    dp_k, gmix_k = _shift_bwd(proj, D, D, mix_rkv[:, D:2 * D], [dzk], seq, name="shift_k_bwd")
    dp_v, gmix_v = _shift_bwd(proj, 2 * D, D, mix_rkv[:, 2 * D:], [dv_p, dv_s], seq, name="shift_v_bwd")
    dp_lora, gmix_lora = _shift_bwd(proj, C_LORA, 512, mix_lora, [jnp.concatenate([dzw, dza, dzg], axis=1)], seq,
                                    name="shift_lora_bwd")
    gw["shift_mix"] = jnp.concatenate([gmix_r, gmix_k, gmix_v, _unpad_lora(gmix_lora)], axis=1)
    dproj = [dp_r, dp_k, dp_v, dq, dk, dv, dzga, dzgb, dp_lora]
    gw["w_in_p"] = _mm_cat_tn(dproj, h1, name="mm_in_dw", after=gw["rel_bias"])
    token = comm.send_late(gw)
    dh1 = _mm_cat_nn(dproj, w_in, name="mm_in_dx", after=token)
    (grad_x,), (gw["g_pre_mix"],) = _rowwise_bwd(_fn_pre, [_win(x)], [g_pre_mix], 0, [[dh1]], name="pre_mix_bwd",
                                                 tm=512, row_grad=[F32], add_to={0: dx0})
    return loss, grad_x, gw


_COL_SHARDED = ("w_in", "decay_up", "iclr_up", "gate_up", "w_o_mem", "w_ffn_in")
_ROW_SHARDED = ("w_branch_a", "w_branch_b", "w_out", "w_q_mem", "w_kv_mem", "w_ffn_out")
_TRANSPOSED = ("w_in", "w_ffn_in")
_FIRST = ("w_in", "decay_up", "iclr_up", "gate_up")
_REST = ("w_o_mem", "w_ffn_in", "w_branch_a", "w_branch_b", "w_out", "w_q_mem", "w_kv_mem", "w_ffn_out")
_REPLICATED = ("g_pre_mix", "g_post_mix", "shift_mix", "decay_base", "iclr_base", "key_norm_scale", "key_iclr_scale",
               "bonus_scale", "lnx_w", "lnx_b", "rel_bias", "g_pre_cross", "g_post_cross", "g_mem", "g_pre_ffn",
               "g_post_ffn")
_WEIGHTS = ("g_pre_mix", "g_post_mix", "w_in", "shift_mix", "decay_base", "decay_up", "iclr_base", "iclr_up", "gate_up",
            "key_norm_scale", "key_iclr_scale", "bonus_scale", "lnx_w", "lnx_b", "rel_bias", "w_branch_a", "w_branch_b",
            "w_out", "g_pre_cross", "g_post_cross", "g_mem", "w_q_mem", "w_kv_mem", "w_o_mem", "g_pre_ffn", "g_post_ffn",
            "w_ffn_in", "w_ffn_out")
_PACK_ROWS = 8 * ((sum({"shift_mix": 3360, "bonus_scale": 1024, "rel_bias": 3072}.get(n, D) for n in _REPLICATED)
                   + 1 + 8 * LANE - 1) // (8 * LANE))


def _pack(vals):
    flat = jnp.concatenate([v.reshape(-1).astype(F32) for v in vals])
    return jnp.pad(flat, (0, _PACK_ROWS * LANE - flat.shape[0])).reshape(_PACK_ROWS, LANE)


def _unpack(packed, shapes):
    flat, out, pos = packed.reshape(-1), [], 0
    for s in shapes:
        n = math.prod(s)
        out.append(flat[pos:pos + n].reshape(s))
        pos += n
    return out


def _step(args, seq, n_mem):
    names = ("x", "mem") + _WEIGHTS + ("loss_target",) + tuple("m_" + n for n in _WEIGHTS) + tuple("v_" + n for n in _WEIGHTS)
    given = dict(zip(names, args))
    nb = given["x"].shape[0]
    x = given["x"].reshape(nb * seq, D)
    mem = given["mem"].reshape(nb * n_mem, D)
    target = given["loss_target"].reshape(nb * seq, D)
    def local(name, prefix=""):
        a = given[prefix + name][0]
        return a.T if name in _TRANSPOSED else a

    shard = {n: local(n) for n in _COL_SHARDED + _ROW_SHARDED}
    stacked = _ROW_SHARDED + _TRANSPOSED
    out = {}

    def wire(name):
        return shard[name].astype(BF16)

    def full(name, g):
        return g.reshape(-1, g.shape[-1]) if name in stacked else _cols_to_full(g)

    def blocks_of(name, g):
        return (g.reshape((N_DEV,) + shard[name].shape) if name in stacked else _full_to_cols(g)).astype(BF16)

    def update(names, landed, after=None):
        done = []
        for n, parts in zip(names, landed):
            res = _adamw(parts, shard[n], local(n, "m_"), local(n, "v_"), name="adamw_" + n, after=after)
            for kind, r in zip(("grad_", "delta_", "new_m_", "new_v_"), res):
                out[kind + n] = (r.T if n in _TRANSPOSED else r)[None]
            done.append(res[0])
        return done


    class Exchanges:
        def __init__(self):
            srcs = [wire(n) for n in _FIRST]
            self.first, self.begun = _exchange_start(srcs, False, srcs[0], name="gather_first_start",
                                                     dists=_SIBLING_AND_SAME_CORE)

        def first_weights(self, after):
            got = _exchange_wait(self.first, after, [wire(n) for n in _FIRST], name="gather_first_wait")
            relayed = _relay_to_sibling(got, name="gather_first_relay")
            pos = _mesh_pos()
            for j, d in enumerate(_OTHER_CHIPS):
                slot = _flat(_peer(pos, d | 1))
                got = [lax.dynamic_update_slice_in_dim(g, r[j][None], slot, 0) for g, r in zip(got, relayed)]
            self.rest, self.first_token = _exchange_start(
                [wire(n) for n in _REST], False, got[0], name="gather_rest_start")
            first = {n: full(n, g) for n, g in zip(_FIRST, got)}
            first["w_in_p"] = _permute_in(first.pop("w_in"), 0)
            return first

        def late_weights(self, after):
            got = _exchange_wait(self.rest, [after], [wire(n) for n in _REST], name="gather_rest_wait")
            return {n: full(n, g) for n, g in zip(_REST, got)}

        def send_early(self, gw):
            self.early_blocks = [blocks_of(n, gw[n]) for n in _REST]
            self.early, token = _exchange_start(self.early_blocks, True, self.early_blocks[-1], name="scatter_rest_start")
            return token

        def send_late(self, gw):
            me = _flat(_mesh_pos())
            own = [lax.dynamic_index_in_dim(b, me, 0, keepdims=False) for b in self.early_blocks]
            landed = _exchange_wait(self.early, [gw["w_in_p"]], own, name="scatter_rest_wait")
            grads = {**gw, "w_in": _unpermute_in(gw["w_in_p"], 0)}
            self.late_blocks = [blocks_of(n, grads[n]) for n in _FIRST]
            self.late, token = _exchange_start(self.late_blocks, True, landed[0], name="scatter_first_start")
            self.updated = update(_REST, landed, after=token)
            return token

        def finish(self, after):
            me = _flat(_mesh_pos())
            own = [lax.dynamic_index_in_dim(b, me, 0, keepdims=False) for b in self.late_blocks]
            update(_FIRST, _exchange_wait(self.late, [*after, *self.updated], own, name="scatter_first_wait"))

    comm = Exchanges()
    wt = {n: given[n][0] for n in _REPLICATED}
    loss_tile, grad_x, gw = _local_step(x, mem, target, wt, seq, n_mem, comm)
    rep_shapes = [given[n].shape for n in _REPLICATED]
    packed, _ = lax.optimization_barrier((_pack([gw[n] for n in _REPLICATED] + [loss_tile[0, 0]]), tuple(comm.updated)))
    small = _exchange([packed], False, name="gather_small")[0]
    zero = jnp.zeros((), F32)
    res = _adamw(small, *[_pack([given[p + n] for n in _REPLICATED] + [zero]) for p in ("", "m_", "v_")],
                 name="adamw_small", tr=_PACK_ROWS)
    for kind, r in zip(("grad_", "delta_", "new_m_", "new_v_"), res):
        for n, val in zip(_REPLICATED, _unpack(r, rep_shapes)):
            out[kind + n] = val
    loss = res[0].reshape(-1)[sum(math.prod(s) for s in rep_shapes)]
    comm.finish([grad_x, res[0]])
    grad_x = grad_x.reshape(nb, seq, D)
    return (loss, grad_x, *[out[k + n] for k in ("grad_", "delta_", "new_m_", "new_v_") for n in _WEIGHTS])


def kernel(x, mem, g_pre_mix, g_post_mix, w_in, shift_mix, decay_base, decay_up, iclr_base, iclr_up, gate_up, key_norm_scale, key_iclr_scale, bonus_scale, lnx_w, lnx_b, rel_bias, w_branch_a, w_branch_b, w_out, g_pre_cross, g_post_cross, g_mem, w_q_mem, w_kv_mem, w_o_mem, g_pre_ffn, g_post_ffn, w_ffn_in, w_ffn_out, loss_target, m_g_pre_mix, m_g_post_mix, m_w_in, m_shift_mix, m_decay_base, m_decay_up, m_iclr_base, m_iclr_up, m_gate_up, m_key_norm_scale, m_key_iclr_scale, m_bonus_scale, m_lnx_w, m_lnx_b, m_rel_bias, m_w_branch_a, m_w_branch_b, m_w_out, m_g_pre_cross, m_g_post_cross, m_g_mem, m_w_q_mem, m_w_kv_mem, m_w_o_mem, m_g_pre_ffn, m_g_post_ffn, m_w_ffn_in, m_w_ffn_out, v_g_pre_mix, v_g_post_mix, v_w_in, v_shift_mix, v_decay_base, v_decay_up, v_iclr_base, v_iclr_up, v_gate_up, v_key_norm_scale, v_key_iclr_scale, v_bonus_scale, v_lnx_w, v_lnx_b, v_rel_bias, v_w_branch_a, v_w_branch_b, v_w_out, v_g_pre_cross, v_g_post_cross, v_g_mem, v_w_q_mem, v_w_kv_mem, v_w_o_mem, v_g_pre_ffn, v_g_post_ffn, v_w_ffn_in, v_w_ffn_out):
    args = (x, mem, g_pre_mix, g_post_mix, w_in, shift_mix, decay_base, decay_up, iclr_base, iclr_up, gate_up, key_norm_scale, key_iclr_scale, bonus_scale, lnx_w, lnx_b, rel_bias, w_branch_a, w_branch_b, w_out, g_pre_cross, g_post_cross, g_mem, w_q_mem, w_kv_mem, w_o_mem, g_pre_ffn, g_post_ffn, w_ffn_in, w_ffn_out, loss_target, m_g_pre_mix, m_g_post_mix, m_w_in, m_shift_mix, m_decay_base, m_decay_up, m_iclr_base, m_iclr_up, m_gate_up, m_key_norm_scale, m_key_iclr_scale, m_bonus_scale, m_lnx_w, m_lnx_b, m_rel_bias, m_w_branch_a, m_w_branch_b, m_w_out, m_g_pre_cross, m_g_post_cross, m_g_mem, m_w_q_mem, m_w_kv_mem, m_w_o_mem, m_g_pre_ffn, m_g_post_ffn, m_w_ffn_in, m_w_ffn_out, v_g_pre_mix, v_g_post_mix, v_w_in, v_shift_mix, v_decay_base, v_decay_up, v_iclr_base, v_iclr_up, v_gate_up, v_key_norm_scale, v_key_iclr_scale, v_bonus_scale, v_lnx_w, v_lnx_b, v_rel_bias, v_w_branch_a, v_w_branch_b, v_w_out, v_g_pre_cross, v_g_post_cross, v_g_mem, v_w_q_mem, v_w_kv_mem, v_w_o_mem, v_g_pre_ffn, v_g_post_ffn, v_w_ffn_in, v_w_ffn_out)
    return _step(args, x.shape[1], mem.shape[1])
```

```python
import functools
import math

import jax
import jax.numpy as jnp
from jax import lax
from jax.experimental import pallas as pl
from jax.experimental.pallas import tpu as pltpu

F32 = jnp.float32
BF16 = jnp.bfloat16

N_DEV = 8
D = 1024
HEAD = 64
N_HEADS = D // HEAD
LANE = 128
CHUNK = 64
LEFT = 8 * CHUNK
BAND = LEFT + CHUNK
REL_CLIP = 128
REL_TABLE = CHUNK + REL_CLIP
MEM_WIDTH = D // 2
MEM_HEADS = 4
FFN = 2816
LORA_W, LORA_A, LORA_G = 64, 64, 160
P_WIDTH = 3 * D + 3 * D + 2 * D + 128 + 128 + 256
C_Q, C_GA, C_LORA = 3 * D, 6 * D, 8 * D
NORM_EPS = 1e-6
GROUP_NORM_EPS = 64e-5
MASK_VALUE = -1e30
ADAM_LR, ADAM_B1, ADAM_B2, ADAM_EPS, ADAM_WD, ADAM_STEP = 0.001, 0.9, 0.999, 1e-08, 0.01, 10
VMEM_LIMIT = 56 * 1024 * 1024


def _cp(*sem):
    return pltpu.CompilerParams(dimension_semantics=sem, vmem_limit_bytes=VMEM_LIMIT)


_NN, _NT, _TN = ((1,), (0,)), ((1,), (1,)), ((0,), (0,))


def _dot_raw(a, b, dims):
    return lax.dot_general(a.astype(BF16), b.astype(BF16), (dims, ((), ())), preferred_element_type=F32)


@functools.partial(jax.custom_vjp, nondiff_argnums=(2,))
def _dot_dims(a, b, dims):
    return _dot_raw(a, b, dims)


def _dot_dims_fwd(a, b, dims):
    return _dot_raw(a, b, dims), (a, b)


def _dot_dims_bwd(dims, res, g):
    a, b = res
    if dims == _NN:
        da, db = _dot_raw(g, b, _NT), _dot_raw(a, g, _TN)
    elif dims == _NT:
        da, db = _dot_raw(g, b, _NN), _dot_raw(g, a, _TN)
    else:
        da, db = _dot_raw(b, g, _NT), _dot_raw(a, g, _NN)
    return da.astype(a.dtype), db.astype(b.dtype)


_dot_dims.defvjp(_dot_dims_fwd, _dot_dims_bwd)


def _dot(a, b, dims=_NN):
    return _dot_dims(a, b, dims)


def _dot_nt(a, b):
    return _dot_dims(a, b, _NT)


def _dot_tn(a, b):
    return _dot_dims(a, b, _TN)


def _split(x, terms):
    parts, rest = [], x.astype(F32)
    for _ in range(terms):
        p = rest.astype(BF16)
        parts.append(p)
        rest = rest - p.astype(F32)
    return parts


def _dot_split_a(a, b, terms=2):
    out = None
    for p in _split(a, terms):
        t = _dot(p, b)
        out = t if out is None else out + t
    return out


def _dot_split_b(a, b, terms=3):
    out = None
    for p in _split(b, terms):
        t = _dot(a, p)
        out = t if out is None else out + t
    return out


MM_VMEM_BUDGET = 30 * 1024 * 1024
MM_HBM_BPS = 3.2e12
MM_MXU_FPS = 8.5e14
MM_STEP_S = 0.35e-6


def _divisors(n, align, cap):
    out = [d for d in range(align, min(n, cap) + 1, align) if n % d == 0]
    return out or [n]


def _mm_tiles(m, n, k, ea, eb, eo, ta):
    best = None
    for tm in _divisors(m, LANE if ta else 8, 2048):
        for tn in _divisors(n, LANE, 2048):
            for tk in _divisors(k, LANE, 2048):
                nk = k // tk
                vmem = 2 * (tm * tk * ea + tk * tn * eb + tm * tn * eo) + (tm * tn * 4 if nk > 1 else 0)
                if vmem > MM_VMEM_BUDGET:
                    continue
                dma = (tm * tk * ea if (nk > 1 or n // tn == 1) else tm * tk * ea * tn / n) + tk * tn * eb + tm * tn * eo / nk
                step = max(2.0 * tm * tn * tk / MM_MXU_FPS, dma / MM_HBM_BPS) + MM_STEP_S
                cost = (m // tm) * (n // tn) * nk * step
                if best is None or cost < best[0]:
                    best = (cost, tm, tn, tk)
    return best[1:]


def _mm(a, b, *, name, ta=False, tb=False, out_dtype=F32, tm=None, tn=None, tk=None, split_a=1, after=None):
    m, k = (a.shape[1], a.shape[0]) if ta else a.shape
    n, kb = (b.shape[0], b.shape[1]) if tb else (b.shape[1], b.shape[0])
    assert k == kb, (a.shape, b.shape, ta, tb)
    if tm is None:
        tm, tn, tk = _mm_tiles(m, n, k, a.dtype.itemsize, b.dtype.itemsize, jnp.dtype(out_dtype).itemsize, ta)
    assert m % tm == 0 and n % tn == 0 and k % tk == 0, (m, n, k, tm, tn, tk)
    nk = k // tk
    dims = ((0 if ta else 1,), (1 if tb else 0,))

    n_after = 0 if after is None else 1

    def body(a_ref, b_ref, *rest):
        o_ref, scratch = rest[n_after], rest[n_after + 1:]
        prod = None
        for p in _split(a_ref[...], split_a) if split_a > 1 else [a_ref[...]]:
            t = _dot_raw(p, b_ref[...], dims)
            prod = t if prod is None else prod + t
        if nk == 1:
            o_ref[...] = prod.astype(o_ref.dtype)
            return
        acc_ref, kk = scratch[0], pl.program_id(2)

        @pl.when(kk == 0)
        def _():
            acc_ref[...] = prod

        @pl.when(kk > 0)
        def _():
            acc_ref[...] += prod

        @pl.when(kk == nk - 1)
        def _():
            o_ref[...] = acc_ref[...].astype(o_ref.dtype)

    a_spec = pl.BlockSpec((tk, tm), lambda i, j, q: (q, i)) if ta else pl.BlockSpec((tm, tk), lambda i, j, q: (i, q))
    b_spec = pl.BlockSpec((tn, tk), lambda i, j, q: (j, q)) if tb else pl.BlockSpec((tk, tn), lambda i, j, q: (q, j))
    return pl.pallas_call(
        body, name=name, grid=(m // tm, n // tn, nk),
        in_specs=[a_spec, b_spec] + [pl.BlockSpec(memory_space=pl.ANY)] * n_after,
        out_specs=pl.BlockSpec((tm, tn), lambda i, j, q: (i, j)),
        out_shape=jax.ShapeDtypeStruct((m, n), out_dtype),
        scratch_shapes=[pltpu.VMEM((tm, tn), F32)] if nk > 1 else [],
        compiler_params=_cp("parallel", "parallel", "arbitrary"),
    )(a, b, *([] if after is None else [after]))


def _piece_steps(pieces, tile):
    counts = [p.shape[1] // tile for p in pieces]
    assert all(p.shape[1] % tile == 0 for p in pieces)
    return [(sum(counts[:i]), c) for i, c in enumerate(counts)], sum(counts)


def _mm_cat_nn(pieces, w, *, name, after=None, tm=2048, tk=256):
    t, n = pieces[0].shape[0], w.shape[1]
    tm = min(tm, t)
    spans, nk = _piece_steps(pieces, tk)
    npc = len(pieces)
    n_after = 0 if after is None else 1

    def body(*refs):
        w_ref, o_ref, acc_ref = refs[npc], refs[npc + 1 + n_after], refs[npc + 2 + n_after]
        q = pl.program_id(1)

        @pl.when(q == 0)
        def _():
            acc_ref[...] = jnp.zeros_like(acc_ref)

        for p_ref, (first, count) in zip(refs[:npc], spans):
            @pl.when(jnp.logical_and(q >= first, q < first + count))
            def _(p_ref=p_ref):
                acc_ref[...] += _dot_raw(p_ref[...], w_ref[...], _NN)

        @pl.when(q == nk - 1)
        def _():
            o_ref[...] = acc_ref[...].astype(o_ref.dtype)

    def piece_spec(first, count):
        return pl.BlockSpec((tm, tk), lambda i, q: (i, jnp.clip(q - first, 0, count - 1)))

    return pl.pallas_call(
        body, name=name, grid=(t // tm, nk),
        in_specs=[piece_spec(*s) for s in spans] + [pl.BlockSpec((tk, n), lambda i, q: (q, 0))]
        + [pl.BlockSpec(memory_space=pl.ANY)] * n_after,
        out_specs=pl.BlockSpec((tm, n), lambda i, q: (i, 0)),
        out_shape=jax.ShapeDtypeStruct((t, n), BF16),
        scratch_shapes=[pltpu.VMEM((tm, n), F32)],
        compiler_params=_cp("parallel", "arbitrary"),
    )(*pieces, w, *([] if after is None else [after]))


def _mm_cat_tn(pieces, a, *, name, after=None, tk=1024, tn=512):
    t, m = a.shape
    tk = min(tk, t)
    spans, nj = _piece_steps(pieces, tn)
    npc, nk = len(pieces), t // tk
    n_after = 0 if after is None else 1

    def body(a_ref, *refs):
        o_ref, acc_ref = refs[npc + n_after], refs[npc + 1 + n_after]
        j, q = pl.program_id(0), pl.program_id(1)

        @pl.when(q == 0)
        def _():
            acc_ref[...] = jnp.zeros_like(acc_ref)

        for p_ref, (first, count) in zip(refs[:npc], spans):
            @pl.when(jnp.logical_and(j >= first, j < first + count))
            def _(p_ref=p_ref):
                acc_ref[...] += _dot_raw(p_ref[...], a_ref[...], _TN)

        @pl.when(q == nk - 1)
        def _():
            o_ref[...] = acc_ref[...].astype(o_ref.dtype)

    def piece_spec(first, count):
        def index(j, q):
            mine = jnp.logical_and(j >= first, j < first + count)
            return jnp.where(mine, q, 0), jnp.clip(j - first, 0, count - 1)
        return pl.BlockSpec((tk, tn), index)

    return pl.pallas_call(
        body, name=name, grid=(nj, nk),
        in_specs=[pl.BlockSpec((tk, m), lambda j, q: (q, 0))] + [piece_spec(*s) for s in spans]
        + [pl.BlockSpec(memory_space=pl.ANY)] * n_after,
        out_specs=pl.BlockSpec((tn, m), lambda j, q: (j, 0)),
        out_shape=jax.ShapeDtypeStruct((nj * tn, m), BF16),
        scratch_shapes=[pltpu.VMEM((tn, m), F32)],
        compiler_params=_cp("parallel", "arbitrary"),
    )(a, *pieces, *([] if after is None else [after]))


def _win(arr, start=0, width=None):
    width = arr.shape[1] if width is None else width
    assert start % width == 0
    return (arr, start // width, width)


def _row_specs(rows, tm):
    return [pl.BlockSpec((tm, w), functools.partial(lambda i, cb: (i, cb), cb=cb)) for (_, cb, w) in rows]


def _full_spec(p):
    nd = p.ndim
    return pl.BlockSpec(p.shape, lambda i, nd=nd: (0,) * nd)


def _rowwise(fn, rows, params, outs, *, name, tm, after=None):
    t = rows[0][0].shape[0]
    tm = min(tm, t)
    assert t % tm == 0
    nr, npar = len(rows), len(params)
    n_after = 0 if after is None else 1

    def body(*refs):
        vals = [r[...] for r in refs[:nr + npar]]
        res = fn(*vals)
        for o_ref, r in zip(refs[nr + npar + n_after:], res):
            o_ref[...] = r.astype(o_ref.dtype)

    return pl.pallas_call(
        body, name=name, grid=(t // tm,),
        in_specs=_row_specs(rows, tm) + [_full_spec(p) for p in params] + [pl.BlockSpec(memory_space=pl.ANY)] * n_after,
        out_specs=[pl.BlockSpec((tm, w), lambda i: (i, 0)) for (w, _) in outs],
        out_shape=[jax.ShapeDtypeStruct((t, w), dt) for (w, dt) in outs],
        compiler_params=_cp("parallel"),
    )(*[r[0] for r in rows], *params, *([] if after is None else [after]))


def _rowwise_bwd(fn, rows, params, n_const, cots, *, name, tm, row_grad, add_to=None, packed=False):
    t = rows[0][0].shape[0]
    tm = min(tm, t)
    assert t % tm == 0
    nr, npar = len(rows), len(params)
    ndp = npar - n_const
    add_to = add_to or {}
    add_idx = sorted(add_to)
    flat_cots = [c for group in cots for c in group]
    kept = [i for i in range(nr) if row_grad[i] is not None]

    def body(*refs):
        pos = 0
        row_v = [r[...] for r in refs[pos:pos + nr]]; pos += nr
        par_v = [r[...] for r in refs[pos:pos + npar]]; pos += npar
        cot_v = [r[...] for r in refs[pos:pos + len(flat_cots)]]; pos += len(flat_cots)
        add_v = [r[...] for r in refs[pos:pos + len(add_idx)]]; pos += len(add_idx)
        if packed:
            offs = [sum(rows[i][2] for i in kept[:q]) for q in range(len(kept))]
            rg_refs = [refs[pos].at[:, o:o + rows[i][2]] for o, i in zip(offs, kept)]; pos += 1
        else:
            rg_refs = refs[pos:pos + len(kept)]; pos += len(kept)
        pg_refs = refs[pos:pos + ndp]

        consts = par_v[ndp:]
        res, vjp = jax.vjp(lambda *args: tuple(fn(*args, *consts)), *row_v, *par_v[:ndp])
        cot_in, q = [], 0
        for j, group in enumerate(cots):
            c = None
            for _ in group:
                cv = cot_v[q].astype(F32); q += 1
                c = cv if c is None else c + cv
            c = jnp.zeros(res[j].shape, F32) if c is None else c
            cot_in.append(c.astype(res[j].dtype))
        grads = vjp(tuple(cot_in))
        for ref, i in zip(rg_refs, kept):
            g = grads[i].astype(F32)
            if i in add_to:
                g = g + add_v[add_idx.index(i)].astype(F32)
            ref[...] = g.astype(ref.dtype)

        @pl.when(pl.program_id(0) == 0)
        def _():
            for ref in pg_refs:
                ref[...] = jnp.zeros_like(ref)

        for ref, g in zip(pg_refs, grads[nr:]):
            ref[...] += g.astype(F32)

    cot_specs = [pl.BlockSpec((tm, c.shape[1]), lambda i: (i, 0)) for c in flat_cots]
    add_specs = [pl.BlockSpec((tm, add_to[i].shape[1]), lambda i_: (i_, 0)) for i in add_idx]
    widths = [sum(rows[i][2] for i in kept)] if packed else [rows[i][2] for i in kept]
    n_rg = len(widths)
    out_specs = [pl.BlockSpec((tm, w), lambda i_: (i_, 0)) for w in widths] + [_full_spec(p) for p in params[:ndp]]
    out_shape = [jax.ShapeDtypeStruct((t, w), row_grad[kept[q]]) for q, w in enumerate(widths)] + [
        jax.ShapeDtypeStruct(p.shape, F32) for p in params[:ndp]]
    res = pl.pallas_call(
        body, name=name, grid=(t // tm,),
        in_specs=_row_specs(rows, tm) + [_full_spec(p) for p in params] + cot_specs + add_specs,
        out_specs=out_specs, out_shape=out_shape,
        compiler_params=_cp("arbitrary"),
    )(*[r[0] for r in rows], *params, *flat_cots, *[add_to[i] for i in add_idx])
    return list(res[:n_rg]), list(res[n_rg:])


def _rms(x, g):
    xf = x.astype(F32)
    return xf * lax.rsqrt(jnp.mean(xf * xf, axis=-1, keepdims=True) + NORM_EPS) * g


def _softplus(x):
    return jnp.maximum(x, 0.0) + jnp.log(1.0 + jnp.exp(-jnp.abs(x)))


def _fn_pre(x, g):
    return (_rms(x, g).astype(BF16),)


def _fn_res(x, u, g_post):
    return (x + _rms(u, g_post),)


def _fn_res_pre(x, u, g_post, g_pre):
    xn = x + _rms(u, g_post)
    return xn, _rms(xn, g_pre).astype(BF16)


def _fn_mix(zga, zgb, ya, yb):
    return ((jax.nn.sigmoid(zga) * ya + jax.nn.sigmoid(zgb) * yb).astype(BF16),)


def _fn_swiglu(gate, up):
    gate, up = gate.astype(F32), up.astype(F32)
    return ((gate * jax.nn.sigmoid(gate) * up).astype(BF16),)


def _fn_prep(zk, zw, za, zg, decay_base, d_up, iclr_base, i_up, g_up, kns, kis, e_hd, e_dh):
    w_log = -_softplus(-(decay_base + _dot(jnp.tanh(zw), d_up))) - 0.5
    lw = -jnp.exp(w_log)
    a = jax.nn.sigmoid(iclr_base + _dot(za, i_up))
    g = _dot(jax.nn.sigmoid(zg), g_up)
    kn = zk * kns
    ss = _dot(kn * kn, e_dh)
    inv = lax.rsqrt(jnp.maximum(ss, 1e-24))
    kk = kn * _dot_split_a(inv, e_hd)
    k2 = zk * (1.0 + (a - 1.0) * kis)
    return lw, k2, kk, a, g


def _fn_post(y, r, k2, v, g, lnx_w, lnx_b, bonus, e_hd, e_dh):
    mu = _dot_split_a(_dot(y, e_dh) * (1.0 / HEAD), e_hd)
    yc = y - mu
    var = _dot(yc * yc, e_dh) * (1.0 / HEAD)
    yn = yc * _dot_split_a(lax.rsqrt(var + GROUP_NORM_EPS), e_hd)
    bs = _dot_split_a(_dot(r * k2 * bonus, e_dh), e_hd)
    return (((yn * lnx_w + lnx_b + bs * v) * g).astype(BF16),)


def _shift_fwd(p, col0, ncols, mix, seq, *, name, cw=256):
    t = p.shape[0]
    assert col0 % cw == 0 and ncols % cw == 0 and t % seq == 0
    cb0 = col0 // cw

    def body(p_ref, m_ref, z_ref):
        pv = p_ref[...]
        row = lax.broadcasted_iota(jnp.int32, pv.shape, 0)
        prev = jnp.where(row == 0, 0.0, pltpu.roll(pv, 1, axis=0))
        z_ref[...] = pv + (prev - pv) * m_ref[...]

    return pl.pallas_call(
        body, name=name, grid=(t // seq, ncols // cw),
        in_specs=[pl.BlockSpec((seq, cw), lambda b, c: (b, c + cb0)), pl.BlockSpec((1, cw), lambda b, c: (0, c))],
        out_specs=pl.BlockSpec((seq, cw), lambda b, c: (b, c)),
        out_shape=jax.ShapeDtypeStruct((t, ncols), F32),
        compiler_params=_cp("parallel", "parallel"),
    )(p, mix)


def _shift_bwd(p, col0, ncols, mix, dz_parts, seq, *, name, cw=256):
    t = p.shape[0]
    cb0 = col0 // cw
    n = len(dz_parts)

    def body(*refs):
        p_ref, m_ref = refs[:2]
        dp_ref, dm_ref = refs[2 + n:]
        dz = refs[2][...].astype(F32)
        for r in refs[3:2 + n]:
            dz = dz + r[...].astype(F32)
        pv = p_ref[...]
        mixv = m_ref[...]
        row = lax.broadcasted_iota(jnp.int32, pv.shape, 0)
        prev = jnp.where(row == 0, 0.0, pltpu.roll(pv, 1, axis=0))
        u = dz * mixv
        nxt = jnp.where(row == seq - 1, 0.0, pltpu.roll(u, seq - 1, axis=0))
        dp_ref[...] = (dz - u + nxt).astype(dp_ref.dtype)

        @pl.when(pl.program_id(1) == 0)
        def _():
            dm_ref[...] = jnp.zeros_like(dm_ref)

        dm_ref[...] += jnp.sum(dz * (prev - pv), axis=0, keepdims=True)

    return pl.pallas_call(
        body, name=name, grid=(ncols // cw, t // seq),
        in_specs=[pl.BlockSpec((seq, cw), lambda c, b: (b, c + cb0)), pl.BlockSpec((1, cw), lambda c, b: (0, c))]
        + [pl.BlockSpec((seq, cw), lambda c, b: (b, c))] * n,
        out_specs=[pl.BlockSpec((seq, cw), lambda c, b: (b, c)), pl.BlockSpec((1, cw), lambda c, b: (0, c))],
        out_shape=[jax.ShapeDtypeStruct((t, ncols), BF16), jax.ShapeDtypeStruct((1, ncols), F32)],
        compiler_params=_cp("parallel", "arbitrary"),
    )(p, mix, *dz_parts)


def _each(f, *lists):
    return [f(*xs) for xs in zip(*lists)]


def _tri_inv(low):
    c = low[0].shape[0]
    ti = lax.broadcasted_iota(jnp.int32, (c, c), 0)
    si = lax.broadcasted_iota(jnp.int32, (c, c), 1)
    eye = (ti == si).astype(F32)
    inside = (ti // 4) == (si // 4)
    base = [jnp.where(inside, m, 0.0) for m in low]
    acc = _each(lambda m: _dot(eye - m, eye + _dot(m, m)), base)
    size = 8
    while size <= c:
        wider = (ti // size) == (si // size)
        keep = jnp.logical_and(wider, jnp.logical_not(inside))
        acc = _each(lambda p, m: p - _dot(_dot(p, jnp.where(keep, m, 0.0)), p), acc, low)
        inside, size = wider, size * 2
    return acc


def _stack_rows(a, b):
    return jnp.concatenate([a, b], axis=0)


@jax.custom_vjp
def _split_rows(x):
    h = x.shape[0] // 2
    return x[:h], x[h:]


def _split_rows_fwd(x):
    return _split_rows(x), None


def _split_rows_bwd(_, g):
    return (jnp.concatenate(g, axis=0),)


_split_rows.defvjp(_split_rows_fwd, _split_rows_bwd)


def _masked_halves(stacked, top_mask, bottom_mask):
    halves = _each(_split_rows, stacked)
    return ([jnp.where(top_mask, t, 0.0) for t, _ in halves], [jnp.where(bottom_mask, b, 0.0) for _, b in halves])


@jax.custom_vjp
def _tri_inv_known(low, inv):
    return inv


def _tri_inv_known_fwd(low, inv):
    return inv, inv


def _tri_inv_known_bwd(inv, g):
    dlow = _each(lambda t, gg: -_dot(_dot(t, gg, _TN), t, _NT), inv, g)
    return dlow, _each(jnp.zeros_like, inv)


_tri_inv_known.defvjp(_tri_inv_known_fwd, _tri_inv_known_bwd)


def _tri_ones(c):
    return (lax.broadcasted_iota(jnp.int32, (c, c), 0) >= lax.broadcasted_iota(jnp.int32, (c, c), 1)).astype(F32)


def _cumsum_rows(lw):
    return _dot_split_b(_tri_ones(lw.shape[0]), lw, 3)


def _wkv_chunk(s0, r, lw, cum, k, v, kk, a, inv=None):
    c = r[0].shape[0]
    ti = lax.broadcasted_iota(jnp.int32, (c, c), 0)
    si = lax.broadcasted_iota(jnp.int32, (c, c), 1)
    incl, strict = ti >= si, ti > si
    eg = _each(jnp.exp, cum)
    egp = _each(lambda cs, x: jnp.exp(cs - x), cum, lw)
    ei = _each(lambda cs: jnp.exp(-cs), cum)
    rh, kkh, kt = _each(jnp.multiply, r, eg), _each(jnp.multiply, kk, egp), _each(jnp.multiply, k, ei)
    bt = _each(lambda p, q, e: (p * q) * e, a, kk, ei)
    both = _each(_stack_rows, kkh, rh)
    on_b, on_k, on_s = _each(_dot_nt, both, bt), _each(_dot_nt, both, kt), _each(_dot_nt, both, s0)
    lb, mb = _masked_halves(on_b, strict, incl)
    lk, mk = _masked_halves(on_k, strict, incl)
    on_s = _each(_split_rows, on_s)
    on_v = _each(lambda p, q, x: _split_rows(_dot(_stack_rows(p, q), x)), lk, mk, v)
    rhs = _each(lambda p, q: p[0] + q[0], on_s, on_v)
    inv = _tri_inv(lb) if inv is None else _tri_inv_known(lb, inv)
    u = _each(lambda t, x: -_dot(t, x), inv, rhs)
    y = _each(lambda p, m1, uu, q: p[1] + _dot(m1, uu) + q[1], on_s, mb, u, on_v)
    s1 = _each(lambda s, uu, x, b, kq, w: (s + _dot_tn(_stack_rows(uu, x), _stack_rows(b, kq)))
               * jnp.exp(jnp.sum(w, axis=0, keepdims=True)), s0, u, v, bt, kt, lw)
    return y, s1, inv


WKV_HEADS = 16
WKV_COLS = WKV_HEADS * HEAD
WKV_GROUPS = N_HEADS // WKV_HEADS


def _head_cols(ref):
    return [ref[:, h * HEAD:(h + 1) * HEAD] for h in range(ref.shape[1] // HEAD)]


def _wkv_specs(seq, rev):
    nc = seq // CHUNK

    def rows(col0):
        cb0 = col0 // WKV_COLS
        if rev:
            return pl.BlockSpec((CHUNK, WKV_COLS), lambda b, h, c: (b * nc + nc - 1 - c, cb0 + h))
        return pl.BlockSpec((CHUNK, WKV_COLS), lambda b, h, c: (b * nc + c, cb0 + h))

    if rev:
        st = pl.BlockSpec((1, 1, WKV_HEADS, HEAD, HEAD), lambda b, h, c: (b * WKV_GROUPS + h, nc - 1 - c, 0, 0, 0))
    else:
        st = pl.BlockSpec((1, 1, WKV_HEADS, HEAD, HEAD), lambda b, h, c: (b * WKV_GROUPS + h, c, 0, 0, 0))
    return rows, st


def _wkv_fwd(z_rkv, lw, k2, kk, a, seq):
    t = z_rkv.shape[0]
    nb, nc = t // seq, seq // CHUNK
    rows, st = _wkv_specs(seq, False)

    def body(r_ref, v_ref, lw_ref, k_ref, kk_ref, a_ref, y_ref, st_ref, inv_ref, s_scr, cum_scr):
        @pl.when(pl.program_id(2) == 0)
        def _():
            s_scr[...] = jnp.zeros_like(s_scr)

        cum_scr[...] = _cumsum_rows(lw_ref[...])
        s0 = [s_scr[h] for h in range(WKV_HEADS)]
        y, s1, inv = _wkv_chunk(s0, *[_head_cols(ref) for ref in (r_ref, lw_ref, cum_scr, k_ref, v_ref, kk_ref, a_ref)])
        for h in range(WKV_HEADS):
            st_ref[0, 0, h] = s0[h]
            inv_ref[0, 0, h] = inv[h].astype(inv_ref.dtype)
            y_ref[:, h * HEAD:(h + 1) * HEAD] = y[h]
            s_scr[h] = s1[h]

    per_chunk = jax.ShapeDtypeStruct((nb * WKV_GROUPS, nc, WKV_HEADS, HEAD, HEAD), F32)
    return pl.pallas_call(
        body, name="wkv_fwd", grid=(nb, WKV_GROUPS, nc),
        in_specs=[rows(0), rows(2 * D), rows(0), rows(0), rows(0), rows(0)],
        out_specs=[rows(0), st, st],
        out_shape=[jax.ShapeDtypeStruct((t, D), F32), per_chunk, jax.ShapeDtypeStruct(per_chunk.shape, BF16)],
        scratch_shapes=[pltpu.VMEM((WKV_HEADS, HEAD, HEAD), F32), pltpu.VMEM((CHUNK, WKV_COLS), F32)],
        compiler_params=_cp("parallel", "parallel", "arbitrary"),
    )(z_rkv, z_rkv, lw, k2, kk, a)


def _wkv_bwd(z_rkv, lw, k2, kk, a, states, invs, dy, seq):
    t = z_rkv.shape[0]
    nb, nc = t // seq, seq // CHUNK
    rows, st = _wkv_specs(seq, True)

    def body(r_ref, v_ref, lw_ref, k_ref, kk_ref, a_ref, st_ref, inv_ref, dy_ref,
             dr_ref, dlw_ref, dk_ref, dv_ref, dkk_ref, da_ref, ds_scr, cum_scr, dlw_scr):
        @pl.when(pl.program_id(2) == 0)
        def _():
            ds_scr[...] = jnp.zeros_like(ds_scr)

        cum_scr[...] = _cumsum_rows(lw_ref[...])
        s0 = [st_ref[0, 0, h] for h in range(WKV_HEADS)]
        inv = [inv_ref[0, 0, h] for h in range(WKV_HEADS)]
        _, vjp = jax.vjp(lambda *args: _wkv_chunk(*args, inv=inv)[:2],
                         s0, *[_head_cols(ref) for ref in (r_ref, lw_ref, cum_scr, k_ref, v_ref, kk_ref, a_ref)])
        ds0, dr, dlw, dcum, dk, dv, dkk, da = vjp(
            ([x.astype(F32) for x in _head_cols(dy_ref)], [ds_scr[h] for h in range(WKV_HEADS)]))
        for h in range(WKV_HEADS):
            sl = slice(h * HEAD, (h + 1) * HEAD)
            ds_scr[h] = ds0[h]
            dlw_scr[:, sl] = dlw[h]
            cum_scr[:, sl] = dcum[h]
            for ref, g in zip((dr_ref, dk_ref, dv_ref, dkk_ref, da_ref), (dr, dk, dv, dkk, da)):
                ref[:, sl] = g[h].astype(ref.dtype)
        dlw_ref[...] = (dlw_scr[...] + _dot_raw(_tri_ones(CHUNK), cum_scr[...], _TN)).astype(dlw_ref.dtype)

    return pl.pallas_call(
        body, name="wkv_bwd", grid=(nb, WKV_GROUPS, nc),
        in_specs=[rows(0), rows(2 * D), rows(0), rows(0), rows(0), rows(0), st, st, rows(0)],
        out_specs=[rows(0)] * 6,
        out_shape=[jax.ShapeDtypeStruct((t, D), BF16)] * 6,
        scratch_shapes=[pltpu.VMEM((WKV_HEADS, HEAD, HEAD), F32)] + [pltpu.VMEM((CHUNK, WKV_COLS), F32)] * 2,
        compiler_params=_cp("parallel", "parallel", "arbitrary"),
    )(z_rkv, z_rkv, lw, k2, kk, a, states, invs, dy)


def _softmax(s):
    e = jnp.exp(s - jnp.max(s, axis=-1, keepdims=True))
    return e * (1.0 / jnp.sum(e, axis=-1, keepdims=True))


ATT_FWD_HEADS = 16
ATT_HEADS = 8
ATT_COLS = ATT_HEADS * HEAD
ATT_GROUPS = N_HEADS // ATT_HEADS


def _attn_chunk(q, kb, vb, bias, valid):
    s = _each(lambda x, y, z: jnp.where(valid, _dot_nt(x * (HEAD ** -0.5), y) + z, MASK_VALUE), q, kb, bias)
    return _each(_dot, _each(_softmax, s), vb)


def _pad_fill(pad_ref, src_ref):
    pad_ref[0:LEFT, :] = jnp.zeros((LEFT, pad_ref.shape[1]), pad_ref.dtype)
    pad_ref[LEFT:, :] = src_ref[...].astype(pad_ref.dtype)


def _band_heads(pad_ref, start):
    return [pad_ref[pl.ds(start, BAND), h * HEAD:(h + 1) * HEAD] for h in range(pad_ref.shape[1] // HEAD)]


def _band_valid(c):
    return (c * CHUNK - LEFT + lax.broadcasted_iota(jnp.int32, (1, BAND), 1)) >= 0


def _bias_spec():
    return pl.BlockSpec((ATT_HEADS, CHUNK, BAND), lambda h, b, c: (h, 0, 0))


def _attn_fwd(proj, bias, seq):
    t = proj.shape[0]
    nb, nc = t // seq, seq // CHUNK
    heads = ATT_FWD_HEADS
    cols, groups = heads * HEAD, N_HEADS // heads
    cq = C_Q // cols

    def body(q_ref, k_ref, v_ref, b_ref, o_ref, kpad, vpad):
        c = pl.program_id(2)

        @pl.when(c == 0)
        def _():
            _pad_fill(kpad, k_ref)
            _pad_fill(vpad, v_ref)

        start = pl.multiple_of(c * CHUNK, CHUNK)
        o = _attn_chunk(_head_cols(q_ref), _band_heads(kpad, start), _band_heads(vpad, start),
                        [b_ref[h] for h in range(heads)], _band_valid(c))
        for h in range(heads):
            o_ref[:, h * HEAD:(h + 1) * HEAD] = o[h].astype(o_ref.dtype)

    return pl.pallas_call(
        body, name="attn_fwd", grid=(groups, nb, nc),
        in_specs=[pl.BlockSpec((CHUNK, cols), lambda h, b, c: (b * nc + c, cq + h)),
                  pl.BlockSpec((seq, cols), lambda h, b, c: (b, cq + groups + h)),
                  pl.BlockSpec((seq, cols), lambda h, b, c: (b, cq + 2 * groups + h)),
                  pl.BlockSpec((heads, CHUNK, BAND), lambda h, b, c: (h, 0, 0))],
        out_specs=pl.BlockSpec((CHUNK, cols), lambda h, b, c: (b * nc + c, h)),
        out_shape=jax.ShapeDtypeStruct((t, D), BF16),
        scratch_shapes=[pltpu.VMEM((seq + LEFT, cols), BF16)] * 2,
        compiler_params=_cp("parallel", "arbitrary", "arbitrary"),
    )(proj, proj, proj, bias)


def _attn_bwd(proj, bias, do, seq):
    t = proj.shape[0]
    nb, nc = t // seq, seq // CHUNK
    cq = C_Q // ATT_COLS

    def body(q_ref, k_ref, v_ref, b_ref, do_ref, dq_ref, dk_ref, dv_ref, db_ref, kpad, vpad, dkpad, dvpad):
        b, c = pl.program_id(1), pl.program_id(2)

        @pl.when(c == 0)
        def _():
            _pad_fill(kpad, k_ref)
            _pad_fill(vpad, v_ref)
            dkpad[...] = jnp.zeros_like(dkpad)
            dvpad[...] = jnp.zeros_like(dvpad)

        @pl.when(jnp.logical_and(b == 0, c == 0))
        def _():
            db_ref[...] = jnp.zeros_like(db_ref)

        start = pl.multiple_of(c * CHUNK, CHUNK)
        _, vjp = jax.vjp(functools.partial(_attn_chunk, valid=_band_valid(c)),
                         _head_cols(q_ref), _band_heads(kpad, start), _band_heads(vpad, start),
                         [b_ref[h] for h in range(ATT_HEADS)])
        dq, dkb, dvb, dbias = vjp([x.astype(F32) for x in _head_cols(do_ref)])
        for h in range(ATT_HEADS):
            sl = slice(h * HEAD, (h + 1) * HEAD)
            dq_ref[:, sl] = dq[h].astype(dq_ref.dtype)
            dkpad[pl.ds(start, BAND), sl] += dkb[h].astype(F32)
            dvpad[pl.ds(start, BAND), sl] += dvb[h].astype(F32)
            db_ref[h] += dbias[h]

        @pl.when(c == nc - 1)
        def _():
            dk_ref[...] = dkpad[LEFT:, :].astype(dk_ref.dtype)
            dv_ref[...] = dvpad[LEFT:, :].astype(dv_ref.dtype)

    kv_out = pl.BlockSpec((seq, ATT_COLS), lambda h, b, c: (b, h))
    return pl.pallas_call(
        body, name="attn_bwd", grid=(ATT_GROUPS, nb, nc),
        in_specs=[pl.BlockSpec((CHUNK, ATT_COLS), lambda h, b, c: (b * nc + c, cq + h)),
                  pl.BlockSpec((seq, ATT_COLS), lambda h, b, c: (b, cq + ATT_GROUPS + h)),
                  pl.BlockSpec((seq, ATT_COLS), lambda h, b, c: (b, cq + 2 * ATT_GROUPS + h)),
                  _bias_spec(),
                  pl.BlockSpec((CHUNK, ATT_COLS), lambda h, b, c: (b * nc + c, h))],
        out_specs=[pl.BlockSpec((CHUNK, ATT_COLS), lambda h, b, c: (b * nc + c, h)), kv_out, kv_out,
                   pl.BlockSpec((ATT_HEADS, CHUNK, BAND), lambda h, b, c: (h, 0, 0))],
        out_shape=[jax.ShapeDtypeStruct((t, D), BF16)] * 3 + [jax.ShapeDtypeStruct((N_HEADS, CHUNK, BAND), F32)],
        scratch_shapes=[pltpu.VMEM((seq + LEFT, ATT_COLS), BF16)] * 2 + [pltpu.VMEM((seq + LEFT, ATT_COLS), F32)] * 2,
        compiler_params=_cp("parallel", "arbitrary", "arbitrary"),
    )(proj, proj, proj, bias, do)


def _xattn_tile(q, k, v):
    s = _dot_nt(q, k) * ((MEM_WIDTH // MEM_HEADS) ** -0.5)
    return _dot(_softmax(s), v)


def _xattn_fwd(qm, kvm, seq, n_mem, tq=2048):
    t = qm.shape[0]
    tq = min(tq, seq)
    nb, nq = t // seq, seq // tq

    def body(q_ref, k_ref, v_ref, o_ref):
        o_ref[...] = _xattn_tile(q_ref[...], k_ref[...], v_ref[...]).astype(o_ref.dtype)

    return pl.pallas_call(
        body, name="xattn_fwd", grid=(nb, MEM_HEADS, nq),
        in_specs=[pl.BlockSpec((tq, LANE), lambda b, h, i: (b * nq + i, h)),
                  pl.BlockSpec((n_mem, LANE), lambda b, h, i: (b, h)),
                  pl.BlockSpec((n_mem, LANE), lambda b, h, i: (b, MEM_HEADS + h))],
        out_specs=pl.BlockSpec((tq, LANE), lambda b, h, i: (b * nq + i, h)),
        out_shape=jax.ShapeDtypeStruct((t, MEM_WIDTH), BF16),
        compiler_params=_cp("parallel", "parallel", "parallel"),
    )(qm, kvm, kvm)


def _xattn_bwd(qm, kvm, do, seq, n_mem, tq=2048):
    t = qm.shape[0]
    tq = min(tq, seq)
    nb, nq = t // seq, seq // tq

    def body(q_ref, k_ref, v_ref, do_ref, dq_ref, dkv_ref, dk_acc, dv_acc):
        i = pl.program_id(2)

        @pl.when(i == 0)
        def _():
            dk_acc[...] = jnp.zeros_like(dk_acc)
            dv_acc[...] = jnp.zeros_like(dv_acc)

        _, vjp = jax.vjp(_xattn_tile, q_ref[...], k_ref[...], v_ref[...])
        dq, dk, dv = vjp(do_ref[...].astype(F32))
        dq_ref[...] = dq.astype(dq_ref.dtype)
        dk_acc[...] += dk
        dv_acc[...] += dv

        @pl.when(i == nq - 1)
        def _():
            dkv_ref[0] = dk_acc[...].astype(dkv_ref.dtype)
            dkv_ref[1] = dv_acc[...].astype(dkv_ref.dtype)

    dq, dkv = pl.pallas_call(
        body, name="xattn_bwd", grid=(nb, MEM_HEADS, nq),
        in_specs=[pl.BlockSpec((tq, LANE), lambda b, h, i: (b * nq + i, h)),
                  pl.BlockSpec((n_mem, LANE), lambda b, h, i: (b, h)),
                  pl.BlockSpec((n_mem, LANE), lambda b, h, i: (b, MEM_HEADS + h)),
                  pl.BlockSpec((tq, LANE), lambda b, h, i: (b * nq + i, h))],
        out_specs=[pl.BlockSpec((tq, LANE), lambda b, h, i: (b * nq + i, h)),
                   pl.BlockSpec((2, n_mem, LANE), lambda b, h, i: (0, b, h))],
        out_shape=[jax.ShapeDtypeStruct((t, MEM_WIDTH), BF16), jax.ShapeDtypeStruct((2, nb * n_mem, MEM_WIDTH), BF16)],
        scratch_shapes=[pltpu.VMEM((n_mem, LANE), F32)] * 2,
        compiler_params=_cp("parallel", "parallel", "arbitrary"),
    )(qm, kvm, kvm, do)
    return dq, jnp.concatenate([dkv[0], dkv[1]], axis=1)


def _loss_head(x, u, g_post, target, tm=512):
    t, d = x.shape
    tm = min(tm, t)

    def tile_loss(xv, uv, gv, tv):
        diff = _fn_res(xv, uv, gv)[0] - tv
        return 0.5 * jnp.sum(jnp.mean(diff * diff, axis=-1, keepdims=True), axis=0, keepdims=True)

    def body(x_ref, u_ref, g_ref, t_ref, l_ref, dx_ref, du_ref, dg_ref):
        @pl.when(pl.program_id(0) == 0)
        def _():
            l_ref[...] = jnp.zeros_like(l_ref)
            dg_ref[...] = jnp.zeros_like(dg_ref)

        tv = t_ref[...]
        part, vjp = jax.vjp(lambda xv, uv, gv: tile_loss(xv, uv, gv, tv), x_ref[...], u_ref[...], g_ref[...])
        dx, du, dg = vjp(jnp.ones((1, 1), F32))
        l_ref[...] += part
        dx_ref[...] = dx
        du_ref[...] = du.astype(du_ref.dtype)
        dg_ref[...] += dg

    rows = pl.BlockSpec((tm, d), lambda i: (i, 0))
    vec = pl.BlockSpec((1, d), lambda i: (0, 0))
    return pl.pallas_call(
        body, name="loss_head", grid=(t // tm,),
        in_specs=[rows, rows, vec, rows],
        out_specs=[pl.BlockSpec((8, LANE), lambda i: (0, 0)), rows, rows, vec],
        out_shape=[jax.ShapeDtypeStruct((8, LANE), F32), jax.ShapeDtypeStruct((t, d), F32),
                   jax.ShapeDtypeStruct((t, d), BF16), jax.ShapeDtypeStruct((1, d), F32)],
        compiler_params=_cp("arbitrary"),
    )(x, u, g_post, target)


def _mesh_pos():
    return lax.axis_index("x"), lax.axis_index("y"), lax.axis_index("c")


def _peer(pos, d):
    x, y, c = pos
    return ((1 - x) if d & 4 else x, (1 - y) if d & 2 else y, (1 - c) if d & 1 else c)


def _flat(pos):
    return 4 * pos[0] + 2 * pos[1] + pos[2]


def _exchange(arrays, scatter, *, name):
    n = len(arrays)
    shapes = [a.shape[1:] if scatter else a.shape for a in arrays]

    def body(*refs):
        ins, outs = refs[:n], refs[n:2 * n]
        send, recv, loc = refs[2 * n:]
        pos = _mesh_pos()
        me = _flat(pos)
        pending = []
        for i in range(n):
            own = pltpu.make_async_copy(ins[i].at[me] if scatter else ins[i], outs[i].at[me], loc.at[i])
            own.start()
            pending.append(own)
            for d in range(1, N_DEV):
                peer = _peer(pos, d)
                src = ins[i].at[_flat(peer)] if scatter else ins[i]
                out_cp = pltpu.make_async_remote_copy(
                    src_ref=src, dst_ref=outs[i].at[me], send_sem=send.at[i, d - 1], recv_sem=recv.at[i, d - 1],
                    device_id=peer, device_id_type=pl.DeviceIdType.MESH)
                out_cp.start()
                pending.append(out_cp)
        for i in range(n):
            own = pending[i * N_DEV]
            for d in range(1, N_DEV):
                peer = _peer(pos, d)
                src = ins[i].at[_flat(peer)] if scatter else ins[i]
                pending[i * N_DEV + d].wait_send()
                pltpu.make_async_remote_copy(
                    src_ref=src, dst_ref=outs[i].at[_flat(peer)], send_sem=send.at[i, d - 1], recv_sem=recv.at[i, d - 1],
                    device_id=peer, device_id_type=pl.DeviceIdType.MESH).wait_recv()
            own.wait()

    hbm = pl.BlockSpec(memory_space=pltpu.HBM)
    return pl.pallas_call(
        body, name=name,
        in_specs=[hbm] * n, out_specs=[hbm] * n,
        out_shape=[jax.ShapeDtypeStruct((N_DEV,) + tuple(s), a.dtype) for s, a in zip(shapes, arrays)],
        scratch_shapes=[pltpu.SemaphoreType.DMA((n, N_DEV - 1)), pltpu.SemaphoreType.DMA((n, N_DEV - 1)),
                        pltpu.SemaphoreType.DMA((n,))],
    )(*arrays)


_HBM = pl.BlockSpec(memory_space=pltpu.HBM)
_SEM = pl.BlockSpec(memory_space=pltpu.SEMAPHORE)
_DATAFLOW = pltpu.SideEffectType.DATAFLOW_SIDE_EFFECTING


_ALL_PEERS = tuple(range(1, N_DEV))
_SIBLING_AND_SAME_CORE = (1, 2, 4, 6)


def _remote_copies(ins, lands, send, recv, scatter, dists):
    pos = _mesh_pos()
    me = _flat(pos)
    out = []
    for i in range(len(ins)):
        for j, d in enumerate(dists):
            peer = _peer(pos, d)
            src = ins[i].at[_flat(peer)] if scatter else ins[i]
            pair = i * len(dists) + j
            sems = dict(send_sem=send.at[pair], recv_sem=recv.at[pair], device_id=peer,
                        device_id_type=pl.DeviceIdType.MESH)
            out.append((pltpu.make_async_remote_copy(src_ref=src, dst_ref=lands[i].at[me], **sems),
                        pltpu.make_async_remote_copy(src_ref=src, dst_ref=lands[i].at[_flat(peer)], **sems)))
    return out


def _exchange_start(arrays, scatter, after, *, name, dists=_ALL_PEERS):
    n = len(arrays)
    shapes = [a.shape[1:] if scatter else a.shape for a in arrays]
    lands = [pltpu.with_memory_space_constraint(lax.empty((N_DEV,) + tuple(s), a.dtype), pltpu.HBM)
             for s, a in zip(shapes, arrays)]
    srcs = [pltpu.with_memory_space_constraint(a, pltpu.HBM) for a in arrays]

    def body(*refs):
        ins, land_refs = refs[:n], refs[n:2 * n]
        send, recv, token = refs[2 * n + 1], refs[2 * n + 2], refs[-1]
        for going, _ in _remote_copies(ins, land_refs, send, recv, scatter, dists):
            going.start()
        token[...] = jnp.zeros_like(token)

    sems = pltpu.SemaphoreType.DMA((n * len(dists),))
    res = pl.pallas_call(
        body, name=name,
        out_shape=(sems, sems, *[pltpu.HBM(a.shape, a.dtype) for a in srcs + lands], jax.ShapeDtypeStruct((8, LANE), F32)),
        in_specs=[_HBM] * (2 * n) + [pl.BlockSpec(memory_space=pl.ANY)],
        out_specs=(_SEM, _SEM, *[_HBM] * (2 * n), pl.BlockSpec(memory_space=pltpu.VMEM)),
        input_output_aliases={i: 2 + i for i in range(2 * n)},
        compiler_params=pltpu.CompilerParams(has_side_effects=_DATAFLOW),
    )(*srcs, *lands, after)
    return (n, scatter, dists, res[0], res[1], list(res[2:2 + 2 * n])), res[-1]


def _exchange_wait(handle, after, own, *, name):
    n, scatter, dists, send, recv, thru = handle

    def body(*refs):
        ins, land_refs = refs[:n], refs[n:2 * n]
        for going, coming in _remote_copies(ins, land_refs, refs[2 * n], refs[2 * n + 1], scatter, dists):
            going.wait_send()
            coming.wait_recv()

    res = pl.pallas_call(
        body, name=name,
        out_shape=tuple(pltpu.HBM(a.shape, a.dtype) for a in thru),
        in_specs=[_HBM] * (2 * n) + [_SEM, _SEM] + [pl.BlockSpec(memory_space=pl.ANY)] * len(after),
        out_specs=tuple([_HBM] * (2 * n)),
        input_output_aliases={i: i for i in range(2 * n)},
        compiler_params=pltpu.CompilerParams(has_side_effects=_DATAFLOW),
    )(*thru, send, recv, *after)
    me = _flat(_mesh_pos())
    return [lax.dynamic_update_slice_in_dim(land, o[None].astype(land.dtype), me, 0) for land, o in zip(res[n:], own)]


_OTHER_CHIPS = (2, 4, 6)


def _relay_to_sibling(gathered, *, name):
    n, k = len(gathered), len(_OTHER_CHIPS)

    def body(*refs):
        ins, outs = refs[:n], refs[n:2 * n]
        send, recv = refs[2 * n:]
        pos = _mesh_pos()
        copies = []
        for i in range(n):
            for j, d in enumerate(_OTHER_CHIPS):
                cp = pltpu.make_async_remote_copy(
                    src_ref=ins[i].at[_flat(_peer(pos, d))], dst_ref=outs[i].at[j],
                    send_sem=send.at[i * k + j], recv_sem=recv.at[i * k + j],
                    device_id=_peer(pos, 1), device_id_type=pl.DeviceIdType.MESH)
                cp.start()
                copies.append(cp)
        for cp in copies:
            cp.wait()

    return pl.pallas_call(
        body, name=name, in_specs=[_HBM] * n, out_specs=[_HBM] * n,
        out_shape=[jax.ShapeDtypeStruct((k,) + g.shape[1:], g.dtype) for g in gathered],
        scratch_shapes=[pltpu.SemaphoreType.DMA((n * k,)), pltpu.SemaphoreType.DMA((n * k,))],
    )(*gathered)


def _adamw(parts, w, m, v, *, name, tr=128, after=None):
    r, c = w.shape
    align = 8 * 4 // parts.dtype.itemsize
    row_tiles = [d for d in range(align, min(tr, r) + 1, align) if r % d == 0]
    tr, tc = (max(row_tiles), c) if row_tiles else (r, LANE)
    assert c % tc == 0
    n_after = 0 if after is None else 1

    def body(p_ref, w_ref, m_ref, v_ref, *rest):
        g_ref, d_ref, nm_ref, nv_ref = rest[n_after:]
        g = p_ref[0].astype(F32)
        for j in range(1, N_DEV):
            g = g + p_ref[j].astype(F32)
        m2 = ADAM_B1 * m_ref[...] + (1.0 - ADAM_B1) * g
        v2 = ADAM_B2 * v_ref[...] + (1.0 - ADAM_B2) * (g * g)
        m_hat = m2 / (1.0 - ADAM_B1 ** ADAM_STEP)
        v_hat = v2 / (1.0 - ADAM_B2 ** ADAM_STEP)
        g_ref[...] = g
        d_ref[...] = -ADAM_LR * (m_hat / (jnp.sqrt(v_hat) + ADAM_EPS) + ADAM_WD * w_ref[...])
        nm_ref[...] = m2
        nv_ref[...] = v2

    spec = pl.BlockSpec((tr, tc), lambda i, j: (i, j))
    return pl.pallas_call(
        body, name=name, grid=(r // tr, c // tc),
        in_specs=[pl.BlockSpec((N_DEV, tr, tc), lambda i, j: (0, i, j)), spec, spec, spec]
        + [pl.BlockSpec(memory_space=pl.ANY)] * n_after,
        out_specs=[spec] * 4, out_shape=[jax.ShapeDtypeStruct((r, c), F32)] * 4,
        compiler_params=_cp("parallel", "parallel"),
    )(parts, w, m, v, *([] if after is None else [after]))


def _cols_to_full(g):
    return jnp.transpose(g, (1, 0, 2)).reshape(g.shape[1], N_DEV * g.shape[2])


def _full_to_cols(w):
    r, c = w.shape
    return jnp.transpose(w.reshape(r, N_DEV, c // N_DEV), (1, 0, 2))


def _cut(a, lo, hi, axis):
    return lax.slice_in_dim(a, lo, hi, axis=axis)


def _pad_to(a, size, axis):
    pads = [(0, 0)] * a.ndim
    pads[axis] = (0, size - a.shape[axis])
    return jnp.pad(a, pads)


def _pad_lora(w, axis=1):
    return jnp.concatenate([
        _pad_to(_cut(w, 0, LORA_W, axis), 128, axis), _pad_to(_cut(w, LORA_W, LORA_W + LORA_A, axis), 128, axis),
        _pad_to(_cut(w, LORA_W + LORA_A, w.shape[axis], axis), 256, axis)], axis=axis)


def _unpad_lora(wp, axis=1):
    return jnp.concatenate([_cut(wp, 0, LORA_W, axis), _cut(wp, 128, 128 + LORA_A, axis),
                            _cut(wp, 256, 256 + LORA_G, axis)], axis=axis)


def _permute_in(w, axis):
    rk = 3 * D
    lo = rk + LORA_W + LORA_A + LORA_G
    return jnp.concatenate([_cut(w, 0, rk, axis), _cut(w, lo, w.shape[axis], axis), _pad_lora(_cut(w, rk, lo, axis), axis)],
                           axis=axis)


def _unpermute_in(wp, axis):
    return jnp.concatenate([_cut(wp, 0, 3 * D, axis), _unpad_lora(_cut(wp, C_LORA, P_WIDTH, axis), axis),
                            _cut(wp, 3 * D, C_LORA, axis)], axis=axis)


def _rel_index():
    dist = jnp.arange(CHUNK)[:, None] - jnp.arange(BAND)[None, :] + LEFT
    return (jnp.minimum(dist, REL_CLIP) + (CHUNK - 1)).reshape(-1)


def _local_step(x, mem, target, wt, seq, n_mem, comm):
    t = x.shape[0]
    row = lambda a: a.reshape(1, -1).astype(F32)
    g_pre_mix, g_post_mix = row(wt["g_pre_mix"]), row(wt["g_post_mix"])
    g_pre_cross, g_post_cross, g_mem = row(wt["g_pre_cross"]), row(wt["g_post_cross"]), row(wt["g_mem"])
    g_pre_ffn, g_post_ffn = row(wt["g_pre_ffn"]), row(wt["g_post_ffn"])
    mix = row(wt["shift_mix"])
    mix_rkv, mix_lora = mix[:, :3 * D], _pad_lora(mix[:, 3 * D:])
    decay_base, iclr_base = row(wt["decay_base"]), row(wt["iclr_base"])
    kns, kis = row(wt["key_norm_scale"]), row(wt["key_iclr_scale"])
    lnx_w, lnx_b, bonus = row(wt["lnx_w"]), row(wt["lnx_b"]), row(wt["bonus_scale"])
    e_dh = (jnp.arange(D)[:, None] // HEAD == jnp.arange(N_HEADS)[None, :]).astype(F32)
    e_hd = e_dh.T
    onehot = (jnp.arange(REL_TABLE)[:, None] == _rel_index()[None, :]).astype(BF16)

    begun = comm.begun
    (h1,) = _rowwise(_fn_pre, [_win(x)], [g_pre_mix], [(D, BF16)], name="pre_mix", tm=512, after=begun)
    (mn,) = _rowwise(_fn_pre, [_win(mem)], [g_mem], [(D, BF16)], name="pre_mem", tm=512, after=begun)
    bias = _mm(wt["rel_bias"].astype(F32), onehot, name="mm_bias", split_a=3, after=begun).reshape(N_HEADS, CHUNK, BAND)
    wt = {**wt, **comm.first_weights([h1, mn, bias])}
    w_in = wt["w_in_p"]
    d_up = jnp.pad(wt["decay_up"].astype(F32), ((0, 128 - LORA_W), (0, 0)))
    i_up = jnp.pad(wt["iclr_up"].astype(F32), ((0, 128 - LORA_A), (0, 0)))
    g_up = jnp.pad(wt["gate_up"].astype(F32), ((0, 256 - LORA_G), (0, 0)))
    proj = _mm(h1, w_in, tb=True, name="mm_in", after=comm.first_token)
    z_rkv = _shift_fwd(proj, 0, 3 * D, mix_rkv, seq, name="shift_rkv")
    z_lora = _shift_fwd(proj, C_LORA, 512, mix_lora, seq, name="shift_lora")
    prep_rows = [_win(z_rkv, D, D), _win(z_lora, 0, 128), _win(z_lora, 128, 128), _win(z_lora, 256, 256)]
    prep_params = [decay_base, d_up, iclr_base, i_up, g_up, kns, kis, e_hd, e_dh]
    lw, k2, kk, a, g = _rowwise(_fn_prep, prep_rows, prep_params, [(D, F32)] * 5, name="rwkv_prep", tm=256)
    y, states, invs = _wkv_fwd(z_rkv, lw, k2, kk, a, seq)
    post_rows = [_win(y), _win(z_rkv, 0, D), _win(k2), _win(z_rkv, 2 * D, D), _win(g)]
    post_params = [lnx_w, lnx_b, bonus, e_hd, e_dh]
    (y_a,) = _rowwise(_fn_post, post_rows, post_params, [(D, BF16)], name="rwkv_post", tm=256)
    y_b = _attn_fwd(proj, bias, seq)
    wt = {**wt, **comm.late_weights(y_b)}
    ya_p = _mm(y_a, wt["w_branch_a"], name="mm_a")
    yb_p = _mm(y_b, wt["w_branch_b"], name="mm_b")
    mix_rows = [_win(proj, C_GA, D), _win(proj, C_GA + D, D), _win(ya_p), _win(yb_p)]
    (mixed,) = _rowwise(_fn_mix, mix_rows, [], [(D, BF16)], name="gate_mix", tm=512)
    mo = _mm(mixed, wt["w_out"], name="mm_out")
    x1, h2 = _rowwise(_fn_res_pre, [_win(x), _win(mo)], [g_post_mix, g_pre_cross], [(D, F32), (D, BF16)],
                      name="res_mix", tm=512)
    qm = _mm(h2, wt["w_q_mem"], name="mm_q", out_dtype=BF16)
    kvm = _mm(mn, wt["w_kv_mem"], name="mm_kv", out_dtype=BF16)
    om = _xattn_fwd(qm, kvm, seq, n_mem)
    co = _mm(om, wt["w_o_mem"], name="mm_o")
    x2, h3 = _rowwise(_fn_res_pre, [_win(x1), _win(co)], [g_post_cross, g_pre_ffn], [(D, F32), (D, BF16)],
                      name="res_cross", tm=512)
    gu = _mm(h3, wt["w_ffn_in"], tb=True, name="mm_ffn_in", out_dtype=BF16)
    (act,) = _rowwise(_fn_swiglu, [_win(gu, 0, FFN), _win(gu, FFN, FFN)], [], [(FFN, BF16)], name="swiglu", tm=512)
    ff = _mm(act, wt["w_ffn_out"], name="mm_ffn_out")

    gw = {}
    loss, dx2, dff, gw["g_post_ffn"] = _loss_head(x2, ff, g_post_ffn, target)
    dact = _mm(dff, wt["w_ffn_out"], tb=True, name="mm_ffn_out_dx", out_dtype=BF16)
    gw["w_ffn_out"] = _mm(act, dff, ta=True, name="mm_ffn_out_dw", out_dtype=BF16)
    (dgu,), _ = _rowwise_bwd(_fn_swiglu, [_win(gu, 0, FFN), _win(gu, FFN, FFN)], [], 0, [[dact]],
                             name="swiglu_bwd", tm=512, row_grad=[BF16, BF16], packed=True)
    dh3 = _mm(dgu, wt["w_ffn_in"], name="mm_ffn_in_dx", out_dtype=BF16)
    gw["w_ffn_in"] = _mm(dgu, h3, ta=True, name="mm_ffn_in_dw", out_dtype=BF16)
    (dx1, dco), (gw["g_post_cross"], gw["g_pre_ffn"]) = _rowwise_bwd(
        _fn_res_pre, [_win(x1), _win(co)], [g_post_cross, g_pre_ffn], 0, [[dx2], [dh3]],
        name="res_cross_bwd", tm=512, row_grad=[F32, BF16])
    dom = _mm(dco, wt["w_o_mem"], tb=True, name="mm_o_dx", out_dtype=BF16)
    gw["w_o_mem"] = _mm(om, dco, ta=True, name="mm_o_dw", out_dtype=BF16)
    dqm, dkvm = _xattn_bwd(qm, kvm, dom, seq, n_mem)
    dh2 = _mm(dqm, wt["w_q_mem"], tb=True, name="mm_q_dx", out_dtype=BF16)
    gw["w_q_mem"] = _mm(h2, dqm, ta=True, name="mm_q_dw", out_dtype=BF16)
    dmn = _mm(dkvm, wt["w_kv_mem"], tb=True, name="mm_kv_dx", out_dtype=BF16)
    gw["w_kv_mem"] = _mm(mn, dkvm, ta=True, name="mm_kv_dw", out_dtype=BF16)
    _, (gw["g_mem"],) = _rowwise_bwd(_fn_pre, [_win(mem)], [g_mem], 0, [[dmn]], name="pre_mem_bwd", tm=256,
                                     row_grad=[None])
    (dx0, dmo), (gw["g_post_mix"], gw["g_pre_cross"]) = _rowwise_bwd(
        _fn_res_pre, [_win(x), _win(mo)], [g_post_mix, g_pre_cross], 0, [[dx1], [dh2]],
        name="res_mix_bwd", tm=512, row_grad=[F32, BF16])
    dmixed = _mm(dmo, wt["w_out"], tb=True, name="mm_out_dx", out_dtype=BF16)
    gw["w_out"] = _mm(mixed, dmo, ta=True, name="mm_out_dw", out_dtype=BF16)
    (dzga, dzgb, dya_p, dyb_p), _ = _rowwise_bwd(_fn_mix, mix_rows, [], 0, [[dmixed]], name="gate_mix_bwd", tm=512,
                                                 row_grad=[BF16] * 4)
    gw["w_branch_a"] = _mm(y_a, dya_p, ta=True, name="mm_a_dw", out_dtype=BF16)
    gw["w_branch_b"] = _mm(y_b, dyb_p, ta=True, name="mm_b_dw", out_dtype=BF16)
    token = comm.send_early(gw)
    dy_a = _mm(dya_p, wt["w_branch_a"], tb=True, name="mm_a_dx", out_dtype=BF16, after=token)
    dy_b = _mm(dyb_p, wt["w_branch_b"], tb=True, name="mm_b_dx", out_dtype=BF16, after=token)
    dq, dk, dv, dbias = _attn_bwd(proj, bias, dy_b, seq)
    gw["rel_bias"] = _mm(dbias.reshape(N_HEADS, CHUNK * BAND), onehot, tb=True, name="mm_bias_dw", split_a=2)
    (dy, dr_p, dk2_p, dv_p, dg), (gw["lnx_w"], gw["lnx_b"], gw["bonus_scale"]) = _rowwise_bwd(
        _fn_post, post_rows, post_params, 2, [[dy_a]], name="rwkv_post_bwd", tm=512, row_grad=[BF16] * 5)
    dr_s, dlw, dk2_s, dv_s, dkk, da = _wkv_bwd(z_rkv, lw, k2, kk, a, states, invs, dy, seq)
    (dzk, dzw, dza, dzg), pg = _rowwise_bwd(
        _fn_prep, prep_rows, prep_params, 2, [[dlw], [dk2_p, dk2_s], [dkk], [da], [dg]],
        name="rwkv_prep_bwd", tm=512, row_grad=[BF16] * 4)
    gw["decay_base"], gd_up, gw["iclr_base"], gi_up, gg_up, gw["key_norm_scale"], gw["key_iclr_scale"] = pg
    gw["decay_up"], gw["iclr_up"], gw["gate_up"] = gd_up[:LORA_W], gi_up[:LORA_A], gg_up[:LORA_G]
    dp_r, gmix_r = _shift_bwd(proj, 0, D, mix_rkv[:, :D], [dr_p, dr_s], seq, name="shift_r_bwd")
    dp_k, gmix_k = _shift_bwd(proj, D, D, mix_rkv[:, D:2 * D], [dzk], seq, name="shift_k_bwd")
    dp_v, gmix_v = _shift_bwd(proj, 2 * D, D, mix_rkv[:, 2 * D:], [dv_p, dv_s], seq, name="shift_v_bwd")
    dp_lora, gmix_lora = _shift_bwd(proj, C_LORA, 512, mix_lora, [jnp.concatenate([dzw, dza, dzg], axis=1)], seq,
                                    name="shift_lora_bwd")
    gw["shift_mix"] = jnp.concatenate([gmix_r, gmix_k, gmix_v, _unpad_lora(gmix_lora)], axis=1)
    dproj = [dp_r, dp_k, dp_v, dq, dk, dv, dzga, dzgb, dp_lora]
    gw["w_in_p"] = _mm_cat_tn(dproj, h1, name="mm_in_dw", after=gw["rel_bias"])
    token = comm.send_late(gw)
    dh1 = _mm_cat_nn(dproj, w_in, name="mm_in_dx", after=token)
    (grad_x,), (gw["g_pre_mix"],) = _rowwise_bwd(_fn_pre, [_win(x)], [g_pre_mix], 0, [[dh1]], name="pre_mix_bwd",
                                                 tm=512, row_grad=[F32], add_to={0: dx0})
    return loss, grad_x, gw


_COL_SHARDED = ("w_in", "decay_up", "iclr_up", "gate_up", "w_o_mem", "w_ffn_in")
_ROW_SHARDED = ("w_branch_a", "w_branch_b", "w_out", "w_q_mem", "w_kv_mem", "w_ffn_out")
_TRANSPOSED = ("w_in", "w_ffn_in")
_FIRST = ("w_in", "decay_up", "iclr_up", "gate_up")
_REST = ("w_o_mem", "w_ffn_in", "w_branch_a", "w_branch_b", "w_out", "w_q_mem", "w_kv_mem", "w_ffn_out")
_REPLICATED = ("g_pre_mix", "g_post_mix", "shift_mix", "decay_base", "iclr_base", "key_norm_scale", "key_iclr_scale",
               "bonus_scale", "lnx_w", "lnx_b", "rel_bias", "g_pre_cross", "g_post_cross", "g_mem", "g_pre_ffn",
               "g_post_ffn")
_WEIGHTS = ("g_pre_mix", "g_post_mix", "w_in", "shift_mix", "decay_base", "decay_up", "iclr_base", "iclr_up", "gate_up",
            "key_norm_scale", "key_iclr_scale", "bonus_scale", "lnx_w", "lnx_b", "rel_bias", "w_branch_a", "w_branch_b",
            "w_out", "g_pre_cross", "g_post_cross", "g_mem", "w_q_mem", "w_kv_mem", "w_o_mem", "g_pre_ffn", "g_post_ffn",
            "w_ffn_in", "w_ffn_out")
_PACK_ROWS = 8 * ((sum({"shift_mix": 3360, "bonus_scale": 1024, "rel_bias": 3072}.get(n, D) for n in _REPLICATED)
                   + 1 + 8 * LANE - 1) // (8 * LANE))


def _pack(vals):
    flat = jnp.concatenate([v.reshape(-1).astype(F32) for v in vals])
    return jnp.pad(flat, (0, _PACK_ROWS * LANE - flat.shape[0])).reshape(_PACK_ROWS, LANE)


def _unpack(packed, shapes):
    flat, out, pos = packed.reshape(-1), [], 0
    for s in shapes:
        n = math.prod(s)
        out.append(flat[pos:pos + n].reshape(s))
        pos += n
    return out


def _step(args, seq, n_mem):
    names = ("x", "mem") + _WEIGHTS + ("loss_target",) + tuple("m_" + n for n in _WEIGHTS) + tuple("v_" + n for n in _WEIGHTS)
    given = dict(zip(names, args))
    nb = given["x"].shape[0]
    x = given["x"].reshape(nb * seq, D)
    mem = given["mem"].reshape(nb * n_mem, D)
    target = given["loss_target"].reshape(nb * seq, D)
    def local(name, prefix=""):
        a = given[prefix + name][0]
        return a.T if name in _TRANSPOSED else a

    shard = {n: local(n) for n in _COL_SHARDED + _ROW_SHARDED}
    stacked = _ROW_SHARDED + _TRANSPOSED
    out = {}

    def wire(name):
        return shard[name].astype(BF16)

    def full(name, g):
        return g.reshape(-1, g.shape[-1]) if name in stacked else _cols_to_full(g)

    def blocks_of(name, g):
        return (g.reshape((N_DEV,) + shard[name].shape) if name in stacked else _full_to_cols(g)).astype(BF16)

    def update(names, landed, after=None):
        done = []
        for n, parts in zip(names, landed):
            res = _adamw(parts, shard[n], local(n, "m_"), local(n, "v_"), name="adamw_" + n, after=after)
            for kind, r in zip(("grad_", "delta_", "new_m_", "new_v_"), res):
                out[kind + n] = (r.T if n in _TRANSPOSED else r)[None]
            done.append(res[0])
        return done


    class Exchanges:
        def __init__(self):
            srcs = [wire(n) for n in _FIRST]
            self.first, self.begun = _exchange_start(srcs, False, srcs[0], name="gather_first_start",
                                                     dists=_SIBLING_AND_SAME_CORE)

        def first_weights(self, after):
            got = _exchange_wait(self.first, after, [wire(n) for n in _FIRST], name="gather_first_wait")
            relayed = _relay_to_sibling(got, name="gather_first_relay")
            pos = _mesh_pos()
            for j, d in enumerate(_OTHER_CHIPS):
                slot = _flat(_peer(pos, d | 1))
                got = [lax.dynamic_update_slice_in_dim(g, r[j][None], slot, 0) for g, r in zip(got, relayed)]
            self.rest, self.first_token = _exchange_start(
                [wire(n) for n in _REST], False, got[0], name="gather_rest_start")
            first = {n: full(n, g) for n, g in zip(_FIRST, got)}
            first["w_in_p"] = _permute_in(first.pop("w_in"), 0)
            return first

        def late_weights(self, after):
            got = _exchange_wait(self.rest, [after], [wire(n) for n in _REST], name="gather_rest_wait")
            return {n: full(n, g) for n, g in zip(_REST, got)}

        def send_early(self, gw):
            self.early_blocks = [blocks_of(n, gw[n]) for n in _REST]
            self.early, token = _exchange_start(self.early_blocks, True, self.early_blocks[-1], name="scatter_rest_start")
            return token

        def send_late(self, gw):
            me = _flat(_mesh_pos())
            own = [lax.dynamic_index_in_dim(b, me, 0, keepdims=False) for b in self.early_blocks]
            landed = _exchange_wait(self.early, [gw["w_in_p"]], own, name="scatter_rest_wait")
            grads = {**gw, "w_in": _unpermute_in(gw["w_in_p"], 0)}
            self.late_blocks = [blocks_of(n, grads[n]) for n in _FIRST]
            self.late, token = _exchange_start(self.late_blocks, True, landed[0], name="scatter_first_start")
            self.updated = update(_REST, landed, after=token)
            return token

        def finish(self, after):
            me = _flat(_mesh_pos())
            own = [lax.dynamic_index_in_dim(b, me, 0, keepdims=False) for b in self.late_blocks]
            update(_FIRST, _exchange_wait(self.late, [*after, *self.updated], own, name="scatter_first_wait"))

    comm = Exchanges()
    wt = {n: given[n][0] for n in _REPLICATED}
    loss_tile, grad_x, gw = _local_step(x, mem, target, wt, seq, n_mem, comm)
    rep_shapes = [given[n].shape for n in _REPLICATED]
    packed, _ = lax.optimization_barrier((_pack([gw[n] for n in _REPLICATED] + [loss_tile[0, 0]]), tuple(comm.updated)))
    small = _exchange([packed], False, name="gather_small")[0]
    zero = jnp.zeros((), F32)
    res = _adamw(small, *[_pack([given[p + n] for n in _REPLICATED] + [zero]) for p in ("", "m_", "v_")],
                 name="adamw_small", tr=_PACK_ROWS)
    for kind, r in zip(("grad_", "delta_", "new_m_", "new_v_"), res):
        for n, val in zip(_REPLICATED, _unpack(r, rep_shapes)):
            out[kind + n] = val
    loss = res[0].reshape(-1)[sum(math.prod(s) for s in rep_shapes)]
    comm.finish([grad_x, res[0]])
    grad_x = grad_x.reshape(nb, seq, D)
    return (loss, grad_x, *[out[k + n] for k in ("grad_", "delta_", "new_m_", "new_v_") for n in _WEIGHTS])


def kernel(x, mem, g_pre_mix, g_post_mix, w_in, shift_mix, decay_base, decay_up, iclr_base, iclr_up, gate_up, key_norm_scale, key_iclr_scale, bonus_scale, lnx_w, lnx_b, rel_bias, w_branch_a, w_branch_b, w_out, g_pre_cross, g_post_cross, g_mem, w_q_mem, w_kv_mem, w_o_mem, g_pre_ffn, g_post_ffn, w_ffn_in, w_ffn_out, loss_target, m_g_pre_mix, m_g_post_mix, m_w_in, m_shift_mix, m_decay_base, m_decay_up, m_iclr_base, m_iclr_up, m_gate_up, m_key_norm_scale, m_key_iclr_scale, m_bonus_scale, m_lnx_w, m_lnx_b, m_rel_bias, m_w_branch_a, m_w_branch_b, m_w_out, m_g_pre_cross, m_g_post_cross, m_g_mem, m_w_q_mem, m_w_kv_mem, m_w_o_mem, m_g_pre_ffn, m_g_post_ffn, m_w_ffn_in, m_w_ffn_out, v_g_pre_mix, v_g_post_mix, v_w_in, v_shift_mix, v_decay_base, v_decay_up, v_iclr_base, v_iclr_up, v_gate_up, v_key_norm_scale, v_key_iclr_scale, v_bonus_scale, v_lnx_w, v_lnx_b, v_rel_bias, v_w_branch_a, v_w_branch_b, v_w_out, v_g_pre_cross, v_g_post_cross, v_g_mem, v_w_q_mem, v_w_kv_mem, v_w_o_mem, v_g_pre_ffn, v_g_post_ffn, v_w_ffn_in, v_w_ffn_out):
    args = (x, mem, g_pre_mix, g_post_mix, w_in, shift_mix, decay_base, decay_up, iclr_base, iclr_up, gate_up, key_norm_scale, key_iclr_scale, bonus_scale, lnx_w, lnx_b, rel_bias, w_branch_a, w_branch_b, w_out, g_pre_cross, g_post_cross, g_mem, w_q_mem, w_kv_mem, w_o_mem, g_pre_ffn, g_post_ffn, w_ffn_in, w_ffn_out, loss_target, m_g_pre_mix, m_g_post_mix, m_w_in, m_shift_mix, m_decay_base, m_decay_up, m_iclr_base, m_iclr_up, m_gate_up, m_key_norm_scale, m_key_iclr_scale, m_bonus_scale, m_lnx_w, m_lnx_b, m_rel_bias, m_w_branch_a, m_w_branch_b, m_w_out, m_g_pre_cross, m_g_post_cross, m_g_mem, m_w_q_mem, m_w_kv_mem, m_w_o_mem, m_g_pre_ffn, m_g_post_ffn, m_w_ffn_in, m_w_ffn_out, v_g_pre_mix, v_g_post_mix, v_w_in, v_shift_mix, v_decay_base, v_decay_up, v_iclr_base, v_iclr_up, v_gate_up, v_key_norm_scale, v_key_iclr_scale, v_bonus_scale, v_lnx_w, v_lnx_b, v_rel_bias, v_w_branch_a, v_w_branch_b, v_w_out, v_g_pre_cross, v_g_post_cross, v_g_mem, v_w_q_mem, v_w_kv_mem, v_w_o_mem, v_g_pre_ffn, v_g_post_ffn, v_w_ffn_in, v_w_ffn_out)
    return _step(args, x.shape[1], mem.shape[1])
```

```python
import functools
import math

import jax
import jax.numpy as jnp
from jax import lax
from jax.experimental import pallas as pl
from jax.experimental.pallas import tpu as pltpu

F32 = jnp.float32
BF16 = jnp.bfloat16

N_DEV = 8
D = 1024
HEAD = 64
N_HEADS = D // HEAD
LANE = 128
CHUNK = 64
LEFT = 8 * CHUNK
BAND = LEFT + CHUNK
REL_CLIP = 128
REL_TABLE = CHUNK + REL_CLIP
MEM_WIDTH = D // 2
MEM_HEADS = 4
FFN = 2816
LORA_W, LORA_A, LORA_G = 64, 64, 160
P_WIDTH = 3 * D + 3 * D + 2 * D + 128 + 128 + 256
C_Q, C_GA, C_LORA = 3 * D, 6 * D, 8 * D
NORM_EPS = 1e-6
GROUP_NORM_EPS = 64e-5
MASK_VALUE = -1e30
ADAM_LR, ADAM_B1, ADAM_B2, ADAM_EPS, ADAM_WD, ADAM_STEP = 0.001, 0.9, 0.999, 1e-08, 0.01, 10
VMEM_LIMIT = 56 * 1024 * 1024


def _cp(*sem):
    return pltpu.CompilerParams(dimension_semantics=sem, vmem_limit_bytes=VMEM_LIMIT)


_NN, _NT, _TN = ((1,), (0,)), ((1,), (1,)), ((0,), (0,))


def _dot_raw(a, b, dims):
    return lax.dot_general(a.astype(BF16), b.astype(BF16), (dims, ((), ())), preferred_element_type=F32)


@functools.partial(jax.custom_vjp, nondiff_argnums=(2,))
def _dot_dims(a, b, dims):
    return _dot_raw(a, b, dims)


def _dot_dims_fwd(a, b, dims):
    return _dot_raw(a, b, dims), (a, b)


def _dot_dims_bwd(dims, res, g):
    a, b = res
    if dims == _NN:
        da, db = _dot_raw(g, b, _NT), _dot_raw(a, g, _TN)
    elif dims == _NT:
        da, db = _dot_raw(g, b, _NN), _dot_raw(g, a, _TN)
    else:
        da, db = _dot_raw(b, g, _NT), _dot_raw(a, g, _NN)
    return da.astype(a.dtype), db.astype(b.dtype)


_dot_dims.defvjp(_dot_dims_fwd, _dot_dims_bwd)


def _dot(a, b, dims=_NN):
    return _dot_dims(a, b, dims)


def _dot_nt(a, b):
    return _dot_dims(a, b, _NT)


def _dot_tn(a, b):
    return _dot_dims(a, b, _TN)


def _split(x, terms):
    parts, rest = [], x.astype(F32)
    for _ in range(terms):
        p = rest.astype(BF16)
        parts.append(p)
        rest = rest - p.astype(F32)
    return parts


def _dot_split_a(a, b, terms=2):
    out = None
    for p in _split(a, terms):
        t = _dot(p, b)
        out = t if out is None else out + t
    return out


def _dot_split_b(a, b, terms=3):
    out = None
    for p in _split(b, terms):
        t = _dot(a, p)
        out = t if out is None else out + t
    return out


MM_VMEM_BUDGET = 30 * 1024 * 1024
MM_HBM_BPS = 3.2e12
MM_MXU_FPS = 8.5e14
MM_STEP_S = 0.35e-6


def _divisors(n, align, cap):
    out = [d for d in range(align, min(n, cap) + 1, align) if n % d == 0]
    return out or [n]


def _mm_tiles(m, n, k, ea, eb, eo, ta):
    best = None
    for tm in _divisors(m, LANE if ta else 8, 2048):
        for tn in _divisors(n, LANE, 2048):
            for tk in _divisors(k, LANE, 2048):
                nk = k // tk
                vmem = 2 * (tm * tk * ea + tk * tn * eb + tm * tn * eo) + (tm * tn * 4 if nk > 1 else 0)
                if vmem > MM_VMEM_BUDGET:
                    continue
                dma = (tm * tk * ea if (nk > 1 or n // tn == 1) else tm * tk * ea * tn / n) + tk * tn * eb + tm * tn * eo / nk
                step = max(2.0 * tm * tn * tk / MM_MXU_FPS, dma / MM_HBM_BPS) + MM_STEP_S
                cost = (m // tm) * (n // tn) * nk * step
                if best is None or cost < best[0]:
                    best = (cost, tm, tn, tk)
    return best[1:]


def _mm(a, b, *, name, ta=False, tb=False, out_dtype=F32, tm=None, tn=None, tk=None, split_a=1, after=None):
    m, k = (a.shape[1], a.shape[0]) if ta else a.shape
    n, kb = (b.shape[0], b.shape[1]) if tb else (b.shape[1], b.shape[0])
    assert k == kb, (a.shape, b.shape, ta, tb)
    if tm is None:
        tm, tn, tk = _mm_tiles(m, n, k, a.dtype.itemsize, b.dtype.itemsize, jnp.dtype(out_dtype).itemsize, ta)
    assert m % tm == 0 and n % tn == 0 and k % tk == 0, (m, n, k, tm, tn, tk)
    nk = k // tk
    dims = ((0 if ta else 1,), (1 if tb else 0,))

    n_after = 0 if after is None else 1

    def body(a_ref, b_ref, *rest):
        o_ref, scratch = rest[n_after], rest[n_after + 1:]
        prod = None
        for p in _split(a_ref[...], split_a) if split_a > 1 else [a_ref[...]]:
            t = _dot_raw(p, b_ref[...], dims)
            prod = t if prod is None else prod + t
        if nk == 1:
            o_ref[...] = prod.astype(o_ref.dtype)
            return
        acc_ref, kk = scratch[0], pl.program_id(2)

        @pl.when(kk == 0)
        def _():
            acc_ref[...] = prod

        @pl.when(kk > 0)
        def _():
            acc_ref[...] += prod

        @pl.when(kk == nk - 1)
        def _():
            o_ref[...] = acc_ref[...].astype(o_ref.dtype)

    a_spec = pl.BlockSpec((tk, tm), lambda i, j, q: (q, i)) if ta else pl.BlockSpec((tm, tk), lambda i, j, q: (i, q))
    b_spec = pl.BlockSpec((tn, tk), lambda i, j, q: (j, q)) if tb else pl.BlockSpec((tk, tn), lambda i, j, q: (q, j))
    return pl.pallas_call(
        body, name=name, grid=(m // tm, n // tn, nk),
        in_specs=[a_spec, b_spec] + [pl.BlockSpec(memory_space=pl.ANY)] * n_after,
        out_specs=pl.BlockSpec((tm, tn), lambda i, j, q: (i, j)),
        out_shape=jax.ShapeDtypeStruct((m, n), out_dtype),
        scratch_shapes=[pltpu.VMEM((tm, tn), F32)] if nk > 1 else [],
        compiler_params=_cp("parallel", "parallel", "arbitrary"),
    )(a, b, *([] if after is None else [after]))


def _piece_steps(pieces, tile):
    counts = [p.shape[1] // tile for p in pieces]
    assert all(p.shape[1] % tile == 0 for p in pieces)
    return [(sum(counts[:i]), c) for i, c in enumerate(counts)], sum(counts)


def _mm_cat_nn(pieces, w, *, name, after=None, tm=2048, tk=256):
    t, n = pieces[0].shape[0], w.shape[1]
    tm = min(tm, t)
    spans, nk = _piece_steps(pieces, tk)
    npc = len(pieces)
    n_after = 0 if after is None else 1

    def body(*refs):
        w_ref, o_ref, acc_ref = refs[npc], refs[npc + 1 + n_after], refs[npc + 2 + n_after]
        q = pl.program_id(1)

        @pl.when(q == 0)
        def _():
            acc_ref[...] = jnp.zeros_like(acc_ref)

        for p_ref, (first, count) in zip(refs[:npc], spans):
            @pl.when(jnp.logical_and(q >= first, q < first + count))
            def _(p_ref=p_ref):
                acc_ref[...] += _dot_raw(p_ref[...], w_ref[...], _NN)

        @pl.when(q == nk - 1)
        def _():
            o_ref[...] = acc_ref[...].astype(o_ref.dtype)

    def piece_spec(first, count):
        return pl.BlockSpec((tm, tk), lambda i, q: (i, jnp.clip(q - first, 0, count - 1)))

    return pl.pallas_call(
        body, name=name, grid=(t // tm, nk),
        in_specs=[piece_spec(*s) for s in spans] + [pl.BlockSpec((tk, n), lambda i, q: (q, 0))]
        + [pl.BlockSpec(memory_space=pl.ANY)] * n_after,
        out_specs=pl.BlockSpec((tm, n), lambda i, q: (i, 0)),
        out_shape=jax.ShapeDtypeStruct((t, n), BF16),
        scratch_shapes=[pltpu.VMEM((tm, n), F32)],
        compiler_params=_cp("parallel", "arbitrary"),
    )(*pieces, w, *([] if after is None else [after]))


def _mm_cat_tn(pieces, a, *, name, after=None, tk=1024, tn=512):
    t, m = a.shape
    tk = min(tk, t)
    spans, nj = _piece_steps(pieces, tn)
    npc, nk = len(pieces), t // tk
    n_after = 0 if after is None else 1

    def body(a_ref, *refs):
        o_ref, acc_ref = refs[npc + n_after], refs[npc + 1 + n_after]
        j, q = pl.program_id(0), pl.program_id(1)

        @pl.when(q == 0)
        def _():
            acc_ref[...] = jnp.zeros_like(acc_ref)

        for p_ref, (first, count) in zip(refs[:npc], spans):
            @pl.when(jnp.logical_and(j >= first, j < first + count))
            def _(p_ref=p_ref):
                acc_ref[...] += _dot_raw(p_ref[...], a_ref[...], _TN)

        @pl.when(q == nk - 1)
        def _():
            o_ref[...] = acc_ref[...].astype(o_ref.dtype)

    def piece_spec(first, count):
        def index(j, q):
            mine = jnp.logical_and(j >= first, j < first + count)
            return jnp.where(mine, q, 0), jnp.clip(j - first, 0, count - 1)
        return pl.BlockSpec((tk, tn), index)

    return pl.pallas_call(
        body, name=name, grid=(nj, nk),
        in_specs=[pl.BlockSpec((tk, m), lambda j, q: (q, 0))] + [piece_spec(*s) for s in spans]
        + [pl.BlockSpec(memory_space=pl.ANY)] * n_after,
        out_specs=pl.BlockSpec((tn, m), lambda j, q: (j, 0)),
        out_shape=jax.ShapeDtypeStruct((nj * tn, m), BF16),
        scratch_shapes=[pltpu.VMEM((tn, m), F32)],
        compiler_params=_cp("parallel", "arbitrary"),
    )(a, *pieces, *([] if after is None else [after]))


def _win(arr, start=0, width=None):
    width = arr.shape[1] if width is None else width
    assert start % width == 0
    return (arr, start // width, width)


def _row_specs(rows, tm):
    return [pl.BlockSpec((tm, w), functools.partial(lambda i, cb: (i, cb), cb=cb)) for (_, cb, w) in rows]


def _full_spec(p):
    nd = p.ndim
    return pl.BlockSpec(p.shape, lambda i, nd=nd: (0,) * nd)


def _rowwise(fn, rows, params, outs, *, name, tm, after=None):
    t = rows[0][0].shape[0]
    tm = min(tm, t)
    assert t % tm == 0
    nr, npar = len(rows), len(params)
    n_after = 0 if after is None else 1

    def body(*refs):
        vals = [r[...] for r in refs[:nr + npar]]
        res = fn(*vals)
        for o_ref, r in zip(refs[nr + npar + n_after:], res):
            o_ref[...] = r.astype(o_ref.dtype)

    return pl.pallas_call(
        body, name=name, grid=(t // tm,),
        in_specs=_row_specs(rows, tm) + [_full_spec(p) for p in params] + [pl.BlockSpec(memory_space=pl.ANY)] * n_after,
        out_specs=[pl.BlockSpec((tm, w), lambda i: (i, 0)) for (w, _) in outs],
        out_shape=[jax.ShapeDtypeStruct((t, w), dt) for (w, dt) in outs],
        compiler_params=_cp("parallel"),
    )(*[r[0] for r in rows], *params, *([] if after is None else [after]))


def _rowwise_bwd(fn, rows, params, n_const, cots, *, name, tm, row_grad, add_to=None, packed=False):
    t = rows[0][0].shape[0]
    tm = min(tm, t)
    assert t % tm == 0
    nr, npar = len(rows), len(params)
    ndp = npar - n_const
    add_to = add_to or {}
    add_idx = sorted(add_to)
    flat_cots = [c for group in cots for c in group]
    kept = [i for i in range(nr) if row_grad[i] is not None]

    def body(*refs):
        pos = 0
        row_v = [r[...] for r in refs[pos:pos + nr]]; pos += nr
        par_v = [r[...] for r in refs[pos:pos + npar]]; pos += npar
        cot_v = [r[...] for r in refs[pos:pos + len(flat_cots)]]; pos += len(flat_cots)
        add_v = [r[...] for r in refs[pos:pos + len(add_idx)]]; pos += len(add_idx)
        if packed:
            offs = [sum(rows[i][2] for i in kept[:q]) for q in range(len(kept))]
            rg_refs = [refs[pos].at[:, o:o + rows[i][2]] for o, i in zip(offs, kept)]; pos += 1
        else:
            rg_refs = refs[pos:pos + len(kept)]; pos += len(kept)
        pg_refs = refs[pos:pos + ndp]

        consts = par_v[ndp:]
        res, vjp = jax.vjp(lambda *args: tuple(fn(*args, *consts)), *row_v, *par_v[:ndp])
        cot_in, q = [], 0
        for j, group in enumerate(cots):
            c = None
            for _ in group:
                cv = cot_v[q].astype(F32); q += 1
                c = cv if c is None else c + cv
            c = jnp.zeros(res[j].shape, F32) if c is None else c
            cot_in.append(c.astype(res[j].dtype))
        grads = vjp(tuple(cot_in))
        for ref, i in zip(rg_refs, kept):
            g = grads[i].astype(F32)
            if i in add_to:
                g = g + add_v[add_idx.index(i)].astype(F32)
            ref[...] = g.astype(ref.dtype)

        @pl.when(pl.program_id(0) == 0)
        def _():
            for ref in pg_refs:
                ref[...] = jnp.zeros_like(ref)

        for ref, g in zip(pg_refs, grads[nr:]):
            ref[...] += g.astype(F32)

    cot_specs = [pl.BlockSpec((tm, c.shape[1]), lambda i: (i, 0)) for c in flat_cots]
    add_specs = [pl.BlockSpec((tm, add_to[i].shape[1]), lambda i_: (i_, 0)) for i in add_idx]
    widths = [sum(rows[i][2] for i in kept)] if packed else [rows[i][2] for i in kept]
    n_rg = len(widths)
    out_specs = [pl.BlockSpec((tm, w), lambda i_: (i_, 0)) for w in widths] + [_full_spec(p) for p in params[:ndp]]
    out_shape = [jax.ShapeDtypeStruct((t, w), row_grad[kept[q]]) for q, w in enumerate(widths)] + [
        jax.ShapeDtypeStruct(p.shape, F32) for p in params[:ndp]]
    res = pl.pallas_call(
        body, name=name, grid=(t // tm,),
        in_specs=_row_specs(rows, tm) + [_full_spec(p) for p in params] + cot_specs + add_specs,
        out_specs=out_specs, out_shape=out_shape,
        compiler_params=_cp("arbitrary"),
    )(*[r[0] for r in rows], *params, *flat_cots, *[add_to[i] for i in add_idx])
    return list(res[:n_rg]), list(res[n_rg:])


def _rms(x, g):
    xf = x.astype(F32)
    return xf * lax.rsqrt(jnp.mean(xf * xf, axis=-1, keepdims=True) + NORM_EPS) * g


def _softplus(x):
    return jnp.maximum(x, 0.0) + jnp.log(1.0 + jnp.exp(-jnp.abs(x)))


def _fn_pre(x, g):
    return (_rms(x, g).astype(BF16),)


def _fn_res(x, u, g_post):
    return (x + _rms(u, g_post),)


def _fn_res_pre(x, u, g_post, g_pre):
    xn = x + _rms(u, g_post)
    return xn, _rms(xn, g_pre).astype(BF16)


def _fn_mix(zga, zgb, ya, yb):
    return ((jax.nn.sigmoid(zga) * ya + jax.nn.sigmoid(zgb) * yb).astype(BF16),)


def _fn_swiglu(gate, up):
    gate, up = gate.astype(F32), up.astype(F32)
    return ((gate * jax.nn.sigmoid(gate) * up).astype(BF16),)


def _fn_prep(zk, zw, za, zg, decay_base, d_up, iclr_base, i_up, g_up, kns, kis, e_hd, e_dh):
    w_log = -_softplus(-(decay_base + _dot(jnp.tanh(zw), d_up))) - 0.5
    lw = -jnp.exp(w_log)
    a = jax.nn.sigmoid(iclr_base + _dot(za, i_up))
    g = _dot(jax.nn.sigmoid(zg), g_up)
    kn = zk * kns
    ss = _dot(kn * kn, e_dh)
    inv = lax.rsqrt(jnp.maximum(ss, 1e-24))
    kk = kn * _dot_split_a(inv, e_hd)
    k2 = zk * (1.0 + (a - 1.0) * kis)
    return lw, k2, kk, a, g


def _fn_post(y, r, k2, v, g, lnx_w, lnx_b, bonus, e_hd, e_dh):
    mu = _dot_split_a(_dot(y, e_dh) * (1.0 / HEAD), e_hd)
    yc = y - mu
    var = _dot(yc * yc, e_dh) * (1.0 / HEAD)
    yn = yc * _dot_split_a(lax.rsqrt(var + GROUP_NORM_EPS), e_hd)
    bs = _dot_split_a(_dot(r * k2 * bonus, e_dh), e_hd)
    return (((yn * lnx_w + lnx_b + bs * v) * g).astype(BF16),)


def _shift_fwd(p, col0, ncols, mix, seq, *, name, cw=256):
    t = p.shape[0]
    assert col0 % cw == 0 and ncols % cw == 0 and t % seq == 0
    cb0 = col0 // cw

    def body(p_ref, m_ref, z_ref):
        pv = p_ref[...]
        row = lax.broadcasted_iota(jnp.int32, pv.shape, 0)
        prev = jnp.where(row == 0, 0.0, pltpu.roll(pv, 1, axis=0))
        z_ref[...] = pv + (prev - pv) * m_ref[...]

    return pl.pallas_call(
        body, name=name, grid=(t // seq, ncols // cw),
        in_specs=[pl.BlockSpec((seq, cw), lambda b, c: (b, c + cb0)), pl.BlockSpec((1, cw), lambda b, c: (0, c))],
        out_specs=pl.BlockSpec((seq, cw), lambda b, c: (b, c)),
        out_shape=jax.ShapeDtypeStruct((t, ncols), F32),
        compiler_params=_cp("parallel", "parallel"),
    )(p, mix)


def _shift_bwd(p, col0, ncols, mix, dz_parts, seq, *, name, cw=256):
    t = p.shape[0]
    cb0 = col0 // cw
    n = len(dz_parts)

    def body(*refs):
        p_ref, m_ref = refs[:2]
        dp_ref, dm_ref = refs[2 + n:]
        dz = refs[2][...].astype(F32)
        for r in refs[3:2 + n]:
            dz = dz + r[...].astype(F32)
        pv = p_ref[...]
        mixv = m_ref[...]
        row = lax.broadcasted_iota(jnp.int32, pv.shape, 0)
        prev = jnp.where(row == 0, 0.0, pltpu.roll(pv, 1, axis=0))
        u = dz * mixv
        nxt = jnp.where(row == seq - 1, 0.0, pltpu.roll(u, seq - 1, axis=0))
        dp_ref[...] = (dz - u + nxt).astype(dp_ref.dtype)

        @pl.when(pl.program_id(1) == 0)
        def _():
            dm_ref[...] = jnp.zeros_like(dm_ref)

        dm_ref[...] += jnp.sum(dz * (prev - pv), axis=0, keepdims=True)

    return pl.pallas_call(
        body, name=name, grid=(ncols // cw, t // seq),
        in_specs=[pl.BlockSpec((seq, cw), lambda c, b: (b, c + cb0)), pl.BlockSpec((1, cw), lambda c, b: (0, c))]
        + [pl.BlockSpec((seq, cw), lambda c, b: (b, c))] * n,
        out_specs=[pl.BlockSpec((seq, cw), lambda c, b: (b, c)), pl.BlockSpec((1, cw), lambda c, b: (0, c))],
        out_shape=[jax.ShapeDtypeStruct((t, ncols), BF16), jax.ShapeDtypeStruct((1, ncols), F32)],
        compiler_params=_cp("parallel", "arbitrary"),
    )(p, mix, *dz_parts)


def _each(f, *lists):
    return [f(*xs) for xs in zip(*lists)]


def _tri_inv(low):
    c = low[0].shape[0]
    ti = lax.broadcasted_iota(jnp.int32, (c, c), 0)
    si = lax.broadcasted_iota(jnp.int32, (c, c), 1)
    eye = (ti == si).astype(F32)
    inside = (ti // 4) == (si // 4)
    base = [jnp.where(inside, m, 0.0) for m in low]
    acc = _each(lambda m: _dot(eye - m, eye + _dot(m, m)), base)
    size = 8
    while size <= c:
        wider = (ti // size) == (si // size)
        keep = jnp.logical_and(wider, jnp.logical_not(inside))
        acc = _each(lambda p, m: p - _dot(_dot(p, jnp.where(keep, m, 0.0)), p), acc, low)
        inside, size = wider, size * 2
    return acc


def _stack_rows(a, b):
    return jnp.concatenate([a, b], axis=0)


@jax.custom_vjp
def _split_rows(x):
    h = x.shape[0] // 2
    return x[:h], x[h:]


def _split_rows_fwd(x):
    return _split_rows(x), None


def _split_rows_bwd(_, g):
    return (jnp.concatenate(g, axis=0),)


_split_rows.defvjp(_split_rows_fwd, _split_rows_bwd)


def _masked_halves(stacked, top_mask, bottom_mask):
    halves = _each(_split_rows, stacked)
    return ([jnp.where(top_mask, t, 0.0) for t, _ in halves], [jnp.where(bottom_mask, b, 0.0) for _, b in halves])


@jax.custom_vjp
def _tri_inv_known(low, inv):
    return inv


def _tri_inv_known_fwd(low, inv):
    return inv, inv


def _tri_inv_known_bwd(inv, g):
    dlow = _each(lambda t, gg: -_dot(_dot(t, gg, _TN), t, _NT), inv, g)
    return dlow, _each(jnp.zeros_like, inv)


_tri_inv_known.defvjp(_tri_inv_known_fwd, _tri_inv_known_bwd)


def _tri_ones(c):
    return (lax.broadcasted_iota(jnp.int32, (c, c), 0) >= lax.broadcasted_iota(jnp.int32, (c, c), 1)).astype(F32)


def _cumsum_rows(lw):
    return _dot_split_b(_tri_ones(lw.shape[0]), lw, 3)


def _wkv_chunk(s0, r, lw, cum, k, v, kk, a, inv=None):
    c = r[0].shape[0]
    ti = lax.broadcasted_iota(jnp.int32, (c, c), 0)
    si = lax.broadcasted_iota(jnp.int32, (c, c), 1)
    incl, strict = ti >= si, ti > si
    eg = _each(jnp.exp, cum)
    egp = _each(lambda cs, x: jnp.exp(cs - x), cum, lw)
    ei = _each(lambda cs: jnp.exp(-cs), cum)
    rh, kkh, kt = _each(jnp.multiply, r, eg), _each(jnp.multiply, kk, egp), _each(jnp.multiply, k, ei)
    bt = _each(lambda p, q, e: (p * q) * e, a, kk, ei)
    both = _each(_stack_rows, kkh, rh)
    on_b, on_k, on_s = _each(_dot_nt, both, bt), _each(_dot_nt, both, kt), _each(_dot_nt, both, s0)
    lb, mb = _masked_halves(on_b, strict, incl)
    lk, mk = _masked_halves(on_k, strict, incl)
    on_s = _each(_split_rows, on_s)
    on_v = _each(lambda p, q, x: _split_rows(_dot(_stack_rows(p, q), x)), lk, mk, v)
    rhs = _each(lambda p, q: p[0] + q[0], on_s, on_v)
    inv = _tri_inv(lb) if inv is None else _tri_inv_known(lb, inv)
    u = _each(lambda t, x: -_dot(t, x), inv, rhs)
    y = _each(lambda p, m1, uu, q: p[1] + _dot(m1, uu) + q[1], on_s, mb, u, on_v)
    s1 = _each(lambda s, uu, x, b, kq, w: (s + _dot_tn(_stack_rows(uu, x), _stack_rows(b, kq)))
               * jnp.exp(jnp.sum(w, axis=0, keepdims=True)), s0, u, v, bt, kt, lw)
    return y, s1, inv


WKV_HEADS = 16
WKV_COLS = WKV_HEADS * HEAD
WKV_GROUPS = N_HEADS // WKV_HEADS


def _head_cols(ref):
    return [ref[:, h * HEAD:(h + 1) * HEAD] for h in range(ref.shape[1] // HEAD)]


def _wkv_specs(seq, rev):
    nc = seq // CHUNK

    def rows(col0):
        cb0 = col0 // WKV_COLS
        if rev:
            return pl.BlockSpec((CHUNK, WKV_COLS), lambda b, h, c: (b * nc + nc - 1 - c, cb0 + h))
        return pl.BlockSpec((CHUNK, WKV_COLS), lambda b, h, c: (b * nc + c, cb0 + h))

    if rev:
        st = pl.BlockSpec((1, 1, WKV_HEADS, HEAD, HEAD), lambda b, h, c: (b * WKV_GROUPS + h, nc - 1 - c, 0, 0, 0))
    else:
        st = pl.BlockSpec((1, 1, WKV_HEADS, HEAD, HEAD), lambda b, h, c: (b * WKV_GROUPS + h, c, 0, 0, 0))
    return rows, st


def _wkv_fwd(z_rkv, lw, k2, kk, a, seq):
    t = z_rkv.shape[0]
    nb, nc = t // seq, seq // CHUNK
    rows, st = _wkv_specs(seq, False)

    def body(r_ref, v_ref, lw_ref, k_ref, kk_ref, a_ref, y_ref, st_ref, inv_ref, s_scr, cum_scr):
        @pl.when(pl.program_id(2) == 0)
        def _():
            s_scr[...] = jnp.zeros_like(s_scr)

        cum_scr[...] = _cumsum_rows(lw_ref[...])
        s0 = [s_scr[h] for h in range(WKV_HEADS)]
        y, s1, inv = _wkv_chunk(s0, *[_head_cols(ref) for ref in (r_ref, lw_ref, cum_scr, k_ref, v_ref, kk_ref, a_ref)])
        for h in range(WKV_HEADS):
            st_ref[0, 0, h] = s0[h]
            inv_ref[0, 0, h] = inv[h].astype(inv_ref.dtype)
            y_ref[:, h * HEAD:(h + 1) * HEAD] = y[h]
            s_scr[h] = s1[h]

    per_chunk = jax.ShapeDtypeStruct((nb * WKV_GROUPS, nc, WKV_HEADS, HEAD, HEAD), F32)
    return pl.pallas_call(
        body, name="wkv_fwd", grid=(nb, WKV_GROUPS, nc),
        in_specs=[rows(0), rows(2 * D), rows(0), rows(0), rows(0), rows(0)],
        out_specs=[rows(0), st, st],
        out_shape=[jax.ShapeDtypeStruct((t, D), F32), per_chunk, jax.ShapeDtypeStruct(per_chunk.shape, BF16)],
        scratch_shapes=[pltpu.VMEM((WKV_HEADS, HEAD, HEAD), F32), pltpu.VMEM((CHUNK, WKV_COLS), F32)],
        compiler_params=_cp("parallel", "parallel", "arbitrary"),
    )(z_rkv, z_rkv, lw, k2, kk, a)


def _wkv_bwd(z_rkv, lw, k2, kk, a, states, invs, dy, seq):
    t = z_rkv.shape[0]
    nb, nc = t // seq, seq // CHUNK
    rows, st = _wkv_specs(seq, True)

    def body(r_ref, v_ref, lw_ref, k_ref, kk_ref, a_ref, st_ref, inv_ref, dy_ref,
             dr_ref, dlw_ref, dk_ref, dv_ref, dkk_ref, da_ref, ds_scr, cum_scr, dlw_scr):
        @pl.when(pl.program_id(2) == 0)
        def _():
            ds_scr[...] = jnp.zeros_like(ds_scr)

        cum_scr[...] = _cumsum_rows(lw_ref[...])
        s0 = [st_ref[0, 0, h] for h in range(WKV_HEADS)]
        inv = [inv_ref[0, 0, h] for h in range(WKV_HEADS)]
        _, vjp = jax.vjp(lambda *args: _wkv_chunk(*args, inv=inv)[:2],
                         s0, *[_head_cols(ref) for ref in (r_ref, lw_ref, cum_scr, k_ref, v_ref, kk_ref, a_ref)])
        ds0, dr, dlw, dcum, dk, dv, dkk, da = vjp(
            ([x.astype(F32) for x in _head_cols(dy_ref)], [ds_scr[h] for h in range(WKV_HEADS)]))
        for h in range(WKV_HEADS):
            sl = slice(h * HEAD, (h + 1) * HEAD)
            ds_scr[h] = ds0[h]
            dlw_scr[:, sl] = dlw[h]
            cum_scr[:, sl] = dcum[h]
            for ref, g in zip((dr_ref, dk_ref, dv_ref, dkk_ref, da_ref), (dr, dk, dv, dkk, da)):
                ref[:, sl] = g[h].astype(ref.dtype)
        dlw_ref[...] = (dlw_scr[...] + _dot_raw(_tri_ones(CHUNK), cum_scr[...], _TN)).astype(dlw_ref.dtype)

    return pl.pallas_call(
        body, name="wkv_bwd", grid=(nb, WKV_GROUPS, nc),
        in_specs=[rows(0), rows(2 * D), rows(0), rows(0), rows(0), rows(0), st, st, rows(0)],
        out_specs=[rows(0)] * 6,
        out_shape=[jax.ShapeDtypeStruct((t, D), BF16)] * 6,
        scratch_shapes=[pltpu.VMEM((WKV_HEADS, HEAD, HEAD), F32)] + [pltpu.VMEM((CHUNK, WKV_COLS), F32)] * 2,
        compiler_params=_cp("parallel", "parallel", "arbitrary"),
    )(z_rkv, z_rkv, lw, k2, kk, a, states, invs, dy)


def _softmax(s):
    e = jnp.exp(s - jnp.max(s, axis=-1, keepdims=True))
    return e * (1.0 / jnp.sum(e, axis=-1, keepdims=True))


ATT_FWD_HEADS = 8
ATT_HEADS = 4
ATT_COLS = ATT_HEADS * HEAD
ATT_GROUPS = N_HEADS // ATT_HEADS


def _attn_chunk(q, kb, vb, bias, valid):
    s = _each(lambda x, y, z: jnp.where(valid, _dot_nt(x * (HEAD ** -0.5), y) + z, MASK_VALUE), q, kb, bias)
    return _each(_dot, _each(_softmax, s), vb)


def _pad_fill(pad_ref, src_ref):
    pad_ref[0:LEFT, :] = jnp.zeros((LEFT, pad_ref.shape[1]), pad_ref.dtype)
    pad_ref[LEFT:, :] = src_ref[...].astype(pad_ref.dtype)


QBLK = 2 * CHUNK
BAND2 = LEFT + QBLK


def _band_heads(pad_ref, start):
    return [pad_ref[pl.ds(start, BAND2), h * HEAD:(h + 1) * HEAD] for h in range(pad_ref.shape[1] // HEAD)]


def _band_valid(c):
    return (c * QBLK - LEFT + lax.broadcasted_iota(jnp.int32, (1, BAND2), 1)) >= 0


def _pair_bias(bias):
    off = jnp.full(bias.shape[:2] + (CHUNK,), MASK_VALUE, bias.dtype)
    return jnp.concatenate([jnp.concatenate([bias, off], axis=2), jnp.concatenate([off, bias], axis=2)], axis=1)


def _unpair_bias(g):
    return g[:, :CHUNK, :BAND] + g[:, CHUNK:, CHUNK:]


def _bias_spec():
    return pl.BlockSpec((ATT_HEADS, QBLK, BAND2), lambda h, b, c: (h, 0, 0))


def _attn_fwd(proj, bias, seq):
    t = proj.shape[0]
    nb, nc = t // seq, seq // QBLK
    heads = ATT_FWD_HEADS
    cols, groups = heads * HEAD, N_HEADS // heads
    cq = C_Q // cols

    def body(q_ref, k_ref, v_ref, b_ref, o_ref, kpad, vpad):
        c = pl.program_id(2)

        @pl.when(c == 0)
        def _():
            _pad_fill(kpad, k_ref)
            _pad_fill(vpad, v_ref)

        start = pl.multiple_of(c * QBLK, QBLK)
        o = _attn_chunk(_head_cols(q_ref), _band_heads(kpad, start), _band_heads(vpad, start),
                        [b_ref[h] for h in range(heads)], _band_valid(c))
        for h in range(heads):
            o_ref[:, h * HEAD:(h + 1) * HEAD] = o[h].astype(o_ref.dtype)

    return pl.pallas_call(
        body, name="attn_fwd", grid=(groups, nb, nc),
        in_specs=[pl.BlockSpec((QBLK, cols), lambda h, b, c: (b * nc + c, cq + h)),
                  pl.BlockSpec((seq, cols), lambda h, b, c: (b, cq + groups + h)),
                  pl.BlockSpec((seq, cols), lambda h, b, c: (b, cq + 2 * groups + h)),
                  pl.BlockSpec((heads, QBLK, BAND2), lambda h, b, c: (h, 0, 0))],
        out_specs=pl.BlockSpec((QBLK, cols), lambda h, b, c: (b * nc + c, h)),
        out_shape=jax.ShapeDtypeStruct((t, D), BF16),
        scratch_shapes=[pltpu.VMEM((seq + LEFT, cols), BF16)] * 2,
        compiler_params=_cp("parallel", "arbitrary", "arbitrary"),
    )(proj, proj, proj, bias)


def _attn_bwd(proj, bias, do, seq):
    t = proj.shape[0]
    nb, nc = t // seq, seq // QBLK
    cq = C_Q // ATT_COLS

    def body(q_ref, k_ref, v_ref, b_ref, do_ref, dq_ref, dk_ref, dv_ref, db_ref, kpad, vpad, dkpad, dvpad):
        b, c = pl.program_id(1), pl.program_id(2)

        @pl.when(c == 0)
        def _():
            _pad_fill(kpad, k_ref)
            _pad_fill(vpad, v_ref)
            dkpad[...] = jnp.zeros_like(dkpad)
            dvpad[...] = jnp.zeros_like(dvpad)

        @pl.when(jnp.logical_and(b == 0, c == 0))
        def _():
            db_ref[...] = jnp.zeros_like(db_ref)

        start = pl.multiple_of(c * QBLK, QBLK)
        _, vjp = jax.vjp(functools.partial(_attn_chunk, valid=_band_valid(c)),
                         _head_cols(q_ref), _band_heads(kpad, start), _band_heads(vpad, start),
                         [b_ref[h] for h in range(ATT_HEADS)])
        dq, dkb, dvb, dbias = vjp([x.astype(F32) for x in _head_cols(do_ref)])
        for h in range(ATT_HEADS):
            sl = slice(h * HEAD, (h + 1) * HEAD)
            dq_ref[:, sl] = dq[h].astype(dq_ref.dtype)
            dkpad[pl.ds(start, BAND2), sl] += dkb[h].astype(F32)
            dvpad[pl.ds(start, BAND2), sl] += dvb[h].astype(F32)
            db_ref[h] += dbias[h]

        @pl.when(c == nc - 1)
        def _():
            dk_ref[...] = dkpad[LEFT:, :].astype(dk_ref.dtype)
            dv_ref[...] = dvpad[LEFT:, :].astype(dv_ref.dtype)

    kv_out = pl.BlockSpec((seq, ATT_COLS), lambda h, b, c: (b, h))
    return pl.pallas_call(
        body, name="attn_bwd", grid=(ATT_GROUPS, nb, nc),
        in_specs=[pl.BlockSpec((QBLK, ATT_COLS), lambda h, b, c: (b * nc + c, cq + h)),
                  pl.BlockSpec((seq, ATT_COLS), lambda h, b, c: (b, cq + ATT_GROUPS + h)),
                  pl.BlockSpec((seq, ATT_COLS), lambda h, b, c: (b, cq + 2 * ATT_GROUPS + h)),
                  _bias_spec(),
                  pl.BlockSpec((QBLK, ATT_COLS), lambda h, b, c: (b * nc + c, h))],
        out_specs=[pl.BlockSpec((QBLK, ATT_COLS), lambda h, b, c: (b * nc + c, h)), kv_out, kv_out, _bias_spec()],
        out_shape=[jax.ShapeDtypeStruct((t, D), BF16)] * 3 + [jax.ShapeDtypeStruct((N_HEADS, QBLK, BAND2), F32)],
        scratch_shapes=[pltpu.VMEM((seq + LEFT, ATT_COLS), BF16)] * 2 + [pltpu.VMEM((seq + LEFT, ATT_COLS), F32)] * 2,
        compiler_params=_cp("parallel", "arbitrary", "arbitrary"),
    )(proj, proj, proj, bias, do)


def _xattn_tile(q, k, v):
    s = _dot_nt(q, k) * ((MEM_WIDTH // MEM_HEADS) ** -0.5)
    return _dot(_softmax(s), v)


def _xattn_fwd(qm, kvm, seq, n_mem, tq=2048):
    t = qm.shape[0]
    tq = min(tq, seq)
    nb, nq = t // seq, seq // tq

    def body(q_ref, k_ref, v_ref, o_ref):
        o_ref[...] = _xattn_tile(q_ref[...], k_ref[...], v_ref[...]).astype(o_ref.dtype)

    return pl.pallas_call(
        body, name="xattn_fwd", grid=(nb, MEM_HEADS, nq),
        in_specs=[pl.BlockSpec((tq, LANE), lambda b, h, i: (b * nq + i, h)),
                  pl.BlockSpec((n_mem, LANE), lambda b, h, i: (b, h)),
                  pl.BlockSpec((n_mem, LANE), lambda b, h, i: (b, MEM_HEADS + h))],
        out_specs=pl.BlockSpec((tq, LANE), lambda b, h, i: (b * nq + i, h)),
        out_shape=jax.ShapeDtypeStruct((t, MEM_WIDTH), BF16),
        compiler_params=_cp("parallel", "parallel", "parallel"),
    )(qm, kvm, kvm)


def _xattn_bwd(qm, kvm, do, seq, n_mem, tq=2048):
    t = qm.shape[0]
    tq = min(tq, seq)
    nb, nq = t // seq, seq // tq

    def body(q_ref, k_ref, v_ref, do_ref, dq_ref, dkv_ref, dk_acc, dv_acc):
        i = pl.program_id(2)

        @pl.when(i == 0)
        def _():
            dk_acc[...] = jnp.zeros_like(dk_acc)
            dv_acc[...] = jnp.zeros_like(dv_acc)

        _, vjp = jax.vjp(_xattn_tile, q_ref[...], k_ref[...], v_ref[...])
        dq, dk, dv = vjp(do_ref[...].astype(F32))
        dq_ref[...] = dq.astype(dq_ref.dtype)
        dk_acc[...] += dk
        dv_acc[...] += dv

        @pl.when(i == nq - 1)
        def _():
            dkv_ref[0] = dk_acc[...].astype(dkv_ref.dtype)
            dkv_ref[1] = dv_acc[...].astype(dkv_ref.dtype)

    dq, dkv = pl.pallas_call(
        body, name="xattn_bwd", grid=(nb, MEM_HEADS, nq),
        in_specs=[pl.BlockSpec((tq, LANE), lambda b, h, i: (b * nq + i, h)),
                  pl.BlockSpec((n_mem, LANE), lambda b, h, i: (b, h)),
                  pl.BlockSpec((n_mem, LANE), lambda b, h, i: (b, MEM_HEADS + h)),
                  pl.BlockSpec((tq, LANE), lambda b, h, i: (b * nq + i, h))],
        out_specs=[pl.BlockSpec((tq, LANE), lambda b, h, i: (b * nq + i, h)),
                   pl.BlockSpec((2, n_mem, LANE), lambda b, h, i: (0, b, h))],
        out_shape=[jax.ShapeDtypeStruct((t, MEM_WIDTH), BF16), jax.ShapeDtypeStruct((2, nb * n_mem, MEM_WIDTH), BF16)],
        scratch_shapes=[pltpu.VMEM((n_mem, LANE), F32)] * 2,
        compiler_params=_cp("parallel", "parallel", "arbitrary"),
    )(qm, kvm, kvm, do)
    return dq, jnp.concatenate([dkv[0], dkv[1]], axis=1)


def _loss_head(x, u, g_post, target, tm=512):
    t, d = x.shape
    tm = min(tm, t)

    def tile_loss(xv, uv, gv, tv):
        diff = _fn_res(xv, uv, gv)[0] - tv
        return 0.5 * jnp.sum(jnp.mean(diff * diff, axis=-1, keepdims=True), axis=0, keepdims=True)

    def body(x_ref, u_ref, g_ref, t_ref, l_ref, dx_ref, du_ref, dg_ref):
        @pl.when(pl.program_id(0) == 0)
        def _():
            l_ref[...] = jnp.zeros_like(l_ref)
            dg_ref[...] = jnp.zeros_like(dg_ref)

        tv = t_ref[...]
        part, vjp = jax.vjp(lambda xv, uv, gv: tile_loss(xv, uv, gv, tv), x_ref[...], u_ref[...], g_ref[...])
        dx, du, dg = vjp(jnp.ones((1, 1), F32))
        l_ref[...] += part
        dx_ref[...] = dx
        du_ref[...] = du.astype(du_ref.dtype)
        dg_ref[...] += dg

    rows = pl.BlockSpec((tm, d), lambda i: (i, 0))
    vec = pl.BlockSpec((1, d), lambda i: (0, 0))
    return pl.pallas_call(
        body, name="loss_head", grid=(t // tm,),
        in_specs=[rows, rows, vec, rows],
        out_specs=[pl.BlockSpec((8, LANE), lambda i: (0, 0)), rows, rows, vec],
        out_shape=[jax.ShapeDtypeStruct((8, LANE), F32), jax.ShapeDtypeStruct((t, d), F32),
                   jax.ShapeDtypeStruct((t, d), BF16), jax.ShapeDtypeStruct((1, d), F32)],
        compiler_params=_cp("arbitrary"),
    )(x, u, g_post, target)


def _mesh_pos():
    return lax.axis_index("x"), lax.axis_index("y"), lax.axis_index("c")


def _peer(pos, d):
    x, y, c = pos
    return ((1 - x) if d & 4 else x, (1 - y) if d & 2 else y, (1 - c) if d & 1 else c)


def _flat(pos):
    return 4 * pos[0] + 2 * pos[1] + pos[2]


def _exchange(arrays, scatter, *, name):
    n = len(arrays)
    shapes = [a.shape[1:] if scatter else a.shape for a in arrays]

    def body(*refs):
        ins, outs = refs[:n], refs[n:2 * n]
        send, recv, loc = refs[2 * n:]
        pos = _mesh_pos()
        me = _flat(pos)
        pending = []
        for i in range(n):
            own = pltpu.make_async_copy(ins[i].at[me] if scatter else ins[i], outs[i].at[me], loc.at[i])
            own.start()
            pending.append(own)
            for d in range(1, N_DEV):
                peer = _peer(pos, d)
                src = ins[i].at[_flat(peer)] if scatter else ins[i]
                out_cp = pltpu.make_async_remote_copy(
                    src_ref=src, dst_ref=outs[i].at[me], send_sem=send.at[i, d - 1], recv_sem=recv.at[i, d - 1],
                    device_id=peer, device_id_type=pl.DeviceIdType.MESH)
                out_cp.start()
                pending.append(out_cp)
        for i in range(n):
            own = pending[i * N_DEV]
            for d in range(1, N_DEV):
                peer = _peer(pos, d)
                src = ins[i].at[_flat(peer)] if scatter else ins[i]
                pending[i * N_DEV + d].wait_send()
                pltpu.make_async_remote_copy(
                    src_ref=src, dst_ref=outs[i].at[_flat(peer)], send_sem=send.at[i, d - 1], recv_sem=recv.at[i, d - 1],
                    device_id=peer, device_id_type=pl.DeviceIdType.MESH).wait_recv()
            own.wait()

    hbm = pl.BlockSpec(memory_space=pltpu.HBM)
    return pl.pallas_call(
        body, name=name,
        in_specs=[hbm] * n, out_specs=[hbm] * n,
        out_shape=[jax.ShapeDtypeStruct((N_DEV,) + tuple(s), a.dtype) for s, a in zip(shapes, arrays)],
        scratch_shapes=[pltpu.SemaphoreType.DMA((n, N_DEV - 1)), pltpu.SemaphoreType.DMA((n, N_DEV - 1)),
                        pltpu.SemaphoreType.DMA((n,))],
    )(*arrays)


_HBM = pl.BlockSpec(memory_space=pltpu.HBM)
_SEM = pl.BlockSpec(memory_space=pltpu.SEMAPHORE)
_DATAFLOW = pltpu.SideEffectType.DATAFLOW_SIDE_EFFECTING


_ALL_PEERS = tuple(range(1, N_DEV))
_SIBLING_AND_SAME_CORE = (1, 2, 4, 6)


def _remote_copies(ins, lands, send, recv, scatter, dists):
    pos = _mesh_pos()
    me = _flat(pos)
    out = []
    for i in range(len(ins)):
        for j, d in enumerate(dists):
            peer = _peer(pos, d)
            src = ins[i].at[_flat(peer)] if scatter else ins[i]
            pair = i * len(dists) + j
            sems = dict(send_sem=send.at[pair], recv_sem=recv.at[pair], device_id=peer,
                        device_id_type=pl.DeviceIdType.MESH)
            out.append((pltpu.make_async_remote_copy(src_ref=src, dst_ref=lands[i].at[me], **sems),
                        pltpu.make_async_remote_copy(src_ref=src, dst_ref=lands[i].at[_flat(peer)], **sems)))
    return out


def _exchange_start(arrays, scatter, after, *, name, dists=_ALL_PEERS):
    n = len(arrays)
    shapes = [a.shape[1:] if scatter else a.shape for a in arrays]
    lands = [pltpu.with_memory_space_constraint(lax.empty((N_DEV,) + tuple(s), a.dtype), pltpu.HBM)
             for s, a in zip(shapes, arrays)]
    srcs = [pltpu.with_memory_space_constraint(a, pltpu.HBM) for a in arrays]

    def body(*refs):
        ins, land_refs = refs[:n], refs[n:2 * n]
        send, recv, token = refs[2 * n + 1], refs[2 * n + 2], refs[-1]
        for going, _ in _remote_copies(ins, land_refs, send, recv, scatter, dists):
            going.start()
        token[...] = jnp.zeros_like(token)

    sems = pltpu.SemaphoreType.DMA((n * len(dists),))
    res = pl.pallas_call(
        body, name=name,
        out_shape=(sems, sems, *[pltpu.HBM(a.shape, a.dtype) for a in srcs + lands], jax.ShapeDtypeStruct((8, LANE), F32)),
        in_specs=[_HBM] * (2 * n) + [pl.BlockSpec(memory_space=pl.ANY)],
        out_specs=(_SEM, _SEM, *[_HBM] * (2 * n), pl.BlockSpec(memory_space=pltpu.VMEM)),
        input_output_aliases={i: 2 + i for i in range(2 * n)},
        compiler_params=pltpu.CompilerParams(has_side_effects=_DATAFLOW),
    )(*srcs, *lands, after)
    return (n, scatter, dists, res[0], res[1], list(res[2:2 + 2 * n])), res[-1]


def _exchange_wait(handle, after, own, *, name):
    n, scatter, dists, send, recv, thru = handle

    def body(*refs):
        ins, land_refs = refs[:n], refs[n:2 * n]
        for going, coming in _remote_copies(ins, land_refs, refs[2 * n], refs[2 * n + 1], scatter, dists):
            going.wait_send()
            coming.wait_recv()

    res = pl.pallas_call(
        body, name=name,
        out_shape=tuple(pltpu.HBM(a.shape, a.dtype) for a in thru),
        in_specs=[_HBM] * (2 * n) + [_SEM, _SEM] + [pl.BlockSpec(memory_space=pl.ANY)] * len(after),
        out_specs=tuple([_HBM] * (2 * n)),
        input_output_aliases={i: i for i in range(2 * n)},
        compiler_params=pltpu.CompilerParams(has_side_effects=_DATAFLOW),
    )(*thru, send, recv, *after)
    me = _flat(_mesh_pos())
    return [lax.dynamic_update_slice_in_dim(land, o[None].astype(land.dtype), me, 0) for land, o in zip(res[n:], own)]


_OTHER_CHIPS = (2, 4, 6)


def _relay_to_sibling(gathered, *, name):
    n, k = len(gathered), len(_OTHER_CHIPS)

    def body(*refs):
        ins, outs = refs[:n], refs[n:2 * n]
        send, recv = refs[2 * n:]
        pos = _mesh_pos()
        copies = []
        for i in range(n):
            for j, d in enumerate(_OTHER_CHIPS):
                cp = pltpu.make_async_remote_copy(
                    src_ref=ins[i].at[_flat(_peer(pos, d))], dst_ref=outs[i].at[j],
                    send_sem=send.at[i * k + j], recv_sem=recv.at[i * k + j],
                    device_id=_peer(pos, 1), device_id_type=pl.DeviceIdType.MESH)
                cp.start()
                copies.append(cp)
        for cp in copies:
            cp.wait()

    return pl.pallas_call(
        body, name=name, in_specs=[_HBM] * n, out_specs=[_HBM] * n,
        out_shape=[jax.ShapeDtypeStruct((k,) + g.shape[1:], g.dtype) for g in gathered],
        scratch_shapes=[pltpu.SemaphoreType.DMA((n * k,)), pltpu.SemaphoreType.DMA((n * k,))],
    )(*gathered)


def _adamw(parts, w, m, v, *, name, tr=128, after=None):
    r, c = w.shape
    align = 8 * 4 // parts.dtype.itemsize
    row_tiles = [d for d in range(align, min(tr, r) + 1, align) if r % d == 0]
    tr, tc = (max(row_tiles), c) if row_tiles else (r, LANE)
    assert c % tc == 0
    n_after = 0 if after is None else 1

    def body(p_ref, w_ref, m_ref, v_ref, *rest):
        g_ref, d_ref, nm_ref, nv_ref = rest[n_after:]
        g = p_ref[0].astype(F32)
        for j in range(1, N_DEV):
            g = g + p_ref[j].astype(F32)
        m2 = ADAM_B1 * m_ref[...] + (1.0 - ADAM_B1) * g
        v2 = ADAM_B2 * v_ref[...] + (1.0 - ADAM_B2) * (g * g)
        m_hat = m2 / (1.0 - ADAM_B1 ** ADAM_STEP)
        v_hat = v2 / (1.0 - ADAM_B2 ** ADAM_STEP)
        g_ref[...] = g
        d_ref[...] = -ADAM_LR * (m_hat / (jnp.sqrt(v_hat) + ADAM_EPS) + ADAM_WD * w_ref[...])
        nm_ref[...] = m2
        nv_ref[...] = v2

    spec = pl.BlockSpec((tr, tc), lambda i, j: (i, j))
    return pl.pallas_call(
        body, name=name, grid=(r // tr, c // tc),
        in_specs=[pl.BlockSpec((N_DEV, tr, tc), lambda i, j: (0, i, j)), spec, spec, spec]
        + [pl.BlockSpec(memory_space=pl.ANY)] * n_after,
        out_specs=[spec] * 4, out_shape=[jax.ShapeDtypeStruct((r, c), F32)] * 4,
        compiler_params=_cp("parallel", "parallel"),
    )(parts, w, m, v, *([] if after is None else [after]))


def _cols_to_full(g):
    return jnp.transpose(g, (1, 0, 2)).reshape(g.shape[1], N_DEV * g.shape[2])


def _full_to_cols(w):
    r, c = w.shape
    return jnp.transpose(w.reshape(r, N_DEV, c // N_DEV), (1, 0, 2))


def _cut(a, lo, hi, axis):
    return lax.slice_in_dim(a, lo, hi, axis=axis)


def _pad_to(a, size, axis):
    pads = [(0, 0)] * a.ndim
    pads[axis] = (0, size - a.shape[axis])
    return jnp.pad(a, pads)


def _pad_lora(w, axis=1):
    return jnp.concatenate([
        _pad_to(_cut(w, 0, LORA_W, axis), 128, axis), _pad_to(_cut(w, LORA_W, LORA_W + LORA_A, axis), 128, axis),
        _pad_to(_cut(w, LORA_W + LORA_A, w.shape[axis], axis), 256, axis)], axis=axis)


def _unpad_lora(wp, axis=1):
    return jnp.concatenate([_cut(wp, 0, LORA_W, axis), _cut(wp, 128, 128 + LORA_A, axis),
                            _cut(wp, 256, 256 + LORA_G, axis)], axis=axis)


def _permute_in(w, axis):
    rk = 3 * D
    lo = rk + LORA_W + LORA_A + LORA_G
    return jnp.concatenate([_cut(w, 0, rk, axis), _cut(w, lo, w.shape[axis], axis), _pad_lora(_cut(w, rk, lo, axis), axis)],
                           axis=axis)


def _unpermute_in(wp, axis):
    return jnp.concatenate([_cut(wp, 0, 3 * D, axis), _unpad_lora(_cut(wp, C_LORA, P_WIDTH, axis), axis),
                            _cut(wp, 3 * D, C_LORA, axis)], axis=axis)


def _rel_index():
    dist = jnp.arange(CHUNK)[:, None] - jnp.arange(BAND)[None, :] + LEFT
    return (jnp.minimum(dist, REL_CLIP) + (CHUNK - 1)).reshape(-1)


def _local_step(x, mem, target, wt, seq, n_mem, comm):
    t = x.shape[0]
    row = lambda a: a.reshape(1, -1).astype(F32)
    g_pre_mix, g_post_mix = row(wt["g_pre_mix"]), row(wt["g_post_mix"])
    g_pre_cross, g_post_cross, g_mem = row(wt["g_pre_cross"]), row(wt["g_post_cross"]), row(wt["g_mem"])
    g_pre_ffn, g_post_ffn = row(wt["g_pre_ffn"]), row(wt["g_post_ffn"])
    mix = row(wt["shift_mix"])
    mix_rkv, mix_lora = mix[:, :3 * D], _pad_lora(mix[:, 3 * D:])
    decay_base, iclr_base = row(wt["decay_base"]), row(wt["iclr_base"])
    kns, kis = row(wt["key_norm_scale"]), row(wt["key_iclr_scale"])
    lnx_w, lnx_b, bonus = row(wt["lnx_w"]), row(wt["lnx_b"]), row(wt["bonus_scale"])
    e_dh = (jnp.arange(D)[:, None] // HEAD == jnp.arange(N_HEADS)[None, :]).astype(F32)
    e_hd = e_dh.T
    onehot = (jnp.arange(REL_TABLE)[:, None] == _rel_index()[None, :]).astype(BF16)

    begun = comm.begun
    (h1,) = _rowwise(_fn_pre, [_win(x)], [g_pre_mix], [(D, BF16)], name="pre_mix", tm=512, after=begun)
    (mn,) = _rowwise(_fn_pre, [_win(mem)], [g_mem], [(D, BF16)], name="pre_mem", tm=512, after=begun)
    bias = _mm(wt["rel_bias"].astype(F32), onehot, name="mm_bias", split_a=3, after=begun).reshape(N_HEADS, CHUNK, BAND)
    wt = {**wt, **comm.first_weights([h1, mn, bias])}
    w_in = wt["w_in_p"]
    d_up = jnp.pad(wt["decay_up"].astype(F32), ((0, 128 - LORA_W), (0, 0)))
    i_up = jnp.pad(wt["iclr_up"].astype(F32), ((0, 128 - LORA_A), (0, 0)))
    g_up = jnp.pad(wt["gate_up"].astype(F32), ((0, 256 - LORA_G), (0, 0)))
    proj = _mm(h1, w_in, tb=True, name="mm_in", after=comm.first_token)
    z_rkv = _shift_fwd(proj, 0, 3 * D, mix_rkv, seq, name="shift_rkv")
    z_lora = _shift_fwd(proj, C_LORA, 512, mix_lora, seq, name="shift_lora")
    prep_rows = [_win(z_rkv, D, D), _win(z_lora, 0, 128), _win(z_lora, 128, 128), _win(z_lora, 256, 256)]
    prep_params = [decay_base, d_up, iclr_base, i_up, g_up, kns, kis, e_hd, e_dh]
    lw, k2, kk, a, g = _rowwise(_fn_prep, prep_rows, prep_params, [(D, F32)] * 5, name="rwkv_prep", tm=256)
    y, states, invs = _wkv_fwd(z_rkv, lw, k2, kk, a, seq)
    post_rows = [_win(y), _win(z_rkv, 0, D), _win(k2), _win(z_rkv, 2 * D, D), _win(g)]
    post_params = [lnx_w, lnx_b, bonus, e_hd, e_dh]
    (y_a,) = _rowwise(_fn_post, post_rows, post_params, [(D, BF16)], name="rwkv_post", tm=256)
    bias = _pair_bias(bias)
    y_b = _attn_fwd(proj, bias, seq)
    wt = {**wt, **comm.late_weights(y_b)}
    ya_p = _mm(y_a, wt["w_branch_a"], name="mm_a")
    yb_p = _mm(y_b, wt["w_branch_b"], name="mm_b")
    mix_rows = [_win(proj, C_GA, D), _win(proj, C_GA + D, D), _win(ya_p), _win(yb_p)]
    (mixed,) = _rowwise(_fn_mix, mix_rows, [], [(D, BF16)], name="gate_mix", tm=512)
    mo = _mm(mixed, wt["w_out"], name="mm_out")
    x1, h2 = _rowwise(_fn_res_pre, [_win(x), _win(mo)], [g_post_mix, g_pre_cross], [(D, F32), (D, BF16)],
                      name="res_mix", tm=512)
    qm = _mm(h2, wt["w_q_mem"], name="mm_q", out_dtype=BF16)
    kvm = _mm(mn, wt["w_kv_mem"], name="mm_kv", out_dtype=BF16)
    om = _xattn_fwd(qm, kvm, seq, n_mem)
    co = _mm(om, wt["w_o_mem"], name="mm_o")
    x2, h3 = _rowwise(_fn_res_pre, [_win(x1), _win(co)], [g_post_cross, g_pre_ffn], [(D, F32), (D, BF16)],
                      name="res_cross", tm=512)
    gu = _mm(h3, wt["w_ffn_in"], tb=True, name="mm_ffn_in", out_dtype=BF16)
    (act,) = _rowwise(_fn_swiglu, [_win(gu, 0, FFN), _win(gu, FFN, FFN)], [], [(FFN, BF16)], name="swiglu", tm=512)
    ff = _mm(act, wt["w_ffn_out"], name="mm_ffn_out")

    gw = {}
    loss, dx2, dff, gw["g_post_ffn"] = _loss_head(x2, ff, g_post_ffn, target)
    dact = _mm(dff, wt["w_ffn_out"], tb=True, name="mm_ffn_out_dx", out_dtype=BF16)
    gw["w_ffn_out"] = _mm(act, dff, ta=True, name="mm_ffn_out_dw", out_dtype=BF16)
    (dgu,), _ = _rowwise_bwd(_fn_swiglu, [_win(gu, 0, FFN), _win(gu, FFN, FFN)], [], 0, [[dact]],
                             name="swiglu_bwd", tm=512, row_grad=[BF16, BF16], packed=True)
    dh3 = _mm(dgu, wt["w_ffn_in"], name="mm_ffn_in_dx", out_dtype=BF16)
    gw["w_ffn_in"] = _mm(dgu, h3, ta=True, name="mm_ffn_in_dw", out_dtype=BF16)
    (dx1, dco), (gw["g_post_cross"], gw["g_pre_ffn"]) = _rowwise_bwd(
        _fn_res_pre, [_win(x1), _win(co)], [g_post_cross, g_pre_ffn], 0, [[dx2], [dh3]],
        name="res_cross_bwd", tm=512, row_grad=[F32, BF16])
    dom = _mm(dco, wt["w_o_mem"], tb=True, name="mm_o_dx", out_dtype=BF16)
    gw["w_o_mem"] = _mm(om, dco, ta=True, name="mm_o_dw", out_dtype=BF16)
    dqm, dkvm = _xattn_bwd(qm, kvm, dom, seq, n_mem)
    dh2 = _mm(dqm, wt["w_q_mem"], tb=True, name="mm_q_dx", out_dtype=BF16)
    gw["w_q_mem"] = _mm(h2, dqm, ta=True, name="mm_q_dw", out_dtype=BF16)
    dmn = _mm(dkvm, wt["w_kv_mem"], tb=True, name="mm_kv_dx", out_dtype=BF16)
    gw["w_kv_mem"] = _mm(mn, dkvm, ta=True, name="mm_kv_dw", out_dtype=BF16)
    _, (gw["g_mem"],) = _rowwise_bwd(_fn_pre, [_win(mem)], [g_mem], 0, [[dmn]], name="pre_mem_bwd", tm=256,
                                     row_grad=[None])
    (dx0, dmo), (gw["g_post_mix"], gw["g_pre_cross"]) = _rowwise_bwd(
        _fn_res_pre, [_win(x), _win(mo)], [g_post_mix, g_pre_cross], 0, [[dx1], [dh2]],
        name="res_mix_bwd", tm=512, row_grad=[F32, BF16])
    dmixed = _mm(dmo, wt["w_out"], tb=True, name="mm_out_dx", out_dtype=BF16)
    gw["w_out"] = _mm(mixed, dmo, ta=True, name="mm_out_dw", out_dtype=BF16)
    (dzga, dzgb, dya_p, dyb_p), _ = _rowwise_bwd(_fn_mix, mix_rows, [], 0, [[dmixed]], name="gate_mix_bwd", tm=512,
                                                 row_grad=[BF16] * 4)
    gw["w_branch_a"] = _mm(y_a, dya_p, ta=True, name="mm_a_dw", out_dtype=BF16)
    gw["w_branch_b"] = _mm(y_b, dyb_p, ta=True, name="mm_b_dw", out_dtype=BF16)
    token = comm.send_early(gw)
    dy_a = _mm(dya_p, wt["w_branch_a"], tb=True, name="mm_a_dx", out_dtype=BF16, after=token)
    dy_b = _mm(dyb_p, wt["w_branch_b"], tb=True, name="mm_b_dx", out_dtype=BF16, after=token)
    dq, dk, dv, dbias = _attn_bwd(proj, bias, dy_b, seq)
    gw["rel_bias"] = _mm(_unpair_bias(dbias).reshape(N_HEADS, CHUNK * BAND), onehot, tb=True, name="mm_bias_dw", split_a=2)
    (dy, dr_p, dk2_p, dv_p, dg), (gw["lnx_w"], gw["lnx_b"], gw["bonus_scale"]) = _rowwise_bwd(
        _fn_post, post_rows, post_params, 2, [[dy_a]], name="rwkv_post_bwd", tm=512, row_grad=[BF16] * 5)
    dr_s, dlw, dk2_s, dv_s, dkk, da = _wkv_bwd(z_rkv, lw, k2, kk, a, states, invs, dy, seq)
    (dzk, dzw, dza, dzg), pg = _rowwise_bwd(
        _fn_prep, prep_rows, prep_params, 2, [[dlw], [dk2_p, dk2_s], [dkk], [da], [dg]],
        name="rwkv_prep_bwd", tm=512, row_grad=[BF16] * 4)
    gw["decay_base"], gd_up, gw["iclr_base"], gi_up, gg_up, gw["key_norm_scale"], gw["key_iclr_scale"] = pg
    gw["decay_up"], gw["iclr_up"], gw["gate_up"] = gd_up[:LORA_W], gi_up[:LORA_A], gg_up[:LORA_G]
    dp_r, gmix_r = _shift_bwd(proj, 0, D, mix_rkv[:, :D], [dr_p, dr_s], seq, name="shift_r_bwd")
    dp_k, gmix_k = _shift_bwd(proj, D, D, mix_rkv[:, D:2 * D], [dzk], seq, name="shift_k_bwd")
    dp_v, gmix_v = _shift_bwd(proj, 2 * D, D, mix_rkv[:, 2 * D:], [dv_p, dv_s], seq, name="shift_v_bwd")
    dp_lora, gmix_lora = _shift_bwd(proj, C_LORA, 512, mix_lora, [jnp.concatenate([dzw, dza, dzg], axis=1)], seq,
                                    name="shift_lora_bwd")
    gw["shift_mix"] = jnp.concatenate([gmix_r, gmix_k, gmix_v, _unpad_lora(gmix_lora)], axis=1)
    dproj = [dp_r, dp_k, dp_v, dq, dk, dv, dzga, dzgb, dp_lora]
    gw["w_in_p"] = _mm_cat_tn(dproj, h1, name="mm_in_dw", after=gw["rel_bias"])
    token = comm.send_late(gw)
    dh1 = _mm_cat_nn(dproj, w_in, name="mm_in_dx", after=token)
    (grad_x,), (gw["g_pre_mix"],) = _rowwise_bwd(_fn_pre, [_win(x)], [g_pre_mix], 0, [[dh1]], name="pre_mix_bwd",
                                                 tm=512, row_grad=[F32], add_to={0: dx0})
    return loss, grad_x, gw


_COL_SHARDED = ("w_in", "decay_up", "iclr_up", "gate_up", "w_o_mem", "w_ffn_in")
_ROW_SHARDED = ("w_branch_a", "w_branch_b", "w_out", "w_q_mem", "w_kv_mem", "w_ffn_out")
_TRANSPOSED = ("w_in", "w_ffn_in")
_FIRST = ("w_in", "decay_up", "iclr_up", "gate_up")
_REST = ("w_o_mem", "w_ffn_in", "w_branch_a", "w_branch_b", "w_out", "w_q_mem", "w_kv_mem", "w_ffn_out")
_REPLICATED = ("g_pre_mix", "g_post_mix", "shift_mix", "decay_base", "iclr_base", "key_norm_scale", "key_iclr_scale",
               "bonus_scale", "lnx_w", "lnx_b", "rel_bias", "g_pre_cross", "g_post_cross", "g_mem", "g_pre_ffn",
               "g_post_ffn")
_WEIGHTS = ("g_pre_mix", "g_post_mix", "w_in", "shift_mix", "decay_base", "decay_up", "iclr_base", "iclr_up", "gate_up",
            "key_norm_scale", "key_iclr_scale", "bonus_scale", "lnx_w", "lnx_b", "rel_bias", "w_branch_a", "w_branch_b",
            "w_out", "g_pre_cross", "g_post_cross", "g_mem", "w_q_mem", "w_kv_mem", "w_o_mem", "g_pre_ffn", "g_post_ffn",
            "w_ffn_in", "w_ffn_out")
_PACK_ROWS = 8 * ((sum({"shift_mix": 3360, "bonus_scale": 1024, "rel_bias": 3072}.get(n, D) for n in _REPLICATED)
                   + 1 + 8 * LANE - 1) // (8 * LANE))


def _pack(vals):
    flat = jnp.concatenate([v.reshape(-1).astype(F32) for v in vals])
    return jnp.pad(flat, (0, _PACK_ROWS * LANE - flat.shape[0])).reshape(_PACK_ROWS, LANE)


def _unpack(packed, shapes):
    flat, out, pos = packed.reshape(-1), [], 0
    for s in shapes:
        n = math.prod(s)
        out.append(flat[pos:pos + n].reshape(s))
        pos += n
    return out


def _step(args, seq, n_mem):
    names = ("x", "mem") + _WEIGHTS + ("loss_target",) + tuple("m_" + n for n in _WEIGHTS) + tuple("v_" + n for n in _WEIGHTS)
    given = dict(zip(names, args))
    nb = given["x"].shape[0]
    x = given["x"].reshape(nb * seq, D)
    mem = given["mem"].reshape(nb * n_mem, D)
    target = given["loss_target"].reshape(nb * seq, D)
    def local(name, prefix=""):
        a = given[prefix + name][0]
        return a.T if name in _TRANSPOSED else a

    shard = {n: local(n) for n in _COL_SHARDED + _ROW_SHARDED}
    stacked = _ROW_SHARDED + _TRANSPOSED
    out = {}

    def wire(name):
        return shard[name].astype(BF16)

    def full(name, g):
        return g.reshape(-1, g.shape[-1]) if name in stacked else _cols_to_full(g)

    def blocks_of(name, g):
        return (g.reshape((N_DEV,) + shard[name].shape) if name in stacked else _full_to_cols(g)).astype(BF16)

    def update(names, landed, after=None):
        done = []
        for n, parts in zip(names, landed):
            res = _adamw(parts, shard[n], local(n, "m_"), local(n, "v_"), name="adamw_" + n, after=after)
            for kind, r in zip(("grad_", "delta_", "new_m_", "new_v_"), res):
                out[kind + n] = (r.T if n in _TRANSPOSED else r)[None]
            done.append(res[0])
        return done


    class Exchanges:
        def __init__(self):
            srcs = [wire(n) for n in _FIRST]
            self.first, self.begun = _exchange_start(srcs, False, srcs[0], name="gather_first_start",
                                                     dists=_SIBLING_AND_SAME_CORE)

        def first_weights(self, after):
            got = _exchange_wait(self.first, after, [wire(n) for n in _FIRST], name="gather_first_wait")
            relayed = _relay_to_sibling(got, name="gather_first_relay")
            pos = _mesh_pos()
            for j, d in enumerate(_OTHER_CHIPS):
                slot = _flat(_peer(pos, d | 1))
                got = [lax.dynamic_update_slice_in_dim(g, r[j][None], slot, 0) for g, r in zip(got, relayed)]
            self.rest, self.first_token = _exchange_start(
                [wire(n) for n in _REST], False, got[0], name="gather_rest_start")
            first = {n: full(n, g) for n, g in zip(_FIRST, got)}
            first["w_in_p"] = _permute_in(first.pop("w_in"), 0)
            return first

        def late_weights(self, after):
            got = _exchange_wait(self.rest, [after], [wire(n) for n in _REST], name="gather_rest_wait")
            return {n: full(n, g) for n, g in zip(_REST, got)}

        def send_early(self, gw):
            self.early_blocks = [blocks_of(n, gw[n]) for n in _REST]
            self.early, token = _exchange_start(self.early_blocks, True, self.early_blocks[-1], name="scatter_rest_start")
            return token

        def send_late(self, gw):
            me = _flat(_mesh_pos())
            own = [lax.dynamic_index_in_dim(b, me, 0, keepdims=False) for b in self.early_blocks]
            landed = _exchange_wait(self.early, [gw["w_in_p"]], own, name="scatter_rest_wait")
            grads = {**gw, "w_in": _unpermute_in(gw["w_in_p"], 0)}
            self.late_blocks = [blocks_of(n, grads[n]) for n in _FIRST]
            self.late, token = _exchange_start(self.late_blocks, True, landed[0], name="scatter_first_start")
            self.updated = update(_REST, landed, after=token)
            return token

        def finish(self, after):
            me = _flat(_mesh_pos())
            own = [lax.dynamic_index_in_dim(b, me, 0, keepdims=False) for b in self.late_blocks]
            update(_FIRST, _exchange_wait(self.late, [*after, *self.updated], own, name="scatter_first_wait"))

    comm = Exchanges()
    wt = {n: given[n][0] for n in _REPLICATED}
    loss_tile, grad_x, gw = _local_step(x, mem, target, wt, seq, n_mem, comm)
    rep_shapes = [given[n].shape for n in _REPLICATED]
    packed, _ = lax.optimization_barrier((_pack([gw[n] for n in _REPLICATED] + [loss_tile[0, 0]]), tuple(comm.updated)))
    small = _exchange([packed], False, name="gather_small")[0]
    zero = jnp.zeros((), F32)
    res = _adamw(small, *[_pack([given[p + n] for n in _REPLICATED] + [zero]) for p in ("", "m_", "v_")],
                 name="adamw_small", tr=_PACK_ROWS)
    for kind, r in zip(("grad_", "delta_", "new_m_", "new_v_"), res):
        for n, val in zip(_REPLICATED, _unpack(r, rep_shapes)):
            out[kind + n] = val
    loss = res[0].reshape(-1)[sum(math.prod(s) for s in rep_shapes)]
    comm.finish([grad_x, res[0]])
    grad_x = grad_x.reshape(nb, seq, D)
    return (loss, grad_x, *[out[k + n] for k in ("grad_", "delta_", "new_m_", "new_v_") for n in _WEIGHTS])


def kernel(x, mem, g_pre_mix, g_post_mix, w_in, shift_mix, decay_base, decay_up, iclr_base, iclr_up, gate_up, key_norm_scale, key_iclr_scale, bonus_scale, lnx_w, lnx_b, rel_bias, w_branch_a, w_branch_b, w_out, g_pre_cross, g_post_cross, g_mem, w_q_mem, w_kv_mem, w_o_mem, g_pre_ffn, g_post_ffn, w_ffn_in, w_ffn_out, loss_target, m_g_pre_mix, m_g_post_mix, m_w_in, m_shift_mix, m_decay_base, m_decay_up, m_iclr_base, m_iclr_up, m_gate_up, m_key_norm_scale, m_key_iclr_scale, m_bonus_scale, m_lnx_w, m_lnx_b, m_rel_bias, m_w_branch_a, m_w_branch_b, m_w_out, m_g_pre_cross, m_g_post_cross, m_g_mem, m_w_q_mem, m_w_kv_mem, m_w_o_mem, m_g_pre_ffn, m_g_post_ffn, m_w_ffn_in, m_w_ffn_out, v_g_pre_mix, v_g_post_mix, v_w_in, v_shift_mix, v_decay_base, v_decay_up, v_iclr_base, v_iclr_up, v_gate_up, v_key_norm_scale, v_key_iclr_scale, v_bonus_scale, v_lnx_w, v_lnx_b, v_rel_bias, v_w_branch_a, v_w_branch_b, v_w_out, v_g_pre_cross, v_g_post_cross, v_g_mem, v_w_q_mem, v_w_kv_mem, v_w_o_mem, v_g_pre_ffn, v_g_post_ffn, v_w_ffn_in, v_w_ffn_out):
    args = (x, mem, g_pre_mix, g_post_mix, w_in, shift_mix, decay_base, decay_up, iclr_base, iclr_up, gate_up, key_norm_scale, key_iclr_scale, bonus_scale, lnx_w, lnx_b, rel_bias, w_branch_a, w_branch_b, w_out, g_pre_cross, g_post_cross, g_mem, w_q_mem, w_kv_mem, w_o_mem, g_pre_ffn, g_post_ffn, w_ffn_in, w_ffn_out, loss_target, m_g_pre_mix, m_g_post_mix, m_w_in, m_shift_mix, m_decay_base, m_decay_up, m_iclr_base, m_iclr_up, m_gate_up, m_key_norm_scale, m_key_iclr_scale, m_bonus_scale, m_lnx_w, m_lnx_b, m_rel_bias, m_w_branch_a, m_w_branch_b, m_w_out, m_g_pre_cross, m_g_post_cross, m_g_mem, m_w_q_mem, m_w_kv_mem, m_w_o_mem, m_g_pre_ffn, m_g_post_ffn, m_w_ffn_in, m_w_ffn_out, v_g_pre_mix, v_g_post_mix, v_w_in, v_shift_mix, v_decay_base, v_decay_up, v_iclr_base, v_iclr_up, v_gate_up, v_key_norm_scale, v_key_iclr_scale, v_bonus_scale, v_lnx_w, v_lnx_b, v_rel_bias, v_w_branch_a, v_w_branch_b, v_w_out, v_g_pre_cross, v_g_post_cross, v_g_mem, v_w_q_mem, v_w_kv_mem, v_w_o_mem, v_g_pre_ffn, v_g_post_ffn, v_w_ffn_in, v_w_ffn_out)
    return _step(args, x.shape[1], mem.shape[1])
```
